```python
import math
import numpy as np
import jax, jax.numpy as jnp
from jax import lax

D_MODEL = 2048
BATCH = 2
SEQ = 8192
DEPTH = 1

F32 = jnp.float32
D_MIX = D_MODEL
SSM_WIDTH = D_MIX // 2
SSM_CH_PER_GROUP = 16
SSM_GROUPS = SSM_WIDTH // SSM_CH_PER_GROUP
SSM_STATE = 64
DT_MIN = 1e-3
DT_MAX = 1e-1
NSA_HEADS = 16
NSA_KV_HEADS = 2
HEAD_DIM = 64
Q_PER_KV = NSA_HEADS // NSA_KV_HEADS
NSA_WIDTH = NSA_HEADS * HEAD_DIM
KV_WIDTH = NSA_KV_HEADS * HEAD_DIM
N_BRANCH = 3
IN_COLS = SSM_WIDTH + NSA_WIDTH + 6 * KV_WIDTH + NSA_HEADS * N_BRANCH
CMP_BLOCK = 32
CMP_STRIDE = 16
CMP_HIDDEN = 128
SEL_BLOCK = 64
SEL_TOPK = 16
WINDOW = 512
Q_BLOCK = 128
ROPE_THETA = 10000.0
N_EXPERTS = 64
TOP_K = 8
N_EXPERT_GROUPS = 8
TOPK_GROUPS = 4
EXPERT_FF = 512
SHARED_FF = 512
ROUTED_SCALE = 2.5
MOE_BLOCK = 128
DEEPNORM_ALPHA = (2.0 * DEPTH) ** 0.25
DEEPNORM_BETA = (8.0 * DEPTH) ** -0.25
LN_EPS = 1e-5
NEG = -1e30
FORCE = 1e4

kernel_name = 'hymba_s5_nsa_moe_deepnorm'


def layer_norm(x, g, b):
    xf = x.astype(F32)
    mu = jnp.mean(xf, -1, keepdims=True)
    var = jnp.mean(jnp.square(xf - mu), -1, keepdims=True)
    return ((xf - mu) * lax.rsqrt(var + LN_EPS) * g.astype(F32) + b.astype(F32)).astype(x.dtype)


def rope(x, pos):
    half = HEAD_DIM // 2
    inv = ROPE_THETA ** (-jnp.arange(half, dtype=F32) / half)
    ang = pos[..., None, None] * inv
    cos, sin = jnp.cos(ang), jnp.sin(ang)
    xf = x.astype(F32)
    x1, x2 = xf[..., :half], xf[..., half:]
    return jnp.concatenate([x1 * cos - x2 * sin, x2 * cos + x1 * sin], -1).astype(x.dtype)


def s5_mixer(u, lam_re, lam_im, log_dt, b_re, b_im, c_re, c_im, d_skip, w_glu):
    bsz, L, _ = u.shape
    uf = u.astype(F32).reshape(bsz, L, SSM_GROUPS, SSM_CH_PER_GROUP)
    lr, li = lam_re.astype(F32), lam_im.astype(F32)
    dt = jnp.exp(log_dt.astype(F32))[:, None]
    mag = jnp.exp(lr * dt)
    ar, ai = mag * jnp.cos(li * dt), mag * jnp.sin(li * dt)
    zr, zi = ar - 1.0, ai
    den = lr * lr + li * li
    fr, fi = (zr * lr + zi * li) / den, (zi * lr - zr * li) / den
    br_, bi_ = b_re.astype(F32), b_im.astype(F32)
    bbr = fr[..., None] * br_ - fi[..., None] * bi_
    bbi = fr[..., None] * bi_ + fi[..., None] * br_
    xr = jnp.einsum('blgh,gph->lbgp', uf, bbr)
    xi = jnp.einsum('blgh,gph->lbgp', uf, bbi)
    ar_t = jnp.broadcast_to(ar[None, None], (L, 1) + ar.shape)
    ai_t = jnp.broadcast_to(ai[None, None], (L, 1) + ai.shape)

    def combine(e1, e2):
        a1r, a1i, b1r, b1i = e1
        a2r, a2i, b2r, b2i = e2
        return (a1r * a2r - a1i * a2i, a1r * a2i + a1i * a2r,
                a2r * b1r - a2i * b1i + b2r, a2r * b1i + a2i * b1r + b2i)

    _, _, sr, si = lax.associative_scan(combine, (ar_t, ai_t, xr, xi), axis=0)
    y = jnp.einsum('lbgp,ghp->blgh', sr, c_re.astype(F32)) - jnp.einsum('lbgp,ghp->blgh', si, c_im.astype(F32))
    y = y + d_skip.astype(F32) * uf
    y = jax.nn.gelu(y.reshape(bsz, L, SSM_WIDTH).astype(u.dtype))
    return y * jax.nn.sigmoid(y @ w_glu)


def attend(scores, mask):
    s = jnp.where(mask, scores.astype(F32), NEG)
    return jax.nn.softmax(s, axis=-1) * mask


def nsa_mixer(q, k_cmp, v_cmp, k_sel, v_sel, k_win, v_win, gate_logits, positions,
              cmp_pos_k, cmp_pos_v, w_cmp_k1, w_cmp_k2, w_cmp_v1, w_cmp_v2):
    bsz, L = q.shape[:2]
    pos = positions.astype(F32)
    n_cmp = (L - CMP_BLOCK) // CMP_STRIDE + 1
    n_sel = L // SEL_BLOCK
    topk = min(SEL_TOPK, n_sel)
    n_qblk = L // Q_BLOCK
    scale = HEAD_DIM ** -0.5
    cidx = np.arange(n_cmp)[:, None] * CMP_STRIDE + np.arange(CMP_BLOCK)[None, :]

    def compress(kv, pe, w1, w2):
        blk = kv[:, cidx] + pe[:, None, :]
        blk = jnp.moveaxis(blk, 3, 2).reshape(bsz, n_cmp, NSA_KV_HEADS, CMP_BLOCK * HEAD_DIM)
        return jax.nn.gelu(blk @ w1) @ w2

    kc = rope(compress(k_cmp, cmp_pos_k, w_cmp_k1, w_cmp_k2), jnp.mean(pos[:, cidx], -1))
    vc = compress(v_cmp, cmp_pos_v, w_cmp_v1, w_cmp_v2)
    q = rope(q, pos)
    ks_t = jnp.transpose(rope(k_sel, pos), (0, 2, 1, 3))
    vs_t = jnp.transpose(v_sel, (0, 2, 1, 3))
    pad = ((0, 0), (WINDOW, 0), (0, 0), (0, 0))
    kw_p = jnp.pad(rope(k_win, pos), pad)
    vw_p = jnp.pad(v_win, pad)

    cs = np.arange(n_cmp) * CMP_STRIDE
    ce = cs + CMP_BLOCK - 1
    ss = np.arange(n_sel) * SEL_BLOCK
    se = ss + SEL_BLOCK - 1
    overlap = jnp.asarray(((cs[:, None] <= se[None, :]) & (ce[:, None] >= ss[None, :])).astype(np.float32))
    cmp_end = jnp.asarray(ce)
    sel_start = jnp.asarray(ss)
    blk_ids = jnp.arange(n_sel)
    b_ix = jnp.arange(bsz)[:, None, None]
    h_ix = jnp.arange(NSA_KV_HEADS)[None, :, None]

    q_blk = jnp.moveaxis(q.reshape(bsz, n_qblk, Q_BLOCK, NSA_KV_HEADS, Q_PER_KV, HEAD_DIM), 1, 0)
    gates = jax.nn.sigmoid(gate_logits)
    g_blk = jnp.moveaxis(gates.reshape(bsz, n_qblk, Q_BLOCK, NSA_KV_HEADS, Q_PER_KV, N_BRANCH), 1, 0)

    def block_fn(args):
        qi, qb, gb = args
        t = qi * Q_BLOCK + jnp.arange(Q_BLOCK)
        s_c = jnp.einsum('bqkgd,bnkd->bkgqn', qb, kc) * scale
        p_c = attend(s_c, cmp_end[None, :] <= t[:, None])
        o_c = jnp.einsum('bkgqn,bnkd->bqkgd', p_c.astype(vc.dtype), vc)
        imp = jnp.einsum('bkgqn,ns->bkqs', p_c, overlap)
        cur = t // SEL_BLOCK
        forced = (blk_ids[None, :] == 0) | (blk_ids[None, :] == cur[:, None]) | (blk_ids[None, :] == cur[:, None] - 1)
        valid = sel_start[None, :] <= t[:, None]
        score = jnp.where(forced, FORCE, jnp.where(valid, imp, -1.0))
        _, sel = lax.top_k(score, topk)
        tok = (sel[..., None] * SEL_BLOCK + jnp.arange(SEL_BLOCK)).reshape(bsz, NSA_KV_HEADS, Q_BLOCK * topk * SEL_BLOCK)
        kg = ks_t[b_ix, h_ix, tok].reshape(bsz, NSA_KV_HEADS, Q_BLOCK, topk * SEL_BLOCK, HEAD_DIM)
        vg = vs_t[b_ix, h_ix, tok].reshape(bsz, NSA_KV_HEADS, Q_BLOCK, topk * SEL_BLOCK, HEAD_DIM)
        tok = tok.reshape(bsz, NSA_KV_HEADS, Q_BLOCK, topk * SEL_BLOCK)
        s_s = jnp.einsum('bqkgd,bkqsd->bkgqs', qb, kg) * scale
        p_s = attend(s_s, (tok <= t[:, None])[:, :, None])
        o_s = jnp.einsum('bkgqs,bkqsd->bqkgd', p_s.astype(vg.dtype), vg)
        kwb = lax.dynamic_slice_in_dim(kw_p, qi * Q_BLOCK, Q_BLOCK + WINDOW, axis=1)
        vwb = lax.dynamic_slice_in_dim(vw_p, qi * Q_BLOCK, Q_BLOCK + WINDOW, axis=1)
        kpos = qi * Q_BLOCK - WINDOW + jnp.arange(Q_BLOCK + WINDOW)
        diff = t[:, None] - kpos[None, :]
        wmask = (diff >= 0) & (diff < WINDOW) & (kpos[None, :] >= 0)
        s_w = jnp.einsum('bqkgd,bskd->bkgqs', qb, kwb) * scale
        p_w = attend(s_w, wmask)
        o_w = jnp.einsum('bkgqs,bskd->bqkgd', p_w.astype(vwb.dtype), vwb)
        return gb[..., 0:1] * o_c + gb[..., 1:2] * o_s + gb[..., 2:3] * o_w

    out = lax.map(block_fn, (jnp.arange(n_qblk), q_blk, g_blk))
    return jnp.moveaxis(out, 0, 1).reshape(bsz, L, NSA_WIDTH)


def swiglu(x, wg, wu, wd):
    return (jax.nn.silu(x @ wg) * (x @ wu)) @ wd


def moe_ffn(x, w_router, router_bias, w_gate, w_up, w_down, ws_gate, ws_up, ws_down):
    bsz, L, d = x.shape
    xt = x.reshape(-1, d)
    n_tok = xt.shape[0]
    affin = jax.nn.sigmoid((xt @ w_router).astype(F32))
    biased = affin + router_bias.astype(F32)
    grp = biased.reshape(n_tok, N_EXPERT_GROUPS, N_EXPERTS // N_EXPERT_GROUPS)
    grp_score = jnp.sum(lax.top_k(grp, 2)[0], -1)
    _, gsel = lax.top_k(grp_score, TOPK_GROUPS)
    gmask = jnp.sum(jax.nn.one_hot(gsel, N_EXPERT_GROUPS, dtype=F32), 1)
    gmask = jnp.repeat(gmask, N_EXPERTS // N_EXPERT_GROUPS, axis=1) > 0
    _, eidx = lax.top_k(jnp.where(gmask, biased, NEG), TOP_K)
    w = jnp.take_along_axis(affin, eidx, axis=1)
    w = w / jnp.sum(w, -1, keepdims=True) * ROUTED_SCALE
    n_asg = n_tok * TOP_K
    flat_e = eidx.reshape(-1)
    order = jnp.argsort(flat_e)
    e_sorted = flat_e[order]
    tok_sorted = (order // TOP_K).astype(jnp.int32)
    w_sorted = w.reshape(-1)[order]
    counts = jnp.zeros(N_EXPERTS, jnp.int32).at[flat_e].add(1)
    padded = (counts + MOE_BLOCK - 1) // MOE_BLOCK * MOE_BLOCK
    pad_end = jnp.cumsum(padded)
    pad_start = pad_end - padded
    start = jnp.cumsum(counts) - counts
    dest = pad_start[e_sorted] + jnp.arange(n_asg) - start[e_sorted]
    n_blk = -(-(n_asg + N_EXPERTS * (MOE_BLOCK - 1)) // MOE_BLOCK)
    buf_tok = jnp.zeros(n_blk * MOE_BLOCK, jnp.int32).at[dest].set(tok_sorted).reshape(n_blk, MOE_BLOCK)
    buf_w = jnp.zeros(n_blk * MOE_BLOCK, F32).at[dest].set(w_sorted).reshape(n_blk, MOE_BLOCK)
    blk_exp = jnp.minimum(jnp.searchsorted(pad_end, jnp.arange(n_blk) * MOE_BLOCK, side='right'), N_EXPERTS - 1)

    def step(acc, args):
        e, tk, wk = args
        yb = swiglu(xt[tk], w_gate[e], w_up[e], w_down[e])
        return acc.at[tk].add(yb * wk[:, None].astype(yb.dtype)), None

    routed, _ = lax.scan(step, jnp.zeros_like(xt), (blk_exp, buf_tok, buf_w))
    shared = swiglu(xt, ws_gate, ws_up, ws_down)
    return (routed + shared).reshape(bsz, L, d)


def hybrid_layer(x, positions, w_in, lam_re, lam_im, log_dt, ssm_b_re, ssm_b_im, ssm_c_re, ssm_c_im, ssm_d,
                 w_glu, cmp_pos_k, cmp_pos_v, w_cmp_k1, w_cmp_k2, w_cmp_v1, w_cmp_v2, w_out, ln1_g, ln1_b,
                 w_router, router_bias, w_gate, w_up, w_down, ws_gate, ws_up, ws_down, ln2_g, ln2_b):
    bsz, L, _ = x.shape
    sizes = [SSM_WIDTH, NSA_WIDTH] + [KV_WIDTH] * 6 + [NSA_HEADS * N_BRANCH]
    offsets = tuple(int(o) for o in np.cumsum(sizes)[:-1])
    proj = x @ w_in
    u, q, k_c, v_c, k_s, v_s, k_w, v_w, g = jnp.split(proj, offsets, axis=-1)
    kv = lambda t: t.reshape(bsz, L, NSA_KV_HEADS, HEAD_DIM)
    y_ssm = s5_mixer(u, lam_re, lam_im, log_dt, ssm_b_re, ssm_b_im, ssm_c_re, ssm_c_im, ssm_d, w_glu)
    y_nsa = nsa_mixer(q.reshape(bsz, L, NSA_HEADS, HEAD_DIM), kv(k_c), kv(v_c), kv(k_s), kv(v_s), kv(k_w), kv(v_w),
                      g.reshape(bsz, L, NSA_HEADS, N_BRANCH), positions,
                      cmp_pos_k, cmp_pos_v, w_cmp_k1, w_cmp_k2, w_cmp_v1, w_cmp_v2)
    mix = jnp.concatenate([y_ssm, y_nsa], -1) @ w_out
    x = layer_norm(DEEPNORM_ALPHA * x + mix, ln1_g, ln1_b)
    ffn = moe_ffn(x, w_router, router_bias, w_gate, w_up, w_down, ws_gate, ws_up, ws_down)
    return layer_norm(DEEPNORM_ALPHA * x + ffn, ln2_g, ln2_b)


def setup_inputs(seed: int = 0) -> dict:
    key = jax.random.key(seed)
    keys = iter(jax.random.split(key, 48))

    def nrm(shape, scale):
        return scale * jax.random.normal(next(keys), (DEPTH,) + shape, F32)

    x = jax.random.normal(next(keys), (BATCH, SEQ, D_MODEL), F32)
    offs = jax.random.randint(next(keys), (BATCH, 1), 0, 1024, jnp.int32)
    positions = offs + jnp.arange(SEQ, dtype=jnp.int32)[None, :]
    n_idx = jnp.arange(SSM_STATE, dtype=F32)
    lam_re = -0.5 + nrm((SSM_GROUPS, SSM_STATE), 0.01)
    lam_im = jnp.broadcast_to(math.pi * n_idx, (DEPTH, SSM_GROUPS, SSM_STATE)) + nrm((SSM_GROUPS, SSM_STATE), 0.01)
    log_dt = jax.random.uniform(next(keys), (DEPTH, SSM_GROUPS), F32, math.log(DT_MIN), math.log(DT_MAX))
    return {
        'x': x,
        'positions': positions,
        'w_in': nrm((D_MODEL, IN_COLS), D_MODEL ** -0.5),
        'lam_re': lam_re,
        'lam_im': lam_im,
        'log_dt': log_dt,
        'ssm_b_re': nrm((SSM_GROUPS, SSM_STATE, SSM_CH_PER_GROUP), (2.0 * SSM_CH_PER_GROUP) ** -0.5),
        'ssm_b_im': nrm((SSM_GROUPS, SSM_STATE, SSM_CH_PER_GROUP), (2.0 * SSM_CH_PER_GROUP) ** -0.5),
        'ssm_c_re': nrm((SSM_GROUPS, SSM_CH_PER_GROUP, SSM_STATE), (2.0 * SSM_STATE) ** -0.5),
        'ssm_c_im': nrm((SSM_GROUPS, SSM_CH_PER_GROUP, SSM_STATE), (2.0 * SSM_STATE) ** -0.5),
        'ssm_d': nrm((SSM_GROUPS, SSM_CH_PER_GROUP), 1.0),
        'w_glu': nrm((SSM_WIDTH, SSM_WIDTH), SSM_WIDTH ** -0.5),
        'cmp_pos_k': nrm((CMP_BLOCK, HEAD_DIM), 0.02),
        'cmp_pos_v': nrm((CMP_BLOCK, HEAD_DIM), 0.02),
        'w_cmp_k1': nrm((CMP_BLOCK * HEAD_DIM, CMP_HIDDEN), (CMP_BLOCK * HEAD_DIM) ** -0.5),
        'w_cmp_k2': nrm((CMP_HIDDEN, HEAD_DIM), CMP_HIDDEN ** -0.5),
        'w_cmp_v1': nrm((CMP_BLOCK * HEAD_DIM, CMP_HIDDEN), (CMP_BLOCK * HEAD_DIM) ** -0.5),
        'w_cmp_v2': nrm((CMP_HIDDEN, HEAD_DIM), CMP_HIDDEN ** -0.5),
        'w_out': nrm((D_MIX, D_MODEL), DEEPNORM_BETA * D_MIX ** -0.5),
        'ln1_g': 1.0 + nrm((D_MODEL,), 0.01),
        'ln1_b': nrm((D_MODEL,), 0.01),
        'w_router': nrm((D_MODEL, N_EXPERTS), D_MODEL ** -0.5),
        'router_bias': nrm((N_EXPERTS,), 0.01),
        'w_gate': nrm((N_EXPERTS, D_MODEL, EXPERT_FF), D_MODEL ** -0.5),
        'w_up': nrm((N_EXPERTS, D_MODEL, EXPERT_FF), D_MODEL ** -0.5),
        'w_down': nrm((N_EXPERTS, EXPERT_FF, D_MODEL), DEEPNORM_BETA * EXPERT_FF ** -0.5),
        'ws_gate': nrm((D_MODEL, SHARED_FF), D_MODEL ** -0.5),
        'ws_up': nrm((D_MODEL, SHARED_FF), D_MODEL ** -0.5),
        'ws_down': nrm((SHARED_FF, D_MODEL), DEEPNORM_BETA * SHARED_FF ** -0.5),
        'ln2_g': 1.0 + nrm((D_MODEL,), 0.01),
        'ln2_b': nrm((D_MODEL,), 0.01),
    }


def reference(x, positions, w_in, lam_re, lam_im, log_dt, ssm_b_re, ssm_b_im, ssm_c_re, ssm_c_im, ssm_d,
              w_glu, cmp_pos_k, cmp_pos_v, w_cmp_k1, w_cmp_k2, w_cmp_v1, w_cmp_v2, w_out, ln1_g, ln1_b,
              w_router, router_bias, w_gate, w_up, w_down, ws_gate, ws_up, ws_down, ln2_g, ln2_b):
    params = (w_in, lam_re, lam_im, log_dt, ssm_b_re, ssm_b_im, ssm_c_re, ssm_c_im, ssm_d,
              w_glu, cmp_pos_k, cmp_pos_v, w_cmp_k1, w_cmp_k2, w_cmp_v1, w_cmp_v2, w_out, ln1_g, ln1_b,
              w_router, router_bias, w_gate, w_up, w_down, ws_gate, ws_up, ws_down, ln2_g, ln2_b)
    for layer in range(DEPTH):
        x = hybrid_layer(x, positions, *(p[layer] for p in params))
    return x
```

```python
import functools
import math

import numpy as np
import jax
import jax.numpy as jnp
from jax import lax
from jax.experimental import pallas as pl
from jax.experimental.pallas import tpu as pltpu

D_MODEL = 2048
SSM_WIDTH = 1024
SSM_CH_PER_GROUP = 16
SSM_GROUPS = 64
SSM_STATE = 64
NSA_HEADS = 16
NSA_KV_HEADS = 2
HEAD_DIM = 64
Q_PER_KV = NSA_HEADS // NSA_KV_HEADS
NSA_WIDTH = NSA_HEADS * HEAD_DIM
KV_WIDTH = NSA_KV_HEADS * HEAD_DIM
N_BRANCH = 3
CMP_BLOCK = 32
CMP_STRIDE = 16
SEL_BLOCK = 64
SEL_TOPK = 16
WINDOW = 512
Q_BLOCK = 128
ROPE_THETA = 10000.0
N_EXPERTS = 64
TOP_K = 8
N_EXPERT_GROUPS = 8
TOPK_GROUPS = 4
ROUTED_SCALE = 2.5
MOE_BLOCK = 128
DEPTH = 1
DEEPNORM_ALPHA = (2.0 * DEPTH) ** 0.25
LN_EPS = 1e-5
NEG = -1e30
FORCE = 1e4
F32 = jnp.float32
BF16 = jnp.bfloat16

V7X_VMEM_LIMIT_BYTES = 56 * 1024 * 1024


def _mm_kernel(a_ref, b_ref, o_ref):
    o_ref[...] = jnp.dot(a_ref[...].astype(BF16), b_ref[...].astype(BF16), preferred_element_type=F32)


def matmul(a, b, tm=512, tn=None):
    m, k = a.shape
    _, n = b.shape
    tn = n if tn is None else tn
    return pl.pallas_call(
        _mm_kernel,
        grid=(m // tm, n // tn),
        in_specs=[pl.BlockSpec((tm, k), lambda i, j: (i, 0)), pl.BlockSpec((k, tn), lambda i, j: (0, j))],
        out_specs=pl.BlockSpec((tm, tn), lambda i, j: (i, j)),
        out_shape=jax.ShapeDtypeStruct((m, n), F32),
        compiler_params=pltpu.CompilerParams(
            dimension_semantics=("arbitrary", "arbitrary"), vmem_limit_bytes=V7X_VMEM_LIMIT_BYTES),
        name="matmul",
    )(a, b)


def layer_norm(x, g, b):
    mu = jnp.mean(x, -1, keepdims=True)
    var = jnp.mean(jnp.square(x - mu), -1, keepdims=True)
    return (x - mu) * lax.rsqrt(var + LN_EPS) * g + b


def rope(x, pos):
    half = HEAD_DIM // 2
    inv = ROPE_THETA ** (-jnp.arange(half, dtype=F32) / half)
    ang = pos[..., None, None] * inv
    cos, sin = jnp.cos(ang), jnp.sin(ang)
    x1, x2 = x[..., :half], x[..., half:]
    return jnp.concatenate([x1 * cos - x2 * sin, x2 * cos + x1 * sin], -1)


def s5_mixer(u, lam_re, lam_im, log_dt, b_re, b_im, c_re, c_im, d_skip, w_glu):
    bsz, L, _ = u.shape
    uf = u.reshape(bsz, L, SSM_GROUPS, SSM_CH_PER_GROUP)
    lr, li = lam_re, lam_im
    dt = jnp.exp(log_dt)[:, None]
    mag = jnp.exp(lr * dt)
    ar, ai = mag * jnp.cos(li * dt), mag * jnp.sin(li * dt)
    zr, zi = ar - 1.0, ai
    den = lr * lr + li * li
    fr, fi = (zr * lr + zi * li) / den, (zi * lr - zr * li) / den
    bbr = fr[..., None] * b_re - fi[..., None] * b_im
    bbi = fr[..., None] * b_im + fi[..., None] * b_re
    xr = jnp.einsum('blgh,gph->lbgp', uf, bbr)
    xi = jnp.einsum('blgh,gph->lbgp', uf, bbi)
    ar_t = jnp.broadcast_to(ar[None, None], (L, 1) + ar.shape)
    ai_t = jnp.broadcast_to(ai[None, None], (L, 1) + ai.shape)

    def combine(e1, e2):
        a1r, a1i, b1r, b1i = e1
        a2r, a2i, b2r, b2i = e2
        return (a1r * a2r - a1i * a2i, a1r * a2i + a1i * a2r,
                a2r * b1r - a2i * b1i + b2r, a2r * b1i + a2i * b1r + b2i)

    _, _, sr, si = lax.associative_scan(combine, (ar_t, ai_t, xr, xi), axis=0)
    y = jnp.einsum('lbgp,ghp->blgh', sr, c_re) - jnp.einsum('lbgp,ghp->blgh', si, c_im)
    y = y + d_skip * uf
    y = jax.nn.gelu(y.reshape(bsz, L, SSM_WIDTH))
    return y * jax.nn.sigmoid(matmul(y.reshape(bsz * L, SSM_WIDTH), w_glu).reshape(bsz, L, SSM_WIDTH))


def attend(scores, mask):
    s = jnp.where(mask, scores, NEG)
    return jax.nn.softmax(s, axis=-1) * mask


def nsa_mixer(q, k_cmp, v_cmp, k_sel, v_sel, k_win, v_win, gate_logits, positions,
              cmp_pos_k, cmp_pos_v, w_cmp_k1, w_cmp_k2, w_cmp_v1, w_cmp_v2):
    bsz, L = q.shape[:2]
    pos = positions.astype(F32)
    n_cmp = (L - CMP_BLOCK) // CMP_STRIDE + 1
    n_sel = L // SEL_BLOCK
    topk = min(SEL_TOPK, n_sel)
    n_qblk = L // Q_BLOCK
    scale = HEAD_DIM ** -0.5
    cidx = np.arange(n_cmp)[:, None] * CMP_STRIDE + np.arange(CMP_BLOCK)[None, :]

    def compress(kv, pe, w1, w2):
        blk = kv[:, cidx] + pe[:, None, :]
        blk = jnp.moveaxis(blk, 3, 2).reshape(bsz, n_cmp, NSA_KV_HEADS, CMP_BLOCK * HEAD_DIM)
        return jax.nn.gelu(blk @ w1) @ w2

    kc = rope(compress(k_cmp, cmp_pos_k, w_cmp_k1, w_cmp_k2), jnp.mean(pos[:, cidx], -1))
    vc = compress(v_cmp, cmp_pos_v, w_cmp_v1, w_cmp_v2)
    q = rope(q, pos)
    ks_t = jnp.transpose(rope(k_sel, pos), (0, 2, 1, 3))
    vs_t = jnp.transpose(v_sel, (0, 2, 1, 3))
    pad = ((0, 0), (WINDOW, 0), (0, 0), (0, 0))
    kw_p = jnp.pad(rope(k_win, pos), pad)
    vw_p = jnp.pad(v_win, pad)

    cs = np.arange(n_cmp) * CMP_STRIDE
    ce = cs + CMP_BLOCK - 1
    ss = np.arange(n_sel) * SEL_BLOCK
    se = ss + SEL_BLOCK - 1
    overlap = jnp.asarray(((cs[:, None] <= se[None, :]) & (ce[:, None] >= ss[None, :])).astype(np.float32))
    cmp_end = jnp.asarray(ce)
    sel_start = jnp.asarray(ss)
    blk_ids = jnp.arange(n_sel)
    b_ix = jnp.arange(bsz)[:, None, None]
    h_ix = jnp.arange(NSA_KV_HEADS)[None, :, None]

    q_blk = jnp.moveaxis(q.reshape(bsz, n_qblk, Q_BLOCK, NSA_KV_HEADS, Q_PER_KV, HEAD_DIM), 1, 0)
    gates = jax.nn.sigmoid(gate_logits)
    g_blk = jnp.moveaxis(gates.reshape(bsz, n_qblk, Q_BLOCK, NSA_KV_HEADS, Q_PER_KV, N_BRANCH), 1, 0)

    def block_fn(args):
        qi, qb, gb = args
        t = qi * Q_BLOCK + jnp.arange(Q_BLOCK)
        s_c = jnp.einsum('bqkgd,bnkd->bkgqn', qb, kc) * scale
        p_c = attend(s_c, cmp_end[None, :] <= t[:, None])
        o_c = jnp.einsum('bkgqn,bnkd->bqkgd', p_c, vc)
        imp = jnp.einsum('bkgqn,ns->bkqs', p_c, overlap)
        cur = t // SEL_BLOCK
        forced = (blk_ids[None, :] == 0) | (blk_ids[None, :] == cur[:, None]) | (blk_ids[None, :] == cur[:, None] - 1)
        valid = sel_start[None, :] <= t[:, None]
        score = jnp.where(forced, FORCE, jnp.where(valid, imp, -1.0))
        _, sel = lax.top_k(score, topk)
        tok = (sel[..., None] * SEL_BLOCK + jnp.arange(SEL_BLOCK)).reshape(bsz, NSA_KV_HEADS, Q_BLOCK * topk * SEL_BLOCK)
        kg = ks_t[b_ix, h_ix, tok].reshape(bsz, NSA_KV_HEADS, Q_BLOCK, topk * SEL_BLOCK, HEAD_DIM)
        vg = vs_t[b_ix, h_ix, tok].reshape(bsz, NSA_KV_HEADS, Q_BLOCK, topk * SEL_BLOCK, HEAD_DIM)
        tok = tok.reshape(bsz, NSA_KV_HEADS, Q_BLOCK, topk * SEL_BLOCK)
        s_s = jnp.einsum('bqkgd,bkqsd->bkgqs', qb, kg) * scale
        p_s = attend(s_s, (tok <= t[:, None])[:, :, None])
        o_s = jnp.einsum('bkgqs,bkqsd->bqkgd', p_s, vg)
        kwb = lax.dynamic_slice_in_dim(kw_p, qi * Q_BLOCK, Q_BLOCK + WINDOW, axis=1)
        vwb = lax.dynamic_slice_in_dim(vw_p, qi * Q_BLOCK, Q_BLOCK + WINDOW, axis=1)
        kpos = qi * Q_BLOCK - WINDOW + jnp.arange(Q_BLOCK + WINDOW)
        diff = t[:, None] - kpos[None, :]
        wmask = (diff >= 0) & (diff < WINDOW) & (kpos[None, :] >= 0)
        s_w = jnp.einsum('bqkgd,bskd->bkgqs', qb, kwb) * scale
        p_w = attend(s_w, wmask)
        o_w = jnp.einsum('bkgqs,bskd->bqkgd', p_w, vwb)
        return gb[..., 0:1] * o_c + gb[..., 1:2] * o_s + gb[..., 2:3] * o_w

    out = lax.map(block_fn, (jnp.arange(n_qblk), q_blk, g_blk))
    return jnp.moveaxis(out, 0, 1).reshape(bsz, L, NSA_WIDTH)


def swiglu(x, wg, wu, wd):
    return (jax.nn.silu(x @ wg) * (x @ wu)) @ wd


def moe_ffn(x, w_router, router_bias, w_gate, w_up, w_down, ws_gate, ws_up, ws_down):
    bsz, L, d = x.shape
    xt = x.reshape(-1, d)
    n_tok = xt.shape[0]
    affin = jax.nn.sigmoid(xt @ w_router)
    biased = affin + router_bias
    grp = biased.reshape(n_tok, N_EXPERT_GROUPS, N_EXPERTS // N_EXPERT_GROUPS)
    grp_score = jnp.sum(lax.top_k(grp, 2)[0], -1)
    _, gsel = lax.top_k(grp_score, TOPK_GROUPS)
    gmask = jnp.sum(jax.nn.one_hot(gsel, N_EXPERT_GROUPS, dtype=F32), 1)
    gmask = jnp.repeat(gmask, N_EXPERTS // N_EXPERT_GROUPS, axis=1) > 0
    _, eidx = lax.top_k(jnp.where(gmask, biased, NEG), TOP_K)
    w = jnp.take_along_axis(affin, eidx, axis=1)
    w = w / jnp.sum(w, -1, keepdims=True) * ROUTED_SCALE
    n_asg = n_tok * TOP_K
    flat_e = eidx.reshape(-1)
    order = jnp.argsort(flat_e)
    e_sorted = flat_e[order]
    tok_sorted = (order // TOP_K).astype(jnp.int32)
    w_sorted = w.reshape(-1)[order]
    counts = jnp.zeros(N_EXPERTS, jnp.int32).at[flat_e].add(1)
    padded = (counts + MOE_BLOCK - 1) // MOE_BLOCK * MOE_BLOCK
    pad_end = jnp.cumsum(padded)
    pad_start = pad_end - padded
    start = jnp.cumsum(counts) - counts
    dest = pad_start[e_sorted] + jnp.arange(n_asg) - start[e_sorted]
    n_blk = -(-(n_asg + N_EXPERTS * (MOE_BLOCK - 1)) // MOE_BLOCK)
    buf_tok = jnp.zeros(n_blk * MOE_BLOCK, jnp.int32).at[dest].set(tok_sorted).reshape(n_blk, MOE_BLOCK)
    buf_w = jnp.zeros(n_blk * MOE_BLOCK, F32).at[dest].set(w_sorted).reshape(n_blk, MOE_BLOCK)
    blk_exp = jnp.minimum(jnp.searchsorted(pad_end, jnp.arange(n_blk) * MOE_BLOCK, side='right'), N_EXPERTS - 1)

    def step(acc, args):
        e, tk, wk = args
        yb = swiglu(xt[tk], w_gate[e], w_up[e], w_down[e])
        return acc.at[tk].add(yb * wk[:, None]), None

    routed, _ = lax.scan(step, jnp.zeros_like(xt), (blk_exp, buf_tok, buf_w))
    h = jax.nn.silu(matmul(xt, ws_gate)) * matmul(xt, ws_up)
    shared = matmul(h, ws_down)
    return (routed + shared).reshape(bsz, L, d)


def hybrid_layer(x, positions, w_in, lam_re, lam_im, log_dt, ssm_b_re, ssm_b_im, ssm_c_re, ssm_c_im, ssm_d,
                 w_glu, cmp_pos_k, cmp_pos_v, w_cmp_k1, w_cmp_k2, w_cmp_v1, w_cmp_v2, w_out, ln1_g, ln1_b,
                 w_router, router_bias, w_gate, w_up, w_down, ws_gate, ws_up, ws_down, ln2_g, ln2_b):
    bsz, L, _ = x.shape
    sizes = [SSM_WIDTH, NSA_WIDTH] + [KV_WIDTH] * 6 + [NSA_HEADS * N_BRANCH]
    offsets = tuple(int(o) for o in np.cumsum(sizes)[:-1])
    xt = x.reshape(bsz * L, D_MODEL)
    n_in = w_in.shape[1]
    n_pad = -(-n_in // 128) * 128
    w_in_p = jnp.pad(w_in, ((0, 0), (0, n_pad - n_in))).astype(BF16)
    proj = matmul(xt, w_in_p)[:, :n_in].reshape(bsz, L, n_in)
    u, q, k_c, v_c, k_s, v_s, k_w, v_w, g = jnp.split(proj, offsets, axis=-1)
    kv = lambda t: t.reshape(bsz, L, NSA_KV_HEADS, HEAD_DIM)
    y_ssm = s5_mixer(u, lam_re, lam_im, log_dt, ssm_b_re, ssm_b_im, ssm_c_re, ssm_c_im, ssm_d, w_glu)
    y_nsa = nsa_mixer(q.reshape(bsz, L, NSA_HEADS, HEAD_DIM), kv(k_c), kv(v_c), kv(k_s), kv(v_s), kv(k_w), kv(v_w),
                      g.reshape(bsz, L, NSA_HEADS, N_BRANCH), positions,
                      cmp_pos_k, cmp_pos_v, w_cmp_k1, w_cmp_k2, w_cmp_v1, w_cmp_v2)
    mix = matmul(jnp.concatenate([y_ssm, y_nsa], -1).reshape(bsz * L, D_MODEL), w_out, tn=1024)
    x = layer_norm(DEEPNORM_ALPHA * x + mix.reshape(bsz, L, D_MODEL), ln1_g, ln1_b)
    ffn = moe_ffn(x, w_router, router_bias, w_gate, w_up, w_down, ws_gate, ws_up, ws_down)
    return layer_norm(DEEPNORM_ALPHA * x + ffn, ln2_g, ln2_b)


def kernel(x, positions, w_in, lam_re, lam_im, log_dt, ssm_b_re, ssm_b_im, ssm_c_re, ssm_c_im, ssm_d, w_glu, cmp_pos_k, cmp_pos_v, w_cmp_k1, w_cmp_k2, w_cmp_v1, w_cmp_v2, w_out, ln1_g, ln1_b, w_router, router_bias, w_gate, w_up, w_down, ws_gate, ws_up, ws_down, ln2_g, ln2_b):
    params = (w_in, lam_re, lam_im, log_dt, ssm_b_re, ssm_b_im, ssm_c_re, ssm_c_im, ssm_d,
              w_glu, cmp_pos_k, cmp_pos_v, w_cmp_k1, w_cmp_k2, w_cmp_v1, w_cmp_v2, w_out, ln1_g, ln1_b,
              w_router, router_bias, w_gate, w_up, w_down, ws_gate, ws_up, ws_down, ln2_g, ln2_b)
    return hybrid_layer(x, positions, *(p[0] for p in params))
```

```python
import functools
import math

import numpy as np
import jax
import jax.numpy as jnp
from jax import lax
from jax.experimental import pallas as pl
from jax.experimental.pallas import tpu as pltpu

D_MODEL = 2048
SSM_WIDTH = 1024
SSM_CH_PER_GROUP = 16
SSM_GROUPS = 64
SSM_STATE = 64
NSA_HEADS = 16
NSA_KV_HEADS = 2
HEAD_DIM = 64
Q_PER_KV = NSA_HEADS // NSA_KV_HEADS
NSA_WIDTH = NSA_HEADS * HEAD_DIM
KV_WIDTH = NSA_KV_HEADS * HEAD_DIM
N_BRANCH = 3
CMP_BLOCK = 32
CMP_STRIDE = 16
SEL_BLOCK = 64
SEL_TOPK = 16
WINDOW = 512
Q_BLOCK = 128
ROPE_THETA = 10000.0
N_EXPERTS = 64
TOP_K = 8
N_EXPERT_GROUPS = 8
TOPK_GROUPS = 4
ROUTED_SCALE = 2.5
MOE_BLOCK = 128
DEPTH = 1
DEEPNORM_ALPHA = (2.0 * DEPTH) ** 0.25
LN_EPS = 1e-5
NEG = -1e30
FORCE = 1e4
F32 = jnp.float32
BF16 = jnp.bfloat16

V7X_VMEM_LIMIT_BYTES = 56 * 1024 * 1024


def _mm_kernel(a_ref, b_ref, o_ref):
    o_ref[...] = jnp.dot(a_ref[...].astype(BF16), b_ref[...].astype(BF16), preferred_element_type=F32)


def matmul(a, b, tm=512, tn=None):
    m, k = a.shape
    _, n = b.shape
    tn = n if tn is None else tn
    return pl.pallas_call(
        _mm_kernel,
        grid=(m // tm, n // tn),
        in_specs=[pl.BlockSpec((tm, k), lambda i, j: (i, 0)), pl.BlockSpec((k, tn), lambda i, j: (0, j))],
        out_specs=pl.BlockSpec((tm, tn), lambda i, j: (i, j)),
        out_shape=jax.ShapeDtypeStruct((m, n), F32),
        compiler_params=pltpu.CompilerParams(
            dimension_semantics=("arbitrary", "arbitrary"), vmem_limit_bytes=V7X_VMEM_LIMIT_BYTES),
        name="matmul",
    )(a, b)


def layer_norm(x, g, b):
    mu = jnp.mean(x, -1, keepdims=True)
    var = jnp.mean(jnp.square(x - mu), -1, keepdims=True)
    return (x - mu) * lax.rsqrt(var + LN_EPS) * g + b


def rope(x, pos):
    half = HEAD_DIM // 2
    inv = ROPE_THETA ** (-jnp.arange(half, dtype=F32) / half)
    ang = pos[..., None, None] * inv
    cos, sin = jnp.cos(ang), jnp.sin(ang)
    x1, x2 = x[..., :half], x[..., half:]
    return jnp.concatenate([x1 * cos - x2 * sin, x2 * cos + x1 * sin], -1)


def s5_mixer(u, lam_re, lam_im, log_dt, b_re, b_im, c_re, c_im, d_skip, w_glu):
    bsz, L, _ = u.shape
    uf = u.reshape(bsz, L, SSM_GROUPS, SSM_CH_PER_GROUP)
    lr, li = lam_re, lam_im
    dt = jnp.exp(log_dt)[:, None]
    mag = jnp.exp(lr * dt)
    ar, ai = mag * jnp.cos(li * dt), mag * jnp.sin(li * dt)
    zr, zi = ar - 1.0, ai
    den = lr * lr + li * li
    fr, fi = (zr * lr + zi * li) / den, (zi * lr - zr * li) / den
    bbr = fr[..., None] * b_re - fi[..., None] * b_im
    bbi = fr[..., None] * b_im + fi[..., None] * b_re
    xr = jnp.einsum('blgh,gph->lbgp', uf, bbr)
    xi = jnp.einsum('blgh,gph->lbgp', uf, bbi)
    ar_t = jnp.broadcast_to(ar[None, None], (L, 1) + ar.shape)
    ai_t = jnp.broadcast_to(ai[None, None], (L, 1) + ai.shape)

    def combine(e1, e2):
        a1r, a1i, b1r, b1i = e1
        a2r, a2i, b2r, b2i = e2
        return (a1r * a2r - a1i * a2i, a1r * a2i + a1i * a2r,
                a2r * b1r - a2i * b1i + b2r, a2r * b1i + a2i * b1r + b2i)

    _, _, sr, si = lax.associative_scan(combine, (ar_t, ai_t, xr, xi), axis=0)
    y = jnp.einsum('lbgp,ghp->blgh', sr, c_re) - jnp.einsum('lbgp,ghp->blgh', si, c_im)
    y = y + d_skip * uf
    y = jax.nn.gelu(y.reshape(bsz, L, SSM_WIDTH))
    return y * jax.nn.sigmoid(matmul(y.reshape(bsz * L, SSM_WIDTH), w_glu).reshape(bsz, L, SSM_WIDTH))


def _softmax_tile(s, m_old):
    m_new = jnp.maximum(m_old, jnp.max(s, axis=1, keepdims=True))
    return m_new, jnp.exp(m_old - m_new), jnp.exp(s - m_new)


def _lane_is_low(shape):
    return lax.broadcasted_iota(jnp.int32, shape, 1) < HEAD_DIM


def _pad_kt(kt, variant):
    z = jnp.zeros_like(kt)
    return jnp.concatenate([kt, z] if variant == 0 else [z, kt], axis=0)


def _pad_v(vv, variant):
    low = _lane_is_low(vv.shape)
    keep = low if variant == 0 else jnp.logical_not(low)
    return jnp.where(keep, vv, jnp.ones_like(vv))


def _finish(acc, variant):
    lane = lax.broadcasted_iota(jnp.int32, acc.shape, 1)
    lsel = lane == (HEAD_DIM if variant == 0 else 0)
    l = jnp.sum(jnp.where(lsel, acc, 0.0), axis=1, keepdims=True)
    keep = (lane < HEAD_DIM) if variant == 0 else (lane >= HEAD_DIM)
    return jnp.where(keep, acc / l, 0.0)


def _nsa_kernel(q_ref, g_ref, kct_ref, vc_ref, kst_ref, vs_ref, kwt_ref, vw_ref, ovl_ref, gx_ref, o_ref,
                m_scr, acc_scr, *, seq_len):
    n_sel = seq_len // SEL_BLOCK
    n_cpad = seq_len // CMP_STRIDE
    sel_tile = 512
    blocks_per_tile = sel_tile // SEL_BLOCK
    win_tiles = WINDOW // Q_BLOCK + 1
    n_pair = Q_PER_KV // 2
    rows = n_pair * Q_BLOCK
    i = pl.program_id(2)
    t0 = i * Q_BLOCK

    qb = q_ref[0]
    qst = jnp.concatenate([qb[:, p * 128:(p + 1) * 128] for p in range(n_pair)], axis=0)

    sig = jax.nn.sigmoid(g_ref[0])
    sig_hi = sig.astype(BF16)
    sig_lo = (sig - sig_hi.astype(F32)).astype(BF16)
    gx = gx_ref[0]
    gexp = (jnp.dot(sig_hi, gx, preferred_element_type=F32) + jnp.dot(sig_lo, gx, preferred_element_type=F32))

    def gate_of(branch):
        base = branch * n_pair * 128
        return jnp.concatenate([gexp[:, base + p * 128: base + (p + 1) * 128] for p in range(n_pair)], axis=0)

    t_row = t0 + lax.broadcasted_iota(jnp.int32, (Q_BLOCK, 1), 0)

    n_iota = lax.broadcasted_iota(jnp.int32, (Q_BLOCK, n_cpad), 1)
    cmask = (n_iota * CMP_STRIDE + (CMP_BLOCK - 1)) <= t_row
    cmask4 = jnp.concatenate([cmask] * n_pair, axis=0)
    kct = kct_ref[0, 0]
    vcd = vc_ref[0, 0]
    p_sum = jnp.zeros((Q_BLOCK, n_cpad), F32)
    out = jnp.zeros((rows, 128), F32)
    o_c = jnp.zeros((rows, 128), F32)
    for v in range(2):
        s = jnp.dot(qst, _pad_kt(kct, v), preferred_element_type=F32)
        s = jnp.where(cmask4, s, NEG)
        m = jnp.max(s, axis=1, keepdims=True)
        e = jnp.where(cmask4, jnp.exp(s - m), 0.0)
        l = jnp.sum(e, axis=1, keepdims=True)
        p = e * (1.0 / jnp.maximum(l, 1e-30))
        for pp in range(n_pair):
            p_sum = p_sum + p[pp * Q_BLOCK:(pp + 1) * Q_BLOCK]
        low = _lane_is_low((n_cpad, 128))
        vz = jnp.where(low if v == 0 else jnp.logical_not(low), vcd, jnp.zeros_like(vcd))
        o_c = o_c + jnp.dot(p.astype(BF16), vz, preferred_element_type=F32)
    out = out + gate_of(0) * o_c

    ps_hi = p_sum.astype(BF16)
    ps_lo = (p_sum - ps_hi.astype(F32)).astype(BF16)
    ovl = ovl_ref[...]
    nt = (((1,), (1,)), ((), ()))
    imp_t = (lax.dot_general(ovl, ps_hi, nt, preferred_element_type=F32)
             + lax.dot_general(ovl, ps_lo, nt, preferred_element_type=F32))
    s_iota = lax.broadcasted_iota(jnp.int32, (n_sel, Q_BLOCK), 0)
    t_lane = t0 + lax.broadcasted_iota(jnp.int32, (n_sel, Q_BLOCK), 1)
    cur = t_lane // SEL_BLOCK
    forced = (s_iota == 0) | (s_iota == cur) | (s_iota == cur - 1)
    valid = s_iota * SEL_BLOCK <= t_lane
    score = jnp.where(forced, FORCE, jnp.where(valid, imp_t, -1.0))
    s_f = s_iota.astype(F32)
    sel_t = jnp.zeros((n_sel, Q_BLOCK), F32)
    for _ in range(min(SEL_TOPK, n_sel)):
        mx = jnp.max(score, axis=0, keepdims=True)
        idx = jnp.min(jnp.where(score == mx, s_f, float(n_sel)), axis=0, keepdims=True)
        hit = s_f == idx
        sel_t = jnp.where(hit, 1.0, sel_t)
        score = jnp.where(hit, -3e38, score)
    selmask = sel_t.T.astype(BF16)

    m_scr[...] = jnp.full(m_scr.shape, NEG, F32)
    acc_scr[...] = jnp.zeros(acc_scr.shape, F32)
    n_tiles = (t0 + Q_BLOCK + sel_tile - 1) // sel_tile

    def sel_body(kt, carry):
        blk = kt * blocks_per_tile + lax.broadcasted_iota(jnp.int32, (n_sel, sel_tile), 1) // SEL_BLOCK
        expand = (lax.broadcasted_iota(jnp.int32, (n_sel, sel_tile), 0) == blk).astype(BF16)
        selexp = jnp.dot(selmask, expand, preferred_element_type=F32)
        kpos = kt * sel_tile + lax.broadcasted_iota(jnp.int32, (Q_BLOCK, sel_tile), 1)
        ok = (selexp > 0.5) & (kpos <= t_row)
        bias = jnp.where(ok, 0.0, NEG)
        bias4 = jnp.concatenate([bias] * n_pair, axis=0)
        kt_tile = kst_ref[0, 0, kt]
        v_tile = vs_ref[0, 0, kt]
        for v in range(2):
            s = jnp.dot(qst, _pad_kt(kt_tile, v), preferred_element_type=F32) + bias4
            m_new, alpha, p = _softmax_tile(s, m_scr[v])
            m_scr[v] = m_new
            acc_scr[v] = alpha * acc_scr[v] + jnp.dot(p.astype(BF16), _pad_v(v_tile, v), preferred_element_type=F32)
        return carry

    lax.fori_loop(0, n_tiles, sel_body, 0)
    out = out + gate_of(1) * (_finish(acc_scr[0], 0) + _finish(acc_scr[1], 1))

    n_kblk = seq_len // Q_BLOCK
    w0 = jnp.clip(i - (win_tiles - 1), 0, n_kblk - win_tiles)
    kw = jnp.concatenate([kwt_ref[0, 0, w0 + j] for j in range(win_tiles)], axis=1)
    vw = jnp.concatenate([vw_ref[0, 0, w0 + j] for j in range(win_tiles)], axis=0)
    kpos = w0 * Q_BLOCK + lax.broadcasted_iota(jnp.int32, (Q_BLOCK, win_tiles * Q_BLOCK), 1)
    diff = t_row - kpos
    wbias = jnp.where((diff >= 0) & (diff < WINDOW), 0.0, NEG)
    wbias4 = jnp.concatenate([wbias] * n_pair, axis=0)
    o_w = jnp.zeros((rows, 128), F32)
    for v in range(2):
        s = jnp.dot(qst, _pad_kt(kw, v), preferred_element_type=F32) + wbias4
        m = jnp.max(s, axis=1, keepdims=True)
        p = jnp.exp(s - m)
        o_w = o_w + _finish(jnp.dot(p.astype(BF16), _pad_v(vw, v), preferred_element_type=F32), v)
    out = out + gate_of(2) * o_w

    o_ref[0] = jnp.concatenate([out[p * Q_BLOCK:(p + 1) * Q_BLOCK] for p in range(n_pair)], axis=1)


def nsa_attention(q, gate_pad, kct, vc, kst, vs, kwt, vw):
    bsz, seq_len, _ = q.shape
    n_sel = seq_len // SEL_BLOCK
    n_cpad = seq_len // CMP_STRIDE
    n_cmp = (seq_len - CMP_BLOCK) // CMP_STRIDE + 1
    n_pair = Q_PER_KV // 2
    cs = np.arange(n_cpad) * CMP_STRIDE
    ce = cs + CMP_BLOCK - 1
    ss = np.arange(n_sel) * SEL_BLOCK
    se = ss + SEL_BLOCK - 1
    ovl = (cs[None, :] <= se[:, None]) & (ce[None, :] >= ss[:, None]) & (np.arange(n_cpad)[None, :] < n_cmp)
    ovl = jnp.asarray(ovl.astype(np.float32), BF16)
    gx = np.zeros((NSA_KV_HEADS, 128, N_BRANCH * n_pair * 128), np.float32)
    for k in range(NSA_KV_HEADS):
        for hl in range(Q_PER_KV):
            for br in range(N_BRANCH):
                c0 = br * n_pair * 128 + hl * HEAD_DIM
                gx[k, (k * Q_PER_KV + hl) * N_BRANCH + br, c0:c0 + HEAD_DIM] = 1.0
    gx = jnp.asarray(gx, BF16)
    width = Q_PER_KV * HEAD_DIM
    full = lambda *shape: pl.BlockSpec((1, 1) + shape, lambda b, k, i: (b, k) + (0,) * len(shape))
    return pl.pallas_call(
        functools.partial(_nsa_kernel, seq_len=seq_len),
        grid=(bsz, NSA_KV_HEADS, seq_len // Q_BLOCK),
        in_specs=[
            pl.BlockSpec((1, Q_BLOCK, width), lambda b, k, i: (b, i, k)),
            pl.BlockSpec((1, Q_BLOCK, 128), lambda b, k, i: (b, i, 0)),
            full(HEAD_DIM, n_cpad), full(n_cpad, 128),
            full(seq_len // 512, HEAD_DIM, 512), full(seq_len // 512, 512, 128),
            full(seq_len // Q_BLOCK, HEAD_DIM, Q_BLOCK), full(seq_len // Q_BLOCK, Q_BLOCK, 128),
            pl.BlockSpec((n_sel, n_cpad), lambda b, k, i: (0, 0)),
            pl.BlockSpec((1, 128, N_BRANCH * n_pair * 128), lambda b, k, i: (k, 0, 0)),
        ],
        out_specs=pl.BlockSpec((1, Q_BLOCK, width), lambda b, k, i: (b, i, k)),
        out_shape=jax.ShapeDtypeStruct((bsz, seq_len, NSA_WIDTH), F32),
        scratch_shapes=[pltpu.VMEM((2, n_pair * Q_BLOCK, 1), F32), pltpu.VMEM((2, n_pair * Q_BLOCK, 128), F32)],
        compiler_params=pltpu.CompilerParams(
            dimension_semantics=("arbitrary", "arbitrary", "arbitrary"), vmem_limit_bytes=V7X_VMEM_LIMIT_BYTES),
        name="nsa_attention",
    )(q, gate_pad, kct, vc, kst, vs, kwt, vw, ovl, gx)


def nsa_mixer(q, k_cmp, v_cmp, k_sel, v_sel, k_win, v_win, gate_logits, positions,
              cmp_pos_k, cmp_pos_v, w_cmp_k1, w_cmp_k2, w_cmp_v1, w_cmp_v2):
    bsz, L = q.shape[:2]
    pos = positions.astype(F32)
    n_cmp = (L - CMP_BLOCK) // CMP_STRIDE + 1
    n_cpad = L // CMP_STRIDE
    scale = HEAD_DIM ** -0.5
    cidx = np.arange(n_cmp)[:, None] * CMP_STRIDE + np.arange(CMP_BLOCK)[None, :]

    def compress(kv, pe, w1, w2):
        blk = kv[:, cidx] + pe[:, None, :]
        blk = jnp.moveaxis(blk, 3, 2).reshape(bsz, n_cmp, NSA_KV_HEADS, CMP_BLOCK * HEAD_DIM)
        return jax.nn.gelu(blk @ w1) @ w2

    kc = rope(compress(k_cmp, cmp_pos_k, w_cmp_k1, w_cmp_k2), jnp.mean(pos[:, cidx], -1))
    vc = compress(v_cmp, cmp_pos_v, w_cmp_v1, w_cmp_v2)
    q = (rope(q, pos) * scale).astype(BF16).reshape(bsz, L, NSA_WIDTH)
    heads_first = lambda t: jnp.transpose(t, (0, 2, 1, 3))
    dup = lambda t: jnp.concatenate([t, t], -1).astype(BF16)
    kt_of = lambda t: jnp.swapaxes(t, -1, -2).astype(BF16)
    padn = ((0, 0), (0, 0), (0, n_cpad - n_cmp), (0, 0))
    kct = kt_of(jnp.pad(heads_first(kc), padn))
    vcd = dup(jnp.pad(heads_first(vc), padn))
    ks = heads_first(rope(k_sel, pos)).reshape(bsz, NSA_KV_HEADS, L // 512, 512, HEAD_DIM)
    kst = kt_of(ks)
    vsd = dup(heads_first(v_sel)).reshape(bsz, NSA_KV_HEADS, L // 512, 512, 128)
    kw = heads_first(rope(k_win, pos)).reshape(bsz, NSA_KV_HEADS, L // Q_BLOCK, Q_BLOCK, HEAD_DIM)
    kwt = kt_of(kw)
    vwd = dup(heads_first(v_win)).reshape(bsz, NSA_KV_HEADS, L // Q_BLOCK, Q_BLOCK, 128)
    gl = gate_logits.reshape(bsz, L, NSA_HEADS * N_BRANCH)
    gate_pad = jnp.pad(gl, ((0, 0), (0, 0), (0, 128 - NSA_HEADS * N_BRANCH)))
    return nsa_attention(q, gate_pad, kct, vcd, kst, vsd, kwt, vwd)


def swiglu(x, wg, wu, wd):
    return (jax.nn.silu(x @ wg) * (x @ wu)) @ wd


def moe_ffn(x, w_router, router_bias, w_gate, w_up, w_down, ws_gate, ws_up, ws_down):
    bsz, L, d = x.shape
    xt = x.reshape(-1, d)
    n_tok = xt.shape[0]
    affin = jax.nn.sigmoid(xt @ w_router)
    biased = affin + router_bias
    grp = biased.reshape(n_tok, N_EXPERT_GROUPS, N_EXPERTS // N_EXPERT_GROUPS)
    grp_score = jnp.sum(lax.top_k(grp, 2)[0], -1)
    _, gsel = lax.top_k(grp_score, TOPK_GROUPS)
    gmask = jnp.sum(jax.nn.one_hot(gsel, N_EXPERT_GROUPS, dtype=F32), 1)
    gmask = jnp.repeat(gmask, N_EXPERTS // N_EXPERT_GROUPS, axis=1) > 0
    _, eidx = lax.top_k(jnp.where(gmask, biased, NEG), TOP_K)
    w = jnp.take_along_axis(affin, eidx, axis=1)
    w = w / jnp.sum(w, -1, keepdims=True) * ROUTED_SCALE
    n_asg = n_tok * TOP_K
    flat_e = eidx.reshape(-1)
    order = jnp.argsort(flat_e)
    e_sorted = flat_e[order]
    tok_sorted = (order // TOP_K).astype(jnp.int32)
    w_sorted = w.reshape(-1)[order]
    counts = jnp.zeros(N_EXPERTS, jnp.int32).at[flat_e].add(1)
    padded = (counts + MOE_BLOCK - 1) // MOE_BLOCK * MOE_BLOCK
    pad_end = jnp.cumsum(padded)
    pad_start = pad_end - padded
    start = jnp.cumsum(counts) - counts
    dest = pad_start[e_sorted] + jnp.arange(n_asg) - start[e_sorted]
    n_blk = -(-(n_asg + N_EXPERTS * (MOE_BLOCK - 1)) // MOE_BLOCK)
    buf_tok = jnp.zeros(n_blk * MOE_BLOCK, jnp.int32).at[dest].set(tok_sorted).reshape(n_blk, MOE_BLOCK)
    buf_w = jnp.zeros(n_blk * MOE_BLOCK, F32).at[dest].set(w_sorted).reshape(n_blk, MOE_BLOCK)
    blk_exp = jnp.minimum(jnp.searchsorted(pad_end, jnp.arange(n_blk) * MOE_BLOCK, side='right'), N_EXPERTS - 1)

    def step(acc, args):
        e, tk, wk = args
        yb = swiglu(xt[tk], w_gate[e], w_up[e], w_down[e])
        return acc.at[tk].add(yb * wk[:, None]), None

    routed, _ = lax.scan(step, jnp.zeros_like(xt), (blk_exp, buf_tok, buf_w))
    h = jax.nn.silu(matmul(xt, ws_gate)) * matmul(xt, ws_up)
    shared = matmul(h, ws_down)
    return (routed + shared).reshape(bsz, L, d)


def hybrid_layer(x, positions, w_in, lam_re, lam_im, log_dt, ssm_b_re, ssm_b_im, ssm_c_re, ssm_c_im, ssm_d,
                 w_glu, cmp_pos_k, cmp_pos_v, w_cmp_k1, w_cmp_k2, w_cmp_v1, w_cmp_v2, w_out, ln1_g, ln1_b,
                 w_router, router_bias, w_gate, w_up, w_down, ws_gate, ws_up, ws_down, ln2_g, ln2_b):
    bsz, L, _ = x.shape
    sizes = [SSM_WIDTH, NSA_WIDTH] + [KV_WIDTH] * 6 + [NSA_HEADS * N_BRANCH]
    offsets = tuple(int(o) for o in np.cumsum(sizes)[:-1])
    xt = x.reshape(bsz * L, D_MODEL)
    n_in = w_in.shape[1]
    n_pad = -(-n_in // 128) * 128
    w_in_p = jnp.pad(w_in, ((0, 0), (0, n_pad - n_in))).astype(BF16)
    proj = matmul(xt, w_in_p)[:, :n_in].reshape(bsz, L, n_in)
    u, q, k_c, v_c, k_s, v_s, k_w, v_w, g = jnp.split(proj, offsets, axis=-1)
    kv = lambda t: t.reshape(bsz, L, NSA_KV_HEADS, HEAD_DIM)
    y_ssm = s5_mixer(u, lam_re, lam_im, log_dt, ssm_b_re, ssm_b_im, ssm_c_re, ssm_c_im, ssm_d, w_glu)
    y_nsa = nsa_mixer(q.reshape(bsz, L, NSA_HEADS, HEAD_DIM), kv(k_c), kv(v_c), kv(k_s), kv(v_s), kv(k_w), kv(v_w),
                      g.reshape(bsz, L, NSA_HEADS, N_BRANCH), positions,
                      cmp_pos_k, cmp_pos_v, w_cmp_k1, w_cmp_k2, w_cmp_v1, w_cmp_v2)
    mix = matmul(jnp.concatenate([y_ssm, y_nsa], -1).reshape(bsz * L, D_MODEL), w_out, tn=1024)
    x = layer_norm(DEEPNORM_ALPHA * x + mix.reshape(bsz, L, D_MODEL), ln1_g, ln1_b)
    ffn = moe_ffn(x, w_router, router_bias, w_gate, w_up, w_down, ws_gate, ws_up, ws_down)
    return layer_norm(DEEPNORM_ALPHA * x + ffn, ln2_g, ln2_b)


def kernel(x, positions, w_in, lam_re, lam_im, log_dt, ssm_b_re, ssm_b_im, ssm_c_re, ssm_c_im, ssm_d, w_glu, cmp_pos_k, cmp_pos_v, w_cmp_k1, w_cmp_k2, w_cmp_v1, w_cmp_v2, w_out, ln1_g, ln1_b, w_router, router_bias, w_gate, w_up, w_down, ws_gate, ws_up, ws_down, ln2_g, ln2_b):
    params = (w_in, lam_re, lam_im, log_dt, ssm_b_re, ssm_b_im, ssm_c_re, ssm_c_im, ssm_d,
              w_glu, cmp_pos_k, cmp_pos_v, w_cmp_k1, w_cmp_k2, w_cmp_v1, w_cmp_v2, w_out, ln1_g, ln1_b,
              w_router, router_bias, w_gate, w_up, w_down, ws_gate, ws_up, ws_down, ln2_g, ln2_b)
    return hybrid_layer(x, positions, *(p[0] for p in params))
```

```python
import functools
import math

import numpy as np
import jax
import jax.numpy as jnp
from jax import lax
from jax.experimental import pallas as pl
from jax.experimental.pallas import tpu as pltpu

D_MODEL = 2048
SSM_WIDTH = 1024
SSM_CH_PER_GROUP = 16
SSM_GROUPS = 64
SSM_STATE = 64
NSA_HEADS = 16
NSA_KV_HEADS = 2
HEAD_DIM = 64
Q_PER_KV = NSA_HEADS // NSA_KV_HEADS
NSA_WIDTH = NSA_HEADS * HEAD_DIM
KV_WIDTH = NSA_KV_HEADS * HEAD_DIM
N_BRANCH = 3
CMP_BLOCK = 32
CMP_STRIDE = 16
SEL_BLOCK = 64
SEL_TOPK = 16
WINDOW = 512
Q_BLOCK = 128
ROPE_THETA = 10000.0
N_EXPERTS = 64
TOP_K = 8
N_EXPERT_GROUPS = 8
TOPK_GROUPS = 4
ROUTED_SCALE = 2.5
EXPERT_FF = 512
DEPTH = 1
DEEPNORM_ALPHA = (2.0 * DEPTH) ** 0.25
LN_EPS = 1e-5
NEG = -1e30
FORCE = 1e4
F32 = jnp.float32
BF16 = jnp.bfloat16

V7X_VMEM_LIMIT_BYTES = 56 * 1024 * 1024


def _mm_kernel(a_ref, b_ref, o_ref):
    o_ref[...] = jnp.dot(a_ref[...].astype(BF16), b_ref[...].astype(BF16), preferred_element_type=F32)


def matmul(a, b, tm=512, tn=None):
    m, k = a.shape
    _, n = b.shape
    tn = n if tn is None else tn
    return pl.pallas_call(
        _mm_kernel,
        grid=(m // tm, n // tn),
        in_specs=[pl.BlockSpec((tm, k), lambda i, j: (i, 0)), pl.BlockSpec((k, tn), lambda i, j: (0, j))],
        out_specs=pl.BlockSpec((tm, tn), lambda i, j: (i, j)),
        out_shape=jax.ShapeDtypeStruct((m, n), F32),
        compiler_params=pltpu.CompilerParams(
            dimension_semantics=("arbitrary", "arbitrary"), vmem_limit_bytes=V7X_VMEM_LIMIT_BYTES),
        name="matmul",
    )(a, b)


def layer_norm(x, g, b):
    mu = jnp.mean(x, -1, keepdims=True)
    var = jnp.mean(jnp.square(x - mu), -1, keepdims=True)
    return (x - mu) * lax.rsqrt(var + LN_EPS) * g + b


def rope(x, pos):
    half = HEAD_DIM // 2
    inv = ROPE_THETA ** (-jnp.arange(half, dtype=F32) / half)
    ang = pos[..., None, None] * inv
    cos, sin = jnp.cos(ang), jnp.sin(ang)
    x1, x2 = x[..., :half], x[..., half:]
    return jnp.concatenate([x1 * cos - x2 * sin, x2 * cos + x1 * sin], -1)


S5_CHUNK = 512
S5_SUB = S5_CHUNK // 8
S5_GROUPS_PER_BLOCK = 8
S5_STATES = S5_GROUPS_PER_BLOCK * SSM_STATE


def _cmul_add(ar, ai, xr, xi, br, bi):
    return ar * xr - ai * xi + br, ar * xi + ai * xr + bi


def _s5_kernel(u_ref, a_ref, bmat_ref, cmat_ref, d_ref, perm_ref, permt_ref, o_ref,
               xr_scr, xi_scr, pr_scr, pi_scr, carry_scr):
    c = pl.program_id(2)
    a_re = jnp.broadcast_to(a_ref[0, 0:1, :], (8, S5_STATES))
    a_im = jnp.broadcast_to(a_ref[0, 1:2, :], (8, S5_STATES))

    @pl.when(c == 0)
    def _():
        carry_scr[...] = jnp.zeros(carry_scr.shape, F32)

        def pw_body(i, pw):
            pr, pi = pw
            pr_scr[i] = pr
            pi_scr[i] = pi
            return a_re * pr - a_im * pi, a_re * pi + a_im * pr

        lax.fori_loop(0, S5_SUB, pw_body, (a_re, a_im))

    u = u_ref[0]
    perm = perm_ref[...]
    u_p = jnp.dot(perm, u.astype(BF16), preferred_element_type=F32).astype(BF16)
    xr_scr[...] = jnp.dot(u_p, bmat_ref[0, 0], preferred_element_type=F32)
    xi_scr[...] = jnp.dot(u_p, bmat_ref[0, 1], preferred_element_type=F32)

    def scan_body(i, x):
        row = pl.multiple_of(i * 8, 8)
        xr, xi = _cmul_add(a_re, a_im, x[0], x[1], xr_scr[pl.ds(row, 8), :], xi_scr[pl.ds(row, 8), :])
        xr_scr[pl.ds(row, 8), :] = xr
        xi_scr[pl.ds(row, 8), :] = xi
        return xr, xi

    zero = jnp.zeros((8, S5_STATES), F32)
    er, ei = lax.fori_loop(0, S5_SUB, scan_body, (zero, zero), unroll=4)

    ar_s = pr_scr[S5_SUB - 1][0:1]
    ai_s = pi_scr[S5_SUB - 1][0:1]
    rows_r = [carry_scr[0:1, :]]
    rows_i = [carry_scr[1:2, :]]
    for j in range(8):
        nr, ni = _cmul_add(ar_s, ai_s, rows_r[-1], rows_i[-1], er[j:j + 1], ei[j:j + 1])
        rows_r.append(nr)
        rows_i.append(ni)
    carry_scr[0:1, :] = rows_r[8]
    carry_scr[1:2, :] = rows_i[8]
    cr = jnp.concatenate(rows_r[:8], axis=0)
    ci = jnp.concatenate(rows_i[:8], axis=0)

    def fix_body(i, carry):
        row = pl.multiple_of(i * 8, 8)
        xr, xi = _cmul_add(pr_scr[i], pi_scr[i], cr, ci, xr_scr[pl.ds(row, 8), :], xi_scr[pl.ds(row, 8), :])
        xr_scr[pl.ds(row, 8), :] = xr
        xi_scr[pl.ds(row, 8), :] = xi
        return carry

    lax.fori_loop(0, S5_SUB, fix_body, 0, unroll=4)

    y_p = (jnp.dot(xr_scr[...].astype(BF16), cmat_ref[0, 0], preferred_element_type=F32)
           - jnp.dot(xi_scr[...].astype(BF16), cmat_ref[0, 1], preferred_element_type=F32))
    y_hi = y_p.astype(BF16)
    y_lo = (y_p - y_hi.astype(F32)).astype(BF16)
    perm_t = permt_ref[...]
    y = jnp.dot(perm_t, y_hi, preferred_element_type=F32) + jnp.dot(perm_t, y_lo, preferred_element_type=F32)
    o_ref[0] = jax.nn.gelu(y + d_ref[0] * u)


def s5_scan(u, lam_re, lam_im, log_dt, b_re, b_im, c_re, c_im, d_skip):
    bsz, seq_len, _ = u.shape
    nb = SSM_GROUPS // S5_GROUPS_PER_BLOCK
    lr, li = lam_re, lam_im
    dt = jnp.exp(log_dt)[:, None]
    mag = jnp.exp(lr * dt)
    ar, ai = mag * jnp.cos(li * dt), mag * jnp.sin(li * dt)
    zr, zi = ar - 1.0, ai
    den = lr * lr + li * li
    fr, fi = (zr * lr + zi * li) / den, (zi * lr - zr * li) / den
    bbr = fr[..., None] * b_re - fi[..., None] * b_im
    bbi = fr[..., None] * b_im + fi[..., None] * b_re
    eye = jnp.eye(S5_GROUPS_PER_BLOCK, dtype=F32)

    def blockdiag_b(m):
        m = jnp.swapaxes(m, 1, 2).reshape(nb, S5_GROUPS_PER_BLOCK, SSM_CH_PER_GROUP, SSM_STATE)
        return jnp.einsum('nghp,gk->nghkp', m, eye).reshape(nb, 128, S5_STATES)

    def blockdiag_c(m):
        m = jnp.swapaxes(m, 1, 2).reshape(nb, S5_GROUPS_PER_BLOCK, SSM_STATE, SSM_CH_PER_GROUP)
        return jnp.einsum('ngph,gk->ngpkh', m, eye).reshape(nb, S5_STATES, 128)

    a = jnp.stack([ar.reshape(nb, S5_STATES), ai.reshape(nb, S5_STATES)], axis=1)
    bmat = jnp.stack([blockdiag_b(bbr), blockdiag_b(bbi)], axis=1).astype(BF16)
    cmat = jnp.stack([blockdiag_c(c_re), blockdiag_c(c_im)], axis=1).astype(BF16)
    d = d_skip.reshape(nb, 1, 128)
    r = np.arange(S5_CHUNK)
    perm = np.zeros((S5_CHUNK, S5_CHUNK), np.float32)
    perm[r, (r % 8) * S5_SUB + r // 8] = 1.0
    perm = jnp.asarray(perm, BF16)
    return pl.pallas_call(
        _s5_kernel,
        grid=(bsz, nb, seq_len // S5_CHUNK),
        in_specs=[
            pl.BlockSpec((1, S5_CHUNK, 128), lambda b, g, c: (b, c, g)),
            pl.BlockSpec((1, 2, S5_STATES), lambda b, g, c: (g, 0, 0)),
            pl.BlockSpec((1, 2, 128, S5_STATES), lambda b, g, c: (g, 0, 0, 0)),
            pl.BlockSpec((1, 2, S5_STATES, 128), lambda b, g, c: (g, 0, 0, 0)),
            pl.BlockSpec((1, 1, 128), lambda b, g, c: (g, 0, 0)),
            pl.BlockSpec((S5_CHUNK, S5_CHUNK), lambda b, g, c: (0, 0)),
            pl.BlockSpec((S5_CHUNK, S5_CHUNK), lambda b, g, c: (0, 0)),
        ],
        out_specs=pl.BlockSpec((1, S5_CHUNK, 128), lambda b, g, c: (b, c, g)),
        out_shape=jax.ShapeDtypeStruct((bsz, seq_len, SSM_WIDTH), F32),
        scratch_shapes=[pltpu.VMEM((S5_CHUNK, S5_STATES), F32), pltpu.VMEM((S5_CHUNK, S5_STATES), F32),
                        pltpu.VMEM((S5_SUB, 8, S5_STATES), F32), pltpu.VMEM((S5_SUB, 8, S5_STATES), F32),
                        pltpu.VMEM((2, S5_STATES), F32)],
        compiler_params=pltpu.CompilerParams(
            dimension_semantics=("arbitrary", "arbitrary", "arbitrary"), vmem_limit_bytes=V7X_VMEM_LIMIT_BYTES),
        name="s5_scan",
    )(u, a, bmat, cmat, d, perm, perm.T)


def _glu_kernel(y_ref, w_ref, o_ref):
    y = y_ref[...]
    o_ref[...] = y * jax.nn.sigmoid(jnp.dot(y.astype(BF16), w_ref[...], preferred_element_type=F32))


def half_glu(y, w_glu, tm=512):
    m, n = y.shape
    return pl.pallas_call(
        _glu_kernel,
        grid=(m // tm,),
        in_specs=[pl.BlockSpec((tm, n), lambda i: (i, 0)), pl.BlockSpec((n, n), lambda i: (0, 0))],
        out_specs=pl.BlockSpec((tm, n), lambda i: (i, 0)),
        out_shape=jax.ShapeDtypeStruct((m, n), F32),
        compiler_params=pltpu.CompilerParams(dimension_semantics=("arbitrary",),
                                             vmem_limit_bytes=V7X_VMEM_LIMIT_BYTES),
        name="half_glu",
    )(y, w_glu.astype(BF16))


def s5_mixer(u, lam_re, lam_im, log_dt, b_re, b_im, c_re, c_im, d_skip, w_glu):
    bsz, L, _ = u.shape
    y = s5_scan(u, lam_re, lam_im, log_dt, b_re, b_im, c_re, c_im, d_skip)
    return half_glu(y.reshape(bsz * L, SSM_WIDTH), w_glu).reshape(bsz, L, SSM_WIDTH)


def _softmax_tile(s, m_old):
    m_new = jnp.maximum(m_old, jnp.max(s, axis=1, keepdims=True))
    return m_new, jnp.exp(m_old - m_new), jnp.exp(s - m_new)


def _lane_is_low(shape):
    return lax.broadcasted_iota(jnp.int32, shape, 1) < HEAD_DIM


def _pad_kt(kt, variant):
    z = jnp.zeros_like(kt)
    return jnp.concatenate([kt, z] if variant == 0 else [z, kt], axis=0)


def _pad_v(vv, variant):
    low = _lane_is_low(vv.shape)
    keep = low if variant == 0 else jnp.logical_not(low)
    return jnp.where(keep, vv, jnp.ones_like(vv))


def _finish(acc, variant):
    lane = lax.broadcasted_iota(jnp.int32, acc.shape, 1)
    lsel = lane == (HEAD_DIM if variant == 0 else 0)
    l = jnp.sum(jnp.where(lsel, acc, 0.0), axis=1, keepdims=True)
    keep = (lane < HEAD_DIM) if variant == 0 else (lane >= HEAD_DIM)
    return jnp.where(keep, acc / l, 0.0)


def _nsa_kernel(q_ref, g_ref, kct_ref, vc_ref, kst_ref, vs_ref, kwt_ref, vw_ref, ovl_ref, gx_ref, o_ref,
                m_scr, acc_scr, *, seq_len):
    n_sel = seq_len // SEL_BLOCK
    n_cpad = seq_len // CMP_STRIDE
    sel_tile = 512
    blocks_per_tile = sel_tile // SEL_BLOCK
    win_tiles = WINDOW // Q_BLOCK + 1
    n_pair = Q_PER_KV // 2
    rows = n_pair * Q_BLOCK
    i = pl.program_id(2)
    t0 = i * Q_BLOCK

    qb = q_ref[0]
    qst = jnp.concatenate([qb[:, p * 128:(p + 1) * 128] for p in range(n_pair)], axis=0)

    sig = jax.nn.sigmoid(g_ref[0])
    sig_hi = sig.astype(BF16)
    sig_lo = (sig - sig_hi.astype(F32)).astype(BF16)
    gx = gx_ref[0]
    gexp = (jnp.dot(sig_hi, gx, preferred_element_type=F32) + jnp.dot(sig_lo, gx, preferred_element_type=F32))

    def gate_of(branch):
        base = branch * n_pair * 128
        return jnp.concatenate([gexp[:, base + p * 128: base + (p + 1) * 128] for p in range(n_pair)], axis=0)

    t_row = t0 + lax.broadcasted_iota(jnp.int32, (Q_BLOCK, 1), 0)

    n_iota = lax.broadcasted_iota(jnp.int32, (Q_BLOCK, n_cpad), 1)
    cmask = (n_iota * CMP_STRIDE + (CMP_BLOCK - 1)) <= t_row
    cmask4 = jnp.concatenate([cmask] * n_pair, axis=0)
    kct = kct_ref[0, 0]
    vcd = vc_ref[0, 0]
    p_sum = jnp.zeros((Q_BLOCK, n_cpad), F32)
    out = jnp.zeros((rows, 128), F32)
    o_c = jnp.zeros((rows, 128), F32)
    for v in range(2):
        s = jnp.dot(qst, _pad_kt(kct, v), preferred_element_type=F32)
        s = jnp.where(cmask4, s, NEG)
        m = jnp.max(s, axis=1, keepdims=True)
        e = jnp.where(cmask4, jnp.exp(s - m), 0.0)
        l = jnp.sum(e, axis=1, keepdims=True)
        p = e * (1.0 / jnp.maximum(l, 1e-30))
        for pp in range(n_pair):
            p_sum = p_sum + p[pp * Q_BLOCK:(pp + 1) * Q_BLOCK]
        low = _lane_is_low((n_cpad, 128))
        vz = jnp.where(low if v == 0 else jnp.logical_not(low), vcd, jnp.zeros_like(vcd))
        o_c = o_c + jnp.dot(p.astype(BF16), vz, preferred_element_type=F32)
    out = out + gate_of(0) * o_c

    ps_hi = p_sum.astype(BF16)
    ps_lo = (p_sum - ps_hi.astype(F32)).astype(BF16)
    ovl = ovl_ref[...]
    nt = (((1,), (1,)), ((), ()))
    imp_t = (lax.dot_general(ovl, ps_hi, nt, preferred_element_type=F32)
             + lax.dot_general(ovl, ps_lo, nt, preferred_element_type=F32))
    s_iota = lax.broadcasted_iota(jnp.int32, (n_sel, Q_BLOCK), 0)
    t_lane = t0 + lax.broadcasted_iota(jnp.int32, (n_sel, Q_BLOCK), 1)
    cur = t_lane // SEL_BLOCK
    forced = (s_iota == 0) | (s_iota == cur) | (s_iota == cur - 1)
    valid = s_iota * SEL_BLOCK <= t_lane
    score = jnp.where(forced, FORCE, jnp.where(valid, imp_t, -1.0))
    s_f = s_iota.astype(F32)
    sel_t = jnp.zeros((n_sel, Q_BLOCK), F32)
    for _ in range(min(SEL_TOPK, n_sel)):
        mx = jnp.max(score, axis=0, keepdims=True)
        idx = jnp.min(jnp.where(score == mx, s_f, float(n_sel)), axis=0, keepdims=True)
        hit = s_f == idx
        sel_t = jnp.where(hit, 1.0, sel_t)
        score = jnp.where(hit, -3e38, score)
    selmask = sel_t.T.astype(BF16)

    m_scr[...] = jnp.full(m_scr.shape, NEG, F32)
    acc_scr[...] = jnp.zeros(acc_scr.shape, F32)
    n_tiles = (t0 + Q_BLOCK + sel_tile - 1) // sel_tile

    def sel_body(kt, carry):
        blk = kt * blocks_per_tile + lax.broadcasted_iota(jnp.int32, (n_sel, sel_tile), 1) // SEL_BLOCK
        expand = (lax.broadcasted_iota(jnp.int32, (n_sel, sel_tile), 0) == blk).astype(BF16)
        selexp = jnp.dot(selmask, expand, preferred_element_type=F32)
        kpos = kt * sel_tile + lax.broadcasted_iota(jnp.int32, (Q_BLOCK, sel_tile), 1)
        ok = (selexp > 0.5) & (kpos <= t_row)
        bias = jnp.where(ok, 0.0, NEG)
        bias4 = jnp.concatenate([bias] * n_pair, axis=0)
        kt_tile = kst_ref[0, 0, kt]
        v_tile = vs_ref[0, 0, kt]
        for v in range(2):
            s = jnp.dot(qst, _pad_kt(kt_tile, v), preferred_element_type=F32) + bias4
            m_new, alpha, p = _softmax_tile(s, m_scr[v])
            m_scr[v] = m_new
            acc_scr[v] = alpha * acc_scr[v] + jnp.dot(p.astype(BF16), _pad_v(v_tile, v), preferred_element_type=F32)
        return carry

    lax.fori_loop(0, n_tiles, sel_body, 0)
    out = out + gate_of(1) * (_finish(acc_scr[0], 0) + _finish(acc_scr[1], 1))

    n_kblk = seq_len // Q_BLOCK
    w0 = jnp.clip(i - (win_tiles - 1), 0, n_kblk - win_tiles)
    kw = jnp.concatenate([kwt_ref[0, 0, w0 + j] for j in range(win_tiles)], axis=1)
    vw = jnp.concatenate([vw_ref[0, 0, w0 + j] for j in range(win_tiles)], axis=0)
    kpos = w0 * Q_BLOCK + lax.broadcasted_iota(jnp.int32, (Q_BLOCK, win_tiles * Q_BLOCK), 1)
    diff = t_row - kpos
    wbias = jnp.where((diff >= 0) & (diff < WINDOW), 0.0, NEG)
    wbias4 = jnp.concatenate([wbias] * n_pair, axis=0)
    o_w = jnp.zeros((rows, 128), F32)
    for v in range(2):
        s = jnp.dot(qst, _pad_kt(kw, v), preferred_element_type=F32) + wbias4
        m = jnp.max(s, axis=1, keepdims=True)
        p = jnp.exp(s - m)
        o_w = o_w + _finish(jnp.dot(p.astype(BF16), _pad_v(vw, v), preferred_element_type=F32), v)
    out = out + gate_of(2) * o_w

    o_ref[0] = jnp.concatenate([out[p * Q_BLOCK:(p + 1) * Q_BLOCK] for p in range(n_pair)], axis=1)


def nsa_attention(q, gate_pad, kct, vc, kst, vs, kwt, vw):
    bsz, seq_len, _ = q.shape
    n_sel = seq_len // SEL_BLOCK
    n_cpad = seq_len // CMP_STRIDE
    n_cmp = (seq_len - CMP_BLOCK) // CMP_STRIDE + 1
    n_pair = Q_PER_KV // 2
    cs = np.arange(n_cpad) * CMP_STRIDE
    ce = cs + CMP_BLOCK - 1
    ss = np.arange(n_sel) * SEL_BLOCK
    se = ss + SEL_BLOCK - 1
    ovl = (cs[None, :] <= se[:, None]) & (ce[None, :] >= ss[:, None]) & (np.arange(n_cpad)[None, :] < n_cmp)
    ovl = jnp.asarray(ovl.astype(np.float32), BF16)
    gx = np.zeros((NSA_KV_HEADS, 128, N_BRANCH * n_pair * 128), np.float32)
    for k in range(NSA_KV_HEADS):
        for hl in range(Q_PER_KV):
            for br in range(N_BRANCH):
                c0 = br * n_pair * 128 + hl * HEAD_DIM
                gx[k, (k * Q_PER_KV + hl) * N_BRANCH + br, c0:c0 + HEAD_DIM] = 1.0
    gx = jnp.asarray(gx, BF16)
    width = Q_PER_KV * HEAD_DIM
    full = lambda *shape: pl.BlockSpec((1, 1) + shape, lambda b, k, i: (b, k) + (0,) * len(shape))
    return pl.pallas_call(
        functools.partial(_nsa_kernel, seq_len=seq_len),
        grid=(bsz, NSA_KV_HEADS, seq_len // Q_BLOCK),
        in_specs=[
            pl.BlockSpec((1, Q_BLOCK, width), lambda b, k, i: (b, i, k)),
            pl.BlockSpec((1, Q_BLOCK, 128), lambda b, k, i: (b, i, 0)),
            full(HEAD_DIM, n_cpad), full(n_cpad, 128),
            full(seq_len // 512, HEAD_DIM, 512), full(seq_len // 512, 512, 128),
            full(seq_len // Q_BLOCK, HEAD_DIM, Q_BLOCK), full(seq_len // Q_BLOCK, Q_BLOCK, 128),
            pl.BlockSpec((n_sel, n_cpad), lambda b, k, i: (0, 0)),
            pl.BlockSpec((1, 128, N_BRANCH * n_pair * 128), lambda b, k, i: (k, 0, 0)),
        ],
        out_specs=pl.BlockSpec((1, Q_BLOCK, width), lambda b, k, i: (b, i, k)),
        out_shape=jax.ShapeDtypeStruct((bsz, seq_len, NSA_WIDTH), F32),
        scratch_shapes=[pltpu.VMEM((2, n_pair * Q_BLOCK, 1), F32), pltpu.VMEM((2, n_pair * Q_BLOCK, 128), F32)],
        compiler_params=pltpu.CompilerParams(
            dimension_semantics=("arbitrary", "arbitrary", "arbitrary"), vmem_limit_bytes=V7X_VMEM_LIMIT_BYTES),
        name="nsa_attention",
    )(q, gate_pad, kct, vc, kst, vs, kwt, vw, ovl, gx)


def nsa_mixer(q, k_cmp, v_cmp, k_sel, v_sel, k_win, v_win, gate_logits, positions,
              cmp_pos_k, cmp_pos_v, w_cmp_k1, w_cmp_k2, w_cmp_v1, w_cmp_v2):
    bsz, L = q.shape[:2]
    pos = positions.astype(F32)
    n_cmp = (L - CMP_BLOCK) // CMP_STRIDE + 1
    n_cpad = L // CMP_STRIDE
    scale = HEAD_DIM ** -0.5
    cidx = np.arange(n_cmp)[:, None] * CMP_STRIDE + np.arange(CMP_BLOCK)[None, :]

    def compress(kv, pe, w1, w2):
        blk = kv[:, cidx] + pe[:, None, :]
        blk = jnp.moveaxis(blk, 3, 2).reshape(bsz, n_cmp, NSA_KV_HEADS, CMP_BLOCK * HEAD_DIM)
        return jax.nn.gelu(blk @ w1) @ w2

    kc = rope(compress(k_cmp, cmp_pos_k, w_cmp_k1, w_cmp_k2), jnp.mean(pos[:, cidx], -1))
    vc = compress(v_cmp, cmp_pos_v, w_cmp_v1, w_cmp_v2)
    q = (rope(q, pos) * scale).astype(BF16).reshape(bsz, L, NSA_WIDTH)
    heads_first = lambda t: jnp.transpose(t, (0, 2, 1, 3))
    dup = lambda t: jnp.concatenate([t, t], -1).astype(BF16)
    kt_of = lambda t: jnp.swapaxes(t, -1, -2).astype(BF16)
    padn = ((0, 0), (0, 0), (0, n_cpad - n_cmp), (0, 0))
    kct = kt_of(jnp.pad(heads_first(kc), padn))
    vcd = dup(jnp.pad(heads_first(vc), padn))
    ks = heads_first(rope(k_sel, pos)).reshape(bsz, NSA_KV_HEADS, L // 512, 512, HEAD_DIM)
    kst = kt_of(ks)
    vsd = dup(heads_first(v_sel)).reshape(bsz, NSA_KV_HEADS, L // 512, 512, 128)
    kw = heads_first(rope(k_win, pos)).reshape(bsz, NSA_KV_HEADS, L // Q_BLOCK, Q_BLOCK, HEAD_DIM)
    kwt = kt_of(kw)
    vwd = dup(heads_first(v_win)).reshape(bsz, NSA_KV_HEADS, L // Q_BLOCK, Q_BLOCK, 128)
    gl = gate_logits.reshape(bsz, L, NSA_HEADS * N_BRANCH)
    gate_pad = jnp.pad(gl, ((0, 0), (0, 0), (0, 128 - NSA_HEADS * N_BRANCH)))
    return nsa_attention(q, gate_pad, kct, vcd, kst, vsd, kwt, vwd)


ROUTER_TILE = 512
MOE_TILE = 1024
MOE_ROWS = 160


def _first_max_mask(x, idx_f, axis):
    mx = jnp.max(x, axis=axis, keepdims=True)
    first = jnp.min(jnp.where(x == mx, idx_f, 1e9), axis=axis, keepdims=True)
    return idx_f == first, mx


def _router_kernel(x_ref, wrt_ref, bias_ref, w_ref, sel_ref):
    per_group = N_EXPERTS // N_EXPERT_GROUPS
    tr = x_ref.shape[0]
    nt = (((1,), (1,)), ((), ()))
    logits = lax.dot_general(wrt_ref[...], x_ref[...].astype(BF16), nt, preferred_element_type=F32)
    aff = jax.nn.sigmoid(logits)
    biased = aff + bias_ref[...]
    grp = biased.reshape(N_EXPERT_GROUPS, per_group, tr)
    in_grp = lax.broadcasted_iota(jnp.int32, grp.shape, 1).astype(F32)
    hit1, m1 = _first_max_mask(grp, in_grp, 1)
    m2 = jnp.max(jnp.where(hit1, -jnp.inf, grp), axis=1, keepdims=True)
    gscore = (m1 + m2).reshape(N_EXPERT_GROUPS, tr)
    g_idx = lax.broadcasted_iota(jnp.int32, gscore.shape, 0).astype(F32)
    gsel = jnp.zeros(gscore.shape, F32)
    for _ in range(TOPK_GROUPS):
        hit, _ = _first_max_mask(gscore, g_idx, 0)
        gsel = jnp.where(hit, 1.0, gsel)
        gscore = jnp.where(hit, -jnp.inf, gscore)
    gmask = jnp.broadcast_to(gsel.reshape(N_EXPERT_GROUPS, 1, tr), grp.shape).reshape(N_EXPERTS, tr)
    cand = jnp.where(gmask > 0.5, biased, NEG)
    e_idx = lax.broadcasted_iota(jnp.int32, cand.shape, 0).astype(F32)
    sel = jnp.zeros(cand.shape, F32)
    for _ in range(TOP_K):
        hit, _ = _first_max_mask(cand, e_idx, 0)
        sel = jnp.where(hit, 1.0, sel)
        cand = jnp.where(hit, -jnp.inf, cand)
    w = jnp.where(sel > 0.5, aff, 0.0)
    w_ref[...] = w / jnp.sum(w, axis=0, keepdims=True) * ROUTED_SCALE
    sel_ref[...] = sel


def moe_router(xt, w_router, router_bias):
    n_tok = xt.shape[0]
    wrt = w_router.T.astype(BF16)
    return pl.pallas_call(
        _router_kernel,
        grid=(n_tok // ROUTER_TILE,),
        in_specs=[pl.BlockSpec((ROUTER_TILE, D_MODEL), lambda i: (i, 0)),
                  pl.BlockSpec((N_EXPERTS, D_MODEL), lambda i: (0, 0)),
                  pl.BlockSpec((N_EXPERTS, 1), lambda i: (0, 0))],
        out_specs=[pl.BlockSpec((N_EXPERTS, ROUTER_TILE), lambda i: (0, i)),
                   pl.BlockSpec((N_EXPERTS, ROUTER_TILE), lambda i: (0, i))],
        out_shape=[jax.ShapeDtypeStruct((N_EXPERTS, n_tok), F32), jax.ShapeDtypeStruct((N_EXPERTS, n_tok), F32)],
        compiler_params=pltpu.CompilerParams(dimension_semantics=("arbitrary",),
                                             vmem_limit_bytes=V7X_VMEM_LIMIT_BYTES),
        name="moe_router",
    )(xt, wrt, router_bias.reshape(N_EXPERTS, 1))


def _moe_kernel(cnt_ref, x_ref, sel_ref, w_ref, init_ref, wg_ref, wu_ref, wd_ref, o_ref, rank_scr):
    i = pl.program_id(0)
    e = pl.program_id(1)
    tm = x_ref.shape[0]

    @pl.when(e == 0)
    def _():
        o_ref[...] = init_ref[...]
        before = (lax.broadcasted_iota(jnp.int32, (tm, tm), 0) < lax.broadcasted_iota(jnp.int32, (tm, tm), 1))
        rank_scr[...] = jnp.dot(sel_ref[...].astype(BF16), jnp.where(before, 1.0, 0.0).astype(BF16),
                                preferred_element_type=F32)

    count = cnt_ref[i * N_EXPERTS + e]
    sel_e = sel_ref[pl.ds(e, 1), :]
    rank_e = rank_scr[pl.ds(e, 1), :]
    w_e = w_ref[pl.ds(e, 1), :]

    def chunk_body(c, carry):
        row = (c * MOE_ROWS + lax.broadcasted_iota(jnp.int32, (MOE_ROWS, tm), 0)).astype(F32)
        hit = (rank_e == row) & (sel_e > 0.5)
        gather = jnp.where(hit, 1.0, 0.0).astype(BF16)
        scatter = jnp.where(hit, w_e, 0.0).astype(BF16)
        xe = jnp.dot(gather, x_ref[...], preferred_element_type=F32).astype(BF16)
        g = jnp.dot(xe, wg_ref[0], preferred_element_type=F32)
        u = jnp.dot(xe, wu_ref[0], preferred_element_type=F32)
        h = (jax.nn.silu(g) * u).astype(BF16)
        y = jnp.dot(h, wd_ref[0], preferred_element_type=F32).astype(BF16)
        tn = (((0,), (0,)), ((), ()))
        o_ref[...] += lax.dot_general(scatter, y, tn, preferred_element_type=F32)
        return carry

    lax.fori_loop(0, (count + MOE_ROWS - 1) // MOE_ROWS, chunk_body, 0)


def moe_routed(x_bf16, sel_t, w_t, init, w_gate, w_up, w_down):
    n_tok = x_bf16.shape[0]
    n_tiles = n_tok // MOE_TILE
    cnt = jnp.sum(sel_t.reshape(N_EXPERTS, n_tiles, MOE_TILE), axis=-1).T.astype(jnp.int32).reshape(-1)
    grid_spec = pltpu.PrefetchScalarGridSpec(
        num_scalar_prefetch=1,
        grid=(n_tiles, N_EXPERTS),
        in_specs=[
            pl.BlockSpec((MOE_TILE, D_MODEL), lambda i, e, cnt: (i, 0)),
            pl.BlockSpec((N_EXPERTS, MOE_TILE), lambda i, e, cnt: (0, i)),
            pl.BlockSpec((N_EXPERTS, MOE_TILE), lambda i, e, cnt: (0, i)),
            pl.BlockSpec((MOE_TILE, D_MODEL), lambda i, e, cnt: (i, 0)),
            pl.BlockSpec((1, D_MODEL, EXPERT_FF), lambda i, e, cnt: (e, 0, 0)),
            pl.BlockSpec((1, D_MODEL, EXPERT_FF), lambda i, e, cnt: (e, 0, 0)),
            pl.BlockSpec((1, EXPERT_FF, D_MODEL), lambda i, e, cnt: (e, 0, 0)),
        ],
        out_specs=pl.BlockSpec((MOE_TILE, D_MODEL), lambda i, e, cnt: (i, 0)),
        scratch_shapes=[pltpu.VMEM((N_EXPERTS, MOE_TILE), F32)],
    )
    return pl.pallas_call(
        _moe_kernel,
        grid_spec=grid_spec,
        out_shape=jax.ShapeDtypeStruct((n_tok, D_MODEL), F32),
        compiler_params=pltpu.CompilerParams(dimension_semantics=("arbitrary", "arbitrary"),
                                             vmem_limit_bytes=V7X_VMEM_LIMIT_BYTES),
        name="moe_routed",
    )(cnt, x_bf16, sel_t, w_t, init, w_gate, w_up, w_down)


def _shared_ffn_kernel(x_ref, wg_ref, wu_ref, wd_ref, o_ref):
    x = x_ref[...]
    h = jax.nn.silu(jnp.dot(x, wg_ref[...], preferred_element_type=F32)) * jnp.dot(
        x, wu_ref[...], preferred_element_type=F32)
    o_ref[...] = jnp.dot(h.astype(BF16), wd_ref[...], preferred_element_type=F32)


def shared_ffn(x_bf16, wg, wu, wd, tm=512):
    n_tok, d = x_bf16.shape
    ff = wg.shape[1]
    return pl.pallas_call(
        _shared_ffn_kernel,
        grid=(n_tok // tm,),
        in_specs=[pl.BlockSpec((tm, d), lambda i: (i, 0)), pl.BlockSpec((d, ff), lambda i: (0, 0)),
                  pl.BlockSpec((d, ff), lambda i: (0, 0)), pl.BlockSpec((ff, d), lambda i: (0, 0))],
        out_specs=pl.BlockSpec((tm, d), lambda i: (i, 0)),
        out_shape=jax.ShapeDtypeStruct((n_tok, d), F32),
        compiler_params=pltpu.CompilerParams(dimension_semantics=("arbitrary",),
                                             vmem_limit_bytes=V7X_VMEM_LIMIT_BYTES),
        name="shared_ffn",
    )(x_bf16, wg.astype(BF16), wu.astype(BF16), wd.astype(BF16))


def moe_ffn(x, w_router, router_bias, w_gate, w_up, w_down, ws_gate, ws_up, ws_down):
    bsz, L, d = x.shape
    xt = x.reshape(-1, d)
    w_t, sel_t = moe_router(xt, w_router, router_bias)
    xb = xt.astype(BF16)
    shared = shared_ffn(xb, ws_gate, ws_up, ws_down)
    out = moe_routed(xb, sel_t, w_t, shared, w_gate.astype(BF16), w_up.astype(BF16), w_down.astype(BF16))
    return out.reshape(bsz, L, d)


def hybrid_layer(x, positions, w_in, lam_re, lam_im, log_dt, ssm_b_re, ssm_b_im, ssm_c_re, ssm_c_im, ssm_d,
                 w_glu, cmp_pos_k, cmp_pos_v, w_cmp_k1, w_cmp_k2, w_cmp_v1, w_cmp_v2, w_out, ln1_g, ln1_b,
                 w_router, router_bias, w_gate, w_up, w_down, ws_gate, ws_up, ws_down, ln2_g, ln2_b):
    bsz, L, _ = x.shape
    sizes = [SSM_WIDTH, NSA_WIDTH] + [KV_WIDTH] * 6 + [NSA_HEADS * N_BRANCH]
    offsets = tuple(int(o) for o in np.cumsum(sizes)[:-1])
    xt = x.reshape(bsz * L, D_MODEL)
    n_in = w_in.shape[1]
    n_pad = -(-n_in // 128) * 128
    w_in_p = jnp.pad(w_in, ((0, 0), (0, n_pad - n_in))).astype(BF16)
    proj = matmul(xt, w_in_p)[:, :n_in].reshape(bsz, L, n_in)
    u, q, k_c, v_c, k_s, v_s, k_w, v_w, g = jnp.split(proj, offsets, axis=-1)
    kv = lambda t: t.reshape(bsz, L, NSA_KV_HEADS, HEAD_DIM)
    y_ssm = s5_mixer(u, lam_re, lam_im, log_dt, ssm_b_re, ssm_b_im, ssm_c_re, ssm_c_im, ssm_d, w_glu)
    y_nsa = nsa_mixer(q.reshape(bsz, L, NSA_HEADS, HEAD_DIM), kv(k_c), kv(v_c), kv(k_s), kv(v_s), kv(k_w), kv(v_w),
                      g.reshape(bsz, L, NSA_HEADS, N_BRANCH), positions,
                      cmp_pos_k, cmp_pos_v, w_cmp_k1, w_cmp_k2, w_cmp_v1, w_cmp_v2)
    mix = matmul(jnp.concatenate([y_ssm, y_nsa], -1).reshape(bsz * L, D_MODEL), w_out, tn=1024)
    x = layer_norm(DEEPNORM_ALPHA * x + mix.reshape(bsz, L, D_MODEL), ln1_g, ln1_b)
    ffn = moe_ffn(x, w_router, router_bias, w_gate, w_up, w_down, ws_gate, ws_up, ws_down)
    return layer_norm(DEEPNORM_ALPHA * x + ffn, ln2_g, ln2_b)


def kernel(x, positions, w_in, lam_re, lam_im, log_dt, ssm_b_re, ssm_b_im, ssm_c_re, ssm_c_im, ssm_d, w_glu, cmp_pos_k, cmp_pos_v, w_cmp_k1, w_cmp_k2, w_cmp_v1, w_cmp_v2, w_out, ln1_g, ln1_b, w_router, router_bias, w_gate, w_up, w_down, ws_gate, ws_up, ws_down, ln2_g, ln2_b):
    params = (w_in, lam_re, lam_im, log_dt, ssm_b_re, ssm_b_im, ssm_c_re, ssm_c_im, ssm_d,
              w_glu, cmp_pos_k, cmp_pos_v, w_cmp_k1, w_cmp_k2, w_cmp_v1, w_cmp_v2, w_out, ln1_g, ln1_b,
              w_router, router_bias, w_gate, w_up, w_down, ws_gate, ws_up, ws_down, ln2_g, ln2_b)
    return hybrid_layer(x, positions, *(p[0] for p in params))
```

```python
import functools
import math

import numpy as np
import jax
import jax.numpy as jnp
from jax import lax
from jax.experimental import pallas as pl
from jax.experimental.pallas import tpu as pltpu

D_MODEL = 2048
SSM_WIDTH = 1024
SSM_CH_PER_GROUP = 16
SSM_GROUPS = 64
SSM_STATE = 64
NSA_HEADS = 16
NSA_KV_HEADS = 2
HEAD_DIM = 64
Q_PER_KV = NSA_HEADS // NSA_KV_HEADS
NSA_WIDTH = NSA_HEADS * HEAD_DIM
KV_WIDTH = NSA_KV_HEADS * HEAD_DIM
N_BRANCH = 3
CMP_BLOCK = 32
CMP_STRIDE = 16
SEL_BLOCK = 64
SEL_TOPK = 16
WINDOW = 512
Q_BLOCK = 128
ROPE_THETA = 10000.0
N_EXPERTS = 64
TOP_K = 8
N_EXPERT_GROUPS = 8
TOPK_GROUPS = 4
ROUTED_SCALE = 2.5
EXPERT_FF = 512
DEPTH = 1
DEEPNORM_ALPHA = (2.0 * DEPTH) ** 0.25
LN_EPS = 1e-5
NEG = -1e30
FORCE = 1e4
F32 = jnp.float32
BF16 = jnp.bfloat16

V7X_VMEM_LIMIT_BYTES = 56 * 1024 * 1024


def _mm_kernel(a_ref, b_ref, o_ref):
    o_ref[...] = jnp.dot(a_ref[...].astype(BF16), b_ref[...].astype(BF16), preferred_element_type=F32)


def matmul(a, b, tm=512, tn=None):
    m, k = a.shape
    _, n = b.shape
    tn = n if tn is None else tn
    return pl.pallas_call(
        _mm_kernel,
        grid=(m // tm, n // tn),
        in_specs=[pl.BlockSpec((tm, k), lambda i, j: (i, 0)), pl.BlockSpec((k, tn), lambda i, j: (0, j))],
        out_specs=pl.BlockSpec((tm, tn), lambda i, j: (i, j)),
        out_shape=jax.ShapeDtypeStruct((m, n), F32),
        compiler_params=pltpu.CompilerParams(
            dimension_semantics=("arbitrary", "arbitrary"), vmem_limit_bytes=V7X_VMEM_LIMIT_BYTES),
        name="matmul",
    )(a, b)


def layer_norm(x, g, b):
    mu = jnp.mean(x, -1, keepdims=True)
    var = jnp.mean(jnp.square(x - mu), -1, keepdims=True)
    return (x - mu) * lax.rsqrt(var + LN_EPS) * g + b


def rope(x, pos):
    half = HEAD_DIM // 2
    inv = ROPE_THETA ** (-jnp.arange(half, dtype=F32) / half)
    ang = pos[..., None, None] * inv
    cos, sin = jnp.cos(ang), jnp.sin(ang)
    x1, x2 = x[..., :half], x[..., half:]
    return jnp.concatenate([x1 * cos - x2 * sin, x2 * cos + x1 * sin], -1)


S5_CHUNK = 512
S5_SUB = S5_CHUNK // 8
S5_GROUPS_PER_BLOCK = 8
S5_STATES = S5_GROUPS_PER_BLOCK * SSM_STATE


def _cmul_add(ar, ai, xr, xi, br, bi):
    return ar * xr - ai * xi + br, ar * xi + ai * xr + bi


def _s5_kernel(u_ref, a_ref, bmat_ref, cmat_ref, d_ref, perm_ref, permt_ref, o_ref,
               xr_scr, xi_scr, pr_scr, pi_scr, carry_scr):
    c = pl.program_id(2)
    a_re = jnp.broadcast_to(a_ref[0, 0:1, :], (8, S5_STATES))
    a_im = jnp.broadcast_to(a_ref[0, 1:2, :], (8, S5_STATES))

    @pl.when(c == 0)
    def _():
        carry_scr[...] = jnp.zeros(carry_scr.shape, F32)

        def pw_body(i, pw):
            pr, pi = pw
            pr_scr[i] = pr
            pi_scr[i] = pi
            return a_re * pr - a_im * pi, a_re * pi + a_im * pr

        lax.fori_loop(0, S5_SUB, pw_body, (a_re, a_im))

    u = u_ref[0]
    perm = perm_ref[...]
    u_p = jnp.dot(perm, u.astype(BF16), preferred_element_type=F32).astype(BF16)
    xr_scr[...] = jnp.dot(u_p, bmat_ref[0, 0], preferred_element_type=F32)
    xi_scr[...] = jnp.dot(u_p, bmat_ref[0, 1], preferred_element_type=F32)

    def scan_body(i, x):
        row = pl.multiple_of(i * 8, 8)
        xr, xi = _cmul_add(a_re, a_im, x[0], x[1], xr_scr[pl.ds(row, 8), :], xi_scr[pl.ds(row, 8), :])
        xr_scr[pl.ds(row, 8), :] = xr
        xi_scr[pl.ds(row, 8), :] = xi
        return xr, xi

    zero = jnp.zeros((8, S5_STATES), F32)
    er, ei = lax.fori_loop(0, S5_SUB, scan_body, (zero, zero), unroll=4)

    ar_s = pr_scr[S5_SUB - 1][0:1]
    ai_s = pi_scr[S5_SUB - 1][0:1]
    rows_r = [carry_scr[0:1, :]]
    rows_i = [carry_scr[1:2, :]]
    for j in range(8):
        nr, ni = _cmul_add(ar_s, ai_s, rows_r[-1], rows_i[-1], er[j:j + 1], ei[j:j + 1])
        rows_r.append(nr)
        rows_i.append(ni)
    carry_scr[0:1, :] = rows_r[8]
    carry_scr[1:2, :] = rows_i[8]
    cr = jnp.concatenate(rows_r[:8], axis=0)
    ci = jnp.concatenate(rows_i[:8], axis=0)

    def fix_body(i, carry):
        row = pl.multiple_of(i * 8, 8)
        xr, xi = _cmul_add(pr_scr[i], pi_scr[i], cr, ci, xr_scr[pl.ds(row, 8), :], xi_scr[pl.ds(row, 8), :])
        xr_scr[pl.ds(row, 8), :] = xr
        xi_scr[pl.ds(row, 8), :] = xi
        return carry

    lax.fori_loop(0, S5_SUB, fix_body, 0, unroll=4)

    y_p = (jnp.dot(xr_scr[...].astype(BF16), cmat_ref[0, 0], preferred_element_type=F32)
           - jnp.dot(xi_scr[...].astype(BF16), cmat_ref[0, 1], preferred_element_type=F32))
    y_hi = y_p.astype(BF16)
    y_lo = (y_p - y_hi.astype(F32)).astype(BF16)
    perm_t = permt_ref[...]
    y = jnp.dot(perm_t, y_hi, preferred_element_type=F32) + jnp.dot(perm_t, y_lo, preferred_element_type=F32)
    o_ref[0] = jax.nn.gelu(y + d_ref[0] * u)


def s5_scan(u, lam_re, lam_im, log_dt, b_re, b_im, c_re, c_im, d_skip):
    bsz, seq_len, _ = u.shape
    nb = SSM_GROUPS // S5_GROUPS_PER_BLOCK
    lr, li = lam_re, lam_im
    dt = jnp.exp(log_dt)[:, None]
    mag = jnp.exp(lr * dt)
    ar, ai = mag * jnp.cos(li * dt), mag * jnp.sin(li * dt)
    zr, zi = ar - 1.0, ai
    den = lr * lr + li * li
    fr, fi = (zr * lr + zi * li) / den, (zi * lr - zr * li) / den
    bbr = fr[..., None] * b_re - fi[..., None] * b_im
    bbi = fr[..., None] * b_im + fi[..., None] * b_re
    eye = jnp.eye(S5_GROUPS_PER_BLOCK, dtype=F32)

    def blockdiag_b(m):
        m = jnp.swapaxes(m, 1, 2).reshape(nb, S5_GROUPS_PER_BLOCK, SSM_CH_PER_GROUP, SSM_STATE)
        return jnp.einsum('nghp,gk->nghkp', m, eye).reshape(nb, 128, S5_STATES)

    def blockdiag_c(m):
        m = jnp.swapaxes(m, 1, 2).reshape(nb, S5_GROUPS_PER_BLOCK, SSM_STATE, SSM_CH_PER_GROUP)
        return jnp.einsum('ngph,gk->ngpkh', m, eye).reshape(nb, S5_STATES, 128)

    a = jnp.stack([ar.reshape(nb, S5_STATES), ai.reshape(nb, S5_STATES)], axis=1)
    bmat = jnp.stack([blockdiag_b(bbr), blockdiag_b(bbi)], axis=1).astype(BF16)
    cmat = jnp.stack([blockdiag_c(c_re), blockdiag_c(c_im)], axis=1).astype(BF16)
    d = d_skip.reshape(nb, 1, 128)
    r = np.arange(S5_CHUNK)
    perm = np.zeros((S5_CHUNK, S5_CHUNK), np.float32)
    perm[r, (r % 8) * S5_SUB + r // 8] = 1.0
    perm = jnp.asarray(perm, BF16)
    return pl.pallas_call(
        _s5_kernel,
        grid=(bsz, nb, seq_len // S5_CHUNK),
        in_specs=[
            pl.BlockSpec((1, S5_CHUNK, 128), lambda b, g, c: (b, c, g)),
            pl.BlockSpec((1, 2, S5_STATES), lambda b, g, c: (g, 0, 0)),
            pl.BlockSpec((1, 2, 128, S5_STATES), lambda b, g, c: (g, 0, 0, 0)),
            pl.BlockSpec((1, 2, S5_STATES, 128), lambda b, g, c: (g, 0, 0, 0)),
            pl.BlockSpec((1, 1, 128), lambda b, g, c: (g, 0, 0)),
            pl.BlockSpec((S5_CHUNK, S5_CHUNK), lambda b, g, c: (0, 0)),
            pl.BlockSpec((S5_CHUNK, S5_CHUNK), lambda b, g, c: (0, 0)),
        ],
        out_specs=pl.BlockSpec((1, S5_CHUNK, 128), lambda b, g, c: (b, c, g)),
        out_shape=jax.ShapeDtypeStruct((bsz, seq_len, SSM_WIDTH), F32),
        scratch_shapes=[pltpu.VMEM((S5_CHUNK, S5_STATES), F32), pltpu.VMEM((S5_CHUNK, S5_STATES), F32),
                        pltpu.VMEM((S5_SUB, 8, S5_STATES), F32), pltpu.VMEM((S5_SUB, 8, S5_STATES), F32),
                        pltpu.VMEM((2, S5_STATES), F32)],
        compiler_params=pltpu.CompilerParams(
            dimension_semantics=("arbitrary", "arbitrary", "arbitrary"), vmem_limit_bytes=V7X_VMEM_LIMIT_BYTES),
        name="s5_scan",
    )(u, a, bmat, cmat, d, perm, perm.T)


def _glu_kernel(y_ref, w_ref, o_ref):
    y = y_ref[...]
    o_ref[...] = y * jax.nn.sigmoid(jnp.dot(y.astype(BF16), w_ref[...], preferred_element_type=F32))


def half_glu(y, w_glu, tm=512):
    m, n = y.shape
    return pl.pallas_call(
        _glu_kernel,
        grid=(m // tm,),
        in_specs=[pl.BlockSpec((tm, n), lambda i: (i, 0)), pl.BlockSpec((n, n), lambda i: (0, 0))],
        out_specs=pl.BlockSpec((tm, n), lambda i: (i, 0)),
        out_shape=jax.ShapeDtypeStruct((m, n), F32),
        compiler_params=pltpu.CompilerParams(dimension_semantics=("arbitrary",),
                                             vmem_limit_bytes=V7X_VMEM_LIMIT_BYTES),
        name="half_glu",
    )(y, w_glu.astype(BF16))


def s5_mixer(u, lam_re, lam_im, log_dt, b_re, b_im, c_re, c_im, d_skip, w_glu):
    bsz, L, _ = u.shape
    y = s5_scan(u, lam_re, lam_im, log_dt, b_re, b_im, c_re, c_im, d_skip)
    return half_glu(y.reshape(bsz * L, SSM_WIDTH), w_glu).reshape(bsz, L, SSM_WIDTH)


def _softmax_tile(s, m_old):
    m_new = jnp.maximum(m_old, jnp.max(s, axis=1, keepdims=True))
    m_wide = jnp.concatenate([m_new] * (s.shape[1] // 128), axis=1)
    return m_new, jnp.exp2(m_old - m_new), jnp.exp2(s - m_wide)


def _lane_is_low(shape):
    return lax.broadcasted_iota(jnp.int32, shape, 1) < HEAD_DIM


def _pad_kt(kt, variant):
    z = jnp.zeros_like(kt)
    return jnp.concatenate([kt, z] if variant == 0 else [z, kt], axis=0)


def _pad_v(vv, variant):
    low = _lane_is_low(vv.shape)
    keep = low if variant == 0 else jnp.logical_not(low)
    return jnp.where(keep, vv, jnp.ones_like(vv))


def _finish(acc, variant):
    lane = lax.broadcasted_iota(jnp.int32, acc.shape, 1)
    lsel = lane == (HEAD_DIM if variant == 0 else 0)
    l = jnp.sum(jnp.where(lsel, acc, 0.0), axis=1, keepdims=True)
    keep = (lane < HEAD_DIM) if variant == 0 else (lane >= HEAD_DIM)
    return jnp.where(keep, acc / l, 0.0)


def _nsa_kernel(q_ref, g_ref, kct_ref, vc_ref, kst_ref, vs_ref, kwt_ref, vw_ref, ovl_ref, gx_ref, o_ref,
                m_scr, acc_scr, s_scr_a, s_scr_b, *, seq_len):
    s_slots = (s_scr_a, s_scr_b)
    n_sel = seq_len // SEL_BLOCK
    n_cpad = seq_len // CMP_STRIDE
    sel_tile = 512
    blocks_per_tile = sel_tile // SEL_BLOCK
    win_tiles = WINDOW // Q_BLOCK + 1
    n_pair = Q_PER_KV // 2
    rows = n_pair * Q_BLOCK
    i = pl.program_id(2)
    t0 = i * Q_BLOCK

    qb = q_ref[0]
    qst = jnp.concatenate([qb[:, p * 128:(p + 1) * 128] for p in range(n_pair)], axis=0)

    sig = jax.nn.sigmoid(g_ref[0])
    sig_hi = sig.astype(BF16)
    sig_lo = (sig - sig_hi.astype(F32)).astype(BF16)
    gx = gx_ref[0]
    gexp = (jnp.dot(sig_hi, gx, preferred_element_type=F32) + jnp.dot(sig_lo, gx, preferred_element_type=F32))

    def gate_of(branch):
        base = branch * n_pair * 128
        return jnp.concatenate([gexp[:, base + p * 128: base + (p + 1) * 128] for p in range(n_pair)], axis=0)

    t_row = t0 + lax.broadcasted_iota(jnp.int32, (Q_BLOCK, 1), 0)

    n_iota = lax.broadcasted_iota(jnp.int32, (Q_BLOCK, n_cpad), 1)
    cmask = (n_iota * CMP_STRIDE + (CMP_BLOCK - 1)) <= t_row
    cmask4 = jnp.concatenate([cmask] * n_pair, axis=0)
    kct = kct_ref[0, 0]
    vcd = vc_ref[0, 0]
    p_sum = jnp.zeros((Q_BLOCK, n_cpad), F32)
    out = jnp.zeros((rows, 128), F32)
    o_c = jnp.zeros((rows, 128), F32)
    for v in range(2):
        s = jnp.dot(qst, _pad_kt(kct, v), preferred_element_type=F32)
        s = jnp.where(cmask4, s, NEG)
        m = jnp.max(s, axis=1, keepdims=True)
        e = jnp.where(cmask4, jnp.exp2(s - m), 0.0)
        l = jnp.sum(e, axis=1, keepdims=True)
        p = e * (1.0 / jnp.maximum(l, 1e-30))
        for pp in range(n_pair):
            p_sum = p_sum + p[pp * Q_BLOCK:(pp + 1) * Q_BLOCK]
        low = _lane_is_low((n_cpad, 128))
        vz = jnp.where(low if v == 0 else jnp.logical_not(low), vcd, jnp.zeros_like(vcd))
        o_c = o_c + jnp.dot(p.astype(BF16), vz, preferred_element_type=F32)
    out = out + gate_of(0) * o_c

    n_kblk = seq_len // Q_BLOCK
    w0 = jnp.clip(i - (win_tiles - 1), 0, n_kblk - win_tiles)
    kw = jnp.concatenate([kwt_ref[0, 0, w0 + j] for j in range(win_tiles)], axis=1)
    vw = jnp.concatenate([vw_ref[0, 0, w0 + j] for j in range(win_tiles)], axis=0)
    kpos_w = w0 * Q_BLOCK + lax.broadcasted_iota(jnp.int32, (Q_BLOCK, win_tiles * Q_BLOCK), 1)
    diff = t_row - kpos_w
    wbias = jnp.where((diff >= 0) & (diff < WINDOW), 0.0, NEG)
    wbias4 = jnp.concatenate([wbias] * n_pair, axis=0)
    o_w = jnp.zeros((rows, 128), F32)
    for v in range(2):
        s = jnp.dot(qst, _pad_kt(kw, v), preferred_element_type=F32) + wbias4
        m = jnp.max(s, axis=1, keepdims=True)
        p = jnp.exp2(s - m)
        o_w = o_w + _finish(jnp.dot(p.astype(BF16), _pad_v(vw, v), preferred_element_type=F32), v)
    out = out + gate_of(2) * o_w

    ps_hi = p_sum.astype(BF16)
    ps_lo = (p_sum - ps_hi.astype(F32)).astype(BF16)
    ovl = ovl_ref[...]
    nt = (((1,), (1,)), ((), ()))
    imp_t = (lax.dot_general(ovl, ps_hi, nt, preferred_element_type=F32)
             + lax.dot_general(ovl, ps_lo, nt, preferred_element_type=F32))
    s_iota = lax.broadcasted_iota(jnp.int32, (n_sel, Q_BLOCK), 0)
    t_lane = t0 + lax.broadcasted_iota(jnp.int32, (n_sel, Q_BLOCK), 1)
    cur = t_lane // SEL_BLOCK
    forced = (s_iota == 0) | (s_iota == cur) | (s_iota == cur - 1)
    valid = s_iota * SEL_BLOCK <= t_lane
    score = jnp.where(forced, FORCE, jnp.where(valid, imp_t, -1.0))
    s_f = s_iota.astype(F32)
    sel_t = jnp.zeros((n_sel, Q_BLOCK), F32)
    for _ in range(min(SEL_TOPK, n_sel)):
        mx = jnp.max(score, axis=0, keepdims=True)
        idx = jnp.min(jnp.where(score == mx, s_f, float(n_sel)), axis=0, keepdims=True)
        hit = s_f == idx
        sel_t = jnp.where(hit, 1.0, sel_t)
        score = jnp.where(hit, -3e38, score)
    selmask = sel_t.T.astype(BF16)

    m_scr[...] = jnp.full(m_scr.shape, NEG, F32)
    acc_scr[...] = jnp.zeros(acc_scr.shape, F32)
    n_tiles = (t0 + Q_BLOCK + sel_tile - 1) // sel_tile

    last_tile = seq_len // sel_tile - 1

    def scores_into(slot, kt):
        kt_tile = kst_ref[0, 0, jnp.minimum(kt, last_tile)]
        for v in range(2):
            s_slots[slot][v] = jnp.dot(qst, _pad_kt(kt_tile, v), preferred_element_type=F32)

    def bias_of(kt):
        blk = kt * blocks_per_tile + lax.broadcasted_iota(jnp.int32, (n_sel, sel_tile), 1) // SEL_BLOCK
        expand = (lax.broadcasted_iota(jnp.int32, (n_sel, sel_tile), 0) == blk).astype(BF16)
        selexp = jnp.dot(selmask, expand, preferred_element_type=F32)
        kpos = kt * sel_tile + lax.broadcasted_iota(jnp.int32, (Q_BLOCK, sel_tile), 1)
        bias = jnp.where((selexp > 0.5) & (kpos <= t_row), 0.0, NEG)
        return jnp.concatenate([bias] * n_pair, axis=0)

    def attend_from(slot, kt, bias4):
        v_tile = vs_ref[0, 0, jnp.minimum(kt, last_tile)]
        for v in range(2):
            m_new, alpha, p = _softmax_tile(s_slots[slot][v] + bias4, m_scr[v])
            m_scr[v] = m_new
            acc_scr[v] = alpha * acc_scr[v] + jnp.dot(p.astype(BF16), _pad_v(v_tile, v), preferred_element_type=F32)

    scores_into(0, 0)

    def sel_body(j, carry):
        kt = 2 * j
        bias_a = bias_of(kt)
        scores_into(1, kt + 1)
        attend_from(0, kt, bias_a)
        bias_b = bias_of(kt + 1)
        scores_into(0, kt + 2)
        attend_from(1, kt + 1, bias_b)
        return carry

    lax.fori_loop(0, (n_tiles + 1) // 2, sel_body, 0)
    out = out + gate_of(1) * (_finish(acc_scr[0], 0) + _finish(acc_scr[1], 1))

    o_ref[0] = jnp.concatenate([out[p * Q_BLOCK:(p + 1) * Q_BLOCK] for p in range(n_pair)], axis=1)


def nsa_attention(q, gate_pad, kct, vc, kst, vs, kwt, vw):
    bsz, seq_len, _ = q.shape
    n_sel = seq_len // SEL_BLOCK
    n_cpad = seq_len // CMP_STRIDE
    n_cmp = (seq_len - CMP_BLOCK) // CMP_STRIDE + 1
    n_pair = Q_PER_KV // 2
    cs = np.arange(n_cpad) * CMP_STRIDE
    ce = cs + CMP_BLOCK - 1
    ss = np.arange(n_sel) * SEL_BLOCK
    se = ss + SEL_BLOCK - 1
    ovl = (cs[None, :] <= se[:, None]) & (ce[None, :] >= ss[:, None]) & (np.arange(n_cpad)[None, :] < n_cmp)
    ovl = jnp.asarray(ovl.astype(np.float32), BF16)
    gx = np.zeros((NSA_KV_HEADS, 128, N_BRANCH * n_pair * 128), np.float32)
    for k in range(NSA_KV_HEADS):
        for hl in range(Q_PER_KV):
            for br in range(N_BRANCH):
                c0 = br * n_pair * 128 + hl * HEAD_DIM
                gx[k, (k * Q_PER_KV + hl) * N_BRANCH + br, c0:c0 + HEAD_DIM] = 1.0
    gx = jnp.asarray(gx, BF16)
    width = Q_PER_KV * HEAD_DIM
    full = lambda *shape: pl.BlockSpec((1, 1) + shape, lambda b, k, i: (b, k) + (0,) * len(shape))
    return pl.pallas_call(
        functools.partial(_nsa_kernel, seq_len=seq_len),
        grid=(bsz, NSA_KV_HEADS, seq_len // Q_BLOCK),
        in_specs=[
            pl.BlockSpec((1, Q_BLOCK, width), lambda b, k, i: (b, i, k)),
            pl.BlockSpec((1, Q_BLOCK, 128), lambda b, k, i: (b, i, 0)),
            full(HEAD_DIM, n_cpad), full(n_cpad, 128),
            full(seq_len // 512, HEAD_DIM, 512), full(seq_len // 512, 512, 128),
            full(seq_len // Q_BLOCK, HEAD_DIM, Q_BLOCK), full(seq_len // Q_BLOCK, Q_BLOCK, 128),
            pl.BlockSpec((n_sel, n_cpad), lambda b, k, i: (0, 0)),
            pl.BlockSpec((1, 128, N_BRANCH * n_pair * 128), lambda b, k, i: (k, 0, 0)),
        ],
        out_specs=pl.BlockSpec((1, Q_BLOCK, width), lambda b, k, i: (b, i, k)),
        out_shape=jax.ShapeDtypeStruct((bsz, seq_len, NSA_WIDTH), F32),
        scratch_shapes=[pltpu.VMEM((2, n_pair * Q_BLOCK, 128), F32), pltpu.VMEM((2, n_pair * Q_BLOCK, 128), F32),
                        pltpu.VMEM((2, n_pair * Q_BLOCK, 512), F32), pltpu.VMEM((2, n_pair * Q_BLOCK, 512), F32)],
        compiler_params=pltpu.CompilerParams(
            dimension_semantics=("arbitrary", "arbitrary", "arbitrary"), vmem_limit_bytes=V7X_VMEM_LIMIT_BYTES),
        name="nsa_attention",
    )(q, gate_pad, kct, vc, kst, vs, kwt, vw, ovl, gx)


def nsa_mixer(q, k_cmp, v_cmp, k_sel, v_sel, k_win, v_win, gate_logits, positions,
              cmp_pos_k, cmp_pos_v, w_cmp_k1, w_cmp_k2, w_cmp_v1, w_cmp_v2):
    bsz, L = q.shape[:2]
    pos = positions.astype(F32)
    n_cmp = (L - CMP_BLOCK) // CMP_STRIDE + 1
    n_cpad = L // CMP_STRIDE
    scale = HEAD_DIM ** -0.5 * math.log2(math.e)
    cidx = np.arange(n_cmp)[:, None] * CMP_STRIDE + np.arange(CMP_BLOCK)[None, :]

    def compress(kv, pe, w1, w2):
        blk = kv[:, cidx] + pe[:, None, :]
        blk = jnp.moveaxis(blk, 3, 2).reshape(bsz, n_cmp, NSA_KV_HEADS, CMP_BLOCK * HEAD_DIM)
        return jax.nn.gelu(blk @ w1) @ w2

    kc = rope(compress(k_cmp, cmp_pos_k, w_cmp_k1, w_cmp_k2), jnp.mean(pos[:, cidx], -1))
    vc = compress(v_cmp, cmp_pos_v, w_cmp_v1, w_cmp_v2)
    q = (rope(q, pos) * scale).astype(BF16).reshape(bsz, L, NSA_WIDTH)
    heads_first = lambda t: jnp.transpose(t, (0, 2, 1, 3))
    dup = lambda t: jnp.concatenate([t, t], -1).astype(BF16)
    kt_of = lambda t: jnp.swapaxes(t, -1, -2).astype(BF16)
    padn = ((0, 0), (0, 0), (0, n_cpad - n_cmp), (0, 0))
    kct = kt_of(jnp.pad(heads_first(kc), padn))
    vcd = dup(jnp.pad(heads_first(vc), padn))
    ks = heads_first(rope(k_sel, pos)).reshape(bsz, NSA_KV_HEADS, L // 512, 512, HEAD_DIM)
    kst = kt_of(ks)
    vsd = dup(heads_first(v_sel)).reshape(bsz, NSA_KV_HEADS, L // 512, 512, 128)
    kw = heads_first(rope(k_win, pos)).reshape(bsz, NSA_KV_HEADS, L // Q_BLOCK, Q_BLOCK, HEAD_DIM)
    kwt = kt_of(kw)
    vwd = dup(heads_first(v_win)).reshape(bsz, NSA_KV_HEADS, L // Q_BLOCK, Q_BLOCK, 128)
    gl = gate_logits.reshape(bsz, L, NSA_HEADS * N_BRANCH)
    gate_pad = jnp.pad(gl, ((0, 0), (0, 0), (0, 128 - NSA_HEADS * N_BRANCH)))
    return nsa_attention(q, gate_pad, kct, vcd, kst, vsd, kwt, vwd)


ROUTER_TILE = 512
MOE_TILE = 1024
MOE_ROWS = 160


def _first_max_mask(x, idx_f, axis):
    mx = jnp.max(x, axis=axis, keepdims=True)
    first = jnp.min(jnp.where(x == mx, idx_f, 1e9), axis=axis, keepdims=True)
    return idx_f == first, mx


def _router_kernel(x_ref, wrt_ref, bias_ref, w_ref, sel_ref):
    per_group = N_EXPERTS // N_EXPERT_GROUPS
    tr = x_ref.shape[0]
    nt = (((1,), (1,)), ((), ()))
    logits = lax.dot_general(wrt_ref[...], x_ref[...].astype(BF16), nt, preferred_element_type=F32)
    aff = jax.nn.sigmoid(logits)
    biased = aff + bias_ref[...]
    grp = biased.reshape(N_EXPERT_GROUPS, per_group, tr)
    in_grp = lax.broadcasted_iota(jnp.int32, grp.shape, 1).astype(F32)
    hit1, m1 = _first_max_mask(grp, in_grp, 1)
    m2 = jnp.max(jnp.where(hit1, -jnp.inf, grp), axis=1, keepdims=True)
    gscore = (m1 + m2).reshape(N_EXPERT_GROUPS, tr)
    g_idx = lax.broadcasted_iota(jnp.int32, gscore.shape, 0).astype(F32)
    gsel = jnp.zeros(gscore.shape, F32)
    for _ in range(TOPK_GROUPS):
        hit, _ = _first_max_mask(gscore, g_idx, 0)
        gsel = jnp.where(hit, 1.0, gsel)
        gscore = jnp.where(hit, -jnp.inf, gscore)
    gmask = jnp.broadcast_to(gsel.reshape(N_EXPERT_GROUPS, 1, tr), grp.shape).reshape(N_EXPERTS, tr)
    cand = jnp.where(gmask > 0.5, biased, NEG)
    e_idx = lax.broadcasted_iota(jnp.int32, cand.shape, 0).astype(F32)
    sel = jnp.zeros(cand.shape, F32)
    for _ in range(TOP_K):
        hit, _ = _first_max_mask(cand, e_idx, 0)
        sel = jnp.where(hit, 1.0, sel)
        cand = jnp.where(hit, -jnp.inf, cand)
    w = jnp.where(sel > 0.5, aff, 0.0)
    w_ref[...] = w / jnp.sum(w, axis=0, keepdims=True) * ROUTED_SCALE
    sel_ref[...] = sel


def moe_router(xt, w_router, router_bias):
    n_tok = xt.shape[0]
    wrt = w_router.T.astype(BF16)
    return pl.pallas_call(
        _router_kernel,
        grid=(n_tok // ROUTER_TILE,),
        in_specs=[pl.BlockSpec((ROUTER_TILE, D_MODEL), lambda i: (i, 0)),
                  pl.BlockSpec((N_EXPERTS, D_MODEL), lambda i: (0, 0)),
                  pl.BlockSpec((N_EXPERTS, 1), lambda i: (0, 0))],
        out_specs=[pl.BlockSpec((N_EXPERTS, ROUTER_TILE), lambda i: (0, i)),
                   pl.BlockSpec((N_EXPERTS, ROUTER_TILE), lambda i: (0, i))],
        out_shape=[jax.ShapeDtypeStruct((N_EXPERTS, n_tok), F32), jax.ShapeDtypeStruct((N_EXPERTS, n_tok), F32)],
        compiler_params=pltpu.CompilerParams(dimension_semantics=("arbitrary",),
                                             vmem_limit_bytes=V7X_VMEM_LIMIT_BYTES),
        name="moe_router",
    )(xt, wrt, router_bias.reshape(N_EXPERTS, 1))


def _moe_kernel(cnt_ref, x_ref, sel_ref, w_ref, init_ref, wg_ref, wu_ref, wd_ref, o_ref, rank_scr):
    i = pl.program_id(0)
    e = pl.program_id(1)
    tm = x_ref.shape[0]

    @pl.when(e == 0)
    def _():
        o_ref[...] = init_ref[...]
        before = (lax.broadcasted_iota(jnp.int32, (tm, tm), 0) < lax.broadcasted_iota(jnp.int32, (tm, tm), 1))
        rank_scr[...] = jnp.dot(sel_ref[...].astype(BF16), jnp.where(before, 1.0, 0.0).astype(BF16),
                                preferred_element_type=F32)

    count = cnt_ref[i * N_EXPERTS + e]
    sel_e = sel_ref[pl.ds(e, 1), :]
    rank_e = rank_scr[pl.ds(e, 1), :]
    w_e = w_ref[pl.ds(e, 1), :]

    def chunk_body(c, carry):
        row = (c * MOE_ROWS + lax.broadcasted_iota(jnp.int32, (MOE_ROWS, tm), 0)).astype(F32)
        hit = (rank_e == row) & (sel_e > 0.5)
        gather = jnp.where(hit, 1.0, 0.0).astype(BF16)
        scatter = jnp.where(hit, w_e, 0.0).astype(BF16)
        xe = jnp.dot(gather, x_ref[...], preferred_element_type=F32).astype(BF16)
        g = jnp.dot(xe, wg_ref[0], preferred_element_type=F32)
        u = jnp.dot(xe, wu_ref[0], preferred_element_type=F32)
        h = (jax.nn.silu(g) * u).astype(BF16)
        y = jnp.dot(h, wd_ref[0], preferred_element_type=F32).astype(BF16)
        tn = (((0,), (0,)), ((), ()))
        o_ref[...] += lax.dot_general(scatter, y, tn, preferred_element_type=F32)
        return carry

    lax.fori_loop(0, (count + MOE_ROWS - 1) // MOE_ROWS, chunk_body, 0)


def moe_routed(x_bf16, sel_t, w_t, init, w_gate, w_up, w_down):
    n_tok = x_bf16.shape[0]
    n_tiles = n_tok // MOE_TILE
    cnt = jnp.sum(sel_t.reshape(N_EXPERTS, n_tiles, MOE_TILE), axis=-1).T.astype(jnp.int32).reshape(-1)
    grid_spec = pltpu.PrefetchScalarGridSpec(
        num_scalar_prefetch=1,
        grid=(n_tiles, N_EXPERTS),
        in_specs=[
            pl.BlockSpec((MOE_TILE, D_MODEL), lambda i, e, cnt: (i, 0)),
            pl.BlockSpec((N_EXPERTS, MOE_TILE), lambda i, e, cnt: (0, i)),
            pl.BlockSpec((N_EXPERTS, MOE_TILE), lambda i, e, cnt: (0, i)),
            pl.BlockSpec((MOE_TILE, D_MODEL), lambda i, e, cnt: (i, 0)),
            pl.BlockSpec((1, D_MODEL, EXPERT_FF), lambda i, e, cnt: (e, 0, 0)),
            pl.BlockSpec((1, D_MODEL, EXPERT_FF), lambda i, e, cnt: (e, 0, 0)),
            pl.BlockSpec((1, EXPERT_FF, D_MODEL), lambda i, e, cnt: (e, 0, 0)),
        ],
        out_specs=pl.BlockSpec((MOE_TILE, D_MODEL), lambda i, e, cnt: (i, 0)),
        scratch_shapes=[pltpu.VMEM((N_EXPERTS, MOE_TILE), F32)],
    )
    return pl.pallas_call(
        _moe_kernel,
        grid_spec=grid_spec,
        out_shape=jax.ShapeDtypeStruct((n_tok, D_MODEL), F32),
        compiler_params=pltpu.CompilerParams(dimension_semantics=("arbitrary", "arbitrary"),
                                             vmem_limit_bytes=V7X_VMEM_LIMIT_BYTES),
        name="moe_routed",
    )(cnt, x_bf16, sel_t, w_t, init, w_gate, w_up, w_down)


def _shared_ffn_kernel(x_ref, wg_ref, wu_ref, wd_ref, o_ref):
    x = x_ref[...]
    h = jax.nn.silu(jnp.dot(x, wg_ref[...], preferred_element_type=F32)) * jnp.dot(
        x, wu_ref[...], preferred_element_type=F32)
    o_ref[...] = jnp.dot(h.astype(BF16), wd_ref[...], preferred_element_type=F32)


def shared_ffn(x_bf16, wg, wu, wd, tm=512):
    n_tok, d = x_bf16.shape
    ff = wg.shape[1]
    return pl.pallas_call(
        _shared_ffn_kernel,
        grid=(n_tok // tm,),
        in_specs=[pl.BlockSpec((tm, d), lambda i: (i, 0)), pl.BlockSpec((d, ff), lambda i: (0, 0)),
                  pl.BlockSpec((d, ff), lambda i: (0, 0)), pl.BlockSpec((ff, d), lambda i: (0, 0))],
        out_specs=pl.BlockSpec((tm, d), lambda i: (i, 0)),
        out_shape=jax.ShapeDtypeStruct((n_tok, d), F32),
        compiler_params=pltpu.CompilerParams(dimension_semantics=("arbitrary",),
                                             vmem_limit_bytes=V7X_VMEM_LIMIT_BYTES),
        name="shared_ffn",
    )(x_bf16, wg.astype(BF16), wu.astype(BF16), wd.astype(BF16))


def moe_ffn(x, w_router, router_bias, w_gate, w_up, w_down, ws_gate, ws_up, ws_down):
    bsz, L, d = x.shape
    xt = x.reshape(-1, d)
    w_t, sel_t = moe_router(xt, w_router, router_bias)
    xb = xt.astype(BF16)
    shared = shared_ffn(xb, ws_gate, ws_up, ws_down)
    out = moe_routed(xb, sel_t, w_t, shared, w_gate.astype(BF16), w_up.astype(BF16), w_down.astype(BF16))
    return out.reshape(bsz, L, d)


def hybrid_layer(x, positions, w_in, lam_re, lam_im, log_dt, ssm_b_re, ssm_b_im, ssm_c_re, ssm_c_im, ssm_d,
                 w_glu, cmp_pos_k, cmp_pos_v, w_cmp_k1, w_cmp_k2, w_cmp_v1, w_cmp_v2, w_out, ln1_g, ln1_b,
                 w_router, router_bias, w_gate, w_up, w_down, ws_gate, ws_up, ws_down, ln2_g, ln2_b):
    bsz, L, _ = x.shape
    sizes = [SSM_WIDTH, NSA_WIDTH] + [KV_WIDTH] * 6 + [NSA_HEADS * N_BRANCH]
    offsets = tuple(int(o) for o in np.cumsum(sizes)[:-1])
    xt = x.reshape(bsz * L, D_MODEL)
    n_in = w_in.shape[1]
    n_pad = -(-n_in // 128) * 128
    w_in_p = jnp.pad(w_in, ((0, 0), (0, n_pad - n_in))).astype(BF16)
    proj = matmul(xt, w_in_p)[:, :n_in].reshape(bsz, L, n_in)
    u, q, k_c, v_c, k_s, v_s, k_w, v_w, g = jnp.split(proj, offsets, axis=-1)
    kv = lambda t: t.reshape(bsz, L, NSA_KV_HEADS, HEAD_DIM)
    y_ssm = s5_mixer(u, lam_re, lam_im, log_dt, ssm_b_re, ssm_b_im, ssm_c_re, ssm_c_im, ssm_d, w_glu)
    y_nsa = nsa_mixer(q.reshape(bsz, L, NSA_HEADS, HEAD_DIM), kv(k_c), kv(v_c), kv(k_s), kv(v_s), kv(k_w), kv(v_w),
                      g.reshape(bsz, L, NSA_HEADS, N_BRANCH), positions,
                      cmp_pos_k, cmp_pos_v, w_cmp_k1, w_cmp_k2, w_cmp_v1, w_cmp_v2)
    mix = matmul(jnp.concatenate([y_ssm, y_nsa], -1).reshape(bsz * L, D_MODEL), w_out, tn=1024)
    x = layer_norm(DEEPNORM_ALPHA * x + mix.reshape(bsz, L, D_MODEL), ln1_g, ln1_b)
    ffn = moe_ffn(x, w_router, router_bias, w_gate, w_up, w_down, ws_gate, ws_up, ws_down)
    return layer_norm(DEEPNORM_ALPHA * x + ffn, ln2_g, ln2_b)


def kernel(x, positions, w_in, lam_re, lam_im, log_dt, ssm_b_re, ssm_b_im, ssm_c_re, ssm_c_im, ssm_d, w_glu, cmp_pos_k, cmp_pos_v, w_cmp_k1, w_cmp_k2, w_cmp_v1, w_cmp_v2, w_out, ln1_g, ln1_b, w_router, router_bias, w_gate, w_up, w_down, ws_gate, ws_up, ws_down, ln2_g, ln2_b):
    params = (w_in, lam_re, lam_im, log_dt, ssm_b_re, ssm_b_im, ssm_c_re, ssm_c_im, ssm_d,
              w_glu, cmp_pos_k, cmp_pos_v, w_cmp_k1, w_cmp_k2, w_cmp_v1, w_cmp_v2, w_out, ln1_g, ln1_b,
              w_router, router_bias, w_gate, w_up, w_down, ws_gate, ws_up, ws_down, ln2_g, ln2_b)
    return hybrid_layer(x, positions, *(p[0] for p in params))
```

```python
import functools
import math

import numpy as np
import jax
import jax.numpy as jnp
from jax import lax
from jax.experimental import pallas as pl
from jax.experimental.pallas import tpu as pltpu

D_MODEL = 2048
SSM_WIDTH = 1024
SSM_CH_PER_GROUP = 16
SSM_GROUPS = 64
SSM_STATE = 64
NSA_HEADS = 16
NSA_KV_HEADS = 2
HEAD_DIM = 64
Q_PER_KV = NSA_HEADS // NSA_KV_HEADS
NSA_WIDTH = NSA_HEADS * HEAD_DIM
KV_WIDTH = NSA_KV_HEADS * HEAD_DIM
N_BRANCH = 3
CMP_BLOCK = 32
CMP_STRIDE = 16
SEL_BLOCK = 64
SEL_TOPK = 16
WINDOW = 512
Q_BLOCK = 128
ROPE_THETA = 10000.0
N_EXPERTS = 64
TOP_K = 8
N_EXPERT_GROUPS = 8
TOPK_GROUPS = 4
ROUTED_SCALE = 2.5
EXPERT_FF = 512
DEPTH = 1
DEEPNORM_ALPHA = (2.0 * DEPTH) ** 0.25
LN_EPS = 1e-5
NEG = -1e30
FORCE = 1e4
F32 = jnp.float32
BF16 = jnp.bfloat16

V7X_VMEM_LIMIT_BYTES = 56 * 1024 * 1024


def _layer_norm(x, g, b):
    mu = jnp.mean(x, -1, keepdims=True)
    var = jnp.mean(jnp.square(x - mu), -1, keepdims=True)
    return (x - mu) * lax.rsqrt(var + LN_EPS) * g + b


def _rope_tables(pos_col, inv_row):
    ang = pos_col * inv_row
    return jnp.cos(ang), jnp.sin(ang)


def _rope_lanes(x, cos, sin):
    lane = lax.broadcasted_iota(jnp.int32, (x.shape[0], 128), 1)
    first_half = (lane % HEAD_DIM) < HEAD_DIM // 2
    outs = []
    for blk in range(x.shape[1] // 128):
        xb = x[:, blk * 128:(blk + 1) * 128]
        rot = jnp.where(first_half, -pltpu.roll(xb, 128 - HEAD_DIM // 2, 1), pltpu.roll(xb, HEAD_DIM // 2, 1))
        outs.append(xb * cos + rot * sin)
    return outs[0] if len(outs) == 1 else jnp.concatenate(outs, axis=1)


def _inv_freq_row():
    half = HEAD_DIM // 2
    inv = ROPE_THETA ** (-jnp.arange(half, dtype=F32) / half)
    return jnp.tile(inv, 128 // half).reshape(1, 128)


PROJ_TILE = 512
Q_SCALE = HEAD_DIM ** -0.5 * math.log2(math.e)


def _proj_uq_kernel(x_ref, w_ref, pos_ref, inv_ref, u_ref, q_ref):
    acc = jnp.dot(x_ref[...].astype(BF16), w_ref[...], preferred_element_type=F32)
    u_ref[...] = acc[:, :SSM_WIDTH]
    cos, sin = _rope_tables(pos_ref[...].astype(F32), inv_ref[...])
    q_ref[...] = (_rope_lanes(acc[:, SSM_WIDTH:], cos, sin) * Q_SCALE).astype(BF16)


def proj_uq(xt, w_uq, pos_col):
    n_tok = xt.shape[0]
    return pl.pallas_call(
        _proj_uq_kernel,
        grid=(n_tok // PROJ_TILE,),
        in_specs=[pl.BlockSpec((PROJ_TILE, D_MODEL), lambda i: (i, 0)),
                  pl.BlockSpec((D_MODEL, SSM_WIDTH + NSA_WIDTH), lambda i: (0, 0)),
                  pl.BlockSpec((PROJ_TILE, 1), lambda i: (i, 0)),
                  pl.BlockSpec((1, 128), lambda i: (0, 0))],
        out_specs=[pl.BlockSpec((PROJ_TILE, SSM_WIDTH), lambda i: (i, 0)),
                   pl.BlockSpec((PROJ_TILE, NSA_WIDTH), lambda i: (i, 0))],
        out_shape=[jax.ShapeDtypeStruct((n_tok, SSM_WIDTH), F32), jax.ShapeDtypeStruct((n_tok, NSA_WIDTH), BF16)],
        compiler_params=pltpu.CompilerParams(dimension_semantics=("arbitrary",),
                                             vmem_limit_bytes=V7X_VMEM_LIMIT_BYTES),
        name="proj_uq",
    )(xt, w_uq, pos_col, _inv_freq_row())


KV_COLS = 4 * KV_WIDTH + 2 * 2 * KV_WIDTH + 128


def _proj_kv_kernel(x_ref, w_ref, pos_ref, inv_ref, kst_ref, kwt_ref, vs_ref, vw_ref, kc_ref, vc_ref, g_ref):
    acc = jnp.dot(x_ref[0].astype(BF16), w_ref[...], preferred_element_type=F32)
    cos, sin = _rope_tables(pos_ref[0].astype(F32), inv_ref[...])
    ks_t = _rope_lanes(acc[:, 0:128], cos, sin).T
    kw_t = _rope_lanes(acc[:, 128:256], cos, sin).T
    for k in range(NSA_KV_HEADS):
        kst_ref[0, k, 0] = ks_t[k * HEAD_DIM:(k + 1) * HEAD_DIM].astype(BF16)
        for j in range(PROJ_TILE // Q_BLOCK):
            kwt_ref[0, k, j] = kw_t[k * HEAD_DIM:(k + 1) * HEAD_DIM, j * Q_BLOCK:(j + 1) * Q_BLOCK].astype(BF16)
        vs_ref[0, k, 0] = acc[:, 256 + k * 128: 256 + (k + 1) * 128].astype(BF16)
        vw_ref[0, k] = acc[:, 512 + k * 128: 512 + (k + 1) * 128].astype(BF16)
    kc_ref[0] = acc[:, 768:896]
    vc_ref[0] = acc[:, 896:1024]
    g_ref[0] = acc[:, 1024:1152]


def proj_kv(x, w_kv, pos_col3):
    bsz, seq_len, _ = x.shape
    n_t = seq_len // PROJ_TILE
    per = PROJ_TILE // Q_BLOCK
    return pl.pallas_call(
        _proj_kv_kernel,
        grid=(bsz, n_t),
        in_specs=[pl.BlockSpec((1, PROJ_TILE, D_MODEL), lambda b, i: (b, i, 0)),
                  pl.BlockSpec((D_MODEL, KV_COLS), lambda b, i: (0, 0)),
                  pl.BlockSpec((1, PROJ_TILE, 1), lambda b, i: (b, i, 0)),
                  pl.BlockSpec((1, 128), lambda b, i: (0, 0))],
        out_specs=[
            pl.BlockSpec((1, NSA_KV_HEADS, 1, HEAD_DIM, PROJ_TILE), lambda b, i: (b, 0, i, 0, 0)),
            pl.BlockSpec((1, NSA_KV_HEADS, per, HEAD_DIM, Q_BLOCK), lambda b, i: (b, 0, i, 0, 0)),
            pl.BlockSpec((1, NSA_KV_HEADS, 1, PROJ_TILE, 128), lambda b, i: (b, 0, i, 0, 0)),
            pl.BlockSpec((1, NSA_KV_HEADS, PROJ_TILE, 128), lambda b, i: (b, 0, i, 0)),
            pl.BlockSpec((1, PROJ_TILE, 128), lambda b, i: (b, i, 0)),
            pl.BlockSpec((1, PROJ_TILE, 128), lambda b, i: (b, i, 0)),
            pl.BlockSpec((1, PROJ_TILE, 128), lambda b, i: (b, i, 0)),
        ],
        out_shape=[
            jax.ShapeDtypeStruct((bsz, NSA_KV_HEADS, n_t, HEAD_DIM, PROJ_TILE), BF16),
            jax.ShapeDtypeStruct((bsz, NSA_KV_HEADS, seq_len // Q_BLOCK, HEAD_DIM, Q_BLOCK), BF16),
            jax.ShapeDtypeStruct((bsz, NSA_KV_HEADS, n_t, PROJ_TILE, 128), BF16),
            jax.ShapeDtypeStruct((bsz, NSA_KV_HEADS, seq_len, 128), BF16),
            jax.ShapeDtypeStruct((bsz, seq_len, 128), F32),
            jax.ShapeDtypeStruct((bsz, seq_len, 128), F32),
            jax.ShapeDtypeStruct((bsz, seq_len, 128), F32),
        ],
        compiler_params=pltpu.CompilerParams(dimension_semantics=("arbitrary", "arbitrary"),
                                             vmem_limit_bytes=V7X_VMEM_LIMIT_BYTES),
        name="proj_kv",
    )(x, w_kv, pos_col3, _inv_freq_row())


def _compress_kernel(ck_ref, cv_ref, pek_ref, pev_ref, w1k_ref, w1v_ref, w2k_ref, w2v_ref, pos_ref, inv_ref,
                     kct_ref, vcd_ref):
    def hidden(c_ref, pe_ref, w1_ref):
        c = c_ref[0]
        lo = jnp.dot((c + pe_ref[0]).astype(BF16), w1_ref[0], preferred_element_type=F32)
        hi = jnp.dot((c + pe_ref[1]).astype(BF16), w1_ref[1], preferred_element_type=F32)
        hi_next = jnp.concatenate([hi[1:], jnp.zeros((1, hi.shape[1]), F32)], axis=0)
        return jax.nn.gelu(lo + hi_next).astype(BF16)

    kc = jnp.dot(hidden(ck_ref, pek_ref, w1k_ref), w2k_ref[...], preferred_element_type=F32)
    cos, sin = _rope_tables(pos_ref[0], inv_ref[...])
    kc_t = _rope_lanes(kc, cos, sin).T
    vc = jnp.dot(hidden(cv_ref, pev_ref, w1v_ref), w2v_ref[...], preferred_element_type=F32)
    for k in range(NSA_KV_HEADS):
        kct_ref[0, k] = kc_t[k * HEAD_DIM:(k + 1) * HEAD_DIM].astype(BF16)
        vcd_ref[0, k] = vc[:, k * 128:(k + 1) * 128].astype(BF16)


def compress_kv(kc_raw, vc_raw, positions, cmp_pos_k, cmp_pos_v, w_k1, w_k2, w_v1, w_v2):
    bsz, seq_len, _ = kc_raw.shape
    n_chunk = seq_len // CMP_STRIDE
    width = CMP_STRIDE * 128
    eye = jnp.eye(NSA_KV_HEADS, dtype=F32)

    def chunk_pe(pe):
        pe = pe.reshape(2, CMP_STRIDE, 1, HEAD_DIM)
        return jnp.broadcast_to(pe, (2, CMP_STRIDE, NSA_KV_HEADS, HEAD_DIM)).reshape(2, 1, width)

    def chunk_w1(w1):
        hid = w1.shape[1]
        w = w1.reshape(2, CMP_STRIDE, HEAD_DIM, hid)
        return jnp.einsum('htdj,kc->htkdcj', w, eye).reshape(2, width, NSA_KV_HEADS * hid).astype(BF16)

    hid = w_k2.shape[0]
    w2k = jnp.einsum('jd,kc->kjcd', w_k2, eye).reshape(NSA_KV_HEADS * hid, NSA_KV_HEADS * HEAD_DIM).astype(BF16)
    w2v = jnp.einsum('jd,kc,r->kjcrd', w_v2, eye, jnp.ones((2,), F32)).reshape(
        NSA_KV_HEADS * hid, NSA_KV_HEADS * 128).astype(BF16)
    pos = positions.astype(F32).reshape(bsz, n_chunk, CMP_STRIDE).sum(-1)
    pos_next = jnp.concatenate([pos[:, 1:], pos[:, -1:]], axis=1)
    cmp_pos = ((pos + pos_next) / CMP_BLOCK).reshape(bsz, n_chunk, 1)
    return pl.pallas_call(
        _compress_kernel,
        grid=(bsz,),
        in_specs=[pl.BlockSpec((1, n_chunk, width), lambda b: (b, 0, 0)),
                  pl.BlockSpec((1, n_chunk, width), lambda b: (b, 0, 0)),
                  pl.BlockSpec((2, 1, width), lambda b: (0, 0, 0)),
                  pl.BlockSpec((2, 1, width), lambda b: (0, 0, 0)),
                  pl.BlockSpec((2, width, NSA_KV_HEADS * hid), lambda b: (0, 0, 0)),
                  pl.BlockSpec((2, width, NSA_KV_HEADS * hid), lambda b: (0, 0, 0)),
                  pl.BlockSpec((NSA_KV_HEADS * hid, NSA_KV_HEADS * HEAD_DIM), lambda b: (0, 0)),
                  pl.BlockSpec((NSA_KV_HEADS * hid, NSA_KV_HEADS * 128), lambda b: (0, 0)),
                  pl.BlockSpec((1, n_chunk, 1), lambda b: (b, 0, 0)),
                  pl.BlockSpec((1, 128), lambda b: (0, 0))],
        out_specs=[pl.BlockSpec((1, NSA_KV_HEADS, HEAD_DIM, n_chunk), lambda b: (b, 0, 0, 0)),
                   pl.BlockSpec((1, NSA_KV_HEADS, n_chunk, 128), lambda b: (b, 0, 0, 0))],
        out_shape=[jax.ShapeDtypeStruct((bsz, NSA_KV_HEADS, HEAD_DIM, n_chunk), BF16),
                   jax.ShapeDtypeStruct((bsz, NSA_KV_HEADS, n_chunk, 128), BF16)],
        compiler_params=pltpu.CompilerParams(dimension_semantics=("arbitrary",),
                                             vmem_limit_bytes=V7X_VMEM_LIMIT_BYTES),
        name="compress_kv",
    )(kc_raw.reshape(bsz, n_chunk, width), vc_raw.reshape(bsz, n_chunk, width), chunk_pe(cmp_pos_k),
      chunk_pe(cmp_pos_v), chunk_w1(w_k1), chunk_w1(w_v1), w2k, w2v, cmp_pos, _inv_freq_row())


def _out_ln_kernel(y_ref, a_ref, x_ref, wglu_ref, wout_ref, g_ref, b_ref, o_ref):
    y = y_ref[...]
    y_ssm = y * jax.nn.sigmoid(jnp.dot(y.astype(BF16), wglu_ref[...], preferred_element_type=F32))
    mix = (jnp.dot(y_ssm.astype(BF16), wout_ref[:SSM_WIDTH, :], preferred_element_type=F32)
           + jnp.dot(a_ref[...].astype(BF16), wout_ref[SSM_WIDTH:, :], preferred_element_type=F32))
    o_ref[...] = _layer_norm(DEEPNORM_ALPHA * x_ref[...] + mix, g_ref[...], b_ref[...])


def out_proj_ln(y_s5, y_nsa, xt, w_glu, w_out, ln_g, ln_b, tm=256):
    n_tok = xt.shape[0]
    row = lambda i: (i, 0)
    const = lambda i: (0, 0)
    return pl.pallas_call(
        _out_ln_kernel,
        grid=(n_tok // tm,),
        in_specs=[pl.BlockSpec((tm, SSM_WIDTH), row), pl.BlockSpec((tm, NSA_WIDTH), row),
                  pl.BlockSpec((tm, D_MODEL), row), pl.BlockSpec((SSM_WIDTH, SSM_WIDTH), const),
                  pl.BlockSpec((D_MODEL, D_MODEL), const), pl.BlockSpec((1, D_MODEL), const),
                  pl.BlockSpec((1, D_MODEL), const)],
        out_specs=pl.BlockSpec((tm, D_MODEL), row),
        out_shape=jax.ShapeDtypeStruct((n_tok, D_MODEL), F32),
        compiler_params=pltpu.CompilerParams(dimension_semantics=("arbitrary",),
                                             vmem_limit_bytes=V7X_VMEM_LIMIT_BYTES),
        name="out_proj_ln",
    )(y_s5, y_nsa, xt, w_glu.astype(BF16), w_out.astype(BF16), ln_g.reshape(1, D_MODEL), ln_b.reshape(1, D_MODEL))


S5_CHUNK = 512
S5_SUB = S5_CHUNK // 8
S5_GROUPS_PER_BLOCK = 8
S5_STATES = S5_GROUPS_PER_BLOCK * SSM_STATE


def _cmul_add(ar, ai, xr, xi, br, bi):
    return ar * xr - ai * xi + br, ar * xi + ai * xr + bi


def _s5_kernel(u_ref, lam_ref, bmat_ref, cmat_ref, d_ref, perm_ref, permt_ref, o_ref,
               xr_scr, xi_scr, pr_scr, pi_scr, carry_scr, a_scr, bbar_scr):
    c = pl.program_id(2)

    @pl.when(c == 0)
    def _():
        lr, li = lam_ref[0, 0:1, :], lam_ref[0, 1:2, :]
        dt = jnp.exp(lam_ref[0, 2:3, :])
        mag = jnp.exp(lr * dt)
        ar, ai = mag * jnp.cos(li * dt), mag * jnp.sin(li * dt)
        zr, zi = ar - 1.0, ai
        den = lr * lr + li * li
        fr, fi = (zr * lr + zi * li) / den, (zi * lr - zr * li) / den
        a_scr[0:1, :] = ar
        a_scr[1:2, :] = ai
        b_re, b_im = bmat_ref[0, 0], bmat_ref[0, 1]
        bbar_scr[0] = (fr * b_re - fi * b_im).astype(BF16)
        bbar_scr[1] = (fr * b_im + fi * b_re).astype(BF16)
        carry_scr[...] = jnp.zeros(carry_scr.shape, F32)
        a_re0 = jnp.broadcast_to(ar, (8, S5_STATES))
        a_im0 = jnp.broadcast_to(ai, (8, S5_STATES))

        def pw_body(i, pw):
            pr, pi = pw
            pr_scr[i] = pr
            pi_scr[i] = pi
            return a_re0 * pr - a_im0 * pi, a_re0 * pi + a_im0 * pr

        lax.fori_loop(0, S5_SUB, pw_body, (a_re0, a_im0))

    a_re = jnp.broadcast_to(a_scr[0:1, :], (8, S5_STATES))
    a_im = jnp.broadcast_to(a_scr[1:2, :], (8, S5_STATES))
    u = u_ref[0]
    perm = perm_ref[...]
    u_p = jnp.dot(perm, u.astype(BF16), preferred_element_type=F32).astype(BF16)
    xr_scr[...] = jnp.dot(u_p, bbar_scr[0], preferred_element_type=F32)
    xi_scr[...] = jnp.dot(u_p, bbar_scr[1], preferred_element_type=F32)

    def scan_body(i, x):
        row = pl.multiple_of(i * 8, 8)
        xr, xi = _cmul_add(a_re, a_im, x[0], x[1], xr_scr[pl.ds(row, 8), :], xi_scr[pl.ds(row, 8), :])
        xr_scr[pl.ds(row, 8), :] = xr
        xi_scr[pl.ds(row, 8), :] = xi
        return xr, xi

    zero = jnp.zeros((8, S5_STATES), F32)
    er, ei = lax.fori_loop(0, S5_SUB, scan_body, (zero, zero), unroll=4)

    ar_s = pr_scr[S5_SUB - 1][0:1]
    ai_s = pi_scr[S5_SUB - 1][0:1]
    rows_r = [carry_scr[0:1, :]]
    rows_i = [carry_scr[1:2, :]]
    for j in range(8):
        nr, ni = _cmul_add(ar_s, ai_s, rows_r[-1], rows_i[-1], er[j:j + 1], ei[j:j + 1])
        rows_r.append(nr)
        rows_i.append(ni)
    carry_scr[0:1, :] = rows_r[8]
    carry_scr[1:2, :] = rows_i[8]
    cr = jnp.concatenate(rows_r[:8], axis=0)
    ci = jnp.concatenate(rows_i[:8], axis=0)

    def fix_body(i, carry):
        row = pl.multiple_of(i * 8, 8)
        xr, xi = _cmul_add(pr_scr[i], pi_scr[i], cr, ci, xr_scr[pl.ds(row, 8), :], xi_scr[pl.ds(row, 8), :])
        xr_scr[pl.ds(row, 8), :] = xr
        xi_scr[pl.ds(row, 8), :] = xi
        return carry

    lax.fori_loop(0, S5_SUB, fix_body, 0, unroll=4)

    y_p = (jnp.dot(xr_scr[...].astype(BF16), cmat_ref[0, 0], preferred_element_type=F32)
           - jnp.dot(xi_scr[...].astype(BF16), cmat_ref[0, 1], preferred_element_type=F32))
    y_hi = y_p.astype(BF16)
    y_lo = (y_p - y_hi.astype(F32)).astype(BF16)
    perm_t = permt_ref[...]
    y = jnp.dot(perm_t, y_hi, preferred_element_type=F32) + jnp.dot(perm_t, y_lo, preferred_element_type=F32)
    o_ref[0] = jax.nn.gelu(y + d_ref[0] * u)


def s5_scan(u, lam_re, lam_im, log_dt, b_re, b_im, c_re, c_im, d_skip):
    bsz, seq_len, _ = u.shape
    nb = SSM_GROUPS // S5_GROUPS_PER_BLOCK
    eye = jnp.eye(S5_GROUPS_PER_BLOCK, dtype=F32)

    def blockdiag_b(m):
        m = jnp.swapaxes(m, 1, 2).reshape(nb, S5_GROUPS_PER_BLOCK, SSM_CH_PER_GROUP, SSM_STATE)
        return jnp.einsum('nghp,gk->nghkp', m, eye).reshape(nb, 128, S5_STATES)

    def blockdiag_c(m):
        m = jnp.swapaxes(m, 1, 2).reshape(nb, S5_GROUPS_PER_BLOCK, SSM_STATE, SSM_CH_PER_GROUP)
        return jnp.einsum('ngph,gk->ngpkh', m, eye).reshape(nb, S5_STATES, 128)

    log_dt_states = jnp.broadcast_to(log_dt[:, None], lam_re.shape)
    lam = jnp.stack([m.reshape(nb, S5_STATES) for m in (lam_re, lam_im, log_dt_states)], axis=1)
    bmat = jnp.stack([blockdiag_b(b_re), blockdiag_b(b_im)], axis=1)
    cmat = jnp.stack([blockdiag_c(c_re), blockdiag_c(c_im)], axis=1).astype(BF16)
    d = d_skip.reshape(nb, 1, 128)
    r = np.arange(S5_CHUNK)
    perm = np.zeros((S5_CHUNK, S5_CHUNK), np.float32)
    perm[r, (r % 8) * S5_SUB + r // 8] = 1.0
    perm = jnp.asarray(perm, BF16)
    return pl.pallas_call(
        _s5_kernel,
        grid=(bsz, nb, seq_len // S5_CHUNK),
        in_specs=[
            pl.BlockSpec((1, S5_CHUNK, 128), lambda b, g, c: (b, c, g)),
            pl.BlockSpec((1, 3, S5_STATES), lambda b, g, c: (g, 0, 0)),
            pl.BlockSpec((1, 2, 128, S5_STATES), lambda b, g, c: (g, 0, 0, 0)),
            pl.BlockSpec((1, 2, S5_STATES, 128), lambda b, g, c: (g, 0, 0, 0)),
            pl.BlockSpec((1, 1, 128), lambda b, g, c: (g, 0, 0)),
            pl.BlockSpec((S5_CHUNK, S5_CHUNK), lambda b, g, c: (0, 0)),
            pl.BlockSpec((S5_CHUNK, S5_CHUNK), lambda b, g, c: (0, 0)),
        ],
        out_specs=pl.BlockSpec((1, S5_CHUNK, 128), lambda b, g, c: (b, c, g)),
        out_shape=jax.ShapeDtypeStruct((bsz, seq_len, SSM_WIDTH), F32),
        scratch_shapes=[pltpu.VMEM((S5_CHUNK, S5_STATES), F32), pltpu.VMEM((S5_CHUNK, S5_STATES), F32),
                        pltpu.VMEM((S5_SUB, 8, S5_STATES), F32), pltpu.VMEM((S5_SUB, 8, S5_STATES), F32),
                        pltpu.VMEM((2, S5_STATES), F32), pltpu.VMEM((2, S5_STATES), F32),
                        pltpu.VMEM((2, 128, S5_STATES), BF16)],
        compiler_params=pltpu.CompilerParams(
            dimension_semantics=("arbitrary", "arbitrary", "arbitrary"), vmem_limit_bytes=V7X_VMEM_LIMIT_BYTES),
        name="s5_scan",
    )(u, lam, bmat, cmat, d, perm, perm.T)


def _softmax_tile(s, m_old):
    m_new = jnp.maximum(m_old, jnp.max(s, axis=1, keepdims=True))
    m_wide = jnp.concatenate([m_new] * (s.shape[1] // 128), axis=1)
    return m_new, jnp.exp2(m_old - m_new), jnp.exp2(s - m_wide)


def _lane_is_low(shape):
    return lax.broadcasted_iota(jnp.int32, shape, 1) < HEAD_DIM


def _pad_kt(kt, variant):
    z = jnp.zeros_like(kt)
    return jnp.concatenate([kt, z] if variant == 0 else [z, kt], axis=0)


def _pad_v(vv, variant):
    low = _lane_is_low(vv.shape)
    keep = low if variant == 0 else jnp.logical_not(low)
    return jnp.where(keep, vv, jnp.ones_like(vv))


def _finish(acc, variant):
    lane = lax.broadcasted_iota(jnp.int32, acc.shape, 1)
    lsel = lane == (HEAD_DIM if variant == 0 else 0)
    l = jnp.sum(jnp.where(lsel, acc, 0.0), axis=1, keepdims=True)
    keep = (lane < HEAD_DIM) if variant == 0 else (lane >= HEAD_DIM)
    return jnp.where(keep, acc / l, 0.0)


def _nsa_kernel(q_ref, g_ref, kct_ref, vc_ref, kst_ref, vs_ref, kwt_ref, vw_ref, ovl_ref, gx_ref, o_ref,
                m_scr, acc_scr, s_scr_a, s_scr_b, *, seq_len):
    s_slots = (s_scr_a, s_scr_b)
    n_sel = seq_len // SEL_BLOCK
    n_cpad = seq_len // CMP_STRIDE
    sel_tile = 512
    blocks_per_tile = sel_tile // SEL_BLOCK
    win_tiles = WINDOW // Q_BLOCK + 1
    n_pair = Q_PER_KV // 2
    rows = n_pair * Q_BLOCK
    i = pl.program_id(2)
    t0 = i * Q_BLOCK

    qb = q_ref[0]
    qst = jnp.concatenate([qb[:, p * 128:(p + 1) * 128] for p in range(n_pair)], axis=0)

    sig = jax.nn.sigmoid(g_ref[0])
    sig_hi = sig.astype(BF16)
    sig_lo = (sig - sig_hi.astype(F32)).astype(BF16)
    gx = gx_ref[0]
    gexp = (jnp.dot(sig_hi, gx, preferred_element_type=F32) + jnp.dot(sig_lo, gx, preferred_element_type=F32))

    def gate_of(branch):
        base = branch * n_pair * 128
        return jnp.concatenate([gexp[:, base + p * 128: base + (p + 1) * 128] for p in range(n_pair)], axis=0)

    t_row = t0 + lax.broadcasted_iota(jnp.int32, (Q_BLOCK, 1), 0)

    n_iota = lax.broadcasted_iota(jnp.int32, (Q_BLOCK, n_cpad), 1)
    cmask = (n_iota * CMP_STRIDE + (CMP_BLOCK - 1)) <= t_row
    cmask4 = jnp.concatenate([cmask] * n_pair, axis=0)
    kct = kct_ref[0, 0]
    vcd = vc_ref[0, 0]
    p_sum = jnp.zeros((Q_BLOCK, n_cpad), F32)
    out = jnp.zeros((rows, 128), F32)
    o_c = jnp.zeros((rows, 128), F32)
    for v in range(2):
        s = jnp.dot(qst, _pad_kt(kct, v), preferred_element_type=F32)
        s = jnp.where(cmask4, s, NEG)
        m = jnp.max(s, axis=1, keepdims=True)
        e = jnp.where(cmask4, jnp.exp2(s - m), 0.0)
        l = jnp.sum(e, axis=1, keepdims=True)
        p = e * (1.0 / jnp.maximum(l, 1e-30))
        for pp in range(n_pair):
            p_sum = p_sum + p[pp * Q_BLOCK:(pp + 1) * Q_BLOCK]
        low = _lane_is_low((n_cpad, 128))
        vz = jnp.where(low if v == 0 else jnp.logical_not(low), vcd, jnp.zeros_like(vcd))
        o_c = o_c + jnp.dot(p.astype(BF16), vz, preferred_element_type=F32)
    out = out + gate_of(0) * o_c

    n_kblk = seq_len // Q_BLOCK
    w0 = jnp.clip(i - (win_tiles - 1), 0, n_kblk - win_tiles)
    kw = jnp.concatenate([kwt_ref[0, 0, w0 + j] for j in range(win_tiles)], axis=1)
    vw = jnp.concatenate([vw_ref[0, 0, w0 + j] for j in range(win_tiles)], axis=0)
    kpos_w = w0 * Q_BLOCK + lax.broadcasted_iota(jnp.int32, (Q_BLOCK, win_tiles * Q_BLOCK), 1)
    diff = t_row - kpos_w
    wbias = jnp.where((diff >= 0) & (diff < WINDOW), 0.0, NEG)
    wbias4 = jnp.concatenate([wbias] * n_pair, axis=0)
    o_w = jnp.zeros((rows, 128), F32)
    for v in range(2):
        s = jnp.dot(qst, _pad_kt(kw, v), preferred_element_type=F32) + wbias4
        m = jnp.max(s, axis=1, keepdims=True)
        p = jnp.exp2(s - m)
        o_w = o_w + _finish(jnp.dot(p.astype(BF16), _pad_v(vw, v), preferred_element_type=F32), v)
    out = out + gate_of(2) * o_w

    ps_hi = p_sum.astype(BF16)
    ps_lo = (p_sum - ps_hi.astype(F32)).astype(BF16)
    ovl = ovl_ref[...]
    nt = (((1,), (1,)), ((), ()))
    imp_t = (lax.dot_general(ovl, ps_hi, nt, preferred_element_type=F32)
             + lax.dot_general(ovl, ps_lo, nt, preferred_element_type=F32))
    s_iota = lax.broadcasted_iota(jnp.int32, (n_sel, Q_BLOCK), 0)
    t_lane = t0 + lax.broadcasted_iota(jnp.int32, (n_sel, Q_BLOCK), 1)
    cur = t_lane // SEL_BLOCK
    forced = (s_iota == 0) | (s_iota == cur) | (s_iota == cur - 1)
    valid = s_iota * SEL_BLOCK <= t_lane
    score = jnp.where(forced, FORCE, jnp.where(valid, imp_t, -1.0))
    s_f = s_iota.astype(F32)
    sel_t = jnp.zeros((n_sel, Q_BLOCK), F32)
    for _ in range(min(SEL_TOPK, n_sel)):
        mx = jnp.max(score, axis=0, keepdims=True)
        idx = jnp.min(jnp.where(score == mx, s_f, float(n_sel)), axis=0, keepdims=True)
        hit = s_f == idx
        sel_t = jnp.where(hit, 1.0, sel_t)
        score = jnp.where(hit, -3e38, score)
    selmask = sel_t.T.astype(BF16)

    m_scr[...] = jnp.full(m_scr.shape, NEG, F32)
    acc_scr[...] = jnp.zeros(acc_scr.shape, F32)
    n_tiles = (t0 + Q_BLOCK + sel_tile - 1) // sel_tile

    last_tile = seq_len // sel_tile - 1

    def scores_into(slot, kt):
        kt_tile = kst_ref[0, 0, jnp.minimum(kt, last_tile)]
        for v in range(2):
            s_slots[slot][v] = jnp.dot(qst, _pad_kt(kt_tile, v), preferred_element_type=F32)

    def bias_of(kt):
        blk = kt * blocks_per_tile + lax.broadcasted_iota(jnp.int32, (n_sel, sel_tile), 1) // SEL_BLOCK
        expand = (lax.broadcasted_iota(jnp.int32, (n_sel, sel_tile), 0) == blk).astype(BF16)
        selexp = jnp.dot(selmask, expand, preferred_element_type=F32)
        kpos = kt * sel_tile + lax.broadcasted_iota(jnp.int32, (Q_BLOCK, sel_tile), 1)
        bias = jnp.where((selexp > 0.5) & (kpos <= t_row), 0.0, NEG)
        return jnp.concatenate([bias] * n_pair, axis=0)

    def attend_from(slot, kt, bias4):
        v_tile = vs_ref[0, 0, jnp.minimum(kt, last_tile)]
        for v in range(2):
            m_new, alpha, p = _softmax_tile(s_slots[slot][v] + bias4, m_scr[v])
            m_scr[v] = m_new
            acc_scr[v] = alpha * acc_scr[v] + jnp.dot(p.astype(BF16), _pad_v(v_tile, v), preferred_element_type=F32)

    scores_into(0, 0)

    def sel_body(j, carry):
        kt = 2 * j
        bias_a = bias_of(kt)
        scores_into(1, kt + 1)
        attend_from(0, kt, bias_a)
        bias_b = bias_of(kt + 1)
        scores_into(0, kt + 2)
        attend_from(1, kt + 1, bias_b)
        return carry

    lax.fori_loop(0, (n_tiles + 1) // 2, sel_body, 0)
    out = out + gate_of(1) * (_finish(acc_scr[0], 0) + _finish(acc_scr[1], 1))

    o_ref[0] = jnp.concatenate([out[p * Q_BLOCK:(p + 1) * Q_BLOCK] for p in range(n_pair)], axis=1)


def nsa_attention(q, gate_pad, kct, vc, kst, vs, kwt, vw):
    bsz, seq_len, _ = q.shape
    n_sel = seq_len // SEL_BLOCK
    n_cpad = seq_len // CMP_STRIDE
    n_cmp = (seq_len - CMP_BLOCK) // CMP_STRIDE + 1
    n_pair = Q_PER_KV // 2
    cs = np.arange(n_cpad) * CMP_STRIDE
    ce = cs + CMP_BLOCK - 1
    ss = np.arange(n_sel) * SEL_BLOCK
    se = ss + SEL_BLOCK - 1
    ovl = (cs[None, :] <= se[:, None]) & (ce[None, :] >= ss[:, None]) & (np.arange(n_cpad)[None, :] < n_cmp)
    ovl = jnp.asarray(ovl.astype(np.float32), BF16)
    gx = np.zeros((NSA_KV_HEADS, 128, N_BRANCH * n_pair * 128), np.float32)
    for k in range(NSA_KV_HEADS):
        for hl in range(Q_PER_KV):
            for br in range(N_BRANCH):
                c0 = br * n_pair * 128 + hl * HEAD_DIM
                gx[k, (k * Q_PER_KV + hl) * N_BRANCH + br, c0:c0 + HEAD_DIM] = 1.0
    gx = jnp.asarray(gx, BF16)
    width = Q_PER_KV * HEAD_DIM
    full = lambda *shape: pl.BlockSpec((1, 1) + shape, lambda b, k, i: (b, k) + (0,) * len(shape))
    return pl.pallas_call(
        functools.partial(_nsa_kernel, seq_len=seq_len),
        grid=(bsz, NSA_KV_HEADS, seq_len // Q_BLOCK),
        in_specs=[
            pl.BlockSpec((1, Q_BLOCK, width), lambda b, k, i: (b, i, k)),
            pl.BlockSpec((1, Q_BLOCK, 128), lambda b, k, i: (b, i, 0)),
            full(HEAD_DIM, n_cpad), full(n_cpad, 128),
            full(seq_len // 512, HEAD_DIM, 512), full(seq_len // 512, 512, 128),
            full(seq_len // Q_BLOCK, HEAD_DIM, Q_BLOCK), full(seq_len // Q_BLOCK, Q_BLOCK, 128),
            pl.BlockSpec((n_sel, n_cpad), lambda b, k, i: (0, 0)),
            pl.BlockSpec((1, 128, N_BRANCH * n_pair * 128), lambda b, k, i: (k, 0, 0)),
        ],
        out_specs=pl.BlockSpec((1, Q_BLOCK, width), lambda b, k, i: (b, i, k)),
        out_shape=jax.ShapeDtypeStruct((bsz, seq_len, NSA_WIDTH), F32),
        scratch_shapes=[pltpu.VMEM((2, n_pair * Q_BLOCK, 128), F32), pltpu.VMEM((2, n_pair * Q_BLOCK, 128), F32),
                        pltpu.VMEM((2, n_pair * Q_BLOCK, 512), F32), pltpu.VMEM((2, n_pair * Q_BLOCK, 512), F32)],
        compiler_params=pltpu.CompilerParams(
            dimension_semantics=("arbitrary", "arbitrary", "arbitrary"), vmem_limit_bytes=V7X_VMEM_LIMIT_BYTES),
        name="nsa_attention",
    )(q, gate_pad, kct, vc, kst, vs, kwt, vw, ovl, gx)


ROUTER_TILE = 512
MOE_TILE = 1024
MOE_ROWS = 160


def _first_max_mask(x, idx_f, axis):
    mx = jnp.max(x, axis=axis, keepdims=True)
    first = jnp.min(jnp.where(x == mx, idx_f, 1e9), axis=axis, keepdims=True)
    return idx_f == first, mx


def _router_kernel(x_ref, wrt_ref, bias_ref, w_ref, sel_ref):
    per_group = N_EXPERTS // N_EXPERT_GROUPS
    tr = x_ref.shape[0]
    nt = (((1,), (1,)), ((), ()))
    logits = lax.dot_general(wrt_ref[...], x_ref[...].astype(BF16), nt, preferred_element_type=F32)
    aff = jax.nn.sigmoid(logits)
    biased = aff + bias_ref[...]
    grp = biased.reshape(N_EXPERT_GROUPS, per_group, tr)
    in_grp = lax.broadcasted_iota(jnp.int32, grp.shape, 1).astype(F32)
    hit1, m1 = _first_max_mask(grp, in_grp, 1)
    m2 = jnp.max(jnp.where(hit1, -jnp.inf, grp), axis=1, keepdims=True)
    gscore = (m1 + m2).reshape(N_EXPERT_GROUPS, tr)
    g_idx = lax.broadcasted_iota(jnp.int32, gscore.shape, 0).astype(F32)
    gsel = jnp.zeros(gscore.shape, F32)
    for _ in range(TOPK_GROUPS):
        hit, _ = _first_max_mask(gscore, g_idx, 0)
        gsel = jnp.where(hit, 1.0, gsel)
        gscore = jnp.where(hit, -jnp.inf, gscore)
    gmask = jnp.broadcast_to(gsel.reshape(N_EXPERT_GROUPS, 1, tr), grp.shape).reshape(N_EXPERTS, tr)
    cand = jnp.where(gmask > 0.5, biased, NEG)
    e_idx = lax.broadcasted_iota(jnp.int32, cand.shape, 0).astype(F32)
    sel = jnp.zeros(cand.shape, F32)
    for _ in range(TOP_K):
        hit, _ = _first_max_mask(cand, e_idx, 0)
        sel = jnp.where(hit, 1.0, sel)
        cand = jnp.where(hit, -jnp.inf, cand)
    w = jnp.where(sel > 0.5, aff, 0.0)
    w_ref[...] = w / jnp.sum(w, axis=0, keepdims=True) * ROUTED_SCALE
    sel_ref[...] = sel


def moe_router(xt, w_router, router_bias):
    n_tok = xt.shape[0]
    wrt = w_router.T.astype(BF16)
    return pl.pallas_call(
        _router_kernel,
        grid=(n_tok // ROUTER_TILE,),
        in_specs=[pl.BlockSpec((ROUTER_TILE, D_MODEL), lambda i: (i, 0)),
                  pl.BlockSpec((N_EXPERTS, D_MODEL), lambda i: (0, 0)),
                  pl.BlockSpec((N_EXPERTS, 1), lambda i: (0, 0))],
        out_specs=[pl.BlockSpec((N_EXPERTS, ROUTER_TILE), lambda i: (0, i)),
                   pl.BlockSpec((N_EXPERTS, ROUTER_TILE), lambda i: (0, i))],
        out_shape=[jax.ShapeDtypeStruct((N_EXPERTS, n_tok), F32), jax.ShapeDtypeStruct((N_EXPERTS, n_tok), F32)],
        compiler_params=pltpu.CompilerParams(dimension_semantics=("arbitrary",),
                                             vmem_limit_bytes=V7X_VMEM_LIMIT_BYTES),
        name="moe_router",
    )(xt, wrt, router_bias.reshape(N_EXPERTS, 1))


def _moe_kernel(cnt_ref, x_ref, sel_ref, w_ref, init_ref, wg_ref, wu_ref, wd_ref, lng_ref, lnb_ref, o_ref, rank_scr):
    i = pl.program_id(0)
    e = pl.program_id(1)
    tm = x_ref.shape[0]

    @pl.when(e == 0)
    def _():
        o_ref[...] = init_ref[...]
        before = (lax.broadcasted_iota(jnp.int32, (tm, tm), 0) < lax.broadcasted_iota(jnp.int32, (tm, tm), 1))
        rank_scr[...] = jnp.dot(sel_ref[...].astype(BF16), jnp.where(before, 1.0, 0.0).astype(BF16),
                                preferred_element_type=F32)

    count = cnt_ref[i * N_EXPERTS + e]
    sel_e = sel_ref[pl.ds(e, 1), :]
    rank_e = rank_scr[pl.ds(e, 1), :]
    w_e = w_ref[pl.ds(e, 1), :]

    def chunk_body(c, carry):
        row = (c * MOE_ROWS + lax.broadcasted_iota(jnp.int32, (MOE_ROWS, tm), 0)).astype(F32)
        hit = (rank_e == row) & (sel_e > 0.5)
        gather = jnp.where(hit, 1.0, 0.0).astype(BF16)
        scatter = jnp.where(hit, w_e, 0.0).astype(BF16)
        xe = jnp.dot(gather, x_ref[...], preferred_element_type=F32).astype(BF16)
        g = jnp.dot(xe, wg_ref[0], preferred_element_type=F32)
        u = jnp.dot(xe, wu_ref[0], preferred_element_type=F32)
        h = (jax.nn.silu(g) * u).astype(BF16)
        y = jnp.dot(h, wd_ref[0], preferred_element_type=F32).astype(BF16)
        tn = (((0,), (0,)), ((), ()))
        o_ref[...] += lax.dot_general(scatter, y, tn, preferred_element_type=F32)
        return carry

    lax.fori_loop(0, (count + MOE_ROWS - 1) // MOE_ROWS, chunk_body, 0)

    @pl.when(e == N_EXPERTS - 1)
    def _():
        o_ref[...] = _layer_norm(o_ref[...], lng_ref[...], lnb_ref[...])


def moe_routed(x_bf16, sel_t, w_t, init, w_gate, w_up, w_down, ln_g, ln_b):
    n_tok = x_bf16.shape[0]
    n_tiles = n_tok // MOE_TILE
    cnt = jnp.sum(sel_t.reshape(N_EXPERTS, n_tiles, MOE_TILE), axis=-1).T.astype(jnp.int32).reshape(-1)
    grid_spec = pltpu.PrefetchScalarGridSpec(
        num_scalar_prefetch=1,
        grid=(n_tiles, N_EXPERTS),
        in_specs=[
            pl.BlockSpec((MOE_TILE, D_MODEL), lambda i, e, cnt: (i, 0)),
            pl.BlockSpec((N_EXPERTS, MOE_TILE), lambda i, e, cnt: (0, i)),
            pl.BlockSpec((N_EXPERTS, MOE_TILE), lambda i, e, cnt: (0, i)),
            pl.BlockSpec((MOE_TILE, D_MODEL), lambda i, e, cnt: (i, 0), pipeline_mode=pl.Buffered(1)),
            pl.BlockSpec((1, D_MODEL, EXPERT_FF), lambda i, e, cnt: (e, 0, 0)),
            pl.BlockSpec((1, D_MODEL, EXPERT_FF), lambda i, e, cnt: (e, 0, 0)),
            pl.BlockSpec((1, EXPERT_FF, D_MODEL), lambda i, e, cnt: (e, 0, 0)),
            pl.BlockSpec((1, D_MODEL), lambda i, e, cnt: (0, 0)),
            pl.BlockSpec((1, D_MODEL), lambda i, e, cnt: (0, 0)),
        ],
        out_specs=pl.BlockSpec((MOE_TILE, D_MODEL), lambda i, e, cnt: (i, 0)),
        scratch_shapes=[pltpu.VMEM((N_EXPERTS, MOE_TILE), F32)],
    )
    return pl.pallas_call(
        _moe_kernel,
        grid_spec=grid_spec,
        out_shape=jax.ShapeDtypeStruct((n_tok, D_MODEL), F32),
        compiler_params=pltpu.CompilerParams(dimension_semantics=("arbitrary", "arbitrary"),
                                             vmem_limit_bytes=V7X_VMEM_LIMIT_BYTES),
        name="moe_routed",
    )(cnt, x_bf16, sel_t, w_t, init, w_gate, w_up, w_down, ln_g.reshape(1, D_MODEL), ln_b.reshape(1, D_MODEL))


def _shared_ffn_kernel(x_ref, wg_ref, wu_ref, wd_ref, o_ref, xb_ref):
    x = x_ref[...]
    xb = x.astype(BF16)
    h = jax.nn.silu(jnp.dot(xb, wg_ref[...], preferred_element_type=F32)) * jnp.dot(
        xb, wu_ref[...], preferred_element_type=F32)
    o_ref[...] = DEEPNORM_ALPHA * x + jnp.dot(h.astype(BF16), wd_ref[...], preferred_element_type=F32)
    xb_ref[...] = xb


def shared_ffn(xt, wg, wu, wd, tm=512):
    n_tok, d = xt.shape
    ff = wg.shape[1]
    return pl.pallas_call(
        _shared_ffn_kernel,
        grid=(n_tok // tm,),
        in_specs=[pl.BlockSpec((tm, d), lambda i: (i, 0)), pl.BlockSpec((d, ff), lambda i: (0, 0)),
                  pl.BlockSpec((d, ff), lambda i: (0, 0)), pl.BlockSpec((ff, d), lambda i: (0, 0))],
        out_specs=[pl.BlockSpec((tm, d), lambda i: (i, 0)), pl.BlockSpec((tm, d), lambda i: (i, 0))],
        out_shape=[jax.ShapeDtypeStruct((n_tok, d), F32), jax.ShapeDtypeStruct((n_tok, d), BF16)],
        compiler_params=pltpu.CompilerParams(dimension_semantics=("arbitrary",),
                                             vmem_limit_bytes=V7X_VMEM_LIMIT_BYTES),
        name="shared_ffn",
    )(xt, wg.astype(BF16), wu.astype(BF16), wd.astype(BF16))


def hybrid_layer(x, positions, w_in, lam_re, lam_im, log_dt, ssm_b_re, ssm_b_im, ssm_c_re, ssm_c_im, ssm_d,
                 w_glu, cmp_pos_k, cmp_pos_v, w_cmp_k1, w_cmp_k2, w_cmp_v1, w_cmp_v2, w_out, ln1_g, ln1_b,
                 w_router, router_bias, w_gate, w_up, w_down, ws_gate, ws_up, ws_down, ln2_g, ln2_b):
    bsz, L, _ = x.shape
    sizes = [SSM_WIDTH, NSA_WIDTH] + [KV_WIDTH] * 6 + [NSA_HEADS * N_BRANCH]
    o = [0] + [int(v) for v in np.cumsum(sizes)]
    col = lambda j: w_in[:, o[j]:o[j + 1]]
    dup = lambda w: jnp.concatenate([w[:, h * HEAD_DIM:(h + 1) * HEAD_DIM] for h in (0, 0, 1, 1)], axis=1)
    gate_cols = jnp.pad(col(8), ((0, 0), (0, 128 - NSA_HEADS * N_BRANCH)))
    w_uq = w_in[:, :o[2]].astype(BF16)
    w_kv = jnp.concatenate([col(4), col(6), dup(col(5)), dup(col(7)), col(2), col(3), gate_cols], axis=1).astype(BF16)

    xt = x.reshape(bsz * L, D_MODEL)
    u, q = proj_uq(xt, w_uq, positions.reshape(bsz * L, 1))
    kst, kwt, vs, vw, kc_raw, vc_raw, gate_pad = proj_kv(x, w_kv, positions.reshape(bsz, L, 1))
    kct, vcd = compress_kv(kc_raw, vc_raw, positions, cmp_pos_k, cmp_pos_v, w_cmp_k1, w_cmp_k2, w_cmp_v1, w_cmp_v2)
    y_s5 = s5_scan(u.reshape(bsz, L, SSM_WIDTH), lam_re, lam_im, log_dt, ssm_b_re, ssm_b_im, ssm_c_re, ssm_c_im, ssm_d)
    vw = vw.reshape(bsz, NSA_KV_HEADS, L // Q_BLOCK, Q_BLOCK, 128)
    y_nsa = nsa_attention(q.reshape(bsz, L, NSA_WIDTH), gate_pad, kct, vcd, kst, vs, kwt, vw)
    x1 = out_proj_ln(y_s5.reshape(bsz * L, SSM_WIDTH), y_nsa.reshape(bsz * L, NSA_WIDTH), xt, w_glu, w_out,
                     ln1_g, ln1_b)
    w_t, sel_t = moe_router(x1, w_router, router_bias)
    acc0, x1b = shared_ffn(x1, ws_gate, ws_up, ws_down)
    out = moe_routed(x1b, sel_t, w_t, acc0, w_gate.astype(BF16), w_up.astype(BF16), w_down.astype(BF16),
                     ln2_g, ln2_b)
    return out.reshape(bsz, L, D_MODEL)


def kernel(x, positions, w_in, lam_re, lam_im, log_dt, ssm_b_re, ssm_b_im, ssm_c_re, ssm_c_im, ssm_d, w_glu, cmp_pos_k, cmp_pos_v, w_cmp_k1, w_cmp_k2, w_cmp_v1, w_cmp_v2, w_out, ln1_g, ln1_b, w_router, router_bias, w_gate, w_up, w_down, ws_gate, ws_up, ws_down, ln2_g, ln2_b):
    params = (w_in, lam_re, lam_im, log_dt, ssm_b_re, ssm_b_im, ssm_c_re, ssm_c_im, ssm_d,
              w_glu, cmp_pos_k, cmp_pos_v, w_cmp_k1, w_cmp_k2, w_cmp_v1, w_cmp_v2, w_out, ln1_g, ln1_b,
              w_router, router_bias, w_gate, w_up, w_down, ws_gate, ws_up, ws_down, ln2_g, ln2_b)
    return hybrid_layer(x, positions, *(p[0] for p in params))
```

```python
import functools
import math

import numpy as np
import jax
import jax.numpy as jnp
from jax import lax
from jax.experimental import pallas as pl
from jax.experimental.pallas import tpu as pltpu

D_MODEL = 2048
SSM_WIDTH = 1024
SSM_CH_PER_GROUP = 16
SSM_GROUPS = 64
SSM_STATE = 64
NSA_HEADS = 16
NSA_KV_HEADS = 2
HEAD_DIM = 64
Q_PER_KV = NSA_HEADS // NSA_KV_HEADS
NSA_WIDTH = NSA_HEADS * HEAD_DIM
KV_WIDTH = NSA_KV_HEADS * HEAD_DIM
N_BRANCH = 3
CMP_BLOCK = 32
CMP_STRIDE = 16
SEL_BLOCK = 64
SEL_TOPK = 16
WINDOW = 512
Q_BLOCK = 128
ROPE_THETA = 10000.0
N_EXPERTS = 64
TOP_K = 8
N_EXPERT_GROUPS = 8
TOPK_GROUPS = 4
ROUTED_SCALE = 2.5
EXPERT_FF = 512
DEPTH = 1
DEEPNORM_ALPHA = (2.0 * DEPTH) ** 0.25
LN_EPS = 1e-5
NEG = -1e30
FORCE = 1e4
F32 = jnp.float32
BF16 = jnp.bfloat16

V7X_VMEM_LIMIT_BYTES = 56 * 1024 * 1024


def _layer_norm(x, g, b):
    mu = jnp.mean(x, -1, keepdims=True)
    var = jnp.mean(jnp.square(x - mu), -1, keepdims=True)
    return (x - mu) * lax.rsqrt(var + LN_EPS) * g + b


def _rope_tables(pos_col, inv_row):
    ang = pos_col * inv_row
    return jnp.cos(ang), jnp.sin(ang)


def _rope_lanes(x, cos, sin):
    lane = lax.broadcasted_iota(jnp.int32, (x.shape[0], 128), 1)
    first_half = (lane % HEAD_DIM) < HEAD_DIM // 2
    outs = []
    for blk in range(x.shape[1] // 128):
        xb = x[:, blk * 128:(blk + 1) * 128]
        rot = jnp.where(first_half, -pltpu.roll(xb, 128 - HEAD_DIM // 2, 1), pltpu.roll(xb, HEAD_DIM // 2, 1))
        outs.append(xb * cos + rot * sin)
    return outs[0] if len(outs) == 1 else jnp.concatenate(outs, axis=1)


def _inv_freq_row():
    half = HEAD_DIM // 2
    inv = ROPE_THETA ** (-jnp.arange(half, dtype=F32) / half)
    return jnp.tile(inv, 128 // half).reshape(1, 128)


PROJ_TILE = 512
Q_SCALE = HEAD_DIM ** -0.5 * math.log2(math.e)


def _proj_uq_kernel(x_ref, w_ref, pos_ref, inv_ref, u_ref, q_ref):
    acc = jnp.dot(x_ref[...].astype(BF16), w_ref[...], preferred_element_type=F32)
    u_ref[...] = acc[:, :SSM_WIDTH]
    cos, sin = _rope_tables(pos_ref[...].astype(F32), inv_ref[...])
    q_ref[...] = (_rope_lanes(acc[:, SSM_WIDTH:], cos, sin) * Q_SCALE).astype(BF16)


def proj_uq(xt, w_uq, pos_col):
    n_tok = xt.shape[0]
    return pl.pallas_call(
        _proj_uq_kernel,
        grid=(n_tok // PROJ_TILE,),
        in_specs=[pl.BlockSpec((PROJ_TILE, D_MODEL), lambda i: (i, 0)),
                  pl.BlockSpec((D_MODEL, SSM_WIDTH + NSA_WIDTH), lambda i: (0, 0)),
                  pl.BlockSpec((PROJ_TILE, 1), lambda i: (i, 0)),
                  pl.BlockSpec((1, 128), lambda i: (0, 0))],
        out_specs=[pl.BlockSpec((PROJ_TILE, SSM_WIDTH), lambda i: (i, 0)),
                   pl.BlockSpec((PROJ_TILE, NSA_WIDTH), lambda i: (i, 0))],
        out_shape=[jax.ShapeDtypeStruct((n_tok, SSM_WIDTH), F32), jax.ShapeDtypeStruct((n_tok, NSA_WIDTH), BF16)],
        compiler_params=pltpu.CompilerParams(dimension_semantics=("arbitrary",),
                                             vmem_limit_bytes=V7X_VMEM_LIMIT_BYTES),
        name="proj_uq",
    )(xt, w_uq, pos_col, _inv_freq_row())


KV_COLS = 4 * KV_WIDTH + 2 * 2 * KV_WIDTH + 128


def _proj_kv_kernel(x_ref, w_ref, pos_ref, inv_ref, kst_ref, kwt_ref, vs_ref, vw_ref, kc_ref, vc_ref, g_ref):
    acc = jnp.dot(x_ref[0].astype(BF16), w_ref[...], preferred_element_type=F32)
    cos, sin = _rope_tables(pos_ref[0].astype(F32), inv_ref[...])
    ks_t = _rope_lanes(acc[:, 0:128], cos, sin).T
    kw_t = _rope_lanes(acc[:, 128:256], cos, sin).T
    for k in range(NSA_KV_HEADS):
        kst_ref[0, k, 0] = ks_t[k * HEAD_DIM:(k + 1) * HEAD_DIM].astype(BF16)
        for j in range(PROJ_TILE // Q_BLOCK):
            kwt_ref[0, k, j] = kw_t[k * HEAD_DIM:(k + 1) * HEAD_DIM, j * Q_BLOCK:(j + 1) * Q_BLOCK].astype(BF16)
        vs_ref[0, k, 0] = acc[:, 256 + k * 128: 256 + (k + 1) * 128].astype(BF16)
        vw_ref[0, k] = acc[:, 512 + k * 128: 512 + (k + 1) * 128].astype(BF16)
    kc_ref[0] = acc[:, 768:896]
    vc_ref[0] = acc[:, 896:1024]
    g_ref[0] = acc[:, 1024:1152]


def proj_kv(x, w_kv, pos_col3):
    bsz, seq_len, _ = x.shape
    n_t = seq_len // PROJ_TILE
    per = PROJ_TILE // Q_BLOCK
    return pl.pallas_call(
        _proj_kv_kernel,
        grid=(bsz, n_t),
        in_specs=[pl.BlockSpec((1, PROJ_TILE, D_MODEL), lambda b, i: (b, i, 0)),
                  pl.BlockSpec((D_MODEL, KV_COLS), lambda b, i: (0, 0)),
                  pl.BlockSpec((1, PROJ_TILE, 1), lambda b, i: (b, i, 0)),
                  pl.BlockSpec((1, 128), lambda b, i: (0, 0))],
        out_specs=[
            pl.BlockSpec((1, NSA_KV_HEADS, 1, HEAD_DIM, PROJ_TILE), lambda b, i: (b, 0, i, 0, 0)),
            pl.BlockSpec((1, NSA_KV_HEADS, per, HEAD_DIM, Q_BLOCK), lambda b, i: (b, 0, i, 0, 0)),
            pl.BlockSpec((1, NSA_KV_HEADS, 1, PROJ_TILE, 128), lambda b, i: (b, 0, i, 0, 0)),
            pl.BlockSpec((1, NSA_KV_HEADS, PROJ_TILE, 128), lambda b, i: (b, 0, i, 0)),
            pl.BlockSpec((1, PROJ_TILE, 128), lambda b, i: (b, i, 0)),
            pl.BlockSpec((1, PROJ_TILE, 128), lambda b, i: (b, i, 0)),
            pl.BlockSpec((1, PROJ_TILE, 128), lambda b, i: (b, i, 0)),
        ],
        out_shape=[
            jax.ShapeDtypeStruct((bsz, NSA_KV_HEADS, n_t, HEAD_DIM, PROJ_TILE), BF16),
            jax.ShapeDtypeStruct((bsz, NSA_KV_HEADS, seq_len // Q_BLOCK, HEAD_DIM, Q_BLOCK), BF16),
            jax.ShapeDtypeStruct((bsz, NSA_KV_HEADS, n_t, PROJ_TILE, 128), BF16),
            jax.ShapeDtypeStruct((bsz, NSA_KV_HEADS, seq_len, 128), BF16),
            jax.ShapeDtypeStruct((bsz, seq_len, 128), F32),
            jax.ShapeDtypeStruct((bsz, seq_len, 128), F32),
            jax.ShapeDtypeStruct((bsz, seq_len, 128), F32),
        ],
        compiler_params=pltpu.CompilerParams(dimension_semantics=("arbitrary", "arbitrary"),
                                             vmem_limit_bytes=V7X_VMEM_LIMIT_BYTES),
        name="proj_kv",
    )(x, w_kv, pos_col3, _inv_freq_row())


def _compress_kernel(ck_ref, cv_ref, pek_ref, pev_ref, w1k_ref, w1v_ref, w2k_ref, w2v_ref, pos_ref, inv_ref,
                     kct_ref, vcd_ref):
    def hidden(c_ref, pe_ref, w1_ref):
        c = c_ref[0]
        lo = jnp.dot((c + pe_ref[0]).astype(BF16), w1_ref[0], preferred_element_type=F32)
        hi = jnp.dot((c + pe_ref[1]).astype(BF16), w1_ref[1], preferred_element_type=F32)
        hi_next = jnp.concatenate([hi[1:], jnp.zeros((1, hi.shape[1]), F32)], axis=0)
        return jax.nn.gelu(lo + hi_next).astype(BF16)

    kc = jnp.dot(hidden(ck_ref, pek_ref, w1k_ref), w2k_ref[...], preferred_element_type=F32)
    cos, sin = _rope_tables(pos_ref[0], inv_ref[...])
    kc_t = _rope_lanes(kc, cos, sin).T
    vc = jnp.dot(hidden(cv_ref, pev_ref, w1v_ref), w2v_ref[...], preferred_element_type=F32)
    for k in range(NSA_KV_HEADS):
        kct_ref[0, k] = kc_t[k * HEAD_DIM:(k + 1) * HEAD_DIM].astype(BF16)
        vcd_ref[0, k] = vc[:, k * 128:(k + 1) * 128].astype(BF16)


def compress_kv(kc_raw, vc_raw, positions, cmp_pos_k, cmp_pos_v, w_k1, w_k2, w_v1, w_v2):
    bsz, seq_len, _ = kc_raw.shape
    n_chunk = seq_len // CMP_STRIDE
    width = CMP_STRIDE * 128
    eye = jnp.eye(NSA_KV_HEADS, dtype=F32)

    def chunk_pe(pe):
        pe = pe.reshape(2, CMP_STRIDE, 1, HEAD_DIM)
        return jnp.broadcast_to(pe, (2, CMP_STRIDE, NSA_KV_HEADS, HEAD_DIM)).reshape(2, 1, width)

    def chunk_w1(w1):
        hid = w1.shape[1]
        w = w1.reshape(2, CMP_STRIDE, HEAD_DIM, hid)
        return jnp.einsum('htdj,kc->htkdcj', w, eye).reshape(2, width, NSA_KV_HEADS * hid).astype(BF16)

    hid = w_k2.shape[0]
    w2k = jnp.einsum('jd,kc->kjcd', w_k2, eye).reshape(NSA_KV_HEADS * hid, NSA_KV_HEADS * HEAD_DIM).astype(BF16)
    w2v = jnp.einsum('jd,kc,r->kjcrd', w_v2, eye, jnp.ones((2,), F32)).reshape(
        NSA_KV_HEADS * hid, NSA_KV_HEADS * 128).astype(BF16)
    pos = positions.astype(F32).reshape(bsz, n_chunk, CMP_STRIDE).sum(-1)
    pos_next = jnp.concatenate([pos[:, 1:], pos[:, -1:]], axis=1)
    cmp_pos = ((pos + pos_next) / CMP_BLOCK).reshape(bsz, n_chunk, 1)
    return pl.pallas_call(
        _compress_kernel,
        grid=(bsz,),
        in_specs=[pl.BlockSpec((1, n_chunk, width), lambda b: (b, 0, 0)),
                  pl.BlockSpec((1, n_chunk, width), lambda b: (b, 0, 0)),
                  pl.BlockSpec((2, 1, width), lambda b: (0, 0, 0)),
                  pl.BlockSpec((2, 1, width), lambda b: (0, 0, 0)),
                  pl.BlockSpec((2, width, NSA_KV_HEADS * hid), lambda b: (0, 0, 0)),
                  pl.BlockSpec((2, width, NSA_KV_HEADS * hid), lambda b: (0, 0, 0)),
                  pl.BlockSpec((NSA_KV_HEADS * hid, NSA_KV_HEADS * HEAD_DIM), lambda b: (0, 0)),
                  pl.BlockSpec((NSA_KV_HEADS * hid, NSA_KV_HEADS * 128), lambda b: (0, 0)),
                  pl.BlockSpec((1, n_chunk, 1), lambda b: (b, 0, 0)),
                  pl.BlockSpec((1, 128), lambda b: (0, 0))],
        out_specs=[pl.BlockSpec((1, NSA_KV_HEADS, HEAD_DIM, n_chunk), lambda b: (b, 0, 0, 0)),
                   pl.BlockSpec((1, NSA_KV_HEADS, n_chunk, 128), lambda b: (b, 0, 0, 0))],
        out_shape=[jax.ShapeDtypeStruct((bsz, NSA_KV_HEADS, HEAD_DIM, n_chunk), BF16),
                   jax.ShapeDtypeStruct((bsz, NSA_KV_HEADS, n_chunk, 128), BF16)],
        compiler_params=pltpu.CompilerParams(dimension_semantics=("arbitrary",),
                                             vmem_limit_bytes=V7X_VMEM_LIMIT_BYTES),
        name="compress_kv",
    )(kc_raw.reshape(bsz, n_chunk, width), vc_raw.reshape(bsz, n_chunk, width), chunk_pe(cmp_pos_k),
      chunk_pe(cmp_pos_v), chunk_w1(w_k1), chunk_w1(w_v1), w2k, w2v, cmp_pos, _inv_freq_row())


def _out_ln_kernel(y_ref, a_ref, x_ref, wglu_ref, wout_ref, g_ref, b_ref, o_ref):
    y = y_ref[...]
    y_ssm = y * jax.nn.sigmoid(jnp.dot(y.astype(BF16), wglu_ref[...], preferred_element_type=F32))
    mix = (jnp.dot(y_ssm.astype(BF16), wout_ref[:SSM_WIDTH, :], preferred_element_type=F32)
           + jnp.dot(a_ref[...].astype(BF16), wout_ref[SSM_WIDTH:, :], preferred_element_type=F32))
    o_ref[...] = _layer_norm(DEEPNORM_ALPHA * x_ref[...] + mix, g_ref[...], b_ref[...])


def out_proj_ln(y_s5, y_nsa, xt, w_glu, w_out, ln_g, ln_b, tm=256):
    n_tok = xt.shape[0]
    row = lambda i: (i, 0)
    const = lambda i: (0, 0)
    return pl.pallas_call(
        _out_ln_kernel,
        grid=(n_tok // tm,),
        in_specs=[pl.BlockSpec((tm, SSM_WIDTH), row), pl.BlockSpec((tm, NSA_WIDTH), row),
                  pl.BlockSpec((tm, D_MODEL), row), pl.BlockSpec((SSM_WIDTH, SSM_WIDTH), const),
                  pl.BlockSpec((D_MODEL, D_MODEL), const), pl.BlockSpec((1, D_MODEL), const),
                  pl.BlockSpec((1, D_MODEL), const)],
        out_specs=pl.BlockSpec((tm, D_MODEL), row),
        out_shape=jax.ShapeDtypeStruct((n_tok, D_MODEL), F32),
        compiler_params=pltpu.CompilerParams(dimension_semantics=("arbitrary",),
                                             vmem_limit_bytes=V7X_VMEM_LIMIT_BYTES),
        name="out_proj_ln",
    )(y_s5, y_nsa, xt, w_glu.astype(BF16), w_out.astype(BF16), ln_g.reshape(1, D_MODEL), ln_b.reshape(1, D_MODEL))


S5_CHUNK = 512
S5_SUB = S5_CHUNK // 8
S5_GROUPS_PER_BLOCK = 8
S5_STATES = S5_GROUPS_PER_BLOCK * SSM_STATE


def _cmul_add(ar, ai, xr, xi, br, bi):
    return ar * xr - ai * xi + br, ar * xi + ai * xr + bi


def _s5_kernel(u_ref, lam_ref, bmat_ref, cmat_ref, d_ref, perm_ref, permt_ref, o_ref,
               xr_scr, xi_scr, pr_scr, pi_scr, carry_scr, a_scr, bbar_scr):
    c = pl.program_id(2)

    @pl.when(c == 0)
    def _():
        lr, li = lam_ref[0, 0:1, :], lam_ref[0, 1:2, :]
        dt = jnp.exp(lam_ref[0, 2:3, :])
        mag = jnp.exp(lr * dt)
        ar, ai = mag * jnp.cos(li * dt), mag * jnp.sin(li * dt)
        zr, zi = ar - 1.0, ai
        den = lr * lr + li * li
        fr, fi = (zr * lr + zi * li) / den, (zi * lr - zr * li) / den
        a_scr[0:1, :] = ar
        a_scr[1:2, :] = ai
        b_re, b_im = bmat_ref[0, 0], bmat_ref[0, 1]
        bbar_scr[0] = (fr * b_re - fi * b_im).astype(BF16)
        bbar_scr[1] = (fr * b_im + fi * b_re).astype(BF16)
        carry_scr[...] = jnp.zeros(carry_scr.shape, F32)
        a_re0 = jnp.broadcast_to(ar, (8, S5_STATES))
        a_im0 = jnp.broadcast_to(ai, (8, S5_STATES))

        def pw_body(i, pw):
            pr, pi = pw
            pr_scr[i] = pr
            pi_scr[i] = pi
            return a_re0 * pr - a_im0 * pi, a_re0 * pi + a_im0 * pr

        lax.fori_loop(0, S5_SUB, pw_body, (a_re0, a_im0))

    a_re = jnp.broadcast_to(a_scr[0:1, :], (8, S5_STATES))
    a_im = jnp.broadcast_to(a_scr[1:2, :], (8, S5_STATES))
    u = u_ref[0]
    perm = perm_ref[...]
    u_p = jnp.dot(perm, u.astype(BF16), preferred_element_type=F32).astype(BF16)
    xr_scr[...] = jnp.dot(u_p, bbar_scr[0], preferred_element_type=F32)
    xi_scr[...] = jnp.dot(u_p, bbar_scr[1], preferred_element_type=F32)

    def scan_body(i, x):
        row = pl.multiple_of(i * 8, 8)
        xr, xi = _cmul_add(a_re, a_im, x[0], x[1], xr_scr[pl.ds(row, 8), :], xi_scr[pl.ds(row, 8), :])
        xr_scr[pl.ds(row, 8), :] = xr
        xi_scr[pl.ds(row, 8), :] = xi
        return xr, xi

    zero = jnp.zeros((8, S5_STATES), F32)
    er, ei = lax.fori_loop(0, S5_SUB, scan_body, (zero, zero), unroll=4)

    ar_s = pr_scr[S5_SUB - 1][0:1]
    ai_s = pi_scr[S5_SUB - 1][0:1]
    rows_r = [carry_scr[0:1, :]]
    rows_i = [carry_scr[1:2, :]]
    for j in range(8):
        nr, ni = _cmul_add(ar_s, ai_s, rows_r[-1], rows_i[-1], er[j:j + 1], ei[j:j + 1])
        rows_r.append(nr)
        rows_i.append(ni)
    carry_scr[0:1, :] = rows_r[8]
    carry_scr[1:2, :] = rows_i[8]
    cr = jnp.concatenate(rows_r[:8], axis=0)
    ci = jnp.concatenate(rows_i[:8], axis=0)

    def fix_body(i, carry):
        row = pl.multiple_of(i * 8, 8)
        xr, xi = _cmul_add(pr_scr[i], pi_scr[i], cr, ci, xr_scr[pl.ds(row, 8), :], xi_scr[pl.ds(row, 8), :])
        xr_scr[pl.ds(row, 8), :] = xr
        xi_scr[pl.ds(row, 8), :] = xi
        return carry

    lax.fori_loop(0, S5_SUB, fix_body, 0, unroll=4)

    y_p = (jnp.dot(xr_scr[...].astype(BF16), cmat_ref[0, 0], preferred_element_type=F32)
           - jnp.dot(xi_scr[...].astype(BF16), cmat_ref[0, 1], preferred_element_type=F32))
    y_hi = y_p.astype(BF16)
    y_lo = (y_p - y_hi.astype(F32)).astype(BF16)
    perm_t = permt_ref[...]
    y = jnp.dot(perm_t, y_hi, preferred_element_type=F32) + jnp.dot(perm_t, y_lo, preferred_element_type=F32)
    o_ref[0] = jax.nn.gelu(y + d_ref[0] * u)


def s5_scan(u, lam_re, lam_im, log_dt, b_re, b_im, c_re, c_im, d_skip):
    bsz, seq_len, _ = u.shape
    nb = SSM_GROUPS // S5_GROUPS_PER_BLOCK
    eye = jnp.eye(S5_GROUPS_PER_BLOCK, dtype=F32)

    def blockdiag_b(m):
        m = jnp.swapaxes(m, 1, 2).reshape(nb, S5_GROUPS_PER_BLOCK, SSM_CH_PER_GROUP, SSM_STATE)
        return jnp.einsum('nghp,gk->nghkp', m, eye).reshape(nb, 128, S5_STATES)

    def blockdiag_c(m):
        m = jnp.swapaxes(m, 1, 2).reshape(nb, S5_GROUPS_PER_BLOCK, SSM_STATE, SSM_CH_PER_GROUP)
        return jnp.einsum('ngph,gk->ngpkh', m, eye).reshape(nb, S5_STATES, 128)

    log_dt_states = jnp.broadcast_to(log_dt[:, None], lam_re.shape)
    lam = jnp.stack([m.reshape(nb, S5_STATES) for m in (lam_re, lam_im, log_dt_states)], axis=1)
    bmat = jnp.stack([blockdiag_b(b_re), blockdiag_b(b_im)], axis=1)
    cmat = jnp.stack([blockdiag_c(c_re), blockdiag_c(c_im)], axis=1).astype(BF16)
    d = d_skip.reshape(nb, 1, 128)
    r = np.arange(S5_CHUNK)
    perm = np.zeros((S5_CHUNK, S5_CHUNK), np.float32)
    perm[r, (r % 8) * S5_SUB + r // 8] = 1.0
    perm = jnp.asarray(perm, BF16)
    return pl.pallas_call(
        _s5_kernel,
        grid=(bsz, nb, seq_len // S5_CHUNK),
        in_specs=[
            pl.BlockSpec((1, S5_CHUNK, 128), lambda b, g, c: (b, c, g)),
            pl.BlockSpec((1, 3, S5_STATES), lambda b, g, c: (g, 0, 0)),
            pl.BlockSpec((1, 2, 128, S5_STATES), lambda b, g, c: (g, 0, 0, 0)),
            pl.BlockSpec((1, 2, S5_STATES, 128), lambda b, g, c: (g, 0, 0, 0)),
            pl.BlockSpec((1, 1, 128), lambda b, g, c: (g, 0, 0)),
            pl.BlockSpec((S5_CHUNK, S5_CHUNK), lambda b, g, c: (0, 0)),
            pl.BlockSpec((S5_CHUNK, S5_CHUNK), lambda b, g, c: (0, 0)),
        ],
        out_specs=pl.BlockSpec((1, S5_CHUNK, 128), lambda b, g, c: (b, c, g)),
        out_shape=jax.ShapeDtypeStruct((bsz, seq_len, SSM_WIDTH), F32),
        scratch_shapes=[pltpu.VMEM((S5_CHUNK, S5_STATES), F32), pltpu.VMEM((S5_CHUNK, S5_STATES), F32),
                        pltpu.VMEM((S5_SUB, 8, S5_STATES), F32), pltpu.VMEM((S5_SUB, 8, S5_STATES), F32),
                        pltpu.VMEM((2, S5_STATES), F32), pltpu.VMEM((2, S5_STATES), F32),
                        pltpu.VMEM((2, 128, S5_STATES), BF16)],
        compiler_params=pltpu.CompilerParams(
            dimension_semantics=("arbitrary", "arbitrary", "arbitrary"), vmem_limit_bytes=V7X_VMEM_LIMIT_BYTES),
        name="s5_scan",
    )(u, lam, bmat, cmat, d, perm, perm.T)


def _softmax_tile(s, m_old):
    m_new = jnp.maximum(m_old, jnp.max(s, axis=1, keepdims=True))
    m_wide = jnp.concatenate([m_new] * (s.shape[1] // 128), axis=1)
    return m_new, jnp.exp2(m_old - m_new), jnp.exp2(s - m_wide)


def _lane_is_low(shape):
    return lax.broadcasted_iota(jnp.int32, shape, 1) < HEAD_DIM


def _pad_kt(kt, variant):
    z = jnp.zeros_like(kt)
    return jnp.concatenate([kt, z] if variant == 0 else [z, kt], axis=0)


def _pad_v(vv, variant):
    low = _lane_is_low(vv.shape)
    keep = low if variant == 0 else jnp.logical_not(low)
    return jnp.where(keep, vv, jnp.ones_like(vv))


def _finish(acc, variant):
    lane = lax.broadcasted_iota(jnp.int32, acc.shape, 1)
    lsel = lane == (HEAD_DIM if variant == 0 else 0)
    l = jnp.sum(jnp.where(lsel, acc, 0.0), axis=1, keepdims=True)
    keep = (lane < HEAD_DIM) if variant == 0 else (lane >= HEAD_DIM)
    return jnp.where(keep, acc / l, 0.0)


def _nsa_kernel(q_ref, g_ref, kct_ref, vc_ref, kst_ref, vs_ref, kwt_ref, vw_ref, ovl_ref, gx_ref, o_ref,
                m_scr, acc_scr, s_scr_a, s_scr_b, *, seq_len):
    s_slots = (s_scr_a, s_scr_b)
    n_sel = seq_len // SEL_BLOCK
    n_cpad = seq_len // CMP_STRIDE
    sel_tile = 512
    blocks_per_tile = sel_tile // SEL_BLOCK
    win_tiles = WINDOW // Q_BLOCK + 1
    n_pair = Q_PER_KV // 2
    rows = n_pair * Q_BLOCK
    i = pl.program_id(2)
    t0 = i * Q_BLOCK

    qb = q_ref[0]
    qst = jnp.concatenate([qb[:, p * 128:(p + 1) * 128] for p in range(n_pair)], axis=0)

    sig = jax.nn.sigmoid(g_ref[0])
    sig_hi = sig.astype(BF16)
    sig_lo = (sig - sig_hi.astype(F32)).astype(BF16)
    gx = gx_ref[0]
    gexp = (jnp.dot(sig_hi, gx, preferred_element_type=F32) + jnp.dot(sig_lo, gx, preferred_element_type=F32))

    def gate_of(branch):
        base = branch * n_pair * 128
        return jnp.concatenate([gexp[:, base + p * 128: base + (p + 1) * 128] for p in range(n_pair)], axis=0)

    t_row = t0 + lax.broadcasted_iota(jnp.int32, (Q_BLOCK, 1), 0)

    n_iota = lax.broadcasted_iota(jnp.int32, (Q_BLOCK, n_cpad), 1)
    cmask = (n_iota * CMP_STRIDE + (CMP_BLOCK - 1)) <= t_row
    cmask4 = jnp.concatenate([cmask] * n_pair, axis=0)
    kct = kct_ref[0, 0]
    vcd = vc_ref[0, 0]
    p_sum = jnp.zeros((Q_BLOCK, n_cpad), F32)
    out = jnp.zeros((rows, 128), F32)
    o_c = jnp.zeros((rows, 128), F32)
    for v in range(2):
        s = jnp.dot(qst, _pad_kt(kct, v), preferred_element_type=F32)
        s = jnp.where(cmask4, s, NEG)
        m = jnp.max(s, axis=1, keepdims=True)
        e = jnp.where(cmask4, jnp.exp2(s - m), 0.0)
        l = jnp.sum(e, axis=1, keepdims=True)
        p = e * (1.0 / jnp.maximum(l, 1e-30))
        for pp in range(n_pair):
            p_sum = p_sum + p[pp * Q_BLOCK:(pp + 1) * Q_BLOCK]
        low = _lane_is_low((n_cpad, 128))
        vz = jnp.where(low if v == 0 else jnp.logical_not(low), vcd, jnp.zeros_like(vcd))
        o_c = o_c + jnp.dot(p.astype(BF16), vz, preferred_element_type=F32)
    out = out + gate_of(0) * o_c

    n_kblk = seq_len // Q_BLOCK
    w0 = jnp.clip(i - (win_tiles - 1), 0, n_kblk - win_tiles)
    kw = jnp.concatenate([kwt_ref[0, 0, w0 + j] for j in range(win_tiles)], axis=1)
    vw = jnp.concatenate([vw_ref[0, 0, w0 + j] for j in range(win_tiles)], axis=0)
    kpos_w = w0 * Q_BLOCK + lax.broadcasted_iota(jnp.int32, (Q_BLOCK, win_tiles * Q_BLOCK), 1)
    diff = t_row - kpos_w
    wbias = jnp.where((diff >= 0) & (diff < WINDOW), 0.0, NEG)
    wbias4 = jnp.concatenate([wbias] * n_pair, axis=0)
    o_w = jnp.zeros((rows, 128), F32)
    for v in range(2):
        s = jnp.dot(qst, _pad_kt(kw, v), preferred_element_type=F32) + wbias4
        m = jnp.max(s, axis=1, keepdims=True)
        p = jnp.exp2(s - m)
        o_w = o_w + _finish(jnp.dot(p.astype(BF16), _pad_v(vw, v), preferred_element_type=F32), v)
    out = out + gate_of(2) * o_w

    ps_hi = p_sum.astype(BF16)
    ps_lo = (p_sum - ps_hi.astype(F32)).astype(BF16)
    ovl = ovl_ref[...]
    nt = (((1,), (1,)), ((), ()))
    imp_t = (lax.dot_general(ovl, ps_hi, nt, preferred_element_type=F32)
             + lax.dot_general(ovl, ps_lo, nt, preferred_element_type=F32))
    s_iota = lax.broadcasted_iota(jnp.int32, (n_sel, Q_BLOCK), 0)
    t_lane = t0 + lax.broadcasted_iota(jnp.int32, (n_sel, Q_BLOCK), 1)
    cur = t_lane // SEL_BLOCK
    forced = (s_iota == 0) | (s_iota == cur) | (s_iota == cur - 1)
    valid = s_iota * SEL_BLOCK <= t_lane
    score = jnp.where(forced, FORCE, jnp.where(valid, imp_t, -1.0))
    s_f = s_iota.astype(F32)
    sel_t = jnp.zeros((n_sel, Q_BLOCK), F32)
    for _ in range(min(SEL_TOPK, n_sel)):
        mx = jnp.max(score, axis=0, keepdims=True)
        idx = jnp.min(jnp.where(score == mx, s_f, float(n_sel)), axis=0, keepdims=True)
        hit = s_f == idx
        sel_t = jnp.where(hit, 1.0, sel_t)
        score = jnp.where(hit, -3e38, score)
    selmask = sel_t.T.astype(BF16)

    m_scr[...] = jnp.full(m_scr.shape, NEG, F32)
    acc_scr[...] = jnp.zeros(acc_scr.shape, F32)
    n_tiles = (t0 + Q_BLOCK + sel_tile - 1) // sel_tile

    last_tile = seq_len // sel_tile - 1

    def scores_into(slot, kt):
        kt_tile = kst_ref[0, 0, jnp.minimum(kt, last_tile)]
        for v in range(2):
            s_slots[slot][v] = jnp.dot(qst, _pad_kt(kt_tile, v), preferred_element_type=F32)

    def bias_of(kt):
        blk = kt * blocks_per_tile + lax.broadcasted_iota(jnp.int32, (n_sel, sel_tile), 1) // SEL_BLOCK
        expand = (lax.broadcasted_iota(jnp.int32, (n_sel, sel_tile), 0) == blk).astype(BF16)
        selexp = jnp.dot(selmask, expand, preferred_element_type=F32)
        kpos = kt * sel_tile + lax.broadcasted_iota(jnp.int32, (Q_BLOCK, sel_tile), 1)
        bias = jnp.where((selexp > 0.5) & (kpos <= t_row), 0.0, NEG)
        return jnp.concatenate([bias] * n_pair, axis=0)

    def attend_from(slot, kt, bias4):
        v_tile = vs_ref[0, 0, jnp.minimum(kt, last_tile)]
        for v in range(2):
            m_new, alpha, p = _softmax_tile(s_slots[slot][v] + bias4, m_scr[v])
            m_scr[v] = m_new
            acc_scr[v] = alpha * acc_scr[v] + jnp.dot(p.astype(BF16), _pad_v(v_tile, v), preferred_element_type=F32)

    scores_into(0, 0)

    def sel_body(j, carry):
        kt = 2 * j
        bias_a = bias_of(kt)
        scores_into(1, kt + 1)
        attend_from(0, kt, bias_a)
        bias_b = bias_of(kt + 1)
        scores_into(0, kt + 2)
        attend_from(1, kt + 1, bias_b)
        return carry

    lax.fori_loop(0, (n_tiles + 1) // 2, sel_body, 0)
    out = out + gate_of(1) * (_finish(acc_scr[0], 0) + _finish(acc_scr[1], 1))

    o_ref[0] = jnp.concatenate([out[p * Q_BLOCK:(p + 1) * Q_BLOCK] for p in range(n_pair)], axis=1)


def nsa_attention(q, gate_pad, kct, vc, kst, vs, kwt, vw):
    bsz, seq_len, _ = q.shape
    n_sel = seq_len // SEL_BLOCK
    n_cpad = seq_len // CMP_STRIDE
    n_cmp = (seq_len - CMP_BLOCK) // CMP_STRIDE + 1
    n_pair = Q_PER_KV // 2
    cs = np.arange(n_cpad) * CMP_STRIDE
    ce = cs + CMP_BLOCK - 1
    ss = np.arange(n_sel) * SEL_BLOCK
    se = ss + SEL_BLOCK - 1
    ovl = (cs[None, :] <= se[:, None]) & (ce[None, :] >= ss[:, None]) & (np.arange(n_cpad)[None, :] < n_cmp)
    ovl = jnp.asarray(ovl.astype(np.float32), BF16)
    gx = np.zeros((NSA_KV_HEADS, 128, N_BRANCH * n_pair * 128), np.float32)
    for k in range(NSA_KV_HEADS):
        for hl in range(Q_PER_KV):
            for br in range(N_BRANCH):
                c0 = br * n_pair * 128 + hl * HEAD_DIM
                gx[k, (k * Q_PER_KV + hl) * N_BRANCH + br, c0:c0 + HEAD_DIM] = 1.0
    gx = jnp.asarray(gx, BF16)
    width = Q_PER_KV * HEAD_DIM
    full = lambda *shape: pl.BlockSpec((1, 1) + shape, lambda b, k, i: (b, k) + (0,) * len(shape))
    return pl.pallas_call(
        functools.partial(_nsa_kernel, seq_len=seq_len),
        grid=(bsz, NSA_KV_HEADS, seq_len // Q_BLOCK),
        in_specs=[
            pl.BlockSpec((1, Q_BLOCK, width), lambda b, k, i: (b, i, k)),
            pl.BlockSpec((1, Q_BLOCK, 128), lambda b, k, i: (b, i, 0)),
            full(HEAD_DIM, n_cpad), full(n_cpad, 128),
            full(seq_len // 512, HEAD_DIM, 512), full(seq_len // 512, 512, 128),
            full(seq_len // Q_BLOCK, HEAD_DIM, Q_BLOCK), full(seq_len // Q_BLOCK, Q_BLOCK, 128),
            pl.BlockSpec((n_sel, n_cpad), lambda b, k, i: (0, 0)),
            pl.BlockSpec((1, 128, N_BRANCH * n_pair * 128), lambda b, k, i: (k, 0, 0)),
        ],
        out_specs=pl.BlockSpec((1, Q_BLOCK, width), lambda b, k, i: (b, i, k)),
        out_shape=jax.ShapeDtypeStruct((bsz, seq_len, NSA_WIDTH), F32),
        scratch_shapes=[pltpu.VMEM((2, n_pair * Q_BLOCK, 128), F32), pltpu.VMEM((2, n_pair * Q_BLOCK, 128), F32),
                        pltpu.VMEM((2, n_pair * Q_BLOCK, 512), F32), pltpu.VMEM((2, n_pair * Q_BLOCK, 512), F32)],
        compiler_params=pltpu.CompilerParams(
            dimension_semantics=("arbitrary", "arbitrary", "arbitrary"), vmem_limit_bytes=V7X_VMEM_LIMIT_BYTES),
        name="nsa_attention",
    )(q, gate_pad, kct, vc, kst, vs, kwt, vw, ovl, gx)


ROUTER_TILE = 512
MOE_TILE = 1024
MOE_SUB = 256
MOE_ROWS = 48
MOE_SLOT = 64
MOE_GROUP = 4


def _first_max_mask(x, idx_f, axis):
    mx = jnp.max(x, axis=axis, keepdims=True)
    first = jnp.min(jnp.where(x == mx, idx_f, 1e9), axis=axis, keepdims=True)
    return idx_f == first, mx


def _router_kernel(x_ref, wrt_ref, bias_ref, w_ref, sel_ref):
    per_group = N_EXPERTS // N_EXPERT_GROUPS
    tr = x_ref.shape[0]
    nt = (((1,), (1,)), ((), ()))
    logits = lax.dot_general(wrt_ref[...], x_ref[...].astype(BF16), nt, preferred_element_type=F32)
    aff = jax.nn.sigmoid(logits)
    biased = aff + bias_ref[...]
    grp = biased.reshape(N_EXPERT_GROUPS, per_group, tr)
    in_grp = lax.broadcasted_iota(jnp.int32, grp.shape, 1).astype(F32)
    hit1, m1 = _first_max_mask(grp, in_grp, 1)
    m2 = jnp.max(jnp.where(hit1, -jnp.inf, grp), axis=1, keepdims=True)
    gscore = (m1 + m2).reshape(N_EXPERT_GROUPS, tr)
    g_idx = lax.broadcasted_iota(jnp.int32, gscore.shape, 0).astype(F32)
    gsel = jnp.zeros(gscore.shape, F32)
    for _ in range(TOPK_GROUPS):
        hit, _ = _first_max_mask(gscore, g_idx, 0)
        gsel = jnp.where(hit, 1.0, gsel)
        gscore = jnp.where(hit, -jnp.inf, gscore)
    gmask = jnp.broadcast_to(gsel.reshape(N_EXPERT_GROUPS, 1, tr), grp.shape).reshape(N_EXPERTS, tr)
    cand = jnp.where(gmask > 0.5, biased, NEG)
    e_idx = lax.broadcasted_iota(jnp.int32, cand.shape, 0).astype(F32)
    sel = jnp.zeros(cand.shape, F32)
    for _ in range(TOP_K):
        hit, _ = _first_max_mask(cand, e_idx, 0)
        sel = jnp.where(hit, 1.0, sel)
        cand = jnp.where(hit, -jnp.inf, cand)
    w = jnp.where(sel > 0.5, aff, 0.0)
    w_ref[...] = w / jnp.sum(w, axis=0, keepdims=True) * ROUTED_SCALE
    sel_ref[...] = sel


def moe_router(xt, w_router, router_bias):
    n_tok = xt.shape[0]
    wrt = w_router.T.astype(BF16)
    return pl.pallas_call(
        _router_kernel,
        grid=(n_tok // ROUTER_TILE,),
        in_specs=[pl.BlockSpec((ROUTER_TILE, D_MODEL), lambda i: (i, 0)),
                  pl.BlockSpec((N_EXPERTS, D_MODEL), lambda i: (0, 0)),
                  pl.BlockSpec((N_EXPERTS, 1), lambda i: (0, 0))],
        out_specs=[pl.BlockSpec((N_EXPERTS, ROUTER_TILE), lambda i: (0, i)),
                   pl.BlockSpec((N_EXPERTS, ROUTER_TILE), lambda i: (0, i))],
        out_shape=[jax.ShapeDtypeStruct((N_EXPERTS, n_tok), F32), jax.ShapeDtypeStruct((N_EXPERTS, n_tok), F32)],
        compiler_params=pltpu.CompilerParams(dimension_semantics=("arbitrary",),
                                             vmem_limit_bytes=V7X_VMEM_LIMIT_BYTES),
        name="moe_router",
    )(xt, wrt, router_bias.reshape(N_EXPERTS, 1))


def _moe_kernel(cnt_ref, x_ref, sel_ref, w_ref, init_ref, wg_ref, wu_ref, wd_ref, lng_ref, lnb_ref, o_ref,
                rank_scr, ybuf_scr, sbuf_scr):
    i = pl.program_id(0)
    e = pl.program_id(1)
    tm = x_ref.shape[0]
    n_sub = tm // MOE_SUB
    tn = (((0,), (0,)), ((), ()))

    @pl.when(e == 0)
    def _():
        o_ref[...] = init_ref[...]
        before = (lax.broadcasted_iota(jnp.int32, (MOE_SUB, MOE_SUB), 0)
                  < lax.broadcasted_iota(jnp.int32, (MOE_SUB, MOE_SUB), 1))
        before = jnp.where(before, 1.0, 0.0).astype(BF16)
        for q in range(n_sub):
            cols = slice(q * MOE_SUB, (q + 1) * MOE_SUB)
            rank_scr[:, cols] = jnp.dot(sel_ref[:, cols].astype(BF16), before, preferred_element_type=F32)

    count = cnt_ref[i * N_EXPERTS + e]
    sel_e = sel_ref[pl.ds(e, 1), :]
    rank_e = rank_scr[pl.ds(e, 1), :]
    w_e = w_ref[pl.ds(e, 1), :]

    def expert_pass(c):
        row = (c * MOE_ROWS + lax.broadcasted_iota(jnp.int32, (MOE_ROWS, MOE_SUB), 0)).astype(F32)
        scatters, xes = [], []
        for q in range(n_sub):
            cols = slice(q * MOE_SUB, (q + 1) * MOE_SUB)
            hit = (rank_e[:, cols] == row) & (sel_e[:, cols] > 0.5)
            gather = jnp.where(hit, 1.0, 0.0).astype(BF16)
            scatters.append(jnp.where(hit, w_e[:, cols], 0.0).astype(BF16))
            xes.append(jnp.dot(gather, x_ref[cols, :], preferred_element_type=F32).astype(BF16))
        xe = jnp.concatenate(xes, axis=0)
        g = jnp.dot(xe, wg_ref[0], preferred_element_type=F32)
        u = jnp.dot(xe, wu_ref[0], preferred_element_type=F32)
        h = (jax.nn.silu(g) * u).astype(BF16)
        return scatters, jnp.dot(h, wd_ref[0], preferred_element_type=F32).astype(BF16)

    slot = e % MOE_GROUP
    scatters, y = expert_pass(0)
    spare = MOE_SLOT - MOE_ROWS
    for q in range(n_sub):
        sbuf_scr[q, slot] = jnp.concatenate([scatters[q], jnp.zeros((spare, MOE_SUB), BF16)], axis=0)
        ybuf_scr[q, slot] = jnp.concatenate(
            [y[q * MOE_ROWS:(q + 1) * MOE_ROWS], jnp.zeros((spare, D_MODEL), BF16)], axis=0)

    @pl.when(slot == MOE_GROUP - 1)
    def _():
        for q in range(n_sub):
            cols = slice(q * MOE_SUB, (q + 1) * MOE_SUB)
            o_ref[cols, :] += lax.dot_general(sbuf_scr[q].reshape(MOE_GROUP * MOE_SLOT, MOE_SUB),
                                              ybuf_scr[q].reshape(MOE_GROUP * MOE_SLOT, D_MODEL), tn,
                                              preferred_element_type=F32)

    def overflow_body(c, carry):
        sc, yy = expert_pass(c)
        for q in range(n_sub):
            cols = slice(q * MOE_SUB, (q + 1) * MOE_SUB)
            o_ref[cols, :] += lax.dot_general(sc[q], yy[q * MOE_ROWS:(q + 1) * MOE_ROWS], tn,
                                              preferred_element_type=F32)
        return carry

    lax.fori_loop(1, (count + MOE_ROWS - 1) // MOE_ROWS, overflow_body, 0)

    @pl.when(e == N_EXPERTS - 1)
    def _():
        o_ref[...] = _layer_norm(o_ref[...], lng_ref[...], lnb_ref[...])


def moe_routed(x_bf16, sel_t, w_t, init, w_gate, w_up, w_down, ln_g, ln_b):
    n_tok = x_bf16.shape[0]
    n_tiles = n_tok // MOE_TILE
    per_sub = jnp.sum(sel_t.reshape(N_EXPERTS, n_tiles, MOE_TILE // MOE_SUB, MOE_SUB), axis=-1)
    cnt = jnp.max(per_sub, axis=-1).T.astype(jnp.int32).reshape(-1)
    grid_spec = pltpu.PrefetchScalarGridSpec(
        num_scalar_prefetch=1,
        grid=(n_tiles, N_EXPERTS),
        in_specs=[
            pl.BlockSpec((MOE_TILE, D_MODEL), lambda i, e, cnt: (i, 0)),
            pl.BlockSpec((N_EXPERTS, MOE_TILE), lambda i, e, cnt: (0, i)),
            pl.BlockSpec((N_EXPERTS, MOE_TILE), lambda i, e, cnt: (0, i)),
            pl.BlockSpec((MOE_TILE, D_MODEL), lambda i, e, cnt: (i, 0), pipeline_mode=pl.Buffered(1)),
            pl.BlockSpec((1, D_MODEL, EXPERT_FF), lambda i, e, cnt: (e, 0, 0)),
            pl.BlockSpec((1, D_MODEL, EXPERT_FF), lambda i, e, cnt: (e, 0, 0)),
            pl.BlockSpec((1, EXPERT_FF, D_MODEL), lambda i, e, cnt: (e, 0, 0)),
            pl.BlockSpec((1, D_MODEL), lambda i, e, cnt: (0, 0)),
            pl.BlockSpec((1, D_MODEL), lambda i, e, cnt: (0, 0)),
        ],
        out_specs=pl.BlockSpec((MOE_TILE, D_MODEL), lambda i, e, cnt: (i, 0)),
        scratch_shapes=[pltpu.VMEM((N_EXPERTS, MOE_TILE), F32),
                        pltpu.VMEM((MOE_TILE // MOE_SUB, MOE_GROUP, MOE_SLOT, D_MODEL), BF16),
                        pltpu.VMEM((MOE_TILE // MOE_SUB, MOE_GROUP, MOE_SLOT, MOE_SUB), BF16)],
    )
    return pl.pallas_call(
        _moe_kernel,
        grid_spec=grid_spec,
        out_shape=jax.ShapeDtypeStruct((n_tok, D_MODEL), F32),
        compiler_params=pltpu.CompilerParams(dimension_semantics=("arbitrary", "arbitrary"),
                                             vmem_limit_bytes=V7X_VMEM_LIMIT_BYTES),
        name="moe_routed",
    )(cnt, x_bf16, sel_t, w_t, init, w_gate, w_up, w_down, ln_g.reshape(1, D_MODEL), ln_b.reshape(1, D_MODEL))


def _shared_ffn_kernel(x_ref, wg_ref, wu_ref, wd_ref, o_ref, xb_ref):
    x = x_ref[...]
    xb = x.astype(BF16)
    h = jax.nn.silu(jnp.dot(xb, wg_ref[...], preferred_element_type=F32)) * jnp.dot(
        xb, wu_ref[...], preferred_element_type=F32)
    o_ref[...] = DEEPNORM_ALPHA * x + jnp.dot(h.astype(BF16), wd_ref[...], preferred_element_type=F32)
    xb_ref[...] = xb


def shared_ffn(xt, wg, wu, wd, tm=512):
    n_tok, d = xt.shape
    ff = wg.shape[1]
    return pl.pallas_call(
        _shared_ffn_kernel,
        grid=(n_tok // tm,),
        in_specs=[pl.BlockSpec((tm, d), lambda i: (i, 0)), pl.BlockSpec((d, ff), lambda i: (0, 0)),
                  pl.BlockSpec((d, ff), lambda i: (0, 0)), pl.BlockSpec((ff, d), lambda i: (0, 0))],
        out_specs=[pl.BlockSpec((tm, d), lambda i: (i, 0)), pl.BlockSpec((tm, d), lambda i: (i, 0))],
        out_shape=[jax.ShapeDtypeStruct((n_tok, d), F32), jax.ShapeDtypeStruct((n_tok, d), BF16)],
        compiler_params=pltpu.CompilerParams(dimension_semantics=("arbitrary",),
                                             vmem_limit_bytes=V7X_VMEM_LIMIT_BYTES),
        name="shared_ffn",
    )(xt, wg.astype(BF16), wu.astype(BF16), wd.astype(BF16))


def hybrid_layer(x, positions, w_in, lam_re, lam_im, log_dt, ssm_b_re, ssm_b_im, ssm_c_re, ssm_c_im, ssm_d,
                 w_glu, cmp_pos_k, cmp_pos_v, w_cmp_k1, w_cmp_k2, w_cmp_v1, w_cmp_v2, w_out, ln1_g, ln1_b,
                 w_router, router_bias, w_gate, w_up, w_down, ws_gate, ws_up, ws_down, ln2_g, ln2_b):
    bsz, L, _ = x.shape
    sizes = [SSM_WIDTH, NSA_WIDTH] + [KV_WIDTH] * 6 + [NSA_HEADS * N_BRANCH]
    o = [0] + [int(v) for v in np.cumsum(sizes)]
    col = lambda j: w_in[:, o[j]:o[j + 1]]
    dup = lambda w: jnp.concatenate([w[:, h * HEAD_DIM:(h + 1) * HEAD_DIM] for h in (0, 0, 1, 1)], axis=1)
    gate_cols = jnp.pad(col(8), ((0, 0), (0, 128 - NSA_HEADS * N_BRANCH)))
    w_uq = w_in[:, :o[2]].astype(BF16)
    w_kv = jnp.concatenate([col(4), col(6), dup(col(5)), dup(col(7)), col(2), col(3), gate_cols], axis=1).astype(BF16)

    xt = x.reshape(bsz * L, D_MODEL)
    u, q = proj_uq(xt, w_uq, positions.reshape(bsz * L, 1))
    kst, kwt, vs, vw, kc_raw, vc_raw, gate_pad = proj_kv(x, w_kv, positions.reshape(bsz, L, 1))
    kct, vcd = compress_kv(kc_raw, vc_raw, positions, cmp_pos_k, cmp_pos_v, w_cmp_k1, w_cmp_k2, w_cmp_v1, w_cmp_v2)
    y_s5 = s5_scan(u.reshape(bsz, L, SSM_WIDTH), lam_re, lam_im, log_dt, ssm_b_re, ssm_b_im, ssm_c_re, ssm_c_im, ssm_d)
    vw = vw.reshape(bsz, NSA_KV_HEADS, L // Q_BLOCK, Q_BLOCK, 128)
    y_nsa = nsa_attention(q.reshape(bsz, L, NSA_WIDTH), gate_pad, kct, vcd, kst, vs, kwt, vw)
    x1 = out_proj_ln(y_s5.reshape(bsz * L, SSM_WIDTH), y_nsa.reshape(bsz * L, NSA_WIDTH), xt, w_glu, w_out,
                     ln1_g, ln1_b)
    w_t, sel_t = moe_router(x1, w_router, router_bias)
    acc0, x1b = shared_ffn(x1, ws_gate, ws_up, ws_down)
    out = moe_routed(x1b, sel_t, w_t, acc0, w_gate.astype(BF16), w_up.astype(BF16), w_down.astype(BF16),
                     ln2_g, ln2_b)
    return out.reshape(bsz, L, D_MODEL)


def kernel(x, positions, w_in, lam_re, lam_im, log_dt, ssm_b_re, ssm_b_im, ssm_c_re, ssm_c_im, ssm_d, w_glu, cmp_pos_k, cmp_pos_v, w_cmp_k1, w_cmp_k2, w_cmp_v1, w_cmp_v2, w_out, ln1_g, ln1_b, w_router, router_bias, w_gate, w_up, w_down, ws_gate, ws_up, ws_down, ln2_g, ln2_b):
    params = (w_in, lam_re, lam_im, log_dt, ssm_b_re, ssm_b_im, ssm_c_re, ssm_c_im, ssm_d,
              w_glu, cmp_pos_k, cmp_pos_v, w_cmp_k1, w_cmp_k2, w_cmp_v1, w_cmp_v2, w_out, ln1_g, ln1_b,
              w_router, router_bias, w_gate, w_up, w_down, ws_gate, ws_up, ws_down, ln2_g, ln2_b)
    return hybrid_layer(x, positions, *(p[0] for p in params))
```

```python
import functools
import math

import numpy as np
import jax
import jax.numpy as jnp
from jax import lax
from jax.experimental import pallas as pl
from jax.experimental.pallas import tpu as pltpu

D_MODEL = 2048
SSM_WIDTH = 1024
SSM_CH_PER_GROUP = 16
SSM_GROUPS = 64
SSM_STATE = 64
NSA_HEADS = 16
NSA_KV_HEADS = 2
HEAD_DIM = 64
Q_PER_KV = NSA_HEADS // NSA_KV_HEADS
NSA_WIDTH = NSA_HEADS * HEAD_DIM
KV_WIDTH = NSA_KV_HEADS * HEAD_DIM
N_BRANCH = 3
CMP_BLOCK = 32
CMP_STRIDE = 16
SEL_BLOCK = 64
SEL_TOPK = 16
WINDOW = 512
Q_BLOCK = 128
ROPE_THETA = 10000.0
N_EXPERTS = 64
TOP_K = 8
N_EXPERT_GROUPS = 8
TOPK_GROUPS = 4
ROUTED_SCALE = 2.5
EXPERT_FF = 512
DEPTH = 1
DEEPNORM_ALPHA = (2.0 * DEPTH) ** 0.25
LN_EPS = 1e-5
NEG = -1e30
FORCE = 1e4
F32 = jnp.float32
BF16 = jnp.bfloat16

V7X_VMEM_LIMIT_BYTES = 56 * 1024 * 1024


def _layer_norm(x, g, b):
    mu = jnp.mean(x, -1, keepdims=True)
    var = jnp.mean(jnp.square(x - mu), -1, keepdims=True)
    return (x - mu) * lax.rsqrt(var + LN_EPS) * g + b


def _rope_tables(pos_col, inv_row):
    ang = pos_col * inv_row
    return jnp.cos(ang), jnp.sin(ang)


def _rope_lanes(x, cos, sin):
    lane = lax.broadcasted_iota(jnp.int32, (x.shape[0], 128), 1)
    first_half = (lane % HEAD_DIM) < HEAD_DIM // 2
    outs = []
    for blk in range(x.shape[1] // 128):
        xb = x[:, blk * 128:(blk + 1) * 128]
        rot = jnp.where(first_half, -pltpu.roll(xb, 128 - HEAD_DIM // 2, 1), pltpu.roll(xb, HEAD_DIM // 2, 1))
        outs.append(xb * cos + rot * sin)
    return outs[0] if len(outs) == 1 else jnp.concatenate(outs, axis=1)


def _inv_freq_row():
    half = HEAD_DIM // 2
    inv = ROPE_THETA ** (-jnp.arange(half, dtype=F32) / half)
    return jnp.tile(inv, 128 // half).reshape(1, 128)


PROJ_TILE = 512
Q_SCALE = HEAD_DIM ** -0.5 * math.log2(math.e)


def _proj_uq_kernel(x_ref, w_ref, pos_ref, inv_ref, u_ref, q_ref):
    acc = jnp.dot(x_ref[...].astype(BF16), w_ref[...], preferred_element_type=F32)
    u_ref[...] = acc[:, :SSM_WIDTH]
    cos, sin = _rope_tables(pos_ref[...].astype(F32), inv_ref[...])
    q_ref[...] = (_rope_lanes(acc[:, SSM_WIDTH:], cos, sin) * Q_SCALE).astype(BF16)


def proj_uq(xt, w_uq, pos_col):
    n_tok = xt.shape[0]
    return pl.pallas_call(
        _proj_uq_kernel,
        grid=(n_tok // PROJ_TILE,),
        in_specs=[pl.BlockSpec((PROJ_TILE, D_MODEL), lambda i: (i, 0)),
                  pl.BlockSpec((D_MODEL, SSM_WIDTH + NSA_WIDTH), lambda i: (0, 0)),
                  pl.BlockSpec((PROJ_TILE, 1), lambda i: (i, 0)),
                  pl.BlockSpec((1, 128), lambda i: (0, 0))],
        out_specs=[pl.BlockSpec((PROJ_TILE, SSM_WIDTH), lambda i: (i, 0)),
                   pl.BlockSpec((PROJ_TILE, NSA_WIDTH), lambda i: (i, 0))],
        out_shape=[jax.ShapeDtypeStruct((n_tok, SSM_WIDTH), F32), jax.ShapeDtypeStruct((n_tok, NSA_WIDTH), BF16)],
        compiler_params=pltpu.CompilerParams(dimension_semantics=("arbitrary",),
                                             vmem_limit_bytes=V7X_VMEM_LIMIT_BYTES),
        name="proj_uq",
    )(xt, w_uq, pos_col, _inv_freq_row())


KV_COLS = 4 * KV_WIDTH + 2 * 2 * KV_WIDTH + 128


def _proj_kv_kernel(x_ref, w_ref, pos_ref, inv_ref, kst_ref, kwt_ref, vs_ref, vw_ref, kc_ref, vc_ref, g_ref):
    acc = jnp.dot(x_ref[0].astype(BF16), w_ref[...], preferred_element_type=F32)
    cos, sin = _rope_tables(pos_ref[0].astype(F32), inv_ref[...])
    ks_t = _rope_lanes(acc[:, 0:128], cos, sin).T
    kw_t = _rope_lanes(acc[:, 128:256], cos, sin).T
    for k in range(NSA_KV_HEADS):
        kst_ref[0, k, 0] = ks_t[k * HEAD_DIM:(k + 1) * HEAD_DIM].astype(BF16)
        for j in range(PROJ_TILE // Q_BLOCK):
            kwt_ref[0, k, j] = kw_t[k * HEAD_DIM:(k + 1) * HEAD_DIM, j * Q_BLOCK:(j + 1) * Q_BLOCK].astype(BF16)
        vs_ref[0, k, 0] = acc[:, 256 + k * 128: 256 + (k + 1) * 128].astype(BF16)
        vw_ref[0, k] = acc[:, 512 + k * 128: 512 + (k + 1) * 128].astype(BF16)
    kc_ref[0] = acc[:, 768:896]
    vc_ref[0] = acc[:, 896:1024]
    g_ref[0] = acc[:, 1024:1152]


def proj_kv(x, w_kv, pos_col3):
    bsz, seq_len, _ = x.shape
    n_t = seq_len // PROJ_TILE
    per = PROJ_TILE // Q_BLOCK
    return pl.pallas_call(
        _proj_kv_kernel,
        grid=(bsz, n_t),
        in_specs=[pl.BlockSpec((1, PROJ_TILE, D_MODEL), lambda b, i: (b, i, 0)),
                  pl.BlockSpec((D_MODEL, KV_COLS), lambda b, i: (0, 0)),
                  pl.BlockSpec((1, PROJ_TILE, 1), lambda b, i: (b, i, 0)),
                  pl.BlockSpec((1, 128), lambda b, i: (0, 0))],
        out_specs=[
            pl.BlockSpec((1, NSA_KV_HEADS, 1, HEAD_DIM, PROJ_TILE), lambda b, i: (b, 0, i, 0, 0)),
            pl.BlockSpec((1, NSA_KV_HEADS, per, HEAD_DIM, Q_BLOCK), lambda b, i: (b, 0, i, 0, 0)),
            pl.BlockSpec((1, NSA_KV_HEADS, 1, PROJ_TILE, 128), lambda b, i: (b, 0, i, 0, 0)),
            pl.BlockSpec((1, NSA_KV_HEADS, PROJ_TILE, 128), lambda b, i: (b, 0, i, 0)),
            pl.BlockSpec((1, PROJ_TILE, 128), lambda b, i: (b, i, 0)),
            pl.BlockSpec((1, PROJ_TILE, 128), lambda b, i: (b, i, 0)),
            pl.BlockSpec((1, PROJ_TILE, 128), lambda b, i: (b, i, 0)),
        ],
        out_shape=[
            jax.ShapeDtypeStruct((bsz, NSA_KV_HEADS, n_t, HEAD_DIM, PROJ_TILE), BF16),
            jax.ShapeDtypeStruct((bsz, NSA_KV_HEADS, seq_len // Q_BLOCK, HEAD_DIM, Q_BLOCK), BF16),
            jax.ShapeDtypeStruct((bsz, NSA_KV_HEADS, n_t, PROJ_TILE, 128), BF16),
            jax.ShapeDtypeStruct((bsz, NSA_KV_HEADS, seq_len, 128), BF16),
            jax.ShapeDtypeStruct((bsz, seq_len, 128), F32),
            jax.ShapeDtypeStruct((bsz, seq_len, 128), F32),
            jax.ShapeDtypeStruct((bsz, seq_len, 128), F32),
        ],
        compiler_params=pltpu.CompilerParams(dimension_semantics=("arbitrary", "arbitrary"),
                                             vmem_limit_bytes=V7X_VMEM_LIMIT_BYTES),
        name="proj_kv",
    )(x, w_kv, pos_col3, _inv_freq_row())


def _compress_kernel(ck_ref, cv_ref, pek_ref, pev_ref, w1k_ref, w1v_ref, w2k_ref, w2v_ref, pos_ref, inv_ref,
                     kct_ref, vcd_ref):
    def hidden(c_ref, pe_ref, w1_ref):
        c = c_ref[0]
        lo = jnp.dot((c + pe_ref[0]).astype(BF16), w1_ref[0], preferred_element_type=F32)
        hi = jnp.dot((c + pe_ref[1]).astype(BF16), w1_ref[1], preferred_element_type=F32)
        hi_next = jnp.concatenate([hi[1:], jnp.zeros((1, hi.shape[1]), F32)], axis=0)
        return jax.nn.gelu(lo + hi_next).astype(BF16)

    kc = jnp.dot(hidden(ck_ref, pek_ref, w1k_ref), w2k_ref[...], preferred_element_type=F32)
    cos, sin = _rope_tables(pos_ref[0], inv_ref[...])
    kc_t = _rope_lanes(kc, cos, sin).T
    vc = jnp.dot(hidden(cv_ref, pev_ref, w1v_ref), w2v_ref[...], preferred_element_type=F32)
    for k in range(NSA_KV_HEADS):
        kct_ref[0, k] = kc_t[k * HEAD_DIM:(k + 1) * HEAD_DIM].astype(BF16)
        vcd_ref[0, k] = vc[:, k * 128:(k + 1) * 128].astype(BF16)


def compress_kv(kc_raw, vc_raw, positions, cmp_pos_k, cmp_pos_v, w_k1, w_k2, w_v1, w_v2):
    bsz, seq_len, _ = kc_raw.shape
    n_chunk = seq_len // CMP_STRIDE
    width = CMP_STRIDE * 128
    eye = jnp.eye(NSA_KV_HEADS, dtype=F32)

    def chunk_pe(pe):
        pe = pe.reshape(2, CMP_STRIDE, 1, HEAD_DIM)
        return jnp.broadcast_to(pe, (2, CMP_STRIDE, NSA_KV_HEADS, HEAD_DIM)).reshape(2, 1, width)

    def chunk_w1(w1):
        hid = w1.shape[1]
        w = w1.reshape(2, CMP_STRIDE, HEAD_DIM, hid)
        return jnp.einsum('htdj,kc->htkdcj', w, eye).reshape(2, width, NSA_KV_HEADS * hid).astype(BF16)

    hid = w_k2.shape[0]
    w2k = jnp.einsum('jd,kc->kjcd', w_k2, eye).reshape(NSA_KV_HEADS * hid, NSA_KV_HEADS * HEAD_DIM).astype(BF16)
    w2v = jnp.einsum('jd,kc,r->kjcrd', w_v2, eye, jnp.ones((2,), F32)).reshape(
        NSA_KV_HEADS * hid, NSA_KV_HEADS * 128).astype(BF16)
    pos = positions.astype(F32).reshape(bsz, n_chunk, CMP_STRIDE).sum(-1)
    pos_next = jnp.concatenate([pos[:, 1:], pos[:, -1:]], axis=1)
    cmp_pos = ((pos + pos_next) / CMP_BLOCK).reshape(bsz, n_chunk, 1)
    return pl.pallas_call(
        _compress_kernel,
        grid=(bsz,),
        in_specs=[pl.BlockSpec((1, n_chunk, width), lambda b: (b, 0, 0)),
                  pl.BlockSpec((1, n_chunk, width), lambda b: (b, 0, 0)),
                  pl.BlockSpec((2, 1, width), lambda b: (0, 0, 0)),
                  pl.BlockSpec((2, 1, width), lambda b: (0, 0, 0)),
                  pl.BlockSpec((2, width, NSA_KV_HEADS * hid), lambda b: (0, 0, 0)),
                  pl.BlockSpec((2, width, NSA_KV_HEADS * hid), lambda b: (0, 0, 0)),
                  pl.BlockSpec((NSA_KV_HEADS * hid, NSA_KV_HEADS * HEAD_DIM), lambda b: (0, 0)),
                  pl.BlockSpec((NSA_KV_HEADS * hid, NSA_KV_HEADS * 128), lambda b: (0, 0)),
                  pl.BlockSpec((1, n_chunk, 1), lambda b: (b, 0, 0)),
                  pl.BlockSpec((1, 128), lambda b: (0, 0))],
        out_specs=[pl.BlockSpec((1, NSA_KV_HEADS, HEAD_DIM, n_chunk), lambda b: (b, 0, 0, 0)),
                   pl.BlockSpec((1, NSA_KV_HEADS, n_chunk, 128), lambda b: (b, 0, 0, 0))],
        out_shape=[jax.ShapeDtypeStruct((bsz, NSA_KV_HEADS, HEAD_DIM, n_chunk), BF16),
                   jax.ShapeDtypeStruct((bsz, NSA_KV_HEADS, n_chunk, 128), BF16)],
        compiler_params=pltpu.CompilerParams(dimension_semantics=("arbitrary",),
                                             vmem_limit_bytes=V7X_VMEM_LIMIT_BYTES),
        name="compress_kv",
    )(kc_raw.reshape(bsz, n_chunk, width), vc_raw.reshape(bsz, n_chunk, width), chunk_pe(cmp_pos_k),
      chunk_pe(cmp_pos_v), chunk_w1(w_k1), chunk_w1(w_v1), w2k, w2v, cmp_pos, _inv_freq_row())


def _out_ln_kernel(y_ref, a_ref, x_ref, wglu_ref, wout_ref, g_ref, b_ref, o_ref):
    y = y_ref[...]
    y_ssm = y * jax.nn.sigmoid(jnp.dot(y.astype(BF16), wglu_ref[...], preferred_element_type=F32))
    mix = (jnp.dot(y_ssm.astype(BF16), wout_ref[:SSM_WIDTH, :], preferred_element_type=F32)
           + jnp.dot(a_ref[...].astype(BF16), wout_ref[SSM_WIDTH:, :], preferred_element_type=F32))
    o_ref[...] = _layer_norm(DEEPNORM_ALPHA * x_ref[...] + mix, g_ref[...], b_ref[...])


def out_proj_ln(y_s5, y_nsa, xt, w_glu, w_out, ln_g, ln_b, tm=256):
    n_tok = xt.shape[0]
    row = lambda i: (i, 0)
    const = lambda i: (0, 0)
    return pl.pallas_call(
        _out_ln_kernel,
        grid=(n_tok // tm,),
        in_specs=[pl.BlockSpec((tm, SSM_WIDTH), row), pl.BlockSpec((tm, NSA_WIDTH), row),
                  pl.BlockSpec((tm, D_MODEL), row), pl.BlockSpec((SSM_WIDTH, SSM_WIDTH), const),
                  pl.BlockSpec((D_MODEL, D_MODEL), const), pl.BlockSpec((1, D_MODEL), const),
                  pl.BlockSpec((1, D_MODEL), const)],
        out_specs=pl.BlockSpec((tm, D_MODEL), row),
        out_shape=jax.ShapeDtypeStruct((n_tok, D_MODEL), F32),
        compiler_params=pltpu.CompilerParams(dimension_semantics=("arbitrary",),
                                             vmem_limit_bytes=V7X_VMEM_LIMIT_BYTES),
        name="out_proj_ln",
    )(y_s5, y_nsa, xt, w_glu.astype(BF16), w_out.astype(BF16), ln_g.reshape(1, D_MODEL), ln_b.reshape(1, D_MODEL))


S5_CHUNK = 512
S5_SUB = S5_CHUNK // 8
S5_GROUPS_PER_BLOCK = 8
S5_STATES = S5_GROUPS_PER_BLOCK * SSM_STATE


def _cmul_add(ar, ai, xr, xi, br, bi):
    return ar * xr - ai * xi + br, ar * xi + ai * xr + bi


def _s5_kernel(u_ref, lam_ref, bmat_ref, cmat_ref, d_ref, perm_ref, permt_ref, o_ref,
               xr_scr, xi_scr, pr_scr, pi_scr, carry_scr, a_scr, bbar_scr):
    c = pl.program_id(2)

    @pl.when(c == 0)
    def _():
        lr, li = lam_ref[0, 0:1, :], lam_ref[0, 1:2, :]
        dt = jnp.exp(lam_ref[0, 2:3, :])
        mag = jnp.exp(lr * dt)
        ar, ai = mag * jnp.cos(li * dt), mag * jnp.sin(li * dt)
        zr, zi = ar - 1.0, ai
        den = lr * lr + li * li
        fr, fi = (zr * lr + zi * li) / den, (zi * lr - zr * li) / den
        a_scr[0:1, :] = ar
        a_scr[1:2, :] = ai
        b_re, b_im = bmat_ref[0, 0], bmat_ref[0, 1]
        bbar_scr[0] = (fr * b_re - fi * b_im).astype(BF16)
        bbar_scr[1] = (fr * b_im + fi * b_re).astype(BF16)
        carry_scr[...] = jnp.zeros(carry_scr.shape, F32)
        a_re0 = jnp.broadcast_to(ar, (8, S5_STATES))
        a_im0 = jnp.broadcast_to(ai, (8, S5_STATES))

        def pw_body(i, pw):
            pr, pi = pw
            pr_scr[i] = pr
            pi_scr[i] = pi
            return a_re0 * pr - a_im0 * pi, a_re0 * pi + a_im0 * pr

        lax.fori_loop(0, S5_SUB, pw_body, (a_re0, a_im0))

    a_re = jnp.broadcast_to(a_scr[0:1, :], (8, S5_STATES))
    a_im = jnp.broadcast_to(a_scr[1:2, :], (8, S5_STATES))
    u = u_ref[0]
    perm = perm_ref[...]
    u_p = jnp.dot(perm, u.astype(BF16), preferred_element_type=F32).astype(BF16)
    xr_scr[...] = jnp.dot(u_p, bbar_scr[0], preferred_element_type=F32)
    xi_scr[...] = jnp.dot(u_p, bbar_scr[1], preferred_element_type=F32)

    def scan_body(i, x):
        row = pl.multiple_of(i * 8, 8)
        xr, xi = _cmul_add(a_re, a_im, x[0], x[1], xr_scr[pl.ds(row, 8), :], xi_scr[pl.ds(row, 8), :])
        xr_scr[pl.ds(row, 8), :] = xr
        xi_scr[pl.ds(row, 8), :] = xi
        return xr, xi

    zero = jnp.zeros((8, S5_STATES), F32)
    er, ei = lax.fori_loop(0, S5_SUB, scan_body, (zero, zero), unroll=4)

    ar_s = pr_scr[S5_SUB - 1][0:1]
    ai_s = pi_scr[S5_SUB - 1][0:1]
    rows_r = [carry_scr[0:1, :]]
    rows_i = [carry_scr[1:2, :]]
    for j in range(8):
        nr, ni = _cmul_add(ar_s, ai_s, rows_r[-1], rows_i[-1], er[j:j + 1], ei[j:j + 1])
        rows_r.append(nr)
        rows_i.append(ni)
    carry_scr[0:1, :] = rows_r[8]
    carry_scr[1:2, :] = rows_i[8]
    cr = jnp.concatenate(rows_r[:8], axis=0)
    ci = jnp.concatenate(rows_i[:8], axis=0)

    def fix_body(i, carry):
        row = pl.multiple_of(i * 8, 8)
        xr, xi = _cmul_add(pr_scr[i], pi_scr[i], cr, ci, xr_scr[pl.ds(row, 8), :], xi_scr[pl.ds(row, 8), :])
        xr_scr[pl.ds(row, 8), :] = xr
        xi_scr[pl.ds(row, 8), :] = xi
        return carry

    lax.fori_loop(0, S5_SUB, fix_body, 0, unroll=4)

    y_p = (jnp.dot(xr_scr[...].astype(BF16), cmat_ref[0, 0], preferred_element_type=F32)
           - jnp.dot(xi_scr[...].astype(BF16), cmat_ref[0, 1], preferred_element_type=F32))
    y_hi = y_p.astype(BF16)
    y_lo = (y_p - y_hi.astype(F32)).astype(BF16)
    perm_t = permt_ref[...]
    y = jnp.dot(perm_t, y_hi, preferred_element_type=F32) + jnp.dot(perm_t, y_lo, preferred_element_type=F32)
    o_ref[0] = jax.nn.gelu(y + d_ref[0] * u)


def s5_scan(u, lam_re, lam_im, log_dt, b_re, b_im, c_re, c_im, d_skip):
    bsz, seq_len, _ = u.shape
    nb = SSM_GROUPS // S5_GROUPS_PER_BLOCK
    eye = jnp.eye(S5_GROUPS_PER_BLOCK, dtype=F32)

    def blockdiag_b(m):
        m = jnp.swapaxes(m, 1, 2).reshape(nb, S5_GROUPS_PER_BLOCK, SSM_CH_PER_GROUP, SSM_STATE)
        return jnp.einsum('nghp,gk->nghkp', m, eye).reshape(nb, 128, S5_STATES)

    def blockdiag_c(m):
        m = jnp.swapaxes(m, 1, 2).reshape(nb, S5_GROUPS_PER_BLOCK, SSM_STATE, SSM_CH_PER_GROUP)
        return jnp.einsum('ngph,gk->ngpkh', m, eye).reshape(nb, S5_STATES, 128)

    log_dt_states = jnp.broadcast_to(log_dt[:, None], lam_re.shape)
    lam = jnp.stack([m.reshape(nb, S5_STATES) for m in (lam_re, lam_im, log_dt_states)], axis=1)
    bmat = jnp.stack([blockdiag_b(b_re), blockdiag_b(b_im)], axis=1)
    cmat = jnp.stack([blockdiag_c(c_re), blockdiag_c(c_im)], axis=1).astype(BF16)
    d = d_skip.reshape(nb, 1, 128)
    r = np.arange(S5_CHUNK)
    perm = np.zeros((S5_CHUNK, S5_CHUNK), np.float32)
    perm[r, (r % 8) * S5_SUB + r // 8] = 1.0
    perm = jnp.asarray(perm, BF16)
    return pl.pallas_call(
        _s5_kernel,
        grid=(bsz, nb, seq_len // S5_CHUNK),
        in_specs=[
            pl.BlockSpec((1, S5_CHUNK, 128), lambda b, g, c: (b, c, g)),
            pl.BlockSpec((1, 3, S5_STATES), lambda b, g, c: (g, 0, 0)),
            pl.BlockSpec((1, 2, 128, S5_STATES), lambda b, g, c: (g, 0, 0, 0)),
            pl.BlockSpec((1, 2, S5_STATES, 128), lambda b, g, c: (g, 0, 0, 0)),
            pl.BlockSpec((1, 1, 128), lambda b, g, c: (g, 0, 0)),
            pl.BlockSpec((S5_CHUNK, S5_CHUNK), lambda b, g, c: (0, 0)),
            pl.BlockSpec((S5_CHUNK, S5_CHUNK), lambda b, g, c: (0, 0)),
        ],
        out_specs=pl.BlockSpec((1, S5_CHUNK, 128), lambda b, g, c: (b, c, g)),
        out_shape=jax.ShapeDtypeStruct((bsz, seq_len, SSM_WIDTH), F32),
        scratch_shapes=[pltpu.VMEM((S5_CHUNK, S5_STATES), F32), pltpu.VMEM((S5_CHUNK, S5_STATES), F32),
                        pltpu.VMEM((S5_SUB, 8, S5_STATES), F32), pltpu.VMEM((S5_SUB, 8, S5_STATES), F32),
                        pltpu.VMEM((2, S5_STATES), F32), pltpu.VMEM((2, S5_STATES), F32),
                        pltpu.VMEM((2, 128, S5_STATES), BF16)],
        compiler_params=pltpu.CompilerParams(
            dimension_semantics=("arbitrary", "arbitrary", "arbitrary"), vmem_limit_bytes=V7X_VMEM_LIMIT_BYTES),
        name="s5_scan",
    )(u, lam, bmat, cmat, d, perm, perm.T)


def _softmax_tile(s, m_old):
    m_new = jnp.maximum(m_old, jnp.max(s, axis=1, keepdims=True))
    m_wide = jnp.concatenate([m_new] * (s.shape[1] // 128), axis=1)
    return m_new, jnp.exp2(m_old - m_new), jnp.exp2(s - m_wide)


def _lane_is_low(shape):
    return lax.broadcasted_iota(jnp.int32, shape, 1) < HEAD_DIM


def _pad_kt(kt, variant):
    z = jnp.zeros_like(kt)
    return jnp.concatenate([kt, z] if variant == 0 else [z, kt], axis=0)


def _pad_v(vv, variant):
    low = _lane_is_low(vv.shape)
    keep = low if variant == 0 else jnp.logical_not(low)
    return jnp.where(keep, vv, jnp.ones_like(vv))


def _finish(acc, variant):
    lane = lax.broadcasted_iota(jnp.int32, acc.shape, 1)
    lsel = lane == (HEAD_DIM if variant == 0 else 0)
    l = jnp.sum(jnp.where(lsel, acc, 0.0), axis=1, keepdims=True)
    keep = (lane < HEAD_DIM) if variant == 0 else (lane >= HEAD_DIM)
    return jnp.where(keep, acc / l, 0.0)


def _nsa_kernel(q_ref, g_ref, kct_ref, vc_ref, kst_ref, vs_ref, kwt_ref, vw_ref, ovl_ref, gx_ref, o_ref,
                m_scr, acc_scr, s_scr_a, s_scr_b, p_scr, *, seq_len):
    s_slots = (s_scr_a, s_scr_b)
    n_sel = seq_len // SEL_BLOCK
    n_cpad = seq_len // CMP_STRIDE
    sel_tile = 512
    blocks_per_tile = sel_tile // SEL_BLOCK
    win_tiles = WINDOW // Q_BLOCK + 1
    n_pair = Q_PER_KV // 2
    rows = n_pair * Q_BLOCK
    i = pl.program_id(2)
    t0 = i * Q_BLOCK

    qb = q_ref[0]
    qst = jnp.concatenate([qb[:, p * 128:(p + 1) * 128] for p in range(n_pair)], axis=0)

    sig = jax.nn.sigmoid(g_ref[0])
    sig_hi = sig.astype(BF16)
    sig_lo = (sig - sig_hi.astype(F32)).astype(BF16)
    gx = gx_ref[0]
    gexp = (jnp.dot(sig_hi, gx, preferred_element_type=F32) + jnp.dot(sig_lo, gx, preferred_element_type=F32))

    def gate_of(branch):
        base = branch * n_pair * 128
        return jnp.concatenate([gexp[:, base + p * 128: base + (p + 1) * 128] for p in range(n_pair)], axis=0)

    t_row = t0 + lax.broadcasted_iota(jnp.int32, (Q_BLOCK, 1), 0)

    slab = 64
    kct = kct_ref[0, 0]
    s_cmp = [jnp.dot(qst, _pad_kt(kct, v), preferred_element_type=F32) for v in range(2)]
    n_kblk = seq_len // Q_BLOCK
    w0 = jnp.clip(i - (win_tiles - 1), 0, n_kblk - win_tiles)
    kw = jnp.concatenate([kwt_ref[0, 0, w0 + j] for j in range(win_tiles)], axis=1)
    s_win = [jnp.dot(qst, _pad_kt(kw, v), preferred_element_type=F32) for v in range(2)]
    for v in range(2):
        s_slots[0][v] = jnp.dot(qst, _pad_kt(kst_ref[0, 0, 0], v), preferred_element_type=F32)

    n_iota = lax.broadcasted_iota(jnp.int32, (Q_BLOCK, n_cpad), 1)
    cmask = (n_iota * CMP_STRIDE + (CMP_BLOCK - 1)) <= t_row
    cmask4 = jnp.concatenate([cmask] * n_pair, axis=0)
    vcd = vc_ref[0, 0]
    p_sum = jnp.zeros((Q_BLOCK, n_cpad), F32)
    out = jnp.zeros((rows, 128), F32)
    o_c = jnp.zeros((rows, 128), F32)
    for v in range(2):
        s = jnp.where(cmask4, s_cmp[v], NEG)
        m = jnp.max(s, axis=1, keepdims=True)
        e = jnp.where(cmask4, jnp.exp2(s - m), 0.0)
        l = jnp.sum(e, axis=1, keepdims=True)
        p = e * (1.0 / jnp.maximum(l, 1e-30))
        for pp in range(n_pair):
            p_sum = p_sum + p[pp * Q_BLOCK:(pp + 1) * Q_BLOCK]
        low = _lane_is_low((n_cpad, 128))
        vz = jnp.where(low if v == 0 else jnp.logical_not(low), vcd, jnp.zeros_like(vcd))
        o_c = o_c + jnp.dot(p.astype(BF16), vz, preferred_element_type=F32)
    out = out + gate_of(0) * o_c

    ps_hi = p_sum.astype(BF16)
    ps_lo = (p_sum - ps_hi.astype(F32)).astype(BF16)
    ovl = ovl_ref[...]
    nt = (((1,), (1,)), ((), ()))
    imp_t = (lax.dot_general(ovl, ps_hi, nt, preferred_element_type=F32)
             + lax.dot_general(ovl, ps_lo, nt, preferred_element_type=F32))

    vw = jnp.concatenate([vw_ref[0, 0, w0 + j] for j in range(win_tiles)], axis=0)
    kpos_w = w0 * Q_BLOCK + lax.broadcasted_iota(jnp.int32, (Q_BLOCK, win_tiles * Q_BLOCK), 1)
    diff = t_row - kpos_w
    wbias = jnp.where((diff >= 0) & (diff < WINDOW), 0.0, NEG)
    wbias4 = jnp.concatenate([wbias] * n_pair, axis=0)
    o_w = jnp.zeros((rows, 128), F32)
    for v in range(2):
        s = s_win[v] + wbias4
        m = jnp.max(s, axis=1, keepdims=True)
        p = jnp.exp2(s - m)
        o_w = o_w + _finish(jnp.dot(p.astype(BF16), _pad_v(vw, v), preferred_element_type=F32), v)
    out = out + gate_of(2) * o_w

    s_iota = lax.broadcasted_iota(jnp.int32, (n_sel, Q_BLOCK), 0)
    t_lane = t0 + lax.broadcasted_iota(jnp.int32, (n_sel, Q_BLOCK), 1)
    cur = t_lane // SEL_BLOCK
    forced = (s_iota == 0) | (s_iota == cur) | (s_iota == cur - 1)
    valid = s_iota * SEL_BLOCK <= t_lane
    score = jnp.where(forced, FORCE, jnp.where(valid, imp_t, -1.0))
    s_f = s_iota.astype(F32)
    sel_t = jnp.zeros((n_sel, Q_BLOCK), F32)
    for _ in range(min(SEL_TOPK, n_sel)):
        mx = jnp.max(score, axis=0, keepdims=True)
        idx = jnp.min(jnp.where(score == mx, s_f, float(n_sel)), axis=0, keepdims=True)
        hit = s_f == idx
        sel_t = jnp.where(hit, 1.0, sel_t)
        score = jnp.where(hit, -3e38, score)
    selmask = sel_t.T.astype(BF16)

    m_scr[...] = jnp.full(m_scr.shape, NEG, F32)
    acc_scr[...] = jnp.zeros(acc_scr.shape, F32)
    n_tiles = (t0 + Q_BLOCK + sel_tile - 1) // sel_tile

    last_tile = seq_len // sel_tile - 1

    def bias_of(kt):
        blk = kt * blocks_per_tile + lax.broadcasted_iota(jnp.int32, (n_sel, sel_tile), 1) // SEL_BLOCK
        expand = (lax.broadcasted_iota(jnp.int32, (n_sel, sel_tile), 0) == blk).astype(BF16)
        selexp = jnp.dot(selmask, expand, preferred_element_type=F32)
        kpos = kt * sel_tile + lax.broadcasted_iota(jnp.int32, (Q_BLOCK, sel_tile), 1)
        bias = jnp.where((selexp > 0.5) & (kpos <= t_row), 0.0, NEG)
        return jnp.concatenate([bias] * n_pair, axis=0)

    def scores_into(slot, kt):
        bias4 = bias_of(kt)
        kt_tile = kst_ref[0, 0, jnp.minimum(kt, last_tile)]
        for v in range(2):
            s_slots[slot][v] = jnp.dot(qst, _pad_kt(kt_tile, v), preferred_element_type=F32) + bias4

    def attend_from(slot, kt):
        v_tile = vs_ref[0, 0, jnp.minimum(kt, last_tile)]
        for v in range(2):
            for h in range(rows // slab):
                r = slice(h * slab, (h + 1) * slab)
                m_new, alpha, p = _softmax_tile(s_slots[slot][v, r, :], m_scr[v, r, :])
                m_scr[v, r, :] = m_new
                acc_scr[v, r, :] = alpha * acc_scr[v, r, :]
                p_scr[v, r, :] = p.astype(BF16)
            acc_scr[v] += jnp.dot(p_scr[v], _pad_v(v_tile, v), preferred_element_type=F32)

    for v in range(2):
        s_slots[0][v] = s_slots[0][v] + bias_of(0)

    def sel_body(j, carry):
        kt = 2 * j
        scores_into(1, kt + 1)
        attend_from(0, kt)
        scores_into(0, kt + 2)
        attend_from(1, kt + 1)
        return carry

    lax.fori_loop(0, (n_tiles + 1) // 2, sel_body, 0)
    out = out + gate_of(1) * (_finish(acc_scr[0], 0) + _finish(acc_scr[1], 1))

    o_ref[0] = jnp.concatenate([out[p * Q_BLOCK:(p + 1) * Q_BLOCK] for p in range(n_pair)], axis=1)


def nsa_attention(q, gate_pad, kct, vc, kst, vs, kwt, vw):
    bsz, seq_len, _ = q.shape
    n_sel = seq_len // SEL_BLOCK
    n_cpad = seq_len // CMP_STRIDE
    n_cmp = (seq_len - CMP_BLOCK) // CMP_STRIDE + 1
    n_pair = Q_PER_KV // 2
    cs = np.arange(n_cpad) * CMP_STRIDE
    ce = cs + CMP_BLOCK - 1
    ss = np.arange(n_sel) * SEL_BLOCK
    se = ss + SEL_BLOCK - 1
    ovl = (cs[None, :] <= se[:, None]) & (ce[None, :] >= ss[:, None]) & (np.arange(n_cpad)[None, :] < n_cmp)
    ovl = jnp.asarray(ovl.astype(np.float32), BF16)
    gx = np.zeros((NSA_KV_HEADS, 128, N_BRANCH * n_pair * 128), np.float32)
    for k in range(NSA_KV_HEADS):
        for hl in range(Q_PER_KV):
            for br in range(N_BRANCH):
                c0 = br * n_pair * 128 + hl * HEAD_DIM
                gx[k, (k * Q_PER_KV + hl) * N_BRANCH + br, c0:c0 + HEAD_DIM] = 1.0
    gx = jnp.asarray(gx, BF16)
    width = Q_PER_KV * HEAD_DIM
    full = lambda *shape: pl.BlockSpec((1, 1) + shape, lambda b, k, i: (b, k) + (0,) * len(shape))
    return pl.pallas_call(
        functools.partial(_nsa_kernel, seq_len=seq_len),
        grid=(bsz, NSA_KV_HEADS, seq_len // Q_BLOCK),
        in_specs=[
            pl.BlockSpec((1, Q_BLOCK, width), lambda b, k, i: (b, i, k)),
            pl.BlockSpec((1, Q_BLOCK, 128), lambda b, k, i: (b, i, 0)),
            full(HEAD_DIM, n_cpad), full(n_cpad, 128),
            full(seq_len // 512, HEAD_DIM, 512), full(seq_len // 512, 512, 128),
            full(seq_len // Q_BLOCK, HEAD_DIM, Q_BLOCK), full(seq_len // Q_BLOCK, Q_BLOCK, 128),
            pl.BlockSpec((n_sel, n_cpad), lambda b, k, i: (0, 0)),
            pl.BlockSpec((1, 128, N_BRANCH * n_pair * 128), lambda b, k, i: (k, 0, 0)),
        ],
        out_specs=pl.BlockSpec((1, Q_BLOCK, width), lambda b, k, i: (b, i, k)),
        out_shape=jax.ShapeDtypeStruct((bsz, seq_len, NSA_WIDTH), F32),
        scratch_shapes=[pltpu.VMEM((2, n_pair * Q_BLOCK, 128), F32), pltpu.VMEM((2, n_pair * Q_BLOCK, 128), F32),
                        pltpu.VMEM((2, n_pair * Q_BLOCK, 512), F32), pltpu.VMEM((2, n_pair * Q_BLOCK, 512), F32),
                        pltpu.VMEM((2, n_pair * Q_BLOCK, 512), BF16)],
        compiler_params=pltpu.CompilerParams(
            dimension_semantics=("arbitrary", "arbitrary", "arbitrary"), vmem_limit_bytes=V7X_VMEM_LIMIT_BYTES),
        name="nsa_attention",
    )(q, gate_pad, kct, vc, kst, vs, kwt, vw, ovl, gx)


ROUTER_TILE = 512
MOE_TILE = 1024
MOE_SUB = 256
MOE_ROWS = 48
MOE_SLOT = 64
MOE_GROUP = 4


def _first_max_mask(x, idx_f, axis):
    mx = jnp.max(x, axis=axis, keepdims=True)
    first = jnp.min(jnp.where(x == mx, idx_f, 1e9), axis=axis, keepdims=True)
    return idx_f == first, mx


def _router_kernel(x_ref, wrt_ref, bias_ref, w_ref, sel_ref):
    per_group = N_EXPERTS // N_EXPERT_GROUPS
    tr = x_ref.shape[0]
    nt = (((1,), (1,)), ((), ()))
    logits = lax.dot_general(wrt_ref[...], x_ref[...].astype(BF16), nt, preferred_element_type=F32)
    aff = jax.nn.sigmoid(logits)
    biased = aff + bias_ref[...]
    grp = biased.reshape(N_EXPERT_GROUPS, per_group, tr)
    in_grp = lax.broadcasted_iota(jnp.int32, grp.shape, 1).astype(F32)
    hit1, m1 = _first_max_mask(grp, in_grp, 1)
    m2 = jnp.max(jnp.where(hit1, -jnp.inf, grp), axis=1, keepdims=True)
    gscore = (m1 + m2).reshape(N_EXPERT_GROUPS, tr)
    g_idx = lax.broadcasted_iota(jnp.int32, gscore.shape, 0).astype(F32)
    gsel = jnp.zeros(gscore.shape, F32)
    for _ in range(TOPK_GROUPS):
        hit, _ = _first_max_mask(gscore, g_idx, 0)
        gsel = jnp.where(hit, 1.0, gsel)
        gscore = jnp.where(hit, -jnp.inf, gscore)
    gmask = jnp.broadcast_to(gsel.reshape(N_EXPERT_GROUPS, 1, tr), grp.shape).reshape(N_EXPERTS, tr)
    cand = jnp.where(gmask > 0.5, biased, NEG)
    e_idx = lax.broadcasted_iota(jnp.int32, cand.shape, 0).astype(F32)
    sel = jnp.zeros(cand.shape, F32)
    for _ in range(TOP_K):
        hit, _ = _first_max_mask(cand, e_idx, 0)
        sel = jnp.where(hit, 1.0, sel)
        cand = jnp.where(hit, -jnp.inf, cand)
    w = jnp.where(sel > 0.5, aff, 0.0)
    w_ref[...] = w / jnp.sum(w, axis=0, keepdims=True) * ROUTED_SCALE
    sel_ref[...] = sel


def moe_router(xt, w_router, router_bias):
    n_tok = xt.shape[0]
    wrt = w_router.T.astype(BF16)
    return pl.pallas_call(
        _router_kernel,
        grid=(n_tok // ROUTER_TILE,),
        in_specs=[pl.BlockSpec((ROUTER_TILE, D_MODEL), lambda i: (i, 0)),
                  pl.BlockSpec((N_EXPERTS, D_MODEL), lambda i: (0, 0)),
                  pl.BlockSpec((N_EXPERTS, 1), lambda i: (0, 0))],
        out_specs=[pl.BlockSpec((N_EXPERTS, ROUTER_TILE), lambda i: (0, i)),
                   pl.BlockSpec((N_EXPERTS, ROUTER_TILE), lambda i: (0, i))],
        out_shape=[jax.ShapeDtypeStruct((N_EXPERTS, n_tok), F32), jax.ShapeDtypeStruct((N_EXPERTS, n_tok), F32)],
        compiler_params=pltpu.CompilerParams(dimension_semantics=("arbitrary",),
                                             vmem_limit_bytes=V7X_VMEM_LIMIT_BYTES),
        name="moe_router",
    )(xt, wrt, router_bias.reshape(N_EXPERTS, 1))


def _moe_kernel(cnt_ref, x_ref, sel_ref, w_ref, init_ref, wg_ref, wu_ref, wd_ref, lng_ref, lnb_ref, o_ref,
                rank_scr, ybuf_scr, sbuf_scr):
    i = pl.program_id(0)
    e = pl.program_id(1)
    tm = x_ref.shape[0]
    n_sub = tm // MOE_SUB
    tn = (((0,), (0,)), ((), ()))

    @pl.when(e == 0)
    def _():
        o_ref[...] = init_ref[...]
        before = (lax.broadcasted_iota(jnp.int32, (MOE_SUB, MOE_SUB), 0)
                  < lax.broadcasted_iota(jnp.int32, (MOE_SUB, MOE_SUB), 1))
        before = jnp.where(before, 1.0, 0.0).astype(BF16)
        for q in range(n_sub):
            cols = slice(q * MOE_SUB, (q + 1) * MOE_SUB)
            rank_scr[:, cols] = jnp.dot(sel_ref[:, cols].astype(BF16), before, preferred_element_type=F32)

    count = cnt_ref[i * N_EXPERTS + e]
    sel_e = sel_ref[pl.ds(e, 1), :]
    rank_e = rank_scr[pl.ds(e, 1), :]
    w_e = w_ref[pl.ds(e, 1), :]

    def expert_pass(c):
        row = (c * MOE_ROWS + lax.broadcasted_iota(jnp.int32, (MOE_ROWS, MOE_SUB), 0)).astype(F32)
        scatters, xes = [], []
        for q in range(n_sub):
            cols = slice(q * MOE_SUB, (q + 1) * MOE_SUB)
            hit = (rank_e[:, cols] == row) & (sel_e[:, cols] > 0.5)
            gather = jnp.where(hit, 1.0, 0.0).astype(BF16)
            scatters.append(jnp.where(hit, w_e[:, cols], 0.0).astype(BF16))
            xes.append(jnp.dot(gather, x_ref[cols, :], preferred_element_type=F32).astype(BF16))
        xe = jnp.concatenate(xes, axis=0)
        g = jnp.dot(xe, wg_ref[0], preferred_element_type=F32)
        u = jnp.dot(xe, wu_ref[0], preferred_element_type=F32)
        h = (jax.nn.silu(g) * u).astype(BF16)
        return scatters, jnp.dot(h, wd_ref[0], preferred_element_type=F32).astype(BF16)

    slot = e % MOE_GROUP
    scatters, y = expert_pass(0)
    spare = MOE_SLOT - MOE_ROWS
    for q in range(n_sub):
        sbuf_scr[q, slot] = jnp.concatenate([scatters[q], jnp.zeros((spare, MOE_SUB), BF16)], axis=0)
        ybuf_scr[q, slot] = jnp.concatenate(
            [y[q * MOE_ROWS:(q + 1) * MOE_ROWS], jnp.zeros((spare, D_MODEL), BF16)], axis=0)

    @pl.when(slot == MOE_GROUP - 1)
    def _():
        for q in range(n_sub):
            cols = slice(q * MOE_SUB, (q + 1) * MOE_SUB)
            o_ref[cols, :] += lax.dot_general(sbuf_scr[q].reshape(MOE_GROUP * MOE_SLOT, MOE_SUB),
                                              ybuf_scr[q].reshape(MOE_GROUP * MOE_SLOT, D_MODEL), tn,
                                              preferred_element_type=F32)

    def overflow_body(c, carry):
        sc, yy = expert_pass(c)
        for q in range(n_sub):
            cols = slice(q * MOE_SUB, (q + 1) * MOE_SUB)
            o_ref[cols, :] += lax.dot_general(sc[q], yy[q * MOE_ROWS:(q + 1) * MOE_ROWS], tn,
                                              preferred_element_type=F32)
        return carry

    lax.fori_loop(1, (count + MOE_ROWS - 1) // MOE_ROWS, overflow_body, 0)

    @pl.when(e == N_EXPERTS - 1)
    def _():
        o_ref[...] = _layer_norm(o_ref[...], lng_ref[...], lnb_ref[...])


def moe_routed(x_bf16, sel_t, w_t, init, w_gate, w_up, w_down, ln_g, ln_b):
    n_tok = x_bf16.shape[0]
    n_tiles = n_tok // MOE_TILE
    per_sub = jnp.sum(sel_t.reshape(N_EXPERTS, n_tiles, MOE_TILE // MOE_SUB, MOE_SUB), axis=-1)
    cnt = jnp.max(per_sub, axis=-1).T.astype(jnp.int32).reshape(-1)
    grid_spec = pltpu.PrefetchScalarGridSpec(
        num_scalar_prefetch=1,
        grid=(n_tiles, N_EXPERTS),
        in_specs=[
            pl.BlockSpec((MOE_TILE, D_MODEL), lambda i, e, cnt: (i, 0)),
            pl.BlockSpec((N_EXPERTS, MOE_TILE), lambda i, e, cnt: (0, i)),
            pl.BlockSpec((N_EXPERTS, MOE_TILE), lambda i, e, cnt: (0, i)),
            pl.BlockSpec((MOE_TILE, D_MODEL), lambda i, e, cnt: (i, 0), pipeline_mode=pl.Buffered(1)),
            pl.BlockSpec((1, D_MODEL, EXPERT_FF), lambda i, e, cnt: (e, 0, 0)),
            pl.BlockSpec((1, D_MODEL, EXPERT_FF), lambda i, e, cnt: (e, 0, 0)),
            pl.BlockSpec((1, EXPERT_FF, D_MODEL), lambda i, e, cnt: (e, 0, 0)),
            pl.BlockSpec((1, D_MODEL), lambda i, e, cnt: (0, 0)),
            pl.BlockSpec((1, D_MODEL), lambda i, e, cnt: (0, 0)),
        ],
        out_specs=pl.BlockSpec((MOE_TILE, D_MODEL), lambda i, e, cnt: (i, 0)),
        scratch_shapes=[pltpu.VMEM((N_EXPERTS, MOE_TILE), F32),
                        pltpu.VMEM((MOE_TILE // MOE_SUB, MOE_GROUP, MOE_SLOT, D_MODEL), BF16),
                        pltpu.VMEM((MOE_TILE // MOE_SUB, MOE_GROUP, MOE_SLOT, MOE_SUB), BF16)],
    )
    return pl.pallas_call(
        _moe_kernel,
        grid_spec=grid_spec,
        out_shape=jax.ShapeDtypeStruct((n_tok, D_MODEL), F32),
        compiler_params=pltpu.CompilerParams(dimension_semantics=("arbitrary", "arbitrary"),
                                             vmem_limit_bytes=V7X_VMEM_LIMIT_BYTES),
        name="moe_routed",
    )(cnt, x_bf16, sel_t, w_t, init, w_gate, w_up, w_down, ln_g.reshape(1, D_MODEL), ln_b.reshape(1, D_MODEL))


def _shared_ffn_kernel(x_ref, wg_ref, wu_ref, wd_ref, o_ref, xb_ref):
    x = x_ref[...]
    xb = x.astype(BF16)
    h = jax.nn.silu(jnp.dot(xb, wg_ref[...], preferred_element_type=F32)) * jnp.dot(
        xb, wu_ref[...], preferred_element_type=F32)
    o_ref[...] = DEEPNORM_ALPHA * x + jnp.dot(h.astype(BF16), wd_ref[...], preferred_element_type=F32)
    xb_ref[...] = xb


def shared_ffn(xt, wg, wu, wd, tm=512):
    n_tok, d = xt.shape
    ff = wg.shape[1]
    return pl.pallas_call(
        _shared_ffn_kernel,
        grid=(n_tok // tm,),
        in_specs=[pl.BlockSpec((tm, d), lambda i: (i, 0)), pl.BlockSpec((d, ff), lambda i: (0, 0)),
                  pl.BlockSpec((d, ff), lambda i: (0, 0)), pl.BlockSpec((ff, d), lambda i: (0, 0))],
        out_specs=[pl.BlockSpec((tm, d), lambda i: (i, 0)), pl.BlockSpec((tm, d), lambda i: (i, 0))],
        out_shape=[jax.ShapeDtypeStruct((n_tok, d), F32), jax.ShapeDtypeStruct((n_tok, d), BF16)],
        compiler_params=pltpu.CompilerParams(dimension_semantics=("arbitrary",),
                                             vmem_limit_bytes=V7X_VMEM_LIMIT_BYTES),
        name="shared_ffn",
    )(xt, wg.astype(BF16), wu.astype(BF16), wd.astype(BF16))


def hybrid_layer(x, positions, w_in, lam_re, lam_im, log_dt, ssm_b_re, ssm_b_im, ssm_c_re, ssm_c_im, ssm_d,
                 w_glu, cmp_pos_k, cmp_pos_v, w_cmp_k1, w_cmp_k2, w_cmp_v1, w_cmp_v2, w_out, ln1_g, ln1_b,
                 w_router, router_bias, w_gate, w_up, w_down, ws_gate, ws_up, ws_down, ln2_g, ln2_b):
    bsz, L, _ = x.shape
    sizes = [SSM_WIDTH, NSA_WIDTH] + [KV_WIDTH] * 6 + [NSA_HEADS * N_BRANCH]
    o = [0] + [int(v) for v in np.cumsum(sizes)]
    col = lambda j: w_in[:, o[j]:o[j + 1]]
    dup = lambda w: jnp.concatenate([w[:, h * HEAD_DIM:(h + 1) * HEAD_DIM] for h in (0, 0, 1, 1)], axis=1)
    gate_cols = jnp.pad(col(8), ((0, 0), (0, 128 - NSA_HEADS * N_BRANCH)))
    w_uq = w_in[:, :o[2]].astype(BF16)
    w_kv = jnp.concatenate([col(4), col(6), dup(col(5)), dup(col(7)), col(2), col(3), gate_cols], axis=1).astype(BF16)

    xt = x.reshape(bsz * L, D_MODEL)
    u, q = proj_uq(xt, w_uq, positions.reshape(bsz * L, 1))
    kst, kwt, vs, vw, kc_raw, vc_raw, gate_pad = proj_kv(x, w_kv, positions.reshape(bsz, L, 1))
    kct, vcd = compress_kv(kc_raw, vc_raw, positions, cmp_pos_k, cmp_pos_v, w_cmp_k1, w_cmp_k2, w_cmp_v1, w_cmp_v2)
    y_s5 = s5_scan(u.reshape(bsz, L, SSM_WIDTH), lam_re, lam_im, log_dt, ssm_b_re, ssm_b_im, ssm_c_re, ssm_c_im, ssm_d)
    vw = vw.reshape(bsz, NSA_KV_HEADS, L // Q_BLOCK, Q_BLOCK, 128)
    y_nsa = nsa_attention(q.reshape(bsz, L, NSA_WIDTH), gate_pad, kct, vcd, kst, vs, kwt, vw)
    x1 = out_proj_ln(y_s5.reshape(bsz * L, SSM_WIDTH), y_nsa.reshape(bsz * L, NSA_WIDTH), xt, w_glu, w_out,
                     ln1_g, ln1_b)
    w_t, sel_t = moe_router(x1, w_router, router_bias)
    acc0, x1b = shared_ffn(x1, ws_gate, ws_up, ws_down)
    out = moe_routed(x1b, sel_t, w_t, acc0, w_gate.astype(BF16), w_up.astype(BF16), w_down.astype(BF16),
                     ln2_g, ln2_b)
    return out.reshape(bsz, L, D_MODEL)


def kernel(x, positions, w_in, lam_re, lam_im, log_dt, ssm_b_re, ssm_b_im, ssm_c_re, ssm_c_im, ssm_d, w_glu, cmp_pos_k, cmp_pos_v, w_cmp_k1, w_cmp_k2, w_cmp_v1, w_cmp_v2, w_out, ln1_g, ln1_b, w_router, router_bias, w_gate, w_up, w_down, ws_gate, ws_up, ws_down, ln2_g, ln2_b):
    params = (w_in, lam_re, lam_im, log_dt, ssm_b_re, ssm_b_im, ssm_c_re, ssm_c_im, ssm_d,
              w_glu, cmp_pos_k, cmp_pos_v, w_cmp_k1, w_cmp_k2, w_cmp_v1, w_cmp_v2, w_out, ln1_g, ln1_b,
              w_router, router_bias, w_gate, w_up, w_down, ws_gate, ws_up, ws_down, ln2_g, ln2_b)
    return hybrid_layer(x, positions, *(p[0] for p in params))
```

```python
import functools
import math

import numpy as np
import jax
import jax.numpy as jnp
from jax import lax
from jax.experimental import pallas as pl
from jax.experimental.pallas import tpu as pltpu

D_MODEL = 2048
SSM_WIDTH = 1024
SSM_CH_PER_GROUP = 16
SSM_GROUPS = 64
SSM_STATE = 64
NSA_HEADS = 16
NSA_KV_HEADS = 2
HEAD_DIM = 64
Q_PER_KV = NSA_HEADS // NSA_KV_HEADS
NSA_WIDTH = NSA_HEADS * HEAD_DIM
KV_WIDTH = NSA_KV_HEADS * HEAD_DIM
N_BRANCH = 3
CMP_BLOCK = 32
CMP_STRIDE = 16
SEL_BLOCK = 64
SEL_TOPK = 16
WINDOW = 512
Q_BLOCK = 128
ROPE_THETA = 10000.0
N_EXPERTS = 64
TOP_K = 8
N_EXPERT_GROUPS = 8
TOPK_GROUPS = 4
ROUTED_SCALE = 2.5
EXPERT_FF = 512
DEPTH = 1
DEEPNORM_ALPHA = (2.0 * DEPTH) ** 0.25
LN_EPS = 1e-5
NEG = -1e30
FORCE = 1e4
F32 = jnp.float32
BF16 = jnp.bfloat16

V7X_VMEM_LIMIT_BYTES = 56 * 1024 * 1024


def _layer_norm(x, g, b):
    mu = jnp.mean(x, -1, keepdims=True)
    var = jnp.mean(jnp.square(x - mu), -1, keepdims=True)
    return (x - mu) * lax.rsqrt(var + LN_EPS) * g + b


def _rope_tables(pos_col, inv_row):
    ang = pos_col * inv_row
    return jnp.cos(ang), jnp.sin(ang)


def _rope_lanes(x, cos, sin):
    lane = lax.broadcasted_iota(jnp.int32, (x.shape[0], 128), 1)
    first_half = (lane % HEAD_DIM) < HEAD_DIM // 2
    outs = []
    for blk in range(x.shape[1] // 128):
        xb = x[:, blk * 128:(blk + 1) * 128]
        rot = jnp.where(first_half, -pltpu.roll(xb, 128 - HEAD_DIM // 2, 1), pltpu.roll(xb, HEAD_DIM // 2, 1))
        outs.append(xb * cos + rot * sin)
    return outs[0] if len(outs) == 1 else jnp.concatenate(outs, axis=1)


def _inv_freq_row():
    half = HEAD_DIM // 2
    inv = ROPE_THETA ** (-jnp.arange(half, dtype=F32) / half)
    return jnp.tile(inv, 128 // half).reshape(1, 128)


PROJ_TILE = 512
Q_SCALE = HEAD_DIM ** -0.5 * math.log2(math.e)


def _proj_uq_kernel(x_ref, w_ref, pos_ref, inv_ref, u_ref, q_ref):
    acc = jnp.dot(x_ref[...].astype(BF16), w_ref[...], preferred_element_type=F32)
    u_ref[...] = acc[:, :SSM_WIDTH]
    cos, sin = _rope_tables(pos_ref[...].astype(F32), inv_ref[...])
    q_ref[...] = (_rope_lanes(acc[:, SSM_WIDTH:], cos, sin) * Q_SCALE).astype(BF16)


def proj_uq(xt, w_uq, pos_col):
    n_tok = xt.shape[0]
    return pl.pallas_call(
        _proj_uq_kernel,
        grid=(n_tok // PROJ_TILE,),
        in_specs=[pl.BlockSpec((PROJ_TILE, D_MODEL), lambda i: (i, 0)),
                  pl.BlockSpec((D_MODEL, SSM_WIDTH + NSA_WIDTH), lambda i: (0, 0)),
                  pl.BlockSpec((PROJ_TILE, 1), lambda i: (i, 0)),
                  pl.BlockSpec((1, 128), lambda i: (0, 0))],
        out_specs=[pl.BlockSpec((PROJ_TILE, SSM_WIDTH), lambda i: (i, 0)),
                   pl.BlockSpec((PROJ_TILE, NSA_WIDTH), lambda i: (i, 0))],
        out_shape=[jax.ShapeDtypeStruct((n_tok, SSM_WIDTH), F32), jax.ShapeDtypeStruct((n_tok, NSA_WIDTH), BF16)],
        compiler_params=pltpu.CompilerParams(dimension_semantics=("arbitrary",),
                                             vmem_limit_bytes=V7X_VMEM_LIMIT_BYTES),
        name="proj_uq",
    )(xt, w_uq, pos_col, _inv_freq_row())


KV_COLS = 4 * KV_WIDTH + 2 * 2 * KV_WIDTH + 128


def _proj_kv_kernel(x_ref, w_ref, pos_ref, inv_ref, kst_ref, kwt_ref, vs_ref, vw_ref, kc_ref, vc_ref, g_ref):
    acc = jnp.dot(x_ref[0].astype(BF16), w_ref[...], preferred_element_type=F32)
    cos, sin = _rope_tables(pos_ref[0].astype(F32), inv_ref[...])
    ks_t = _rope_lanes(acc[:, 0:128], cos, sin).T
    kw_t = _rope_lanes(acc[:, 128:256], cos, sin).T
    for k in range(NSA_KV_HEADS):
        kst_ref[0, k, 0] = ks_t[k * HEAD_DIM:(k + 1) * HEAD_DIM].astype(BF16)
        for j in range(PROJ_TILE // Q_BLOCK):
            kwt_ref[0, k, j] = kw_t[k * HEAD_DIM:(k + 1) * HEAD_DIM, j * Q_BLOCK:(j + 1) * Q_BLOCK].astype(BF16)
        vs_ref[0, k, 0] = acc[:, 256 + k * 128: 256 + (k + 1) * 128].astype(BF16)
        vw_ref[0, k] = acc[:, 512 + k * 128: 512 + (k + 1) * 128].astype(BF16)
    kc_ref[0] = acc[:, 768:896]
    vc_ref[0] = acc[:, 896:1024]
    g_ref[0] = acc[:, 1024:1152]


def proj_kv(x, w_kv, pos_col3):
    bsz, seq_len, _ = x.shape
    n_t = seq_len // PROJ_TILE
    per = PROJ_TILE // Q_BLOCK
    return pl.pallas_call(
        _proj_kv_kernel,
        grid=(bsz, n_t),
        in_specs=[pl.BlockSpec((1, PROJ_TILE, D_MODEL), lambda b, i: (b, i, 0)),
                  pl.BlockSpec((D_MODEL, KV_COLS), lambda b, i: (0, 0)),
                  pl.BlockSpec((1, PROJ_TILE, 1), lambda b, i: (b, i, 0)),
                  pl.BlockSpec((1, 128), lambda b, i: (0, 0))],
        out_specs=[
            pl.BlockSpec((1, NSA_KV_HEADS, 1, HEAD_DIM, PROJ_TILE), lambda b, i: (b, 0, i, 0, 0)),
            pl.BlockSpec((1, NSA_KV_HEADS, per, HEAD_DIM, Q_BLOCK), lambda b, i: (b, 0, i, 0, 0)),
            pl.BlockSpec((1, NSA_KV_HEADS, 1, PROJ_TILE, 128), lambda b, i: (b, 0, i, 0, 0)),
            pl.BlockSpec((1, NSA_KV_HEADS, PROJ_TILE, 128), lambda b, i: (b, 0, i, 0)),
            pl.BlockSpec((1, PROJ_TILE, 128), lambda b, i: (b, i, 0)),
            pl.BlockSpec((1, PROJ_TILE, 128), lambda b, i: (b, i, 0)),
            pl.BlockSpec((1, PROJ_TILE, 128), lambda b, i: (b, i, 0)),
        ],
        out_shape=[
            jax.ShapeDtypeStruct((bsz, NSA_KV_HEADS, n_t, HEAD_DIM, PROJ_TILE), BF16),
            jax.ShapeDtypeStruct((bsz, NSA_KV_HEADS, seq_len // Q_BLOCK, HEAD_DIM, Q_BLOCK), BF16),
            jax.ShapeDtypeStruct((bsz, NSA_KV_HEADS, n_t, PROJ_TILE, 128), BF16),
            jax.ShapeDtypeStruct((bsz, NSA_KV_HEADS, seq_len, 128), BF16),
            jax.ShapeDtypeStruct((bsz, seq_len, 128), F32),
            jax.ShapeDtypeStruct((bsz, seq_len, 128), F32),
            jax.ShapeDtypeStruct((bsz, seq_len, 128), F32),
        ],
        compiler_params=pltpu.CompilerParams(dimension_semantics=("arbitrary", "arbitrary"),
                                             vmem_limit_bytes=V7X_VMEM_LIMIT_BYTES),
        name="proj_kv",
    )(x, w_kv, pos_col3, _inv_freq_row())


def _compress_kernel(ck_ref, cv_ref, pek_ref, pev_ref, w1k_ref, w1v_ref, w2k_ref, w2v_ref, pos_ref, inv_ref,
                     kct_ref, vcd_ref):
    def hidden(c_ref, pe_ref, w1_ref):
        c = c_ref[0]
        lo = jnp.dot((c + pe_ref[0]).astype(BF16), w1_ref[0], preferred_element_type=F32)
        hi = jnp.dot((c + pe_ref[1]).astype(BF16), w1_ref[1], preferred_element_type=F32)
        hi_next = jnp.concatenate([hi[1:], jnp.zeros((1, hi.shape[1]), F32)], axis=0)
        return jax.nn.gelu(lo + hi_next).astype(BF16)

    kc = jnp.dot(hidden(ck_ref, pek_ref, w1k_ref), w2k_ref[...], preferred_element_type=F32)
    cos, sin = _rope_tables(pos_ref[0], inv_ref[...])
    kc_t = _rope_lanes(kc, cos, sin).T
    vc = jnp.dot(hidden(cv_ref, pev_ref, w1v_ref), w2v_ref[...], preferred_element_type=F32)
    for k in range(NSA_KV_HEADS):
        kct_ref[0, k] = kc_t[k * HEAD_DIM:(k + 1) * HEAD_DIM].astype(BF16)
        vcd_ref[0, k] = vc[:, k * 128:(k + 1) * 128].astype(BF16)


def compress_kv(kc_raw, vc_raw, positions, cmp_pos_k, cmp_pos_v, w_k1, w_k2, w_v1, w_v2):
    bsz, seq_len, _ = kc_raw.shape
    n_chunk = seq_len // CMP_STRIDE
    width = CMP_STRIDE * 128
    eye = jnp.eye(NSA_KV_HEADS, dtype=F32)

    def chunk_pe(pe):
        pe = pe.reshape(2, CMP_STRIDE, 1, HEAD_DIM)
        return jnp.broadcast_to(pe, (2, CMP_STRIDE, NSA_KV_HEADS, HEAD_DIM)).reshape(2, 1, width)

    def chunk_w1(w1):
        hid = w1.shape[1]
        w = w1.reshape(2, CMP_STRIDE, HEAD_DIM, hid)
        return jnp.einsum('htdj,kc->htkdcj', w, eye).reshape(2, width, NSA_KV_HEADS * hid).astype(BF16)

    hid = w_k2.shape[0]
    w2k = jnp.einsum('jd,kc->kjcd', w_k2, eye).reshape(NSA_KV_HEADS * hid, NSA_KV_HEADS * HEAD_DIM).astype(BF16)
    w2v = jnp.einsum('jd,kc,r->kjcrd', w_v2, eye, jnp.ones((2,), F32)).reshape(
        NSA_KV_HEADS * hid, NSA_KV_HEADS * 128).astype(BF16)
    pos = positions.astype(F32).reshape(bsz, n_chunk, CMP_STRIDE).sum(-1)
    pos_next = jnp.concatenate([pos[:, 1:], pos[:, -1:]], axis=1)
    cmp_pos = ((pos + pos_next) / CMP_BLOCK).reshape(bsz, n_chunk, 1)
    return pl.pallas_call(
        _compress_kernel,
        grid=(bsz,),
        in_specs=[pl.BlockSpec((1, n_chunk, width), lambda b: (b, 0, 0)),
                  pl.BlockSpec((1, n_chunk, width), lambda b: (b, 0, 0)),
                  pl.BlockSpec((2, 1, width), lambda b: (0, 0, 0)),
                  pl.BlockSpec((2, 1, width), lambda b: (0, 0, 0)),
                  pl.BlockSpec((2, width, NSA_KV_HEADS * hid), lambda b: (0, 0, 0)),
                  pl.BlockSpec((2, width, NSA_KV_HEADS * hid), lambda b: (0, 0, 0)),
                  pl.BlockSpec((NSA_KV_HEADS * hid, NSA_KV_HEADS * HEAD_DIM), lambda b: (0, 0)),
                  pl.BlockSpec((NSA_KV_HEADS * hid, NSA_KV_HEADS * 128), lambda b: (0, 0)),
                  pl.BlockSpec((1, n_chunk, 1), lambda b: (b, 0, 0)),
                  pl.BlockSpec((1, 128), lambda b: (0, 0))],
        out_specs=[pl.BlockSpec((1, NSA_KV_HEADS, HEAD_DIM, n_chunk), lambda b: (b, 0, 0, 0)),
                   pl.BlockSpec((1, NSA_KV_HEADS, n_chunk, 128), lambda b: (b, 0, 0, 0))],
        out_shape=[jax.ShapeDtypeStruct((bsz, NSA_KV_HEADS, HEAD_DIM, n_chunk), BF16),
                   jax.ShapeDtypeStruct((bsz, NSA_KV_HEADS, n_chunk, 128), BF16)],
        compiler_params=pltpu.CompilerParams(dimension_semantics=("arbitrary",),
                                             vmem_limit_bytes=V7X_VMEM_LIMIT_BYTES),
        name="compress_kv",
    )(kc_raw.reshape(bsz, n_chunk, width), vc_raw.reshape(bsz, n_chunk, width), chunk_pe(cmp_pos_k),
      chunk_pe(cmp_pos_v), chunk_w1(w_k1), chunk_w1(w_v1), w2k, w2v, cmp_pos, _inv_freq_row())


def _out_ln_kernel(y_ref, a_ref, x_ref, wglu_ref, wout_ref, g_ref, b_ref, o_ref):
    y = y_ref[...]
    y_ssm = y * jax.nn.sigmoid(jnp.dot(y.astype(BF16), wglu_ref[...], preferred_element_type=F32))
    mix = (jnp.dot(y_ssm.astype(BF16), wout_ref[:SSM_WIDTH, :], preferred_element_type=F32)
           + jnp.dot(a_ref[...].astype(BF16), wout_ref[SSM_WIDTH:, :], preferred_element_type=F32))
    o_ref[...] = _layer_norm(DEEPNORM_ALPHA * x_ref[...] + mix, g_ref[...], b_ref[...])


def out_proj_ln(y_s5, y_nsa, xt, w_glu, w_out, ln_g, ln_b, tm=256):
    n_tok = xt.shape[0]
    row = lambda i: (i, 0)
    const = lambda i: (0, 0)
    return pl.pallas_call(
        _out_ln_kernel,
        grid=(n_tok // tm,),
        in_specs=[pl.BlockSpec((tm, SSM_WIDTH), row), pl.BlockSpec((tm, NSA_WIDTH), row),
                  pl.BlockSpec((tm, D_MODEL), row), pl.BlockSpec((SSM_WIDTH, SSM_WIDTH), const),
                  pl.BlockSpec((D_MODEL, D_MODEL), const), pl.BlockSpec((1, D_MODEL), const),
                  pl.BlockSpec((1, D_MODEL), const)],
        out_specs=pl.BlockSpec((tm, D_MODEL), row),
        out_shape=jax.ShapeDtypeStruct((n_tok, D_MODEL), F32),
        compiler_params=pltpu.CompilerParams(dimension_semantics=("arbitrary",),
                                             vmem_limit_bytes=V7X_VMEM_LIMIT_BYTES),
        name="out_proj_ln",
    )(y_s5, y_nsa, xt, w_glu.astype(BF16), w_out.astype(BF16), ln_g.reshape(1, D_MODEL), ln_b.reshape(1, D_MODEL))


S5_CHUNK = 512
S5_SUB = S5_CHUNK // 8
S5_GROUPS_PER_BLOCK = 8
S5_STATES = S5_GROUPS_PER_BLOCK * SSM_STATE


def _cmul_add(ar, ai, xr, xi, br, bi):
    return ar * xr - ai * xi + br, ar * xi + ai * xr + bi


def _s5_kernel(u_ref, lam_ref, bmat_ref, cmat_ref, d_ref, perm_ref, permt_ref, o_ref,
               xr_scr, xi_scr, pr_scr, pi_scr, carry_scr, a_scr, bbar_scr):
    c = pl.program_id(2)

    @pl.when(c == 0)
    def _():
        lr, li = lam_ref[0, 0:1, :], lam_ref[0, 1:2, :]
        dt = jnp.exp(lam_ref[0, 2:3, :])
        mag = jnp.exp(lr * dt)
        ar, ai = mag * jnp.cos(li * dt), mag * jnp.sin(li * dt)
        zr, zi = ar - 1.0, ai
        den = lr * lr + li * li
        fr, fi = (zr * lr + zi * li) / den, (zi * lr - zr * li) / den
        a_scr[0:1, :] = ar
        a_scr[1:2, :] = ai
        b_re, b_im = bmat_ref[0, 0], bmat_ref[0, 1]
        bbar_scr[0] = (fr * b_re - fi * b_im).astype(BF16)
        bbar_scr[1] = (fr * b_im + fi * b_re).astype(BF16)
        carry_scr[...] = jnp.zeros(carry_scr.shape, F32)
        a_re0 = jnp.broadcast_to(ar, (8, S5_STATES))
        a_im0 = jnp.broadcast_to(ai, (8, S5_STATES))

        def pw_body(i, pw):
            pr, pi = pw
            pr_scr[i] = pr
            pi_scr[i] = pi
            return a_re0 * pr - a_im0 * pi, a_re0 * pi + a_im0 * pr

        lax.fori_loop(0, S5_SUB, pw_body, (a_re0, a_im0))

    a_re = jnp.broadcast_to(a_scr[0:1, :], (8, S5_STATES))
    a_im = jnp.broadcast_to(a_scr[1:2, :], (8, S5_STATES))
    u = u_ref[0]
    perm = perm_ref[...]
    u_p = jnp.dot(perm, u.astype(BF16), preferred_element_type=F32).astype(BF16)
    xr_scr[...] = jnp.dot(u_p, bbar_scr[0], preferred_element_type=F32)
    xi_scr[...] = jnp.dot(u_p, bbar_scr[1], preferred_element_type=F32)

    def scan_body(i, x):
        row = pl.multiple_of(i * 8, 8)
        xr, xi = _cmul_add(a_re, a_im, x[0], x[1], xr_scr[pl.ds(row, 8), :], xi_scr[pl.ds(row, 8), :])
        xr_scr[pl.ds(row, 8), :] = xr
        xi_scr[pl.ds(row, 8), :] = xi
        return xr, xi

    zero = jnp.zeros((8, S5_STATES), F32)
    er, ei = lax.fori_loop(0, S5_SUB, scan_body, (zero, zero), unroll=4)

    ar_s = pr_scr[S5_SUB - 1][0:1]
    ai_s = pi_scr[S5_SUB - 1][0:1]
    rows_r = [carry_scr[0:1, :]]
    rows_i = [carry_scr[1:2, :]]
    for j in range(8):
        nr, ni = _cmul_add(ar_s, ai_s, rows_r[-1], rows_i[-1], er[j:j + 1], ei[j:j + 1])
        rows_r.append(nr)
        rows_i.append(ni)
    carry_scr[0:1, :] = rows_r[8]
    carry_scr[1:2, :] = rows_i[8]
    cr = jnp.concatenate(rows_r[:8], axis=0)
    ci = jnp.concatenate(rows_i[:8], axis=0)

    def fix_body(i, carry):
        row = pl.multiple_of(i * 8, 8)
        xr, xi = _cmul_add(pr_scr[i], pi_scr[i], cr, ci, xr_scr[pl.ds(row, 8), :], xi_scr[pl.ds(row, 8), :])
        xr_scr[pl.ds(row, 8), :] = xr
        xi_scr[pl.ds(row, 8), :] = xi
        return carry

    lax.fori_loop(0, S5_SUB, fix_body, 0, unroll=4)

    y_p = (jnp.dot(xr_scr[...].astype(BF16), cmat_ref[0, 0], preferred_element_type=F32)
           - jnp.dot(xi_scr[...].astype(BF16), cmat_ref[0, 1], preferred_element_type=F32))
    y_hi = y_p.astype(BF16)
    y_lo = (y_p - y_hi.astype(F32)).astype(BF16)
    perm_t = permt_ref[...]
    y = jnp.dot(perm_t, y_hi, preferred_element_type=F32) + jnp.dot(perm_t, y_lo, preferred_element_type=F32)
    o_ref[0] = jax.nn.gelu(y + d_ref[0] * u)


def s5_scan(u, lam_re, lam_im, log_dt, b_re, b_im, c_re, c_im, d_skip):
    bsz, seq_len, _ = u.shape
    nb = SSM_GROUPS // S5_GROUPS_PER_BLOCK
    eye = jnp.eye(S5_GROUPS_PER_BLOCK, dtype=F32)

    def blockdiag_b(m):
        m = jnp.swapaxes(m, 1, 2).reshape(nb, S5_GROUPS_PER_BLOCK, SSM_CH_PER_GROUP, SSM_STATE)
        return jnp.einsum('nghp,gk->nghkp', m, eye).reshape(nb, 128, S5_STATES)

    def blockdiag_c(m):
        m = jnp.swapaxes(m, 1, 2).reshape(nb, S5_GROUPS_PER_BLOCK, SSM_STATE, SSM_CH_PER_GROUP)
        return jnp.einsum('ngph,gk->ngpkh', m, eye).reshape(nb, S5_STATES, 128)

    log_dt_states = jnp.broadcast_to(log_dt[:, None], lam_re.shape)
    lam = jnp.stack([m.reshape(nb, S5_STATES) for m in (lam_re, lam_im, log_dt_states)], axis=1)
    bmat = jnp.stack([blockdiag_b(b_re), blockdiag_b(b_im)], axis=1)
    cmat = jnp.stack([blockdiag_c(c_re), blockdiag_c(c_im)], axis=1).astype(BF16)
    d = d_skip.reshape(nb, 1, 128)
    r = np.arange(S5_CHUNK)
    perm = np.zeros((S5_CHUNK, S5_CHUNK), np.float32)
    perm[r, (r % 8) * S5_SUB + r // 8] = 1.0
    perm = jnp.asarray(perm, BF16)
    return pl.pallas_call(
        _s5_kernel,
        grid=(bsz, nb, seq_len // S5_CHUNK),
        in_specs=[
            pl.BlockSpec((1, S5_CHUNK, 128), lambda b, g, c: (b, c, g)),
            pl.BlockSpec((1, 3, S5_STATES), lambda b, g, c: (g, 0, 0)),
            pl.BlockSpec((1, 2, 128, S5_STATES), lambda b, g, c: (g, 0, 0, 0)),
            pl.BlockSpec((1, 2, S5_STATES, 128), lambda b, g, c: (g, 0, 0, 0)),
            pl.BlockSpec((1, 1, 128), lambda b, g, c: (g, 0, 0)),
            pl.BlockSpec((S5_CHUNK, S5_CHUNK), lambda b, g, c: (0, 0)),
            pl.BlockSpec((S5_CHUNK, S5_CHUNK), lambda b, g, c: (0, 0)),
        ],
        out_specs=pl.BlockSpec((1, S5_CHUNK, 128), lambda b, g, c: (b, c, g)),
        out_shape=jax.ShapeDtypeStruct((bsz, seq_len, SSM_WIDTH), F32),
        scratch_shapes=[pltpu.VMEM((S5_CHUNK, S5_STATES), F32), pltpu.VMEM((S5_CHUNK, S5_STATES), F32),
                        pltpu.VMEM((S5_SUB, 8, S5_STATES), F32), pltpu.VMEM((S5_SUB, 8, S5_STATES), F32),
                        pltpu.VMEM((2, S5_STATES), F32), pltpu.VMEM((2, S5_STATES), F32),
                        pltpu.VMEM((2, 128, S5_STATES), BF16)],
        compiler_params=pltpu.CompilerParams(
            dimension_semantics=("arbitrary", "arbitrary", "arbitrary"), vmem_limit_bytes=V7X_VMEM_LIMIT_BYTES),
        name="s5_scan",
    )(u, lam, bmat, cmat, d, perm, perm.T)


def _softmax_tile(s, m_old):
    m_new = jnp.maximum(m_old, jnp.max(s, axis=1, keepdims=True))
    m_wide = jnp.concatenate([m_new] * (s.shape[1] // 128), axis=1)
    return m_new, jnp.exp2(m_old - m_new), jnp.exp2(s - m_wide)


def _lane_is_low(shape):
    return lax.broadcasted_iota(jnp.int32, shape, 1) < HEAD_DIM


def _pad_kt(kt, variant):
    z = jnp.zeros_like(kt)
    return jnp.concatenate([kt, z] if variant == 0 else [z, kt], axis=0)


def _pad_v(vv, variant):
    low = _lane_is_low(vv.shape)
    keep = low if variant == 0 else jnp.logical_not(low)
    return jnp.where(keep, vv, jnp.ones_like(vv))


def _finish(acc, variant):
    lane = lax.broadcasted_iota(jnp.int32, acc.shape, 1)
    lsel = lane == (HEAD_DIM if variant == 0 else 0)
    l = jnp.sum(jnp.where(lsel, acc, 0.0), axis=1, keepdims=True)
    keep = (lane < HEAD_DIM) if variant == 0 else (lane >= HEAD_DIM)
    return jnp.where(keep, acc / l, 0.0)


def _nsa_kernel(q_ref, g_ref, kct_ref, vc_ref, kst_ref, vs_ref, kwt_ref, vw_ref, ovl_ref, gx_ref, o_ref,
                m_scr, acc_scr, s_scr_a, s_scr_b, p_scr, *, seq_len):
    s_slots = (s_scr_a, s_scr_b)
    n_sel = seq_len // SEL_BLOCK
    n_cpad = seq_len // CMP_STRIDE
    sel_tile = 512
    blocks_per_tile = sel_tile // SEL_BLOCK
    win_tiles = WINDOW // Q_BLOCK + 1
    n_pair = Q_PER_KV // 2
    rows = n_pair * Q_BLOCK
    i = pl.program_id(2)
    t0 = i * Q_BLOCK

    qb = q_ref[0]
    qst = jnp.concatenate([qb[:, p * 128:(p + 1) * 128] for p in range(n_pair)], axis=0)

    sig = jax.nn.sigmoid(g_ref[0])
    sig_hi = sig.astype(BF16)
    sig_lo = (sig - sig_hi.astype(F32)).astype(BF16)
    gx = gx_ref[0]
    gexp = (jnp.dot(sig_hi, gx, preferred_element_type=F32) + jnp.dot(sig_lo, gx, preferred_element_type=F32))

    def gate_of(branch):
        base = branch * n_pair * 128
        return jnp.concatenate([gexp[:, base + p * 128: base + (p + 1) * 128] for p in range(n_pair)], axis=0)

    t_row = t0 + lax.broadcasted_iota(jnp.int32, (Q_BLOCK, 1), 0)

    slab = 64
    kct = kct_ref[0, 0]
    s_cmp = [jnp.dot(qst, _pad_kt(kct, v), preferred_element_type=F32) for v in range(2)]
    n_kblk = seq_len // Q_BLOCK
    w0 = jnp.clip(i - (win_tiles - 1), 0, n_kblk - win_tiles)
    kw = jnp.concatenate([kwt_ref[0, 0, w0 + j] for j in range(win_tiles)], axis=1)
    s_win = [jnp.dot(qst, _pad_kt(kw, v), preferred_element_type=F32) for v in range(2)]
    for v in range(2):
        s_slots[0][v] = jnp.dot(qst, _pad_kt(kst_ref[0, 0, 0], v), preferred_element_type=F32)

    n_iota = lax.broadcasted_iota(jnp.int32, (Q_BLOCK, n_cpad), 1)
    cmask = (n_iota * CMP_STRIDE + (CMP_BLOCK - 1)) <= t_row
    cmask4 = jnp.concatenate([cmask] * n_pair, axis=0)
    vcd = vc_ref[0, 0]
    p_sum = jnp.zeros((Q_BLOCK, n_cpad), F32)
    out = jnp.zeros((rows, 128), F32)
    o_c = jnp.zeros((rows, 128), F32)
    for v in range(2):
        s = jnp.where(cmask4, s_cmp[v], NEG)
        m = jnp.max(s, axis=1, keepdims=True)
        e = jnp.where(cmask4, jnp.exp2(s - m), 0.0)
        l = jnp.sum(e, axis=1, keepdims=True)
        p = e * (1.0 / jnp.maximum(l, 1e-30))
        for pp in range(n_pair):
            p_sum = p_sum + p[pp * Q_BLOCK:(pp + 1) * Q_BLOCK]
        low = _lane_is_low((n_cpad, 128))
        vz = jnp.where(low if v == 0 else jnp.logical_not(low), vcd, jnp.zeros_like(vcd))
        o_c = o_c + jnp.dot(p.astype(BF16), vz, preferred_element_type=F32)
    out = out + gate_of(0) * o_c

    ps_hi = p_sum.astype(BF16)
    ps_lo = (p_sum - ps_hi.astype(F32)).astype(BF16)
    ovl = ovl_ref[...]
    nt = (((1,), (1,)), ((), ()))
    imp_t = (lax.dot_general(ovl, ps_hi, nt, preferred_element_type=F32)
             + lax.dot_general(ovl, ps_lo, nt, preferred_element_type=F32))

    vw = jnp.concatenate([vw_ref[0, 0, w0 + j] for j in range(win_tiles)], axis=0)
    kpos_w = w0 * Q_BLOCK + lax.broadcasted_iota(jnp.int32, (Q_BLOCK, win_tiles * Q_BLOCK), 1)
    diff = t_row - kpos_w
    wbias = jnp.where((diff >= 0) & (diff < WINDOW), 0.0, NEG)
    wbias4 = jnp.concatenate([wbias] * n_pair, axis=0)
    o_w = jnp.zeros((rows, 128), F32)
    for v in range(2):
        s = s_win[v] + wbias4
        m = jnp.max(s, axis=1, keepdims=True)
        p = jnp.exp2(s - m)
        o_w = o_w + _finish(jnp.dot(p.astype(BF16), _pad_v(vw, v), preferred_element_type=F32), v)
    out = out + gate_of(2) * o_w

    s_iota = lax.broadcasted_iota(jnp.int32, (n_sel, Q_BLOCK), 0)
    t_lane = t0 + lax.broadcasted_iota(jnp.int32, (n_sel, Q_BLOCK), 1)
    cur = t_lane // SEL_BLOCK
    forced = (s_iota == 0) | (s_iota == cur) | (s_iota == cur - 1)
    valid = s_iota * SEL_BLOCK <= t_lane
    score = jnp.where(forced, FORCE, jnp.where(valid, imp_t, -1.0))
    s_f = s_iota.astype(F32)
    sel_t = jnp.zeros((n_sel, Q_BLOCK), F32)
    for _ in range(min(SEL_TOPK, n_sel)):
        mx = jnp.max(score, axis=0, keepdims=True)
        idx = jnp.min(jnp.where(score == mx, s_f, float(n_sel)), axis=0, keepdims=True)
        hit = s_f == idx
        sel_t = jnp.where(hit, 1.0, sel_t)
        score = jnp.where(hit, -3e38, score)
    selmask = sel_t.T.astype(BF16)

    m_scr[...] = jnp.full(m_scr.shape, NEG, F32)
    acc_scr[...] = jnp.zeros(acc_scr.shape, F32)
    n_tiles = (t0 + Q_BLOCK + sel_tile - 1) // sel_tile

    last_tile = seq_len // sel_tile - 1

    def bias_of(kt):
        blk = kt * blocks_per_tile + lax.broadcasted_iota(jnp.int32, (n_sel, sel_tile), 1) // SEL_BLOCK
        expand = (lax.broadcasted_iota(jnp.int32, (n_sel, sel_tile), 0) == blk).astype(BF16)
        selexp = jnp.dot(selmask, expand, preferred_element_type=F32)
        kpos = kt * sel_tile + lax.broadcasted_iota(jnp.int32, (Q_BLOCK, sel_tile), 1)
        bias = jnp.where((selexp > 0.5) & (kpos <= t_row), 0.0, NEG)
        return jnp.concatenate([bias] * n_pair, axis=0)

    def scores_into(slot, kt):
        bias4 = bias_of(kt)
        kt_tile = kst_ref[0, 0, jnp.minimum(kt, last_tile)]
        for v in range(2):
            s_slots[slot][v] = jnp.dot(qst, _pad_kt(kt_tile, v), preferred_element_type=F32) + bias4

    def attend_from(slot, kt):
        v_tile = vs_ref[0, 0, jnp.minimum(kt, last_tile)]
        for v in range(2):
            for h in range(rows // slab):
                r = slice(h * slab, (h + 1) * slab)
                m_new, alpha, p = _softmax_tile(s_slots[slot][v, r, :], m_scr[v, r, :])
                m_scr[v, r, :] = m_new
                acc_scr[v, r, :] = alpha * acc_scr[v, r, :]
                p_scr[v, r, :] = p.astype(BF16)
            acc_scr[v] += jnp.dot(p_scr[v], _pad_v(v_tile, v), preferred_element_type=F32)

    for v in range(2):
        s_slots[0][v] = s_slots[0][v] + bias_of(0)

    def sel_body(j, carry):
        kt = 2 * j
        scores_into(1, kt + 1)
        attend_from(0, kt)
        scores_into(0, kt + 2)
        attend_from(1, kt + 1)
        return carry

    lax.fori_loop(0, (n_tiles + 1) // 2, sel_body, 0)
    out = out + gate_of(1) * (_finish(acc_scr[0], 0) + _finish(acc_scr[1], 1))

    o_ref[0] = jnp.concatenate([out[p * Q_BLOCK:(p + 1) * Q_BLOCK] for p in range(n_pair)], axis=1)


def nsa_attention(q, gate_pad, kct, vc, kst, vs, kwt, vw):
    bsz, seq_len, _ = q.shape
    n_sel = seq_len // SEL_BLOCK
    n_cpad = seq_len // CMP_STRIDE
    n_cmp = (seq_len - CMP_BLOCK) // CMP_STRIDE + 1
    n_pair = Q_PER_KV // 2
    cs = np.arange(n_cpad) * CMP_STRIDE
    ce = cs + CMP_BLOCK - 1
    ss = np.arange(n_sel) * SEL_BLOCK
    se = ss + SEL_BLOCK - 1
    ovl = (cs[None, :] <= se[:, None]) & (ce[None, :] >= ss[:, None]) & (np.arange(n_cpad)[None, :] < n_cmp)
    ovl = jnp.asarray(ovl.astype(np.float32), BF16)
    gx = np.zeros((NSA_KV_HEADS, 128, N_BRANCH * n_pair * 128), np.float32)
    for k in range(NSA_KV_HEADS):
        for hl in range(Q_PER_KV):
            for br in range(N_BRANCH):
                c0 = br * n_pair * 128 + hl * HEAD_DIM
                gx[k, (k * Q_PER_KV + hl) * N_BRANCH + br, c0:c0 + HEAD_DIM] = 1.0
    gx = jnp.asarray(gx, BF16)
    width = Q_PER_KV * HEAD_DIM
    full = lambda *shape: pl.BlockSpec((1, 1) + shape, lambda b, k, i: (b, k) + (0,) * len(shape))
    return pl.pallas_call(
        functools.partial(_nsa_kernel, seq_len=seq_len),
        grid=(bsz, NSA_KV_HEADS, seq_len // Q_BLOCK),
        in_specs=[
            pl.BlockSpec((1, Q_BLOCK, width), lambda b, k, i: (b, i, k)),
            pl.BlockSpec((1, Q_BLOCK, 128), lambda b, k, i: (b, i, 0)),
            full(HEAD_DIM, n_cpad), full(n_cpad, 128),
            full(seq_len // 512, HEAD_DIM, 512), full(seq_len // 512, 512, 128),
            full(seq_len // Q_BLOCK, HEAD_DIM, Q_BLOCK), full(seq_len // Q_BLOCK, Q_BLOCK, 128),
            pl.BlockSpec((n_sel, n_cpad), lambda b, k, i: (0, 0)),
            pl.BlockSpec((1, 128, N_BRANCH * n_pair * 128), lambda b, k, i: (k, 0, 0)),
        ],
        out_specs=pl.BlockSpec((1, Q_BLOCK, width), lambda b, k, i: (b, i, k)),
        out_shape=jax.ShapeDtypeStruct((bsz, seq_len, NSA_WIDTH), F32),
        scratch_shapes=[pltpu.VMEM((2, n_pair * Q_BLOCK, 128), F32), pltpu.VMEM((2, n_pair * Q_BLOCK, 128), F32),
                        pltpu.VMEM((2, n_pair * Q_BLOCK, 512), F32), pltpu.VMEM((2, n_pair * Q_BLOCK, 512), F32),
                        pltpu.VMEM((2, n_pair * Q_BLOCK, 512), BF16)],
        compiler_params=pltpu.CompilerParams(
            dimension_semantics=("arbitrary", "arbitrary", "arbitrary"), vmem_limit_bytes=V7X_VMEM_LIMIT_BYTES),
        name="nsa_attention",
    )(q, gate_pad, kct, vc, kst, vs, kwt, vw, ovl, gx)


ROUTER_TILE = 512
MOE_TILE = 1024
MOE_SUB = 256
MOE_ROWS = 48
MOE_SLOT = 64
MOE_GROUP = 4


def _first_max_mask(x, idx_f, axis):
    mx = jnp.max(x, axis=axis, keepdims=True)
    first = jnp.min(jnp.where(x == mx, idx_f, 1e9), axis=axis, keepdims=True)
    return idx_f == first, mx


def _router_kernel(x_ref, wrt_ref, bias_ref, w_ref, sel_ref):
    per_group = N_EXPERTS // N_EXPERT_GROUPS
    tr = x_ref.shape[0]
    nt = (((1,), (1,)), ((), ()))
    logits = lax.dot_general(wrt_ref[...], x_ref[...].astype(BF16), nt, preferred_element_type=F32)
    aff = jax.nn.sigmoid(logits)
    biased = aff + bias_ref[...]
    grp = biased.reshape(N_EXPERT_GROUPS, per_group, tr)
    in_grp = lax.broadcasted_iota(jnp.int32, grp.shape, 1).astype(F32)
    hit1, m1 = _first_max_mask(grp, in_grp, 1)
    m2 = jnp.max(jnp.where(hit1, -jnp.inf, grp), axis=1, keepdims=True)
    gscore = (m1 + m2).reshape(N_EXPERT_GROUPS, tr)
    g_idx = lax.broadcasted_iota(jnp.int32, gscore.shape, 0).astype(F32)
    gsel = jnp.zeros(gscore.shape, F32)
    for _ in range(TOPK_GROUPS):
        hit, _ = _first_max_mask(gscore, g_idx, 0)
        gsel = jnp.where(hit, 1.0, gsel)
        gscore = jnp.where(hit, -jnp.inf, gscore)
    gmask = jnp.broadcast_to(gsel.reshape(N_EXPERT_GROUPS, 1, tr), grp.shape).reshape(N_EXPERTS, tr)
    cand = jnp.where(gmask > 0.5, biased, NEG)
    e_idx = lax.broadcasted_iota(jnp.int32, cand.shape, 0).astype(F32)
    sel = jnp.zeros(cand.shape, F32)
    for _ in range(TOP_K):
        hit, _ = _first_max_mask(cand, e_idx, 0)
        sel = jnp.where(hit, 1.0, sel)
        cand = jnp.where(hit, -jnp.inf, cand)
    w = jnp.where(sel > 0.5, aff, 0.0)
    w_ref[...] = w / jnp.sum(w, axis=0, keepdims=True) * ROUTED_SCALE
    sel_ref[...] = sel


def moe_router(xt, w_router, router_bias):
    n_tok = xt.shape[0]
    wrt = w_router.T.astype(BF16)
    return pl.pallas_call(
        _router_kernel,
        grid=(n_tok // ROUTER_TILE,),
        in_specs=[pl.BlockSpec((ROUTER_TILE, D_MODEL), lambda i: (i, 0)),
                  pl.BlockSpec((N_EXPERTS, D_MODEL), lambda i: (0, 0)),
                  pl.BlockSpec((N_EXPERTS, 1), lambda i: (0, 0))],
        out_specs=[pl.BlockSpec((N_EXPERTS, ROUTER_TILE), lambda i: (0, i)),
                   pl.BlockSpec((N_EXPERTS, ROUTER_TILE), lambda i: (0, i))],
        out_shape=[jax.ShapeDtypeStruct((N_EXPERTS, n_tok), F32), jax.ShapeDtypeStruct((N_EXPERTS, n_tok), F32)],
        compiler_params=pltpu.CompilerParams(dimension_semantics=("arbitrary",),
                                             vmem_limit_bytes=V7X_VMEM_LIMIT_BYTES),
        name="moe_router",
    )(xt, wrt, router_bias.reshape(N_EXPERTS, 1))


def _moe_kernel(cnt_ref, x_ref, sel_ref, w_ref, init_ref, wg_hbm, wu_hbm, wd_hbm, lng_ref, lnb_ref, o_ref,
                rank_scr, ybuf_scr, sbuf_scr, wg_buf, wu_buf, wd_buf, sem):
    i = pl.program_id(0)
    tm = x_ref.shape[0]
    n_sub = tm // MOE_SUB
    tn = (((0,), (0,)), ((), ()))

    def weight_copies(e, slot):
        return (pltpu.make_async_copy(wg_hbm.at[e], wg_buf.at[slot], sem.at[slot, 0]),
                pltpu.make_async_copy(wu_hbm.at[e], wu_buf.at[slot], sem.at[slot, 1]),
                pltpu.make_async_copy(wd_hbm.at[e], wd_buf.at[slot], sem.at[slot, 2]))

    for cp in weight_copies(0, 0):
        cp.start()

    o_ref[...] = init_ref[...]
    before = (lax.broadcasted_iota(jnp.int32, (MOE_SUB, MOE_SUB), 0)
              < lax.broadcasted_iota(jnp.int32, (MOE_SUB, MOE_SUB), 1))
    before = jnp.where(before, 1.0, 0.0).astype(BF16)
    for q in range(n_sub):
        cols = slice(q * MOE_SUB, (q + 1) * MOE_SUB)
        rank_scr[:, cols] = jnp.dot(sel_ref[:, cols].astype(BF16), before, preferred_element_type=F32)

    def run_expert(e, wslot, gslot):
        count = cnt_ref[i * N_EXPERTS + e]
        sel_e = sel_ref[pl.ds(e, 1), :]
        rank_e = rank_scr[pl.ds(e, 1), :]
        w_e = w_ref[pl.ds(e, 1), :]

        def expert_pass(c):
            row = (c * MOE_ROWS + lax.broadcasted_iota(jnp.int32, (MOE_ROWS, MOE_SUB), 0)).astype(F32)
            scatters, xes = [], []
            for q in range(n_sub):
                cols = slice(q * MOE_SUB, (q + 1) * MOE_SUB)
                hit = (rank_e[:, cols] == row) & (sel_e[:, cols] > 0.5)
                gather = jnp.where(hit, 1.0, 0.0).astype(BF16)
                scatters.append(jnp.where(hit, w_e[:, cols], 0.0).astype(BF16))
                xes.append(jnp.dot(gather, x_ref[cols, :], preferred_element_type=F32).astype(BF16))
            xe = jnp.concatenate(xes, axis=0)
            g = jnp.dot(xe, wg_buf[wslot], preferred_element_type=F32)
            u = jnp.dot(xe, wu_buf[wslot], preferred_element_type=F32)
            h = (jax.nn.silu(g) * u).astype(BF16)
            return scatters, jnp.dot(h, wd_buf[wslot], preferred_element_type=F32).astype(BF16)

        scatters, y = expert_pass(0)
        spare = MOE_SLOT - MOE_ROWS
        for q in range(n_sub):
            sbuf_scr[q, gslot] = jnp.concatenate([scatters[q], jnp.zeros((spare, MOE_SUB), BF16)], axis=0)
            ybuf_scr[q, gslot] = jnp.concatenate(
                [y[q * MOE_ROWS:(q + 1) * MOE_ROWS], jnp.zeros((spare, D_MODEL), BF16)], axis=0)

        def overflow_body(c, carry):
            sc, yy = expert_pass(c)
            for q in range(n_sub):
                cols = slice(q * MOE_SUB, (q + 1) * MOE_SUB)
                o_ref[cols, :] += lax.dot_general(sc[q], yy[q * MOE_ROWS:(q + 1) * MOE_ROWS], tn,
                                                  preferred_element_type=F32)
            return carry

        lax.fori_loop(1, (count + MOE_ROWS - 1) // MOE_ROWS, overflow_body, 0)

    def group_body(j, carry):
        for k in range(MOE_GROUP):
            e = j * MOE_GROUP + k
            for cp in weight_copies(e, k % 2):
                cp.wait()

            @pl.when(e + 1 < N_EXPERTS)
            def _():
                for cp in weight_copies(e + 1, (k + 1) % 2):
                    cp.start()

            run_expert(e, k % 2, k)
        for q in range(n_sub):
            cols = slice(q * MOE_SUB, (q + 1) * MOE_SUB)
            o_ref[cols, :] += lax.dot_general(sbuf_scr[q].reshape(MOE_GROUP * MOE_SLOT, MOE_SUB),
                                              ybuf_scr[q].reshape(MOE_GROUP * MOE_SLOT, D_MODEL), tn,
                                              preferred_element_type=F32)
        return carry

    lax.fori_loop(0, N_EXPERTS // MOE_GROUP, group_body, 0)
    o_ref[...] = _layer_norm(o_ref[...], lng_ref[...], lnb_ref[...])


def moe_routed(x_bf16, sel_t, w_t, init, w_gate, w_up, w_down, ln_g, ln_b):
    n_tok = x_bf16.shape[0]
    n_tiles = n_tok // MOE_TILE
    per_sub = jnp.sum(sel_t.reshape(N_EXPERTS, n_tiles, MOE_TILE // MOE_SUB, MOE_SUB), axis=-1)
    cnt = jnp.max(per_sub, axis=-1).T.astype(jnp.int32).reshape(-1)
    grid_spec = pltpu.PrefetchScalarGridSpec(
        num_scalar_prefetch=1,
        grid=(n_tiles,),
        in_specs=[
            pl.BlockSpec((MOE_TILE, D_MODEL), lambda i, cnt: (i, 0), pipeline_mode=pl.Buffered(1)),
            pl.BlockSpec((N_EXPERTS, MOE_TILE), lambda i, cnt: (0, i)),
            pl.BlockSpec((N_EXPERTS, MOE_TILE), lambda i, cnt: (0, i)),
            pl.BlockSpec((MOE_TILE, D_MODEL), lambda i, cnt: (i, 0), pipeline_mode=pl.Buffered(1)),
            pl.BlockSpec(memory_space=pl.ANY),
            pl.BlockSpec(memory_space=pl.ANY),
            pl.BlockSpec(memory_space=pl.ANY),
            pl.BlockSpec((1, D_MODEL), lambda i, cnt: (0, 0)),
            pl.BlockSpec((1, D_MODEL), lambda i, cnt: (0, 0)),
        ],
        out_specs=pl.BlockSpec((MOE_TILE, D_MODEL), lambda i, cnt: (i, 0)),
        scratch_shapes=[pltpu.VMEM((N_EXPERTS, MOE_TILE), F32),
                        pltpu.VMEM((MOE_TILE // MOE_SUB, MOE_GROUP, MOE_SLOT, D_MODEL), BF16),
                        pltpu.VMEM((MOE_TILE // MOE_SUB, MOE_GROUP, MOE_SLOT, MOE_SUB), BF16),
                        pltpu.VMEM((2, D_MODEL, EXPERT_FF), BF16), pltpu.VMEM((2, D_MODEL, EXPERT_FF), BF16),
                        pltpu.VMEM((2, EXPERT_FF, D_MODEL), BF16), pltpu.SemaphoreType.DMA((2, 3))],
    )
    return pl.pallas_call(
        _moe_kernel,
        grid_spec=grid_spec,
        out_shape=jax.ShapeDtypeStruct((n_tok, D_MODEL), F32),
        compiler_params=pltpu.CompilerParams(dimension_semantics=("arbitrary",),
                                             vmem_limit_bytes=V7X_VMEM_LIMIT_BYTES),
        name="moe_routed",
    )(cnt, x_bf16, sel_t, w_t, init, w_gate, w_up, w_down, ln_g.reshape(1, D_MODEL), ln_b.reshape(1, D_MODEL))


def _shared_ffn_kernel(x_ref, wg_ref, wu_ref, wd_ref, o_ref, xb_ref):
    x = x_ref[...]
    xb = x.astype(BF16)
    h = jax.nn.silu(jnp.dot(xb, wg_ref[...], preferred_element_type=F32)) * jnp.dot(
        xb, wu_ref[...], preferred_element_type=F32)
    o_ref[...] = DEEPNORM_ALPHA * x + jnp.dot(h.astype(BF16), wd_ref[...], preferred_element_type=F32)
    xb_ref[...] = xb


def shared_ffn(xt, wg, wu, wd, tm=512):
    n_tok, d = xt.shape
    ff = wg.shape[1]
    return pl.pallas_call(
        _shared_ffn_kernel,
        grid=(n_tok // tm,),
        in_specs=[pl.BlockSpec((tm, d), lambda i: (i, 0)), pl.BlockSpec((d, ff), lambda i: (0, 0)),
                  pl.BlockSpec((d, ff), lambda i: (0, 0)), pl.BlockSpec((ff, d), lambda i: (0, 0))],
        out_specs=[pl.BlockSpec((tm, d), lambda i: (i, 0)), pl.BlockSpec((tm, d), lambda i: (i, 0))],
        out_shape=[jax.ShapeDtypeStruct((n_tok, d), F32), jax.ShapeDtypeStruct((n_tok, d), BF16)],
        compiler_params=pltpu.CompilerParams(dimension_semantics=("arbitrary",),
                                             vmem_limit_bytes=V7X_VMEM_LIMIT_BYTES),
        name="shared_ffn",
    )(xt, wg.astype(BF16), wu.astype(BF16), wd.astype(BF16))


def hybrid_layer(x, positions, w_in, lam_re, lam_im, log_dt, ssm_b_re, ssm_b_im, ssm_c_re, ssm_c_im, ssm_d,
                 w_glu, cmp_pos_k, cmp_pos_v, w_cmp_k1, w_cmp_k2, w_cmp_v1, w_cmp_v2, w_out, ln1_g, ln1_b,
                 w_router, router_bias, w_gate, w_up, w_down, ws_gate, ws_up, ws_down, ln2_g, ln2_b):
    bsz, L, _ = x.shape
    sizes = [SSM_WIDTH, NSA_WIDTH] + [KV_WIDTH] * 6 + [NSA_HEADS * N_BRANCH]
    o = [0] + [int(v) for v in np.cumsum(sizes)]
    col = lambda j: w_in[:, o[j]:o[j + 1]]
    dup = lambda w: jnp.concatenate([w[:, h * HEAD_DIM:(h + 1) * HEAD_DIM] for h in (0, 0, 1, 1)], axis=1)
    gate_cols = jnp.pad(col(8), ((0, 0), (0, 128 - NSA_HEADS * N_BRANCH)))
    w_uq = w_in[:, :o[2]].astype(BF16)
    w_kv = jnp.concatenate([col(4), col(6), dup(col(5)), dup(col(7)), col(2), col(3), gate_cols], axis=1).astype(BF16)

    xt = x.reshape(bsz * L, D_MODEL)
    u, q = proj_uq(xt, w_uq, positions.reshape(bsz * L, 1))
    kst, kwt, vs, vw, kc_raw, vc_raw, gate_pad = proj_kv(x, w_kv, positions.reshape(bsz, L, 1))
    kct, vcd = compress_kv(kc_raw, vc_raw, positions, cmp_pos_k, cmp_pos_v, w_cmp_k1, w_cmp_k2, w_cmp_v1, w_cmp_v2)
    y_s5 = s5_scan(u.reshape(bsz, L, SSM_WIDTH), lam_re, lam_im, log_dt, ssm_b_re, ssm_b_im, ssm_c_re, ssm_c_im, ssm_d)
    vw = vw.reshape(bsz, NSA_KV_HEADS, L // Q_BLOCK, Q_BLOCK, 128)
    y_nsa = nsa_attention(q.reshape(bsz, L, NSA_WIDTH), gate_pad, kct, vcd, kst, vs, kwt, vw)
    x1 = out_proj_ln(y_s5.reshape(bsz * L, SSM_WIDTH), y_nsa.reshape(bsz * L, NSA_WIDTH), xt, w_glu, w_out,
                     ln1_g, ln1_b)
    w_t, sel_t = moe_router(x1, w_router, router_bias)
    acc0, x1b = shared_ffn(x1, ws_gate, ws_up, ws_down)
    out = moe_routed(x1b, sel_t, w_t, acc0, w_gate.astype(BF16), w_up.astype(BF16), w_down.astype(BF16),
                     ln2_g, ln2_b)
    return out.reshape(bsz, L, D_MODEL)


def kernel(x, positions, w_in, lam_re, lam_im, log_dt, ssm_b_re, ssm_b_im, ssm_c_re, ssm_c_im, ssm_d, w_glu, cmp_pos_k, cmp_pos_v, w_cmp_k1, w_cmp_k2, w_cmp_v1, w_cmp_v2, w_out, ln1_g, ln1_b, w_router, router_bias, w_gate, w_up, w_down, ws_gate, ws_up, ws_down, ln2_g, ln2_b):
    params = (w_in, lam_re, lam_im, log_dt, ssm_b_re, ssm_b_im, ssm_c_re, ssm_c_im, ssm_d,
              w_glu, cmp_pos_k, cmp_pos_v, w_cmp_k1, w_cmp_k2, w_cmp_v1, w_cmp_v2, w_out, ln1_g, ln1_b,
              w_router, router_bias, w_gate, w_up, w_down, ws_gate, ws_up, ws_down, ln2_g, ln2_b)
    return hybrid_layer(x, positions, *(p[0] for p in params))
```

```python
import functools
import math

import numpy as np
import jax
import jax.numpy as jnp
from jax import lax
from jax.experimental import pallas as pl
from jax.experimental.pallas import tpu as pltpu

D_MODEL = 2048
SSM_WIDTH = 1024
SSM_CH_PER_GROUP = 16
SSM_GROUPS = 64
SSM_STATE = 64
NSA_HEADS = 16
NSA_KV_HEADS = 2
HEAD_DIM = 64
Q_PER_KV = NSA_HEADS // NSA_KV_HEADS
NSA_WIDTH = NSA_HEADS * HEAD_DIM
KV_WIDTH = NSA_KV_HEADS * HEAD_DIM
N_BRANCH = 3
CMP_BLOCK = 32
CMP_STRIDE = 16
SEL_BLOCK = 64
SEL_TOPK = 16
WINDOW = 512
Q_BLOCK = 128
ROPE_THETA = 10000.0
N_EXPERTS = 64
TOP_K = 8
N_EXPERT_GROUPS = 8
TOPK_GROUPS = 4
ROUTED_SCALE = 2.5
EXPERT_FF = 512
DEPTH = 1
DEEPNORM_ALPHA = (2.0 * DEPTH) ** 0.25
LN_EPS = 1e-5
NEG = -1e30
FORCE = 1e4
F32 = jnp.float32
BF16 = jnp.bfloat16

V7X_VMEM_LIMIT_BYTES = 56 * 1024 * 1024


def _layer_norm(x, g, b):
    mu = jnp.mean(x, -1, keepdims=True)
    var = jnp.mean(jnp.square(x - mu), -1, keepdims=True)
    return (x - mu) * lax.rsqrt(var + LN_EPS) * g + b


def _rope_tables(pos_col, inv_row):
    ang = pos_col * inv_row
    return jnp.cos(ang), jnp.sin(ang)


def _rope_lanes(x, cos, sin):
    lane = lax.broadcasted_iota(jnp.int32, (x.shape[0], 128), 1)
    first_half = (lane % HEAD_DIM) < HEAD_DIM // 2
    outs = []
    for blk in range(x.shape[1] // 128):
        xb = x[:, blk * 128:(blk + 1) * 128]
        rot = jnp.where(first_half, -pltpu.roll(xb, 128 - HEAD_DIM // 2, 1), pltpu.roll(xb, HEAD_DIM // 2, 1))
        outs.append(xb * cos + rot * sin)
    return outs[0] if len(outs) == 1 else jnp.concatenate(outs, axis=1)


def _inv_freq_row():
    half = HEAD_DIM // 2
    inv = ROPE_THETA ** (-jnp.arange(half, dtype=F32) / half)
    return jnp.tile(inv, 128 // half).reshape(1, 128)


PROJ_TILE = 512
Q_SCALE = HEAD_DIM ** -0.5 * math.log2(math.e)


def _proj_uq_kernel(x_ref, w_ref, pos_ref, inv_ref, u_ref, q_ref):
    acc = jnp.dot(x_ref[...].astype(BF16), w_ref[...], preferred_element_type=F32)
    u_ref[...] = acc[:, :SSM_WIDTH]
    cos, sin = _rope_tables(pos_ref[...].astype(F32), inv_ref[...])
    q_ref[...] = (_rope_lanes(acc[:, SSM_WIDTH:], cos, sin) * Q_SCALE).astype(BF16)


def proj_uq(xt, w_uq, pos_col):
    n_tok = xt.shape[0]
    return pl.pallas_call(
        _proj_uq_kernel,
        grid=(n_tok // PROJ_TILE,),
        in_specs=[pl.BlockSpec((PROJ_TILE, D_MODEL), lambda i: (i, 0)),
                  pl.BlockSpec((D_MODEL, SSM_WIDTH + NSA_WIDTH), lambda i: (0, 0)),
                  pl.BlockSpec((PROJ_TILE, 1), lambda i: (i, 0)),
                  pl.BlockSpec((1, 128), lambda i: (0, 0))],
        out_specs=[pl.BlockSpec((PROJ_TILE, SSM_WIDTH), lambda i: (i, 0)),
                   pl.BlockSpec((PROJ_TILE, NSA_WIDTH), lambda i: (i, 0))],
        out_shape=[jax.ShapeDtypeStruct((n_tok, SSM_WIDTH), F32), jax.ShapeDtypeStruct((n_tok, NSA_WIDTH), BF16)],
        compiler_params=pltpu.CompilerParams(dimension_semantics=("arbitrary",),
                                             vmem_limit_bytes=V7X_VMEM_LIMIT_BYTES),
        name="proj_uq",
    )(xt, w_uq, pos_col, _inv_freq_row())


KV_COLS = 4 * KV_WIDTH + 2 * 2 * KV_WIDTH + 128


def _proj_kv_kernel(x_ref, w_ref, pos_ref, inv_ref, kst_ref, kwt_ref, vs_ref, vw_ref, kc_ref, vc_ref, g_ref):
    acc = jnp.dot(x_ref[0].astype(BF16), w_ref[...], preferred_element_type=F32)
    cos, sin = _rope_tables(pos_ref[0].astype(F32), inv_ref[...])
    ks_t = _rope_lanes(acc[:, 0:128], cos, sin).T
    kw_t = _rope_lanes(acc[:, 128:256], cos, sin).T
    for k in range(NSA_KV_HEADS):
        kst_ref[0, k, 0] = ks_t[k * HEAD_DIM:(k + 1) * HEAD_DIM].astype(BF16)
        for j in range(PROJ_TILE // Q_BLOCK):
            kwt_ref[0, k, j] = kw_t[k * HEAD_DIM:(k + 1) * HEAD_DIM, j * Q_BLOCK:(j + 1) * Q_BLOCK].astype(BF16)
        vs_ref[0, k, 0] = acc[:, 256 + k * 128: 256 + (k + 1) * 128].astype(BF16)
        vw_ref[0, k] = acc[:, 512 + k * 128: 512 + (k + 1) * 128].astype(BF16)
    kc_ref[0] = acc[:, 768:896]
    vc_ref[0] = acc[:, 896:1024]
    g_ref[0] = acc[:, 1024:1152]


def proj_kv(x, w_kv, pos_col3):
    bsz, seq_len, _ = x.shape
    n_t = seq_len // PROJ_TILE
    per = PROJ_TILE // Q_BLOCK
    return pl.pallas_call(
        _proj_kv_kernel,
        grid=(bsz, n_t),
        in_specs=[pl.BlockSpec((1, PROJ_TILE, D_MODEL), lambda b, i: (b, i, 0)),
                  pl.BlockSpec((D_MODEL, KV_COLS), lambda b, i: (0, 0)),
                  pl.BlockSpec((1, PROJ_TILE, 1), lambda b, i: (b, i, 0)),
                  pl.BlockSpec((1, 128), lambda b, i: (0, 0))],
        out_specs=[
            pl.BlockSpec((1, NSA_KV_HEADS, 1, HEAD_DIM, PROJ_TILE), lambda b, i: (b, 0, i, 0, 0)),
            pl.BlockSpec((1, NSA_KV_HEADS, per, HEAD_DIM, Q_BLOCK), lambda b, i: (b, 0, i, 0, 0)),
            pl.BlockSpec((1, NSA_KV_HEADS, 1, PROJ_TILE, 128), lambda b, i: (b, 0, i, 0, 0)),
            pl.BlockSpec((1, NSA_KV_HEADS, PROJ_TILE, 128), lambda b, i: (b, 0, i, 0)),
            pl.BlockSpec((1, PROJ_TILE, 128), lambda b, i: (b, i, 0)),
            pl.BlockSpec((1, PROJ_TILE, 128), lambda b, i: (b, i, 0)),
            pl.BlockSpec((1, PROJ_TILE, 128), lambda b, i: (b, i, 0)),
        ],
        out_shape=[
            jax.ShapeDtypeStruct((bsz, NSA_KV_HEADS, n_t, HEAD_DIM, PROJ_TILE), BF16),
            jax.ShapeDtypeStruct((bsz, NSA_KV_HEADS, seq_len // Q_BLOCK, HEAD_DIM, Q_BLOCK), BF16),
            jax.ShapeDtypeStruct((bsz, NSA_KV_HEADS, n_t, PROJ_TILE, 128), BF16),
            jax.ShapeDtypeStruct((bsz, NSA_KV_HEADS, seq_len, 128), BF16),
            jax.ShapeDtypeStruct((bsz, seq_len, 128), F32),
            jax.ShapeDtypeStruct((bsz, seq_len, 128), F32),
            jax.ShapeDtypeStruct((bsz, seq_len, 128), F32),
        ],
        compiler_params=pltpu.CompilerParams(dimension_semantics=("arbitrary", "arbitrary"),
                                             vmem_limit_bytes=V7X_VMEM_LIMIT_BYTES),
        name="proj_kv",
    )(x, w_kv, pos_col3, _inv_freq_row())


def _compress_kernel(ck_ref, cv_ref, pek_ref, pev_ref, w1k_ref, w1v_ref, w2k_ref, w2v_ref, pos_ref, inv_ref,
                     kct_ref, vcd_ref):
    def hidden(c_ref, pe_ref, w1_ref):
        c = c_ref[0]
        lo = jnp.dot((c + pe_ref[0]).astype(BF16), w1_ref[0], preferred_element_type=F32)
        hi = jnp.dot((c + pe_ref[1]).astype(BF16), w1_ref[1], preferred_element_type=F32)
        hi_next = jnp.concatenate([hi[1:], jnp.zeros((1, hi.shape[1]), F32)], axis=0)
        return jax.nn.gelu(lo + hi_next).astype(BF16)

    kc = jnp.dot(hidden(ck_ref, pek_ref, w1k_ref), w2k_ref[...], preferred_element_type=F32)
    cos, sin = _rope_tables(pos_ref[0], inv_ref[...])
    kc_t = _rope_lanes(kc, cos, sin).T
    vc = jnp.dot(hidden(cv_ref, pev_ref, w1v_ref), w2v_ref[...], preferred_element_type=F32)
    for k in range(NSA_KV_HEADS):
        kct_ref[0, k] = kc_t[k * HEAD_DIM:(k + 1) * HEAD_DIM].astype(BF16)
        vcd_ref[0, k] = vc[:, k * 128:(k + 1) * 128].astype(BF16)


def compress_kv(kc_raw, vc_raw, positions, cmp_pos_k, cmp_pos_v, w_k1, w_k2, w_v1, w_v2):
    bsz, seq_len, _ = kc_raw.shape
    n_chunk = seq_len // CMP_STRIDE
    width = CMP_STRIDE * 128
    eye = jnp.eye(NSA_KV_HEADS, dtype=F32)

    def chunk_pe(pe):
        pe = pe.reshape(2, CMP_STRIDE, 1, HEAD_DIM)
        return jnp.broadcast_to(pe, (2, CMP_STRIDE, NSA_KV_HEADS, HEAD_DIM)).reshape(2, 1, width)

    def chunk_w1(w1):
        hid = w1.shape[1]
        w = w1.reshape(2, CMP_STRIDE, HEAD_DIM, hid)
        return jnp.einsum('htdj,kc->htkdcj', w, eye).reshape(2, width, NSA_KV_HEADS * hid).astype(BF16)

    hid = w_k2.shape[0]
    w2k = jnp.einsum('jd,kc->kjcd', w_k2, eye).reshape(NSA_KV_HEADS * hid, NSA_KV_HEADS * HEAD_DIM).astype(BF16)
    w2v = jnp.einsum('jd,kc,r->kjcrd', w_v2, eye, jnp.ones((2,), F32)).reshape(
        NSA_KV_HEADS * hid, NSA_KV_HEADS * 128).astype(BF16)
    pos = positions.astype(F32).reshape(bsz, n_chunk, CMP_STRIDE).sum(-1)
    pos_next = jnp.concatenate([pos[:, 1:], pos[:, -1:]], axis=1)
    cmp_pos = ((pos + pos_next) / CMP_BLOCK).reshape(bsz, n_chunk, 1)
    return pl.pallas_call(
        _compress_kernel,
        grid=(bsz,),
        in_specs=[pl.BlockSpec((1, n_chunk, width), lambda b: (b, 0, 0)),
                  pl.BlockSpec((1, n_chunk, width), lambda b: (b, 0, 0)),
                  pl.BlockSpec((2, 1, width), lambda b: (0, 0, 0)),
                  pl.BlockSpec((2, 1, width), lambda b: (0, 0, 0)),
                  pl.BlockSpec((2, width, NSA_KV_HEADS * hid), lambda b: (0, 0, 0)),
                  pl.BlockSpec((2, width, NSA_KV_HEADS * hid), lambda b: (0, 0, 0)),
                  pl.BlockSpec((NSA_KV_HEADS * hid, NSA_KV_HEADS * HEAD_DIM), lambda b: (0, 0)),
                  pl.BlockSpec((NSA_KV_HEADS * hid, NSA_KV_HEADS * 128), lambda b: (0, 0)),
                  pl.BlockSpec((1, n_chunk, 1), lambda b: (b, 0, 0)),
                  pl.BlockSpec((1, 128), lambda b: (0, 0))],
        out_specs=[pl.BlockSpec((1, NSA_KV_HEADS, HEAD_DIM, n_chunk), lambda b: (b, 0, 0, 0)),
                   pl.BlockSpec((1, NSA_KV_HEADS, n_chunk, 128), lambda b: (b, 0, 0, 0))],
        out_shape=[jax.ShapeDtypeStruct((bsz, NSA_KV_HEADS, HEAD_DIM, n_chunk), BF16),
                   jax.ShapeDtypeStruct((bsz, NSA_KV_HEADS, n_chunk, 128), BF16)],
        compiler_params=pltpu.CompilerParams(dimension_semantics=("arbitrary",),
                                             vmem_limit_bytes=V7X_VMEM_LIMIT_BYTES),
        name="compress_kv",
    )(kc_raw.reshape(bsz, n_chunk, width), vc_raw.reshape(bsz, n_chunk, width), chunk_pe(cmp_pos_k),
      chunk_pe(cmp_pos_v), chunk_w1(w_k1), chunk_w1(w_v1), w2k, w2v, cmp_pos, _inv_freq_row())


def _out_ln_kernel(y_ref, a_ref, x_ref, wglu_ref, wout_ref, g_ref, b_ref, o_ref):
    y = y_ref[...]
    y_ssm = y * jax.nn.sigmoid(jnp.dot(y.astype(BF16), wglu_ref[...], preferred_element_type=F32))
    mix = (jnp.dot(y_ssm.astype(BF16), wout_ref[:SSM_WIDTH, :], preferred_element_type=F32)
           + jnp.dot(a_ref[...].astype(BF16), wout_ref[SSM_WIDTH:, :], preferred_element_type=F32))
    o_ref[...] = _layer_norm(DEEPNORM_ALPHA * x_ref[...] + mix, g_ref[...], b_ref[...])


def out_proj_ln(y_s5, y_nsa, xt, w_glu, w_out, ln_g, ln_b, tm=256):
    n_tok = xt.shape[0]
    row = lambda i: (i, 0)
    const = lambda i: (0, 0)
    return pl.pallas_call(
        _out_ln_kernel,
        grid=(n_tok // tm,),
        in_specs=[pl.BlockSpec((tm, SSM_WIDTH), row), pl.BlockSpec((tm, NSA_WIDTH), row),
                  pl.BlockSpec((tm, D_MODEL), row), pl.BlockSpec((SSM_WIDTH, SSM_WIDTH), const),
                  pl.BlockSpec((D_MODEL, D_MODEL), const), pl.BlockSpec((1, D_MODEL), const),
                  pl.BlockSpec((1, D_MODEL), const)],
        out_specs=pl.BlockSpec((tm, D_MODEL), row),
        out_shape=jax.ShapeDtypeStruct((n_tok, D_MODEL), F32),
        compiler_params=pltpu.CompilerParams(dimension_semantics=("arbitrary",),
                                             vmem_limit_bytes=V7X_VMEM_LIMIT_BYTES),
        name="out_proj_ln",
    )(y_s5, y_nsa, xt, w_glu.astype(BF16), w_out.astype(BF16), ln_g.reshape(1, D_MODEL), ln_b.reshape(1, D_MODEL))


S5_CHUNK = 512
S5_SUB = S5_CHUNK // 8
S5_GROUPS_PER_BLOCK = 8
S5_STATES = S5_GROUPS_PER_BLOCK * SSM_STATE


def _cmul_add(ar, ai, xr, xi, br, bi):
    return ar * xr - ai * xi + br, ar * xi + ai * xr + bi


def _s5_kernel(u_ref, lam_ref, bmat_ref, cmat_ref, d_ref, perm_ref, permt_ref, o_ref,
               xr_scr, xi_scr, pr_scr, pi_scr, carry_scr, a_scr, bbar_scr):
    c = pl.program_id(2)

    @pl.when(c == 0)
    def _():
        lr, li = lam_ref[0, 0:1, :], lam_ref[0, 1:2, :]
        dt = jnp.exp(lam_ref[0, 2:3, :])
        mag = jnp.exp(lr * dt)
        ar, ai = mag * jnp.cos(li * dt), mag * jnp.sin(li * dt)
        zr, zi = ar - 1.0, ai
        den = lr * lr + li * li
        fr, fi = (zr * lr + zi * li) / den, (zi * lr - zr * li) / den
        a_scr[0:1, :] = ar
        a_scr[1:2, :] = ai
        b_re, b_im = bmat_ref[0, 0], bmat_ref[0, 1]
        bbar_scr[0] = (fr * b_re - fi * b_im).astype(BF16)
        bbar_scr[1] = (fr * b_im + fi * b_re).astype(BF16)
        carry_scr[...] = jnp.zeros(carry_scr.shape, F32)
        a_re0 = jnp.broadcast_to(ar, (8, S5_STATES))
        a_im0 = jnp.broadcast_to(ai, (8, S5_STATES))

        def pw_body(i, pw):
            pr, pi = pw
            pr_scr[i] = pr
            pi_scr[i] = pi
            return a_re0 * pr - a_im0 * pi, a_re0 * pi + a_im0 * pr

        lax.fori_loop(0, S5_SUB, pw_body, (a_re0, a_im0))

    a_re = jnp.broadcast_to(a_scr[0:1, :], (8, S5_STATES))
    a_im = jnp.broadcast_to(a_scr[1:2, :], (8, S5_STATES))
    u = u_ref[0]
    perm = perm_ref[...]
    u_p = jnp.dot(perm, u.astype(BF16), preferred_element_type=F32).astype(BF16)
    xr_scr[...] = jnp.dot(u_p, bbar_scr[0], preferred_element_type=F32)
    xi_scr[...] = jnp.dot(u_p, bbar_scr[1], preferred_element_type=F32)

    def scan_body(i, x):
        row = pl.multiple_of(i * 8, 8)
        xr, xi = _cmul_add(a_re, a_im, x[0], x[1], xr_scr[pl.ds(row, 8), :], xi_scr[pl.ds(row, 8), :])
        xr_scr[pl.ds(row, 8), :] = xr
        xi_scr[pl.ds(row, 8), :] = xi
        return xr, xi

    zero = jnp.zeros((8, S5_STATES), F32)
    er, ei = lax.fori_loop(0, S5_SUB, scan_body, (zero, zero), unroll=4)

    ar_s = pr_scr[S5_SUB - 1][0:1]
    ai_s = pi_scr[S5_SUB - 1][0:1]
    rows_r = [carry_scr[0:1, :]]
    rows_i = [carry_scr[1:2, :]]
    for j in range(8):
        nr, ni = _cmul_add(ar_s, ai_s, rows_r[-1], rows_i[-1], er[j:j + 1], ei[j:j + 1])
        rows_r.append(nr)
        rows_i.append(ni)
    carry_scr[0:1, :] = rows_r[8]
    carry_scr[1:2, :] = rows_i[8]
    cr = jnp.concatenate(rows_r[:8], axis=0)
    ci = jnp.concatenate(rows_i[:8], axis=0)

    def fix_body(i, carry):
        row = pl.multiple_of(i * 8, 8)
        xr, xi = _cmul_add(pr_scr[i], pi_scr[i], cr, ci, xr_scr[pl.ds(row, 8), :], xi_scr[pl.ds(row, 8), :])
        xr_scr[pl.ds(row, 8), :] = xr
        xi_scr[pl.ds(row, 8), :] = xi
        return carry

    lax.fori_loop(0, S5_SUB, fix_body, 0, unroll=4)

    y_p = (jnp.dot(xr_scr[...].astype(BF16), cmat_ref[0, 0], preferred_element_type=F32)
           - jnp.dot(xi_scr[...].astype(BF16), cmat_ref[0, 1], preferred_element_type=F32))
    y_hi = y_p.astype(BF16)
    y_lo = (y_p - y_hi.astype(F32)).astype(BF16)
    perm_t = permt_ref[...]
    y = jnp.dot(perm_t, y_hi, preferred_element_type=F32) + jnp.dot(perm_t, y_lo, preferred_element_type=F32)
    o_ref[0] = jax.nn.gelu(y + d_ref[0] * u)


def s5_scan(u, lam_re, lam_im, log_dt, b_re, b_im, c_re, c_im, d_skip):
    bsz, seq_len, _ = u.shape
    nb = SSM_GROUPS // S5_GROUPS_PER_BLOCK
    eye = jnp.eye(S5_GROUPS_PER_BLOCK, dtype=F32)

    def blockdiag_b(m):
        m = jnp.swapaxes(m, 1, 2).reshape(nb, S5_GROUPS_PER_BLOCK, SSM_CH_PER_GROUP, SSM_STATE)
        return jnp.einsum('nghp,gk->nghkp', m, eye).reshape(nb, 128, S5_STATES)

    def blockdiag_c(m):
        m = jnp.swapaxes(m, 1, 2).reshape(nb, S5_GROUPS_PER_BLOCK, SSM_STATE, SSM_CH_PER_GROUP)
        return jnp.einsum('ngph,gk->ngpkh', m, eye).reshape(nb, S5_STATES, 128)

    log_dt_states = jnp.broadcast_to(log_dt[:, None], lam_re.shape)
    lam = jnp.stack([m.reshape(nb, S5_STATES) for m in (lam_re, lam_im, log_dt_states)], axis=1)
    bmat = jnp.stack([blockdiag_b(b_re), blockdiag_b(b_im)], axis=1)
    cmat = jnp.stack([blockdiag_c(c_re), blockdiag_c(c_im)], axis=1).astype(BF16)
    d = d_skip.reshape(nb, 1, 128)
    r = np.arange(S5_CHUNK)
    perm = np.zeros((S5_CHUNK, S5_CHUNK), np.float32)
    perm[r, (r % 8) * S5_SUB + r // 8] = 1.0
    perm = jnp.asarray(perm, BF16)
    return pl.pallas_call(
        _s5_kernel,
        grid=(bsz, nb, seq_len // S5_CHUNK),
        in_specs=[
            pl.BlockSpec((1, S5_CHUNK, 128), lambda b, g, c: (b, c, g)),
            pl.BlockSpec((1, 3, S5_STATES), lambda b, g, c: (g, 0, 0)),
            pl.BlockSpec((1, 2, 128, S5_STATES), lambda b, g, c: (g, 0, 0, 0)),
            pl.BlockSpec((1, 2, S5_STATES, 128), lambda b, g, c: (g, 0, 0, 0)),
            pl.BlockSpec((1, 1, 128), lambda b, g, c: (g, 0, 0)),
            pl.BlockSpec((S5_CHUNK, S5_CHUNK), lambda b, g, c: (0, 0)),
            pl.BlockSpec((S5_CHUNK, S5_CHUNK), lambda b, g, c: (0, 0)),
        ],
        out_specs=pl.BlockSpec((1, S5_CHUNK, 128), lambda b, g, c: (b, c, g)),
        out_shape=jax.ShapeDtypeStruct((bsz, seq_len, SSM_WIDTH), F32),
        scratch_shapes=[pltpu.VMEM((S5_CHUNK, S5_STATES), F32), pltpu.VMEM((S5_CHUNK, S5_STATES), F32),
                        pltpu.VMEM((S5_SUB, 8, S5_STATES), F32), pltpu.VMEM((S5_SUB, 8, S5_STATES), F32),
                        pltpu.VMEM((2, S5_STATES), F32), pltpu.VMEM((2, S5_STATES), F32),
                        pltpu.VMEM((2, 128, S5_STATES), BF16)],
        compiler_params=pltpu.CompilerParams(
            dimension_semantics=("arbitrary", "arbitrary", "arbitrary"), vmem_limit_bytes=V7X_VMEM_LIMIT_BYTES),
        name="s5_scan",
    )(u, lam, bmat, cmat, d, perm, perm.T)


def _softmax_tile(s, m_old):
    m_new = jnp.maximum(m_old, jnp.max(s, axis=1, keepdims=True))
    m_wide = jnp.concatenate([m_new] * (s.shape[1] // 128), axis=1)
    return m_new, jnp.exp2(m_old - m_new), jnp.exp2(s - m_wide)


def _lane_is_low(shape):
    return lax.broadcasted_iota(jnp.int32, shape, 1) < HEAD_DIM


def _pad_kt(kt, variant):
    z = jnp.zeros_like(kt)
    return jnp.concatenate([kt, z] if variant == 0 else [z, kt], axis=0)


def _pad_v(vv, variant):
    low = _lane_is_low(vv.shape)
    keep = low if variant == 0 else jnp.logical_not(low)
    return jnp.where(keep, vv, jnp.ones_like(vv))


def _finish(acc, variant):
    lane = lax.broadcasted_iota(jnp.int32, acc.shape, 1)
    lsel = lane == (HEAD_DIM if variant == 0 else 0)
    l = jnp.sum(jnp.where(lsel, acc, 0.0), axis=1, keepdims=True)
    keep = (lane < HEAD_DIM) if variant == 0 else (lane >= HEAD_DIM)
    return jnp.where(keep, acc / l, 0.0)


def _nsa_kernel(q_ref, g_ref, kct_ref, vc_ref, kst_ref, vs_ref, kwt_ref, vw_ref, ovl_ref, gx_ref, o_ref,
                m_scr, acc_scr, s_scr_a, s_scr_b, p_scr, *, seq_len):
    s_slots = (s_scr_a, s_scr_b)
    n_sel = seq_len // SEL_BLOCK
    n_cpad = seq_len // CMP_STRIDE
    sel_tile = 512
    blocks_per_tile = sel_tile // SEL_BLOCK
    win_tiles = WINDOW // Q_BLOCK + 1
    n_pair = Q_PER_KV // 2
    rows = n_pair * Q_BLOCK
    i = pl.program_id(2)
    t0 = i * Q_BLOCK

    qb = q_ref[0]
    qst = jnp.concatenate([qb[:, p * 128:(p + 1) * 128] for p in range(n_pair)], axis=0)

    sig = jax.nn.sigmoid(g_ref[0])
    sig_hi = sig.astype(BF16)
    sig_lo = (sig - sig_hi.astype(F32)).astype(BF16)
    gx = gx_ref[0]
    gexp = (jnp.dot(sig_hi, gx, preferred_element_type=F32) + jnp.dot(sig_lo, gx, preferred_element_type=F32))

    def gate_of(branch):
        base = branch * n_pair * 128
        return jnp.concatenate([gexp[:, base + p * 128: base + (p + 1) * 128] for p in range(n_pair)], axis=0)

    t_row = t0 + lax.broadcasted_iota(jnp.int32, (Q_BLOCK, 1), 0)

    slab = 64
    kct = kct_ref[0, 0]
    s_cmp = [jnp.dot(qst, _pad_kt(kct, v), preferred_element_type=F32) for v in range(2)]
    n_kblk = seq_len // Q_BLOCK
    w0 = jnp.clip(i - (win_tiles - 1), 0, n_kblk - win_tiles)
    kw = jnp.concatenate([kwt_ref[0, 0, w0 + j] for j in range(win_tiles)], axis=1)
    s_win = [jnp.dot(qst, _pad_kt(kw, v), preferred_element_type=F32) for v in range(2)]
    for v in range(2):
        s_slots[0][v] = jnp.dot(qst, _pad_kt(kst_ref[0, 0, 0], v), preferred_element_type=F32)

    n_iota = lax.broadcasted_iota(jnp.int32, (Q_BLOCK, n_cpad), 1)
    cmask = (n_iota * CMP_STRIDE + (CMP_BLOCK - 1)) <= t_row
    cmask4 = jnp.concatenate([cmask] * n_pair, axis=0)
    vcd = vc_ref[0, 0]
    p_sum = jnp.zeros((Q_BLOCK, n_cpad), F32)
    out = jnp.zeros((rows, 128), F32)
    o_c = jnp.zeros((rows, 128), F32)
    for v in range(2):
        s = jnp.where(cmask4, s_cmp[v], NEG)
        m = jnp.max(s, axis=1, keepdims=True)
        e = jnp.where(cmask4, jnp.exp2(s - m), 0.0)
        l = jnp.sum(e, axis=1, keepdims=True)
        p = e * (1.0 / jnp.maximum(l, 1e-30))
        for pp in range(n_pair):
            p_sum = p_sum + p[pp * Q_BLOCK:(pp + 1) * Q_BLOCK]
        low = _lane_is_low((n_cpad, 128))
        vz = jnp.where(low if v == 0 else jnp.logical_not(low), vcd, jnp.zeros_like(vcd))
        o_c = o_c + jnp.dot(p.astype(BF16), vz, preferred_element_type=F32)
    out = out + gate_of(0) * o_c

    ps_hi = p_sum.astype(BF16)
    ps_lo = (p_sum - ps_hi.astype(F32)).astype(BF16)
    ovl = ovl_ref[...]
    nt = (((1,), (1,)), ((), ()))
    imp_t = (lax.dot_general(ovl, ps_hi, nt, preferred_element_type=F32)
             + lax.dot_general(ovl, ps_lo, nt, preferred_element_type=F32))

    vw = jnp.concatenate([vw_ref[0, 0, w0 + j] for j in range(win_tiles)], axis=0)
    kpos_w = w0 * Q_BLOCK + lax.broadcasted_iota(jnp.int32, (Q_BLOCK, win_tiles * Q_BLOCK), 1)
    diff = t_row - kpos_w
    wbias = jnp.where((diff >= 0) & (diff < WINDOW), 0.0, NEG)
    wbias4 = jnp.concatenate([wbias] * n_pair, axis=0)
    o_w = jnp.zeros((rows, 128), F32)
    for v in range(2):
        s = s_win[v] + wbias4
        m = jnp.max(s, axis=1, keepdims=True)
        p = jnp.exp2(s - m)
        o_w = o_w + _finish(jnp.dot(p.astype(BF16), _pad_v(vw, v), preferred_element_type=F32), v)
    out = out + gate_of(2) * o_w

    s_iota = lax.broadcasted_iota(jnp.int32, (n_sel, Q_BLOCK), 0)
    t_lane = t0 + lax.broadcasted_iota(jnp.int32, (n_sel, Q_BLOCK), 1)
    cur = t_lane // SEL_BLOCK
    forced = (s_iota == 0) | (s_iota == cur) | (s_iota == cur - 1)
    valid = s_iota * SEL_BLOCK <= t_lane
    score = jnp.where(forced, FORCE, jnp.where(valid, imp_t, -1.0))
    s_f = s_iota.astype(F32)
    sel_t = jnp.zeros((n_sel, Q_BLOCK), F32)
    for _ in range(min(SEL_TOPK, n_sel)):
        mx = jnp.max(score, axis=0, keepdims=True)
        idx = jnp.min(jnp.where(score == mx, s_f, float(n_sel)), axis=0, keepdims=True)
        hit = s_f == idx
        sel_t = jnp.where(hit, 1.0, sel_t)
        score = jnp.where(hit, -3e38, score)
    selmask = sel_t.T.astype(BF16)

    m_scr[...] = jnp.full(m_scr.shape, NEG, F32)
    acc_scr[...] = jnp.zeros(acc_scr.shape, F32)
    n_tiles = (t0 + Q_BLOCK + sel_tile - 1) // sel_tile

    last_tile = seq_len // sel_tile - 1

    def bias_of(kt):
        blk = kt * blocks_per_tile + lax.broadcasted_iota(jnp.int32, (n_sel, sel_tile), 1) // SEL_BLOCK
        expand = (lax.broadcasted_iota(jnp.int32, (n_sel, sel_tile), 0) == blk).astype(BF16)
        selexp = jnp.dot(selmask, expand, preferred_element_type=F32)
        kpos = kt * sel_tile + lax.broadcasted_iota(jnp.int32, (Q_BLOCK, sel_tile), 1)
        bias = jnp.where((selexp > 0.5) & (kpos <= t_row), 0.0, NEG)
        return jnp.concatenate([bias] * n_pair, axis=0)

    def scores_into(slot, kt):
        bias4 = bias_of(kt)
        kt_tile = kst_ref[0, 0, jnp.minimum(kt, last_tile)]
        for v in range(2):
            s_slots[slot][v] = jnp.dot(qst, _pad_kt(kt_tile, v), preferred_element_type=F32) + bias4

    def attend_from(slot, kt):
        v_tile = vs_ref[0, 0, jnp.minimum(kt, last_tile)]
        for v in range(2):
            for h in range(rows // slab):
                r = slice(h * slab, (h + 1) * slab)
                m_new, alpha, p = _softmax_tile(s_slots[slot][v, r, :], m_scr[v, r, :])
                m_scr[v, r, :] = m_new
                acc_scr[v, r, :] = alpha * acc_scr[v, r, :]
                p_scr[v, r, :] = p.astype(BF16)
            acc_scr[v] += jnp.dot(p_scr[v], _pad_v(v_tile, v), preferred_element_type=F32)

    for v in range(2):
        s_slots[0][v] = s_slots[0][v] + bias_of(0)

    def sel_body(j, carry):
        kt = 2 * j
        scores_into(1, kt + 1)
        attend_from(0, kt)
        scores_into(0, kt + 2)
        attend_from(1, kt + 1)
        return carry

    lax.fori_loop(0, (n_tiles + 1) // 2, sel_body, 0)
    out = out + gate_of(1) * (_finish(acc_scr[0], 0) + _finish(acc_scr[1], 1))

    o_ref[0] = jnp.concatenate([out[p * Q_BLOCK:(p + 1) * Q_BLOCK] for p in range(n_pair)], axis=1)


def nsa_attention(q, gate_pad, kct, vc, kst, vs, kwt, vw):
    bsz, seq_len, _ = q.shape
    n_sel = seq_len // SEL_BLOCK
    n_cpad = seq_len // CMP_STRIDE
    n_cmp = (seq_len - CMP_BLOCK) // CMP_STRIDE + 1
    n_pair = Q_PER_KV // 2
    cs = np.arange(n_cpad) * CMP_STRIDE
    ce = cs + CMP_BLOCK - 1
    ss = np.arange(n_sel) * SEL_BLOCK
    se = ss + SEL_BLOCK - 1
    ovl = (cs[None, :] <= se[:, None]) & (ce[None, :] >= ss[:, None]) & (np.arange(n_cpad)[None, :] < n_cmp)
    ovl = jnp.asarray(ovl.astype(np.float32), BF16)
    gx = np.zeros((NSA_KV_HEADS, 128, N_BRANCH * n_pair * 128), np.float32)
    for k in range(NSA_KV_HEADS):
        for hl in range(Q_PER_KV):
            for br in range(N_BRANCH):
                c0 = br * n_pair * 128 + hl * HEAD_DIM
                gx[k, (k * Q_PER_KV + hl) * N_BRANCH + br, c0:c0 + HEAD_DIM] = 1.0
    gx = jnp.asarray(gx, BF16)
    width = Q_PER_KV * HEAD_DIM
    full = lambda *shape: pl.BlockSpec((1, 1) + shape, lambda b, k, i: (b, k) + (0,) * len(shape))
    return pl.pallas_call(
        functools.partial(_nsa_kernel, seq_len=seq_len),
        grid=(bsz, NSA_KV_HEADS, seq_len // Q_BLOCK),
        in_specs=[
            pl.BlockSpec((1, Q_BLOCK, width), lambda b, k, i: (b, i, k)),
            pl.BlockSpec((1, Q_BLOCK, 128), lambda b, k, i: (b, i, 0)),
            full(HEAD_DIM, n_cpad), full(n_cpad, 128),
            full(seq_len // 512, HEAD_DIM, 512), full(seq_len // 512, 512, 128),
            full(seq_len // Q_BLOCK, HEAD_DIM, Q_BLOCK), full(seq_len // Q_BLOCK, Q_BLOCK, 128),
            pl.BlockSpec((n_sel, n_cpad), lambda b, k, i: (0, 0)),
            pl.BlockSpec((1, 128, N_BRANCH * n_pair * 128), lambda b, k, i: (k, 0, 0)),
        ],
        out_specs=pl.BlockSpec((1, Q_BLOCK, width), lambda b, k, i: (b, i, k)),
        out_shape=jax.ShapeDtypeStruct((bsz, seq_len, NSA_WIDTH), F32),
        scratch_shapes=[pltpu.VMEM((2, n_pair * Q_BLOCK, 128), F32), pltpu.VMEM((2, n_pair * Q_BLOCK, 128), F32),
                        pltpu.VMEM((2, n_pair * Q_BLOCK, 512), F32), pltpu.VMEM((2, n_pair * Q_BLOCK, 512), F32),
                        pltpu.VMEM((2, n_pair * Q_BLOCK, 512), BF16)],
        compiler_params=pltpu.CompilerParams(
            dimension_semantics=("arbitrary", "arbitrary", "arbitrary"), vmem_limit_bytes=V7X_VMEM_LIMIT_BYTES),
        name="nsa_attention",
    )(q, gate_pad, kct, vc, kst, vs, kwt, vw, ovl, gx)


ROUTER_TILE = 512
MOE_TILE = 1024
MOE_SUB = 256
MOE_ROWS = 48
MOE_SLOT = 64
MOE_GROUP = 4


def _first_max_mask(x, idx_f, axis):
    mx = jnp.max(x, axis=axis, keepdims=True)
    first = jnp.min(jnp.where(x == mx, idx_f, 1e9), axis=axis, keepdims=True)
    return idx_f == first, mx


def _router_kernel(x_ref, wrt_ref, bias_ref, w_ref, sel_ref):
    per_group = N_EXPERTS // N_EXPERT_GROUPS
    tr = x_ref.shape[0]
    nt = (((1,), (1,)), ((), ()))
    logits = lax.dot_general(wrt_ref[...], x_ref[...].astype(BF16), nt, preferred_element_type=F32)
    aff = jax.nn.sigmoid(logits)
    biased = aff + bias_ref[...]
    grp = biased.reshape(N_EXPERT_GROUPS, per_group, tr)
    in_grp = lax.broadcasted_iota(jnp.int32, grp.shape, 1).astype(F32)
    hit1, m1 = _first_max_mask(grp, in_grp, 1)
    m2 = jnp.max(jnp.where(hit1, -jnp.inf, grp), axis=1, keepdims=True)
    gscore = (m1 + m2).reshape(N_EXPERT_GROUPS, tr)
    g_idx = lax.broadcasted_iota(jnp.int32, gscore.shape, 0).astype(F32)
    gsel = jnp.zeros(gscore.shape, F32)
    for _ in range(TOPK_GROUPS):
        hit, _ = _first_max_mask(gscore, g_idx, 0)
        gsel = jnp.where(hit, 1.0, gsel)
        gscore = jnp.where(hit, -jnp.inf, gscore)
    gmask = jnp.broadcast_to(gsel.reshape(N_EXPERT_GROUPS, 1, tr), grp.shape).reshape(N_EXPERTS, tr)
    cand = jnp.where(gmask > 0.5, biased, NEG)
    e_idx = lax.broadcasted_iota(jnp.int32, cand.shape, 0).astype(F32)
    sel = jnp.zeros(cand.shape, F32)
    for _ in range(TOP_K):
        hit, _ = _first_max_mask(cand, e_idx, 0)
        sel = jnp.where(hit, 1.0, sel)
        cand = jnp.where(hit, -jnp.inf, cand)
    w = jnp.where(sel > 0.5, aff, 0.0)
    w_ref[...] = w / jnp.sum(w, axis=0, keepdims=True) * ROUTED_SCALE
    sel_ref[...] = sel


def moe_router(xt, w_router, router_bias):
    n_tok = xt.shape[0]
    wrt = w_router.T.astype(BF16)
    return pl.pallas_call(
        _router_kernel,
        grid=(n_tok // ROUTER_TILE,),
        in_specs=[pl.BlockSpec((ROUTER_TILE, D_MODEL), lambda i: (i, 0)),
                  pl.BlockSpec((N_EXPERTS, D_MODEL), lambda i: (0, 0)),
                  pl.BlockSpec((N_EXPERTS, 1), lambda i: (0, 0))],
        out_specs=[pl.BlockSpec((N_EXPERTS, ROUTER_TILE), lambda i: (0, i)),
                   pl.BlockSpec((N_EXPERTS, ROUTER_TILE), lambda i: (0, i))],
        out_shape=[jax.ShapeDtypeStruct((N_EXPERTS, n_tok), F32), jax.ShapeDtypeStruct((N_EXPERTS, n_tok), F32)],
        compiler_params=pltpu.CompilerParams(dimension_semantics=("arbitrary",),
                                             vmem_limit_bytes=V7X_VMEM_LIMIT_BYTES),
        name="moe_router",
    )(xt, wrt, router_bias.reshape(N_EXPERTS, 1))


def _moe_kernel(cnt_ref, x_ref, sel_ref, w_ref, init_ref, wg_ref, wu_ref, wd_ref, lng_ref, lnb_ref, o_ref,
                rank_scr, ybuf_scr, sbuf_scr, xe_scr):
    i = pl.program_id(0)
    e = pl.program_id(1)
    tm = x_ref.shape[0]
    n_sub = tm // MOE_SUB
    tn = (((0,), (0,)), ((), ()))

    @pl.when(e == 0)
    def _():
        o_ref[...] = init_ref[...]
        before = (lax.broadcasted_iota(jnp.int32, (MOE_SUB, MOE_SUB), 0)
                  < lax.broadcasted_iota(jnp.int32, (MOE_SUB, MOE_SUB), 1))
        before = jnp.where(before, 1.0, 0.0).astype(BF16)
        for q in range(n_sub):
            cols = slice(q * MOE_SUB, (q + 1) * MOE_SUB)
            rank_scr[:, cols] = jnp.dot(sel_ref[:, cols].astype(BF16), before, preferred_element_type=F32)

    count = cnt_ref[i * N_EXPERTS + e]
    sel_e = sel_ref[pl.ds(e, 1), :]
    rank_e = rank_scr[pl.ds(e, 1), :]
    w_e = w_ref[pl.ds(e, 1), :]

    def one_hots(rank_row, sel_row, w_row, c):
        row = (c * MOE_ROWS + lax.broadcasted_iota(jnp.int32, (MOE_ROWS, MOE_SUB), 0)).astype(F32)
        hits = []
        for q in range(n_sub):
            cols = slice(q * MOE_SUB, (q + 1) * MOE_SUB)
            hits.append((rank_row[:, cols] == row) & (sel_row[:, cols] > 0.5))
        return hits

    def swiglu(xe):
        g = jnp.dot(xe, wg_ref[0], preferred_element_type=F32)
        u = jnp.dot(xe, wu_ref[0], preferred_element_type=F32)
        h = (jax.nn.silu(g) * u).astype(BF16)
        return jnp.dot(h, wd_ref[0], preferred_element_type=F32).astype(BF16)

    def weighted(hits):
        return [jnp.where(hits[q], w_e[:, q * MOE_SUB:(q + 1) * MOE_SUB], 0.0).astype(BF16) for q in range(n_sub)]

    slot = e % MOE_GROUP

    @pl.when(slot == 0)
    def _():
        stacks = [[] for _ in range(n_sub)]
        for gi in range(MOE_GROUP):
            hits = one_hots(rank_scr[pl.ds(e + gi, 1), :], sel_ref[pl.ds(e + gi, 1), :], None, 0)
            for q in range(n_sub):
                stacks[q].append(jnp.where(hits[q], 1.0, 0.0).astype(BF16))
        for q in range(n_sub):
            cols = slice(q * MOE_SUB, (q + 1) * MOE_SUB)
            xg = jnp.dot(jnp.concatenate(stacks[q], axis=0), x_ref[cols, :],
                         preferred_element_type=F32).astype(BF16)
            for gi in range(MOE_GROUP):
                xe_scr[gi, q] = xg[gi * MOE_ROWS:(gi + 1) * MOE_ROWS]

    scatters = weighted(one_hots(rank_e, sel_e, w_e, 0))
    y = swiglu(xe_scr[slot].reshape(n_sub * MOE_ROWS, D_MODEL))
    spare = MOE_SLOT - MOE_ROWS
    for q in range(n_sub):
        sbuf_scr[q, slot] = jnp.concatenate([scatters[q], jnp.zeros((spare, MOE_SUB), BF16)], axis=0)
        ybuf_scr[q, slot] = jnp.concatenate(
            [y[q * MOE_ROWS:(q + 1) * MOE_ROWS], jnp.zeros((spare, D_MODEL), BF16)], axis=0)

    @pl.when(slot == MOE_GROUP - 1)
    def _():
        for q in range(n_sub):
            cols = slice(q * MOE_SUB, (q + 1) * MOE_SUB)
            o_ref[cols, :] += lax.dot_general(sbuf_scr[q].reshape(MOE_GROUP * MOE_SLOT, MOE_SUB),
                                              ybuf_scr[q].reshape(MOE_GROUP * MOE_SLOT, D_MODEL), tn,
                                              preferred_element_type=F32)

    def overflow_body(c, carry):
        hits = one_hots(rank_e, sel_e, w_e, c)
        sc = weighted(hits)
        xe = jnp.concatenate(
            [jnp.dot(jnp.where(hits[q], 1.0, 0.0).astype(BF16), x_ref[q * MOE_SUB:(q + 1) * MOE_SUB, :],
                     preferred_element_type=F32).astype(BF16) for q in range(n_sub)], axis=0)
        yy = swiglu(xe)
        for q in range(n_sub):
            cols = slice(q * MOE_SUB, (q + 1) * MOE_SUB)
            o_ref[cols, :] += lax.dot_general(sc[q], yy[q * MOE_ROWS:(q + 1) * MOE_ROWS], tn,
                                              preferred_element_type=F32)
        return carry

    lax.fori_loop(1, (count + MOE_ROWS - 1) // MOE_ROWS, overflow_body, 0)

    @pl.when(e == N_EXPERTS - 1)
    def _():
        o_ref[...] = _layer_norm(o_ref[...], lng_ref[...], lnb_ref[...])


def moe_routed(x_bf16, sel_t, w_t, init, w_gate, w_up, w_down, ln_g, ln_b):
    n_tok = x_bf16.shape[0]
    n_tiles = n_tok // MOE_TILE
    per_sub = jnp.sum(sel_t.reshape(N_EXPERTS, n_tiles, MOE_TILE // MOE_SUB, MOE_SUB), axis=-1)
    cnt = jnp.max(per_sub, axis=-1).T.astype(jnp.int32).reshape(-1)
    grid_spec = pltpu.PrefetchScalarGridSpec(
        num_scalar_prefetch=1,
        grid=(n_tiles, N_EXPERTS),
        in_specs=[
            pl.BlockSpec((MOE_TILE, D_MODEL), lambda i, e, cnt: (i, 0), pipeline_mode=pl.Buffered(1)),
            pl.BlockSpec((N_EXPERTS, MOE_TILE), lambda i, e, cnt: (0, i)),
            pl.BlockSpec((N_EXPERTS, MOE_TILE), lambda i, e, cnt: (0, i)),
            pl.BlockSpec((MOE_TILE, D_MODEL), lambda i, e, cnt: (i, 0), pipeline_mode=pl.Buffered(1)),
            pl.BlockSpec((1, D_MODEL, EXPERT_FF), lambda i, e, cnt: (e, 0, 0)),
            pl.BlockSpec((1, D_MODEL, EXPERT_FF), lambda i, e, cnt: (e, 0, 0)),
            pl.BlockSpec((1, EXPERT_FF, D_MODEL), lambda i, e, cnt: (e, 0, 0)),
            pl.BlockSpec((1, D_MODEL), lambda i, e, cnt: (0, 0)),
            pl.BlockSpec((1, D_MODEL), lambda i, e, cnt: (0, 0)),
        ],
        out_specs=pl.BlockSpec((MOE_TILE, D_MODEL), lambda i, e, cnt: (i, 0)),
        scratch_shapes=[pltpu.VMEM((N_EXPERTS, MOE_TILE), F32),
                        pltpu.VMEM((MOE_TILE // MOE_SUB, MOE_GROUP, MOE_SLOT, D_MODEL), BF16),
                        pltpu.VMEM((MOE_TILE // MOE_SUB, MOE_GROUP, MOE_SLOT, MOE_SUB), BF16),
                        pltpu.VMEM((MOE_GROUP, MOE_TILE // MOE_SUB, MOE_ROWS, D_MODEL), BF16)],
    )
    return pl.pallas_call(
        _moe_kernel,
        grid_spec=grid_spec,
        out_shape=jax.ShapeDtypeStruct((n_tok, D_MODEL), F32),
        compiler_params=pltpu.CompilerParams(dimension_semantics=("arbitrary", "arbitrary"),
                                             vmem_limit_bytes=V7X_VMEM_LIMIT_BYTES),
        name="moe_routed",
    )(cnt, x_bf16, sel_t, w_t, init, w_gate, w_up, w_down, ln_g.reshape(1, D_MODEL), ln_b.reshape(1, D_MODEL))


def _shared_ffn_kernel(x_ref, wg_ref, wu_ref, wd_ref, o_ref, xb_ref):
    x = x_ref[...]
    xb = x.astype(BF16)
    h = jax.nn.silu(jnp.dot(xb, wg_ref[...], preferred_element_type=F32)) * jnp.dot(
        xb, wu_ref[...], preferred_element_type=F32)
    o_ref[...] = DEEPNORM_ALPHA * x + jnp.dot(h.astype(BF16), wd_ref[...], preferred_element_type=F32)
    xb_ref[...] = xb


def shared_ffn(xt, wg, wu, wd, tm=512):
    n_tok, d = xt.shape
    ff = wg.shape[1]
    return pl.pallas_call(
        _shared_ffn_kernel,
        grid=(n_tok // tm,),
        in_specs=[pl.BlockSpec((tm, d), lambda i: (i, 0)), pl.BlockSpec((d, ff), lambda i: (0, 0)),
                  pl.BlockSpec((d, ff), lambda i: (0, 0)), pl.BlockSpec((ff, d), lambda i: (0, 0))],
        out_specs=[pl.BlockSpec((tm, d), lambda i: (i, 0)), pl.BlockSpec((tm, d), lambda i: (i, 0))],
        out_shape=[jax.ShapeDtypeStruct((n_tok, d), F32), jax.ShapeDtypeStruct((n_tok, d), BF16)],
        compiler_params=pltpu.CompilerParams(dimension_semantics=("arbitrary",),
                                             vmem_limit_bytes=V7X_VMEM_LIMIT_BYTES),
        name="shared_ffn",
    )(xt, wg.astype(BF16), wu.astype(BF16), wd.astype(BF16))


def hybrid_layer(x, positions, w_in, lam_re, lam_im, log_dt, ssm_b_re, ssm_b_im, ssm_c_re, ssm_c_im, ssm_d,
                 w_glu, cmp_pos_k, cmp_pos_v, w_cmp_k1, w_cmp_k2, w_cmp_v1, w_cmp_v2, w_out, ln1_g, ln1_b,
                 w_router, router_bias, w_gate, w_up, w_down, ws_gate, ws_up, ws_down, ln2_g, ln2_b):
    bsz, L, _ = x.shape
    sizes = [SSM_WIDTH, NSA_WIDTH] + [KV_WIDTH] * 6 + [NSA_HEADS * N_BRANCH]
    o = [0] + [int(v) for v in np.cumsum(sizes)]
    col = lambda j: w_in[:, o[j]:o[j + 1]]
    dup = lambda w: jnp.concatenate([w[:, h * HEAD_DIM:(h + 1) * HEAD_DIM] for h in (0, 0, 1, 1)], axis=1)
    gate_cols = jnp.pad(col(8), ((0, 0), (0, 128 - NSA_HEADS * N_BRANCH)))
    w_uq = w_in[:, :o[2]].astype(BF16)
    w_kv = jnp.concatenate([col(4), col(6), dup(col(5)), dup(col(7)), col(2), col(3), gate_cols], axis=1).astype(BF16)

    xt = x.reshape(bsz * L, D_MODEL)
    u, q = proj_uq(xt, w_uq, positions.reshape(bsz * L, 1))
    kst, kwt, vs, vw, kc_raw, vc_raw, gate_pad = proj_kv(x, w_kv, positions.reshape(bsz, L, 1))
    kct, vcd = compress_kv(kc_raw, vc_raw, positions, cmp_pos_k, cmp_pos_v, w_cmp_k1, w_cmp_k2, w_cmp_v1, w_cmp_v2)
    y_s5 = s5_scan(u.reshape(bsz, L, SSM_WIDTH), lam_re, lam_im, log_dt, ssm_b_re, ssm_b_im, ssm_c_re, ssm_c_im, ssm_d)
    vw = vw.reshape(bsz, NSA_KV_HEADS, L // Q_BLOCK, Q_BLOCK, 128)
    y_nsa = nsa_attention(q.reshape(bsz, L, NSA_WIDTH), gate_pad, kct, vcd, kst, vs, kwt, vw)
    x1 = out_proj_ln(y_s5.reshape(bsz * L, SSM_WIDTH), y_nsa.reshape(bsz * L, NSA_WIDTH), xt, w_glu, w_out,
                     ln1_g, ln1_b)
    w_t, sel_t = moe_router(x1, w_router, router_bias)
    acc0, x1b = shared_ffn(x1, ws_gate, ws_up, ws_down)
    out = moe_routed(x1b, sel_t, w_t, acc0, w_gate.astype(BF16), w_up.astype(BF16), w_down.astype(BF16),
                     ln2_g, ln2_b)
    return out.reshape(bsz, L, D_MODEL)


def kernel(x, positions, w_in, lam_re, lam_im, log_dt, ssm_b_re, ssm_b_im, ssm_c_re, ssm_c_im, ssm_d, w_glu, cmp_pos_k, cmp_pos_v, w_cmp_k1, w_cmp_k2, w_cmp_v1, w_cmp_v2, w_out, ln1_g, ln1_b, w_router, router_bias, w_gate, w_up, w_down, ws_gate, ws_up, ws_down, ln2_g, ln2_b):
    params = (w_in, lam_re, lam_im, log_dt, ssm_b_re, ssm_b_im, ssm_c_re, ssm_c_im, ssm_d,
              w_glu, cmp_pos_k, cmp_pos_v, w_cmp_k1, w_cmp_k2, w_cmp_v1, w_cmp_v2, w_out, ln1_g, ln1_b,
              w_router, router_bias, w_gate, w_up, w_down, ws_gate, ws_up, ws_down, ln2_g, ln2_b)
    return hybrid_layer(x, positions, *(p[0] for p in params))
```

```python
import functools
import math

import numpy as np
import jax
import jax.numpy as jnp
from jax import lax
from jax.experimental import pallas as pl
from jax.experimental.pallas import tpu as pltpu

D_MODEL = 2048
SSM_WIDTH = 1024
SSM_CH_PER_GROUP = 16
SSM_GROUPS = 64
SSM_STATE = 64
NSA_HEADS = 16
NSA_KV_HEADS = 2
HEAD_DIM = 64
Q_PER_KV = NSA_HEADS // NSA_KV_HEADS
NSA_WIDTH = NSA_HEADS * HEAD_DIM
KV_WIDTH = NSA_KV_HEADS * HEAD_DIM
N_BRANCH = 3
CMP_BLOCK = 32
CMP_STRIDE = 16
SEL_BLOCK = 64
SEL_TOPK = 16
WINDOW = 512
Q_BLOCK = 128
ROPE_THETA = 10000.0
N_EXPERTS = 64
TOP_K = 8
N_EXPERT_GROUPS = 8
TOPK_GROUPS = 4
ROUTED_SCALE = 2.5
EXPERT_FF = 512
DEPTH = 1
DEEPNORM_ALPHA = (2.0 * DEPTH) ** 0.25
LN_EPS = 1e-5
NEG = -1e30
FORCE = 1e4
F32 = jnp.float32
BF16 = jnp.bfloat16

V7X_VMEM_LIMIT_BYTES = 56 * 1024 * 1024


def _layer_norm(x, g, b):
    mu = jnp.mean(x, -1, keepdims=True)
    var = jnp.mean(jnp.square(x - mu), -1, keepdims=True)
    return (x - mu) * lax.rsqrt(var + LN_EPS) * g + b


def _rope_tables(pos_col, inv_row):
    ang = pos_col * inv_row
    return jnp.cos(ang), jnp.sin(ang)


def _rope_lanes(x, cos, sin):
    lane = lax.broadcasted_iota(jnp.int32, (x.shape[0], 128), 1)
    first_half = (lane % HEAD_DIM) < HEAD_DIM // 2
    outs = []
    for blk in range(x.shape[1] // 128):
        xb = x[:, blk * 128:(blk + 1) * 128]
        rot = jnp.where(first_half, -pltpu.roll(xb, 128 - HEAD_DIM // 2, 1), pltpu.roll(xb, HEAD_DIM // 2, 1))
        outs.append(xb * cos + rot * sin)
    return outs[0] if len(outs) == 1 else jnp.concatenate(outs, axis=1)


def _inv_freq_row():
    half = HEAD_DIM // 2
    inv = ROPE_THETA ** (-jnp.arange(half, dtype=F32) / half)
    return jnp.tile(inv, 128 // half).reshape(1, 128)


PROJ_TILE = 512
Q_SCALE = HEAD_DIM ** -0.5 * math.log2(math.e)


def _proj_uq_kernel(x_ref, w_ref, pos_ref, inv_ref, u_ref, q_ref):
    acc = jnp.dot(x_ref[...].astype(BF16), w_ref[...], preferred_element_type=F32)
    u_ref[...] = acc[:, :SSM_WIDTH]
    cos, sin = _rope_tables(pos_ref[...].astype(F32), inv_ref[...])
    q_ref[...] = (_rope_lanes(acc[:, SSM_WIDTH:], cos, sin) * Q_SCALE).astype(BF16)


def proj_uq(xt, w_uq, pos_col, tm=1024):
    n_tok = xt.shape[0]
    return pl.pallas_call(
        _proj_uq_kernel,
        grid=(n_tok // tm,),
        in_specs=[pl.BlockSpec((tm, D_MODEL), lambda i: (i, 0)),
                  pl.BlockSpec((D_MODEL, SSM_WIDTH + NSA_WIDTH), lambda i: (0, 0), pipeline_mode=pl.Buffered(1)),
                  pl.BlockSpec((tm, 1), lambda i: (i, 0)),
                  pl.BlockSpec((1, 128), lambda i: (0, 0))],
        out_specs=[pl.BlockSpec((tm, SSM_WIDTH), lambda i: (i, 0)),
                   pl.BlockSpec((tm, NSA_WIDTH), lambda i: (i, 0))],
        out_shape=[jax.ShapeDtypeStruct((n_tok, SSM_WIDTH), F32), jax.ShapeDtypeStruct((n_tok, NSA_WIDTH), BF16)],
        compiler_params=pltpu.CompilerParams(dimension_semantics=("arbitrary",),
                                             vmem_limit_bytes=V7X_VMEM_LIMIT_BYTES),
        name="proj_uq",
    )(xt, w_uq, pos_col, _inv_freq_row())


KV_COLS = 4 * KV_WIDTH + 2 * 2 * KV_WIDTH + 128


def _proj_kv_kernel(x_ref, w_ref, pos_ref, inv_ref, kst_ref, kwt_ref, vs_ref, vw_ref, kc_ref, vc_ref, g_ref):
    acc = jnp.dot(x_ref[0].astype(BF16), w_ref[...], preferred_element_type=F32)
    cos, sin = _rope_tables(pos_ref[0].astype(F32), inv_ref[...])
    ks_t = _rope_lanes(acc[:, 0:128], cos, sin).T
    kw_t = _rope_lanes(acc[:, 128:256], cos, sin).T
    for k in range(NSA_KV_HEADS):
        kst_ref[0, k, 0] = ks_t[k * HEAD_DIM:(k + 1) * HEAD_DIM].astype(BF16)
        for j in range(PROJ_TILE // Q_BLOCK):
            kwt_ref[0, k, j] = kw_t[k * HEAD_DIM:(k + 1) * HEAD_DIM, j * Q_BLOCK:(j + 1) * Q_BLOCK].astype(BF16)
        vs_ref[0, k, 0] = acc[:, 256 + k * 128: 256 + (k + 1) * 128].astype(BF16)
        vw_ref[0, k] = acc[:, 512 + k * 128: 512 + (k + 1) * 128].astype(BF16)
    kc_ref[0] = acc[:, 768:896]
    vc_ref[0] = acc[:, 896:1024]
    g_ref[0] = acc[:, 1024:1152]


def proj_kv(x, w_kv, pos_col3):
    bsz, seq_len, _ = x.shape
    n_t = seq_len // PROJ_TILE
    per = PROJ_TILE // Q_BLOCK
    return pl.pallas_call(
        _proj_kv_kernel,
        grid=(bsz, n_t),
        in_specs=[pl.BlockSpec((1, PROJ_TILE, D_MODEL), lambda b, i: (b, i, 0)),
                  pl.BlockSpec((D_MODEL, KV_COLS), lambda b, i: (0, 0)),
                  pl.BlockSpec((1, PROJ_TILE, 1), lambda b, i: (b, i, 0)),
                  pl.BlockSpec((1, 128), lambda b, i: (0, 0))],
        out_specs=[
            pl.BlockSpec((1, NSA_KV_HEADS, 1, HEAD_DIM, PROJ_TILE), lambda b, i: (b, 0, i, 0, 0)),
            pl.BlockSpec((1, NSA_KV_HEADS, per, HEAD_DIM, Q_BLOCK), lambda b, i: (b, 0, i, 0, 0)),
            pl.BlockSpec((1, NSA_KV_HEADS, 1, PROJ_TILE, 128), lambda b, i: (b, 0, i, 0, 0)),
            pl.BlockSpec((1, NSA_KV_HEADS, PROJ_TILE, 128), lambda b, i: (b, 0, i, 0)),
            pl.BlockSpec((1, PROJ_TILE, 128), lambda b, i: (b, i, 0)),
            pl.BlockSpec((1, PROJ_TILE, 128), lambda b, i: (b, i, 0)),
            pl.BlockSpec((1, PROJ_TILE, 128), lambda b, i: (b, i, 0)),
        ],
        out_shape=[
            jax.ShapeDtypeStruct((bsz, NSA_KV_HEADS, n_t, HEAD_DIM, PROJ_TILE), BF16),
            jax.ShapeDtypeStruct((bsz, NSA_KV_HEADS, seq_len // Q_BLOCK, HEAD_DIM, Q_BLOCK), BF16),
            jax.ShapeDtypeStruct((bsz, NSA_KV_HEADS, n_t, PROJ_TILE, 128), BF16),
            jax.ShapeDtypeStruct((bsz, NSA_KV_HEADS, seq_len, 128), BF16),
            jax.ShapeDtypeStruct((bsz, seq_len, 128), F32),
            jax.ShapeDtypeStruct((bsz, seq_len, 128), F32),
            jax.ShapeDtypeStruct((bsz, seq_len, 128), F32),
        ],
        compiler_params=pltpu.CompilerParams(dimension_semantics=("arbitrary", "arbitrary"),
                                             vmem_limit_bytes=V7X_VMEM_LIMIT_BYTES),
        name="proj_kv",
    )(x, w_kv, pos_col3, _inv_freq_row())


def _compress_kernel(ck_ref, cv_ref, pek_ref, pev_ref, w1k_ref, w1v_ref, w2k_ref, w2v_ref, pos_ref, inv_ref,
                     kct_ref, vcd_ref):
    def hidden(c_ref, pe_ref, w1_ref):
        c = c_ref[0]
        lo = jnp.dot((c + pe_ref[0]).astype(BF16), w1_ref[0], preferred_element_type=F32)
        hi = jnp.dot((c + pe_ref[1]).astype(BF16), w1_ref[1], preferred_element_type=F32)
        hi_next = jnp.concatenate([hi[1:], jnp.zeros((1, hi.shape[1]), F32)], axis=0)
        return jax.nn.gelu(lo + hi_next).astype(BF16)

    kc = jnp.dot(hidden(ck_ref, pek_ref, w1k_ref), w2k_ref[...], preferred_element_type=F32)
    cos, sin = _rope_tables(pos_ref[0], inv_ref[...])
    kc_t = _rope_lanes(kc, cos, sin).T
    vc = jnp.dot(hidden(cv_ref, pev_ref, w1v_ref), w2v_ref[...], preferred_element_type=F32)
    for k in range(NSA_KV_HEADS):
        kct_ref[0, k] = kc_t[k * HEAD_DIM:(k + 1) * HEAD_DIM].astype(BF16)
        vcd_ref[0, k] = vc[:, k * 128:(k + 1) * 128].astype(BF16)


def compress_kv(kc_raw, vc_raw, positions, cmp_pos_k, cmp_pos_v, w_k1, w_k2, w_v1, w_v2):
    bsz, seq_len, _ = kc_raw.shape
    n_chunk = seq_len // CMP_STRIDE
    width = CMP_STRIDE * 128
    eye = jnp.eye(NSA_KV_HEADS, dtype=F32)

    def chunk_pe(pe):
        pe = pe.reshape(2, CMP_STRIDE, 1, HEAD_DIM)
        return jnp.broadcast_to(pe, (2, CMP_STRIDE, NSA_KV_HEADS, HEAD_DIM)).reshape(2, 1, width)

    def chunk_w1(w1):
        hid = w1.shape[1]
        w = w1.reshape(2, CMP_STRIDE, HEAD_DIM, hid)
        return jnp.einsum('htdj,kc->htkdcj', w, eye).reshape(2, width, NSA_KV_HEADS * hid).astype(BF16)

    hid = w_k2.shape[0]
    w2k = jnp.einsum('jd,kc->kjcd', w_k2, eye).reshape(NSA_KV_HEADS * hid, NSA_KV_HEADS * HEAD_DIM).astype(BF16)
    w2v = jnp.einsum('jd,kc,r->kjcrd', w_v2, eye, jnp.ones((2,), F32)).reshape(
        NSA_KV_HEADS * hid, NSA_KV_HEADS * 128).astype(BF16)
    pos = positions.astype(F32).reshape(bsz, n_chunk, CMP_STRIDE).sum(-1)
    pos_next = jnp.concatenate([pos[:, 1:], pos[:, -1:]], axis=1)
    cmp_pos = ((pos + pos_next) / CMP_BLOCK).reshape(bsz, n_chunk, 1)
    return pl.pallas_call(
        _compress_kernel,
        grid=(bsz,),
        in_specs=[pl.BlockSpec((1, n_chunk, width), lambda b: (b, 0, 0)),
                  pl.BlockSpec((1, n_chunk, width), lambda b: (b, 0, 0)),
                  pl.BlockSpec((2, 1, width), lambda b: (0, 0, 0)),
                  pl.BlockSpec((2, 1, width), lambda b: (0, 0, 0)),
                  pl.BlockSpec((2, width, NSA_KV_HEADS * hid), lambda b: (0, 0, 0)),
                  pl.BlockSpec((2, width, NSA_KV_HEADS * hid), lambda b: (0, 0, 0)),
                  pl.BlockSpec((NSA_KV_HEADS * hid, NSA_KV_HEADS * HEAD_DIM), lambda b: (0, 0)),
                  pl.BlockSpec((NSA_KV_HEADS * hid, NSA_KV_HEADS * 128), lambda b: (0, 0)),
                  pl.BlockSpec((1, n_chunk, 1), lambda b: (b, 0, 0)),
                  pl.BlockSpec((1, 128), lambda b: (0, 0))],
        out_specs=[pl.BlockSpec((1, NSA_KV_HEADS, HEAD_DIM, n_chunk), lambda b: (b, 0, 0, 0)),
                   pl.BlockSpec((1, NSA_KV_HEADS, n_chunk, 128), lambda b: (b, 0, 0, 0))],
        out_shape=[jax.ShapeDtypeStruct((bsz, NSA_KV_HEADS, HEAD_DIM, n_chunk), BF16),
                   jax.ShapeDtypeStruct((bsz, NSA_KV_HEADS, n_chunk, 128), BF16)],
        compiler_params=pltpu.CompilerParams(dimension_semantics=("arbitrary",),
                                             vmem_limit_bytes=V7X_VMEM_LIMIT_BYTES),
        name="compress_kv",
    )(kc_raw.reshape(bsz, n_chunk, width), vc_raw.reshape(bsz, n_chunk, width), chunk_pe(cmp_pos_k),
      chunk_pe(cmp_pos_v), chunk_w1(w_k1), chunk_w1(w_v1), w2k, w2v, cmp_pos, _inv_freq_row())


def _out_ln_kernel(y_ref, a_ref, x_ref, wglu_ref, wout_ref, g_ref, b_ref, o_ref):
    y = y_ref[...]
    y_ssm = y * jax.nn.sigmoid(jnp.dot(y.astype(BF16), wglu_ref[...], preferred_element_type=F32))
    mix = (jnp.dot(y_ssm.astype(BF16), wout_ref[:SSM_WIDTH, :], preferred_element_type=F32)
           + jnp.dot(a_ref[...].astype(BF16), wout_ref[SSM_WIDTH:, :], preferred_element_type=F32))
    o_ref[...] = _layer_norm(DEEPNORM_ALPHA * x_ref[...] + mix, g_ref[...], b_ref[...])


def out_proj_ln(y_s5, y_nsa, xt, w_glu, w_out, ln_g, ln_b, tm=512):
    n_tok = xt.shape[0]
    row = lambda i: (i, 0)
    const = lambda i: (0, 0)
    return pl.pallas_call(
        _out_ln_kernel,
        grid=(n_tok // tm,),
        in_specs=[pl.BlockSpec((tm, SSM_WIDTH), row), pl.BlockSpec((tm, NSA_WIDTH), row),
                  pl.BlockSpec((tm, D_MODEL), row),
                  pl.BlockSpec((SSM_WIDTH, SSM_WIDTH), const, pipeline_mode=pl.Buffered(1)),
                  pl.BlockSpec((D_MODEL, D_MODEL), const, pipeline_mode=pl.Buffered(1)),
                  pl.BlockSpec((1, D_MODEL), const),
                  pl.BlockSpec((1, D_MODEL), const)],
        out_specs=pl.BlockSpec((tm, D_MODEL), row),
        out_shape=jax.ShapeDtypeStruct((n_tok, D_MODEL), F32),
        compiler_params=pltpu.CompilerParams(dimension_semantics=("arbitrary",),
                                             vmem_limit_bytes=V7X_VMEM_LIMIT_BYTES),
        name="out_proj_ln",
    )(y_s5, y_nsa, xt, w_glu.astype(BF16), w_out.astype(BF16), ln_g.reshape(1, D_MODEL), ln_b.reshape(1, D_MODEL))


S5_CHUNK = 1024
S5_SUB = S5_CHUNK // 8
S5_GROUPS_PER_BLOCK = 8
S5_STATES = S5_GROUPS_PER_BLOCK * SSM_STATE


def _cmul_add(ar, ai, xr, xi, br, bi):
    return ar * xr - ai * xi + br, ar * xi + ai * xr + bi


def _s5_kernel(u_ref, lam_ref, bmat_ref, cmat_ref, d_ref, perm_ref, permt_ref, o_ref,
               xr_scr, xi_scr, pr_scr, pi_scr, carry_scr, a_scr, bbar_scr):
    c = pl.program_id(2)

    @pl.when(c == 0)
    def _():
        lr, li = lam_ref[0, 0:1, :], lam_ref[0, 1:2, :]
        dt = jnp.exp(lam_ref[0, 2:3, :])
        mag = jnp.exp(lr * dt)
        ar, ai = mag * jnp.cos(li * dt), mag * jnp.sin(li * dt)
        zr, zi = ar - 1.0, ai
        den = lr * lr + li * li
        fr, fi = (zr * lr + zi * li) / den, (zi * lr - zr * li) / den
        a_scr[0:1, :] = ar
        a_scr[1:2, :] = ai
        b_re, b_im = bmat_ref[0, 0], bmat_ref[0, 1]
        bbar_scr[0] = (fr * b_re - fi * b_im).astype(BF16)
        bbar_scr[1] = (fr * b_im + fi * b_re).astype(BF16)
        carry_scr[...] = jnp.zeros(carry_scr.shape, F32)
        a_re0 = jnp.broadcast_to(ar, (8, S5_STATES))
        a_im0 = jnp.broadcast_to(ai, (8, S5_STATES))

        def pw_body(i, pw):
            pr, pi = pw
            pr_scr[i] = pr
            pi_scr[i] = pi
            return a_re0 * pr - a_im0 * pi, a_re0 * pi + a_im0 * pr

        lax.fori_loop(0, S5_SUB, pw_body, (a_re0, a_im0))

    a_re = jnp.broadcast_to(a_scr[0:1, :], (8, S5_STATES))
    a_im = jnp.broadcast_to(a_scr[1:2, :], (8, S5_STATES))
    u = u_ref[0]
    perm = perm_ref[...]
    u_p = jnp.dot(perm, u.astype(BF16), preferred_element_type=F32).astype(BF16)
    xr_scr[...] = jnp.dot(u_p, bbar_scr[0], preferred_element_type=F32)
    xi_scr[...] = jnp.dot(u_p, bbar_scr[1], preferred_element_type=F32)

    def scan_body(i, x):
        row = pl.multiple_of(i * 8, 8)
        xr, xi = _cmul_add(a_re, a_im, x[0], x[1], xr_scr[pl.ds(row, 8), :], xi_scr[pl.ds(row, 8), :])
        xr_scr[pl.ds(row, 8), :] = xr
        xi_scr[pl.ds(row, 8), :] = xi
        return xr, xi

    zero = jnp.zeros((8, S5_STATES), F32)
    er, ei = lax.fori_loop(0, S5_SUB, scan_body, (zero, zero), unroll=4)

    ar_s = pr_scr[S5_SUB - 1][0:1]
    ai_s = pi_scr[S5_SUB - 1][0:1]
    rows_r = [carry_scr[0:1, :]]
    rows_i = [carry_scr[1:2, :]]
    for j in range(8):
        nr, ni = _cmul_add(ar_s, ai_s, rows_r[-1], rows_i[-1], er[j:j + 1], ei[j:j + 1])
        rows_r.append(nr)
        rows_i.append(ni)
    carry_scr[0:1, :] = rows_r[8]
    carry_scr[1:2, :] = rows_i[8]
    cr = jnp.concatenate(rows_r[:8], axis=0)
    ci = jnp.concatenate(rows_i[:8], axis=0)

    def fix_body(i, carry):
        row = pl.multiple_of(i * 8, 8)
        xr, xi = _cmul_add(pr_scr[i], pi_scr[i], cr, ci, xr_scr[pl.ds(row, 8), :], xi_scr[pl.ds(row, 8), :])
        xr_scr[pl.ds(row, 8), :] = xr
        xi_scr[pl.ds(row, 8), :] = xi
        return carry

    lax.fori_loop(0, S5_SUB, fix_body, 0, unroll=4)

    y_p = (jnp.dot(xr_scr[...].astype(BF16), cmat_ref[0, 0], preferred_element_type=F32)
           - jnp.dot(xi_scr[...].astype(BF16), cmat_ref[0, 1], preferred_element_type=F32))
    y_hi = y_p.astype(BF16)
    y_lo = (y_p - y_hi.astype(F32)).astype(BF16)
    perm_t = permt_ref[...]
    y = jnp.dot(perm_t, y_hi, preferred_element_type=F32) + jnp.dot(perm_t, y_lo, preferred_element_type=F32)
    o_ref[0] = jax.nn.gelu(y + d_ref[0] * u)


def s5_scan(u, lam_re, lam_im, log_dt, b_re, b_im, c_re, c_im, d_skip):
    bsz, seq_len, _ = u.shape
    nb = SSM_GROUPS // S5_GROUPS_PER_BLOCK
    eye = jnp.eye(S5_GROUPS_PER_BLOCK, dtype=F32)

    def blockdiag_b(m):
        m = jnp.swapaxes(m, 1, 2).reshape(nb, S5_GROUPS_PER_BLOCK, SSM_CH_PER_GROUP, SSM_STATE)
        return jnp.einsum('nghp,gk->nghkp', m, eye).reshape(nb, 128, S5_STATES)

    def blockdiag_c(m):
        m = jnp.swapaxes(m, 1, 2).reshape(nb, S5_GROUPS_PER_BLOCK, SSM_STATE, SSM_CH_PER_GROUP)
        return jnp.einsum('ngph,gk->ngpkh', m, eye).reshape(nb, S5_STATES, 128)

    log_dt_states = jnp.broadcast_to(log_dt[:, None], lam_re.shape)
    lam = jnp.stack([m.reshape(nb, S5_STATES) for m in (lam_re, lam_im, log_dt_states)], axis=1)
    bmat = jnp.stack([blockdiag_b(b_re), blockdiag_b(b_im)], axis=1)
    cmat = jnp.stack([blockdiag_c(c_re), blockdiag_c(c_im)], axis=1).astype(BF16)
    d = d_skip.reshape(nb, 1, 128)
    r = np.arange(S5_CHUNK)
    perm = np.zeros((S5_CHUNK, S5_CHUNK), np.float32)
    perm[r, (r % 8) * S5_SUB + r // 8] = 1.0
    perm = jnp.asarray(perm, BF16)
    return pl.pallas_call(
        _s5_kernel,
        grid=(bsz, nb, seq_len // S5_CHUNK),
        in_specs=[
            pl.BlockSpec((1, S5_CHUNK, 128), lambda b, g, c: (b, c, g)),
            pl.BlockSpec((1, 3, S5_STATES), lambda b, g, c: (g, 0, 0)),
            pl.BlockSpec((1, 2, 128, S5_STATES), lambda b, g, c: (g, 0, 0, 0)),
            pl.BlockSpec((1, 2, S5_STATES, 128), lambda b, g, c: (g, 0, 0, 0)),
            pl.BlockSpec((1, 1, 128), lambda b, g, c: (g, 0, 0)),
            pl.BlockSpec((S5_CHUNK, S5_CHUNK), lambda b, g, c: (0, 0)),
            pl.BlockSpec((S5_CHUNK, S5_CHUNK), lambda b, g, c: (0, 0)),
        ],
        out_specs=pl.BlockSpec((1, S5_CHUNK, 128), lambda b, g, c: (b, c, g)),
        out_shape=jax.ShapeDtypeStruct((bsz, seq_len, SSM_WIDTH), F32),
        scratch_shapes=[pltpu.VMEM((S5_CHUNK, S5_STATES), F32), pltpu.VMEM((S5_CHUNK, S5_STATES), F32),
                        pltpu.VMEM((S5_SUB, 8, S5_STATES), F32), pltpu.VMEM((S5_SUB, 8, S5_STATES), F32),
                        pltpu.VMEM((2, S5_STATES), F32), pltpu.VMEM((2, S5_STATES), F32),
                        pltpu.VMEM((2, 128, S5_STATES), BF16)],
        compiler_params=pltpu.CompilerParams(
            dimension_semantics=("arbitrary", "arbitrary", "arbitrary"), vmem_limit_bytes=V7X_VMEM_LIMIT_BYTES),
        name="s5_scan",
    )(u, lam, bmat, cmat, d, perm, perm.T)


def _softmax_tile(s, m_old):
    m_new = jnp.maximum(m_old, jnp.max(s, axis=1, keepdims=True))
    m_wide = jnp.concatenate([m_new] * (s.shape[1] // 128), axis=1)
    return m_new, jnp.exp2(m_old - m_new), jnp.exp2(s - m_wide)


def _lane_is_low(shape):
    return lax.broadcasted_iota(jnp.int32, shape, 1) < HEAD_DIM


def _pad_kt(kt, variant):
    z = jnp.zeros_like(kt)
    return jnp.concatenate([kt, z] if variant == 0 else [z, kt], axis=0)


def _pad_v(vv, variant):
    low = _lane_is_low(vv.shape)
    keep = low if variant == 0 else jnp.logical_not(low)
    return jnp.where(keep, vv, jnp.ones_like(vv))


def _finish(acc, variant):
    lane = lax.broadcasted_iota(jnp.int32, acc.shape, 1)
    lsel = lane == (HEAD_DIM if variant == 0 else 0)
    l = jnp.sum(jnp.where(lsel, acc, 0.0), axis=1, keepdims=True)
    keep = (lane < HEAD_DIM) if variant == 0 else (lane >= HEAD_DIM)
    return jnp.where(keep, acc / l, 0.0)


def _nsa_kernel(q_ref, g_ref, kct_ref, vc_ref, kst_ref, vs_ref, kwt_ref, vw_ref, ovl_ref, gx_ref, o_ref,
                m_scr, acc_scr, s_scr_a, s_scr_b, p_scr, *, seq_len):
    s_slots = (s_scr_a, s_scr_b)
    n_sel = seq_len // SEL_BLOCK
    n_cpad = seq_len // CMP_STRIDE
    sel_tile = 512
    blocks_per_tile = sel_tile // SEL_BLOCK
    win_tiles = WINDOW // Q_BLOCK + 1
    n_pair = Q_PER_KV // 2
    rows = n_pair * Q_BLOCK
    i = pl.program_id(2)
    t0 = i * Q_BLOCK

    qb = q_ref[0]
    qst = jnp.concatenate([qb[:, p * 128:(p + 1) * 128] for p in range(n_pair)], axis=0)

    sig = jax.nn.sigmoid(g_ref[0])
    sig_hi = sig.astype(BF16)
    sig_lo = (sig - sig_hi.astype(F32)).astype(BF16)
    gx = gx_ref[0]
    gexp = (jnp.dot(sig_hi, gx, preferred_element_type=F32) + jnp.dot(sig_lo, gx, preferred_element_type=F32))

    def gate_of(branch):
        base = branch * n_pair * 128
        return jnp.concatenate([gexp[:, base + p * 128: base + (p + 1) * 128] for p in range(n_pair)], axis=0)

    t_row = t0 + lax.broadcasted_iota(jnp.int32, (Q_BLOCK, 1), 0)

    slab = 64
    kct = kct_ref[0, 0]
    s_cmp = [jnp.dot(qst, _pad_kt(kct, v), preferred_element_type=F32) for v in range(2)]
    n_kblk = seq_len // Q_BLOCK
    w0 = jnp.clip(i - (win_tiles - 1), 0, n_kblk - win_tiles)
    kw = jnp.concatenate([kwt_ref[0, 0, w0 + j] for j in range(win_tiles)], axis=1)
    s_win = [jnp.dot(qst, _pad_kt(kw, v), preferred_element_type=F32) for v in range(2)]
    for v in range(2):
        s_slots[0][v] = jnp.dot(qst, _pad_kt(kst_ref[0, 0, 0], v), preferred_element_type=F32)

    n_iota = lax.broadcasted_iota(jnp.int32, (Q_BLOCK, n_cpad), 1)
    cmask = (n_iota * CMP_STRIDE + (CMP_BLOCK - 1)) <= t_row
    cmask4 = jnp.concatenate([cmask] * n_pair, axis=0)
    vcd = vc_ref[0, 0]
    p_sum = jnp.zeros((Q_BLOCK, n_cpad), F32)
    out = jnp.zeros((rows, 128), F32)
    o_c = jnp.zeros((rows, 128), F32)
    for v in range(2):
        s = jnp.where(cmask4, s_cmp[v], NEG)
        m = jnp.max(s, axis=1, keepdims=True)
        e = jnp.where(cmask4, jnp.exp2(s - m), 0.0)
        l = jnp.sum(e, axis=1, keepdims=True)
        p = e * (1.0 / jnp.maximum(l, 1e-30))
        for pp in range(n_pair):
            p_sum = p_sum + p[pp * Q_BLOCK:(pp + 1) * Q_BLOCK]
        low = _lane_is_low((n_cpad, 128))
        vz = jnp.where(low if v == 0 else jnp.logical_not(low), vcd, jnp.zeros_like(vcd))
        o_c = o_c + jnp.dot(p.astype(BF16), vz, preferred_element_type=F32)
    out = out + gate_of(0) * o_c

    ps_hi = p_sum.astype(BF16)
    ps_lo = (p_sum - ps_hi.astype(F32)).astype(BF16)
    ovl = ovl_ref[...]
    nt = (((1,), (1,)), ((), ()))
    imp_t = (lax.dot_general(ovl, ps_hi, nt, preferred_element_type=F32)
             + lax.dot_general(ovl, ps_lo, nt, preferred_element_type=F32))

    vw = jnp.concatenate([vw_ref[0, 0, w0 + j] for j in range(win_tiles)], axis=0)
    kpos_w = w0 * Q_BLOCK + lax.broadcasted_iota(jnp.int32, (Q_BLOCK, win_tiles * Q_BLOCK), 1)
    diff = t_row - kpos_w
    wbias = jnp.where((diff >= 0) & (diff < WINDOW), 0.0, NEG)
    wbias4 = jnp.concatenate([wbias] * n_pair, axis=0)
    o_w = jnp.zeros((rows, 128), F32)
    for v in range(2):
        s = s_win[v] + wbias4
        m = jnp.max(s, axis=1, keepdims=True)
        p = jnp.exp2(s - m)
        o_w = o_w + _finish(jnp.dot(p.astype(BF16), _pad_v(vw, v), preferred_element_type=F32), v)
    out = out + gate_of(2) * o_w

    s_iota = lax.broadcasted_iota(jnp.int32, (n_sel, Q_BLOCK), 0)
    t_lane = t0 + lax.broadcasted_iota(jnp.int32, (n_sel, Q_BLOCK), 1)
    cur = t_lane // SEL_BLOCK
    forced = (s_iota == 0) | (s_iota == cur) | (s_iota == cur - 1)
    valid = s_iota * SEL_BLOCK <= t_lane
    score = jnp.where(forced, FORCE, jnp.where(valid, imp_t, -1.0))
    s_f = s_iota.astype(F32)
    sel_t = jnp.zeros((n_sel, Q_BLOCK), F32)
    for _ in range(min(SEL_TOPK, n_sel)):
        mx = jnp.max(score, axis=0, keepdims=True)
        idx = jnp.min(jnp.where(score == mx, s_f, float(n_sel)), axis=0, keepdims=True)
        hit = s_f == idx
        sel_t = jnp.where(hit, 1.0, sel_t)
        score = jnp.where(hit, -3e38, score)
    selmask = sel_t.T.astype(BF16)

    m_scr[...] = jnp.full(m_scr.shape, NEG, F32)
    acc_scr[...] = jnp.zeros(acc_scr.shape, F32)
    n_tiles = (t0 + Q_BLOCK + sel_tile - 1) // sel_tile

    last_tile = seq_len // sel_tile - 1

    def bias_of(kt):
        blk = kt * blocks_per_tile + lax.broadcasted_iota(jnp.int32, (n_sel, sel_tile), 1) // SEL_BLOCK
        expand = (lax.broadcasted_iota(jnp.int32, (n_sel, sel_tile), 0) == blk).astype(BF16)
        selexp = jnp.dot(selmask, expand, preferred_element_type=F32)
        kpos = kt * sel_tile + lax.broadcasted_iota(jnp.int32, (Q_BLOCK, sel_tile), 1)
        bias = jnp.where((selexp > 0.5) & (kpos <= t_row), 0.0, NEG)
        return jnp.concatenate([bias] * n_pair, axis=0)

    def scores_into(slot, kt):
        bias4 = bias_of(kt)
        kt_tile = kst_ref[0, 0, jnp.minimum(kt, last_tile)]
        for v in range(2):
            s_slots[slot][v] = jnp.dot(qst, _pad_kt(kt_tile, v), preferred_element_type=F32) + bias4

    def attend_from(slot, kt):
        v_tile = vs_ref[0, 0, jnp.minimum(kt, last_tile)]
        for v in range(2):
            for h in range(rows // slab):
                r = slice(h * slab, (h + 1) * slab)
                m_new, alpha, p = _softmax_tile(s_slots[slot][v, r, :], m_scr[v, r, :])
                m_scr[v, r, :] = m_new
                acc_scr[v, r, :] = alpha * acc_scr[v, r, :]
                p_scr[v, r, :] = p.astype(BF16)
            acc_scr[v] += jnp.dot(p_scr[v], _pad_v(v_tile, v), preferred_element_type=F32)

    for v in range(2):
        s_slots[0][v] = s_slots[0][v] + bias_of(0)

    def sel_body(j, carry):
        kt = 2 * j
        scores_into(1, kt + 1)
        attend_from(0, kt)
        scores_into(0, kt + 2)
        attend_from(1, kt + 1)
        return carry

    lax.fori_loop(0, (n_tiles + 1) // 2, sel_body, 0)
    out = out + gate_of(1) * (_finish(acc_scr[0], 0) + _finish(acc_scr[1], 1))

    o_ref[0] = jnp.concatenate([out[p * Q_BLOCK:(p + 1) * Q_BLOCK] for p in range(n_pair)], axis=1)


def nsa_attention(q, gate_pad, kct, vc, kst, vs, kwt, vw):
    bsz, seq_len, _ = q.shape
    n_sel = seq_len // SEL_BLOCK
    n_cpad = seq_len // CMP_STRIDE
    n_cmp = (seq_len - CMP_BLOCK) // CMP_STRIDE + 1
    n_pair = Q_PER_KV // 2
    cs = np.arange(n_cpad) * CMP_STRIDE
    ce = cs + CMP_BLOCK - 1
    ss = np.arange(n_sel) * SEL_BLOCK
    se = ss + SEL_BLOCK - 1
    ovl = (cs[None, :] <= se[:, None]) & (ce[None, :] >= ss[:, None]) & (np.arange(n_cpad)[None, :] < n_cmp)
    ovl = jnp.asarray(ovl.astype(np.float32), BF16)
    gx = np.zeros((NSA_KV_HEADS, 128, N_BRANCH * n_pair * 128), np.float32)
    for k in range(NSA_KV_HEADS):
        for hl in range(Q_PER_KV):
            for br in range(N_BRANCH):
                c0 = br * n_pair * 128 + hl * HEAD_DIM
                gx[k, (k * Q_PER_KV + hl) * N_BRANCH + br, c0:c0 + HEAD_DIM] = 1.0
    gx = jnp.asarray(gx, BF16)
    width = Q_PER_KV * HEAD_DIM
    full = lambda *shape: pl.BlockSpec((1, 1) + shape, lambda b, k, i: (b, k) + (0,) * len(shape))
    return pl.pallas_call(
        functools.partial(_nsa_kernel, seq_len=seq_len),
        grid=(bsz, NSA_KV_HEADS, seq_len // Q_BLOCK),
        in_specs=[
            pl.BlockSpec((1, Q_BLOCK, width), lambda b, k, i: (b, i, k)),
            pl.BlockSpec((1, Q_BLOCK, 128), lambda b, k, i: (b, i, 0)),
            full(HEAD_DIM, n_cpad), full(n_cpad, 128),
            full(seq_len // 512, HEAD_DIM, 512), full(seq_len // 512, 512, 128),
            full(seq_len // Q_BLOCK, HEAD_DIM, Q_BLOCK), full(seq_len // Q_BLOCK, Q_BLOCK, 128),
            pl.BlockSpec((n_sel, n_cpad), lambda b, k, i: (0, 0)),
            pl.BlockSpec((1, 128, N_BRANCH * n_pair * 128), lambda b, k, i: (k, 0, 0)),
        ],
        out_specs=pl.BlockSpec((1, Q_BLOCK, width), lambda b, k, i: (b, i, k)),
        out_shape=jax.ShapeDtypeStruct((bsz, seq_len, NSA_WIDTH), F32),
        scratch_shapes=[pltpu.VMEM((2, n_pair * Q_BLOCK, 128), F32), pltpu.VMEM((2, n_pair * Q_BLOCK, 128), F32),
                        pltpu.VMEM((2, n_pair * Q_BLOCK, 512), F32), pltpu.VMEM((2, n_pair * Q_BLOCK, 512), F32),
                        pltpu.VMEM((2, n_pair * Q_BLOCK, 512), BF16)],
        compiler_params=pltpu.CompilerParams(
            dimension_semantics=("arbitrary", "arbitrary", "arbitrary"), vmem_limit_bytes=V7X_VMEM_LIMIT_BYTES),
        name="nsa_attention",
    )(q, gate_pad, kct, vc, kst, vs, kwt, vw, ovl, gx)


ROUTER_TILE = 512
MOE_TILE = 1024
MOE_SUB = 256
MOE_ROWS = 48
MOE_SLOT = 64
MOE_GROUP = 4


def _first_max_mask(x, idx_f, axis):
    mx = jnp.max(x, axis=axis, keepdims=True)
    first = jnp.min(jnp.where(x == mx, idx_f, 1e9), axis=axis, keepdims=True)
    return idx_f == first, mx


def _router_kernel(x_ref, wrt_ref, bias_ref, w_ref, sel_ref):
    per_group = N_EXPERTS // N_EXPERT_GROUPS
    tr = x_ref.shape[0]
    nt = (((1,), (1,)), ((), ()))
    logits = lax.dot_general(wrt_ref[...], x_ref[...].astype(BF16), nt, preferred_element_type=F32)
    aff = jax.nn.sigmoid(logits)
    biased = aff + bias_ref[...]
    grp = biased.reshape(N_EXPERT_GROUPS, per_group, tr)
    in_grp = lax.broadcasted_iota(jnp.int32, grp.shape, 1).astype(F32)
    hit1, m1 = _first_max_mask(grp, in_grp, 1)
    m2 = jnp.max(jnp.where(hit1, -jnp.inf, grp), axis=1, keepdims=True)
    gscore = (m1 + m2).reshape(N_EXPERT_GROUPS, tr)
    g_idx = lax.broadcasted_iota(jnp.int32, gscore.shape, 0).astype(F32)
    gsel = jnp.zeros(gscore.shape, F32)
    for _ in range(TOPK_GROUPS):
        hit, _ = _first_max_mask(gscore, g_idx, 0)
        gsel = jnp.where(hit, 1.0, gsel)
        gscore = jnp.where(hit, -jnp.inf, gscore)
    gmask = jnp.broadcast_to(gsel.reshape(N_EXPERT_GROUPS, 1, tr), grp.shape).reshape(N_EXPERTS, tr)
    cand = jnp.where(gmask > 0.5, biased, NEG)
    e_idx = lax.broadcasted_iota(jnp.int32, cand.shape, 0).astype(F32)
    sel = jnp.zeros(cand.shape, F32)
    for _ in range(TOP_K):
        hit, _ = _first_max_mask(cand, e_idx, 0)
        sel = jnp.where(hit, 1.0, sel)
        cand = jnp.where(hit, -jnp.inf, cand)
    w = jnp.where(sel > 0.5, aff, 0.0)
    w_ref[...] = w / jnp.sum(w, axis=0, keepdims=True) * ROUTED_SCALE
    sel_ref[...] = sel


def moe_router(xt, w_router, router_bias):
    n_tok = xt.shape[0]
    wrt = w_router.T.astype(BF16)
    return pl.pallas_call(
        _router_kernel,
        grid=(n_tok // ROUTER_TILE,),
        in_specs=[pl.BlockSpec((ROUTER_TILE, D_MODEL), lambda i: (i, 0)),
                  pl.BlockSpec((N_EXPERTS, D_MODEL), lambda i: (0, 0)),
                  pl.BlockSpec((N_EXPERTS, 1), lambda i: (0, 0))],
        out_specs=[pl.BlockSpec((N_EXPERTS, ROUTER_TILE), lambda i: (0, i)),
                   pl.BlockSpec((N_EXPERTS, ROUTER_TILE), lambda i: (0, i))],
        out_shape=[jax.ShapeDtypeStruct((N_EXPERTS, n_tok), F32), jax.ShapeDtypeStruct((N_EXPERTS, n_tok), F32)],
        compiler_params=pltpu.CompilerParams(dimension_semantics=("arbitrary",),
                                             vmem_limit_bytes=V7X_VMEM_LIMIT_BYTES),
        name="moe_router",
    )(xt, wrt, router_bias.reshape(N_EXPERTS, 1))


def _moe_kernel(cnt_ref, x_ref, sel_ref, w_ref, init_ref, wg_ref, wu_ref, wd_ref, lng_ref, lnb_ref, o_ref,
                rank_scr, ybuf_scr, sbuf_scr, xe_scr):
    i = pl.program_id(0)
    e = pl.program_id(1)
    tm = x_ref.shape[0]
    n_sub = tm // MOE_SUB
    tn = (((0,), (0,)), ((), ()))

    @pl.when(e == 0)
    def _():
        o_ref[...] = init_ref[...]
        before = (lax.broadcasted_iota(jnp.int32, (MOE_SUB, MOE_SUB), 0)
                  < lax.broadcasted_iota(jnp.int32, (MOE_SUB, MOE_SUB), 1))
        before = jnp.where(before, 1.0, 0.0).astype(BF16)
        for q in range(n_sub):
            cols = slice(q * MOE_SUB, (q + 1) * MOE_SUB)
            rank_scr[:, cols] = jnp.dot(sel_ref[:, cols].astype(BF16), before, preferred_element_type=F32)

    count = cnt_ref[i * N_EXPERTS + e]
    sel_e = sel_ref[pl.ds(e, 1), :]
    rank_e = rank_scr[pl.ds(e, 1), :]
    w_e = w_ref[pl.ds(e, 1), :]

    def one_hots(rank_row, sel_row, w_row, c):
        row = (c * MOE_ROWS + lax.broadcasted_iota(jnp.int32, (MOE_ROWS, MOE_SUB), 0)).astype(F32)
        hits = []
        for q in range(n_sub):
            cols = slice(q * MOE_SUB, (q + 1) * MOE_SUB)
            hits.append((rank_row[:, cols] == row) & (sel_row[:, cols] > 0.5))
        return hits

    def swiglu(xe):
        g = jnp.dot(xe, wg_ref[0], preferred_element_type=F32)
        u = jnp.dot(xe, wu_ref[0], preferred_element_type=F32)
        h = (jax.nn.silu(g) * u).astype(BF16)
        return jnp.dot(h, wd_ref[0], preferred_element_type=F32).astype(BF16)

    def weighted(hits):
        return [jnp.where(hits[q], w_e[:, q * MOE_SUB:(q + 1) * MOE_SUB], 0.0).astype(BF16) for q in range(n_sub)]

    slot = e % MOE_GROUP

    @pl.when(slot == 0)
    def _():
        stacks = [[] for _ in range(n_sub)]
        for gi in range(MOE_GROUP):
            hits = one_hots(rank_scr[pl.ds(e + gi, 1), :], sel_ref[pl.ds(e + gi, 1), :], None, 0)
            for q in range(n_sub):
                stacks[q].append(jnp.where(hits[q], 1.0, 0.0).astype(BF16))
        for q in range(n_sub):
            cols = slice(q * MOE_SUB, (q + 1) * MOE_SUB)
            xg = jnp.dot(jnp.concatenate(stacks[q], axis=0), x_ref[cols, :],
                         preferred_element_type=F32).astype(BF16)
            for gi in range(MOE_GROUP):
                xe_scr[gi, q] = xg[gi * MOE_ROWS:(gi + 1) * MOE_ROWS]

    scatters = weighted(one_hots(rank_e, sel_e, w_e, 0))
    y = swiglu(xe_scr[slot].reshape(n_sub * MOE_ROWS, D_MODEL))
    spare = MOE_SLOT - MOE_ROWS
    for q in range(n_sub):
        sbuf_scr[q, slot] = jnp.concatenate([scatters[q], jnp.zeros((spare, MOE_SUB), BF16)], axis=0)
        ybuf_scr[q, slot] = jnp.concatenate(
            [y[q * MOE_ROWS:(q + 1) * MOE_ROWS], jnp.zeros((spare, D_MODEL), BF16)], axis=0)

    @pl.when(slot == MOE_GROUP - 1)
    def _():
        for q in range(n_sub):
            cols = slice(q * MOE_SUB, (q + 1) * MOE_SUB)
            o_ref[cols, :] += lax.dot_general(sbuf_scr[q].reshape(MOE_GROUP * MOE_SLOT, MOE_SUB),
                                              ybuf_scr[q].reshape(MOE_GROUP * MOE_SLOT, D_MODEL), tn,
                                              preferred_element_type=F32)

    def overflow_body(c, carry):
        hits = one_hots(rank_e, sel_e, w_e, c)
        sc = weighted(hits)
        xe = jnp.concatenate(
            [jnp.dot(jnp.where(hits[q], 1.0, 0.0).astype(BF16), x_ref[q * MOE_SUB:(q + 1) * MOE_SUB, :],
                     preferred_element_type=F32).astype(BF16) for q in range(n_sub)], axis=0)
        yy = swiglu(xe)
        for q in range(n_sub):
            cols = slice(q * MOE_SUB, (q + 1) * MOE_SUB)
            o_ref[cols, :] += lax.dot_general(sc[q], yy[q * MOE_ROWS:(q + 1) * MOE_ROWS], tn,
                                              preferred_element_type=F32)
        return carry

    lax.fori_loop(1, (count + MOE_ROWS - 1) // MOE_ROWS, overflow_body, 0)

    @pl.when(e == N_EXPERTS - 1)
    def _():
        o_ref[...] = _layer_norm(o_ref[...], lng_ref[...], lnb_ref[...])


def moe_routed(x_bf16, sel_t, w_t, init, w_gate, w_up, w_down, ln_g, ln_b):
    n_tok = x_bf16.shape[0]
    n_tiles = n_tok // MOE_TILE
    per_sub = jnp.sum(sel_t.reshape(N_EXPERTS, n_tiles, MOE_TILE // MOE_SUB, MOE_SUB), axis=-1)
    cnt = jnp.max(per_sub, axis=-1).T.astype(jnp.int32).reshape(-1)
    grid_spec = pltpu.PrefetchScalarGridSpec(
        num_scalar_prefetch=1,
        grid=(n_tiles, N_EXPERTS),
        in_specs=[
            pl.BlockSpec((MOE_TILE, D_MODEL), lambda i, e, cnt: (i, 0), pipeline_mode=pl.Buffered(1)),
            pl.BlockSpec((N_EXPERTS, MOE_TILE), lambda i, e, cnt: (0, i)),
            pl.BlockSpec((N_EXPERTS, MOE_TILE), lambda i, e, cnt: (0, i)),
            pl.BlockSpec((MOE_TILE, D_MODEL), lambda i, e, cnt: (i, 0), pipeline_mode=pl.Buffered(1)),
            pl.BlockSpec((1, D_MODEL, EXPERT_FF), lambda i, e, cnt: (e, 0, 0)),
            pl.BlockSpec((1, D_MODEL, EXPERT_FF), lambda i, e, cnt: (e, 0, 0)),
            pl.BlockSpec((1, EXPERT_FF, D_MODEL), lambda i, e, cnt: (e, 0, 0)),
            pl.BlockSpec((1, D_MODEL), lambda i, e, cnt: (0, 0)),
            pl.BlockSpec((1, D_MODEL), lambda i, e, cnt: (0, 0)),
        ],
        out_specs=pl.BlockSpec((MOE_TILE, D_MODEL), lambda i, e, cnt: (i, 0)),
        scratch_shapes=[pltpu.VMEM((N_EXPERTS, MOE_TILE), F32),
                        pltpu.VMEM((MOE_TILE // MOE_SUB, MOE_GROUP, MOE_SLOT, D_MODEL), BF16),
                        pltpu.VMEM((MOE_TILE // MOE_SUB, MOE_GROUP, MOE_SLOT, MOE_SUB), BF16),
                        pltpu.VMEM((MOE_GROUP, MOE_TILE // MOE_SUB, MOE_ROWS, D_MODEL), BF16)],
    )
    return pl.pallas_call(
        _moe_kernel,
        grid_spec=grid_spec,
        out_shape=jax.ShapeDtypeStruct((n_tok, D_MODEL), F32),
        compiler_params=pltpu.CompilerParams(dimension_semantics=("arbitrary", "arbitrary"),
                                             vmem_limit_bytes=V7X_VMEM_LIMIT_BYTES),
        name="moe_routed",
    )(cnt, x_bf16, sel_t, w_t, init, w_gate, w_up, w_down, ln_g.reshape(1, D_MODEL), ln_b.reshape(1, D_MODEL))


def _shared_ffn_kernel(x_ref, wg_ref, wu_ref, wd_ref, o_ref, xb_ref):
    x = x_ref[...]
    xb = x.astype(BF16)
    h = jax.nn.silu(jnp.dot(xb, wg_ref[...], preferred_element_type=F32)) * jnp.dot(
        xb, wu_ref[...], preferred_element_type=F32)
    o_ref[...] = DEEPNORM_ALPHA * x + jnp.dot(h.astype(BF16), wd_ref[...], preferred_element_type=F32)
    xb_ref[...] = xb


def shared_ffn(xt, wg, wu, wd, tm=1024):
    n_tok, d = xt.shape
    ff = wg.shape[1]
    once = pl.Buffered(1)
    return pl.pallas_call(
        _shared_ffn_kernel,
        grid=(n_tok // tm,),
        in_specs=[pl.BlockSpec((tm, d), lambda i: (i, 0)),
                  pl.BlockSpec((d, ff), lambda i: (0, 0), pipeline_mode=once),
                  pl.BlockSpec((d, ff), lambda i: (0, 0), pipeline_mode=once),
                  pl.BlockSpec((ff, d), lambda i: (0, 0), pipeline_mode=once)],
        out_specs=[pl.BlockSpec((tm, d), lambda i: (i, 0)), pl.BlockSpec((tm, d), lambda i: (i, 0))],
        out_shape=[jax.ShapeDtypeStruct((n_tok, d), F32), jax.ShapeDtypeStruct((n_tok, d), BF16)],
        compiler_params=pltpu.CompilerParams(dimension_semantics=("arbitrary",),
                                             vmem_limit_bytes=V7X_VMEM_LIMIT_BYTES),
        name="shared_ffn",
    )(xt, wg.astype(BF16), wu.astype(BF16), wd.astype(BF16))


def hybrid_layer(x, positions, w_in, lam_re, lam_im, log_dt, ssm_b_re, ssm_b_im, ssm_c_re, ssm_c_im, ssm_d,
                 w_glu, cmp_pos_k, cmp_pos_v, w_cmp_k1, w_cmp_k2, w_cmp_v1, w_cmp_v2, w_out, ln1_g, ln1_b,
                 w_router, router_bias, w_gate, w_up, w_down, ws_gate, ws_up, ws_down, ln2_g, ln2_b):
    bsz, L, _ = x.shape
    sizes = [SSM_WIDTH, NSA_WIDTH] + [KV_WIDTH] * 6 + [NSA_HEADS * N_BRANCH]
    o = [0] + [int(v) for v in np.cumsum(sizes)]
    col = lambda j: w_in[:, o[j]:o[j + 1]]
    dup = lambda w: jnp.concatenate([w[:, h * HEAD_DIM:(h + 1) * HEAD_DIM] for h in (0, 0, 1, 1)], axis=1)
    gate_cols = jnp.pad(col(8), ((0, 0), (0, 128 - NSA_HEADS * N_BRANCH)))
    w_uq = w_in[:, :o[2]].astype(BF16)
    w_kv = jnp.concatenate([col(4), col(6), dup(col(5)), dup(col(7)), col(2), col(3), gate_cols], axis=1).astype(BF16)

    xt = x.reshape(bsz * L, D_MODEL)
    u, q = proj_uq(xt, w_uq, positions.reshape(bsz * L, 1))
    kst, kwt, vs, vw, kc_raw, vc_raw, gate_pad = proj_kv(x, w_kv, positions.reshape(bsz, L, 1))
    kct, vcd = compress_kv(kc_raw, vc_raw, positions, cmp_pos_k, cmp_pos_v, w_cmp_k1, w_cmp_k2, w_cmp_v1, w_cmp_v2)
    y_s5 = s5_scan(u.reshape(bsz, L, SSM_WIDTH), lam_re, lam_im, log_dt, ssm_b_re, ssm_b_im, ssm_c_re, ssm_c_im, ssm_d)
    vw = vw.reshape(bsz, NSA_KV_HEADS, L // Q_BLOCK, Q_BLOCK, 128)
    y_nsa = nsa_attention(q.reshape(bsz, L, NSA_WIDTH), gate_pad, kct, vcd, kst, vs, kwt, vw)
    x1 = out_proj_ln(y_s5.reshape(bsz * L, SSM_WIDTH), y_nsa.reshape(bsz * L, NSA_WIDTH), xt, w_glu, w_out,
                     ln1_g, ln1_b)
    w_t, sel_t = moe_router(x1, w_router, router_bias)
    acc0, x1b = shared_ffn(x1, ws_gate, ws_up, ws_down)
    out = moe_routed(x1b, sel_t, w_t, acc0, w_gate.astype(BF16), w_up.astype(BF16), w_down.astype(BF16),
                     ln2_g, ln2_b)
    return out.reshape(bsz, L, D_MODEL)


def kernel(x, positions, w_in, lam_re, lam_im, log_dt, ssm_b_re, ssm_b_im, ssm_c_re, ssm_c_im, ssm_d, w_glu, cmp_pos_k, cmp_pos_v, w_cmp_k1, w_cmp_k2, w_cmp_v1, w_cmp_v2, w_out, ln1_g, ln1_b, w_router, router_bias, w_gate, w_up, w_down, ws_gate, ws_up, ws_down, ln2_g, ln2_b):
    params = (w_in, lam_re, lam_im, log_dt, ssm_b_re, ssm_b_im, ssm_c_re, ssm_c_im, ssm_d,
              w_glu, cmp_pos_k, cmp_pos_v, w_cmp_k1, w_cmp_k2, w_cmp_v1, w_cmp_v2, w_out, ln1_g, ln1_b,
              w_router, router_bias, w_gate, w_up, w_down, ws_gate, ws_up, ws_down, ln2_g, ln2_b)
    return hybrid_layer(x, positions, *(p[0] for p in params))
```

```python
import functools
import math

import numpy as np
import jax
import jax.numpy as jnp
from jax import lax
from jax.experimental import pallas as pl
from jax.experimental.pallas import tpu as pltpu

D_MODEL = 2048
SSM_WIDTH = 1024
SSM_CH_PER_GROUP = 16
SSM_GROUPS = 64
SSM_STATE = 64
NSA_HEADS = 16
NSA_KV_HEADS = 2
HEAD_DIM = 64
Q_PER_KV = NSA_HEADS // NSA_KV_HEADS
NSA_WIDTH = NSA_HEADS * HEAD_DIM
KV_WIDTH = NSA_KV_HEADS * HEAD_DIM
N_BRANCH = 3
CMP_BLOCK = 32
CMP_STRIDE = 16
SEL_BLOCK = 64
SEL_TOPK = 16
WINDOW = 512
Q_BLOCK = 128
ROPE_THETA = 10000.0
N_EXPERTS = 64
TOP_K = 8
N_EXPERT_GROUPS = 8
TOPK_GROUPS = 4
ROUTED_SCALE = 2.5
EXPERT_FF = 512
DEPTH = 1
DEEPNORM_ALPHA = (2.0 * DEPTH) ** 0.25
LN_EPS = 1e-5
NEG = -1e30
FORCE = 1e4
F32 = jnp.float32
BF16 = jnp.bfloat16

V7X_VMEM_LIMIT_BYTES = 56 * 1024 * 1024


def _layer_norm(x, g, b):
    mu = jnp.mean(x, -1, keepdims=True)
    var = jnp.mean(jnp.square(x - mu), -1, keepdims=True)
    return (x - mu) * lax.rsqrt(var + LN_EPS) * g + b


def _rope_tables(pos_col, inv_row):
    ang = pos_col * inv_row
    return jnp.cos(ang), jnp.sin(ang)


def _rope_lanes(x, cos, sin):
    lane = lax.broadcasted_iota(jnp.int32, (x.shape[0], 128), 1)
    first_half = (lane % HEAD_DIM) < HEAD_DIM // 2
    outs = []
    for blk in range(x.shape[1] // 128):
        xb = x[:, blk * 128:(blk + 1) * 128]
        rot = jnp.where(first_half, -pltpu.roll(xb, 128 - HEAD_DIM // 2, 1), pltpu.roll(xb, HEAD_DIM // 2, 1))
        outs.append(xb * cos + rot * sin)
    return outs[0] if len(outs) == 1 else jnp.concatenate(outs, axis=1)


def _inv_freq_row():
    half = HEAD_DIM // 2
    inv = ROPE_THETA ** (-jnp.arange(half, dtype=F32) / half)
    return jnp.tile(inv, 128 // half).reshape(1, 128)


PROJ_TILE = 512
Q_SCALE = HEAD_DIM ** -0.5 * math.log2(math.e)


def _proj_uq_kernel(x_ref, w_ref, pos_ref, inv_ref, u_ref, q_ref):
    acc = jnp.dot(x_ref[...].astype(BF16), w_ref[...], preferred_element_type=F32)
    u_ref[...] = acc[:, :SSM_WIDTH]
    cos, sin = _rope_tables(pos_ref[...].astype(F32), inv_ref[...])
    q_ref[...] = (_rope_lanes(acc[:, SSM_WIDTH:], cos, sin) * Q_SCALE).astype(BF16)


def proj_uq(xt, w_uq, pos_col, tm=1024):
    n_tok = xt.shape[0]
    return pl.pallas_call(
        _proj_uq_kernel,
        grid=(n_tok // tm,),
        in_specs=[pl.BlockSpec((tm, D_MODEL), lambda i: (i, 0)),
                  pl.BlockSpec((D_MODEL, SSM_WIDTH + NSA_WIDTH), lambda i: (0, 0), pipeline_mode=pl.Buffered(1)),
                  pl.BlockSpec((tm, 1), lambda i: (i, 0)),
                  pl.BlockSpec((1, 128), lambda i: (0, 0))],
        out_specs=[pl.BlockSpec((tm, SSM_WIDTH), lambda i: (i, 0)),
                   pl.BlockSpec((tm, NSA_WIDTH), lambda i: (i, 0))],
        out_shape=[jax.ShapeDtypeStruct((n_tok, SSM_WIDTH), F32), jax.ShapeDtypeStruct((n_tok, NSA_WIDTH), BF16)],
        compiler_params=pltpu.CompilerParams(dimension_semantics=("arbitrary",),
                                             vmem_limit_bytes=V7X_VMEM_LIMIT_BYTES),
        name="proj_uq",
    )(xt, w_uq, pos_col, _inv_freq_row())


KV_COLS = 4 * KV_WIDTH + 2 * 2 * KV_WIDTH + 128


def _proj_kv_kernel(x_ref, w_ref, pos_ref, inv_ref, kst_ref, kwt_ref, vs_ref, vw_ref, kc_ref, vc_ref, g_ref):
    acc = jnp.dot(x_ref[0].astype(BF16), w_ref[...], preferred_element_type=F32)
    cos, sin = _rope_tables(pos_ref[0].astype(F32), inv_ref[...])
    ks_t = _rope_lanes(acc[:, 0:128], cos, sin).T
    kw_t = _rope_lanes(acc[:, 128:256], cos, sin).T
    for k in range(NSA_KV_HEADS):
        kst_ref[0, k, 0] = ks_t[k * HEAD_DIM:(k + 1) * HEAD_DIM].astype(BF16)
        for j in range(PROJ_TILE // Q_BLOCK):
            kwt_ref[0, k, j] = kw_t[k * HEAD_DIM:(k + 1) * HEAD_DIM, j * Q_BLOCK:(j + 1) * Q_BLOCK].astype(BF16)
        vs_ref[0, k, 0] = acc[:, 256 + k * 128: 256 + (k + 1) * 128].astype(BF16)
        vw_ref[0, k] = acc[:, 512 + k * 128: 512 + (k + 1) * 128].astype(BF16)
    kc_ref[0] = acc[:, 768:896]
    vc_ref[0] = acc[:, 896:1024]
    g_ref[0] = acc[:, 1024:1152]


def proj_kv(x, w_kv, pos_col3):
    bsz, seq_len, _ = x.shape
    n_t = seq_len // PROJ_TILE
    per = PROJ_TILE // Q_BLOCK
    return pl.pallas_call(
        _proj_kv_kernel,
        grid=(bsz, n_t),
        in_specs=[pl.BlockSpec((1, PROJ_TILE, D_MODEL), lambda b, i: (b, i, 0)),
                  pl.BlockSpec((D_MODEL, KV_COLS), lambda b, i: (0, 0)),
                  pl.BlockSpec((1, PROJ_TILE, 1), lambda b, i: (b, i, 0)),
                  pl.BlockSpec((1, 128), lambda b, i: (0, 0))],
        out_specs=[
            pl.BlockSpec((1, NSA_KV_HEADS, 1, HEAD_DIM, PROJ_TILE), lambda b, i: (b, 0, i, 0, 0)),
            pl.BlockSpec((1, NSA_KV_HEADS, per, HEAD_DIM, Q_BLOCK), lambda b, i: (b, 0, i, 0, 0)),
            pl.BlockSpec((1, NSA_KV_HEADS, 1, PROJ_TILE, 128), lambda b, i: (b, 0, i, 0, 0)),
            pl.BlockSpec((1, NSA_KV_HEADS, PROJ_TILE, 128), lambda b, i: (b, 0, i, 0)),
            pl.BlockSpec((1, PROJ_TILE, 128), lambda b, i: (b, i, 0)),
            pl.BlockSpec((1, PROJ_TILE, 128), lambda b, i: (b, i, 0)),
            pl.BlockSpec((1, PROJ_TILE, 128), lambda b, i: (b, i, 0)),
        ],
        out_shape=[
            jax.ShapeDtypeStruct((bsz, NSA_KV_HEADS, n_t, HEAD_DIM, PROJ_TILE), BF16),
            jax.ShapeDtypeStruct((bsz, NSA_KV_HEADS, seq_len // Q_BLOCK, HEAD_DIM, Q_BLOCK), BF16),
            jax.ShapeDtypeStruct((bsz, NSA_KV_HEADS, n_t, PROJ_TILE, 128), BF16),
            jax.ShapeDtypeStruct((bsz, NSA_KV_HEADS, seq_len, 128), BF16),
            jax.ShapeDtypeStruct((bsz, seq_len, 128), F32),
            jax.ShapeDtypeStruct((bsz, seq_len, 128), F32),
            jax.ShapeDtypeStruct((bsz, seq_len, 128), F32),
        ],
        compiler_params=pltpu.CompilerParams(dimension_semantics=("arbitrary", "arbitrary"),
                                             vmem_limit_bytes=V7X_VMEM_LIMIT_BYTES),
        name="proj_kv",
    )(x, w_kv, pos_col3, _inv_freq_row())


def _compress_kernel(ck_ref, cv_ref, pek_ref, pev_ref, w1k_ref, w1v_ref, w2k_ref, w2v_ref, pos_ref, inv_ref,
                     kct_ref, vcd_ref):
    def hidden(c_ref, pe_ref, w1_ref):
        c = c_ref[0]
        lo = jnp.dot((c + pe_ref[0]).astype(BF16), w1_ref[0], preferred_element_type=F32)
        hi = jnp.dot((c + pe_ref[1]).astype(BF16), w1_ref[1], preferred_element_type=F32)
        hi_next = jnp.concatenate([hi[1:], jnp.zeros((1, hi.shape[1]), F32)], axis=0)
        return jax.nn.gelu(lo + hi_next).astype(BF16)

    kc = jnp.dot(hidden(ck_ref, pek_ref, w1k_ref), w2k_ref[...], preferred_element_type=F32)
    cos, sin = _rope_tables(pos_ref[0], inv_ref[...])
    kc_t = _rope_lanes(kc, cos, sin).T
    vc = jnp.dot(hidden(cv_ref, pev_ref, w1v_ref), w2v_ref[...], preferred_element_type=F32)
    for k in range(NSA_KV_HEADS):
        kct_ref[0, k] = kc_t[k * HEAD_DIM:(k + 1) * HEAD_DIM].astype(BF16)
        vcd_ref[0, k] = vc[:, k * 128:(k + 1) * 128].astype(BF16)


def compress_kv(kc_raw, vc_raw, positions, cmp_pos_k, cmp_pos_v, w_k1, w_k2, w_v1, w_v2):
    bsz, seq_len, _ = kc_raw.shape
    n_chunk = seq_len // CMP_STRIDE
    width = CMP_STRIDE * 128
    eye = jnp.eye(NSA_KV_HEADS, dtype=F32)

    def chunk_pe(pe):
        pe = pe.reshape(2, CMP_STRIDE, 1, HEAD_DIM)
        return jnp.broadcast_to(pe, (2, CMP_STRIDE, NSA_KV_HEADS, HEAD_DIM)).reshape(2, 1, width)

    def chunk_w1(w1):
        hid = w1.shape[1]
        w = w1.reshape(2, CMP_STRIDE, HEAD_DIM, hid)
        return jnp.einsum('htdj,kc->htkdcj', w, eye).reshape(2, width, NSA_KV_HEADS * hid).astype(BF16)

    hid = w_k2.shape[0]
    w2k = jnp.einsum('jd,kc->kjcd', w_k2, eye).reshape(NSA_KV_HEADS * hid, NSA_KV_HEADS * HEAD_DIM).astype(BF16)
    w2v = jnp.einsum('jd,kc,r->kjcrd', w_v2, eye, jnp.ones((2,), F32)).reshape(
        NSA_KV_HEADS * hid, NSA_KV_HEADS * 128).astype(BF16)
    pos = positions.astype(F32).reshape(bsz, n_chunk, CMP_STRIDE).sum(-1)
    pos_next = jnp.concatenate([pos[:, 1:], pos[:, -1:]], axis=1)
    cmp_pos = ((pos + pos_next) / CMP_BLOCK).reshape(bsz, n_chunk, 1)
    return pl.pallas_call(
        _compress_kernel,
        grid=(bsz,),
        in_specs=[pl.BlockSpec((1, n_chunk, width), lambda b: (b, 0, 0)),
                  pl.BlockSpec((1, n_chunk, width), lambda b: (b, 0, 0)),
                  pl.BlockSpec((2, 1, width), lambda b: (0, 0, 0)),
                  pl.BlockSpec((2, 1, width), lambda b: (0, 0, 0)),
                  pl.BlockSpec((2, width, NSA_KV_HEADS * hid), lambda b: (0, 0, 0)),
                  pl.BlockSpec((2, width, NSA_KV_HEADS * hid), lambda b: (0, 0, 0)),
                  pl.BlockSpec((NSA_KV_HEADS * hid, NSA_KV_HEADS * HEAD_DIM), lambda b: (0, 0)),
                  pl.BlockSpec((NSA_KV_HEADS * hid, NSA_KV_HEADS * 128), lambda b: (0, 0)),
                  pl.BlockSpec((1, n_chunk, 1), lambda b: (b, 0, 0)),
                  pl.BlockSpec((1, 128), lambda b: (0, 0))],
        out_specs=[pl.BlockSpec((1, NSA_KV_HEADS, HEAD_DIM, n_chunk), lambda b: (b, 0, 0, 0)),
                   pl.BlockSpec((1, NSA_KV_HEADS, n_chunk, 128), lambda b: (b, 0, 0, 0))],
        out_shape=[jax.ShapeDtypeStruct((bsz, NSA_KV_HEADS, HEAD_DIM, n_chunk), BF16),
                   jax.ShapeDtypeStruct((bsz, NSA_KV_HEADS, n_chunk, 128), BF16)],
        compiler_params=pltpu.CompilerParams(dimension_semantics=("arbitrary",),
                                             vmem_limit_bytes=V7X_VMEM_LIMIT_BYTES),
        name="compress_kv",
    )(kc_raw.reshape(bsz, n_chunk, width), vc_raw.reshape(bsz, n_chunk, width), chunk_pe(cmp_pos_k),
      chunk_pe(cmp_pos_v), chunk_w1(w_k1), chunk_w1(w_v1), w2k, w2v, cmp_pos, _inv_freq_row())


def _out_ln_kernel(y_ref, a_ref, x_ref, wglu_ref, wout_ref, g_ref, b_ref, o_ref):
    y = y_ref[...]
    y_ssm = y * jax.nn.sigmoid(jnp.dot(y.astype(BF16), wglu_ref[...], preferred_element_type=F32))
    mix = (jnp.dot(y_ssm.astype(BF16), wout_ref[:SSM_WIDTH, :], preferred_element_type=F32)
           + jnp.dot(a_ref[...].astype(BF16), wout_ref[SSM_WIDTH:, :], preferred_element_type=F32))
    o_ref[...] = _layer_norm(DEEPNORM_ALPHA * x_ref[...] + mix, g_ref[...], b_ref[...])


def out_proj_ln(y_s5, y_nsa, xt, w_glu, w_out, ln_g, ln_b, tm=512):
    n_tok = xt.shape[0]
    row = lambda i: (i, 0)
    const = lambda i: (0, 0)
    return pl.pallas_call(
        _out_ln_kernel,
        grid=(n_tok // tm,),
        in_specs=[pl.BlockSpec((tm, SSM_WIDTH), row), pl.BlockSpec((tm, NSA_WIDTH), row),
                  pl.BlockSpec((tm, D_MODEL), row),
                  pl.BlockSpec((SSM_WIDTH, SSM_WIDTH), const, pipeline_mode=pl.Buffered(1)),
                  pl.BlockSpec((D_MODEL, D_MODEL), const, pipeline_mode=pl.Buffered(1)),
                  pl.BlockSpec((1, D_MODEL), const),
                  pl.BlockSpec((1, D_MODEL), const)],
        out_specs=pl.BlockSpec((tm, D_MODEL), row),
        out_shape=jax.ShapeDtypeStruct((n_tok, D_MODEL), F32),
        compiler_params=pltpu.CompilerParams(dimension_semantics=("arbitrary",),
                                             vmem_limit_bytes=V7X_VMEM_LIMIT_BYTES),
        name="out_proj_ln",
    )(y_s5, y_nsa, xt, w_glu.astype(BF16), w_out.astype(BF16), ln_g.reshape(1, D_MODEL), ln_b.reshape(1, D_MODEL))


S5_CHUNK = 512
S5_SUB = S5_CHUNK // 8
S5_GROUPS_PER_BLOCK = 8
S5_STATES = S5_GROUPS_PER_BLOCK * SSM_STATE
S5_STREAMS = 2


def _cmul_add(ar, ai, xr, xi, br, bi):
    return ar * xr - ai * xi + br, ar * xi + ai * xr + bi


def _s5_kernel(u_ref, lam_ref, bmat_ref, cmat_ref, d_ref, perm_ref, permt_ref, o_ref,
               xr_scr, xi_scr, pr_scr, pi_scr, carry_scr, a_scr, bbar_scr):
    c = pl.program_id(2)
    streams = range(S5_STREAMS)

    @pl.when(c == 0)
    def _():
        powers = []
        for s in streams:
            lr, li = lam_ref[s, 0:1, :], lam_ref[s, 1:2, :]
            dt = jnp.exp(lam_ref[s, 2:3, :])
            mag = jnp.exp(lr * dt)
            ar, ai = mag * jnp.cos(li * dt), mag * jnp.sin(li * dt)
            zr, zi = ar - 1.0, ai
            den = lr * lr + li * li
            fr, fi = (zr * lr + zi * li) / den, (zi * lr - zr * li) / den
            a_scr[s, 0:1, :] = ar
            a_scr[s, 1:2, :] = ai
            b_re, b_im = bmat_ref[s, 0], bmat_ref[s, 1]
            bbar_scr[s, 0] = (fr * b_re - fi * b_im).astype(BF16)
            bbar_scr[s, 1] = (fr * b_im + fi * b_re).astype(BF16)
            powers += [jnp.broadcast_to(ar, (8, S5_STATES)), jnp.broadcast_to(ai, (8, S5_STATES))]
        carry_scr[...] = jnp.zeros(carry_scr.shape, F32)
        base = tuple(powers)

        def pw_body(i, pw):
            nxt = []
            for s in streams:
                pr, pi = pw[2 * s], pw[2 * s + 1]
                pr_scr[s, i] = pr
                pi_scr[s, i] = pi
                nxt += [base[2 * s] * pr - base[2 * s + 1] * pi, base[2 * s] * pi + base[2 * s + 1] * pr]
            return tuple(nxt)

        lax.fori_loop(0, S5_SUB, pw_body, base)

    a_re = [jnp.broadcast_to(a_scr[s, 0:1, :], (8, S5_STATES)) for s in streams]
    a_im = [jnp.broadcast_to(a_scr[s, 1:2, :], (8, S5_STATES)) for s in streams]
    perm = perm_ref[...]
    u = [u_ref[0, :, s * 128:(s + 1) * 128] for s in streams]
    for s in streams:
        u_p = jnp.dot(perm, u[s].astype(BF16), preferred_element_type=F32).astype(BF16)
        xr_scr[s] = jnp.dot(u_p, bbar_scr[s, 0], preferred_element_type=F32)
        xi_scr[s] = jnp.dot(u_p, bbar_scr[s, 1], preferred_element_type=F32)

    def scan_body(i, x):
        row = pl.multiple_of(i * 8, 8)
        out = []
        for s in streams:
            xr, xi = _cmul_add(a_re[s], a_im[s], x[2 * s], x[2 * s + 1],
                               xr_scr[s, pl.ds(row, 8), :], xi_scr[s, pl.ds(row, 8), :])
            xr_scr[s, pl.ds(row, 8), :] = xr
            xi_scr[s, pl.ds(row, 8), :] = xi
            out += [xr, xi]
        return tuple(out)

    zero = jnp.zeros((8, S5_STATES), F32)
    ends = lax.fori_loop(0, S5_SUB, scan_body, (zero,) * (2 * S5_STREAMS), unroll=4)

    cr, ci = [], []
    for s in streams:
        er, ei = ends[2 * s], ends[2 * s + 1]
        ar_s = pr_scr[s, S5_SUB - 1][0:1]
        ai_s = pi_scr[s, S5_SUB - 1][0:1]
        rows_r = [carry_scr[s, 0:1, :]]
        rows_i = [carry_scr[s, 1:2, :]]
        for j in range(8):
            nr, ni = _cmul_add(ar_s, ai_s, rows_r[-1], rows_i[-1], er[j:j + 1], ei[j:j + 1])
            rows_r.append(nr)
            rows_i.append(ni)
        carry_scr[s, 0:1, :] = rows_r[8]
        carry_scr[s, 1:2, :] = rows_i[8]
        cr.append(jnp.concatenate(rows_r[:8], axis=0))
        ci.append(jnp.concatenate(rows_i[:8], axis=0))

    def fix_body(i, carry):
        row = pl.multiple_of(i * 8, 8)
        for s in streams:
            xr, xi = _cmul_add(pr_scr[s, i], pi_scr[s, i], cr[s], ci[s],
                               xr_scr[s, pl.ds(row, 8), :], xi_scr[s, pl.ds(row, 8), :])
            xr_scr[s, pl.ds(row, 8), :] = xr
            xi_scr[s, pl.ds(row, 8), :] = xi
        return carry

    lax.fori_loop(0, S5_SUB, fix_body, 0, unroll=4)

    perm_t = permt_ref[...]
    for s in streams:
        y_p = (jnp.dot(xr_scr[s].astype(BF16), cmat_ref[s, 0], preferred_element_type=F32)
               - jnp.dot(xi_scr[s].astype(BF16), cmat_ref[s, 1], preferred_element_type=F32))
        y_hi = y_p.astype(BF16)
        y_lo = (y_p - y_hi.astype(F32)).astype(BF16)
        y = jnp.dot(perm_t, y_hi, preferred_element_type=F32) + jnp.dot(perm_t, y_lo, preferred_element_type=F32)
        o_ref[0, :, s * 128:(s + 1) * 128] = jax.nn.gelu(y + d_ref[s] * u[s])


def s5_scan(u, lam_re, lam_im, log_dt, b_re, b_im, c_re, c_im, d_skip):
    bsz, seq_len, _ = u.shape
    nb = SSM_GROUPS // S5_GROUPS_PER_BLOCK
    g = S5_STREAMS
    eye = jnp.eye(S5_GROUPS_PER_BLOCK, dtype=F32)

    def blockdiag_b(m):
        m = jnp.swapaxes(m, 1, 2).reshape(nb, S5_GROUPS_PER_BLOCK, SSM_CH_PER_GROUP, SSM_STATE)
        return jnp.einsum('nghp,gk->nghkp', m, eye).reshape(nb, 128, S5_STATES)

    def blockdiag_c(m):
        m = jnp.swapaxes(m, 1, 2).reshape(nb, S5_GROUPS_PER_BLOCK, SSM_STATE, SSM_CH_PER_GROUP)
        return jnp.einsum('ngph,gk->ngpkh', m, eye).reshape(nb, S5_STATES, 128)

    log_dt_states = jnp.broadcast_to(log_dt[:, None], lam_re.shape)
    lam = jnp.stack([m.reshape(nb, S5_STATES) for m in (lam_re, lam_im, log_dt_states)], axis=1)
    bmat = jnp.stack([blockdiag_b(b_re), blockdiag_b(b_im)], axis=1)
    cmat = jnp.stack([blockdiag_c(c_re), blockdiag_c(c_im)], axis=1).astype(BF16)
    d = d_skip.reshape(nb, 1, 128)
    r = np.arange(S5_CHUNK)
    perm = np.zeros((S5_CHUNK, S5_CHUNK), np.float32)
    perm[r, (r % 8) * S5_SUB + r // 8] = 1.0
    perm = jnp.asarray(perm, BF16)
    return pl.pallas_call(
        _s5_kernel,
        grid=(bsz, nb // g, seq_len // S5_CHUNK),
        in_specs=[
            pl.BlockSpec((1, S5_CHUNK, 128 * g), lambda b, k, c: (b, c, k)),
            pl.BlockSpec((g, 3, S5_STATES), lambda b, k, c: (k, 0, 0)),
            pl.BlockSpec((g, 2, 128, S5_STATES), lambda b, k, c: (k, 0, 0, 0)),
            pl.BlockSpec((g, 2, S5_STATES, 128), lambda b, k, c: (k, 0, 0, 0)),
            pl.BlockSpec((g, 1, 128), lambda b, k, c: (k, 0, 0)),
            pl.BlockSpec((S5_CHUNK, S5_CHUNK), lambda b, k, c: (0, 0)),
            pl.BlockSpec((S5_CHUNK, S5_CHUNK), lambda b, k, c: (0, 0)),
        ],
        out_specs=pl.BlockSpec((1, S5_CHUNK, 128 * g), lambda b, k, c: (b, c, k)),
        out_shape=jax.ShapeDtypeStruct((bsz, seq_len, SSM_WIDTH), F32),
        scratch_shapes=[pltpu.VMEM((g, S5_CHUNK, S5_STATES), F32), pltpu.VMEM((g, S5_CHUNK, S5_STATES), F32),
                        pltpu.VMEM((g, S5_SUB, 8, S5_STATES), F32), pltpu.VMEM((g, S5_SUB, 8, S5_STATES), F32),
                        pltpu.VMEM((g, 2, S5_STATES), F32), pltpu.VMEM((g, 2, S5_STATES), F32),
                        pltpu.VMEM((g, 2, 128, S5_STATES), BF16)],
        compiler_params=pltpu.CompilerParams(
            dimension_semantics=("arbitrary", "arbitrary", "arbitrary"), vmem_limit_bytes=V7X_VMEM_LIMIT_BYTES),
        name="s5_scan",
    )(u, lam, bmat, cmat, d, perm, perm.T)


def _softmax_tile(s, m_old):
    m_new = jnp.maximum(m_old, jnp.max(s, axis=1, keepdims=True))
    m_wide = jnp.concatenate([m_new] * (s.shape[1] // 128), axis=1)
    return m_new, jnp.exp2(m_old - m_new), jnp.exp2(s - m_wide)


def _lane_is_low(shape):
    return lax.broadcasted_iota(jnp.int32, shape, 1) < HEAD_DIM


def _pad_kt(kt, variant):
    z = jnp.zeros_like(kt)
    return jnp.concatenate([kt, z] if variant == 0 else [z, kt], axis=0)


def _pad_v(vv, variant):
    low = _lane_is_low(vv.shape)
    keep = low if variant == 0 else jnp.logical_not(low)
    return jnp.where(keep, vv, jnp.ones_like(vv))


def _finish(acc, variant):
    lane = lax.broadcasted_iota(jnp.int32, acc.shape, 1)
    lsel = lane == (HEAD_DIM if variant == 0 else 0)
    l = jnp.sum(jnp.where(lsel, acc, 0.0), axis=1, keepdims=True)
    keep = (lane < HEAD_DIM) if variant == 0 else (lane >= HEAD_DIM)
    return jnp.where(keep, acc / l, 0.0)


def _nsa_kernel(q_ref, g_ref, kct_ref, vc_ref, kst_ref, vs_ref, kwt_ref, vw_ref, ovl_ref, gx_ref, o_ref,
                m_scr, acc_scr, s_scr_a, s_scr_b, p_scr, *, seq_len):
    s_slots = (s_scr_a, s_scr_b)
    n_sel = seq_len // SEL_BLOCK
    n_cpad = seq_len // CMP_STRIDE
    sel_tile = PROJ_TILE
    blocks_per_tile = sel_tile // SEL_BLOCK
    win_tiles = WINDOW // Q_BLOCK + 1
    n_pair = Q_PER_KV // 2
    rows = n_pair * Q_BLOCK
    i = pl.program_id(2)
    t0 = i * Q_BLOCK

    qb = q_ref[0]
    qst = jnp.concatenate([qb[:, p * 128:(p + 1) * 128] for p in range(n_pair)], axis=0)

    sig = jax.nn.sigmoid(g_ref[0])
    sig_hi = sig.astype(BF16)
    sig_lo = (sig - sig_hi.astype(F32)).astype(BF16)
    gx = gx_ref[0]
    gexp = (jnp.dot(sig_hi, gx, preferred_element_type=F32) + jnp.dot(sig_lo, gx, preferred_element_type=F32))

    def gate_of(branch):
        base = branch * n_pair * 128
        return jnp.concatenate([gexp[:, base + p * 128: base + (p + 1) * 128] for p in range(n_pair)], axis=0)

    t_row = t0 + lax.broadcasted_iota(jnp.int32, (Q_BLOCK, 1), 0)

    slab = 64
    kct = kct_ref[0, 0]
    s_cmp = [jnp.dot(qst, _pad_kt(kct, v), preferred_element_type=F32) for v in range(2)]
    n_kblk = seq_len // Q_BLOCK
    w0 = jnp.clip(i - (win_tiles - 1), 0, n_kblk - win_tiles)
    kw = jnp.concatenate([kwt_ref[0, 0, w0 + j] for j in range(win_tiles)], axis=1)
    s_win = [jnp.dot(qst, _pad_kt(kw, v), preferred_element_type=F32) for v in range(2)]
    for v in range(2):
        s_slots[0][v] = jnp.dot(qst, _pad_kt(kst_ref[0, 0, 0], v), preferred_element_type=F32)

    n_iota = lax.broadcasted_iota(jnp.int32, (Q_BLOCK, n_cpad), 1)
    cmask = (n_iota * CMP_STRIDE + (CMP_BLOCK - 1)) <= t_row
    cmask4 = jnp.concatenate([cmask] * n_pair, axis=0)
    vcd = vc_ref[0, 0]
    p_sum = jnp.zeros((Q_BLOCK, n_cpad), F32)
    out = jnp.zeros((rows, 128), F32)
    o_c = jnp.zeros((rows, 128), F32)
    for v in range(2):
        s = jnp.where(cmask4, s_cmp[v], NEG)
        m = jnp.max(s, axis=1, keepdims=True)
        e = jnp.where(cmask4, jnp.exp2(s - m), 0.0)
        l = jnp.sum(e, axis=1, keepdims=True)
        p = e * (1.0 / jnp.maximum(l, 1e-30))
        for pp in range(n_pair):
            p_sum = p_sum + p[pp * Q_BLOCK:(pp + 1) * Q_BLOCK]
        low = _lane_is_low((n_cpad, 128))
        vz = jnp.where(low if v == 0 else jnp.logical_not(low), vcd, jnp.zeros_like(vcd))
        o_c = o_c + jnp.dot(p.astype(BF16), vz, preferred_element_type=F32)
    out = out + gate_of(0) * o_c

    ps_hi = p_sum.astype(BF16)
    ps_lo = (p_sum - ps_hi.astype(F32)).astype(BF16)
    ovl = ovl_ref[...]
    nt = (((1,), (1,)), ((), ()))
    imp_t = (lax.dot_general(ovl, ps_hi, nt, preferred_element_type=F32)
             + lax.dot_general(ovl, ps_lo, nt, preferred_element_type=F32))

    vw = jnp.concatenate([vw_ref[0, 0, w0 + j] for j in range(win_tiles)], axis=0)
    kpos_w = w0 * Q_BLOCK + lax.broadcasted_iota(jnp.int32, (Q_BLOCK, win_tiles * Q_BLOCK), 1)
    diff = t_row - kpos_w
    wbias = jnp.where((diff >= 0) & (diff < WINDOW), 0.0, NEG)
    wbias4 = jnp.concatenate([wbias] * n_pair, axis=0)
    o_w = jnp.zeros((rows, 128), F32)
    for v in range(2):
        s = s_win[v] + wbias4
        m = jnp.max(s, axis=1, keepdims=True)
        p = jnp.exp2(s - m)
        o_w = o_w + _finish(jnp.dot(p.astype(BF16), _pad_v(vw, v), preferred_element_type=F32), v)
    out = out + gate_of(2) * o_w

    s_iota = lax.broadcasted_iota(jnp.int32, (n_sel, Q_BLOCK), 0)
    t_lane = t0 + lax.broadcasted_iota(jnp.int32, (n_sel, Q_BLOCK), 1)
    cur = t_lane // SEL_BLOCK
    forced = (s_iota == 0) | (s_iota == cur) | (s_iota == cur - 1)
    valid = s_iota * SEL_BLOCK <= t_lane
    score = jnp.where(forced, FORCE, jnp.where(valid, imp_t, -1.0))
    s_f = s_iota.astype(F32)
    sel_t = jnp.zeros((n_sel, Q_BLOCK), F32)
    for _ in range(min(SEL_TOPK, n_sel)):
        mx = jnp.max(score, axis=0, keepdims=True)
        idx = jnp.min(jnp.where(score == mx, s_f, float(n_sel)), axis=0, keepdims=True)
        hit = s_f == idx
        sel_t = jnp.where(hit, 1.0, sel_t)
        score = jnp.where(hit, -3e38, score)
    selmask = sel_t.T.astype(BF16)

    m_scr[...] = jnp.full(m_scr.shape, NEG, F32)
    acc_scr[...] = jnp.zeros(acc_scr.shape, F32)
    n_tiles = (t0 + Q_BLOCK + sel_tile - 1) // sel_tile

    last_tile = seq_len // sel_tile - 1

    def bias_of(kt):
        blk = kt * blocks_per_tile + lax.broadcasted_iota(jnp.int32, (n_sel, sel_tile), 1) // SEL_BLOCK
        expand = (lax.broadcasted_iota(jnp.int32, (n_sel, sel_tile), 0) == blk).astype(BF16)
        selexp = jnp.dot(selmask, expand, preferred_element_type=F32)
        kpos = kt * sel_tile + lax.broadcasted_iota(jnp.int32, (Q_BLOCK, sel_tile), 1)
        bias = jnp.where((selexp > 0.5) & (kpos <= t_row), 0.0, NEG)
        return jnp.concatenate([bias] * n_pair, axis=0)

    def scores_into(slot, kt):
        bias4 = bias_of(kt)
        kt_tile = kst_ref[0, 0, jnp.minimum(kt, last_tile)]
        for v in range(2):
            s_slots[slot][v] = jnp.dot(qst, _pad_kt(kt_tile, v), preferred_element_type=F32) + bias4

    def attend_from(slot, kt):
        v_tile = vs_ref[0, 0, jnp.minimum(kt, last_tile)]
        for v in range(2):
            for h in range(rows // slab):
                r = slice(h * slab, (h + 1) * slab)
                m_new, alpha, p = _softmax_tile(s_slots[slot][v, r, :], m_scr[v, r, :])
                m_scr[v, r, :] = m_new
                acc_scr[v, r, :] = alpha * acc_scr[v, r, :]
                p_scr[v, r, :] = p.astype(BF16)
            acc_scr[v] += jnp.dot(p_scr[v], _pad_v(v_tile, v), preferred_element_type=F32)

    bias_first = bias_of(0)
    for v in range(2):
        s_slots[0][v] = s_slots[0][v] + bias_first

    def sel_body(j, carry):
        kt = 2 * j
        scores_into(1, kt + 1)
        attend_from(0, kt)
        scores_into(0, kt + 2)
        attend_from(1, kt + 1)
        return carry

    lax.fori_loop(0, (n_tiles + 1) // 2, sel_body, 0)
    out = out + gate_of(1) * (_finish(acc_scr[0], 0) + _finish(acc_scr[1], 1))

    o_ref[0] = jnp.concatenate([out[p * Q_BLOCK:(p + 1) * Q_BLOCK] for p in range(n_pair)], axis=1)


def nsa_attention(q, gate_pad, kct, vc, kst, vs, kwt, vw):
    bsz, seq_len, _ = q.shape
    n_sel = seq_len // SEL_BLOCK
    n_cpad = seq_len // CMP_STRIDE
    n_cmp = (seq_len - CMP_BLOCK) // CMP_STRIDE + 1
    n_pair = Q_PER_KV // 2
    cs = np.arange(n_cpad) * CMP_STRIDE
    ce = cs + CMP_BLOCK - 1
    ss = np.arange(n_sel) * SEL_BLOCK
    se = ss + SEL_BLOCK - 1
    ovl = (cs[None, :] <= se[:, None]) & (ce[None, :] >= ss[:, None]) & (np.arange(n_cpad)[None, :] < n_cmp)
    ovl = jnp.asarray(ovl.astype(np.float32), BF16)
    gx = np.zeros((NSA_KV_HEADS, 128, N_BRANCH * n_pair * 128), np.float32)
    for k in range(NSA_KV_HEADS):
        for hl in range(Q_PER_KV):
            for br in range(N_BRANCH):
                c0 = br * n_pair * 128 + hl * HEAD_DIM
                gx[k, (k * Q_PER_KV + hl) * N_BRANCH + br, c0:c0 + HEAD_DIM] = 1.0
    gx = jnp.asarray(gx, BF16)
    width = Q_PER_KV * HEAD_DIM
    full = lambda *shape: pl.BlockSpec((1, 1) + shape, lambda b, k, i: (b, k) + (0,) * len(shape))
    return pl.pallas_call(
        functools.partial(_nsa_kernel, seq_len=seq_len),
        grid=(bsz, NSA_KV_HEADS, seq_len // Q_BLOCK),
        in_specs=[
            pl.BlockSpec((1, Q_BLOCK, width), lambda b, k, i: (b, i, k)),
            pl.BlockSpec((1, Q_BLOCK, 128), lambda b, k, i: (b, i, 0)),
            full(HEAD_DIM, n_cpad), full(n_cpad, 128),
            full(seq_len // 512, HEAD_DIM, 512), full(seq_len // 512, 512, 128),
            full(seq_len // Q_BLOCK, HEAD_DIM, Q_BLOCK), full(seq_len // Q_BLOCK, Q_BLOCK, 128),
            pl.BlockSpec((n_sel, n_cpad), lambda b, k, i: (0, 0)),
            pl.BlockSpec((1, 128, N_BRANCH * n_pair * 128), lambda b, k, i: (k, 0, 0)),
        ],
        out_specs=pl.BlockSpec((1, Q_BLOCK, width), lambda b, k, i: (b, i, k)),
        out_shape=jax.ShapeDtypeStruct((bsz, seq_len, NSA_WIDTH), F32),
        scratch_shapes=[pltpu.VMEM((2, n_pair * Q_BLOCK, 128), F32), pltpu.VMEM((2, n_pair * Q_BLOCK, 128), F32),
                        pltpu.VMEM((2, n_pair * Q_BLOCK, 512), F32), pltpu.VMEM((2, n_pair * Q_BLOCK, 512), F32),
                        pltpu.VMEM((2, n_pair * Q_BLOCK, 512), BF16)],
        compiler_params=pltpu.CompilerParams(
            dimension_semantics=("arbitrary", "arbitrary", "arbitrary"), vmem_limit_bytes=V7X_VMEM_LIMIT_BYTES),
        name="nsa_attention",
    )(q, gate_pad, kct, vc, kst, vs, kwt, vw, ovl, gx)


ROUTER_TILE = 512
MOE_TILE = 1024
MOE_SUB = 256
MOE_ROWS = 48
MOE_SLOT = 64
MOE_GROUP = 4


def _first_max_mask(x, idx_f, axis):
    mx = jnp.max(x, axis=axis, keepdims=True)
    first = jnp.min(jnp.where(x == mx, idx_f, 1e9), axis=axis, keepdims=True)
    return idx_f == first, mx


def _router_kernel(x_ref, wrt_ref, bias_ref, w_ref, sel_ref):
    per_group = N_EXPERTS // N_EXPERT_GROUPS
    tr = x_ref.shape[0]
    nt = (((1,), (1,)), ((), ()))
    logits = lax.dot_general(wrt_ref[...], x_ref[...].astype(BF16), nt, preferred_element_type=F32)
    aff = jax.nn.sigmoid(logits)
    biased = aff + bias_ref[...]
    grp = biased.reshape(N_EXPERT_GROUPS, per_group, tr)
    in_grp = lax.broadcasted_iota(jnp.int32, grp.shape, 1).astype(F32)
    hit1, m1 = _first_max_mask(grp, in_grp, 1)
    m2 = jnp.max(jnp.where(hit1, -jnp.inf, grp), axis=1, keepdims=True)
    gscore = (m1 + m2).reshape(N_EXPERT_GROUPS, tr)
    g_idx = lax.broadcasted_iota(jnp.int32, gscore.shape, 0).astype(F32)
    gsel = jnp.zeros(gscore.shape, F32)
    for _ in range(TOPK_GROUPS):
        hit, _ = _first_max_mask(gscore, g_idx, 0)
        gsel = jnp.where(hit, 1.0, gsel)
        gscore = jnp.where(hit, -jnp.inf, gscore)
    gmask = jnp.broadcast_to(gsel.reshape(N_EXPERT_GROUPS, 1, tr), grp.shape).reshape(N_EXPERTS, tr)
    cand = jnp.where(gmask > 0.5, biased, NEG)
    e_idx = lax.broadcasted_iota(jnp.int32, cand.shape, 0).astype(F32)
    sel = jnp.zeros(cand.shape, F32)
    for _ in range(TOP_K):
        hit, _ = _first_max_mask(cand, e_idx, 0)
        sel = jnp.where(hit, 1.0, sel)
        cand = jnp.where(hit, -jnp.inf, cand)
    w = jnp.where(sel > 0.5, aff, 0.0)
    w_ref[...] = w / jnp.sum(w, axis=0, keepdims=True) * ROUTED_SCALE
    sel_ref[...] = sel


def moe_router(xt, w_router, router_bias):
    n_tok = xt.shape[0]
    wrt = w_router.T.astype(BF16)
    return pl.pallas_call(
        _router_kernel,
        grid=(n_tok // ROUTER_TILE,),
        in_specs=[pl.BlockSpec((ROUTER_TILE, D_MODEL), lambda i: (i, 0)),
                  pl.BlockSpec((N_EXPERTS, D_MODEL), lambda i: (0, 0)),
                  pl.BlockSpec((N_EXPERTS, 1), lambda i: (0, 0))],
        out_specs=[pl.BlockSpec((N_EXPERTS, ROUTER_TILE), lambda i: (0, i)),
                   pl.BlockSpec((N_EXPERTS, ROUTER_TILE), lambda i: (0, i))],
        out_shape=[jax.ShapeDtypeStruct((N_EXPERTS, n_tok), F32), jax.ShapeDtypeStruct((N_EXPERTS, n_tok), F32)],
        compiler_params=pltpu.CompilerParams(dimension_semantics=("arbitrary",),
                                             vmem_limit_bytes=V7X_VMEM_LIMIT_BYTES),
        name="moe_router",
    )(xt, wrt, router_bias.reshape(N_EXPERTS, 1))


def _moe_kernel(cnt_ref, x_ref, sel_ref, w_ref, init_ref, wg_ref, wu_ref, wd_ref, lng_ref, lnb_ref, o_ref,
                rank_scr, ybuf_scr, sbuf_scr, xe_scr):
    i = pl.program_id(0)
    e = pl.program_id(1)
    tm = x_ref.shape[0]
    n_sub = tm // MOE_SUB
    tn = (((0,), (0,)), ((), ()))

    @pl.when(e == 0)
    def _():
        o_ref[...] = init_ref[...]
        before = (lax.broadcasted_iota(jnp.int32, (MOE_SUB, MOE_SUB), 0)
                  < lax.broadcasted_iota(jnp.int32, (MOE_SUB, MOE_SUB), 1))
        before = jnp.where(before, 1.0, 0.0).astype(BF16)
        for q in range(n_sub):
            cols = slice(q * MOE_SUB, (q + 1) * MOE_SUB)
            rank_scr[:, cols] = jnp.dot(sel_ref[:, cols].astype(BF16), before, preferred_element_type=F32)

    count = cnt_ref[i * N_EXPERTS + e]
    sel_e = sel_ref[pl.ds(e, 1), :]
    rank_e = rank_scr[pl.ds(e, 1), :]
    w_e = w_ref[pl.ds(e, 1), :]

    def one_hots(rank_row, sel_row, w_row, c):
        row = (c * MOE_ROWS + lax.broadcasted_iota(jnp.int32, (MOE_ROWS, MOE_SUB), 0)).astype(F32)
        hits = []
        for q in range(n_sub):
            cols = slice(q * MOE_SUB, (q + 1) * MOE_SUB)
            hits.append((rank_row[:, cols] == row) & (sel_row[:, cols] > 0.5))
        return hits

    def swiglu(xe):
        g = jnp.dot(xe, wg_ref[0], preferred_element_type=F32)
        u = jnp.dot(xe, wu_ref[0], preferred_element_type=F32)
        h = (jax.nn.silu(g) * u).astype(BF16)
        return jnp.dot(h, wd_ref[0], preferred_element_type=F32).astype(BF16)

    def weighted(hits):
        return [jnp.where(hits[q], w_e[:, q * MOE_SUB:(q + 1) * MOE_SUB], 0.0).astype(BF16) for q in range(n_sub)]

    slot = e % MOE_GROUP

    @pl.when(slot == 0)
    def _():
        stacks = [[] for _ in range(n_sub)]
        for gi in range(MOE_GROUP):
            hits = one_hots(rank_scr[pl.ds(e + gi, 1), :], sel_ref[pl.ds(e + gi, 1), :], None, 0)
            for q in range(n_sub):
                stacks[q].append(jnp.where(hits[q], 1.0, 0.0).astype(BF16))
        for q in range(n_sub):
            cols = slice(q * MOE_SUB, (q + 1) * MOE_SUB)
            xg = jnp.dot(jnp.concatenate(stacks[q], axis=0), x_ref[cols, :],
                         preferred_element_type=F32).astype(BF16)
            for gi in range(MOE_GROUP):
                xe_scr[gi, q] = xg[gi * MOE_ROWS:(gi + 1) * MOE_ROWS]

    scatters = weighted(one_hots(rank_e, sel_e, w_e, 0))
    y = swiglu(xe_scr[slot].reshape(n_sub * MOE_ROWS, D_MODEL))
    spare = MOE_SLOT - MOE_ROWS
    for q in range(n_sub):
        sbuf_scr[q, slot] = jnp.concatenate([scatters[q], jnp.zeros((spare, MOE_SUB), BF16)], axis=0)
        ybuf_scr[q, slot] = jnp.concatenate(
            [y[q * MOE_ROWS:(q + 1) * MOE_ROWS], jnp.zeros((spare, D_MODEL), BF16)], axis=0)

    @pl.when(slot == MOE_GROUP - 1)
    def _():
        for q in range(n_sub):
            cols = slice(q * MOE_SUB, (q + 1) * MOE_SUB)
            o_ref[cols, :] += lax.dot_general(sbuf_scr[q].reshape(MOE_GROUP * MOE_SLOT, MOE_SUB),
                                              ybuf_scr[q].reshape(MOE_GROUP * MOE_SLOT, D_MODEL), tn,
                                              preferred_element_type=F32)

    def overflow_body(c, carry):
        hits = one_hots(rank_e, sel_e, w_e, c)
        sc = weighted(hits)
        xe = jnp.concatenate(
            [jnp.dot(jnp.where(hits[q], 1.0, 0.0).astype(BF16), x_ref[q * MOE_SUB:(q + 1) * MOE_SUB, :],
                     preferred_element_type=F32).astype(BF16) for q in range(n_sub)], axis=0)
        yy = swiglu(xe)
        for q in range(n_sub):
            cols = slice(q * MOE_SUB, (q + 1) * MOE_SUB)
            o_ref[cols, :] += lax.dot_general(sc[q], yy[q * MOE_ROWS:(q + 1) * MOE_ROWS], tn,
                                              preferred_element_type=F32)
        return carry

    lax.fori_loop(1, (count + MOE_ROWS - 1) // MOE_ROWS, overflow_body, 0)

    @pl.when(e == N_EXPERTS - 1)
    def _():
        o_ref[...] = _layer_norm(o_ref[...], lng_ref[...], lnb_ref[...])


def moe_routed(x_bf16, sel_t, w_t, init, w_gate, w_up, w_down, ln_g, ln_b):
    n_tok = x_bf16.shape[0]
    n_tiles = n_tok // MOE_TILE
    per_sub = jnp.sum(sel_t.reshape(N_EXPERTS, n_tiles, MOE_TILE // MOE_SUB, MOE_SUB), axis=-1)
    cnt = jnp.max(per_sub, axis=-1).T.astype(jnp.int32).reshape(-1)
    grid_spec = pltpu.PrefetchScalarGridSpec(
        num_scalar_prefetch=1,
        grid=(n_tiles, N_EXPERTS),
        in_specs=[
            pl.BlockSpec((MOE_TILE, D_MODEL), lambda i, e, cnt: (i, 0), pipeline_mode=pl.Buffered(1)),
            pl.BlockSpec((N_EXPERTS, MOE_TILE), lambda i, e, cnt: (0, i)),
            pl.BlockSpec((N_EXPERTS, MOE_TILE), lambda i, e, cnt: (0, i)),
            pl.BlockSpec((MOE_TILE, D_MODEL), lambda i, e, cnt: (i, 0), pipeline_mode=pl.Buffered(1)),
            pl.BlockSpec((1, D_MODEL, EXPERT_FF), lambda i, e, cnt: (e, 0, 0)),
            pl.BlockSpec((1, D_MODEL, EXPERT_FF), lambda i, e, cnt: (e, 0, 0)),
            pl.BlockSpec((1, EXPERT_FF, D_MODEL), lambda i, e, cnt: (e, 0, 0)),
            pl.BlockSpec((1, D_MODEL), lambda i, e, cnt: (0, 0)),
            pl.BlockSpec((1, D_MODEL), lambda i, e, cnt: (0, 0)),
        ],
        out_specs=pl.BlockSpec((MOE_TILE, D_MODEL), lambda i, e, cnt: (i, 0)),
        scratch_shapes=[pltpu.VMEM((N_EXPERTS, MOE_TILE), F32),
                        pltpu.VMEM((MOE_TILE // MOE_SUB, MOE_GROUP, MOE_SLOT, D_MODEL), BF16),
                        pltpu.VMEM((MOE_TILE // MOE_SUB, MOE_GROUP, MOE_SLOT, MOE_SUB), BF16),
                        pltpu.VMEM((MOE_GROUP, MOE_TILE // MOE_SUB, MOE_ROWS, D_MODEL), BF16)],
    )
    return pl.pallas_call(
        _moe_kernel,
        grid_spec=grid_spec,
        out_shape=jax.ShapeDtypeStruct((n_tok, D_MODEL), F32),
        compiler_params=pltpu.CompilerParams(dimension_semantics=("arbitrary", "arbitrary"),
                                             vmem_limit_bytes=V7X_VMEM_LIMIT_BYTES),
        name="moe_routed",
    )(cnt, x_bf16, sel_t, w_t, init, w_gate, w_up, w_down, ln_g.reshape(1, D_MODEL), ln_b.reshape(1, D_MODEL))


def _shared_ffn_kernel(x_ref, wg_ref, wu_ref, wd_ref, o_ref, xb_ref):
    x = x_ref[...]
    xb = x.astype(BF16)
    h = jax.nn.silu(jnp.dot(xb, wg_ref[...], preferred_element_type=F32)) * jnp.dot(
        xb, wu_ref[...], preferred_element_type=F32)
    o_ref[...] = DEEPNORM_ALPHA * x + jnp.dot(h.astype(BF16), wd_ref[...], preferred_element_type=F32)
    xb_ref[...] = xb


def shared_ffn(xt, wg, wu, wd, tm=1024):
    n_tok, d = xt.shape
    ff = wg.shape[1]
    once = pl.Buffered(1)
    return pl.pallas_call(
        _shared_ffn_kernel,
        grid=(n_tok // tm,),
        in_specs=[pl.BlockSpec((tm, d), lambda i: (i, 0)),
                  pl.BlockSpec((d, ff), lambda i: (0, 0), pipeline_mode=once),
                  pl.BlockSpec((d, ff), lambda i: (0, 0), pipeline_mode=once),
                  pl.BlockSpec((ff, d), lambda i: (0, 0), pipeline_mode=once)],
        out_specs=[pl.BlockSpec((tm, d), lambda i: (i, 0)), pl.BlockSpec((tm, d), lambda i: (i, 0))],
        out_shape=[jax.ShapeDtypeStruct((n_tok, d), F32), jax.ShapeDtypeStruct((n_tok, d), BF16)],
        compiler_params=pltpu.CompilerParams(dimension_semantics=("arbitrary",),
                                             vmem_limit_bytes=V7X_VMEM_LIMIT_BYTES),
        name="shared_ffn",
    )(xt, wg.astype(BF16), wu.astype(BF16), wd.astype(BF16))


def hybrid_layer(x, positions, w_in, lam_re, lam_im, log_dt, ssm_b_re, ssm_b_im, ssm_c_re, ssm_c_im, ssm_d,
                 w_glu, cmp_pos_k, cmp_pos_v, w_cmp_k1, w_cmp_k2, w_cmp_v1, w_cmp_v2, w_out, ln1_g, ln1_b,
                 w_router, router_bias, w_gate, w_up, w_down, ws_gate, ws_up, ws_down, ln2_g, ln2_b):
    bsz, L, _ = x.shape
    sizes = [SSM_WIDTH, NSA_WIDTH] + [KV_WIDTH] * 6 + [NSA_HEADS * N_BRANCH]
    o = [0] + [int(v) for v in np.cumsum(sizes)]
    col = lambda j: w_in[:, o[j]:o[j + 1]]
    dup = lambda w: jnp.concatenate([w[:, h * HEAD_DIM:(h + 1) * HEAD_DIM] for h in (0, 0, 1, 1)], axis=1)
    gate_cols = jnp.pad(col(8), ((0, 0), (0, 128 - NSA_HEADS * N_BRANCH)))
    w_uq = w_in[:, :o[2]].astype(BF16)
    w_kv = jnp.concatenate([col(4), col(6), dup(col(5)), dup(col(7)), col(2), col(3), gate_cols], axis=1).astype(BF16)

    xt = x.reshape(bsz * L, D_MODEL)
    u, q = proj_uq(xt, w_uq, positions.reshape(bsz * L, 1))
    kst, kwt, vs, vw, kc_raw, vc_raw, gate_pad = proj_kv(x, w_kv, positions.reshape(bsz, L, 1))
    kct, vcd = compress_kv(kc_raw, vc_raw, positions, cmp_pos_k, cmp_pos_v, w_cmp_k1, w_cmp_k2, w_cmp_v1, w_cmp_v2)
    y_s5 = s5_scan(u.reshape(bsz, L, SSM_WIDTH), lam_re, lam_im, log_dt, ssm_b_re, ssm_b_im, ssm_c_re, ssm_c_im, ssm_d)
    vw = vw.reshape(bsz, NSA_KV_HEADS, L // Q_BLOCK, Q_BLOCK, 128)
    y_nsa = nsa_attention(q.reshape(bsz, L, NSA_WIDTH), gate_pad, kct, vcd, kst, vs, kwt, vw)
    x1 = out_proj_ln(y_s5.reshape(bsz * L, SSM_WIDTH), y_nsa.reshape(bsz * L, NSA_WIDTH), xt, w_glu, w_out,
                     ln1_g, ln1_b)
    w_t, sel_t = moe_router(x1, w_router, router_bias)
    acc0, x1b = shared_ffn(x1, ws_gate, ws_up, ws_down)
    out = moe_routed(x1b, sel_t, w_t, acc0, w_gate.astype(BF16), w_up.astype(BF16), w_down.astype(BF16),
                     ln2_g, ln2_b)
    return out.reshape(bsz, L, D_MODEL)


def kernel(x, positions, w_in, lam_re, lam_im, log_dt, ssm_b_re, ssm_b_im, ssm_c_re, ssm_c_im, ssm_d, w_glu, cmp_pos_k, cmp_pos_v, w_cmp_k1, w_cmp_k2, w_cmp_v1, w_cmp_v2, w_out, ln1_g, ln1_b, w_router, router_bias, w_gate, w_up, w_down, ws_gate, ws_up, ws_down, ln2_g, ln2_b):
    params = (w_in, lam_re, lam_im, log_dt, ssm_b_re, ssm_b_im, ssm_c_re, ssm_c_im, ssm_d,
              w_glu, cmp_pos_k, cmp_pos_v, w_cmp_k1, w_cmp_k2, w_cmp_v1, w_cmp_v2, w_out, ln1_g, ln1_b,
              w_router, router_bias, w_gate, w_up, w_down, ws_gate, ws_up, ws_down, ln2_g, ln2_b)
    return hybrid_layer(x, positions, *(p[0] for p in params))
```

```python
import functools
import math

import numpy as np
import jax
import jax.numpy as jnp
from jax import lax
from jax.experimental import pallas as pl
from jax.experimental.pallas import tpu as pltpu

D_MODEL = 2048
SSM_WIDTH = 1024
SSM_CH_PER_GROUP = 16
SSM_GROUPS = 64
SSM_STATE = 64
NSA_HEADS = 16
NSA_KV_HEADS = 2
HEAD_DIM = 64
Q_PER_KV = NSA_HEADS // NSA_KV_HEADS
NSA_WIDTH = NSA_HEADS * HEAD_DIM
KV_WIDTH = NSA_KV_HEADS * HEAD_DIM
N_BRANCH = 3
CMP_BLOCK = 32
CMP_STRIDE = 16
SEL_BLOCK = 64
SEL_TOPK = 16
WINDOW = 512
Q_BLOCK = 128
ROPE_THETA = 10000.0
N_EXPERTS = 64
TOP_K = 8
N_EXPERT_GROUPS = 8
TOPK_GROUPS = 4
ROUTED_SCALE = 2.5
EXPERT_FF = 512
DEPTH = 1
DEEPNORM_ALPHA = (2.0 * DEPTH) ** 0.25
LN_EPS = 1e-5
NEG = -1e30
FORCE = 1e4
F32 = jnp.float32
BF16 = jnp.bfloat16

V7X_VMEM_LIMIT_BYTES = 56 * 1024 * 1024


def _layer_norm(x, g, b):
    mu = jnp.mean(x, -1, keepdims=True)
    var = jnp.mean(jnp.square(x - mu), -1, keepdims=True)
    return (x - mu) * lax.rsqrt(var + LN_EPS) * g + b


def _rope_tables(pos_col, inv_row):
    ang = pos_col * inv_row
    return jnp.cos(ang), jnp.sin(ang)


def _rope_lanes(x, cos, sin):
    lane = lax.broadcasted_iota(jnp.int32, (x.shape[0], 128), 1)
    first_half = (lane % HEAD_DIM) < HEAD_DIM // 2
    outs = []
    for blk in range(x.shape[1] // 128):
        xb = x[:, blk * 128:(blk + 1) * 128]
        rot = jnp.where(first_half, -pltpu.roll(xb, 128 - HEAD_DIM // 2, 1), pltpu.roll(xb, HEAD_DIM // 2, 1))
        outs.append(xb * cos + rot * sin)
    return outs[0] if len(outs) == 1 else jnp.concatenate(outs, axis=1)


def _inv_freq_row():
    half = HEAD_DIM // 2
    inv = ROPE_THETA ** (-jnp.arange(half, dtype=F32) / half)
    return jnp.tile(inv, 128 // half).reshape(1, 128)


PROJ_TILE = 512
Q_SCALE = HEAD_DIM ** -0.5 * math.log2(math.e)


def _proj_uq_kernel(x_ref, w_ref, pos_ref, inv_ref, u_ref, q_ref):
    acc = jnp.dot(x_ref[...].astype(BF16), w_ref[...], preferred_element_type=F32)
    u_ref[...] = acc[:, :SSM_WIDTH]
    cos, sin = _rope_tables(pos_ref[...].astype(F32), inv_ref[...])
    q_ref[...] = (_rope_lanes(acc[:, SSM_WIDTH:], cos, sin) * Q_SCALE).astype(BF16)


def proj_uq(xt, w_uq, pos_col, tm=1024):
    n_tok = xt.shape[0]
    return pl.pallas_call(
        _proj_uq_kernel,
        grid=(n_tok // tm,),
        in_specs=[pl.BlockSpec((tm, D_MODEL), lambda i: (i, 0)),
                  pl.BlockSpec((D_MODEL, SSM_WIDTH + NSA_WIDTH), lambda i: (0, 0), pipeline_mode=pl.Buffered(1)),
                  pl.BlockSpec((tm, 1), lambda i: (i, 0)),
                  pl.BlockSpec((1, 128), lambda i: (0, 0))],
        out_specs=[pl.BlockSpec((tm, SSM_WIDTH), lambda i: (i, 0)),
                   pl.BlockSpec((tm, NSA_WIDTH), lambda i: (i, 0))],
        out_shape=[jax.ShapeDtypeStruct((n_tok, SSM_WIDTH), F32), jax.ShapeDtypeStruct((n_tok, NSA_WIDTH), BF16)],
        compiler_params=pltpu.CompilerParams(dimension_semantics=("arbitrary",),
                                             vmem_limit_bytes=V7X_VMEM_LIMIT_BYTES),
        name="proj_uq",
    )(xt, w_uq, pos_col, _inv_freq_row())


KV_COLS = 4 * KV_WIDTH + 2 * 2 * KV_WIDTH + 128


def _proj_kv_kernel(x_ref, w_ref, pos_ref, inv_ref, kst_ref, kwt_ref, vs_ref, vw_ref, kc_ref, vc_ref, g_ref):
    acc = jnp.dot(x_ref[0].astype(BF16), w_ref[...], preferred_element_type=F32)
    cos, sin = _rope_tables(pos_ref[0].astype(F32), inv_ref[...])
    ks_t = _rope_lanes(acc[:, 0:128], cos, sin).T
    kw_t = _rope_lanes(acc[:, 128:256], cos, sin).T
    for k in range(NSA_KV_HEADS):
        kst_ref[0, k, 0] = ks_t[k * HEAD_DIM:(k + 1) * HEAD_DIM].astype(BF16)
        for j in range(PROJ_TILE // Q_BLOCK):
            kwt_ref[0, k, j] = kw_t[k * HEAD_DIM:(k + 1) * HEAD_DIM, j * Q_BLOCK:(j + 1) * Q_BLOCK].astype(BF16)
        vs_ref[0, k, 0] = acc[:, 256 + k * 128: 256 + (k + 1) * 128].astype(BF16)
        vw_ref[0, k] = acc[:, 512 + k * 128: 512 + (k + 1) * 128].astype(BF16)
    kc_ref[0] = acc[:, 768:896]
    vc_ref[0] = acc[:, 896:1024]
    g_ref[0] = acc[:, 1024:1152]


def proj_kv(x, w_kv, pos_col3):
    bsz, seq_len, _ = x.shape
    n_t = seq_len // PROJ_TILE
    per = PROJ_TILE // Q_BLOCK
    return pl.pallas_call(
        _proj_kv_kernel,
        grid=(bsz, n_t),
        in_specs=[pl.BlockSpec((1, PROJ_TILE, D_MODEL), lambda b, i: (b, i, 0)),
                  pl.BlockSpec((D_MODEL, KV_COLS), lambda b, i: (0, 0)),
                  pl.BlockSpec((1, PROJ_TILE, 1), lambda b, i: (b, i, 0)),
                  pl.BlockSpec((1, 128), lambda b, i: (0, 0))],
        out_specs=[
            pl.BlockSpec((1, NSA_KV_HEADS, 1, HEAD_DIM, PROJ_TILE), lambda b, i: (b, 0, i, 0, 0)),
            pl.BlockSpec((1, NSA_KV_HEADS, per, HEAD_DIM, Q_BLOCK), lambda b, i: (b, 0, i, 0, 0)),
            pl.BlockSpec((1, NSA_KV_HEADS, 1, PROJ_TILE, 128), lambda b, i: (b, 0, i, 0, 0)),
            pl.BlockSpec((1, NSA_KV_HEADS, PROJ_TILE, 128), lambda b, i: (b, 0, i, 0)),
            pl.BlockSpec((1, PROJ_TILE, 128), lambda b, i: (b, i, 0)),
            pl.BlockSpec((1, PROJ_TILE, 128), lambda b, i: (b, i, 0)),
            pl.BlockSpec((1, PROJ_TILE, 128), lambda b, i: (b, i, 0)),
        ],
        out_shape=[
            jax.ShapeDtypeStruct((bsz, NSA_KV_HEADS, n_t, HEAD_DIM, PROJ_TILE), BF16),
            jax.ShapeDtypeStruct((bsz, NSA_KV_HEADS, seq_len // Q_BLOCK, HEAD_DIM, Q_BLOCK), BF16),
            jax.ShapeDtypeStruct((bsz, NSA_KV_HEADS, n_t, PROJ_TILE, 128), BF16),
            jax.ShapeDtypeStruct((bsz, NSA_KV_HEADS, seq_len, 128), BF16),
            jax.ShapeDtypeStruct((bsz, seq_len, 128), F32),
            jax.ShapeDtypeStruct((bsz, seq_len, 128), F32),
            jax.ShapeDtypeStruct((bsz, seq_len, 128), F32),
        ],
        compiler_params=pltpu.CompilerParams(dimension_semantics=("arbitrary", "arbitrary"),
                                             vmem_limit_bytes=V7X_VMEM_LIMIT_BYTES),
        name="proj_kv",
    )(x, w_kv, pos_col3, _inv_freq_row())


def _compress_kernel(ck_ref, cv_ref, pek_ref, pev_ref, w1k_ref, w1v_ref, w2k_ref, w2v_ref, pos_ref, inv_ref,
                     kct_ref, vcd_ref):
    def hidden(c_ref, pe_ref, w1_ref):
        c = c_ref[0]
        lo = jnp.dot((c + pe_ref[0]).astype(BF16), w1_ref[0], preferred_element_type=F32)
        hi = jnp.dot((c + pe_ref[1]).astype(BF16), w1_ref[1], preferred_element_type=F32)
        hi_next = jnp.concatenate([hi[1:], jnp.zeros((1, hi.shape[1]), F32)], axis=0)
        return jax.nn.gelu(lo + hi_next).astype(BF16)

    kc = jnp.dot(hidden(ck_ref, pek_ref, w1k_ref), w2k_ref[...], preferred_element_type=F32)
    cos, sin = _rope_tables(pos_ref[0], inv_ref[...])
    kc_t = _rope_lanes(kc, cos, sin).T
    vc = jnp.dot(hidden(cv_ref, pev_ref, w1v_ref), w2v_ref[...], preferred_element_type=F32)
    for k in range(NSA_KV_HEADS):
        kct_ref[0, k] = kc_t[k * HEAD_DIM:(k + 1) * HEAD_DIM].astype(BF16)
        vcd_ref[0, k] = vc[:, k * 128:(k + 1) * 128].astype(BF16)


def compress_kv(kc_raw, vc_raw, positions, cmp_pos_k, cmp_pos_v, w_k1, w_k2, w_v1, w_v2):
    bsz, seq_len, _ = kc_raw.shape
    n_chunk = seq_len // CMP_STRIDE
    width = CMP_STRIDE * 128
    eye = jnp.eye(NSA_KV_HEADS, dtype=F32)

    def chunk_pe(pe):
        pe = pe.reshape(2, CMP_STRIDE, 1, HEAD_DIM)
        return jnp.broadcast_to(pe, (2, CMP_STRIDE, NSA_KV_HEADS, HEAD_DIM)).reshape(2, 1, width)

    def chunk_w1(w1):
        hid = w1.shape[1]
        w = w1.reshape(2, CMP_STRIDE, HEAD_DIM, hid)
        return jnp.einsum('htdj,kc->htkdcj', w, eye).reshape(2, width, NSA_KV_HEADS * hid).astype(BF16)

    hid = w_k2.shape[0]
    w2k = jnp.einsum('jd,kc->kjcd', w_k2, eye).reshape(NSA_KV_HEADS * hid, NSA_KV_HEADS * HEAD_DIM).astype(BF16)
    w2v = jnp.einsum('jd,kc,r->kjcrd', w_v2, eye, jnp.ones((2,), F32)).reshape(
        NSA_KV_HEADS * hid, NSA_KV_HEADS * 128).astype(BF16)
    pos = positions.astype(F32).reshape(bsz, n_chunk, CMP_STRIDE).sum(-1)
    pos_next = jnp.concatenate([pos[:, 1:], pos[:, -1:]], axis=1)
    cmp_pos = ((pos + pos_next) / CMP_BLOCK).reshape(bsz, n_chunk, 1)
    return pl.pallas_call(
        _compress_kernel,
        grid=(bsz,),
        in_specs=[pl.BlockSpec((1, n_chunk, width), lambda b: (b, 0, 0)),
                  pl.BlockSpec((1, n_chunk, width), lambda b: (b, 0, 0)),
                  pl.BlockSpec((2, 1, width), lambda b: (0, 0, 0)),
                  pl.BlockSpec((2, 1, width), lambda b: (0, 0, 0)),
                  pl.BlockSpec((2, width, NSA_KV_HEADS * hid), lambda b: (0, 0, 0)),
                  pl.BlockSpec((2, width, NSA_KV_HEADS * hid), lambda b: (0, 0, 0)),
                  pl.BlockSpec((NSA_KV_HEADS * hid, NSA_KV_HEADS * HEAD_DIM), lambda b: (0, 0)),
                  pl.BlockSpec((NSA_KV_HEADS * hid, NSA_KV_HEADS * 128), lambda b: (0, 0)),
                  pl.BlockSpec((1, n_chunk, 1), lambda b: (b, 0, 0)),
                  pl.BlockSpec((1, 128), lambda b: (0, 0))],
        out_specs=[pl.BlockSpec((1, NSA_KV_HEADS, HEAD_DIM, n_chunk), lambda b: (b, 0, 0, 0)),
                   pl.BlockSpec((1, NSA_KV_HEADS, n_chunk, 128), lambda b: (b, 0, 0, 0))],
        out_shape=[jax.ShapeDtypeStruct((bsz, NSA_KV_HEADS, HEAD_DIM, n_chunk), BF16),
                   jax.ShapeDtypeStruct((bsz, NSA_KV_HEADS, n_chunk, 128), BF16)],
        compiler_params=pltpu.CompilerParams(dimension_semantics=("arbitrary",),
                                             vmem_limit_bytes=V7X_VMEM_LIMIT_BYTES),
        name="compress_kv",
    )(kc_raw.reshape(bsz, n_chunk, width), vc_raw.reshape(bsz, n_chunk, width), chunk_pe(cmp_pos_k),
      chunk_pe(cmp_pos_v), chunk_w1(w_k1), chunk_w1(w_v1), w2k, w2v, cmp_pos, _inv_freq_row())


def _out_ln_kernel(y_ref, a_ref, x_ref, wglu_ref, wout_ref, g_ref, b_ref, o_ref):
    y = y_ref[...]
    y_ssm = y * jax.nn.sigmoid(jnp.dot(y.astype(BF16), wglu_ref[...], preferred_element_type=F32))
    mix = (jnp.dot(y_ssm.astype(BF16), wout_ref[:SSM_WIDTH, :], preferred_element_type=F32)
           + jnp.dot(a_ref[...].astype(BF16), wout_ref[SSM_WIDTH:, :], preferred_element_type=F32))
    o_ref[...] = _layer_norm(DEEPNORM_ALPHA * x_ref[...] + mix, g_ref[...], b_ref[...])


def out_proj_ln(y_s5, y_nsa, xt, w_glu, w_out, ln_g, ln_b, tm=512):
    n_tok = xt.shape[0]
    row = lambda i: (i, 0)
    const = lambda i: (0, 0)
    return pl.pallas_call(
        _out_ln_kernel,
        grid=(n_tok // tm,),
        in_specs=[pl.BlockSpec((tm, SSM_WIDTH), row), pl.BlockSpec((tm, NSA_WIDTH), row),
                  pl.BlockSpec((tm, D_MODEL), row),
                  pl.BlockSpec((SSM_WIDTH, SSM_WIDTH), const, pipeline_mode=pl.Buffered(1)),
                  pl.BlockSpec((D_MODEL, D_MODEL), const, pipeline_mode=pl.Buffered(1)),
                  pl.BlockSpec((1, D_MODEL), const),
                  pl.BlockSpec((1, D_MODEL), const)],
        out_specs=pl.BlockSpec((tm, D_MODEL), row),
        out_shape=jax.ShapeDtypeStruct((n_tok, D_MODEL), F32),
        compiler_params=pltpu.CompilerParams(dimension_semantics=("arbitrary",),
                                             vmem_limit_bytes=V7X_VMEM_LIMIT_BYTES),
        name="out_proj_ln",
    )(y_s5, y_nsa, xt, w_glu.astype(BF16), w_out.astype(BF16), ln_g.reshape(1, D_MODEL), ln_b.reshape(1, D_MODEL))


S5_CHUNK = 512
S5_SUB = S5_CHUNK // 8
S5_GROUPS_PER_BLOCK = 8
S5_STATES = S5_GROUPS_PER_BLOCK * SSM_STATE
S5_STREAMS = 2


def _cmul_add(ar, ai, xr, xi, br, bi):
    return ar * xr - ai * xi + br, ar * xi + ai * xr + bi


def _s5_kernel(u_ref, lam_ref, bmat_ref, cmat_ref, d_ref, perm_ref, permt_ref, o_ref,
               xr_scr, xi_scr, pr_scr, pi_scr, carry_scr, a_scr, bbar_scr):
    c = pl.program_id(2)
    streams = range(S5_STREAMS)

    @pl.when(c == 0)
    def _():
        powers = []
        for s in streams:
            lr, li = lam_ref[s, 0:1, :], lam_ref[s, 1:2, :]
            dt = jnp.exp(lam_ref[s, 2:3, :])
            mag = jnp.exp(lr * dt)
            ar, ai = mag * jnp.cos(li * dt), mag * jnp.sin(li * dt)
            zr, zi = ar - 1.0, ai
            den = lr * lr + li * li
            fr, fi = (zr * lr + zi * li) / den, (zi * lr - zr * li) / den
            a_scr[s, 0:1, :] = ar
            a_scr[s, 1:2, :] = ai
            b_re, b_im = bmat_ref[s, 0], bmat_ref[s, 1]
            bbar_scr[s, 0] = (fr * b_re - fi * b_im).astype(BF16)
            bbar_scr[s, 1] = (fr * b_im + fi * b_re).astype(BF16)
            powers += [jnp.broadcast_to(ar, (8, S5_STATES)), jnp.broadcast_to(ai, (8, S5_STATES))]
        carry_scr[...] = jnp.zeros(carry_scr.shape, F32)
        base = tuple(powers)

        def pw_body(i, pw):
            nxt = []
            for s in streams:
                pr, pi = pw[2 * s], pw[2 * s + 1]
                pr_scr[s, i] = pr
                pi_scr[s, i] = pi
                nxt += [base[2 * s] * pr - base[2 * s + 1] * pi, base[2 * s] * pi + base[2 * s + 1] * pr]
            return tuple(nxt)

        lax.fori_loop(0, S5_SUB, pw_body, base)

    a_re = [jnp.broadcast_to(a_scr[s, 0:1, :], (8, S5_STATES)) for s in streams]
    a_im = [jnp.broadcast_to(a_scr[s, 1:2, :], (8, S5_STATES)) for s in streams]
    perm = perm_ref[...]
    u = [u_ref[0, :, s * 128:(s + 1) * 128] for s in streams]
    for s in streams:
        u_p = jnp.dot(perm, u[s].astype(BF16), preferred_element_type=F32).astype(BF16)
        xr_scr[s] = jnp.dot(u_p, bbar_scr[s, 0], preferred_element_type=F32)
        xi_scr[s] = jnp.dot(u_p, bbar_scr[s, 1], preferred_element_type=F32)

    def scan_body(i, x):
        row = pl.multiple_of(i * 8, 8)
        out = []
        for s in streams:
            xr, xi = _cmul_add(a_re[s], a_im[s], x[2 * s], x[2 * s + 1],
                               xr_scr[s, pl.ds(row, 8), :], xi_scr[s, pl.ds(row, 8), :])
            xr_scr[s, pl.ds(row, 8), :] = xr
            xi_scr[s, pl.ds(row, 8), :] = xi
            out += [xr, xi]
        return tuple(out)

    zero = jnp.zeros((8, S5_STATES), F32)
    ends = lax.fori_loop(0, S5_SUB, scan_body, (zero,) * (2 * S5_STREAMS), unroll=4)

    cr, ci = [], []
    for s in streams:
        er, ei = ends[2 * s], ends[2 * s + 1]
        ar_s = pr_scr[s, S5_SUB - 1][0:1]
        ai_s = pi_scr[s, S5_SUB - 1][0:1]
        rows_r = [carry_scr[s, 0:1, :]]
        rows_i = [carry_scr[s, 1:2, :]]
        for j in range(8):
            nr, ni = _cmul_add(ar_s, ai_s, rows_r[-1], rows_i[-1], er[j:j + 1], ei[j:j + 1])
            rows_r.append(nr)
            rows_i.append(ni)
        carry_scr[s, 0:1, :] = rows_r[8]
        carry_scr[s, 1:2, :] = rows_i[8]
        cr.append(jnp.concatenate(rows_r[:8], axis=0))
        ci.append(jnp.concatenate(rows_i[:8], axis=0))

    def fix_body(i, carry):
        row = pl.multiple_of(i * 8, 8)
        for s in streams:
            xr, xi = _cmul_add(pr_scr[s, i], pi_scr[s, i], cr[s], ci[s],
                               xr_scr[s, pl.ds(row, 8), :], xi_scr[s, pl.ds(row, 8), :])
            xr_scr[s, pl.ds(row, 8), :] = xr
            xi_scr[s, pl.ds(row, 8), :] = xi
        return carry

    lax.fori_loop(0, S5_SUB, fix_body, 0, unroll=4)

    perm_t = permt_ref[...]
    for s in streams:
        y_p = (jnp.dot(xr_scr[s].astype(BF16), cmat_ref[s, 0], preferred_element_type=F32)
               - jnp.dot(xi_scr[s].astype(BF16), cmat_ref[s, 1], preferred_element_type=F32))
        y_hi = y_p.astype(BF16)
        y_lo = (y_p - y_hi.astype(F32)).astype(BF16)
        y = jnp.dot(perm_t, y_hi, preferred_element_type=F32) + jnp.dot(perm_t, y_lo, preferred_element_type=F32)
        o_ref[0, :, s * 128:(s + 1) * 128] = jax.nn.gelu(y + d_ref[s] * u[s])


def s5_scan(u, lam_re, lam_im, log_dt, b_re, b_im, c_re, c_im, d_skip):
    bsz, seq_len, _ = u.shape
    nb = SSM_GROUPS // S5_GROUPS_PER_BLOCK
    g = S5_STREAMS
    eye = jnp.eye(S5_GROUPS_PER_BLOCK, dtype=F32)

    def blockdiag_b(m):
        m = jnp.swapaxes(m, 1, 2).reshape(nb, S5_GROUPS_PER_BLOCK, SSM_CH_PER_GROUP, SSM_STATE)
        return jnp.einsum('nghp,gk->nghkp', m, eye).reshape(nb, 128, S5_STATES)

    def blockdiag_c(m):
        m = jnp.swapaxes(m, 1, 2).reshape(nb, S5_GROUPS_PER_BLOCK, SSM_STATE, SSM_CH_PER_GROUP)
        return jnp.einsum('ngph,gk->ngpkh', m, eye).reshape(nb, S5_STATES, 128)

    log_dt_states = jnp.broadcast_to(log_dt[:, None], lam_re.shape)
    lam = jnp.stack([m.reshape(nb, S5_STATES) for m in (lam_re, lam_im, log_dt_states)], axis=1)
    bmat = jnp.stack([blockdiag_b(b_re), blockdiag_b(b_im)], axis=1)
    cmat = jnp.stack([blockdiag_c(c_re), blockdiag_c(c_im)], axis=1).astype(BF16)
    d = d_skip.reshape(nb, 1, 128)
    r = np.arange(S5_CHUNK)
    perm = np.zeros((S5_CHUNK, S5_CHUNK), np.float32)
    perm[r, (r % 8) * S5_SUB + r // 8] = 1.0
    perm = jnp.asarray(perm, BF16)
    return pl.pallas_call(
        _s5_kernel,
        grid=(bsz, nb // g, seq_len // S5_CHUNK),
        in_specs=[
            pl.BlockSpec((1, S5_CHUNK, 128 * g), lambda b, k, c: (b, c, k)),
            pl.BlockSpec((g, 3, S5_STATES), lambda b, k, c: (k, 0, 0)),
            pl.BlockSpec((g, 2, 128, S5_STATES), lambda b, k, c: (k, 0, 0, 0)),
            pl.BlockSpec((g, 2, S5_STATES, 128), lambda b, k, c: (k, 0, 0, 0)),
            pl.BlockSpec((g, 1, 128), lambda b, k, c: (k, 0, 0)),
            pl.BlockSpec((S5_CHUNK, S5_CHUNK), lambda b, k, c: (0, 0)),
            pl.BlockSpec((S5_CHUNK, S5_CHUNK), lambda b, k, c: (0, 0)),
        ],
        out_specs=pl.BlockSpec((1, S5_CHUNK, 128 * g), lambda b, k, c: (b, c, k)),
        out_shape=jax.ShapeDtypeStruct((bsz, seq_len, SSM_WIDTH), F32),
        scratch_shapes=[pltpu.VMEM((g, S5_CHUNK, S5_STATES), F32), pltpu.VMEM((g, S5_CHUNK, S5_STATES), F32),
                        pltpu.VMEM((g, S5_SUB, 8, S5_STATES), F32), pltpu.VMEM((g, S5_SUB, 8, S5_STATES), F32),
                        pltpu.VMEM((g, 2, S5_STATES), F32), pltpu.VMEM((g, 2, S5_STATES), F32),
                        pltpu.VMEM((g, 2, 128, S5_STATES), BF16)],
        compiler_params=pltpu.CompilerParams(
            dimension_semantics=("arbitrary", "arbitrary", "arbitrary"), vmem_limit_bytes=V7X_VMEM_LIMIT_BYTES),
        name="s5_scan",
    )(u, lam, bmat, cmat, d, perm, perm.T)


def _softmax_tile(s, m_old):
    m_new = jnp.maximum(m_old, jnp.max(s, axis=1, keepdims=True))
    m_wide = jnp.concatenate([m_new] * (s.shape[1] // 128), axis=1)
    return m_new, jnp.exp2(m_old - m_new), jnp.exp2(s - m_wide)


def _lane_is_low(shape):
    return lax.broadcasted_iota(jnp.int32, shape, 1) < HEAD_DIM


def _pad_kt(kt, variant):
    z = jnp.zeros_like(kt)
    return jnp.concatenate([kt, z] if variant == 0 else [z, kt], axis=0)


def _pad_v(vv, variant):
    low = _lane_is_low(vv.shape)
    keep = low if variant == 0 else jnp.logical_not(low)
    return jnp.where(keep, vv, jnp.ones_like(vv))


def _finish(acc, variant):
    lane = lax.broadcasted_iota(jnp.int32, acc.shape, 1)
    lsel = lane == (HEAD_DIM if variant == 0 else 0)
    l = jnp.sum(jnp.where(lsel, acc, 0.0), axis=1, keepdims=True)
    keep = (lane < HEAD_DIM) if variant == 0 else (lane >= HEAD_DIM)
    return jnp.where(keep, acc / l, 0.0)


def _nsa_kernel(q_ref, g_ref, kct_ref, vc_ref, kst_ref, vs_ref, kwt_ref, vw_ref, ovl_ref, gx_ref, o_ref,
                m_scr, acc_scr, s_scr_a, s_scr_b, p_scr, *, seq_len):
    s_slots = (s_scr_a, s_scr_b)
    n_sel = seq_len // SEL_BLOCK
    n_cpad = seq_len // CMP_STRIDE
    sel_tile = PROJ_TILE
    blocks_per_tile = sel_tile // SEL_BLOCK
    win_tiles = WINDOW // Q_BLOCK + 1
    n_pair = Q_PER_KV // 2
    rows = n_pair * Q_BLOCK
    i = pl.program_id(2)
    t0 = i * Q_BLOCK

    qb = q_ref[0]
    qst = jnp.concatenate([qb[:, p * 128:(p + 1) * 128] for p in range(n_pair)], axis=0)

    sig = jax.nn.sigmoid(g_ref[0])
    sig_hi = sig.astype(BF16)
    sig_lo = (sig - sig_hi.astype(F32)).astype(BF16)
    gx = gx_ref[0]
    gexp = (jnp.dot(sig_hi, gx, preferred_element_type=F32) + jnp.dot(sig_lo, gx, preferred_element_type=F32))

    def gate_of(branch):
        base = branch * n_pair * 128
        return jnp.concatenate([gexp[:, base + p * 128: base + (p + 1) * 128] for p in range(n_pair)], axis=0)

    t_row = t0 + lax.broadcasted_iota(jnp.int32, (Q_BLOCK, 1), 0)

    slab = 64
    kct = kct_ref[0, 0]
    s_cmp = [jnp.dot(qst, _pad_kt(kct, v), preferred_element_type=F32) for v in range(2)]
    n_kblk = seq_len // Q_BLOCK
    w0 = jnp.clip(i - (win_tiles - 1), 0, n_kblk - win_tiles)
    kw = jnp.concatenate([kwt_ref[0, 0, w0 + j] for j in range(win_tiles)], axis=1)
    s_win = [jnp.dot(qst, _pad_kt(kw, v), preferred_element_type=F32) for v in range(2)]
    for v in range(2):
        s_slots[0][v] = jnp.dot(qst, _pad_kt(kst_ref[0, 0, 0], v), preferred_element_type=F32)

    n_iota = lax.broadcasted_iota(jnp.int32, (Q_BLOCK, n_cpad), 1)
    cmask = (n_iota * CMP_STRIDE + (CMP_BLOCK - 1)) <= t_row
    cmask4 = jnp.concatenate([cmask] * n_pair, axis=0)
    vcd = vc_ref[0, 0]
    p_sum = jnp.zeros((Q_BLOCK, n_cpad), F32)
    out = jnp.zeros((rows, 128), F32)
    o_c = jnp.zeros((rows, 128), F32)
    for v in range(2):
        s = jnp.where(cmask4, s_cmp[v], NEG)
        m = jnp.max(s, axis=1, keepdims=True)
        e = jnp.where(cmask4, jnp.exp2(s - m), 0.0)
        l = jnp.sum(e, axis=1, keepdims=True)
        p = e * (1.0 / jnp.maximum(l, 1e-30))
        for pp in range(n_pair):
            p_sum = p_sum + p[pp * Q_BLOCK:(pp + 1) * Q_BLOCK]
        low = _lane_is_low((n_cpad, 128))
        vz = jnp.where(low if v == 0 else jnp.logical_not(low), vcd, jnp.zeros_like(vcd))
        o_c = o_c + jnp.dot(p.astype(BF16), vz, preferred_element_type=F32)
    out = out + gate_of(0) * o_c

    ps_hi = p_sum.astype(BF16)
    ps_lo = (p_sum - ps_hi.astype(F32)).astype(BF16)
    ovl = ovl_ref[...]
    nt = (((1,), (1,)), ((), ()))
    imp_t = (lax.dot_general(ovl, ps_hi, nt, preferred_element_type=F32)
             + lax.dot_general(ovl, ps_lo, nt, preferred_element_type=F32))

    vw = jnp.concatenate([vw_ref[0, 0, w0 + j] for j in range(win_tiles)], axis=0)
    kpos_w = w0 * Q_BLOCK + lax.broadcasted_iota(jnp.int32, (Q_BLOCK, win_tiles * Q_BLOCK), 1)
    diff = t_row - kpos_w
    wbias = jnp.where((diff >= 0) & (diff < WINDOW), 0.0, NEG)
    wbias4 = jnp.concatenate([wbias] * n_pair, axis=0)
    o_w = jnp.zeros((rows, 128), F32)
    for v in range(2):
        s = s_win[v] + wbias4
        m = jnp.max(s, axis=1, keepdims=True)
        p = jnp.exp2(s - m)
        o_w = o_w + _finish(jnp.dot(p.astype(BF16), _pad_v(vw, v), preferred_element_type=F32), v)
    out = out + gate_of(2) * o_w

    s_iota = lax.broadcasted_iota(jnp.int32, (n_sel, Q_BLOCK), 0)
    t_lane = t0 + lax.broadcasted_iota(jnp.int32, (n_sel, Q_BLOCK), 1)
    cur = t_lane // SEL_BLOCK
    forced = (s_iota == 0) | (s_iota == cur) | (s_iota == cur - 1)
    valid = s_iota * SEL_BLOCK <= t_lane
    score = jnp.where(forced, FORCE, jnp.where(valid, imp_t, -1.0))
    s_f = s_iota.astype(F32)
    sel_t = jnp.zeros((n_sel, Q_BLOCK), F32)
    for _ in range(min(SEL_TOPK, n_sel)):
        mx = jnp.max(score, axis=0, keepdims=True)
        idx = jnp.min(jnp.where(score == mx, s_f, float(n_sel)), axis=0, keepdims=True)
        hit = s_f == idx
        sel_t = jnp.where(hit, 1.0, sel_t)
        score = jnp.where(hit, -3e38, score)
    selmask = sel_t.T.astype(BF16)

    m_scr[...] = jnp.full(m_scr.shape, NEG, F32)
    acc_scr[...] = jnp.zeros(acc_scr.shape, F32)
    n_tiles = (t0 + Q_BLOCK + sel_tile - 1) // sel_tile

    last_tile = seq_len // sel_tile - 1

    def bias_of(kt):
        blk = kt * blocks_per_tile + lax.broadcasted_iota(jnp.int32, (n_sel, sel_tile), 1) // SEL_BLOCK
        expand = (lax.broadcasted_iota(jnp.int32, (n_sel, sel_tile), 0) == blk).astype(BF16)
        selexp = jnp.dot(selmask, expand, preferred_element_type=F32)
        kpos = kt * sel_tile + lax.broadcasted_iota(jnp.int32, (Q_BLOCK, sel_tile), 1)
        bias = jnp.where((selexp > 0.5) & (kpos <= t_row), 0.0, NEG)
        return jnp.concatenate([bias] * n_pair, axis=0)

    def scores_into(slot, kt):
        bias4 = bias_of(kt)
        kt_tile = kst_ref[0, 0, jnp.minimum(kt, last_tile)]
        for v in range(2):
            s_slots[slot][v] = jnp.dot(qst, _pad_kt(kt_tile, v), preferred_element_type=F32) + bias4

    def attend_from(slot, kt):
        v_tile = vs_ref[0, 0, jnp.minimum(kt, last_tile)]
        for v in range(2):
            for h in range(rows // slab):
                r = slice(h * slab, (h + 1) * slab)
                m_new, alpha, p = _softmax_tile(s_slots[slot][v, r, :], m_scr[v, r, :])
                m_scr[v, r, :] = m_new
                acc_scr[v, r, :] = alpha * acc_scr[v, r, :]
                p_scr[v, r, :] = p.astype(BF16)
            acc_scr[v] += jnp.dot(p_scr[v], _pad_v(v_tile, v), preferred_element_type=F32)

    bias_first = bias_of(0)
    for v in range(2):
        s_slots[0][v] = s_slots[0][v] + bias_first

    def sel_body(j, carry):
        kt = 2 * j
        scores_into(1, kt + 1)
        attend_from(0, kt)
        scores_into(0, kt + 2)
        attend_from(1, kt + 1)
        return carry

    lax.fori_loop(0, (n_tiles + 1) // 2, sel_body, 0)
    out = out + gate_of(1) * (_finish(acc_scr[0], 0) + _finish(acc_scr[1], 1))

    o_ref[0] = jnp.concatenate([out[p * Q_BLOCK:(p + 1) * Q_BLOCK] for p in range(n_pair)], axis=1)


def nsa_attention(q, gate_pad, kct, vc, kst, vs, kwt, vw):
    bsz, seq_len, _ = q.shape
    n_sel = seq_len // SEL_BLOCK
    n_cpad = seq_len // CMP_STRIDE
    n_cmp = (seq_len - CMP_BLOCK) // CMP_STRIDE + 1
    n_pair = Q_PER_KV // 2
    cs = np.arange(n_cpad) * CMP_STRIDE
    ce = cs + CMP_BLOCK - 1
    ss = np.arange(n_sel) * SEL_BLOCK
    se = ss + SEL_BLOCK - 1
    ovl = (cs[None, :] <= se[:, None]) & (ce[None, :] >= ss[:, None]) & (np.arange(n_cpad)[None, :] < n_cmp)
    ovl = jnp.asarray(ovl.astype(np.float32), BF16)
    gx = np.zeros((NSA_KV_HEADS, 128, N_BRANCH * n_pair * 128), np.float32)
    for k in range(NSA_KV_HEADS):
        for hl in range(Q_PER_KV):
            for br in range(N_BRANCH):
                c0 = br * n_pair * 128 + hl * HEAD_DIM
                gx[k, (k * Q_PER_KV + hl) * N_BRANCH + br, c0:c0 + HEAD_DIM] = 1.0
    gx = jnp.asarray(gx, BF16)
    width = Q_PER_KV * HEAD_DIM
    full = lambda *shape: pl.BlockSpec((1, 1) + shape, lambda b, k, i: (b, k) + (0,) * len(shape))
    return pl.pallas_call(
        functools.partial(_nsa_kernel, seq_len=seq_len),
        grid=(bsz, NSA_KV_HEADS, seq_len // Q_BLOCK),
        in_specs=[
            pl.BlockSpec((1, Q_BLOCK, width), lambda b, k, i: (b, i, k)),
            pl.BlockSpec((1, Q_BLOCK, 128), lambda b, k, i: (b, i, 0)),
            full(HEAD_DIM, n_cpad), full(n_cpad, 128),
            full(seq_len // PROJ_TILE, HEAD_DIM, PROJ_TILE), full(seq_len // PROJ_TILE, PROJ_TILE, 128),
            full(seq_len // Q_BLOCK, HEAD_DIM, Q_BLOCK), full(seq_len // Q_BLOCK, Q_BLOCK, 128),
            pl.BlockSpec((n_sel, n_cpad), lambda b, k, i: (0, 0)),
            pl.BlockSpec((1, 128, N_BRANCH * n_pair * 128), lambda b, k, i: (k, 0, 0)),
        ],
        out_specs=pl.BlockSpec((1, Q_BLOCK, width), lambda b, k, i: (b, i, k)),
        out_shape=jax.ShapeDtypeStruct((bsz, seq_len, NSA_WIDTH), F32),
        scratch_shapes=[pltpu.VMEM((2, n_pair * Q_BLOCK, 128), F32), pltpu.VMEM((2, n_pair * Q_BLOCK, 128), F32),
                        pltpu.VMEM((2, n_pair * Q_BLOCK, PROJ_TILE), F32),
                        pltpu.VMEM((2, n_pair * Q_BLOCK, PROJ_TILE), F32),
                        pltpu.VMEM((2, n_pair * Q_BLOCK, PROJ_TILE), BF16)],
        compiler_params=pltpu.CompilerParams(
            dimension_semantics=("arbitrary", "arbitrary", "arbitrary"), vmem_limit_bytes=V7X_VMEM_LIMIT_BYTES),
        name="nsa_attention",
    )(q, gate_pad, kct, vc, kst, vs, kwt, vw, ovl, gx)


ROUTER_TILE = 512
MOE_TILE = 1024
MOE_SUB = 256
MOE_ROWS = 48
MOE_SLOT = 64
MOE_GROUP = 4
MOE_GATHER_GROUP = 4


def _first_max_mask(x, idx_f, axis):
    mx = jnp.max(x, axis=axis, keepdims=True)
    first = jnp.min(jnp.where(x == mx, idx_f, 1e9), axis=axis, keepdims=True)
    return idx_f == first, mx


def _router_kernel(x_ref, wrt_ref, bias_ref, w_ref, sel_ref):
    per_group = N_EXPERTS // N_EXPERT_GROUPS
    tr = x_ref.shape[0]
    nt = (((1,), (1,)), ((), ()))
    logits = lax.dot_general(wrt_ref[...], x_ref[...].astype(BF16), nt, preferred_element_type=F32)
    aff = jax.nn.sigmoid(logits)
    biased = aff + bias_ref[...]
    grp = biased.reshape(N_EXPERT_GROUPS, per_group, tr)
    in_grp = lax.broadcasted_iota(jnp.int32, grp.shape, 1).astype(F32)
    hit1, m1 = _first_max_mask(grp, in_grp, 1)
    m2 = jnp.max(jnp.where(hit1, -jnp.inf, grp), axis=1, keepdims=True)
    gscore = (m1 + m2).reshape(N_EXPERT_GROUPS, tr)
    g_idx = lax.broadcasted_iota(jnp.int32, gscore.shape, 0).astype(F32)
    gsel = jnp.zeros(gscore.shape, F32)
    for _ in range(TOPK_GROUPS):
        hit, _ = _first_max_mask(gscore, g_idx, 0)
        gsel = jnp.where(hit, 1.0, gsel)
        gscore = jnp.where(hit, -jnp.inf, gscore)
    gmask = jnp.broadcast_to(gsel.reshape(N_EXPERT_GROUPS, 1, tr), grp.shape).reshape(N_EXPERTS, tr)
    cand = jnp.where(gmask > 0.5, biased, NEG)
    e_idx = lax.broadcasted_iota(jnp.int32, cand.shape, 0).astype(F32)
    sel = jnp.zeros(cand.shape, F32)
    for _ in range(TOP_K):
        hit, _ = _first_max_mask(cand, e_idx, 0)
        sel = jnp.where(hit, 1.0, sel)
        cand = jnp.where(hit, -jnp.inf, cand)
    w = jnp.where(sel > 0.5, aff, 0.0)
    w_ref[...] = w / jnp.sum(w, axis=0, keepdims=True) * ROUTED_SCALE
    sel_ref[...] = sel


def moe_router(xt, w_router, router_bias):
    n_tok = xt.shape[0]
    wrt = w_router.T.astype(BF16)
    return pl.pallas_call(
        _router_kernel,
        grid=(n_tok // ROUTER_TILE,),
        in_specs=[pl.BlockSpec((ROUTER_TILE, D_MODEL), lambda i: (i, 0)),
                  pl.BlockSpec((N_EXPERTS, D_MODEL), lambda i: (0, 0)),
                  pl.BlockSpec((N_EXPERTS, 1), lambda i: (0, 0))],
        out_specs=[pl.BlockSpec((N_EXPERTS, ROUTER_TILE), lambda i: (0, i)),
                   pl.BlockSpec((N_EXPERTS, ROUTER_TILE), lambda i: (0, i))],
        out_shape=[jax.ShapeDtypeStruct((N_EXPERTS, n_tok), F32), jax.ShapeDtypeStruct((N_EXPERTS, n_tok), F32)],
        compiler_params=pltpu.CompilerParams(dimension_semantics=("arbitrary",),
                                             vmem_limit_bytes=V7X_VMEM_LIMIT_BYTES),
        name="moe_router",
    )(xt, wrt, router_bias.reshape(N_EXPERTS, 1))


def _moe_kernel(cnt_ref, x_ref, sel_ref, w_ref, init_ref, wg_ref, wu_ref, wd_ref, lng_ref, lnb_ref, o_ref,
                rank_scr, ybuf_scr, sbuf_scr, xe_scr):
    i = pl.program_id(0)
    e = pl.program_id(1)
    tm = x_ref.shape[0]
    n_sub = tm // MOE_SUB
    tn = (((0,), (0,)), ((), ()))

    @pl.when(e == 0)
    def _():
        o_ref[...] = init_ref[...]
        before = (lax.broadcasted_iota(jnp.int32, (MOE_SUB, MOE_SUB), 0)
                  < lax.broadcasted_iota(jnp.int32, (MOE_SUB, MOE_SUB), 1))
        before = jnp.where(before, 1.0, 0.0).astype(BF16)
        for q in range(n_sub):
            cols = slice(q * MOE_SUB, (q + 1) * MOE_SUB)
            rank_scr[:, cols] = jnp.dot(sel_ref[:, cols].astype(BF16), before, preferred_element_type=F32)

    count = cnt_ref[i * N_EXPERTS + e]
    sel_e = sel_ref[pl.ds(e, 1), :]
    rank_e = rank_scr[pl.ds(e, 1), :]
    w_e = w_ref[pl.ds(e, 1), :]

    def one_hots(rank_row, sel_row, c):
        row = (c * MOE_ROWS + lax.broadcasted_iota(jnp.int32, (MOE_ROWS, MOE_SUB), 0)).astype(F32)
        hits = []
        for q in range(n_sub):
            cols = slice(q * MOE_SUB, (q + 1) * MOE_SUB)
            hits.append((rank_row[:, cols] == row) & (sel_row[:, cols] > 0.5))
        return hits

    def swiglu(xe):
        g = jnp.dot(xe, wg_ref[0], preferred_element_type=F32)
        u = jnp.dot(xe, wu_ref[0], preferred_element_type=F32)
        h = (jax.nn.silu(g) * u).astype(BF16)
        return jnp.dot(h, wd_ref[0], preferred_element_type=F32).astype(BF16)

    def weighted(hits):
        return [jnp.where(hits[q], w_e[:, q * MOE_SUB:(q + 1) * MOE_SUB], 0.0).astype(BF16) for q in range(n_sub)]

    slot = e % MOE_GROUP

    gslot = e % MOE_GATHER_GROUP

    @pl.when(gslot == 0)
    def _():
        stacks = [[] for _ in range(n_sub)]
        for gi in range(MOE_GATHER_GROUP):
            hits = one_hots(rank_scr[pl.ds(e + gi, 1), :], sel_ref[pl.ds(e + gi, 1), :], 0)
            for q in range(n_sub):
                stacks[q].append(jnp.where(hits[q], 1.0, 0.0).astype(BF16))
        for q in range(n_sub):
            cols = slice(q * MOE_SUB, (q + 1) * MOE_SUB)
            xg = jnp.dot(jnp.concatenate(stacks[q], axis=0), x_ref[cols, :],
                         preferred_element_type=F32).astype(BF16)
            for gi in range(MOE_GATHER_GROUP):
                xe_scr[gi, q] = xg[gi * MOE_ROWS:(gi + 1) * MOE_ROWS]

    scatters = weighted(one_hots(rank_e, sel_e, 0))
    y = swiglu(xe_scr[gslot].reshape(n_sub * MOE_ROWS, D_MODEL))
    spare = MOE_SLOT - MOE_ROWS
    for q in range(n_sub):
        sbuf_scr[q, slot] = jnp.concatenate([scatters[q], jnp.zeros((spare, MOE_SUB), BF16)], axis=0)
        ybuf_scr[q, slot] = jnp.concatenate(
            [y[q * MOE_ROWS:(q + 1) * MOE_ROWS], jnp.zeros((spare, D_MODEL), BF16)], axis=0)

    @pl.when(slot == MOE_GROUP - 1)
    def _():
        for q in range(n_sub):
            cols = slice(q * MOE_SUB, (q + 1) * MOE_SUB)
            o_ref[cols, :] += lax.dot_general(sbuf_scr[q].reshape(MOE_GROUP * MOE_SLOT, MOE_SUB),
                                              ybuf_scr[q].reshape(MOE_GROUP * MOE_SLOT, D_MODEL), tn,
                                              preferred_element_type=F32)

    def overflow_body(c, carry):
        hits = one_hots(rank_e, sel_e, c)
        sc = weighted(hits)
        xe = jnp.concatenate(
            [jnp.dot(jnp.where(hits[q], 1.0, 0.0).astype(BF16), x_ref[q * MOE_SUB:(q + 1) * MOE_SUB, :],
                     preferred_element_type=F32).astype(BF16) for q in range(n_sub)], axis=0)
        yy = swiglu(xe)
        for q in range(n_sub):
            cols = slice(q * MOE_SUB, (q + 1) * MOE_SUB)
            o_ref[cols, :] += lax.dot_general(sc[q], yy[q * MOE_ROWS:(q + 1) * MOE_ROWS], tn,
                                              preferred_element_type=F32)
        return carry

    lax.fori_loop(1, (count + MOE_ROWS - 1) // MOE_ROWS, overflow_body, 0)

    @pl.when(e == N_EXPERTS - 1)
    def _():
        o_ref[...] = _layer_norm(o_ref[...], lng_ref[...], lnb_ref[...])


def moe_routed(x_bf16, sel_t, w_t, init, w_gate, w_up, w_down, ln_g, ln_b):
    n_tok = x_bf16.shape[0]
    n_tiles = n_tok // MOE_TILE
    per_sub = jnp.sum(sel_t.reshape(N_EXPERTS, n_tiles, MOE_TILE // MOE_SUB, MOE_SUB), axis=-1)
    cnt = jnp.max(per_sub, axis=-1).T.astype(jnp.int32).reshape(-1)
    grid_spec = pltpu.PrefetchScalarGridSpec(
        num_scalar_prefetch=1,
        grid=(n_tiles, N_EXPERTS),
        in_specs=[
            pl.BlockSpec((MOE_TILE, D_MODEL), lambda i, e, cnt: (i, 0), pipeline_mode=pl.Buffered(1)),
            pl.BlockSpec((N_EXPERTS, MOE_TILE), lambda i, e, cnt: (0, i)),
            pl.BlockSpec((N_EXPERTS, MOE_TILE), lambda i, e, cnt: (0, i)),
            pl.BlockSpec((MOE_TILE, D_MODEL), lambda i, e, cnt: (i, 0), pipeline_mode=pl.Buffered(1)),
            pl.BlockSpec((1, D_MODEL, EXPERT_FF), lambda i, e, cnt: (e, 0, 0)),
            pl.BlockSpec((1, D_MODEL, EXPERT_FF), lambda i, e, cnt: (e, 0, 0)),
            pl.BlockSpec((1, EXPERT_FF, D_MODEL), lambda i, e, cnt: (e, 0, 0)),
            pl.BlockSpec((1, D_MODEL), lambda i, e, cnt: (0, 0)),
            pl.BlockSpec((1, D_MODEL), lambda i, e, cnt: (0, 0)),
        ],
        out_specs=pl.BlockSpec((MOE_TILE, D_MODEL), lambda i, e, cnt: (i, 0)),
        scratch_shapes=[pltpu.VMEM((N_EXPERTS, MOE_TILE), F32),
                        pltpu.VMEM((MOE_TILE // MOE_SUB, MOE_GROUP, MOE_SLOT, D_MODEL), BF16),
                        pltpu.VMEM((MOE_TILE // MOE_SUB, MOE_GROUP, MOE_SLOT, MOE_SUB), BF16),
                        pltpu.VMEM((MOE_GATHER_GROUP, MOE_TILE // MOE_SUB, MOE_ROWS, D_MODEL), BF16)],
    )
    return pl.pallas_call(
        _moe_kernel,
        grid_spec=grid_spec,
        out_shape=jax.ShapeDtypeStruct((n_tok, D_MODEL), F32),
        compiler_params=pltpu.CompilerParams(dimension_semantics=("arbitrary", "arbitrary"),
                                             vmem_limit_bytes=V7X_VMEM_LIMIT_BYTES),
        name="moe_routed",
    )(cnt, x_bf16, sel_t, w_t, init, w_gate, w_up, w_down, ln_g.reshape(1, D_MODEL), ln_b.reshape(1, D_MODEL))


def _shared_ffn_kernel(x_ref, wg_ref, wu_ref, wd_ref, o_ref, xb_ref):
    x = x_ref[...]
    xb = x.astype(BF16)
    h = jax.nn.silu(jnp.dot(xb, wg_ref[...], preferred_element_type=F32)) * jnp.dot(
        xb, wu_ref[...], preferred_element_type=F32)
    o_ref[...] = DEEPNORM_ALPHA * x + jnp.dot(h.astype(BF16), wd_ref[...], preferred_element_type=F32)
    xb_ref[...] = xb


def shared_ffn(xt, wg, wu, wd, tm=1024):
    n_tok, d = xt.shape
    ff = wg.shape[1]
    once = pl.Buffered(1)
    return pl.pallas_call(
        _shared_ffn_kernel,
        grid=(n_tok // tm,),
        in_specs=[pl.BlockSpec((tm, d), lambda i: (i, 0)),
                  pl.BlockSpec((d, ff), lambda i: (0, 0), pipeline_mode=once),
                  pl.BlockSpec((d, ff), lambda i: (0, 0), pipeline_mode=once),
                  pl.BlockSpec((ff, d), lambda i: (0, 0), pipeline_mode=once)],
        out_specs=[pl.BlockSpec((tm, d), lambda i: (i, 0)), pl.BlockSpec((tm, d), lambda i: (i, 0))],
        out_shape=[jax.ShapeDtypeStruct((n_tok, d), F32), jax.ShapeDtypeStruct((n_tok, d), BF16)],
        compiler_params=pltpu.CompilerParams(dimension_semantics=("arbitrary",),
                                             vmem_limit_bytes=V7X_VMEM_LIMIT_BYTES),
        name="shared_ffn",
    )(xt, wg.astype(BF16), wu.astype(BF16), wd.astype(BF16))


def hybrid_layer(x, positions, w_in, lam_re, lam_im, log_dt, ssm_b_re, ssm_b_im, ssm_c_re, ssm_c_im, ssm_d,
                 w_glu, cmp_pos_k, cmp_pos_v, w_cmp_k1, w_cmp_k2, w_cmp_v1, w_cmp_v2, w_out, ln1_g, ln1_b,
                 w_router, router_bias, w_gate, w_up, w_down, ws_gate, ws_up, ws_down, ln2_g, ln2_b):
    bsz, L, _ = x.shape
    sizes = [SSM_WIDTH, NSA_WIDTH] + [KV_WIDTH] * 6 + [NSA_HEADS * N_BRANCH]
    o = [0] + [int(v) for v in np.cumsum(sizes)]
    col = lambda j: w_in[:, o[j]:o[j + 1]]
    dup = lambda w: jnp.concatenate([w[:, h * HEAD_DIM:(h + 1) * HEAD_DIM] for h in (0, 0, 1, 1)], axis=1)
    gate_cols = jnp.pad(col(8), ((0, 0), (0, 128 - NSA_HEADS * N_BRANCH)))
    w_uq = w_in[:, :o[2]].astype(BF16)
    w_kv = jnp.concatenate([col(4), col(6), dup(col(5)), dup(col(7)), col(2), col(3), gate_cols], axis=1).astype(BF16)

    xt = x.reshape(bsz * L, D_MODEL)
    u, q = proj_uq(xt, w_uq, positions.reshape(bsz * L, 1))
    kst, kwt, vs, vw, kc_raw, vc_raw, gate_pad = proj_kv(x, w_kv, positions.reshape(bsz, L, 1))
    kct, vcd = compress_kv(kc_raw, vc_raw, positions, cmp_pos_k, cmp_pos_v, w_cmp_k1, w_cmp_k2, w_cmp_v1, w_cmp_v2)
    y_s5 = s5_scan(u.reshape(bsz, L, SSM_WIDTH), lam_re, lam_im, log_dt, ssm_b_re, ssm_b_im, ssm_c_re, ssm_c_im, ssm_d)
    vw = vw.reshape(bsz, NSA_KV_HEADS, L // Q_BLOCK, Q_BLOCK, 128)
    y_nsa = nsa_attention(q.reshape(bsz, L, NSA_WIDTH), gate_pad, kct, vcd, kst, vs, kwt, vw)
    x1 = out_proj_ln(y_s5.reshape(bsz * L, SSM_WIDTH), y_nsa.reshape(bsz * L, NSA_WIDTH), xt, w_glu, w_out,
                     ln1_g, ln1_b)
    w_t, sel_t = moe_router(x1, w_router, router_bias)
    acc0, x1b = shared_ffn(x1, ws_gate, ws_up, ws_down)
    out = moe_routed(x1b, sel_t, w_t, acc0, w_gate.astype(BF16), w_up.astype(BF16), w_down.astype(BF16),
                     ln2_g, ln2_b)
    return out.reshape(bsz, L, D_MODEL)


def kernel(x, positions, w_in, lam_re, lam_im, log_dt, ssm_b_re, ssm_b_im, ssm_c_re, ssm_c_im, ssm_d, w_glu, cmp_pos_k, cmp_pos_v, w_cmp_k1, w_cmp_k2, w_cmp_v1, w_cmp_v2, w_out, ln1_g, ln1_b, w_router, router_bias, w_gate, w_up, w_down, ws_gate, ws_up, ws_down, ln2_g, ln2_b):
    params = (w_in, lam_re, lam_im, log_dt, ssm_b_re, ssm_b_im, ssm_c_re, ssm_c_im, ssm_d,
              w_glu, cmp_pos_k, cmp_pos_v, w_cmp_k1, w_cmp_k2, w_cmp_v1, w_cmp_v2, w_out, ln1_g, ln1_b,
              w_router, router_bias, w_gate, w_up, w_down, ws_gate, ws_up, ws_down, ln2_g, ln2_b)
    return hybrid_layer(x, positions, *(p[0] for p in params))
```

```python
import functools
import math

import numpy as np
import jax
import jax.numpy as jnp
from jax import lax
from jax.experimental import pallas as pl
from jax.experimental.pallas import tpu as pltpu

D_MODEL = 2048
SSM_WIDTH = 1024
SSM_CH_PER_GROUP = 16
SSM_GROUPS = 64
SSM_STATE = 64
NSA_HEADS = 16
NSA_KV_HEADS = 2
HEAD_DIM = 64
Q_PER_KV = NSA_HEADS // NSA_KV_HEADS
NSA_WIDTH = NSA_HEADS * HEAD_DIM
KV_WIDTH = NSA_KV_HEADS * HEAD_DIM
N_BRANCH = 3
CMP_BLOCK = 32
CMP_STRIDE = 16
SEL_BLOCK = 64
SEL_TOPK = 16
WINDOW = 512
Q_BLOCK = 128
ROPE_THETA = 10000.0
N_EXPERTS = 64
TOP_K = 8
N_EXPERT_GROUPS = 8
TOPK_GROUPS = 4
ROUTED_SCALE = 2.5
EXPERT_FF = 512
DEPTH = 1
DEEPNORM_ALPHA = (2.0 * DEPTH) ** 0.25
LN_EPS = 1e-5
NEG = -1e30
FORCE = 1e4
F32 = jnp.float32
BF16 = jnp.bfloat16

V7X_VMEM_LIMIT_BYTES = 56 * 1024 * 1024


def _layer_norm(x, g, b):
    mu = jnp.mean(x, -1, keepdims=True)
    var = jnp.mean(jnp.square(x - mu), -1, keepdims=True)
    return (x - mu) * lax.rsqrt(var + LN_EPS) * g + b


def _rope_tables(pos_col, inv_row):
    ang = pos_col * inv_row
    return jnp.cos(ang), jnp.sin(ang)


def _rope_lanes(x, cos, sin):
    lane = lax.broadcasted_iota(jnp.int32, (x.shape[0], 128), 1)
    first_half = (lane % HEAD_DIM) < HEAD_DIM // 2
    outs = []
    for blk in range(x.shape[1] // 128):
        xb = x[:, blk * 128:(blk + 1) * 128]
        rot = jnp.where(first_half, -pltpu.roll(xb, 128 - HEAD_DIM // 2, 1), pltpu.roll(xb, HEAD_DIM // 2, 1))
        outs.append(xb * cos + rot * sin)
    return outs[0] if len(outs) == 1 else jnp.concatenate(outs, axis=1)


def _inv_freq_row():
    half = HEAD_DIM // 2
    inv = ROPE_THETA ** (-jnp.arange(half, dtype=F32) / half)
    return jnp.tile(inv, 128 // half).reshape(1, 128)


PROJ_TILE = 512
Q_SCALE = HEAD_DIM ** -0.5 * math.log2(math.e)


def _proj_uq_kernel(x_ref, w_ref, pos_ref, inv_ref, u_ref, q_ref):
    acc = jnp.dot(x_ref[...].astype(BF16), w_ref[...], preferred_element_type=F32)
    u_ref[...] = acc[:, :SSM_WIDTH]
    cos, sin = _rope_tables(pos_ref[...].astype(F32), inv_ref[...])
    q_ref[...] = (_rope_lanes(acc[:, SSM_WIDTH:], cos, sin) * Q_SCALE).astype(BF16)


def proj_uq(xt, w_uq, pos_col, tm=1024):
    n_tok = xt.shape[0]
    return pl.pallas_call(
        _proj_uq_kernel,
        grid=(n_tok // tm,),
        in_specs=[pl.BlockSpec((tm, D_MODEL), lambda i: (i, 0)),
                  pl.BlockSpec((D_MODEL, SSM_WIDTH + NSA_WIDTH), lambda i: (0, 0), pipeline_mode=pl.Buffered(1)),
                  pl.BlockSpec((tm, 1), lambda i: (i, 0)),
                  pl.BlockSpec((1, 128), lambda i: (0, 0))],
        out_specs=[pl.BlockSpec((tm, SSM_WIDTH), lambda i: (i, 0)),
                   pl.BlockSpec((tm, NSA_WIDTH), lambda i: (i, 0))],
        out_shape=[jax.ShapeDtypeStruct((n_tok, SSM_WIDTH), F32), jax.ShapeDtypeStruct((n_tok, NSA_WIDTH), BF16)],
        compiler_params=pltpu.CompilerParams(dimension_semantics=("arbitrary",),
                                             vmem_limit_bytes=V7X_VMEM_LIMIT_BYTES),
        name="proj_uq",
    )(xt, w_uq, pos_col, _inv_freq_row())


KV_COLS = 4 * KV_WIDTH + 2 * 2 * KV_WIDTH + 128


def _proj_kv_kernel(x_ref, w_ref, pos_ref, inv_ref, kst_ref, kwt_ref, vs_ref, vw_ref, kc_ref, vc_ref, g_ref):
    acc = jnp.dot(x_ref[0].astype(BF16), w_ref[...], preferred_element_type=F32)
    cos, sin = _rope_tables(pos_ref[0].astype(F32), inv_ref[...])
    ks_t = _rope_lanes(acc[:, 0:128], cos, sin).T
    kw_t = _rope_lanes(acc[:, 128:256], cos, sin).T
    for k in range(NSA_KV_HEADS):
        kst_ref[0, k, 0] = ks_t[k * HEAD_DIM:(k + 1) * HEAD_DIM].astype(BF16)
        for j in range(PROJ_TILE // Q_BLOCK):
            kwt_ref[0, k, j] = kw_t[k * HEAD_DIM:(k + 1) * HEAD_DIM, j * Q_BLOCK:(j + 1) * Q_BLOCK].astype(BF16)
        vs_ref[0, k, 0] = acc[:, 256 + k * 128: 256 + (k + 1) * 128].astype(BF16)
        vw_ref[0, k] = acc[:, 512 + k * 128: 512 + (k + 1) * 128].astype(BF16)
    kc_ref[0] = acc[:, 768:896]
    vc_ref[0] = acc[:, 896:1024]
    g_ref[0] = acc[:, 1024:1152]


def proj_kv(x, w_kv, pos_col3):
    bsz, seq_len, _ = x.shape
    n_t = seq_len // PROJ_TILE
    per = PROJ_TILE // Q_BLOCK
    return pl.pallas_call(
        _proj_kv_kernel,
        grid=(bsz, n_t),
        in_specs=[pl.BlockSpec((1, PROJ_TILE, D_MODEL), lambda b, i: (b, i, 0)),
                  pl.BlockSpec((D_MODEL, KV_COLS), lambda b, i: (0, 0)),
                  pl.BlockSpec((1, PROJ_TILE, 1), lambda b, i: (b, i, 0)),
                  pl.BlockSpec((1, 128), lambda b, i: (0, 0))],
        out_specs=[
            pl.BlockSpec((1, NSA_KV_HEADS, 1, HEAD_DIM, PROJ_TILE), lambda b, i: (b, 0, i, 0, 0)),
            pl.BlockSpec((1, NSA_KV_HEADS, per, HEAD_DIM, Q_BLOCK), lambda b, i: (b, 0, i, 0, 0)),
            pl.BlockSpec((1, NSA_KV_HEADS, 1, PROJ_TILE, 128), lambda b, i: (b, 0, i, 0, 0)),
            pl.BlockSpec((1, NSA_KV_HEADS, PROJ_TILE, 128), lambda b, i: (b, 0, i, 0)),
            pl.BlockSpec((1, PROJ_TILE, 128), lambda b, i: (b, i, 0)),
            pl.BlockSpec((1, PROJ_TILE, 128), lambda b, i: (b, i, 0)),
            pl.BlockSpec((1, PROJ_TILE, 128), lambda b, i: (b, i, 0)),
        ],
        out_shape=[
            jax.ShapeDtypeStruct((bsz, NSA_KV_HEADS, n_t, HEAD_DIM, PROJ_TILE), BF16),
            jax.ShapeDtypeStruct((bsz, NSA_KV_HEADS, seq_len // Q_BLOCK, HEAD_DIM, Q_BLOCK), BF16),
            jax.ShapeDtypeStruct((bsz, NSA_KV_HEADS, n_t, PROJ_TILE, 128), BF16),
            jax.ShapeDtypeStruct((bsz, NSA_KV_HEADS, seq_len, 128), BF16),
            jax.ShapeDtypeStruct((bsz, seq_len, 128), F32),
            jax.ShapeDtypeStruct((bsz, seq_len, 128), F32),
            jax.ShapeDtypeStruct((bsz, seq_len, 128), F32),
        ],
        compiler_params=pltpu.CompilerParams(dimension_semantics=("arbitrary", "arbitrary"),
                                             vmem_limit_bytes=V7X_VMEM_LIMIT_BYTES),
        name="proj_kv",
    )(x, w_kv, pos_col3, _inv_freq_row())


def _compress_kernel(ck_ref, cv_ref, pek_ref, pev_ref, w1k_ref, w1v_ref, w2k_ref, w2v_ref, pos_ref, inv_ref,
                     kct_ref, vcd_ref):
    def hidden(c_ref, pe_ref, w1_ref):
        c = c_ref[0]
        lo = jnp.dot((c + pe_ref[0]).astype(BF16), w1_ref[0], preferred_element_type=F32)
        hi = jnp.dot((c + pe_ref[1]).astype(BF16), w1_ref[1], preferred_element_type=F32)
        hi_next = jnp.concatenate([hi[1:], jnp.zeros((1, hi.shape[1]), F32)], axis=0)
        return jax.nn.gelu(lo + hi_next).astype(BF16)

    kc = jnp.dot(hidden(ck_ref, pek_ref, w1k_ref), w2k_ref[...], preferred_element_type=F32)
    cos, sin = _rope_tables(pos_ref[0], inv_ref[...])
    kc_t = _rope_lanes(kc, cos, sin).T
    vc = jnp.dot(hidden(cv_ref, pev_ref, w1v_ref), w2v_ref[...], preferred_element_type=F32)
    for k in range(NSA_KV_HEADS):
        kct_ref[0, k] = kc_t[k * HEAD_DIM:(k + 1) * HEAD_DIM].astype(BF16)
        vcd_ref[0, k] = vc[:, k * 128:(k + 1) * 128].astype(BF16)


def compress_kv(kc_raw, vc_raw, positions, cmp_pos_k, cmp_pos_v, w_k1, w_k2, w_v1, w_v2):
    bsz, seq_len, _ = kc_raw.shape
    n_chunk = seq_len // CMP_STRIDE
    width = CMP_STRIDE * 128
    eye = jnp.eye(NSA_KV_HEADS, dtype=F32)

    def chunk_pe(pe):
        pe = pe.reshape(2, CMP_STRIDE, 1, HEAD_DIM)
        return jnp.broadcast_to(pe, (2, CMP_STRIDE, NSA_KV_HEADS, HEAD_DIM)).reshape(2, 1, width)

    def chunk_w1(w1):
        hid = w1.shape[1]
        w = w1.reshape(2, CMP_STRIDE, HEAD_DIM, hid)
        return jnp.einsum('htdj,kc->htkdcj', w, eye).reshape(2, width, NSA_KV_HEADS * hid).astype(BF16)

    hid = w_k2.shape[0]
    w2k = jnp.einsum('jd,kc->kjcd', w_k2, eye).reshape(NSA_KV_HEADS * hid, NSA_KV_HEADS * HEAD_DIM).astype(BF16)
    w2v = jnp.einsum('jd,kc,r->kjcrd', w_v2, eye, jnp.ones((2,), F32)).reshape(
        NSA_KV_HEADS * hid, NSA_KV_HEADS * 128).astype(BF16)
    pos = positions.astype(F32).reshape(bsz, n_chunk, CMP_STRIDE).sum(-1)
    pos_next = jnp.concatenate([pos[:, 1:], pos[:, -1:]], axis=1)
    cmp_pos = ((pos + pos_next) / CMP_BLOCK).reshape(bsz, n_chunk, 1)
    return pl.pallas_call(
        _compress_kernel,
        grid=(bsz,),
        in_specs=[pl.BlockSpec((1, n_chunk, width), lambda b: (b, 0, 0)),
                  pl.BlockSpec((1, n_chunk, width), lambda b: (b, 0, 0)),
                  pl.BlockSpec((2, 1, width), lambda b: (0, 0, 0)),
                  pl.BlockSpec((2, 1, width), lambda b: (0, 0, 0)),
                  pl.BlockSpec((2, width, NSA_KV_HEADS * hid), lambda b: (0, 0, 0)),
                  pl.BlockSpec((2, width, NSA_KV_HEADS * hid), lambda b: (0, 0, 0)),
                  pl.BlockSpec((NSA_KV_HEADS * hid, NSA_KV_HEADS * HEAD_DIM), lambda b: (0, 0)),
                  pl.BlockSpec((NSA_KV_HEADS * hid, NSA_KV_HEADS * 128), lambda b: (0, 0)),
                  pl.BlockSpec((1, n_chunk, 1), lambda b: (b, 0, 0)),
                  pl.BlockSpec((1, 128), lambda b: (0, 0))],
        out_specs=[pl.BlockSpec((1, NSA_KV_HEADS, HEAD_DIM, n_chunk), lambda b: (b, 0, 0, 0)),
                   pl.BlockSpec((1, NSA_KV_HEADS, n_chunk, 128), lambda b: (b, 0, 0, 0))],
        out_shape=[jax.ShapeDtypeStruct((bsz, NSA_KV_HEADS, HEAD_DIM, n_chunk), BF16),
                   jax.ShapeDtypeStruct((bsz, NSA_KV_HEADS, n_chunk, 128), BF16)],
        compiler_params=pltpu.CompilerParams(dimension_semantics=("arbitrary",),
                                             vmem_limit_bytes=V7X_VMEM_LIMIT_BYTES),
        name="compress_kv",
    )(kc_raw.reshape(bsz, n_chunk, width), vc_raw.reshape(bsz, n_chunk, width), chunk_pe(cmp_pos_k),
      chunk_pe(cmp_pos_v), chunk_w1(w_k1), chunk_w1(w_v1), w2k, w2v, cmp_pos, _inv_freq_row())


def _out_ln_kernel(y_ref, a_ref, x_ref, wglu_ref, wout_ref, g_ref, b_ref, o_ref):
    y = y_ref[...]
    y_ssm = y * jax.nn.sigmoid(jnp.dot(y.astype(BF16), wglu_ref[...], preferred_element_type=F32))
    mix = (jnp.dot(y_ssm.astype(BF16), wout_ref[:SSM_WIDTH, :], preferred_element_type=F32)
           + jnp.dot(a_ref[...].astype(BF16), wout_ref[SSM_WIDTH:, :], preferred_element_type=F32))
    o_ref[...] = _layer_norm(DEEPNORM_ALPHA * x_ref[...] + mix, g_ref[...], b_ref[...])


def out_proj_ln(y_s5, y_nsa, xt, w_glu, w_out, ln_g, ln_b, tm=512):
    n_tok = xt.shape[0]
    row = lambda i: (i, 0)
    const = lambda i: (0, 0)
    return pl.pallas_call(
        _out_ln_kernel,
        grid=(n_tok // tm,),
        in_specs=[pl.BlockSpec((tm, SSM_WIDTH), row), pl.BlockSpec((tm, NSA_WIDTH), row),
                  pl.BlockSpec((tm, D_MODEL), row),
                  pl.BlockSpec((SSM_WIDTH, SSM_WIDTH), const, pipeline_mode=pl.Buffered(1)),
                  pl.BlockSpec((D_MODEL, D_MODEL), const, pipeline_mode=pl.Buffered(1)),
                  pl.BlockSpec((1, D_MODEL), const),
                  pl.BlockSpec((1, D_MODEL), const)],
        out_specs=pl.BlockSpec((tm, D_MODEL), row),
        out_shape=jax.ShapeDtypeStruct((n_tok, D_MODEL), F32),
        compiler_params=pltpu.CompilerParams(dimension_semantics=("arbitrary",),
                                             vmem_limit_bytes=V7X_VMEM_LIMIT_BYTES),
        name="out_proj_ln",
    )(y_s5, y_nsa, xt, w_glu.astype(BF16), w_out.astype(BF16), ln_g.reshape(1, D_MODEL), ln_b.reshape(1, D_MODEL))


S5_CHUNK = 512
S5_SUB = S5_CHUNK // 8
S5_GROUPS_PER_BLOCK = 8
S5_STATES = S5_GROUPS_PER_BLOCK * SSM_STATE
S5_STREAMS = 2


def _cmul_add(ar, ai, xr, xi, br, bi):
    return ar * xr - ai * xi + br, ar * xi + ai * xr + bi


def _s5_kernel(u_ref, lam_ref, bmat_ref, cmat_ref, d_ref, perm_ref, permt_ref, o_ref,
               xr_scr, xi_scr, pr_scr, pi_scr, carry_scr, a_scr, bbar_scr):
    c = pl.program_id(2)
    streams = range(S5_STREAMS)

    @pl.when(c == 0)
    def _():
        powers = []
        for s in streams:
            lr, li = lam_ref[s, 0:1, :], lam_ref[s, 1:2, :]
            dt = jnp.exp(lam_ref[s, 2:3, :])
            mag = jnp.exp(lr * dt)
            ar, ai = mag * jnp.cos(li * dt), mag * jnp.sin(li * dt)
            zr, zi = ar - 1.0, ai
            den = lr * lr + li * li
            fr, fi = (zr * lr + zi * li) / den, (zi * lr - zr * li) / den
            a_scr[s, 0:1, :] = ar
            a_scr[s, 1:2, :] = ai
            b_re, b_im = bmat_ref[s, 0], bmat_ref[s, 1]
            bbar_scr[s, 0] = (fr * b_re - fi * b_im).astype(BF16)
            bbar_scr[s, 1] = (fr * b_im + fi * b_re).astype(BF16)
            powers += [jnp.broadcast_to(ar, (8, S5_STATES)), jnp.broadcast_to(ai, (8, S5_STATES))]
        carry_scr[...] = jnp.zeros(carry_scr.shape, F32)
        base = tuple(powers)

        def pw_body(i, pw):
            nxt = []
            for s in streams:
                pr, pi = pw[2 * s], pw[2 * s + 1]
                pr_scr[s, i] = pr
                pi_scr[s, i] = pi
                nxt += [base[2 * s] * pr - base[2 * s + 1] * pi, base[2 * s] * pi + base[2 * s + 1] * pr]
            return tuple(nxt)

        lax.fori_loop(0, S5_SUB, pw_body, base)

    a_re = [jnp.broadcast_to(a_scr[s, 0:1, :], (8, S5_STATES)) for s in streams]
    a_im = [jnp.broadcast_to(a_scr[s, 1:2, :], (8, S5_STATES)) for s in streams]
    perm = perm_ref[...]
    u = [u_ref[0, :, s * 128:(s + 1) * 128] for s in streams]
    for s in streams:
        u_p = jnp.dot(perm, u[s].astype(BF16), preferred_element_type=F32).astype(BF16)
        xr_scr[s] = jnp.dot(u_p, bbar_scr[s, 0], preferred_element_type=F32)
        xi_scr[s] = jnp.dot(u_p, bbar_scr[s, 1], preferred_element_type=F32)

    def scan_body(i, x):
        row = pl.multiple_of(i * 8, 8)
        out = []
        for s in streams:
            xr, xi = _cmul_add(a_re[s], a_im[s], x[2 * s], x[2 * s + 1],
                               xr_scr[s, pl.ds(row, 8), :], xi_scr[s, pl.ds(row, 8), :])
            xr_scr[s, pl.ds(row, 8), :] = xr
            xi_scr[s, pl.ds(row, 8), :] = xi
            out += [xr, xi]
        return tuple(out)

    zero = jnp.zeros((8, S5_STATES), F32)
    ends = lax.fori_loop(0, S5_SUB, scan_body, (zero,) * (2 * S5_STREAMS), unroll=4)

    cr, ci = [], []
    for s in streams:
        er, ei = ends[2 * s], ends[2 * s + 1]
        ar_s = pr_scr[s, S5_SUB - 1][0:1]
        ai_s = pi_scr[s, S5_SUB - 1][0:1]
        rows_r = [carry_scr[s, 0:1, :]]
        rows_i = [carry_scr[s, 1:2, :]]
        for j in range(8):
            nr, ni = _cmul_add(ar_s, ai_s, rows_r[-1], rows_i[-1], er[j:j + 1], ei[j:j + 1])
            rows_r.append(nr)
            rows_i.append(ni)
        carry_scr[s, 0:1, :] = rows_r[8]
        carry_scr[s, 1:2, :] = rows_i[8]
        cr.append(jnp.concatenate(rows_r[:8], axis=0))
        ci.append(jnp.concatenate(rows_i[:8], axis=0))

    def fix_body(i, carry):
        row = pl.multiple_of(i * 8, 8)
        for s in streams:
            xr, xi = _cmul_add(pr_scr[s, i], pi_scr[s, i], cr[s], ci[s],
                               xr_scr[s, pl.ds(row, 8), :], xi_scr[s, pl.ds(row, 8), :])
            xr_scr[s, pl.ds(row, 8), :] = xr
            xi_scr[s, pl.ds(row, 8), :] = xi
        return carry

    lax.fori_loop(0, S5_SUB, fix_body, 0, unroll=4)

    perm_t = permt_ref[...]
    for s in streams:
        y_p = (jnp.dot(xr_scr[s].astype(BF16), cmat_ref[s, 0], preferred_element_type=F32)
               - jnp.dot(xi_scr[s].astype(BF16), cmat_ref[s, 1], preferred_element_type=F32))
        y_hi = y_p.astype(BF16)
        y_lo = (y_p - y_hi.astype(F32)).astype(BF16)
        y = jnp.dot(perm_t, y_hi, preferred_element_type=F32) + jnp.dot(perm_t, y_lo, preferred_element_type=F32)
        o_ref[0, :, s * 128:(s + 1) * 128] = jax.nn.gelu(y + d_ref[s] * u[s])


def s5_scan(u, lam_re, lam_im, log_dt, b_re, b_im, c_re, c_im, d_skip):
    bsz, seq_len, _ = u.shape
    nb = SSM_GROUPS // S5_GROUPS_PER_BLOCK
    g = S5_STREAMS
    eye = jnp.eye(S5_GROUPS_PER_BLOCK, dtype=F32)

    def blockdiag_b(m):
        m = jnp.swapaxes(m, 1, 2).reshape(nb, S5_GROUPS_PER_BLOCK, SSM_CH_PER_GROUP, SSM_STATE)
        return jnp.einsum('nghp,gk->nghkp', m, eye).reshape(nb, 128, S5_STATES)

    def blockdiag_c(m):
        m = jnp.swapaxes(m, 1, 2).reshape(nb, S5_GROUPS_PER_BLOCK, SSM_STATE, SSM_CH_PER_GROUP)
        return jnp.einsum('ngph,gk->ngpkh', m, eye).reshape(nb, S5_STATES, 128)

    log_dt_states = jnp.broadcast_to(log_dt[:, None], lam_re.shape)
    lam = jnp.stack([m.reshape(nb, S5_STATES) for m in (lam_re, lam_im, log_dt_states)], axis=1)
    bmat = jnp.stack([blockdiag_b(b_re), blockdiag_b(b_im)], axis=1)
    cmat = jnp.stack([blockdiag_c(c_re), blockdiag_c(c_im)], axis=1).astype(BF16)
    d = d_skip.reshape(nb, 1, 128)
    r = np.arange(S5_CHUNK)
    perm = np.zeros((S5_CHUNK, S5_CHUNK), np.float32)
    perm[r, (r % 8) * S5_SUB + r // 8] = 1.0
    perm = jnp.asarray(perm, BF16)
    return pl.pallas_call(
        _s5_kernel,
        grid=(bsz, nb // g, seq_len // S5_CHUNK),
        in_specs=[
            pl.BlockSpec((1, S5_CHUNK, 128 * g), lambda b, k, c: (b, c, k)),
            pl.BlockSpec((g, 3, S5_STATES), lambda b, k, c: (k, 0, 0)),
            pl.BlockSpec((g, 2, 128, S5_STATES), lambda b, k, c: (k, 0, 0, 0)),
            pl.BlockSpec((g, 2, S5_STATES, 128), lambda b, k, c: (k, 0, 0, 0)),
            pl.BlockSpec((g, 1, 128), lambda b, k, c: (k, 0, 0)),
            pl.BlockSpec((S5_CHUNK, S5_CHUNK), lambda b, k, c: (0, 0)),
            pl.BlockSpec((S5_CHUNK, S5_CHUNK), lambda b, k, c: (0, 0)),
        ],
        out_specs=pl.BlockSpec((1, S5_CHUNK, 128 * g), lambda b, k, c: (b, c, k)),
        out_shape=jax.ShapeDtypeStruct((bsz, seq_len, SSM_WIDTH), F32),
        scratch_shapes=[pltpu.VMEM((g, S5_CHUNK, S5_STATES), F32), pltpu.VMEM((g, S5_CHUNK, S5_STATES), F32),
                        pltpu.VMEM((g, S5_SUB, 8, S5_STATES), F32), pltpu.VMEM((g, S5_SUB, 8, S5_STATES), F32),
                        pltpu.VMEM((g, 2, S5_STATES), F32), pltpu.VMEM((g, 2, S5_STATES), F32),
                        pltpu.VMEM((g, 2, 128, S5_STATES), BF16)],
        compiler_params=pltpu.CompilerParams(
            dimension_semantics=("arbitrary", "arbitrary", "arbitrary"), vmem_limit_bytes=V7X_VMEM_LIMIT_BYTES),
        name="s5_scan",
    )(u, lam, bmat, cmat, d, perm, perm.T)


def _softmax_tile(s, m_old):
    m_new = jnp.maximum(m_old, jnp.max(s, axis=1, keepdims=True))
    m_wide = jnp.concatenate([m_new] * (s.shape[1] // 128), axis=1)
    return m_new, jnp.exp2(m_old - m_new), jnp.exp2(s - m_wide)


def _lane_is_low(shape):
    return lax.broadcasted_iota(jnp.int32, shape, 1) < HEAD_DIM


def _pad_kt(kt, variant):
    z = jnp.zeros_like(kt)
    return jnp.concatenate([kt, z] if variant == 0 else [z, kt], axis=0)


def _pad_v(vv, variant):
    low = _lane_is_low(vv.shape)
    keep = low if variant == 0 else jnp.logical_not(low)
    return jnp.where(keep, vv, jnp.ones_like(vv))


def _finish(acc, variant):
    lane = lax.broadcasted_iota(jnp.int32, acc.shape, 1)
    lsel = lane == (HEAD_DIM if variant == 0 else 0)
    l = jnp.sum(jnp.where(lsel, acc, 0.0), axis=1, keepdims=True)
    keep = (lane < HEAD_DIM) if variant == 0 else (lane >= HEAD_DIM)
    return jnp.where(keep, acc / l, 0.0)


def _nsa_kernel(q_ref, g_ref, kct_ref, vc_ref, kst_ref, vs_ref, kwt_ref, vw_ref, ovl_ref, gx_ref, o_ref,
                m_scr, acc_scr, s_scr_a, s_scr_b, p_scr, *, seq_len):
    s_slots = (s_scr_a, s_scr_b)
    n_sel = seq_len // SEL_BLOCK
    n_cpad = seq_len // CMP_STRIDE
    sel_tile = PROJ_TILE
    blocks_per_tile = sel_tile // SEL_BLOCK
    win_tiles = WINDOW // Q_BLOCK + 1
    n_pair = Q_PER_KV // 2
    rows = n_pair * Q_BLOCK
    i = pl.program_id(2)
    t0 = i * Q_BLOCK

    qb = q_ref[0]
    qst = jnp.concatenate([qb[:, p * 128:(p + 1) * 128] for p in range(n_pair)], axis=0)

    sig = jax.nn.sigmoid(g_ref[0])
    sig_hi = sig.astype(BF16)
    sig_lo = (sig - sig_hi.astype(F32)).astype(BF16)
    gx = gx_ref[0]
    gexp = (jnp.dot(sig_hi, gx, preferred_element_type=F32) + jnp.dot(sig_lo, gx, preferred_element_type=F32))

    def gate_of(branch):
        base = branch * n_pair * 128
        return jnp.concatenate([gexp[:, base + p * 128: base + (p + 1) * 128] for p in range(n_pair)], axis=0)

    t_row = t0 + lax.broadcasted_iota(jnp.int32, (Q_BLOCK, 1), 0)

    slab = 64
    kct = kct_ref[0, 0]
    s_cmp = [jnp.dot(qst, _pad_kt(kct, v), preferred_element_type=F32) for v in range(2)]
    n_kblk = seq_len // Q_BLOCK
    w0 = jnp.clip(i - (win_tiles - 1), 0, n_kblk - win_tiles)
    kw = jnp.concatenate([kwt_ref[0, 0, w0 + j] for j in range(win_tiles)], axis=1)
    s_win = [jnp.dot(qst, _pad_kt(kw, v), preferred_element_type=F32) for v in range(2)]
    for v in range(2):
        s_slots[0][v] = jnp.dot(qst, _pad_kt(kst_ref[0, 0, 0], v), preferred_element_type=F32)

    n_iota = lax.broadcasted_iota(jnp.int32, (Q_BLOCK, n_cpad), 1)
    cmask = (n_iota * CMP_STRIDE + (CMP_BLOCK - 1)) <= t_row
    cmask4 = jnp.concatenate([cmask] * n_pair, axis=0)
    vcd = vc_ref[0, 0]
    p_sum = jnp.zeros((Q_BLOCK, n_cpad), F32)
    out = jnp.zeros((rows, 128), F32)
    o_c = jnp.zeros((rows, 128), F32)
    for v in range(2):
        s = jnp.where(cmask4, s_cmp[v], NEG)
        m = jnp.max(s, axis=1, keepdims=True)
        e = jnp.where(cmask4, jnp.exp2(s - m), 0.0)
        l = jnp.sum(e, axis=1, keepdims=True)
        p = e * (1.0 / jnp.maximum(l, 1e-30))
        for pp in range(n_pair):
            p_sum = p_sum + p[pp * Q_BLOCK:(pp + 1) * Q_BLOCK]
        low = _lane_is_low((n_cpad, 128))
        vz = jnp.where(low if v == 0 else jnp.logical_not(low), vcd, jnp.zeros_like(vcd))
        o_c = o_c + jnp.dot(p.astype(BF16), vz, preferred_element_type=F32)
    out = out + gate_of(0) * o_c

    ps_hi = p_sum.astype(BF16)
    ps_lo = (p_sum - ps_hi.astype(F32)).astype(BF16)
    ovl = ovl_ref[...]
    nt = (((1,), (1,)), ((), ()))
    imp_t = (lax.dot_general(ovl, ps_hi, nt, preferred_element_type=F32)
             + lax.dot_general(ovl, ps_lo, nt, preferred_element_type=F32))

    vw = jnp.concatenate([vw_ref[0, 0, w0 + j] for j in range(win_tiles)], axis=0)
    kpos_w = w0 * Q_BLOCK + lax.broadcasted_iota(jnp.int32, (Q_BLOCK, win_tiles * Q_BLOCK), 1)
    diff = t_row - kpos_w
    wbias = jnp.where((diff >= 0) & (diff < WINDOW), 0.0, NEG)
    wbias4 = jnp.concatenate([wbias] * n_pair, axis=0)
    o_w = jnp.zeros((rows, 128), F32)
    for v in range(2):
        s = s_win[v] + wbias4
        m = jnp.max(s, axis=1, keepdims=True)
        p = jnp.exp2(s - m)
        o_w = o_w + _finish(jnp.dot(p.astype(BF16), _pad_v(vw, v), preferred_element_type=F32), v)
    out = out + gate_of(2) * o_w

    s_iota = lax.broadcasted_iota(jnp.int32, (n_sel, Q_BLOCK), 0)
    t_lane = t0 + lax.broadcasted_iota(jnp.int32, (n_sel, Q_BLOCK), 1)
    cur = t_lane // SEL_BLOCK
    forced = (s_iota == 0) | (s_iota == cur) | (s_iota == cur - 1)
    valid = s_iota * SEL_BLOCK <= t_lane
    score = jnp.where(forced, FORCE, jnp.where(valid, imp_t, -1.0))
    s_f = s_iota.astype(F32)
    sel_t = jnp.zeros((n_sel, Q_BLOCK), F32)
    for _ in range(min(SEL_TOPK, n_sel)):
        mx = jnp.max(score, axis=0, keepdims=True)
        idx = jnp.min(jnp.where(score == mx, s_f, float(n_sel)), axis=0, keepdims=True)
        hit = s_f == idx
        sel_t = jnp.where(hit, 1.0, sel_t)
        score = jnp.where(hit, -3e38, score)
    selmask = sel_t.T.astype(BF16)

    m_scr[...] = jnp.full(m_scr.shape, NEG, F32)
    acc_scr[...] = jnp.zeros(acc_scr.shape, F32)
    n_tiles = (t0 + Q_BLOCK + sel_tile - 1) // sel_tile

    last_tile = seq_len // sel_tile - 1

    def bias_of(kt):
        blk = kt * blocks_per_tile + lax.broadcasted_iota(jnp.int32, (n_sel, sel_tile), 1) // SEL_BLOCK
        expand = (lax.broadcasted_iota(jnp.int32, (n_sel, sel_tile), 0) == blk).astype(BF16)
        selexp = jnp.dot(selmask, expand, preferred_element_type=F32)
        kpos = kt * sel_tile + lax.broadcasted_iota(jnp.int32, (Q_BLOCK, sel_tile), 1)
        bias = jnp.where((selexp > 0.5) & (kpos <= t_row), 0.0, NEG)
        return jnp.concatenate([bias] * n_pair, axis=0)

    def scores_into(slot, kt):
        bias4 = bias_of(kt)
        kt_tile = kst_ref[0, 0, jnp.minimum(kt, last_tile)]
        for v in range(2):
            s_slots[slot][v] = jnp.dot(qst, _pad_kt(kt_tile, v), preferred_element_type=F32) + bias4

    def attend_from(slot, kt):
        v_tile = vs_ref[0, 0, jnp.minimum(kt, last_tile)]
        for v in range(2):
            for h in range(rows // slab):
                r = slice(h * slab, (h + 1) * slab)
                m_new, alpha, p = _softmax_tile(s_slots[slot][v, r, :], m_scr[v, r, :])
                m_scr[v, r, :] = m_new
                acc_scr[v, r, :] = alpha * acc_scr[v, r, :]
                p_scr[v, r, :] = p.astype(BF16)
            acc_scr[v] += jnp.dot(p_scr[v], _pad_v(v_tile, v), preferred_element_type=F32)

    bias_first = bias_of(0)
    for v in range(2):
        s_slots[0][v] = s_slots[0][v] + bias_first

    def sel_body(j, carry):
        kt = 2 * j
        scores_into(1, kt + 1)
        attend_from(0, kt)
        scores_into(0, kt + 2)
        attend_from(1, kt + 1)
        return carry

    lax.fori_loop(0, (n_tiles + 1) // 2, sel_body, 0)
    out = out + gate_of(1) * (_finish(acc_scr[0], 0) + _finish(acc_scr[1], 1))

    o_ref[0] = jnp.concatenate([out[p * Q_BLOCK:(p + 1) * Q_BLOCK] for p in range(n_pair)], axis=1)


def nsa_attention(q, gate_pad, kct, vc, kst, vs, kwt, vw):
    bsz, seq_len, _ = q.shape
    n_sel = seq_len // SEL_BLOCK
    n_cpad = seq_len // CMP_STRIDE
    n_cmp = (seq_len - CMP_BLOCK) // CMP_STRIDE + 1
    n_pair = Q_PER_KV // 2
    cs = np.arange(n_cpad) * CMP_STRIDE
    ce = cs + CMP_BLOCK - 1
    ss = np.arange(n_sel) * SEL_BLOCK
    se = ss + SEL_BLOCK - 1
    ovl = (cs[None, :] <= se[:, None]) & (ce[None, :] >= ss[:, None]) & (np.arange(n_cpad)[None, :] < n_cmp)
    ovl = jnp.asarray(ovl.astype(np.float32), BF16)
    gx = np.zeros((NSA_KV_HEADS, 128, N_BRANCH * n_pair * 128), np.float32)
    for k in range(NSA_KV_HEADS):
        for hl in range(Q_PER_KV):
            for br in range(N_BRANCH):
                c0 = br * n_pair * 128 + hl * HEAD_DIM
                gx[k, (k * Q_PER_KV + hl) * N_BRANCH + br, c0:c0 + HEAD_DIM] = 1.0
    gx = jnp.asarray(gx, BF16)
    width = Q_PER_KV * HEAD_DIM
    full = lambda *shape: pl.BlockSpec((1, 1) + shape, lambda b, k, i: (b, k) + (0,) * len(shape))
    return pl.pallas_call(
        functools.partial(_nsa_kernel, seq_len=seq_len),
        grid=(bsz, NSA_KV_HEADS, seq_len // Q_BLOCK),
        in_specs=[
            pl.BlockSpec((1, Q_BLOCK, width), lambda b, k, i: (b, i, k)),
            pl.BlockSpec((1, Q_BLOCK, 128), lambda b, k, i: (b, i, 0)),
            full(HEAD_DIM, n_cpad), full(n_cpad, 128),
            full(seq_len // PROJ_TILE, HEAD_DIM, PROJ_TILE), full(seq_len // PROJ_TILE, PROJ_TILE, 128),
            full(seq_len // Q_BLOCK, HEAD_DIM, Q_BLOCK), full(seq_len // Q_BLOCK, Q_BLOCK, 128),
            pl.BlockSpec((n_sel, n_cpad), lambda b, k, i: (0, 0)),
            pl.BlockSpec((1, 128, N_BRANCH * n_pair * 128), lambda b, k, i: (k, 0, 0)),
        ],
        out_specs=pl.BlockSpec((1, Q_BLOCK, width), lambda b, k, i: (b, i, k)),
        out_shape=jax.ShapeDtypeStruct((bsz, seq_len, NSA_WIDTH), F32),
        scratch_shapes=[pltpu.VMEM((2, n_pair * Q_BLOCK, 128), F32), pltpu.VMEM((2, n_pair * Q_BLOCK, 128), F32),
                        pltpu.VMEM((2, n_pair * Q_BLOCK, PROJ_TILE), F32),
                        pltpu.VMEM((2, n_pair * Q_BLOCK, PROJ_TILE), F32),
                        pltpu.VMEM((2, n_pair * Q_BLOCK, PROJ_TILE), BF16)],
        compiler_params=pltpu.CompilerParams(
            dimension_semantics=("arbitrary", "arbitrary", "arbitrary"), vmem_limit_bytes=V7X_VMEM_LIMIT_BYTES),
        name="nsa_attention",
    )(q, gate_pad, kct, vc, kst, vs, kwt, vw, ovl, gx)


ROUTER_TILE = 512
MOE_TILE = 1024
MOE_SUB = 256
MOE_DENSE = 32
MOE_SPILL = 32
MOE_SLOT = MOE_DENSE + MOE_SPILL
MOE_GROUP = 4
MOE_GATHER_GROUP = 4


def _first_max_mask(x, idx_f, axis):
    mx = jnp.max(x, axis=axis, keepdims=True)
    first = jnp.min(jnp.where(x == mx, idx_f, 1e9), axis=axis, keepdims=True)
    return idx_f == first, mx


def _router_kernel(x_ref, wrt_ref, bias_ref, w_ref, sel_ref):
    per_group = N_EXPERTS // N_EXPERT_GROUPS
    tr = x_ref.shape[0]
    nt = (((1,), (1,)), ((), ()))
    logits = lax.dot_general(wrt_ref[...], x_ref[...].astype(BF16), nt, preferred_element_type=F32)
    aff = jax.nn.sigmoid(logits)
    biased = aff + bias_ref[...]
    grp = biased.reshape(N_EXPERT_GROUPS, per_group, tr)
    in_grp = lax.broadcasted_iota(jnp.int32, grp.shape, 1).astype(F32)
    hit1, m1 = _first_max_mask(grp, in_grp, 1)
    m2 = jnp.max(jnp.where(hit1, -jnp.inf, grp), axis=1, keepdims=True)
    gscore = (m1 + m2).reshape(N_EXPERT_GROUPS, tr)
    g_idx = lax.broadcasted_iota(jnp.int32, gscore.shape, 0).astype(F32)
    gsel = jnp.zeros(gscore.shape, F32)
    for _ in range(TOPK_GROUPS):
        hit, _ = _first_max_mask(gscore, g_idx, 0)
        gsel = jnp.where(hit, 1.0, gsel)
        gscore = jnp.where(hit, -jnp.inf, gscore)
    gmask = jnp.broadcast_to(gsel.reshape(N_EXPERT_GROUPS, 1, tr), grp.shape).reshape(N_EXPERTS, tr)
    cand = jnp.where(gmask > 0.5, biased, NEG)
    e_idx = lax.broadcasted_iota(jnp.int32, cand.shape, 0).astype(F32)
    sel = jnp.zeros(cand.shape, F32)
    for _ in range(TOP_K):
        hit, _ = _first_max_mask(cand, e_idx, 0)
        sel = jnp.where(hit, 1.0, sel)
        cand = jnp.where(hit, -jnp.inf, cand)
    w = jnp.where(sel > 0.5, aff, 0.0)
    w_ref[...] = w / jnp.sum(w, axis=0, keepdims=True) * ROUTED_SCALE
    sel_ref[...] = sel


def moe_router(xt, w_router, router_bias):
    n_tok = xt.shape[0]
    wrt = w_router.T.astype(BF16)
    return pl.pallas_call(
        _router_kernel,
        grid=(n_tok // ROUTER_TILE,),
        in_specs=[pl.BlockSpec((ROUTER_TILE, D_MODEL), lambda i: (i, 0)),
                  pl.BlockSpec((N_EXPERTS, D_MODEL), lambda i: (0, 0)),
                  pl.BlockSpec((N_EXPERTS, 1), lambda i: (0, 0))],
        out_specs=[pl.BlockSpec((N_EXPERTS, ROUTER_TILE), lambda i: (0, i)),
                   pl.BlockSpec((N_EXPERTS, ROUTER_TILE), lambda i: (0, i))],
        out_shape=[jax.ShapeDtypeStruct((N_EXPERTS, n_tok), F32), jax.ShapeDtypeStruct((N_EXPERTS, n_tok), F32)],
        compiler_params=pltpu.CompilerParams(dimension_semantics=("arbitrary",),
                                             vmem_limit_bytes=V7X_VMEM_LIMIT_BYTES),
        name="moe_router",
    )(xt, wrt, router_bias.reshape(N_EXPERTS, 1))


def _moe_kernel(cnt_ref, x_ref, sel_ref, w_ref, init_ref, wg_ref, wu_ref, wd_ref, lng_ref, lnb_ref, o_ref,
                rank_scr, ybuf_scr, sbuf_scr, xe_scr):
    i = pl.program_id(0)
    e = pl.program_id(1)
    tm = x_ref.shape[0]
    n_sub = tm // MOE_SUB
    tn = (((0,), (0,)), ((), ()))
    sub = lambda q: slice(q * MOE_SUB, (q + 1) * MOE_SUB)

    @pl.when(e == 0)
    def _():
        o_ref[...] = init_ref[...]
        before = (lax.broadcasted_iota(jnp.int32, (MOE_SUB, MOE_SUB), 0)
                  < lax.broadcasted_iota(jnp.int32, (MOE_SUB, MOE_SUB), 1))
        before = jnp.where(before, 1.0, 0.0).astype(BF16)
        for q in range(n_sub):
            rank_scr[:, sub(q)] = jnp.dot(sel_ref[:, sub(q)].astype(BF16), before, preferred_element_type=F32)

    def spill_offsets(ee):
        offs, total = [], jnp.int32(0)
        for q in range(n_sub):
            offs.append(total)
            total = total + jnp.maximum(cnt_ref[(i * N_EXPERTS + ee) * n_sub + q] - MOE_DENSE, 0)
        return offs, total

    def dense_hits(rank_row, sel_row):
        row = lax.broadcasted_iota(jnp.int32, (MOE_DENSE, MOE_SUB), 0).astype(F32)
        return [(rank_row[:, sub(q)] == row) & (sel_row[:, sub(q)] > 0.5) for q in range(n_sub)]

    def spill_hits(rank_row, sel_row, offs, c):
        row = (c * MOE_SPILL + lax.broadcasted_iota(jnp.int32, (MOE_SPILL, MOE_SUB), 0)).astype(F32)
        hits = []
        for q in range(n_sub):
            rk = rank_row[:, sub(q)]
            pos = rk - float(MOE_DENSE) + offs[q].astype(F32)
            hits.append((pos == row) & (rk >= float(MOE_DENSE)) & (sel_row[:, sub(q)] > 0.5))
        return hits

    as_bf16 = lambda hit: jnp.where(hit, 1.0, 0.0).astype(BF16)

    def swiglu(xe):
        g = jnp.dot(xe, wg_ref[0], preferred_element_type=F32)
        u = jnp.dot(xe, wu_ref[0], preferred_element_type=F32)
        h = (jax.nn.silu(g) * u).astype(BF16)
        return jnp.dot(h, wd_ref[0], preferred_element_type=F32).astype(BF16)

    sel_e = sel_ref[pl.ds(e, 1), :]
    rank_e = rank_scr[pl.ds(e, 1), :]
    w_e = w_ref[pl.ds(e, 1), :]
    weighted = lambda hits: [jnp.where(hits[q], w_e[:, sub(q)], 0.0).astype(BF16) for q in range(n_sub)]
    slot = e % MOE_GROUP
    gslot = e % MOE_GATHER_GROUP

    @pl.when(gslot == 0)
    def _():
        dense = [[] for _ in range(n_sub)]
        spill = [[] for _ in range(n_sub)]
        for gi in range(MOE_GATHER_GROUP):
            rk, sl = rank_scr[pl.ds(e + gi, 1), :], sel_ref[pl.ds(e + gi, 1), :]
            offs_g, _ = spill_offsets(e + gi)
            hd, hs = dense_hits(rk, sl), spill_hits(rk, sl, offs_g, 0)
            for q in range(n_sub):
                dense[q].append(as_bf16(hd[q]))
                spill[q].append(as_bf16(hs[q]))
        n_d = MOE_GATHER_GROUP * MOE_DENSE
        x_spill = jnp.zeros((MOE_GATHER_GROUP * MOE_SPILL, D_MODEL), F32)
        for q in range(n_sub):
            xg = jnp.dot(jnp.concatenate(dense[q] + spill[q], axis=0), x_ref[sub(q), :],
                         preferred_element_type=F32)
            for gi in range(MOE_GATHER_GROUP):
                xe_scr[gi, q] = xg[gi * MOE_DENSE:(gi + 1) * MOE_DENSE].astype(BF16)
            x_spill = x_spill + xg[n_d:]
        for gi in range(MOE_GATHER_GROUP):
            xe_scr[gi, n_sub] = x_spill[gi * MOE_SPILL:(gi + 1) * MOE_SPILL].astype(BF16)

    offs, n_spill = spill_offsets(e)
    sc_dense = weighted(dense_hits(rank_e, sel_e))
    sc_spill = weighted(spill_hits(rank_e, sel_e, offs, 0))
    y = swiglu(xe_scr[gslot].reshape((n_sub + 1) * MOE_DENSE, D_MODEL))
    y_spill = y[n_sub * MOE_DENSE:]
    for q in range(n_sub):
        sbuf_scr[q, slot] = jnp.concatenate([sc_dense[q], sc_spill[q]], axis=0)
        ybuf_scr[q, slot] = jnp.concatenate([y[q * MOE_DENSE:(q + 1) * MOE_DENSE], y_spill], axis=0)

    @pl.when(slot == MOE_GROUP - 1)
    def _():
        for q in range(n_sub):
            o_ref[sub(q), :] += lax.dot_general(sbuf_scr[q].reshape(MOE_GROUP * MOE_SLOT, MOE_SUB),
                                                ybuf_scr[q].reshape(MOE_GROUP * MOE_SLOT, D_MODEL), tn,
                                                preferred_element_type=F32)

    def more_spill(c, carry):
        hits = spill_hits(rank_e, sel_e, offs, c)
        sc = weighted(hits)
        xs = jnp.zeros((MOE_SPILL, D_MODEL), F32)
        for q in range(n_sub):
            xs = xs + jnp.dot(as_bf16(hits[q]), x_ref[sub(q), :], preferred_element_type=F32)
        yy = swiglu(xs.astype(BF16))
        for q in range(n_sub):
            o_ref[sub(q), :] += lax.dot_general(sc[q], yy, tn, preferred_element_type=F32)
        return carry

    lax.fori_loop(1, (n_spill + MOE_SPILL - 1) // MOE_SPILL, more_spill, 0)

    @pl.when(e == N_EXPERTS - 1)
    def _():
        o_ref[...] = _layer_norm(o_ref[...], lng_ref[...], lnb_ref[...])


def moe_routed(x_bf16, sel_t, w_t, init, w_gate, w_up, w_down, ln_g, ln_b):
    n_tok = x_bf16.shape[0]
    n_tiles = n_tok // MOE_TILE
    per_sub = jnp.sum(sel_t.reshape(N_EXPERTS, n_tiles, MOE_TILE // MOE_SUB, MOE_SUB), axis=-1)
    cnt = jnp.transpose(per_sub, (1, 0, 2)).astype(jnp.int32).reshape(-1)
    grid_spec = pltpu.PrefetchScalarGridSpec(
        num_scalar_prefetch=1,
        grid=(n_tiles, N_EXPERTS),
        in_specs=[
            pl.BlockSpec((MOE_TILE, D_MODEL), lambda i, e, cnt: (i, 0), pipeline_mode=pl.Buffered(1)),
            pl.BlockSpec((N_EXPERTS, MOE_TILE), lambda i, e, cnt: (0, i)),
            pl.BlockSpec((N_EXPERTS, MOE_TILE), lambda i, e, cnt: (0, i)),
            pl.BlockSpec((MOE_TILE, D_MODEL), lambda i, e, cnt: (i, 0), pipeline_mode=pl.Buffered(1)),
            pl.BlockSpec((1, D_MODEL, EXPERT_FF), lambda i, e, cnt: (e, 0, 0)),
            pl.BlockSpec((1, D_MODEL, EXPERT_FF), lambda i, e, cnt: (e, 0, 0)),
            pl.BlockSpec((1, EXPERT_FF, D_MODEL), lambda i, e, cnt: (e, 0, 0)),
            pl.BlockSpec((1, D_MODEL), lambda i, e, cnt: (0, 0)),
            pl.BlockSpec((1, D_MODEL), lambda i, e, cnt: (0, 0)),
        ],
        out_specs=pl.BlockSpec((MOE_TILE, D_MODEL), lambda i, e, cnt: (i, 0)),
        scratch_shapes=[pltpu.VMEM((N_EXPERTS, MOE_TILE), F32),
                        pltpu.VMEM((MOE_TILE // MOE_SUB, MOE_GROUP, MOE_SLOT, D_MODEL), BF16),
                        pltpu.VMEM((MOE_TILE // MOE_SUB, MOE_GROUP, MOE_SLOT, MOE_SUB), BF16),
                        pltpu.VMEM((MOE_GATHER_GROUP, MOE_TILE // MOE_SUB + 1, MOE_DENSE, D_MODEL), BF16)],
    )
    return pl.pallas_call(
        _moe_kernel,
        grid_spec=grid_spec,
        out_shape=jax.ShapeDtypeStruct((n_tok, D_MODEL), F32),
        compiler_params=pltpu.CompilerParams(dimension_semantics=("arbitrary", "arbitrary"),
                                             vmem_limit_bytes=V7X_VMEM_LIMIT_BYTES),
        name="moe_routed",
    )(cnt, x_bf16, sel_t, w_t, init, w_gate, w_up, w_down, ln_g.reshape(1, D_MODEL), ln_b.reshape(1, D_MODEL))


def _shared_ffn_kernel(x_ref, wg_ref, wu_ref, wd_ref, o_ref, xb_ref):
    x = x_ref[...]
    xb = x.astype(BF16)
    h = jax.nn.silu(jnp.dot(xb, wg_ref[...], preferred_element_type=F32)) * jnp.dot(
        xb, wu_ref[...], preferred_element_type=F32)
    o_ref[...] = DEEPNORM_ALPHA * x + jnp.dot(h.astype(BF16), wd_ref[...], preferred_element_type=F32)
    xb_ref[...] = xb


def shared_ffn(xt, wg, wu, wd, tm=1024):
    n_tok, d = xt.shape
    ff = wg.shape[1]
    once = pl.Buffered(1)
    return pl.pallas_call(
        _shared_ffn_kernel,
        grid=(n_tok // tm,),
        in_specs=[pl.BlockSpec((tm, d), lambda i: (i, 0)),
                  pl.BlockSpec((d, ff), lambda i: (0, 0), pipeline_mode=once),
                  pl.BlockSpec((d, ff), lambda i: (0, 0), pipeline_mode=once),
                  pl.BlockSpec((ff, d), lambda i: (0, 0), pipeline_mode=once)],
        out_specs=[pl.BlockSpec((tm, d), lambda i: (i, 0)), pl.BlockSpec((tm, d), lambda i: (i, 0))],
        out_shape=[jax.ShapeDtypeStruct((n_tok, d), F32), jax.ShapeDtypeStruct((n_tok, d), BF16)],
        compiler_params=pltpu.CompilerParams(dimension_semantics=("arbitrary",),
                                             vmem_limit_bytes=V7X_VMEM_LIMIT_BYTES),
        name="shared_ffn",
    )(xt, wg.astype(BF16), wu.astype(BF16), wd.astype(BF16))


def hybrid_layer(x, positions, w_in, lam_re, lam_im, log_dt, ssm_b_re, ssm_b_im, ssm_c_re, ssm_c_im, ssm_d,
                 w_glu, cmp_pos_k, cmp_pos_v, w_cmp_k1, w_cmp_k2, w_cmp_v1, w_cmp_v2, w_out, ln1_g, ln1_b,
                 w_router, router_bias, w_gate, w_up, w_down, ws_gate, ws_up, ws_down, ln2_g, ln2_b):
    bsz, L, _ = x.shape
    sizes = [SSM_WIDTH, NSA_WIDTH] + [KV_WIDTH] * 6 + [NSA_HEADS * N_BRANCH]
    o = [0] + [int(v) for v in np.cumsum(sizes)]
    col = lambda j: w_in[:, o[j]:o[j + 1]]
    dup = lambda w: jnp.concatenate([w[:, h * HEAD_DIM:(h + 1) * HEAD_DIM] for h in (0, 0, 1, 1)], axis=1)
    gate_cols = jnp.pad(col(8), ((0, 0), (0, 128 - NSA_HEADS * N_BRANCH)))
    w_uq = w_in[:, :o[2]].astype(BF16)
    w_kv = jnp.concatenate([col(4), col(6), dup(col(5)), dup(col(7)), col(2), col(3), gate_cols], axis=1).astype(BF16)

    xt = x.reshape(bsz * L, D_MODEL)
    u, q = proj_uq(xt, w_uq, positions.reshape(bsz * L, 1))
    kst, kwt, vs, vw, kc_raw, vc_raw, gate_pad = proj_kv(x, w_kv, positions.reshape(bsz, L, 1))
    kct, vcd = compress_kv(kc_raw, vc_raw, positions, cmp_pos_k, cmp_pos_v, w_cmp_k1, w_cmp_k2, w_cmp_v1, w_cmp_v2)
    y_s5 = s5_scan(u.reshape(bsz, L, SSM_WIDTH), lam_re, lam_im, log_dt, ssm_b_re, ssm_b_im, ssm_c_re, ssm_c_im, ssm_d)
    vw = vw.reshape(bsz, NSA_KV_HEADS, L // Q_BLOCK, Q_BLOCK, 128)
    y_nsa = nsa_attention(q.reshape(bsz, L, NSA_WIDTH), gate_pad, kct, vcd, kst, vs, kwt, vw)
    x1 = out_proj_ln(y_s5.reshape(bsz * L, SSM_WIDTH), y_nsa.reshape(bsz * L, NSA_WIDTH), xt, w_glu, w_out,
                     ln1_g, ln1_b)
    w_t, sel_t = moe_router(x1, w_router, router_bias)
    acc0, x1b = shared_ffn(x1, ws_gate, ws_up, ws_down)
    out = moe_routed(x1b, sel_t, w_t, acc0, w_gate.astype(BF16), w_up.astype(BF16), w_down.astype(BF16),
                     ln2_g, ln2_b)
    return out.reshape(bsz, L, D_MODEL)


def kernel(x, positions, w_in, lam_re, lam_im, log_dt, ssm_b_re, ssm_b_im, ssm_c_re, ssm_c_im, ssm_d, w_glu, cmp_pos_k, cmp_pos_v, w_cmp_k1, w_cmp_k2, w_cmp_v1, w_cmp_v2, w_out, ln1_g, ln1_b, w_router, router_bias, w_gate, w_up, w_down, ws_gate, ws_up, ws_down, ln2_g, ln2_b):
    params = (w_in, lam_re, lam_im, log_dt, ssm_b_re, ssm_b_im, ssm_c_re, ssm_c_im, ssm_d,
              w_glu, cmp_pos_k, cmp_pos_v, w_cmp_k1, w_cmp_k2, w_cmp_v1, w_cmp_v2, w_out, ln1_g, ln1_b,
              w_router, router_bias, w_gate, w_up, w_down, ws_gate, ws_up, ws_down, ln2_g, ln2_b)
    return hybrid_layer(x, positions, *(p[0] for p in params))
```

```python
import functools
import math

import numpy as np
import jax
import jax.numpy as jnp
from jax import lax
from jax.experimental import pallas as pl
from jax.experimental.pallas import tpu as pltpu

D_MODEL = 2048
SSM_WIDTH = 1024
SSM_CH_PER_GROUP = 16
SSM_GROUPS = 64
SSM_STATE = 64
NSA_HEADS = 16
NSA_KV_HEADS = 2
HEAD_DIM = 64
Q_PER_KV = NSA_HEADS // NSA_KV_HEADS
NSA_WIDTH = NSA_HEADS * HEAD_DIM
KV_WIDTH = NSA_KV_HEADS * HEAD_DIM
N_BRANCH = 3
CMP_BLOCK = 32
CMP_STRIDE = 16
SEL_BLOCK = 64
SEL_TOPK = 16
WINDOW = 512
Q_BLOCK = 128
ROPE_THETA = 10000.0
N_EXPERTS = 64
TOP_K = 8
N_EXPERT_GROUPS = 8
TOPK_GROUPS = 4
ROUTED_SCALE = 2.5
EXPERT_FF = 512
DEPTH = 1
DEEPNORM_ALPHA = (2.0 * DEPTH) ** 0.25
LN_EPS = 1e-5
NEG = -1e30
FORCE = 1e4
F32 = jnp.float32
BF16 = jnp.bfloat16

V7X_VMEM_LIMIT_BYTES = 56 * 1024 * 1024


def _layer_norm(x, g, b):
    mu = jnp.mean(x, -1, keepdims=True)
    var = jnp.mean(jnp.square(x - mu), -1, keepdims=True)
    return (x - mu) * lax.rsqrt(var + LN_EPS) * g + b


def _rope_tables(pos_col, inv_row):
    ang = pos_col * inv_row
    return jnp.cos(ang), jnp.sin(ang)


def _rope_lanes(x, cos, sin):
    lane = lax.broadcasted_iota(jnp.int32, (x.shape[0], 128), 1)
    first_half = (lane % HEAD_DIM) < HEAD_DIM // 2
    outs = []
    for blk in range(x.shape[1] // 128):
        xb = x[:, blk * 128:(blk + 1) * 128]
        rot = jnp.where(first_half, -pltpu.roll(xb, 128 - HEAD_DIM // 2, 1), pltpu.roll(xb, HEAD_DIM // 2, 1))
        outs.append(xb * cos + rot * sin)
    return outs[0] if len(outs) == 1 else jnp.concatenate(outs, axis=1)


def _inv_freq_row():
    half = HEAD_DIM // 2
    inv = ROPE_THETA ** (-jnp.arange(half, dtype=F32) / half)
    return jnp.tile(inv, 128 // half).reshape(1, 128)


PROJ_TILE = 512
Q_SCALE = HEAD_DIM ** -0.5 * math.log2(math.e)


def _proj_uq_kernel(x_ref, w_ref, pos_ref, inv_ref, u_ref, q_ref):
    acc = jnp.dot(x_ref[...].astype(BF16), w_ref[...], preferred_element_type=F32)
    u_ref[...] = acc[:, :SSM_WIDTH]
    cos, sin = _rope_tables(pos_ref[...].astype(F32), inv_ref[...])
    q_ref[...] = (_rope_lanes(acc[:, SSM_WIDTH:], cos, sin) * Q_SCALE).astype(BF16)


def proj_uq(xt, w_uq, pos_col, tm=1024):
    n_tok = xt.shape[0]
    return pl.pallas_call(
        _proj_uq_kernel,
        grid=(n_tok // tm,),
        in_specs=[pl.BlockSpec((tm, D_MODEL), lambda i: (i, 0)),
                  pl.BlockSpec((D_MODEL, SSM_WIDTH + NSA_WIDTH), lambda i: (0, 0), pipeline_mode=pl.Buffered(1)),
                  pl.BlockSpec((tm, 1), lambda i: (i, 0)),
                  pl.BlockSpec((1, 128), lambda i: (0, 0))],
        out_specs=[pl.BlockSpec((tm, SSM_WIDTH), lambda i: (i, 0)),
                   pl.BlockSpec((tm, NSA_WIDTH), lambda i: (i, 0))],
        out_shape=[jax.ShapeDtypeStruct((n_tok, SSM_WIDTH), F32), jax.ShapeDtypeStruct((n_tok, NSA_WIDTH), BF16)],
        compiler_params=pltpu.CompilerParams(dimension_semantics=("arbitrary",),
                                             vmem_limit_bytes=V7X_VMEM_LIMIT_BYTES),
        name="proj_uq",
    )(xt, w_uq, pos_col, _inv_freq_row())


KV_COLS = 4 * KV_WIDTH + 2 * 2 * KV_WIDTH + 128


def _proj_kv_kernel(x_ref, w_ref, pos_ref, inv_ref, kst_ref, kwt_ref, vs_ref, vw_ref, kc_ref, vc_ref, g_ref):
    acc = jnp.dot(x_ref[0].astype(BF16), w_ref[...], preferred_element_type=F32)
    cos, sin = _rope_tables(pos_ref[0].astype(F32), inv_ref[...])
    ks_t = _rope_lanes(acc[:, 0:128], cos, sin).T
    kw_t = _rope_lanes(acc[:, 128:256], cos, sin).T
    for k in range(NSA_KV_HEADS):
        kst_ref[0, k, 0] = ks_t[k * HEAD_DIM:(k + 1) * HEAD_DIM].astype(BF16)
        for j in range(PROJ_TILE // Q_BLOCK):
            kwt_ref[0, k, j] = kw_t[k * HEAD_DIM:(k + 1) * HEAD_DIM, j * Q_BLOCK:(j + 1) * Q_BLOCK].astype(BF16)
        vs_ref[0, k, 0] = acc[:, 256 + k * 128: 256 + (k + 1) * 128].astype(BF16)
        vw_ref[0, k] = acc[:, 512 + k * 128: 512 + (k + 1) * 128].astype(BF16)
    kc_ref[0] = acc[:, 768:896]
    vc_ref[0] = acc[:, 896:1024]
    g_ref[0] = acc[:, 1024:1152]


def proj_kv(x, w_kv, pos_col3):
    bsz, seq_len, _ = x.shape
    n_t = seq_len // PROJ_TILE
    per = PROJ_TILE // Q_BLOCK
    return pl.pallas_call(
        _proj_kv_kernel,
        grid=(bsz, n_t),
        in_specs=[pl.BlockSpec((1, PROJ_TILE, D_MODEL), lambda b, i: (b, i, 0)),
                  pl.BlockSpec((D_MODEL, KV_COLS), lambda b, i: (0, 0)),
                  pl.BlockSpec((1, PROJ_TILE, 1), lambda b, i: (b, i, 0)),
                  pl.BlockSpec((1, 128), lambda b, i: (0, 0))],
        out_specs=[
            pl.BlockSpec((1, NSA_KV_HEADS, 1, HEAD_DIM, PROJ_TILE), lambda b, i: (b, 0, i, 0, 0)),
            pl.BlockSpec((1, NSA_KV_HEADS, per, HEAD_DIM, Q_BLOCK), lambda b, i: (b, 0, i, 0, 0)),
            pl.BlockSpec((1, NSA_KV_HEADS, 1, PROJ_TILE, 128), lambda b, i: (b, 0, i, 0, 0)),
            pl.BlockSpec((1, NSA_KV_HEADS, PROJ_TILE, 128), lambda b, i: (b, 0, i, 0)),
            pl.BlockSpec((1, PROJ_TILE, 128), lambda b, i: (b, i, 0)),
            pl.BlockSpec((1, PROJ_TILE, 128), lambda b, i: (b, i, 0)),
            pl.BlockSpec((1, PROJ_TILE, 128), lambda b, i: (b, i, 0)),
        ],
        out_shape=[
            jax.ShapeDtypeStruct((bsz, NSA_KV_HEADS, n_t, HEAD_DIM, PROJ_TILE), BF16),
            jax.ShapeDtypeStruct((bsz, NSA_KV_HEADS, seq_len // Q_BLOCK, HEAD_DIM, Q_BLOCK), BF16),
            jax.ShapeDtypeStruct((bsz, NSA_KV_HEADS, n_t, PROJ_TILE, 128), BF16),
            jax.ShapeDtypeStruct((bsz, NSA_KV_HEADS, seq_len, 128), BF16),
            jax.ShapeDtypeStruct((bsz, seq_len, 128), F32),
            jax.ShapeDtypeStruct((bsz, seq_len, 128), F32),
            jax.ShapeDtypeStruct((bsz, seq_len, 128), F32),
        ],
        compiler_params=pltpu.CompilerParams(dimension_semantics=("arbitrary", "arbitrary"),
                                             vmem_limit_bytes=V7X_VMEM_LIMIT_BYTES),
        name="proj_kv",
    )(x, w_kv, pos_col3, _inv_freq_row())


def _compress_kernel(ck_ref, cv_ref, pek_ref, pev_ref, w1k_ref, w1v_ref, w2k_ref, w2v_ref, pos_ref, inv_ref,
                     kct_ref, vcd_ref):
    def hidden(c_ref, pe_ref, w1_ref):
        c = c_ref[0]
        lo = jnp.dot((c + pe_ref[0]).astype(BF16), w1_ref[0], preferred_element_type=F32)
        hi = jnp.dot((c + pe_ref[1]).astype(BF16), w1_ref[1], preferred_element_type=F32)
        hi_next = jnp.concatenate([hi[1:], jnp.zeros((1, hi.shape[1]), F32)], axis=0)
        return jax.nn.gelu(lo + hi_next).astype(BF16)

    kc = jnp.dot(hidden(ck_ref, pek_ref, w1k_ref), w2k_ref[...], preferred_element_type=F32)
    cos, sin = _rope_tables(pos_ref[0], inv_ref[...])
    kc_t = _rope_lanes(kc, cos, sin).T
    vc = jnp.dot(hidden(cv_ref, pev_ref, w1v_ref), w2v_ref[...], preferred_element_type=F32)
    for k in range(NSA_KV_HEADS):
        kct_ref[0, k] = kc_t[k * HEAD_DIM:(k + 1) * HEAD_DIM].astype(BF16)
        vcd_ref[0, k] = vc[:, k * 128:(k + 1) * 128].astype(BF16)


def compress_kv(kc_raw, vc_raw, positions, cmp_pos_k, cmp_pos_v, w_k1, w_k2, w_v1, w_v2):
    bsz, seq_len, _ = kc_raw.shape
    n_chunk = seq_len // CMP_STRIDE
    width = CMP_STRIDE * 128
    eye = jnp.eye(NSA_KV_HEADS, dtype=F32)

    def chunk_pe(pe):
        pe = pe.reshape(2, CMP_STRIDE, 1, HEAD_DIM)
        return jnp.broadcast_to(pe, (2, CMP_STRIDE, NSA_KV_HEADS, HEAD_DIM)).reshape(2, 1, width)

    def chunk_w1(w1):
        hid = w1.shape[1]
        w = w1.reshape(2, CMP_STRIDE, HEAD_DIM, hid)
        return jnp.einsum('htdj,kc->htkdcj', w, eye).reshape(2, width, NSA_KV_HEADS * hid).astype(BF16)

    hid = w_k2.shape[0]
    w2k = jnp.einsum('jd,kc->kjcd', w_k2, eye).reshape(NSA_KV_HEADS * hid, NSA_KV_HEADS * HEAD_DIM).astype(BF16)
    w2v = jnp.einsum('jd,kc,r->kjcrd', w_v2, eye, jnp.ones((2,), F32)).reshape(
        NSA_KV_HEADS * hid, NSA_KV_HEADS * 128).astype(BF16)
    pos = positions.astype(F32).reshape(bsz, n_chunk, CMP_STRIDE).sum(-1)
    pos_next = jnp.concatenate([pos[:, 1:], pos[:, -1:]], axis=1)
    cmp_pos = ((pos + pos_next) / CMP_BLOCK).reshape(bsz, n_chunk, 1)
    return pl.pallas_call(
        _compress_kernel,
        grid=(bsz,),
        in_specs=[pl.BlockSpec((1, n_chunk, width), lambda b: (b, 0, 0)),
                  pl.BlockSpec((1, n_chunk, width), lambda b: (b, 0, 0)),
                  pl.BlockSpec((2, 1, width), lambda b: (0, 0, 0)),
                  pl.BlockSpec((2, 1, width), lambda b: (0, 0, 0)),
                  pl.BlockSpec((2, width, NSA_KV_HEADS * hid), lambda b: (0, 0, 0)),
                  pl.BlockSpec((2, width, NSA_KV_HEADS * hid), lambda b: (0, 0, 0)),
                  pl.BlockSpec((NSA_KV_HEADS * hid, NSA_KV_HEADS * HEAD_DIM), lambda b: (0, 0)),
                  pl.BlockSpec((NSA_KV_HEADS * hid, NSA_KV_HEADS * 128), lambda b: (0, 0)),
                  pl.BlockSpec((1, n_chunk, 1), lambda b: (b, 0, 0)),
                  pl.BlockSpec((1, 128), lambda b: (0, 0))],
        out_specs=[pl.BlockSpec((1, NSA_KV_HEADS, HEAD_DIM, n_chunk), lambda b: (b, 0, 0, 0)),
                   pl.BlockSpec((1, NSA_KV_HEADS, n_chunk, 128), lambda b: (b, 0, 0, 0))],
        out_shape=[jax.ShapeDtypeStruct((bsz, NSA_KV_HEADS, HEAD_DIM, n_chunk), BF16),
                   jax.ShapeDtypeStruct((bsz, NSA_KV_HEADS, n_chunk, 128), BF16)],
        compiler_params=pltpu.CompilerParams(dimension_semantics=("arbitrary",),
                                             vmem_limit_bytes=V7X_VMEM_LIMIT_BYTES),
        name="compress_kv",
    )(kc_raw.reshape(bsz, n_chunk, width), vc_raw.reshape(bsz, n_chunk, width), chunk_pe(cmp_pos_k),
      chunk_pe(cmp_pos_v), chunk_w1(w_k1), chunk_w1(w_v1), w2k, w2v, cmp_pos, _inv_freq_row())


S5_CHUNK = 512
S5_SUB = S5_CHUNK // 8
S5_GROUPS_PER_BLOCK = 8
S5_STATES = S5_GROUPS_PER_BLOCK * SSM_STATE
S5_STREAMS = 2


def _cmul_add(ar, ai, xr, xi, br, bi):
    return ar * xr - ai * xi + br, ar * xi + ai * xr + bi


def _s5_kernel(u_ref, lam_ref, bmat_ref, cmat_ref, d_ref, perm_ref, permt_ref, o_ref,
               xr_scr, xi_scr, pr_scr, pi_scr, carry_scr, a_scr, bbar_scr):
    c = pl.program_id(2)
    streams = range(S5_STREAMS)

    @pl.when(c == 0)
    def _():
        powers = []
        for s in streams:
            lr, li = lam_ref[s, 0:1, :], lam_ref[s, 1:2, :]
            dt = jnp.exp(lam_ref[s, 2:3, :])
            mag = jnp.exp(lr * dt)
            ar, ai = mag * jnp.cos(li * dt), mag * jnp.sin(li * dt)
            zr, zi = ar - 1.0, ai
            den = lr * lr + li * li
            fr, fi = (zr * lr + zi * li) / den, (zi * lr - zr * li) / den
            a_scr[s, 0:1, :] = ar
            a_scr[s, 1:2, :] = ai
            b_re, b_im = bmat_ref[s, 0], bmat_ref[s, 1]
            bbar_scr[s, 0] = (fr * b_re - fi * b_im).astype(BF16)
            bbar_scr[s, 1] = (fr * b_im + fi * b_re).astype(BF16)
            powers += [jnp.broadcast_to(ar, (8, S5_STATES)), jnp.broadcast_to(ai, (8, S5_STATES))]
        carry_scr[...] = jnp.zeros(carry_scr.shape, F32)
        base = tuple(powers)

        def pw_body(i, pw):
            nxt = []
            for s in streams:
                pr, pi = pw[2 * s], pw[2 * s + 1]
                pr_scr[s, i] = pr
                pi_scr[s, i] = pi
                nxt += [base[2 * s] * pr - base[2 * s + 1] * pi, base[2 * s] * pi + base[2 * s + 1] * pr]
            return tuple(nxt)

        lax.fori_loop(0, S5_SUB, pw_body, base)

    a_re = [jnp.broadcast_to(a_scr[s, 0:1, :], (8, S5_STATES)) for s in streams]
    a_im = [jnp.broadcast_to(a_scr[s, 1:2, :], (8, S5_STATES)) for s in streams]
    perm = perm_ref[...]
    u = [u_ref[0, :, s * 128:(s + 1) * 128] for s in streams]
    for s in streams:
        u_p = jnp.dot(perm, u[s].astype(BF16), preferred_element_type=F32).astype(BF16)
        xr_scr[s] = jnp.dot(u_p, bbar_scr[s, 0], preferred_element_type=F32)
        xi_scr[s] = jnp.dot(u_p, bbar_scr[s, 1], preferred_element_type=F32)

    def scan_body(i, x):
        row = pl.multiple_of(i * 8, 8)
        out = []
        for s in streams:
            xr, xi = _cmul_add(a_re[s], a_im[s], x[2 * s], x[2 * s + 1],
                               xr_scr[s, pl.ds(row, 8), :], xi_scr[s, pl.ds(row, 8), :])
            xr_scr[s, pl.ds(row, 8), :] = xr
            xi_scr[s, pl.ds(row, 8), :] = xi
            out += [xr, xi]
        return tuple(out)

    zero = jnp.zeros((8, S5_STATES), F32)
    ends = lax.fori_loop(0, S5_SUB, scan_body, (zero,) * (2 * S5_STREAMS), unroll=4)

    cr, ci = [], []
    for s in streams:
        er, ei = ends[2 * s], ends[2 * s + 1]
        ar_s = pr_scr[s, S5_SUB - 1][0:1]
        ai_s = pi_scr[s, S5_SUB - 1][0:1]
        rows_r = [carry_scr[s, 0:1, :]]
        rows_i = [carry_scr[s, 1:2, :]]
        for j in range(8):
            nr, ni = _cmul_add(ar_s, ai_s, rows_r[-1], rows_i[-1], er[j:j + 1], ei[j:j + 1])
            rows_r.append(nr)
            rows_i.append(ni)
        carry_scr[s, 0:1, :] = rows_r[8]
        carry_scr[s, 1:2, :] = rows_i[8]
        cr.append(jnp.concatenate(rows_r[:8], axis=0))
        ci.append(jnp.concatenate(rows_i[:8], axis=0))

    def fix_body(i, carry):
        row = pl.multiple_of(i * 8, 8)
        for s in streams:
            xr, xi = _cmul_add(pr_scr[s, i], pi_scr[s, i], cr[s], ci[s],
                               xr_scr[s, pl.ds(row, 8), :], xi_scr[s, pl.ds(row, 8), :])
            xr_scr[s, pl.ds(row, 8), :] = xr
            xi_scr[s, pl.ds(row, 8), :] = xi
        return carry

    lax.fori_loop(0, S5_SUB, fix_body, 0, unroll=4)

    perm_t = permt_ref[...]
    for s in streams:
        y_p = (jnp.dot(xr_scr[s].astype(BF16), cmat_ref[s, 0], preferred_element_type=F32)
               - jnp.dot(xi_scr[s].astype(BF16), cmat_ref[s, 1], preferred_element_type=F32))
        y_hi = y_p.astype(BF16)
        y_lo = (y_p - y_hi.astype(F32)).astype(BF16)
        y = jnp.dot(perm_t, y_hi, preferred_element_type=F32) + jnp.dot(perm_t, y_lo, preferred_element_type=F32)
        o_ref[0, :, s * 128:(s + 1) * 128] = jax.nn.gelu(y + d_ref[s] * u[s])


def s5_scan(u, lam_re, lam_im, log_dt, b_re, b_im, c_re, c_im, d_skip):
    bsz, seq_len, _ = u.shape
    nb = SSM_GROUPS // S5_GROUPS_PER_BLOCK
    g = S5_STREAMS
    eye = jnp.eye(S5_GROUPS_PER_BLOCK, dtype=F32)

    def blockdiag_b(m):
        m = jnp.swapaxes(m, 1, 2).reshape(nb, S5_GROUPS_PER_BLOCK, SSM_CH_PER_GROUP, SSM_STATE)
        return jnp.einsum('nghp,gk->nghkp', m, eye).reshape(nb, 128, S5_STATES)

    def blockdiag_c(m):
        m = jnp.swapaxes(m, 1, 2).reshape(nb, S5_GROUPS_PER_BLOCK, SSM_STATE, SSM_CH_PER_GROUP)
        return jnp.einsum('ngph,gk->ngpkh', m, eye).reshape(nb, S5_STATES, 128)

    log_dt_states = jnp.broadcast_to(log_dt[:, None], lam_re.shape)
    lam = jnp.stack([m.reshape(nb, S5_STATES) for m in (lam_re, lam_im, log_dt_states)], axis=1)
    bmat = jnp.stack([blockdiag_b(b_re), blockdiag_b(b_im)], axis=1)
    cmat = jnp.stack([blockdiag_c(c_re), blockdiag_c(c_im)], axis=1).astype(BF16)
    d = d_skip.reshape(nb, 1, 128)
    r = np.arange(S5_CHUNK)
    perm = np.zeros((S5_CHUNK, S5_CHUNK), np.float32)
    perm[r, (r % 8) * S5_SUB + r // 8] = 1.0
    perm = jnp.asarray(perm, BF16)
    return pl.pallas_call(
        _s5_kernel,
        grid=(bsz, nb // g, seq_len // S5_CHUNK),
        in_specs=[
            pl.BlockSpec((1, S5_CHUNK, 128 * g), lambda b, k, c: (b, c, k)),
            pl.BlockSpec((g, 3, S5_STATES), lambda b, k, c: (k, 0, 0)),
            pl.BlockSpec((g, 2, 128, S5_STATES), lambda b, k, c: (k, 0, 0, 0)),
            pl.BlockSpec((g, 2, S5_STATES, 128), lambda b, k, c: (k, 0, 0, 0)),
            pl.BlockSpec((g, 1, 128), lambda b, k, c: (k, 0, 0)),
            pl.BlockSpec((S5_CHUNK, S5_CHUNK), lambda b, k, c: (0, 0)),
            pl.BlockSpec((S5_CHUNK, S5_CHUNK), lambda b, k, c: (0, 0)),
        ],
        out_specs=pl.BlockSpec((1, S5_CHUNK, 128 * g), lambda b, k, c: (b, c, k)),
        out_shape=jax.ShapeDtypeStruct((bsz, seq_len, SSM_WIDTH), F32),
        scratch_shapes=[pltpu.VMEM((g, S5_CHUNK, S5_STATES), F32), pltpu.VMEM((g, S5_CHUNK, S5_STATES), F32),
                        pltpu.VMEM((g, S5_SUB, 8, S5_STATES), F32), pltpu.VMEM((g, S5_SUB, 8, S5_STATES), F32),
                        pltpu.VMEM((g, 2, S5_STATES), F32), pltpu.VMEM((g, 2, S5_STATES), F32),
                        pltpu.VMEM((g, 2, 128, S5_STATES), BF16)],
        compiler_params=pltpu.CompilerParams(
            dimension_semantics=("arbitrary", "arbitrary", "arbitrary"), vmem_limit_bytes=V7X_VMEM_LIMIT_BYTES),
        name="s5_scan",
    )(u, lam, bmat, cmat, d, perm, perm.T)


def _softmax_tile(s, m_old):
    m_new = jnp.maximum(m_old, jnp.max(s, axis=1, keepdims=True))
    m_wide = jnp.concatenate([m_new] * (s.shape[1] // 128), axis=1)
    return m_new, jnp.exp2(m_old - m_new), jnp.exp2(s - m_wide)


def _lane_is_low(shape):
    return lax.broadcasted_iota(jnp.int32, shape, 1) < HEAD_DIM


def _pad_kt(kt, variant):
    z = jnp.zeros_like(kt)
    return jnp.concatenate([kt, z] if variant == 0 else [z, kt], axis=0)


def _pad_v(vv, variant):
    low = _lane_is_low(vv.shape)
    keep = low if variant == 0 else jnp.logical_not(low)
    return jnp.where(keep, vv, jnp.ones_like(vv))


def _finish(acc, variant):
    lane = lax.broadcasted_iota(jnp.int32, acc.shape, 1)
    lsel = lane == (HEAD_DIM if variant == 0 else 0)
    l = jnp.sum(jnp.where(lsel, acc, 0.0), axis=1, keepdims=True)
    keep = (lane < HEAD_DIM) if variant == 0 else (lane >= HEAD_DIM)
    return jnp.where(keep, acc / l, 0.0)


def _nsa_kernel(q_ref, g_ref, kct_ref, vc_ref, kst_ref, vs_ref, kwt_ref, vw_ref, ovl_ref, gx_ref, o_ref,
                m_scr, acc_scr, s_scr_a, s_scr_b, p_scr, *, seq_len):
    s_slots = (s_scr_a, s_scr_b)
    n_sel = seq_len // SEL_BLOCK
    n_cpad = seq_len // CMP_STRIDE
    sel_tile = PROJ_TILE
    blocks_per_tile = sel_tile // SEL_BLOCK
    win_tiles = WINDOW // Q_BLOCK + 1
    n_pair = Q_PER_KV // 2
    rows = n_pair * Q_BLOCK
    i = pl.program_id(2)
    t0 = i * Q_BLOCK

    qb = q_ref[0]
    qst = jnp.concatenate([qb[:, p * 128:(p + 1) * 128] for p in range(n_pair)], axis=0)

    sig = jax.nn.sigmoid(g_ref[0])
    sig_hi = sig.astype(BF16)
    sig_lo = (sig - sig_hi.astype(F32)).astype(BF16)
    gx = gx_ref[0]
    gexp = (jnp.dot(sig_hi, gx, preferred_element_type=F32) + jnp.dot(sig_lo, gx, preferred_element_type=F32))

    def gate_of(branch):
        base = branch * n_pair * 128
        return jnp.concatenate([gexp[:, base + p * 128: base + (p + 1) * 128] for p in range(n_pair)], axis=0)

    t_row = t0 + lax.broadcasted_iota(jnp.int32, (Q_BLOCK, 1), 0)

    slab = 64
    kct = kct_ref[0, 0]
    s_cmp = [jnp.dot(qst, _pad_kt(kct, v), preferred_element_type=F32) for v in range(2)]
    n_kblk = seq_len // Q_BLOCK
    w0 = jnp.clip(i - (win_tiles - 1), 0, n_kblk - win_tiles)
    kw = jnp.concatenate([kwt_ref[0, 0, w0 + j] for j in range(win_tiles)], axis=1)
    s_win = [jnp.dot(qst, _pad_kt(kw, v), preferred_element_type=F32) for v in range(2)]
    for v in range(2):
        s_slots[0][v] = jnp.dot(qst, _pad_kt(kst_ref[0, 0, 0], v), preferred_element_type=F32)

    n_iota = lax.broadcasted_iota(jnp.int32, (Q_BLOCK, n_cpad), 1)
    cmask = (n_iota * CMP_STRIDE + (CMP_BLOCK - 1)) <= t_row
    cmask4 = jnp.concatenate([cmask] * n_pair, axis=0)
    vcd = vc_ref[0, 0]
    p_sum = jnp.zeros((Q_BLOCK, n_cpad), F32)
    out = jnp.zeros((rows, 128), F32)
    o_c = jnp.zeros((rows, 128), F32)
    for v in range(2):
        s = jnp.where(cmask4, s_cmp[v], NEG)
        m = jnp.max(s, axis=1, keepdims=True)
        e = jnp.where(cmask4, jnp.exp2(s - m), 0.0)
        l = jnp.sum(e, axis=1, keepdims=True)
        p = e * (1.0 / jnp.maximum(l, 1e-30))
        for pp in range(n_pair):
            p_sum = p_sum + p[pp * Q_BLOCK:(pp + 1) * Q_BLOCK]
        low = _lane_is_low((n_cpad, 128))
        vz = jnp.where(low if v == 0 else jnp.logical_not(low), vcd, jnp.zeros_like(vcd))
        o_c = o_c + jnp.dot(p.astype(BF16), vz, preferred_element_type=F32)
    out = out + gate_of(0) * o_c

    ps_hi = p_sum.astype(BF16)
    ps_lo = (p_sum - ps_hi.astype(F32)).astype(BF16)
    ovl = ovl_ref[...]
    nt = (((1,), (1,)), ((), ()))
    imp_t = (lax.dot_general(ovl, ps_hi, nt, preferred_element_type=F32)
             + lax.dot_general(ovl, ps_lo, nt, preferred_element_type=F32))

    vw = jnp.concatenate([vw_ref[0, 0, w0 + j] for j in range(win_tiles)], axis=0)
    kpos_w = w0 * Q_BLOCK + lax.broadcasted_iota(jnp.int32, (Q_BLOCK, win_tiles * Q_BLOCK), 1)
    diff = t_row - kpos_w
    wbias = jnp.where((diff >= 0) & (diff < WINDOW), 0.0, NEG)
    wbias4 = jnp.concatenate([wbias] * n_pair, axis=0)
    o_w = jnp.zeros((rows, 128), F32)
    for v in range(2):
        s = s_win[v] + wbias4
        m = jnp.max(s, axis=1, keepdims=True)
        p = jnp.exp2(s - m)
        o_w = o_w + _finish(jnp.dot(p.astype(BF16), _pad_v(vw, v), preferred_element_type=F32), v)
    out = out + gate_of(2) * o_w

    s_iota = lax.broadcasted_iota(jnp.int32, (n_sel, Q_BLOCK), 0)
    t_lane = t0 + lax.broadcasted_iota(jnp.int32, (n_sel, Q_BLOCK), 1)
    cur = t_lane // SEL_BLOCK
    forced = (s_iota == 0) | (s_iota == cur) | (s_iota == cur - 1)
    valid = s_iota * SEL_BLOCK <= t_lane
    score = jnp.where(forced, FORCE, jnp.where(valid, imp_t, -1.0))
    s_f = s_iota.astype(F32)
    sel_t = jnp.zeros((n_sel, Q_BLOCK), F32)
    for _ in range(min(SEL_TOPK, n_sel)):
        mx = jnp.max(score, axis=0, keepdims=True)
        idx = jnp.min(jnp.where(score == mx, s_f, float(n_sel)), axis=0, keepdims=True)
        hit = s_f == idx
        sel_t = jnp.where(hit, 1.0, sel_t)
        score = jnp.where(hit, -3e38, score)
    selmask = sel_t.T.astype(BF16)

    m_scr[...] = jnp.full(m_scr.shape, NEG, F32)
    acc_scr[...] = jnp.zeros(acc_scr.shape, F32)
    n_tiles = (t0 + Q_BLOCK + sel_tile - 1) // sel_tile

    last_tile = seq_len // sel_tile - 1

    def bias_of(kt):
        blk = kt * blocks_per_tile + lax.broadcasted_iota(jnp.int32, (n_sel, sel_tile), 1) // SEL_BLOCK
        expand = (lax.broadcasted_iota(jnp.int32, (n_sel, sel_tile), 0) == blk).astype(BF16)
        selexp = jnp.dot(selmask, expand, preferred_element_type=F32)
        kpos = kt * sel_tile + lax.broadcasted_iota(jnp.int32, (Q_BLOCK, sel_tile), 1)
        bias = jnp.where((selexp > 0.5) & (kpos <= t_row), 0.0, NEG)
        return jnp.concatenate([bias] * n_pair, axis=0)

    def scores_into(slot, kt):
        bias4 = bias_of(kt)
        kt_tile = kst_ref[0, 0, jnp.minimum(kt, last_tile)]
        for v in range(2):
            s_slots[slot][v] = jnp.dot(qst, _pad_kt(kt_tile, v), preferred_element_type=F32) + bias4

    def attend_from(slot, kt):
        v_tile = vs_ref[0, 0, jnp.minimum(kt, last_tile)]
        for v in range(2):
            for h in range(rows // slab):
                r = slice(h * slab, (h + 1) * slab)
                m_new, alpha, p = _softmax_tile(s_slots[slot][v, r, :], m_scr[v, r, :])
                m_scr[v, r, :] = m_new
                acc_scr[v, r, :] = alpha * acc_scr[v, r, :]
                p_scr[v, r, :] = p.astype(BF16)
            acc_scr[v] += jnp.dot(p_scr[v], _pad_v(v_tile, v), preferred_element_type=F32)

    bias_first = bias_of(0)
    for v in range(2):
        s_slots[0][v] = s_slots[0][v] + bias_first

    def sel_body(j, carry):
        kt = 2 * j
        scores_into(1, kt + 1)
        attend_from(0, kt)
        scores_into(0, kt + 2)
        attend_from(1, kt + 1)
        return carry

    lax.fori_loop(0, (n_tiles + 1) // 2, sel_body, 0)
    out = out + gate_of(1) * (_finish(acc_scr[0], 0) + _finish(acc_scr[1], 1))

    o_ref[0] = jnp.concatenate([out[p * Q_BLOCK:(p + 1) * Q_BLOCK] for p in range(n_pair)], axis=1)


def nsa_attention(q, gate_pad, kct, vc, kst, vs, kwt, vw):
    bsz, seq_len, _ = q.shape
    n_sel = seq_len // SEL_BLOCK
    n_cpad = seq_len // CMP_STRIDE
    n_cmp = (seq_len - CMP_BLOCK) // CMP_STRIDE + 1
    n_pair = Q_PER_KV // 2
    cs = np.arange(n_cpad) * CMP_STRIDE
    ce = cs + CMP_BLOCK - 1
    ss = np.arange(n_sel) * SEL_BLOCK
    se = ss + SEL_BLOCK - 1
    ovl = (cs[None, :] <= se[:, None]) & (ce[None, :] >= ss[:, None]) & (np.arange(n_cpad)[None, :] < n_cmp)
    ovl = jnp.asarray(ovl.astype(np.float32), BF16)
    gx = np.zeros((NSA_KV_HEADS, 128, N_BRANCH * n_pair * 128), np.float32)
    for k in range(NSA_KV_HEADS):
        for hl in range(Q_PER_KV):
            for br in range(N_BRANCH):
                c0 = br * n_pair * 128 + hl * HEAD_DIM
                gx[k, (k * Q_PER_KV + hl) * N_BRANCH + br, c0:c0 + HEAD_DIM] = 1.0
    gx = jnp.asarray(gx, BF16)
    width = Q_PER_KV * HEAD_DIM
    full = lambda *shape: pl.BlockSpec((1, 1) + shape, lambda b, k, i: (b, k) + (0,) * len(shape))
    return pl.pallas_call(
        functools.partial(_nsa_kernel, seq_len=seq_len),
        grid=(bsz, NSA_KV_HEADS, seq_len // Q_BLOCK),
        in_specs=[
            pl.BlockSpec((1, Q_BLOCK, width), lambda b, k, i: (b, i, k)),
            pl.BlockSpec((1, Q_BLOCK, 128), lambda b, k, i: (b, i, 0)),
            full(HEAD_DIM, n_cpad), full(n_cpad, 128),
            full(seq_len // PROJ_TILE, HEAD_DIM, PROJ_TILE), full(seq_len // PROJ_TILE, PROJ_TILE, 128),
            full(seq_len // Q_BLOCK, HEAD_DIM, Q_BLOCK), full(seq_len // Q_BLOCK, Q_BLOCK, 128),
            pl.BlockSpec((n_sel, n_cpad), lambda b, k, i: (0, 0)),
            pl.BlockSpec((1, 128, N_BRANCH * n_pair * 128), lambda b, k, i: (k, 0, 0)),
        ],
        out_specs=pl.BlockSpec((1, Q_BLOCK, width), lambda b, k, i: (b, i, k)),
        out_shape=jax.ShapeDtypeStruct((bsz, seq_len, NSA_WIDTH), F32),
        scratch_shapes=[pltpu.VMEM((2, n_pair * Q_BLOCK, 128), F32), pltpu.VMEM((2, n_pair * Q_BLOCK, 128), F32),
                        pltpu.VMEM((2, n_pair * Q_BLOCK, PROJ_TILE), F32),
                        pltpu.VMEM((2, n_pair * Q_BLOCK, PROJ_TILE), F32),
                        pltpu.VMEM((2, n_pair * Q_BLOCK, PROJ_TILE), BF16)],
        compiler_params=pltpu.CompilerParams(
            dimension_semantics=("arbitrary", "arbitrary", "arbitrary"), vmem_limit_bytes=V7X_VMEM_LIMIT_BYTES),
        name="nsa_attention",
    )(q, gate_pad, kct, vc, kst, vs, kwt, vw, ovl, gx)


MOE_TILE = 1024
MOE_SUB = 256
MOE_ROWS = 48
MOE_SLOT = 64
MOE_GROUP = 4
MOE_GATHER_GROUP = 4


def _first_max_mask(x, idx_f, axis):
    mx = jnp.max(x, axis=axis, keepdims=True)
    first = jnp.min(jnp.where(x == mx, idx_f, 1e9), axis=axis, keepdims=True)
    return idx_f == first, mx


def _route(logits, bias):
    per_group = N_EXPERTS // N_EXPERT_GROUPS
    tr = logits.shape[1]
    aff = jax.nn.sigmoid(logits)
    biased = aff + bias
    grp = biased.reshape(N_EXPERT_GROUPS, per_group, tr)
    in_grp = lax.broadcasted_iota(jnp.int32, grp.shape, 1).astype(F32)
    hit1, m1 = _first_max_mask(grp, in_grp, 1)
    m2 = jnp.max(jnp.where(hit1, -jnp.inf, grp), axis=1, keepdims=True)
    gscore = (m1 + m2).reshape(N_EXPERT_GROUPS, tr)
    g_idx = lax.broadcasted_iota(jnp.int32, gscore.shape, 0).astype(F32)
    gsel = jnp.zeros(gscore.shape, F32)
    for _ in range(TOPK_GROUPS):
        hit, _ = _first_max_mask(gscore, g_idx, 0)
        gsel = jnp.where(hit, 1.0, gsel)
        gscore = jnp.where(hit, -jnp.inf, gscore)
    gmask = jnp.broadcast_to(gsel.reshape(N_EXPERT_GROUPS, 1, tr), grp.shape).reshape(N_EXPERTS, tr)
    cand = jnp.where(gmask > 0.5, biased, NEG)
    e_idx = lax.broadcasted_iota(jnp.int32, cand.shape, 0).astype(F32)
    sel = jnp.zeros(cand.shape, F32)
    for _ in range(TOP_K):
        hit, _ = _first_max_mask(cand, e_idx, 0)
        sel = jnp.where(hit, 1.0, sel)
        cand = jnp.where(hit, -jnp.inf, cand)
    w = jnp.where(sel > 0.5, aff, 0.0)
    return w / jnp.sum(w, axis=0, keepdims=True) * ROUTED_SCALE, sel


def _moe_kernel(cnt_ref, x_ref, sel_ref, w_ref, init_ref, wg_ref, wu_ref, wd_ref, lng_ref, lnb_ref, o_ref,
                rank_scr, ybuf_scr, sbuf_scr, xe_scr):
    i = pl.program_id(0)
    e = pl.program_id(1)
    tm = x_ref.shape[0]
    n_sub = tm // MOE_SUB
    tn = (((0,), (0,)), ((), ()))

    @pl.when(e == 0)
    def _():
        o_ref[...] = init_ref[...]
        before = (lax.broadcasted_iota(jnp.int32, (MOE_SUB, MOE_SUB), 0)
                  < lax.broadcasted_iota(jnp.int32, (MOE_SUB, MOE_SUB), 1))
        before = jnp.where(before, 1.0, 0.0).astype(BF16)
        for q in range(n_sub):
            cols = slice(q * MOE_SUB, (q + 1) * MOE_SUB)
            rank_scr[:, cols] = jnp.dot(sel_ref[:, cols].astype(BF16), before, preferred_element_type=F32)

    count = cnt_ref[i * N_EXPERTS + e]
    sel_e = sel_ref[pl.ds(e, 1), :]
    rank_e = rank_scr[pl.ds(e, 1), :]
    w_e = w_ref[pl.ds(e, 1), :]

    def one_hots(rank_row, sel_row, c):
        row = (c * MOE_ROWS + lax.broadcasted_iota(jnp.int32, (MOE_ROWS, MOE_SUB), 0)).astype(F32)
        hits = []
        for q in range(n_sub):
            cols = slice(q * MOE_SUB, (q + 1) * MOE_SUB)
            hits.append((rank_row[:, cols] == row) & (sel_row[:, cols] > 0.5))
        return hits

    def swiglu(xe):
        g = jnp.dot(xe, wg_ref[0], preferred_element_type=F32)
        u = jnp.dot(xe, wu_ref[0], preferred_element_type=F32)
        h = (jax.nn.silu(g) * u).astype(BF16)
        return jnp.dot(h, wd_ref[0], preferred_element_type=F32).astype(BF16)

    def weighted(hits):
        return [jnp.where(hits[q], w_e[:, q * MOE_SUB:(q + 1) * MOE_SUB], 0.0).astype(BF16) for q in range(n_sub)]

    slot = e % MOE_GROUP

    gslot = e % MOE_GATHER_GROUP

    @pl.when(gslot == 0)
    def _():
        stacks = [[] for _ in range(n_sub)]
        for gi in range(MOE_GATHER_GROUP):
            hits = one_hots(rank_scr[pl.ds(e + gi, 1), :], sel_ref[pl.ds(e + gi, 1), :], 0)
            for q in range(n_sub):
                stacks[q].append(jnp.where(hits[q], 1.0, 0.0).astype(BF16))
        for q in range(n_sub):
            cols = slice(q * MOE_SUB, (q + 1) * MOE_SUB)
            xg = jnp.dot(jnp.concatenate(stacks[q], axis=0), x_ref[cols, :],
                         preferred_element_type=F32).astype(BF16)
            for gi in range(MOE_GATHER_GROUP):
                xe_scr[gi, q] = xg[gi * MOE_ROWS:(gi + 1) * MOE_ROWS]

    scatters = weighted(one_hots(rank_e, sel_e, 0))
    y = swiglu(xe_scr[gslot].reshape(n_sub * MOE_ROWS, D_MODEL))
    spare = MOE_SLOT - MOE_ROWS
    for q in range(n_sub):
        sbuf_scr[q, slot] = jnp.concatenate([scatters[q], jnp.zeros((spare, MOE_SUB), BF16)], axis=0)
        ybuf_scr[q, slot] = jnp.concatenate(
            [y[q * MOE_ROWS:(q + 1) * MOE_ROWS], jnp.zeros((spare, D_MODEL), BF16)], axis=0)

    @pl.when(slot == MOE_GROUP - 1)
    def _():
        for q in range(n_sub):
            cols = slice(q * MOE_SUB, (q + 1) * MOE_SUB)
            o_ref[cols, :] += lax.dot_general(sbuf_scr[q].reshape(MOE_GROUP * MOE_SLOT, MOE_SUB),
                                              ybuf_scr[q].reshape(MOE_GROUP * MOE_SLOT, D_MODEL), tn,
                                              preferred_element_type=F32)

    def overflow_body(c, carry):
        hits = one_hots(rank_e, sel_e, c)
        sc = weighted(hits)
        xe = jnp.concatenate(
            [jnp.dot(jnp.where(hits[q], 1.0, 0.0).astype(BF16), x_ref[q * MOE_SUB:(q + 1) * MOE_SUB, :],
                     preferred_element_type=F32).astype(BF16) for q in range(n_sub)], axis=0)
        yy = swiglu(xe)
        for q in range(n_sub):
            cols = slice(q * MOE_SUB, (q + 1) * MOE_SUB)
            o_ref[cols, :] += lax.dot_general(sc[q], yy[q * MOE_ROWS:(q + 1) * MOE_ROWS], tn,
                                              preferred_element_type=F32)
        return carry

    lax.fori_loop(1, (count + MOE_ROWS - 1) // MOE_ROWS, overflow_body, 0)

    @pl.when(e == N_EXPERTS - 1)
    def _():
        o_ref[...] = _layer_norm(o_ref[...], lng_ref[...], lnb_ref[...])


def moe_routed(x_bf16, sel_t, w_t, init, w_gate, w_up, w_down, ln_g, ln_b):
    n_tok = x_bf16.shape[0]
    n_tiles = n_tok // MOE_TILE
    per_sub = jnp.sum(sel_t.reshape(N_EXPERTS, n_tiles, MOE_TILE // MOE_SUB, MOE_SUB), axis=-1)
    cnt = jnp.max(per_sub, axis=-1).T.astype(jnp.int32).reshape(-1)
    grid_spec = pltpu.PrefetchScalarGridSpec(
        num_scalar_prefetch=1,
        grid=(n_tiles, N_EXPERTS),
        in_specs=[
            pl.BlockSpec((MOE_TILE, D_MODEL), lambda i, e, cnt: (i, 0), pipeline_mode=pl.Buffered(1)),
            pl.BlockSpec((N_EXPERTS, MOE_TILE), lambda i, e, cnt: (0, i)),
            pl.BlockSpec((N_EXPERTS, MOE_TILE), lambda i, e, cnt: (0, i)),
            pl.BlockSpec((MOE_TILE, D_MODEL), lambda i, e, cnt: (i, 0), pipeline_mode=pl.Buffered(1)),
            pl.BlockSpec((1, D_MODEL, EXPERT_FF), lambda i, e, cnt: (e, 0, 0)),
            pl.BlockSpec((1, D_MODEL, EXPERT_FF), lambda i, e, cnt: (e, 0, 0)),
            pl.BlockSpec((1, EXPERT_FF, D_MODEL), lambda i, e, cnt: (e, 0, 0)),
            pl.BlockSpec((1, D_MODEL), lambda i, e, cnt: (0, 0)),
            pl.BlockSpec((1, D_MODEL), lambda i, e, cnt: (0, 0)),
        ],
        out_specs=pl.BlockSpec((MOE_TILE, D_MODEL), lambda i, e, cnt: (i, 0)),
        scratch_shapes=[pltpu.VMEM((N_EXPERTS, MOE_TILE), F32),
                        pltpu.VMEM((MOE_TILE // MOE_SUB, MOE_GROUP, MOE_SLOT, D_MODEL), BF16),
                        pltpu.VMEM((MOE_TILE // MOE_SUB, MOE_GROUP, MOE_SLOT, MOE_SUB), BF16),
                        pltpu.VMEM((MOE_GATHER_GROUP, MOE_TILE // MOE_SUB, MOE_ROWS, D_MODEL), BF16)],
    )
    return pl.pallas_call(
        _moe_kernel,
        grid_spec=grid_spec,
        out_shape=jax.ShapeDtypeStruct((n_tok, D_MODEL), F32),
        compiler_params=pltpu.CompilerParams(dimension_semantics=("arbitrary", "arbitrary"),
                                             vmem_limit_bytes=V7X_VMEM_LIMIT_BYTES),
        name="moe_routed",
    )(cnt, x_bf16, sel_t, w_t, init, w_gate, w_up, w_down, ln_g.reshape(1, D_MODEL), ln_b.reshape(1, D_MODEL))


def _post_mix_kernel(y_ref, a_ref, x_ref, wglu_ref, wout_ref, g_ref, b_ref, wrt_ref, rbias_ref,
                     wsg_ref, wsu_ref, wsd_ref, acc_ref, xb_ref, w_ref, sel_ref):
    y = y_ref[...]
    y_ssm = y * jax.nn.sigmoid(jnp.dot(y.astype(BF16), wglu_ref[...], preferred_element_type=F32))
    mix = (jnp.dot(y_ssm.astype(BF16), wout_ref[:SSM_WIDTH, :], preferred_element_type=F32)
           + jnp.dot(a_ref[...].astype(BF16), wout_ref[SSM_WIDTH:, :], preferred_element_type=F32))
    x1 = _layer_norm(DEEPNORM_ALPHA * x_ref[...] + mix, g_ref[...], b_ref[...])
    xb = x1.astype(BF16)
    xb_ref[...] = xb
    nt = (((1,), (1,)), ((), ()))
    w, sel = _route(lax.dot_general(wrt_ref[...], xb, nt, preferred_element_type=F32), rbias_ref[...])
    w_ref[...] = w
    sel_ref[...] = sel
    h = jax.nn.silu(jnp.dot(xb, wsg_ref[...], preferred_element_type=F32)) * jnp.dot(
        xb, wsu_ref[...], preferred_element_type=F32)
    acc_ref[...] = DEEPNORM_ALPHA * x1 + jnp.dot(h.astype(BF16), wsd_ref[...], preferred_element_type=F32)


def post_mix(y_s5, y_nsa, xt, w_glu, w_out, ln_g, ln_b, w_router, router_bias, wsg, wsu, wsd, tm=512):
    n_tok = xt.shape[0]
    row = lambda i: (i, 0)
    const = lambda i: (0, 0)
    once = pl.Buffered(1)
    wspec = lambda r, c: pl.BlockSpec((r, c), const, pipeline_mode=once)
    return pl.pallas_call(
        _post_mix_kernel,
        grid=(n_tok // tm,),
        in_specs=[pl.BlockSpec((tm, SSM_WIDTH), row), pl.BlockSpec((tm, NSA_WIDTH), row),
                  pl.BlockSpec((tm, D_MODEL), row),
                  wspec(SSM_WIDTH, SSM_WIDTH), wspec(D_MODEL, D_MODEL),
                  pl.BlockSpec((1, D_MODEL), const), pl.BlockSpec((1, D_MODEL), const),
                  wspec(N_EXPERTS, D_MODEL), pl.BlockSpec((N_EXPERTS, 1), const),
                  wspec(D_MODEL, EXPERT_FF), wspec(D_MODEL, EXPERT_FF), wspec(EXPERT_FF, D_MODEL)],
        out_specs=[pl.BlockSpec((tm, D_MODEL), row), pl.BlockSpec((tm, D_MODEL), row),
                   pl.BlockSpec((N_EXPERTS, tm), lambda i: (0, i)), pl.BlockSpec((N_EXPERTS, tm), lambda i: (0, i))],
        out_shape=[jax.ShapeDtypeStruct((n_tok, D_MODEL), F32), jax.ShapeDtypeStruct((n_tok, D_MODEL), BF16),
                   jax.ShapeDtypeStruct((N_EXPERTS, n_tok), F32), jax.ShapeDtypeStruct((N_EXPERTS, n_tok), F32)],
        compiler_params=pltpu.CompilerParams(dimension_semantics=("arbitrary",),
                                             vmem_limit_bytes=V7X_VMEM_LIMIT_BYTES),
        name="post_mix",
    )(y_s5, y_nsa, xt, w_glu.astype(BF16), w_out.astype(BF16), ln_g.reshape(1, D_MODEL), ln_b.reshape(1, D_MODEL),
      w_router.T.astype(BF16), router_bias.reshape(N_EXPERTS, 1), wsg.astype(BF16), wsu.astype(BF16),
      wsd.astype(BF16))


def hybrid_layer(x, positions, w_in, lam_re, lam_im, log_dt, ssm_b_re, ssm_b_im, ssm_c_re, ssm_c_im, ssm_d,
                 w_glu, cmp_pos_k, cmp_pos_v, w_cmp_k1, w_cmp_k2, w_cmp_v1, w_cmp_v2, w_out, ln1_g, ln1_b,
                 w_router, router_bias, w_gate, w_up, w_down, ws_gate, ws_up, ws_down, ln2_g, ln2_b):
    bsz, L, _ = x.shape
    sizes = [SSM_WIDTH, NSA_WIDTH] + [KV_WIDTH] * 6 + [NSA_HEADS * N_BRANCH]
    o = [0] + [int(v) for v in np.cumsum(sizes)]
    col = lambda j: w_in[:, o[j]:o[j + 1]]
    dup = lambda w: jnp.concatenate([w[:, h * HEAD_DIM:(h + 1) * HEAD_DIM] for h in (0, 0, 1, 1)], axis=1)
    gate_cols = jnp.pad(col(8), ((0, 0), (0, 128 - NSA_HEADS * N_BRANCH)))
    w_uq = w_in[:, :o[2]].astype(BF16)
    w_kv = jnp.concatenate([col(4), col(6), dup(col(5)), dup(col(7)), col(2), col(3), gate_cols], axis=1).astype(BF16)

    xt = x.reshape(bsz * L, D_MODEL)
    u, q = proj_uq(xt, w_uq, positions.reshape(bsz * L, 1))
    kst, kwt, vs, vw, kc_raw, vc_raw, gate_pad = proj_kv(x, w_kv, positions.reshape(bsz, L, 1))
    kct, vcd = compress_kv(kc_raw, vc_raw, positions, cmp_pos_k, cmp_pos_v, w_cmp_k1, w_cmp_k2, w_cmp_v1, w_cmp_v2)
    y_s5 = s5_scan(u.reshape(bsz, L, SSM_WIDTH), lam_re, lam_im, log_dt, ssm_b_re, ssm_b_im, ssm_c_re, ssm_c_im, ssm_d)
    vw = vw.reshape(bsz, NSA_KV_HEADS, L // Q_BLOCK, Q_BLOCK, 128)
    y_nsa = nsa_attention(q.reshape(bsz, L, NSA_WIDTH), gate_pad, kct, vcd, kst, vs, kwt, vw)
    acc0, x1b, w_t, sel_t = post_mix(y_s5.reshape(bsz * L, SSM_WIDTH), y_nsa.reshape(bsz * L, NSA_WIDTH), xt,
                                     w_glu, w_out, ln1_g, ln1_b, w_router, router_bias, ws_gate, ws_up, ws_down)
    out = moe_routed(x1b, sel_t, w_t, acc0, w_gate.astype(BF16), w_up.astype(BF16), w_down.astype(BF16),
                     ln2_g, ln2_b)
    return out.reshape(bsz, L, D_MODEL)


def kernel(x, positions, w_in, lam_re, lam_im, log_dt, ssm_b_re, ssm_b_im, ssm_c_re, ssm_c_im, ssm_d, w_glu, cmp_pos_k, cmp_pos_v, w_cmp_k1, w_cmp_k2, w_cmp_v1, w_cmp_v2, w_out, ln1_g, ln1_b, w_router, router_bias, w_gate, w_up, w_down, ws_gate, ws_up, ws_down, ln2_g, ln2_b):
    params = (w_in, lam_re, lam_im, log_dt, ssm_b_re, ssm_b_im, ssm_c_re, ssm_c_im, ssm_d,
              w_glu, cmp_pos_k, cmp_pos_v, w_cmp_k1, w_cmp_k2, w_cmp_v1, w_cmp_v2, w_out, ln1_g, ln1_b,
              w_router, router_bias, w_gate, w_up, w_down, ws_gate, ws_up, ws_down, ln2_g, ln2_b)
    return hybrid_layer(x, positions, *(p[0] for p in params))
```

```python
import functools
import math

import numpy as np
import jax
import jax.numpy as jnp
from jax import lax
from jax.experimental import pallas as pl
from jax.experimental.pallas import tpu as pltpu

D_MODEL = 2048
SSM_WIDTH = 1024
SSM_CH_PER_GROUP = 16
SSM_GROUPS = 64
SSM_STATE = 64
NSA_HEADS = 16
NSA_KV_HEADS = 2
HEAD_DIM = 64
Q_PER_KV = NSA_HEADS // NSA_KV_HEADS
NSA_WIDTH = NSA_HEADS * HEAD_DIM
KV_WIDTH = NSA_KV_HEADS * HEAD_DIM
N_BRANCH = 3
CMP_BLOCK = 32
CMP_STRIDE = 16
SEL_BLOCK = 64
SEL_TOPK = 16
WINDOW = 512
Q_BLOCK = 128
ROPE_THETA = 10000.0
N_EXPERTS = 64
TOP_K = 8
N_EXPERT_GROUPS = 8
TOPK_GROUPS = 4
ROUTED_SCALE = 2.5
EXPERT_FF = 512
DEPTH = 1
DEEPNORM_ALPHA = (2.0 * DEPTH) ** 0.25
LN_EPS = 1e-5
NEG = -1e30
FORCE = 1e4
F32 = jnp.float32
BF16 = jnp.bfloat16

V7X_VMEM_LIMIT_BYTES = 56 * 1024 * 1024


def _layer_norm(x, g, b):
    mu = jnp.mean(x, -1, keepdims=True)
    var = jnp.mean(jnp.square(x - mu), -1, keepdims=True)
    return (x - mu) * lax.rsqrt(var + LN_EPS) * g + b


def _rope_tables(pos_col, inv_row):
    ang = pos_col * inv_row
    return jnp.cos(ang), jnp.sin(ang)


def _rope_lanes(x, cos, sin):
    lane = lax.broadcasted_iota(jnp.int32, (x.shape[0], 128), 1)
    first_half = (lane % HEAD_DIM) < HEAD_DIM // 2
    outs = []
    for blk in range(x.shape[1] // 128):
        xb = x[:, blk * 128:(blk + 1) * 128]
        rot = jnp.where(first_half, -pltpu.roll(xb, 128 - HEAD_DIM // 2, 1), pltpu.roll(xb, HEAD_DIM // 2, 1))
        outs.append(xb * cos + rot * sin)
    return outs[0] if len(outs) == 1 else jnp.concatenate(outs, axis=1)


def _inv_freq_row():
    half = HEAD_DIM // 2
    inv = ROPE_THETA ** (-jnp.arange(half, dtype=F32) / half)
    return jnp.tile(inv, 128 // half).reshape(1, 128)


PROJ_TILE = 512
Q_SCALE = HEAD_DIM ** -0.5 * math.log2(math.e)


def _proj_uq_kernel(x_ref, w_ref, pos_ref, inv_ref, u_ref, q_ref):
    acc = jnp.dot(x_ref[...].astype(BF16), w_ref[...], preferred_element_type=F32)
    u_ref[...] = acc[:, :SSM_WIDTH]
    cos, sin = _rope_tables(pos_ref[...].astype(F32), inv_ref[...])
    q_ref[...] = (_rope_lanes(acc[:, SSM_WIDTH:], cos, sin) * Q_SCALE).astype(BF16)


def proj_uq(xt, w_uq, pos_col, tm=1024):
    n_tok = xt.shape[0]
    return pl.pallas_call(
        _proj_uq_kernel,
        grid=(n_tok // tm,),
        in_specs=[pl.BlockSpec((tm, D_MODEL), lambda i: (i, 0)),
                  pl.BlockSpec((D_MODEL, SSM_WIDTH + NSA_WIDTH), lambda i: (0, 0), pipeline_mode=pl.Buffered(1)),
                  pl.BlockSpec((tm, 1), lambda i: (i, 0)),
                  pl.BlockSpec((1, 128), lambda i: (0, 0))],
        out_specs=[pl.BlockSpec((tm, SSM_WIDTH), lambda i: (i, 0)),
                   pl.BlockSpec((tm, NSA_WIDTH), lambda i: (i, 0))],
        out_shape=[jax.ShapeDtypeStruct((n_tok, SSM_WIDTH), F32), jax.ShapeDtypeStruct((n_tok, NSA_WIDTH), BF16)],
        compiler_params=pltpu.CompilerParams(dimension_semantics=("arbitrary",),
                                             vmem_limit_bytes=V7X_VMEM_LIMIT_BYTES),
        name="proj_uq",
    )(xt, w_uq, pos_col, _inv_freq_row())


KV_COLS = 4 * KV_WIDTH + 2 * 2 * KV_WIDTH + 128


def _proj_kv_kernel(x_ref, w_ref, pos_ref, inv_ref, kst_ref, kwt_ref, vs_ref, vw_ref, kc_ref, vc_ref, g_ref):
    acc = jnp.dot(x_ref[0].astype(BF16), w_ref[...], preferred_element_type=F32)
    cos, sin = _rope_tables(pos_ref[0].astype(F32), inv_ref[...])
    ks_t = _rope_lanes(acc[:, 0:128], cos, sin).T
    kw_t = _rope_lanes(acc[:, 128:256], cos, sin).T
    for k in range(NSA_KV_HEADS):
        kst_ref[0, k, 0] = ks_t[k * HEAD_DIM:(k + 1) * HEAD_DIM].astype(BF16)
        for j in range(PROJ_TILE // Q_BLOCK):
            kwt_ref[0, k, j] = kw_t[k * HEAD_DIM:(k + 1) * HEAD_DIM, j * Q_BLOCK:(j + 1) * Q_BLOCK].astype(BF16)
        vs_ref[0, k, 0] = acc[:, 256 + k * 128: 256 + (k + 1) * 128].astype(BF16)
        vw_ref[0, k] = acc[:, 512 + k * 128: 512 + (k + 1) * 128].astype(BF16)
    kc_ref[0] = acc[:, 768:896]
    vc_ref[0] = acc[:, 896:1024]
    g_ref[0] = acc[:, 1024:1152]


def proj_kv(x, w_kv, pos_col3):
    bsz, seq_len, _ = x.shape
    n_t = seq_len // PROJ_TILE
    per = PROJ_TILE // Q_BLOCK
    return pl.pallas_call(
        _proj_kv_kernel,
        grid=(bsz, n_t),
        in_specs=[pl.BlockSpec((1, PROJ_TILE, D_MODEL), lambda b, i: (b, i, 0)),
                  pl.BlockSpec((D_MODEL, KV_COLS), lambda b, i: (0, 0)),
                  pl.BlockSpec((1, PROJ_TILE, 1), lambda b, i: (b, i, 0)),
                  pl.BlockSpec((1, 128), lambda b, i: (0, 0))],
        out_specs=[
            pl.BlockSpec((1, NSA_KV_HEADS, 1, HEAD_DIM, PROJ_TILE), lambda b, i: (b, 0, i, 0, 0)),
            pl.BlockSpec((1, NSA_KV_HEADS, per, HEAD_DIM, Q_BLOCK), lambda b, i: (b, 0, i, 0, 0)),
            pl.BlockSpec((1, NSA_KV_HEADS, 1, PROJ_TILE, 128), lambda b, i: (b, 0, i, 0, 0)),
            pl.BlockSpec((1, NSA_KV_HEADS, PROJ_TILE, 128), lambda b, i: (b, 0, i, 0)),
            pl.BlockSpec((1, PROJ_TILE, 128), lambda b, i: (b, i, 0)),
            pl.BlockSpec((1, PROJ_TILE, 128), lambda b, i: (b, i, 0)),
            pl.BlockSpec((1, PROJ_TILE, 128), lambda b, i: (b, i, 0)),
        ],
        out_shape=[
            jax.ShapeDtypeStruct((bsz, NSA_KV_HEADS, n_t, HEAD_DIM, PROJ_TILE), BF16),
            jax.ShapeDtypeStruct((bsz, NSA_KV_HEADS, seq_len // Q_BLOCK, HEAD_DIM, Q_BLOCK), BF16),
            jax.ShapeDtypeStruct((bsz, NSA_KV_HEADS, n_t, PROJ_TILE, 128), BF16),
            jax.ShapeDtypeStruct((bsz, NSA_KV_HEADS, seq_len, 128), BF16),
            jax.ShapeDtypeStruct((bsz, seq_len, 128), F32),
            jax.ShapeDtypeStruct((bsz, seq_len, 128), F32),
            jax.ShapeDtypeStruct((bsz, seq_len, 128), F32),
        ],
        compiler_params=pltpu.CompilerParams(dimension_semantics=("arbitrary", "arbitrary"),
                                             vmem_limit_bytes=V7X_VMEM_LIMIT_BYTES),
        name="proj_kv",
    )(x, w_kv, pos_col3, _inv_freq_row())


def _compress_kernel(ck_ref, cv_ref, pek_ref, pev_ref, w1k_ref, w1v_ref, w2k_ref, w2v_ref, pos_ref, inv_ref,
                     kct_ref, vcd_ref):
    def hidden(c_ref, pe_ref, w1_ref):
        c = c_ref[0]
        lo = jnp.dot((c + pe_ref[0]).astype(BF16), w1_ref[0], preferred_element_type=F32)
        hi = jnp.dot((c + pe_ref[1]).astype(BF16), w1_ref[1], preferred_element_type=F32)
        hi_next = jnp.concatenate([hi[1:], jnp.zeros((1, hi.shape[1]), F32)], axis=0)
        return jax.nn.gelu(lo + hi_next).astype(BF16)

    kc = jnp.dot(hidden(ck_ref, pek_ref, w1k_ref), w2k_ref[...], preferred_element_type=F32)
    cos, sin = _rope_tables(pos_ref[0], inv_ref[...])
    kc_t = _rope_lanes(kc, cos, sin).T
    vc = jnp.dot(hidden(cv_ref, pev_ref, w1v_ref), w2v_ref[...], preferred_element_type=F32)
    for k in range(NSA_KV_HEADS):
        kct_ref[0, k] = kc_t[k * HEAD_DIM:(k + 1) * HEAD_DIM].astype(BF16)
        vcd_ref[0, k] = vc[:, k * 128:(k + 1) * 128].astype(BF16)


def compress_kv(kc_raw, vc_raw, positions, cmp_pos_k, cmp_pos_v, w_k1, w_k2, w_v1, w_v2):
    bsz, seq_len, _ = kc_raw.shape
    n_chunk = seq_len // CMP_STRIDE
    width = CMP_STRIDE * 128
    eye = jnp.eye(NSA_KV_HEADS, dtype=F32)

    def chunk_pe(pe):
        pe = pe.reshape(2, CMP_STRIDE, 1, HEAD_DIM)
        return jnp.broadcast_to(pe, (2, CMP_STRIDE, NSA_KV_HEADS, HEAD_DIM)).reshape(2, 1, width)

    def chunk_w1(w1):
        hid = w1.shape[1]
        w = w1.reshape(2, CMP_STRIDE, HEAD_DIM, hid)
        return jnp.einsum('htdj,kc->htkdcj', w, eye).reshape(2, width, NSA_KV_HEADS * hid).astype(BF16)

    hid = w_k2.shape[0]
    w2k = jnp.einsum('jd,kc->kjcd', w_k2, eye).reshape(NSA_KV_HEADS * hid, NSA_KV_HEADS * HEAD_DIM).astype(BF16)
    w2v = jnp.einsum('jd,kc,r->kjcrd', w_v2, eye, jnp.ones((2,), F32)).reshape(
        NSA_KV_HEADS * hid, NSA_KV_HEADS * 128).astype(BF16)
    pos = positions.astype(F32).reshape(bsz, n_chunk, CMP_STRIDE).sum(-1)
    pos_next = jnp.concatenate([pos[:, 1:], pos[:, -1:]], axis=1)
    cmp_pos = ((pos + pos_next) / CMP_BLOCK).reshape(bsz, n_chunk, 1)
    return pl.pallas_call(
        _compress_kernel,
        grid=(bsz,),
        in_specs=[pl.BlockSpec((1, n_chunk, width), lambda b: (b, 0, 0)),
                  pl.BlockSpec((1, n_chunk, width), lambda b: (b, 0, 0)),
                  pl.BlockSpec((2, 1, width), lambda b: (0, 0, 0)),
                  pl.BlockSpec((2, 1, width), lambda b: (0, 0, 0)),
                  pl.BlockSpec((2, width, NSA_KV_HEADS * hid), lambda b: (0, 0, 0)),
                  pl.BlockSpec((2, width, NSA_KV_HEADS * hid), lambda b: (0, 0, 0)),
                  pl.BlockSpec((NSA_KV_HEADS * hid, NSA_KV_HEADS * HEAD_DIM), lambda b: (0, 0)),
                  pl.BlockSpec((NSA_KV_HEADS * hid, NSA_KV_HEADS * 128), lambda b: (0, 0)),
                  pl.BlockSpec((1, n_chunk, 1), lambda b: (b, 0, 0)),
                  pl.BlockSpec((1, 128), lambda b: (0, 0))],
        out_specs=[pl.BlockSpec((1, NSA_KV_HEADS, HEAD_DIM, n_chunk), lambda b: (b, 0, 0, 0)),
                   pl.BlockSpec((1, NSA_KV_HEADS, n_chunk, 128), lambda b: (b, 0, 0, 0))],
        out_shape=[jax.ShapeDtypeStruct((bsz, NSA_KV_HEADS, HEAD_DIM, n_chunk), BF16),
                   jax.ShapeDtypeStruct((bsz, NSA_KV_HEADS, n_chunk, 128), BF16)],
        compiler_params=pltpu.CompilerParams(dimension_semantics=("arbitrary",),
                                             vmem_limit_bytes=V7X_VMEM_LIMIT_BYTES),
        name="compress_kv",
    )(kc_raw.reshape(bsz, n_chunk, width), vc_raw.reshape(bsz, n_chunk, width), chunk_pe(cmp_pos_k),
      chunk_pe(cmp_pos_v), chunk_w1(w_k1), chunk_w1(w_v1), w2k, w2v, cmp_pos, _inv_freq_row())


S5_CHUNK = 512
S5_SUB = S5_CHUNK // 8
S5_GROUPS_PER_BLOCK = 8
S5_STATES = S5_GROUPS_PER_BLOCK * SSM_STATE
S5_STREAMS = 4


def _cmul_add(ar, ai, xr, xi, br, bi):
    return ar * xr - ai * xi + br, ar * xi + ai * xr + bi


def _s5_kernel(u_ref, lam_ref, bmat_ref, cmat_ref, d_ref, perm_ref, permt_ref, o_ref,
               xr_scr, xi_scr, pr_scr, pi_scr, carry_scr, a_scr, bbar_scr):
    c = pl.program_id(2)
    streams = range(S5_STREAMS)

    @pl.when(c == 0)
    def _():
        powers = []
        for s in streams:
            lr, li = lam_ref[s, 0:1, :], lam_ref[s, 1:2, :]
            dt = jnp.exp(lam_ref[s, 2:3, :])
            mag = jnp.exp(lr * dt)
            ar, ai = mag * jnp.cos(li * dt), mag * jnp.sin(li * dt)
            zr, zi = ar - 1.0, ai
            den = lr * lr + li * li
            fr, fi = (zr * lr + zi * li) / den, (zi * lr - zr * li) / den
            a_scr[s, 0:1, :] = ar
            a_scr[s, 1:2, :] = ai
            b_re, b_im = bmat_ref[s, 0], bmat_ref[s, 1]
            bbar_scr[s, 0] = (fr * b_re - fi * b_im).astype(BF16)
            bbar_scr[s, 1] = (fr * b_im + fi * b_re).astype(BF16)
            powers += [jnp.broadcast_to(ar, (8, S5_STATES)), jnp.broadcast_to(ai, (8, S5_STATES))]
        carry_scr[...] = jnp.zeros(carry_scr.shape, F32)
        base = tuple(powers)

        def pw_body(i, pw):
            nxt = []
            for s in streams:
                pr, pi = pw[2 * s], pw[2 * s + 1]
                pr_scr[s, i] = pr
                pi_scr[s, i] = pi
                nxt += [base[2 * s] * pr - base[2 * s + 1] * pi, base[2 * s] * pi + base[2 * s + 1] * pr]
            return tuple(nxt)

        lax.fori_loop(0, S5_SUB, pw_body, base)

    a_re = [jnp.broadcast_to(a_scr[s, 0:1, :], (8, S5_STATES)) for s in streams]
    a_im = [jnp.broadcast_to(a_scr[s, 1:2, :], (8, S5_STATES)) for s in streams]
    perm = perm_ref[...]
    u = [u_ref[0, :, s * 128:(s + 1) * 128] for s in streams]
    for s in streams:
        u_p = jnp.dot(perm, u[s].astype(BF16), preferred_element_type=F32).astype(BF16)
        xr_scr[s] = jnp.dot(u_p, bbar_scr[s, 0], preferred_element_type=F32)
        xi_scr[s] = jnp.dot(u_p, bbar_scr[s, 1], preferred_element_type=F32)

    def scan_body(i, x):
        row = pl.multiple_of(i * 8, 8)
        out = []
        for s in streams:
            xr, xi = _cmul_add(a_re[s], a_im[s], x[2 * s], x[2 * s + 1],
                               xr_scr[s, pl.ds(row, 8), :], xi_scr[s, pl.ds(row, 8), :])
            xr_scr[s, pl.ds(row, 8), :] = xr
            xi_scr[s, pl.ds(row, 8), :] = xi
            out += [xr, xi]
        return tuple(out)

    zero = jnp.zeros((8, S5_STATES), F32)
    ends = lax.fori_loop(0, S5_SUB, scan_body, (zero,) * (2 * S5_STREAMS), unroll=4)

    cr, ci = [], []
    for s in streams:
        er, ei = ends[2 * s], ends[2 * s + 1]
        ar_s = pr_scr[s, S5_SUB - 1][0:1]
        ai_s = pi_scr[s, S5_SUB - 1][0:1]
        rows_r = [carry_scr[s, 0:1, :]]
        rows_i = [carry_scr[s, 1:2, :]]
        for j in range(8):
            nr, ni = _cmul_add(ar_s, ai_s, rows_r[-1], rows_i[-1], er[j:j + 1], ei[j:j + 1])
            rows_r.append(nr)
            rows_i.append(ni)
        carry_scr[s, 0:1, :] = rows_r[8]
        carry_scr[s, 1:2, :] = rows_i[8]
        cr.append(jnp.concatenate(rows_r[:8], axis=0))
        ci.append(jnp.concatenate(rows_i[:8], axis=0))

    def fix_body(i, carry):
        row = pl.multiple_of(i * 8, 8)
        for s in streams:
            xr, xi = _cmul_add(pr_scr[s, i], pi_scr[s, i], cr[s], ci[s],
                               xr_scr[s, pl.ds(row, 8), :], xi_scr[s, pl.ds(row, 8), :])
            xr_scr[s, pl.ds(row, 8), :] = xr
            xi_scr[s, pl.ds(row, 8), :] = xi
        return carry

    lax.fori_loop(0, S5_SUB, fix_body, 0, unroll=4)

    perm_t = permt_ref[...]
    for s in streams:
        y_p = (jnp.dot(xr_scr[s].astype(BF16), cmat_ref[s, 0], preferred_element_type=F32)
               - jnp.dot(xi_scr[s].astype(BF16), cmat_ref[s, 1], preferred_element_type=F32))
        y_hi = y_p.astype(BF16)
        y_lo = (y_p - y_hi.astype(F32)).astype(BF16)
        y = jnp.dot(perm_t, y_hi, preferred_element_type=F32) + jnp.dot(perm_t, y_lo, preferred_element_type=F32)
        o_ref[0, :, s * 128:(s + 1) * 128] = jax.nn.gelu(y + d_ref[s] * u[s])


def s5_scan(u, lam_re, lam_im, log_dt, b_re, b_im, c_re, c_im, d_skip):
    bsz, seq_len, _ = u.shape
    nb = SSM_GROUPS // S5_GROUPS_PER_BLOCK
    g = S5_STREAMS
    eye = jnp.eye(S5_GROUPS_PER_BLOCK, dtype=F32)

    def blockdiag_b(m):
        m = jnp.swapaxes(m, 1, 2).reshape(nb, S5_GROUPS_PER_BLOCK, SSM_CH_PER_GROUP, SSM_STATE)
        return jnp.einsum('nghp,gk->nghkp', m, eye).reshape(nb, 128, S5_STATES)

    def blockdiag_c(m):
        m = jnp.swapaxes(m, 1, 2).reshape(nb, S5_GROUPS_PER_BLOCK, SSM_STATE, SSM_CH_PER_GROUP)
        return jnp.einsum('ngph,gk->ngpkh', m, eye).reshape(nb, S5_STATES, 128)

    log_dt_states = jnp.broadcast_to(log_dt[:, None], lam_re.shape)
    lam = jnp.stack([m.reshape(nb, S5_STATES) for m in (lam_re, lam_im, log_dt_states)], axis=1)
    bmat = jnp.stack([blockdiag_b(b_re), blockdiag_b(b_im)], axis=1)
    cmat = jnp.stack([blockdiag_c(c_re), blockdiag_c(c_im)], axis=1).astype(BF16)
    d = d_skip.reshape(nb, 1, 128)
    r = np.arange(S5_CHUNK)
    perm = np.zeros((S5_CHUNK, S5_CHUNK), np.float32)
    perm[r, (r % 8) * S5_SUB + r // 8] = 1.0
    perm = jnp.asarray(perm, BF16)
    return pl.pallas_call(
        _s5_kernel,
        grid=(bsz, nb // g, seq_len // S5_CHUNK),
        in_specs=[
            pl.BlockSpec((1, S5_CHUNK, 128 * g), lambda b, k, c: (b, c, k)),
            pl.BlockSpec((g, 3, S5_STATES), lambda b, k, c: (k, 0, 0)),
            pl.BlockSpec((g, 2, 128, S5_STATES), lambda b, k, c: (k, 0, 0, 0)),
            pl.BlockSpec((g, 2, S5_STATES, 128), lambda b, k, c: (k, 0, 0, 0)),
            pl.BlockSpec((g, 1, 128), lambda b, k, c: (k, 0, 0)),
            pl.BlockSpec((S5_CHUNK, S5_CHUNK), lambda b, k, c: (0, 0)),
            pl.BlockSpec((S5_CHUNK, S5_CHUNK), lambda b, k, c: (0, 0)),
        ],
        out_specs=pl.BlockSpec((1, S5_CHUNK, 128 * g), lambda b, k, c: (b, c, k)),
        out_shape=jax.ShapeDtypeStruct((bsz, seq_len, SSM_WIDTH), F32),
        scratch_shapes=[pltpu.VMEM((g, S5_CHUNK, S5_STATES), F32), pltpu.VMEM((g, S5_CHUNK, S5_STATES), F32),
                        pltpu.VMEM((g, S5_SUB, 8, S5_STATES), F32), pltpu.VMEM((g, S5_SUB, 8, S5_STATES), F32),
                        pltpu.VMEM((g, 2, S5_STATES), F32), pltpu.VMEM((g, 2, S5_STATES), F32),
                        pltpu.VMEM((g, 2, 128, S5_STATES), BF16)],
        compiler_params=pltpu.CompilerParams(
            dimension_semantics=("arbitrary", "arbitrary", "arbitrary"), vmem_limit_bytes=V7X_VMEM_LIMIT_BYTES),
        name="s5_scan",
    )(u, lam, bmat, cmat, d, perm, perm.T)


def _softmax_tile(s, m_old):
    m_new = jnp.maximum(m_old, jnp.max(s, axis=1, keepdims=True))
    m_wide = jnp.concatenate([m_new] * (s.shape[1] // 128), axis=1)
    return m_new, jnp.exp2(m_old - m_new), jnp.exp2(s - m_wide)


def _lane_is_low(shape):
    return lax.broadcasted_iota(jnp.int32, shape, 1) < HEAD_DIM


def _pad_kt(kt, variant):
    z = jnp.zeros_like(kt)
    return jnp.concatenate([kt, z] if variant == 0 else [z, kt], axis=0)


def _pad_v(vv, variant):
    low = _lane_is_low(vv.shape)
    keep = low if variant == 0 else jnp.logical_not(low)
    return jnp.where(keep, vv, jnp.ones_like(vv))


def _finish(acc, variant):
    lane = lax.broadcasted_iota(jnp.int32, acc.shape, 1)
    lsel = lane == (HEAD_DIM if variant == 0 else 0)
    l = jnp.sum(jnp.where(lsel, acc, 0.0), axis=1, keepdims=True)
    keep = (lane < HEAD_DIM) if variant == 0 else (lane >= HEAD_DIM)
    return jnp.where(keep, acc / l, 0.0)


def _nsa_kernel(q_ref, g_ref, kct_ref, vc_ref, kst_ref, vs_ref, kwt_ref, vw_ref, ovl_ref, gx_ref, o_ref,
                m_scr, acc_scr, s_scr_a, s_scr_b, p_scr, *, seq_len):
    s_slots = (s_scr_a, s_scr_b)
    n_sel = seq_len // SEL_BLOCK
    n_cpad = seq_len // CMP_STRIDE
    sel_tile = PROJ_TILE
    blocks_per_tile = sel_tile // SEL_BLOCK
    win_tiles = WINDOW // Q_BLOCK + 1
    n_pair = Q_PER_KV // 2
    rows = n_pair * Q_BLOCK
    i = pl.program_id(2)
    t0 = i * Q_BLOCK

    qb = q_ref[0]
    qst = jnp.concatenate([qb[:, p * 128:(p + 1) * 128] for p in range(n_pair)], axis=0)

    sig = jax.nn.sigmoid(g_ref[0])
    sig_hi = sig.astype(BF16)
    sig_lo = (sig - sig_hi.astype(F32)).astype(BF16)
    gx = gx_ref[0]
    gexp = (jnp.dot(sig_hi, gx, preferred_element_type=F32) + jnp.dot(sig_lo, gx, preferred_element_type=F32))

    def gate_of(branch):
        base = branch * n_pair * 128
        return jnp.concatenate([gexp[:, base + p * 128: base + (p + 1) * 128] for p in range(n_pair)], axis=0)

    t_row = t0 + lax.broadcasted_iota(jnp.int32, (Q_BLOCK, 1), 0)

    slab = 64
    kct = kct_ref[0, 0]
    s_cmp = [jnp.dot(qst, _pad_kt(kct, v), preferred_element_type=F32) for v in range(2)]
    n_kblk = seq_len // Q_BLOCK
    w0 = jnp.clip(i - (win_tiles - 1), 0, n_kblk - win_tiles)
    kw = jnp.concatenate([kwt_ref[0, 0, w0 + j] for j in range(win_tiles)], axis=1)
    s_win = [jnp.dot(qst, _pad_kt(kw, v), preferred_element_type=F32) for v in range(2)]
    for v in range(2):
        s_slots[0][v] = jnp.dot(qst, _pad_kt(kst_ref[0, 0, 0], v), preferred_element_type=F32)

    n_iota = lax.broadcasted_iota(jnp.int32, (Q_BLOCK, n_cpad), 1)
    cmask = (n_iota * CMP_STRIDE + (CMP_BLOCK - 1)) <= t_row
    cmask4 = jnp.concatenate([cmask] * n_pair, axis=0)
    vcd = vc_ref[0, 0]
    p_sum = jnp.zeros((Q_BLOCK, n_cpad), F32)
    out = jnp.zeros((rows, 128), F32)
    o_c = jnp.zeros((rows, 128), F32)
    for v in range(2):
        s = jnp.where(cmask4, s_cmp[v], NEG)
        m = jnp.max(s, axis=1, keepdims=True)
        e = jnp.where(cmask4, jnp.exp2(s - m), 0.0)
        l = jnp.sum(e, axis=1, keepdims=True)
        p = e * (1.0 / jnp.maximum(l, 1e-30))
        for pp in range(n_pair):
            p_sum = p_sum + p[pp * Q_BLOCK:(pp + 1) * Q_BLOCK]
        low = _lane_is_low((n_cpad, 128))
        vz = jnp.where(low if v == 0 else jnp.logical_not(low), vcd, jnp.zeros_like(vcd))
        o_c = o_c + jnp.dot(p.astype(BF16), vz, preferred_element_type=F32)
    out = out + gate_of(0) * o_c

    ps_hi = p_sum.astype(BF16)
    ps_lo = (p_sum - ps_hi.astype(F32)).astype(BF16)
    ovl = ovl_ref[...]
    nt = (((1,), (1,)), ((), ()))
    imp_t = (lax.dot_general(ovl, ps_hi, nt, preferred_element_type=F32)
             + lax.dot_general(ovl, ps_lo, nt, preferred_element_type=F32))

    vw = jnp.concatenate([vw_ref[0, 0, w0 + j] for j in range(win_tiles)], axis=0)
    kpos_w = w0 * Q_BLOCK + lax.broadcasted_iota(jnp.int32, (Q_BLOCK, win_tiles * Q_BLOCK), 1)
    diff = t_row - kpos_w
    wbias = jnp.where((diff >= 0) & (diff < WINDOW), 0.0, NEG)
    wbias4 = jnp.concatenate([wbias] * n_pair, axis=0)
    o_w = jnp.zeros((rows, 128), F32)
    for v in range(2):
        s = s_win[v] + wbias4
        m = jnp.max(s, axis=1, keepdims=True)
        p = jnp.exp2(s - m)
        o_w = o_w + _finish(jnp.dot(p.astype(BF16), _pad_v(vw, v), preferred_element_type=F32), v)
    out = out + gate_of(2) * o_w

    s_iota = lax.broadcasted_iota(jnp.int32, (n_sel, Q_BLOCK), 0)
    t_lane = t0 + lax.broadcasted_iota(jnp.int32, (n_sel, Q_BLOCK), 1)
    cur = t_lane // SEL_BLOCK
    forced = (s_iota == 0) | (s_iota == cur) | (s_iota == cur - 1)
    valid = s_iota * SEL_BLOCK <= t_lane
    score = jnp.where(forced, FORCE, jnp.where(valid, imp_t, -1.0))
    s_f = s_iota.astype(F32)
    sel_t = jnp.zeros((n_sel, Q_BLOCK), F32)
    for _ in range(min(SEL_TOPK, n_sel)):
        mx = jnp.max(score, axis=0, keepdims=True)
        idx = jnp.min(jnp.where(score == mx, s_f, float(n_sel)), axis=0, keepdims=True)
        hit = s_f == idx
        sel_t = jnp.where(hit, 1.0, sel_t)
        score = jnp.where(hit, -3e38, score)
    selmask = sel_t.T.astype(BF16)

    m_scr[...] = jnp.full(m_scr.shape, NEG, F32)
    acc_scr[...] = jnp.zeros(acc_scr.shape, F32)
    n_tiles = (t0 + Q_BLOCK + sel_tile - 1) // sel_tile

    last_tile = seq_len // sel_tile - 1

    def bias_of(kt):
        blk = kt * blocks_per_tile + lax.broadcasted_iota(jnp.int32, (n_sel, sel_tile), 1) // SEL_BLOCK
        expand = (lax.broadcasted_iota(jnp.int32, (n_sel, sel_tile), 0) == blk).astype(BF16)
        selexp = jnp.dot(selmask, expand, preferred_element_type=F32)
        kpos = kt * sel_tile + lax.broadcasted_iota(jnp.int32, (Q_BLOCK, sel_tile), 1)
        bias = jnp.where((selexp > 0.5) & (kpos <= t_row), 0.0, NEG)
        return jnp.concatenate([bias] * n_pair, axis=0)

    def scores_into(slot, kt):
        bias4 = bias_of(kt)
        kt_tile = kst_ref[0, 0, jnp.minimum(kt, last_tile)]
        for v in range(2):
            s_slots[slot][v] = jnp.dot(qst, _pad_kt(kt_tile, v), preferred_element_type=F32) + bias4

    def attend_from(slot, kt):
        v_tile = vs_ref[0, 0, jnp.minimum(kt, last_tile)]
        for v in range(2):
            for h in range(rows // slab):
                r = slice(h * slab, (h + 1) * slab)
                m_new, alpha, p = _softmax_tile(s_slots[slot][v, r, :], m_scr[v, r, :])
                m_scr[v, r, :] = m_new
                acc_scr[v, r, :] = alpha * acc_scr[v, r, :]
                p_scr[v, r, :] = p.astype(BF16)
            acc_scr[v] += jnp.dot(p_scr[v], _pad_v(v_tile, v), preferred_element_type=F32)

    bias_first = bias_of(0)
    for v in range(2):
        s_slots[0][v] = s_slots[0][v] + bias_first

    def sel_body(j, carry):
        kt = 2 * j
        scores_into(1, kt + 1)
        attend_from(0, kt)
        scores_into(0, kt + 2)
        attend_from(1, kt + 1)
        return carry

    lax.fori_loop(0, (n_tiles + 1) // 2, sel_body, 0)
    out = out + gate_of(1) * (_finish(acc_scr[0], 0) + _finish(acc_scr[1], 1))

    o_ref[0] = jnp.concatenate([out[p * Q_BLOCK:(p + 1) * Q_BLOCK] for p in range(n_pair)], axis=1)


def nsa_attention(q, gate_pad, kct, vc, kst, vs, kwt, vw):
    bsz, seq_len, _ = q.shape
    n_sel = seq_len // SEL_BLOCK
    n_cpad = seq_len // CMP_STRIDE
    n_cmp = (seq_len - CMP_BLOCK) // CMP_STRIDE + 1
    n_pair = Q_PER_KV // 2
    cs = np.arange(n_cpad) * CMP_STRIDE
    ce = cs + CMP_BLOCK - 1
    ss = np.arange(n_sel) * SEL_BLOCK
    se = ss + SEL_BLOCK - 1
    ovl = (cs[None, :] <= se[:, None]) & (ce[None, :] >= ss[:, None]) & (np.arange(n_cpad)[None, :] < n_cmp)
    ovl = jnp.asarray(ovl.astype(np.float32), BF16)
    gx = np.zeros((NSA_KV_HEADS, 128, N_BRANCH * n_pair * 128), np.float32)
    for k in range(NSA_KV_HEADS):
        for hl in range(Q_PER_KV):
            for br in range(N_BRANCH):
                c0 = br * n_pair * 128 + hl * HEAD_DIM
                gx[k, (k * Q_PER_KV + hl) * N_BRANCH + br, c0:c0 + HEAD_DIM] = 1.0
    gx = jnp.asarray(gx, BF16)
    width = Q_PER_KV * HEAD_DIM
    full = lambda *shape: pl.BlockSpec((1, 1) + shape, lambda b, k, i: (b, k) + (0,) * len(shape))
    return pl.pallas_call(
        functools.partial(_nsa_kernel, seq_len=seq_len),
        grid=(bsz, NSA_KV_HEADS, seq_len // Q_BLOCK),
        in_specs=[
            pl.BlockSpec((1, Q_BLOCK, width), lambda b, k, i: (b, i, k)),
            pl.BlockSpec((1, Q_BLOCK, 128), lambda b, k, i: (b, i, 0)),
            full(HEAD_DIM, n_cpad), full(n_cpad, 128),
            full(seq_len // PROJ_TILE, HEAD_DIM, PROJ_TILE), full(seq_len // PROJ_TILE, PROJ_TILE, 128),
            full(seq_len // Q_BLOCK, HEAD_DIM, Q_BLOCK), full(seq_len // Q_BLOCK, Q_BLOCK, 128),
            pl.BlockSpec((n_sel, n_cpad), lambda b, k, i: (0, 0)),
            pl.BlockSpec((1, 128, N_BRANCH * n_pair * 128), lambda b, k, i: (k, 0, 0)),
        ],
        out_specs=pl.BlockSpec((1, Q_BLOCK, width), lambda b, k, i: (b, i, k)),
        out_shape=jax.ShapeDtypeStruct((bsz, seq_len, NSA_WIDTH), F32),
        scratch_shapes=[pltpu.VMEM((2, n_pair * Q_BLOCK, 128), F32), pltpu.VMEM((2, n_pair * Q_BLOCK, 128), F32),
                        pltpu.VMEM((2, n_pair * Q_BLOCK, PROJ_TILE), F32),
                        pltpu.VMEM((2, n_pair * Q_BLOCK, PROJ_TILE), F32),
                        pltpu.VMEM((2, n_pair * Q_BLOCK, PROJ_TILE), BF16)],
        compiler_params=pltpu.CompilerParams(
            dimension_semantics=("arbitrary", "arbitrary", "arbitrary"), vmem_limit_bytes=V7X_VMEM_LIMIT_BYTES),
        name="nsa_attention",
    )(q, gate_pad, kct, vc, kst, vs, kwt, vw, ovl, gx)


MOE_TILE = 1024
MOE_SUB = 256
MOE_ROWS = 48
MOE_SLOT = 64
MOE_GROUP = 4
MOE_GATHER_GROUP = 4


def _first_max_mask(x, idx_f, axis):
    mx = jnp.max(x, axis=axis, keepdims=True)
    first = jnp.min(jnp.where(x == mx, idx_f, 1e9), axis=axis, keepdims=True)
    return idx_f == first, mx


def _route(logits, bias):
    per_group = N_EXPERTS // N_EXPERT_GROUPS
    tr = logits.shape[1]
    aff = jax.nn.sigmoid(logits)
    biased = aff + bias
    grp = biased.reshape(N_EXPERT_GROUPS, per_group, tr)
    in_grp = lax.broadcasted_iota(jnp.int32, grp.shape, 1).astype(F32)
    hit1, m1 = _first_max_mask(grp, in_grp, 1)
    m2 = jnp.max(jnp.where(hit1, -jnp.inf, grp), axis=1, keepdims=True)
    gscore = (m1 + m2).reshape(N_EXPERT_GROUPS, tr)
    g_idx = lax.broadcasted_iota(jnp.int32, gscore.shape, 0).astype(F32)
    gsel = jnp.zeros(gscore.shape, F32)
    for _ in range(TOPK_GROUPS):
        hit, _ = _first_max_mask(gscore, g_idx, 0)
        gsel = jnp.where(hit, 1.0, gsel)
        gscore = jnp.where(hit, -jnp.inf, gscore)
    gmask = jnp.broadcast_to(gsel.reshape(N_EXPERT_GROUPS, 1, tr), grp.shape).reshape(N_EXPERTS, tr)
    cand = jnp.where(gmask > 0.5, biased, NEG)
    e_idx = lax.broadcasted_iota(jnp.int32, cand.shape, 0).astype(F32)
    sel = jnp.zeros(cand.shape, F32)
    for _ in range(TOP_K):
        hit, _ = _first_max_mask(cand, e_idx, 0)
        sel = jnp.where(hit, 1.0, sel)
        cand = jnp.where(hit, -jnp.inf, cand)
    w = jnp.where(sel > 0.5, aff, 0.0)
    return w / jnp.sum(w, axis=0, keepdims=True) * ROUTED_SCALE, sel


def _moe_kernel(cnt_ref, x_ref, sel_ref, w_ref, init_ref, wg_ref, wu_ref, wd_ref, lng_ref, lnb_ref, o_ref,
                rank_scr, ybuf_scr, sbuf_scr, xe_scr):
    i = pl.program_id(0)
    e = pl.program_id(1)
    tm = x_ref.shape[0]
    n_sub = tm // MOE_SUB
    tn = (((0,), (0,)), ((), ()))

    @pl.when(e == 0)
    def _():
        o_ref[...] = init_ref[...]
        before = (lax.broadcasted_iota(jnp.int32, (MOE_SUB, MOE_SUB), 0)
                  < lax.broadcasted_iota(jnp.int32, (MOE_SUB, MOE_SUB), 1))
        before = jnp.where(before, 1.0, 0.0).astype(BF16)
        for q in range(n_sub):
            cols = slice(q * MOE_SUB, (q + 1) * MOE_SUB)
            rank_scr[:, cols] = jnp.dot(sel_ref[:, cols].astype(BF16), before, preferred_element_type=F32)

    count = cnt_ref[i * N_EXPERTS + e]
    sel_e = sel_ref[pl.ds(e, 1), :]
    rank_e = rank_scr[pl.ds(e, 1), :]
    w_e = w_ref[pl.ds(e, 1), :]

    def one_hots(rank_row, sel_row, c):
        row = (c * MOE_ROWS + lax.broadcasted_iota(jnp.int32, (MOE_ROWS, MOE_SUB), 0)).astype(F32)
        hits = []
        for q in range(n_sub):
            cols = slice(q * MOE_SUB, (q + 1) * MOE_SUB)
            hits.append((rank_row[:, cols] == row) & (sel_row[:, cols] > 0.5))
        return hits

    def swiglu(xe):
        g = jnp.dot(xe, wg_ref[0], preferred_element_type=F32)
        u = jnp.dot(xe, wu_ref[0], preferred_element_type=F32)
        h = (jax.nn.silu(g) * u).astype(BF16)
        return jnp.dot(h, wd_ref[0], preferred_element_type=F32).astype(BF16)

    def weighted(hits):
        return [jnp.where(hits[q], w_e[:, q * MOE_SUB:(q + 1) * MOE_SUB], 0.0).astype(BF16) for q in range(n_sub)]

    slot = e % MOE_GROUP

    gslot = e % MOE_GATHER_GROUP

    @pl.when(gslot == 0)
    def _():
        stacks = [[] for _ in range(n_sub)]
        for gi in range(MOE_GATHER_GROUP):
            hits = one_hots(rank_scr[pl.ds(e + gi, 1), :], sel_ref[pl.ds(e + gi, 1), :], 0)
            for q in range(n_sub):
                stacks[q].append(jnp.where(hits[q], 1.0, 0.0).astype(BF16))
        for q in range(n_sub):
            cols = slice(q * MOE_SUB, (q + 1) * MOE_SUB)
            xg = jnp.dot(jnp.concatenate(stacks[q], axis=0), x_ref[cols, :],
                         preferred_element_type=F32).astype(BF16)
            for gi in range(MOE_GATHER_GROUP):
                xe_scr[gi, q] = xg[gi * MOE_ROWS:(gi + 1) * MOE_ROWS]

    scatters = weighted(one_hots(rank_e, sel_e, 0))
    y = swiglu(xe_scr[gslot].reshape(n_sub * MOE_ROWS, D_MODEL))
    spare = MOE_SLOT - MOE_ROWS
    for q in range(n_sub):
        sbuf_scr[q, slot] = jnp.concatenate([scatters[q], jnp.zeros((spare, MOE_SUB), BF16)], axis=0)
        ybuf_scr[q, slot] = jnp.concatenate(
            [y[q * MOE_ROWS:(q + 1) * MOE_ROWS], jnp.zeros((spare, D_MODEL), BF16)], axis=0)

    @pl.when(slot == MOE_GROUP - 1)
    def _():
        for q in range(n_sub):
            cols = slice(q * MOE_SUB, (q + 1) * MOE_SUB)
            o_ref[cols, :] += lax.dot_general(sbuf_scr[q].reshape(MOE_GROUP * MOE_SLOT, MOE_SUB),
                                              ybuf_scr[q].reshape(MOE_GROUP * MOE_SLOT, D_MODEL), tn,
                                              preferred_element_type=F32)

    def overflow_body(c, carry):
        hits = one_hots(rank_e, sel_e, c)
        sc = weighted(hits)
        xe = jnp.concatenate(
            [jnp.dot(jnp.where(hits[q], 1.0, 0.0).astype(BF16), x_ref[q * MOE_SUB:(q + 1) * MOE_SUB, :],
                     preferred_element_type=F32).astype(BF16) for q in range(n_sub)], axis=0)
        yy = swiglu(xe)
        for q in range(n_sub):
            cols = slice(q * MOE_SUB, (q + 1) * MOE_SUB)
            o_ref[cols, :] += lax.dot_general(sc[q], yy[q * MOE_ROWS:(q + 1) * MOE_ROWS], tn,
                                              preferred_element_type=F32)
        return carry

    lax.fori_loop(1, (count + MOE_ROWS - 1) // MOE_ROWS, overflow_body, 0)

    @pl.when(e == N_EXPERTS - 1)
    def _():
        o_ref[...] = _layer_norm(o_ref[...], lng_ref[...], lnb_ref[...])


def moe_routed(x_bf16, sel_t, w_t, init, w_gate, w_up, w_down, ln_g, ln_b):
    n_tok = x_bf16.shape[0]
    n_tiles = n_tok // MOE_TILE
    per_sub = jnp.sum(sel_t.reshape(N_EXPERTS, n_tiles, MOE_TILE // MOE_SUB, MOE_SUB), axis=-1)
    cnt = jnp.max(per_sub, axis=-1).T.astype(jnp.int32).reshape(-1)
    grid_spec = pltpu.PrefetchScalarGridSpec(
        num_scalar_prefetch=1,
        grid=(n_tiles, N_EXPERTS),
        in_specs=[
            pl.BlockSpec((MOE_TILE, D_MODEL), lambda i, e, cnt: (i, 0), pipeline_mode=pl.Buffered(1)),
            pl.BlockSpec((N_EXPERTS, MOE_TILE), lambda i, e, cnt: (0, i)),
            pl.BlockSpec((N_EXPERTS, MOE_TILE), lambda i, e, cnt: (0, i)),
            pl.BlockSpec((MOE_TILE, D_MODEL), lambda i, e, cnt: (i, 0), pipeline_mode=pl.Buffered(1)),
            pl.BlockSpec((1, D_MODEL, EXPERT_FF), lambda i, e, cnt: (e, 0, 0)),
            pl.BlockSpec((1, D_MODEL, EXPERT_FF), lambda i, e, cnt: (e, 0, 0)),
            pl.BlockSpec((1, EXPERT_FF, D_MODEL), lambda i, e, cnt: (e, 0, 0)),
            pl.BlockSpec((1, D_MODEL), lambda i, e, cnt: (0, 0)),
            pl.BlockSpec((1, D_MODEL), lambda i, e, cnt: (0, 0)),
        ],
        out_specs=pl.BlockSpec((MOE_TILE, D_MODEL), lambda i, e, cnt: (i, 0)),
        scratch_shapes=[pltpu.VMEM((N_EXPERTS, MOE_TILE), F32),
                        pltpu.VMEM((MOE_TILE // MOE_SUB, MOE_GROUP, MOE_SLOT, D_MODEL), BF16),
                        pltpu.VMEM((MOE_TILE // MOE_SUB, MOE_GROUP, MOE_SLOT, MOE_SUB), BF16),
                        pltpu.VMEM((MOE_GATHER_GROUP, MOE_TILE // MOE_SUB, MOE_ROWS, D_MODEL), BF16)],
    )
    return pl.pallas_call(
        _moe_kernel,
        grid_spec=grid_spec,
        out_shape=jax.ShapeDtypeStruct((n_tok, D_MODEL), F32),
        compiler_params=pltpu.CompilerParams(dimension_semantics=("arbitrary", "arbitrary"),
                                             vmem_limit_bytes=V7X_VMEM_LIMIT_BYTES),
        name="moe_routed",
    )(cnt, x_bf16, sel_t, w_t, init, w_gate, w_up, w_down, ln_g.reshape(1, D_MODEL), ln_b.reshape(1, D_MODEL))


def _post_mix_kernel(y_ref, a_ref, x_ref, wglu_ref, wout_ref, g_ref, b_ref, wrt_ref, rbias_ref,
                     wsg_ref, wsu_ref, wsd_ref, acc_ref, xb_ref, w_ref, sel_ref):
    y = y_ref[...]
    y_ssm = y * jax.nn.sigmoid(jnp.dot(y.astype(BF16), wglu_ref[...], preferred_element_type=F32))
    mix = (jnp.dot(y_ssm.astype(BF16), wout_ref[:SSM_WIDTH, :], preferred_element_type=F32)
           + jnp.dot(a_ref[...].astype(BF16), wout_ref[SSM_WIDTH:, :], preferred_element_type=F32))
    x1 = _layer_norm(DEEPNORM_ALPHA * x_ref[...] + mix, g_ref[...], b_ref[...])
    xb = x1.astype(BF16)
    xb_ref[...] = xb
    nt = (((1,), (1,)), ((), ()))
    w, sel = _route(lax.dot_general(wrt_ref[...], xb, nt, preferred_element_type=F32), rbias_ref[...])
    w_ref[...] = w
    sel_ref[...] = sel
    h = jax.nn.silu(jnp.dot(xb, wsg_ref[...], preferred_element_type=F32)) * jnp.dot(
        xb, wsu_ref[...], preferred_element_type=F32)
    acc_ref[...] = DEEPNORM_ALPHA * x1 + jnp.dot(h.astype(BF16), wsd_ref[...], preferred_element_type=F32)


def post_mix(y_s5, y_nsa, xt, w_glu, w_out, ln_g, ln_b, w_router, router_bias, wsg, wsu, wsd, tm=512):
    n_tok = xt.shape[0]
    row = lambda i: (i, 0)
    const = lambda i: (0, 0)
    once = pl.Buffered(1)
    wspec = lambda r, c: pl.BlockSpec((r, c), const, pipeline_mode=once)
    return pl.pallas_call(
        _post_mix_kernel,
        grid=(n_tok // tm,),
        in_specs=[pl.BlockSpec((tm, SSM_WIDTH), row), pl.BlockSpec((tm, NSA_WIDTH), row),
                  pl.BlockSpec((tm, D_MODEL), row),
                  wspec(SSM_WIDTH, SSM_WIDTH), wspec(D_MODEL, D_MODEL),
                  pl.BlockSpec((1, D_MODEL), const), pl.BlockSpec((1, D_MODEL), const),
                  wspec(N_EXPERTS, D_MODEL), pl.BlockSpec((N_EXPERTS, 1), const),
                  wspec(D_MODEL, EXPERT_FF), wspec(D_MODEL, EXPERT_FF), wspec(EXPERT_FF, D_MODEL)],
        out_specs=[pl.BlockSpec((tm, D_MODEL), row), pl.BlockSpec((tm, D_MODEL), row),
                   pl.BlockSpec((N_EXPERTS, tm), lambda i: (0, i)), pl.BlockSpec((N_EXPERTS, tm), lambda i: (0, i))],
        out_shape=[jax.ShapeDtypeStruct((n_tok, D_MODEL), F32), jax.ShapeDtypeStruct((n_tok, D_MODEL), BF16),
                   jax.ShapeDtypeStruct((N_EXPERTS, n_tok), F32), jax.ShapeDtypeStruct((N_EXPERTS, n_tok), F32)],
        compiler_params=pltpu.CompilerParams(dimension_semantics=("arbitrary",),
                                             vmem_limit_bytes=V7X_VMEM_LIMIT_BYTES),
        name="post_mix",
    )(y_s5, y_nsa, xt, w_glu.astype(BF16), w_out.astype(BF16), ln_g.reshape(1, D_MODEL), ln_b.reshape(1, D_MODEL),
      w_router.T.astype(BF16), router_bias.reshape(N_EXPERTS, 1), wsg.astype(BF16), wsu.astype(BF16),
      wsd.astype(BF16))


def hybrid_layer(x, positions, w_in, lam_re, lam_im, log_dt, ssm_b_re, ssm_b_im, ssm_c_re, ssm_c_im, ssm_d,
                 w_glu, cmp_pos_k, cmp_pos_v, w_cmp_k1, w_cmp_k2, w_cmp_v1, w_cmp_v2, w_out, ln1_g, ln1_b,
                 w_router, router_bias, w_gate, w_up, w_down, ws_gate, ws_up, ws_down, ln2_g, ln2_b):
    bsz, L, _ = x.shape
    sizes = [SSM_WIDTH, NSA_WIDTH] + [KV_WIDTH] * 6 + [NSA_HEADS * N_BRANCH]
    o = [0] + [int(v) for v in np.cumsum(sizes)]
    col = lambda j: w_in[:, o[j]:o[j + 1]]
    dup = lambda w: jnp.concatenate([w[:, h * HEAD_DIM:(h + 1) * HEAD_DIM] for h in (0, 0, 1, 1)], axis=1)
    gate_cols = jnp.pad(col(8), ((0, 0), (0, 128 - NSA_HEADS * N_BRANCH)))
    w_uq = w_in[:, :o[2]].astype(BF16)
    w_kv = jnp.concatenate([col(4), col(6), dup(col(5)), dup(col(7)), col(2), col(3), gate_cols], axis=1).astype(BF16)

    xt = x.reshape(bsz * L, D_MODEL)
    u, q = proj_uq(xt, w_uq, positions.reshape(bsz * L, 1))
    kst, kwt, vs, vw, kc_raw, vc_raw, gate_pad = proj_kv(x, w_kv, positions.reshape(bsz, L, 1))
    kct, vcd = compress_kv(kc_raw, vc_raw, positions, cmp_pos_k, cmp_pos_v, w_cmp_k1, w_cmp_k2, w_cmp_v1, w_cmp_v2)
    y_s5 = s5_scan(u.reshape(bsz, L, SSM_WIDTH), lam_re, lam_im, log_dt, ssm_b_re, ssm_b_im, ssm_c_re, ssm_c_im, ssm_d)
    vw = vw.reshape(bsz, NSA_KV_HEADS, L // Q_BLOCK, Q_BLOCK, 128)
    y_nsa = nsa_attention(q.reshape(bsz, L, NSA_WIDTH), gate_pad, kct, vcd, kst, vs, kwt, vw)
    acc0, x1b, w_t, sel_t = post_mix(y_s5.reshape(bsz * L, SSM_WIDTH), y_nsa.reshape(bsz * L, NSA_WIDTH), xt,
                                     w_glu, w_out, ln1_g, ln1_b, w_router, router_bias, ws_gate, ws_up, ws_down)
    out = moe_routed(x1b, sel_t, w_t, acc0, w_gate.astype(BF16), w_up.astype(BF16), w_down.astype(BF16),
                     ln2_g, ln2_b)
    return out.reshape(bsz, L, D_MODEL)


def kernel(x, positions, w_in, lam_re, lam_im, log_dt, ssm_b_re, ssm_b_im, ssm_c_re, ssm_c_im, ssm_d, w_glu, cmp_pos_k, cmp_pos_v, w_cmp_k1, w_cmp_k2, w_cmp_v1, w_cmp_v2, w_out, ln1_g, ln1_b, w_router, router_bias, w_gate, w_up, w_down, ws_gate, ws_up, ws_down, ln2_g, ln2_b):
    params = (w_in, lam_re, lam_im, log_dt, ssm_b_re, ssm_b_im, ssm_c_re, ssm_c_im, ssm_d,
              w_glu, cmp_pos_k, cmp_pos_v, w_cmp_k1, w_cmp_k2, w_cmp_v1, w_cmp_v2, w_out, ln1_g, ln1_b,
              w_router, router_bias, w_gate, w_up, w_down, ws_gate, ws_up, ws_down, ln2_g, ln2_b)
    return hybrid_layer(x, positions, *(p[0] for p in params))
```

```python
import functools
import math

import numpy as np
import jax
import jax.numpy as jnp
from jax import lax
from jax.experimental import pallas as pl
from jax.experimental.pallas import tpu as pltpu

D_MODEL = 2048
SSM_WIDTH = 1024
SSM_CH_PER_GROUP = 16
SSM_GROUPS = 64
SSM_STATE = 64
NSA_HEADS = 16
NSA_KV_HEADS = 2
HEAD_DIM = 64
Q_PER_KV = NSA_HEADS // NSA_KV_HEADS
NSA_WIDTH = NSA_HEADS * HEAD_DIM
KV_WIDTH = NSA_KV_HEADS * HEAD_DIM
N_BRANCH = 3
CMP_BLOCK = 32
CMP_STRIDE = 16
SEL_BLOCK = 64
SEL_TOPK = 16
WINDOW = 512
Q_BLOCK = 128
ROPE_THETA = 10000.0
N_EXPERTS = 64
TOP_K = 8
N_EXPERT_GROUPS = 8
TOPK_GROUPS = 4
ROUTED_SCALE = 2.5
EXPERT_FF = 512
DEPTH = 1
DEEPNORM_ALPHA = (2.0 * DEPTH) ** 0.25
LN_EPS = 1e-5
NEG = -1e30
FORCE = 1e4
F32 = jnp.float32
BF16 = jnp.bfloat16

V7X_VMEM_LIMIT_BYTES = 56 * 1024 * 1024


def _layer_norm(x, g, b):
    mu = jnp.mean(x, -1, keepdims=True)
    var = jnp.mean(jnp.square(x - mu), -1, keepdims=True)
    return (x - mu) * lax.rsqrt(var + LN_EPS) * g + b


def _rope_tables(pos_col, inv_row):
    ang = pos_col * inv_row
    return jnp.cos(ang), jnp.sin(ang)


def _rope_lanes(x, cos, sin):
    lane = lax.broadcasted_iota(jnp.int32, (x.shape[0], 128), 1)
    first_half = (lane % HEAD_DIM) < HEAD_DIM // 2
    outs = []
    for blk in range(x.shape[1] // 128):
        xb = x[:, blk * 128:(blk + 1) * 128]
        rot = jnp.where(first_half, -pltpu.roll(xb, 128 - HEAD_DIM // 2, 1), pltpu.roll(xb, HEAD_DIM // 2, 1))
        outs.append(xb * cos + rot * sin)
    return outs[0] if len(outs) == 1 else jnp.concatenate(outs, axis=1)


def _inv_freq_row():
    half = HEAD_DIM // 2
    inv = ROPE_THETA ** (-jnp.arange(half, dtype=F32) / half)
    return jnp.tile(inv, 128 // half).reshape(1, 128)


PROJ_TILE = 512
Q_SCALE = HEAD_DIM ** -0.5 * math.log2(math.e)


KV_COLS = 4 * KV_WIDTH + 2 * 2 * KV_WIDTH + 128


def _proj_in_kernel(x_ref, wuq_ref, wkv_ref, pos_ref, inv_ref,
                    u_ref, q_ref, kst_ref, kwt_ref, vs_ref, vw_ref, kc_ref, vc_ref, g_ref):
    xb = x_ref[0].astype(BF16)
    cos, sin = _rope_tables(pos_ref[0].astype(F32), inv_ref[...])
    uq = jnp.dot(xb, wuq_ref[...], preferred_element_type=F32)
    u_ref[0] = uq[:, :SSM_WIDTH]
    q_ref[0] = (_rope_lanes(uq[:, SSM_WIDTH:], cos, sin) * Q_SCALE).astype(BF16)
    acc = jnp.dot(xb, wkv_ref[...], preferred_element_type=F32)
    ks_t = _rope_lanes(acc[:, 0:128], cos, sin).T
    kw_t = _rope_lanes(acc[:, 128:256], cos, sin).T
    for k in range(NSA_KV_HEADS):
        kst_ref[0, k, 0] = ks_t[k * HEAD_DIM:(k + 1) * HEAD_DIM].astype(BF16)
        for j in range(PROJ_TILE // Q_BLOCK):
            kwt_ref[0, k, j] = kw_t[k * HEAD_DIM:(k + 1) * HEAD_DIM, j * Q_BLOCK:(j + 1) * Q_BLOCK].astype(BF16)
        vs_ref[0, k, 0] = acc[:, 256 + k * 128: 256 + (k + 1) * 128].astype(BF16)
        vw_ref[0, k] = acc[:, 512 + k * 128: 512 + (k + 1) * 128].astype(BF16)
    kc_ref[0] = acc[:, 768:896]
    vc_ref[0] = acc[:, 896:1024]
    g_ref[0] = acc[:, 1024:1152]


def proj_in(x, w_uq, w_kv, pos_col3):
    bsz, seq_len, _ = x.shape
    n_t = seq_len // PROJ_TILE
    per = PROJ_TILE // Q_BLOCK
    tok = lambda width: pl.BlockSpec((1, PROJ_TILE, width), lambda b, i: (b, i, 0))
    once = pl.Buffered(1)
    return pl.pallas_call(
        _proj_in_kernel,
        grid=(bsz, n_t),
        in_specs=[tok(D_MODEL),
                  pl.BlockSpec((D_MODEL, SSM_WIDTH + NSA_WIDTH), lambda b, i: (0, 0), pipeline_mode=once),
                  pl.BlockSpec((D_MODEL, KV_COLS), lambda b, i: (0, 0), pipeline_mode=once),
                  tok(1),
                  pl.BlockSpec((1, 128), lambda b, i: (0, 0))],
        out_specs=[
            tok(SSM_WIDTH), tok(NSA_WIDTH),
            pl.BlockSpec((1, NSA_KV_HEADS, 1, HEAD_DIM, PROJ_TILE), lambda b, i: (b, 0, i, 0, 0)),
            pl.BlockSpec((1, NSA_KV_HEADS, per, HEAD_DIM, Q_BLOCK), lambda b, i: (b, 0, i, 0, 0)),
            pl.BlockSpec((1, NSA_KV_HEADS, 1, PROJ_TILE, 128), lambda b, i: (b, 0, i, 0, 0)),
            pl.BlockSpec((1, NSA_KV_HEADS, PROJ_TILE, 128), lambda b, i: (b, 0, i, 0)),
            tok(128), tok(128), tok(128),
        ],
        out_shape=[
            jax.ShapeDtypeStruct((bsz, seq_len, SSM_WIDTH), F32),
            jax.ShapeDtypeStruct((bsz, seq_len, NSA_WIDTH), BF16),
            jax.ShapeDtypeStruct((bsz, NSA_KV_HEADS, n_t, HEAD_DIM, PROJ_TILE), BF16),
            jax.ShapeDtypeStruct((bsz, NSA_KV_HEADS, seq_len // Q_BLOCK, HEAD_DIM, Q_BLOCK), BF16),
            jax.ShapeDtypeStruct((bsz, NSA_KV_HEADS, n_t, PROJ_TILE, 128), BF16),
            jax.ShapeDtypeStruct((bsz, NSA_KV_HEADS, seq_len, 128), BF16),
            jax.ShapeDtypeStruct((bsz, seq_len, 128), F32),
            jax.ShapeDtypeStruct((bsz, seq_len, 128), F32),
            jax.ShapeDtypeStruct((bsz, seq_len, 128), F32),
        ],
        compiler_params=pltpu.CompilerParams(dimension_semantics=("arbitrary", "arbitrary"),
                                             vmem_limit_bytes=V7X_VMEM_LIMIT_BYTES),
        name="proj_in",
    )(x, w_uq, w_kv, pos_col3, _inv_freq_row())


def _compress_kernel(ck_ref, cv_ref, pek_ref, pev_ref, w1k_ref, w1v_ref, w2k_ref, w2v_ref, pos_ref, inv_ref,
                     kct_ref, vcd_ref):
    def hidden(c_ref, pe_ref, w1_ref):
        c = c_ref[0]
        lo = jnp.dot((c + pe_ref[0]).astype(BF16), w1_ref[0], preferred_element_type=F32)
        hi = jnp.dot((c + pe_ref[1]).astype(BF16), w1_ref[1], preferred_element_type=F32)
        hi_next = jnp.concatenate([hi[1:], jnp.zeros((1, hi.shape[1]), F32)], axis=0)
        return jax.nn.gelu(lo + hi_next).astype(BF16)

    kc = jnp.dot(hidden(ck_ref, pek_ref, w1k_ref), w2k_ref[...], preferred_element_type=F32)
    cos, sin = _rope_tables(pos_ref[0], inv_ref[...])
    kc_t = _rope_lanes(kc, cos, sin).T
    vc = jnp.dot(hidden(cv_ref, pev_ref, w1v_ref), w2v_ref[...], preferred_element_type=F32)
    for k in range(NSA_KV_HEADS):
        kct_ref[0, k] = kc_t[k * HEAD_DIM:(k + 1) * HEAD_DIM].astype(BF16)
        vcd_ref[0, k] = vc[:, k * 128:(k + 1) * 128].astype(BF16)


def compress_kv(kc_raw, vc_raw, positions, cmp_pos_k, cmp_pos_v, w_k1, w_k2, w_v1, w_v2):
    bsz, seq_len, _ = kc_raw.shape
    n_chunk = seq_len // CMP_STRIDE
    width = CMP_STRIDE * 128
    eye = jnp.eye(NSA_KV_HEADS, dtype=F32)

    def chunk_pe(pe):
        pe = pe.reshape(2, CMP_STRIDE, 1, HEAD_DIM)
        return jnp.broadcast_to(pe, (2, CMP_STRIDE, NSA_KV_HEADS, HEAD_DIM)).reshape(2, 1, width)

    def chunk_w1(w1):
        hid = w1.shape[1]
        w = w1.reshape(2, CMP_STRIDE, HEAD_DIM, hid)
        return jnp.einsum('htdj,kc->htkdcj', w, eye).reshape(2, width, NSA_KV_HEADS * hid).astype(BF16)

    hid = w_k2.shape[0]
    w2k = jnp.einsum('jd,kc->kjcd', w_k2, eye).reshape(NSA_KV_HEADS * hid, NSA_KV_HEADS * HEAD_DIM).astype(BF16)
    w2v = jnp.einsum('jd,kc,r->kjcrd', w_v2, eye, jnp.ones((2,), F32)).reshape(
        NSA_KV_HEADS * hid, NSA_KV_HEADS * 128).astype(BF16)
    pos = positions.astype(F32).reshape(bsz, n_chunk, CMP_STRIDE).sum(-1)
    pos_next = jnp.concatenate([pos[:, 1:], pos[:, -1:]], axis=1)
    cmp_pos = ((pos + pos_next) / CMP_BLOCK).reshape(bsz, n_chunk, 1)
    return pl.pallas_call(
        _compress_kernel,
        grid=(bsz,),
        in_specs=[pl.BlockSpec((1, n_chunk, width), lambda b: (b, 0, 0)),
                  pl.BlockSpec((1, n_chunk, width), lambda b: (b, 0, 0)),
                  pl.BlockSpec((2, 1, width), lambda b: (0, 0, 0)),
                  pl.BlockSpec((2, 1, width), lambda b: (0, 0, 0)),
                  pl.BlockSpec((2, width, NSA_KV_HEADS * hid), lambda b: (0, 0, 0)),
                  pl.BlockSpec((2, width, NSA_KV_HEADS * hid), lambda b: (0, 0, 0)),
                  pl.BlockSpec((NSA_KV_HEADS * hid, NSA_KV_HEADS * HEAD_DIM), lambda b: (0, 0)),
                  pl.BlockSpec((NSA_KV_HEADS * hid, NSA_KV_HEADS * 128), lambda b: (0, 0)),
                  pl.BlockSpec((1, n_chunk, 1), lambda b: (b, 0, 0)),
                  pl.BlockSpec((1, 128), lambda b: (0, 0))],
        out_specs=[pl.BlockSpec((1, NSA_KV_HEADS, HEAD_DIM, n_chunk), lambda b: (b, 0, 0, 0)),
                   pl.BlockSpec((1, NSA_KV_HEADS, n_chunk, 128), lambda b: (b, 0, 0, 0))],
        out_shape=[jax.ShapeDtypeStruct((bsz, NSA_KV_HEADS, HEAD_DIM, n_chunk), BF16),
                   jax.ShapeDtypeStruct((bsz, NSA_KV_HEADS, n_chunk, 128), BF16)],
        compiler_params=pltpu.CompilerParams(dimension_semantics=("arbitrary",),
                                             vmem_limit_bytes=V7X_VMEM_LIMIT_BYTES),
        name="compress_kv",
    )(kc_raw.reshape(bsz, n_chunk, width), vc_raw.reshape(bsz, n_chunk, width), chunk_pe(cmp_pos_k),
      chunk_pe(cmp_pos_v), chunk_w1(w_k1), chunk_w1(w_v1), w2k, w2v, cmp_pos, _inv_freq_row())


S5_CHUNK = 512
S5_SUB = S5_CHUNK // 8
S5_GROUPS_PER_BLOCK = 8
S5_STATES = S5_GROUPS_PER_BLOCK * SSM_STATE
S5_STREAMS = 4


def _cmul_add(ar, ai, xr, xi, br, bi):
    return ar * xr - ai * xi + br, ar * xi + ai * xr + bi


def _s5_kernel(u_ref, lam_ref, bmat_ref, cmat_ref, d_ref, perm_ref, permt_ref, o_ref,
               xr_scr, xi_scr, pr_scr, pi_scr, carry_scr, a_scr, bbar_scr):
    c = pl.program_id(2)
    streams = range(S5_STREAMS)

    @pl.when(c == 0)
    def _():
        powers = []
        for s in streams:
            lr, li = lam_ref[s, 0:1, :], lam_ref[s, 1:2, :]
            dt = jnp.exp(lam_ref[s, 2:3, :])
            mag = jnp.exp(lr * dt)
            ar, ai = mag * jnp.cos(li * dt), mag * jnp.sin(li * dt)
            zr, zi = ar - 1.0, ai
            den = lr * lr + li * li
            fr, fi = (zr * lr + zi * li) / den, (zi * lr - zr * li) / den
            a_scr[s, 0:1, :] = ar
            a_scr[s, 1:2, :] = ai
            b_re, b_im = bmat_ref[s, 0], bmat_ref[s, 1]
            bbar_scr[s, 0] = (fr * b_re - fi * b_im).astype(BF16)
            bbar_scr[s, 1] = (fr * b_im + fi * b_re).astype(BF16)
            powers += [jnp.broadcast_to(ar, (8, S5_STATES)), jnp.broadcast_to(ai, (8, S5_STATES))]
        carry_scr[...] = jnp.zeros(carry_scr.shape, F32)
        base = tuple(powers)

        def pw_body(i, pw):
            nxt = []
            for s in streams:
                pr, pi = pw[2 * s], pw[2 * s + 1]
                pr_scr[s, i] = pr
                pi_scr[s, i] = pi
                nxt += [base[2 * s] * pr - base[2 * s + 1] * pi, base[2 * s] * pi + base[2 * s + 1] * pr]
            return tuple(nxt)

        lax.fori_loop(0, S5_SUB, pw_body, base)

    a_re = [jnp.broadcast_to(a_scr[s, 0:1, :], (8, S5_STATES)) for s in streams]
    a_im = [jnp.broadcast_to(a_scr[s, 1:2, :], (8, S5_STATES)) for s in streams]
    perm = perm_ref[...]
    u = [u_ref[0, :, s * 128:(s + 1) * 128] for s in streams]
    for s in streams:
        u_p = jnp.dot(perm, u[s].astype(BF16), preferred_element_type=F32).astype(BF16)
        xr_scr[s] = jnp.dot(u_p, bbar_scr[s, 0], preferred_element_type=F32)
        xi_scr[s] = jnp.dot(u_p, bbar_scr[s, 1], preferred_element_type=F32)

    def scan_body(i, x):
        row = pl.multiple_of(i * 8, 8)
        out = []
        for s in streams:
            xr, xi = _cmul_add(a_re[s], a_im[s], x[2 * s], x[2 * s + 1],
                               xr_scr[s, pl.ds(row, 8), :], xi_scr[s, pl.ds(row, 8), :])
            xr_scr[s, pl.ds(row, 8), :] = xr
            xi_scr[s, pl.ds(row, 8), :] = xi
            out += [xr, xi]
        return tuple(out)

    zero = jnp.zeros((8, S5_STATES), F32)
    ends = lax.fori_loop(0, S5_SUB, scan_body, (zero,) * (2 * S5_STREAMS), unroll=4)

    cr, ci = [], []
    for s in streams:
        er, ei = ends[2 * s], ends[2 * s + 1]
        ar_s = pr_scr[s, S5_SUB - 1][0:1]
        ai_s = pi_scr[s, S5_SUB - 1][0:1]
        rows_r = [carry_scr[s, 0:1, :]]
        rows_i = [carry_scr[s, 1:2, :]]
        for j in range(8):
            nr, ni = _cmul_add(ar_s, ai_s, rows_r[-1], rows_i[-1], er[j:j + 1], ei[j:j + 1])
            rows_r.append(nr)
            rows_i.append(ni)
        carry_scr[s, 0:1, :] = rows_r[8]
        carry_scr[s, 1:2, :] = rows_i[8]
        cr.append(jnp.concatenate(rows_r[:8], axis=0))
        ci.append(jnp.concatenate(rows_i[:8], axis=0))

    def fix_body(i, carry):
        row = pl.multiple_of(i * 8, 8)
        for s in streams:
            xr, xi = _cmul_add(pr_scr[s, i], pi_scr[s, i], cr[s], ci[s],
                               xr_scr[s, pl.ds(row, 8), :], xi_scr[s, pl.ds(row, 8), :])
            xr_scr[s, pl.ds(row, 8), :] = xr
            xi_scr[s, pl.ds(row, 8), :] = xi
        return carry

    lax.fori_loop(0, S5_SUB, fix_body, 0, unroll=4)

    perm_t = permt_ref[...]
    for s in streams:
        y_p = (jnp.dot(xr_scr[s].astype(BF16), cmat_ref[s, 0], preferred_element_type=F32)
               - jnp.dot(xi_scr[s].astype(BF16), cmat_ref[s, 1], preferred_element_type=F32))
        y_hi = y_p.astype(BF16)
        y_lo = (y_p - y_hi.astype(F32)).astype(BF16)
        y = jnp.dot(perm_t, y_hi, preferred_element_type=F32) + jnp.dot(perm_t, y_lo, preferred_element_type=F32)
        o_ref[0, :, s * 128:(s + 1) * 128] = jax.nn.gelu(y + d_ref[s] * u[s])


def s5_scan(u, lam_re, lam_im, log_dt, b_re, b_im, c_re, c_im, d_skip):
    bsz, seq_len, _ = u.shape
    nb = SSM_GROUPS // S5_GROUPS_PER_BLOCK
    g = S5_STREAMS
    eye = jnp.eye(S5_GROUPS_PER_BLOCK, dtype=F32)

    def blockdiag_b(m):
        m = jnp.swapaxes(m, 1, 2).reshape(nb, S5_GROUPS_PER_BLOCK, SSM_CH_PER_GROUP, SSM_STATE)
        return jnp.einsum('nghp,gk->nghkp', m, eye).reshape(nb, 128, S5_STATES)

    def blockdiag_c(m):
        m = jnp.swapaxes(m, 1, 2).reshape(nb, S5_GROUPS_PER_BLOCK, SSM_STATE, SSM_CH_PER_GROUP)
        return jnp.einsum('ngph,gk->ngpkh', m, eye).reshape(nb, S5_STATES, 128)

    log_dt_states = jnp.broadcast_to(log_dt[:, None], lam_re.shape)
    lam = jnp.stack([m.reshape(nb, S5_STATES) for m in (lam_re, lam_im, log_dt_states)], axis=1)
    bmat = jnp.stack([blockdiag_b(b_re), blockdiag_b(b_im)], axis=1)
    cmat = jnp.stack([blockdiag_c(c_re), blockdiag_c(c_im)], axis=1).astype(BF16)
    d = d_skip.reshape(nb, 1, 128)
    r = np.arange(S5_CHUNK)
    perm = np.zeros((S5_CHUNK, S5_CHUNK), np.float32)
    perm[r, (r % 8) * S5_SUB + r // 8] = 1.0
    perm = jnp.asarray(perm, BF16)
    return pl.pallas_call(
        _s5_kernel,
        grid=(bsz, nb // g, seq_len // S5_CHUNK),
        in_specs=[
            pl.BlockSpec((1, S5_CHUNK, 128 * g), lambda b, k, c: (b, c, k)),
            pl.BlockSpec((g, 3, S5_STATES), lambda b, k, c: (k, 0, 0)),
            pl.BlockSpec((g, 2, 128, S5_STATES), lambda b, k, c: (k, 0, 0, 0)),
            pl.BlockSpec((g, 2, S5_STATES, 128), lambda b, k, c: (k, 0, 0, 0)),
            pl.BlockSpec((g, 1, 128), lambda b, k, c: (k, 0, 0)),
            pl.BlockSpec((S5_CHUNK, S5_CHUNK), lambda b, k, c: (0, 0)),
            pl.BlockSpec((S5_CHUNK, S5_CHUNK), lambda b, k, c: (0, 0)),
        ],
        out_specs=pl.BlockSpec((1, S5_CHUNK, 128 * g), lambda b, k, c: (b, c, k)),
        out_shape=jax.ShapeDtypeStruct((bsz, seq_len, SSM_WIDTH), F32),
        scratch_shapes=[pltpu.VMEM((g, S5_CHUNK, S5_STATES), F32), pltpu.VMEM((g, S5_CHUNK, S5_STATES), F32),
                        pltpu.VMEM((g, S5_SUB, 8, S5_STATES), F32), pltpu.VMEM((g, S5_SUB, 8, S5_STATES), F32),
                        pltpu.VMEM((g, 2, S5_STATES), F32), pltpu.VMEM((g, 2, S5_STATES), F32),
                        pltpu.VMEM((g, 2, 128, S5_STATES), BF16)],
        compiler_params=pltpu.CompilerParams(
            dimension_semantics=("arbitrary", "arbitrary", "arbitrary"), vmem_limit_bytes=V7X_VMEM_LIMIT_BYTES),
        name="s5_scan",
    )(u, lam, bmat, cmat, d, perm, perm.T)


def _softmax_tile(s, m_old):
    m_new = jnp.maximum(m_old, jnp.max(s, axis=1, keepdims=True))
    m_wide = jnp.concatenate([m_new] * (s.shape[1] // 128), axis=1)
    return m_new, jnp.exp2(m_old - m_new), jnp.exp2(s - m_wide)


def _lane_is_low(shape):
    return lax.broadcasted_iota(jnp.int32, shape, 1) < HEAD_DIM


def _pad_kt(kt, variant):
    z = jnp.zeros_like(kt)
    return jnp.concatenate([kt, z] if variant == 0 else [z, kt], axis=0)


def _pad_v(vv, variant):
    low = _lane_is_low(vv.shape)
    keep = low if variant == 0 else jnp.logical_not(low)
    return jnp.where(keep, vv, jnp.ones_like(vv))


def _finish(acc, variant):
    lane = lax.broadcasted_iota(jnp.int32, acc.shape, 1)
    lsel = lane == (HEAD_DIM if variant == 0 else 0)
    l = jnp.sum(jnp.where(lsel, acc, 0.0), axis=1, keepdims=True)
    keep = (lane < HEAD_DIM) if variant == 0 else (lane >= HEAD_DIM)
    return jnp.where(keep, acc / l, 0.0)


def _nsa_kernel(q_ref, g_ref, kct_ref, vc_ref, kst_ref, vs_ref, kwt_ref, vw_ref, ovl_ref, gx_ref, o_ref,
                m_scr, acc_scr, s_scr_a, s_scr_b, p_scr, *, seq_len):
    s_slots = (s_scr_a, s_scr_b)
    n_sel = seq_len // SEL_BLOCK
    n_cpad = seq_len // CMP_STRIDE
    sel_tile = PROJ_TILE
    blocks_per_tile = sel_tile // SEL_BLOCK
    win_tiles = WINDOW // Q_BLOCK + 1
    n_pair = Q_PER_KV // 2
    rows = n_pair * Q_BLOCK
    i = pl.program_id(2)
    t0 = i * Q_BLOCK

    qb = q_ref[0]
    qst = jnp.concatenate([qb[:, p * 128:(p + 1) * 128] for p in range(n_pair)], axis=0)

    sig = jax.nn.sigmoid(g_ref[0])
    sig_hi = sig.astype(BF16)
    sig_lo = (sig - sig_hi.astype(F32)).astype(BF16)
    gx = gx_ref[0]
    gexp = (jnp.dot(sig_hi, gx, preferred_element_type=F32) + jnp.dot(sig_lo, gx, preferred_element_type=F32))

    def gate_of(branch):
        base = branch * n_pair * 128
        return jnp.concatenate([gexp[:, base + p * 128: base + (p + 1) * 128] for p in range(n_pair)], axis=0)

    t_row = t0 + lax.broadcasted_iota(jnp.int32, (Q_BLOCK, 1), 0)

    slab = 64
    kct = kct_ref[0, 0]
    s_cmp = [jnp.dot(qst, _pad_kt(kct, v), preferred_element_type=F32) for v in range(2)]
    n_kblk = seq_len // Q_BLOCK
    w0 = jnp.clip(i - (win_tiles - 1), 0, n_kblk - win_tiles)
    kw = jnp.concatenate([kwt_ref[0, 0, w0 + j] for j in range(win_tiles)], axis=1)
    s_win = [jnp.dot(qst, _pad_kt(kw, v), preferred_element_type=F32) for v in range(2)]
    for v in range(2):
        s_slots[0][v] = jnp.dot(qst, _pad_kt(kst_ref[0, 0, 0], v), preferred_element_type=F32)

    n_iota = lax.broadcasted_iota(jnp.int32, (Q_BLOCK, n_cpad), 1)
    cmask = (n_iota * CMP_STRIDE + (CMP_BLOCK - 1)) <= t_row
    cmask4 = jnp.concatenate([cmask] * n_pair, axis=0)
    vcd = vc_ref[0, 0]
    p_sum = jnp.zeros((Q_BLOCK, n_cpad), F32)
    out = jnp.zeros((rows, 128), F32)
    o_c = jnp.zeros((rows, 128), F32)
    for v in range(2):
        s = jnp.where(cmask4, s_cmp[v], NEG)
        m = jnp.max(s, axis=1, keepdims=True)
        e = jnp.where(cmask4, jnp.exp2(s - m), 0.0)
        l = jnp.sum(e, axis=1, keepdims=True)
        p = e * (1.0 / jnp.maximum(l, 1e-30))
        for pp in range(n_pair):
            p_sum = p_sum + p[pp * Q_BLOCK:(pp + 1) * Q_BLOCK]
        low = _lane_is_low((n_cpad, 128))
        vz = jnp.where(low if v == 0 else jnp.logical_not(low), vcd, jnp.zeros_like(vcd))
        o_c = o_c + jnp.dot(p.astype(BF16), vz, preferred_element_type=F32)
    out = out + gate_of(0) * o_c

    ps_hi = p_sum.astype(BF16)
    ps_lo = (p_sum - ps_hi.astype(F32)).astype(BF16)
    ovl = ovl_ref[...]
    nt = (((1,), (1,)), ((), ()))
    imp_t = (lax.dot_general(ovl, ps_hi, nt, preferred_element_type=F32)
             + lax.dot_general(ovl, ps_lo, nt, preferred_element_type=F32))

    vw = jnp.concatenate([vw_ref[0, 0, w0 + j] for j in range(win_tiles)], axis=0)
    kpos_w = w0 * Q_BLOCK + lax.broadcasted_iota(jnp.int32, (Q_BLOCK, win_tiles * Q_BLOCK), 1)
    diff = t_row - kpos_w
    wbias = jnp.where((diff >= 0) & (diff < WINDOW), 0.0, NEG)
    wbias4 = jnp.concatenate([wbias] * n_pair, axis=0)
    o_w = jnp.zeros((rows, 128), F32)
    for v in range(2):
        s = s_win[v] + wbias4
        m = jnp.max(s, axis=1, keepdims=True)
        p = jnp.exp2(s - m)
        o_w = o_w + _finish(jnp.dot(p.astype(BF16), _pad_v(vw, v), preferred_element_type=F32), v)
    out = out + gate_of(2) * o_w

    s_iota = lax.broadcasted_iota(jnp.int32, (n_sel, Q_BLOCK), 0)
    t_lane = t0 + lax.broadcasted_iota(jnp.int32, (n_sel, Q_BLOCK), 1)
    cur = t_lane // SEL_BLOCK
    forced = (s_iota == 0) | (s_iota == cur) | (s_iota == cur - 1)
    valid = s_iota * SEL_BLOCK <= t_lane
    score = jnp.where(forced, FORCE, jnp.where(valid, imp_t, -1.0))
    s_f = s_iota.astype(F32)
    sel_t = jnp.zeros((n_sel, Q_BLOCK), F32)
    for _ in range(min(SEL_TOPK, n_sel)):
        mx = jnp.max(score, axis=0, keepdims=True)
        idx = jnp.min(jnp.where(score == mx, s_f, float(n_sel)), axis=0, keepdims=True)
        hit = s_f == idx
        sel_t = jnp.where(hit, 1.0, sel_t)
        score = jnp.where(hit, -3e38, score)
    selmask = sel_t.T.astype(BF16)

    m_scr[...] = jnp.full(m_scr.shape, NEG, F32)
    acc_scr[...] = jnp.zeros(acc_scr.shape, F32)
    n_tiles = (t0 + Q_BLOCK + sel_tile - 1) // sel_tile

    last_tile = seq_len // sel_tile - 1

    def bias_of(kt):
        blk = kt * blocks_per_tile + lax.broadcasted_iota(jnp.int32, (n_sel, sel_tile), 1) // SEL_BLOCK
        expand = (lax.broadcasted_iota(jnp.int32, (n_sel, sel_tile), 0) == blk).astype(BF16)
        selexp = jnp.dot(selmask, expand, preferred_element_type=F32)
        kpos = kt * sel_tile + lax.broadcasted_iota(jnp.int32, (Q_BLOCK, sel_tile), 1)
        bias = jnp.where((selexp > 0.5) & (kpos <= t_row), 0.0, NEG)
        return jnp.concatenate([bias] * n_pair, axis=0)

    def scores_into(slot, kt):
        bias4 = bias_of(kt)
        kt_tile = kst_ref[0, 0, jnp.minimum(kt, last_tile)]
        for v in range(2):
            s_slots[slot][v] = jnp.dot(qst, _pad_kt(kt_tile, v), preferred_element_type=F32) + bias4

    def attend_from(slot, kt):
        v_tile = vs_ref[0, 0, jnp.minimum(kt, last_tile)]
        for v in range(2):
            for h in range(rows // slab):
                r = slice(h * slab, (h + 1) * slab)
                m_new, alpha, p = _softmax_tile(s_slots[slot][v, r, :], m_scr[v, r, :])
                m_scr[v, r, :] = m_new
                acc_scr[v, r, :] = alpha * acc_scr[v, r, :]
                p_scr[v, r, :] = p.astype(BF16)
            acc_scr[v] += jnp.dot(p_scr[v], _pad_v(v_tile, v), preferred_element_type=F32)

    bias_first = bias_of(0)
    for v in range(2):
        s_slots[0][v] = s_slots[0][v] + bias_first

    def sel_body(j, carry):
        kt = 2 * j
        scores_into(1, kt + 1)
        attend_from(0, kt)
        scores_into(0, kt + 2)
        attend_from(1, kt + 1)
        return carry

    lax.fori_loop(0, (n_tiles + 1) // 2, sel_body, 0)
    out = out + gate_of(1) * (_finish(acc_scr[0], 0) + _finish(acc_scr[1], 1))

    o_ref[0] = jnp.concatenate([out[p * Q_BLOCK:(p + 1) * Q_BLOCK] for p in range(n_pair)], axis=1)


def nsa_attention(q, gate_pad, kct, vc, kst, vs, kwt, vw):
    bsz, seq_len, _ = q.shape
    n_sel = seq_len // SEL_BLOCK
    n_cpad = seq_len // CMP_STRIDE
    n_cmp = (seq_len - CMP_BLOCK) // CMP_STRIDE + 1
    n_pair = Q_PER_KV // 2
    cs = np.arange(n_cpad) * CMP_STRIDE
    ce = cs + CMP_BLOCK - 1
    ss = np.arange(n_sel) * SEL_BLOCK
    se = ss + SEL_BLOCK - 1
    ovl = (cs[None, :] <= se[:, None]) & (ce[None, :] >= ss[:, None]) & (np.arange(n_cpad)[None, :] < n_cmp)
    ovl = jnp.asarray(ovl.astype(np.float32), BF16)
    gx = np.zeros((NSA_KV_HEADS, 128, N_BRANCH * n_pair * 128), np.float32)
    for k in range(NSA_KV_HEADS):
        for hl in range(Q_PER_KV):
            for br in range(N_BRANCH):
                c0 = br * n_pair * 128 + hl * HEAD_DIM
                gx[k, (k * Q_PER_KV + hl) * N_BRANCH + br, c0:c0 + HEAD_DIM] = 1.0
    gx = jnp.asarray(gx, BF16)
    width = Q_PER_KV * HEAD_DIM
    full = lambda *shape: pl.BlockSpec((1, 1) + shape, lambda b, k, i: (b, k) + (0,) * len(shape))
    return pl.pallas_call(
        functools.partial(_nsa_kernel, seq_len=seq_len),
        grid=(bsz, NSA_KV_HEADS, seq_len // Q_BLOCK),
        in_specs=[
            pl.BlockSpec((1, Q_BLOCK, width), lambda b, k, i: (b, i, k)),
            pl.BlockSpec((1, Q_BLOCK, 128), lambda b, k, i: (b, i, 0)),
            full(HEAD_DIM, n_cpad), full(n_cpad, 128),
            full(seq_len // PROJ_TILE, HEAD_DIM, PROJ_TILE), full(seq_len // PROJ_TILE, PROJ_TILE, 128),
            full(seq_len // Q_BLOCK, HEAD_DIM, Q_BLOCK), full(seq_len // Q_BLOCK, Q_BLOCK, 128),
            pl.BlockSpec((n_sel, n_cpad), lambda b, k, i: (0, 0)),
            pl.BlockSpec((1, 128, N_BRANCH * n_pair * 128), lambda b, k, i: (k, 0, 0)),
        ],
        out_specs=pl.BlockSpec((1, Q_BLOCK, width), lambda b, k, i: (b, i, k)),
        out_shape=jax.ShapeDtypeStruct((bsz, seq_len, NSA_WIDTH), F32),
        scratch_shapes=[pltpu.VMEM((2, n_pair * Q_BLOCK, 128), F32), pltpu.VMEM((2, n_pair * Q_BLOCK, 128), F32),
                        pltpu.VMEM((2, n_pair * Q_BLOCK, PROJ_TILE), F32),
                        pltpu.VMEM((2, n_pair * Q_BLOCK, PROJ_TILE), F32),
                        pltpu.VMEM((2, n_pair * Q_BLOCK, PROJ_TILE), BF16)],
        compiler_params=pltpu.CompilerParams(
            dimension_semantics=("arbitrary", "arbitrary", "arbitrary"), vmem_limit_bytes=V7X_VMEM_LIMIT_BYTES),
        name="nsa_attention",
    )(q, gate_pad, kct, vc, kst, vs, kwt, vw, ovl, gx)


MOE_TILE = 1024
MOE_SUB = 256
MOE_ROWS = 48
MOE_SLOT = 64
MOE_GROUP = 4
MOE_GATHER_GROUP = 4


def _first_max_mask(x, idx_f, axis):
    mx = jnp.max(x, axis=axis, keepdims=True)
    first = jnp.min(jnp.where(x == mx, idx_f, 1e9), axis=axis, keepdims=True)
    return idx_f == first, mx


def _route(logits, bias):
    per_group = N_EXPERTS // N_EXPERT_GROUPS
    tr = logits.shape[1]
    aff = jax.nn.sigmoid(logits)
    biased = aff + bias
    grp = biased.reshape(N_EXPERT_GROUPS, per_group, tr)
    in_grp = lax.broadcasted_iota(jnp.int32, grp.shape, 1).astype(F32)
    hit1, m1 = _first_max_mask(grp, in_grp, 1)
    m2 = jnp.max(jnp.where(hit1, -jnp.inf, grp), axis=1, keepdims=True)
    gscore = (m1 + m2).reshape(N_EXPERT_GROUPS, tr)
    g_idx = lax.broadcasted_iota(jnp.int32, gscore.shape, 0).astype(F32)
    gsel = jnp.zeros(gscore.shape, F32)
    for _ in range(TOPK_GROUPS):
        hit, _ = _first_max_mask(gscore, g_idx, 0)
        gsel = jnp.where(hit, 1.0, gsel)
        gscore = jnp.where(hit, -jnp.inf, gscore)
    gmask = jnp.broadcast_to(gsel.reshape(N_EXPERT_GROUPS, 1, tr), grp.shape).reshape(N_EXPERTS, tr)
    cand = jnp.where(gmask > 0.5, biased, NEG)
    e_idx = lax.broadcasted_iota(jnp.int32, cand.shape, 0).astype(F32)
    sel = jnp.zeros(cand.shape, F32)
    for _ in range(TOP_K):
        hit, _ = _first_max_mask(cand, e_idx, 0)
        sel = jnp.where(hit, 1.0, sel)
        cand = jnp.where(hit, -jnp.inf, cand)
    w = jnp.where(sel > 0.5, aff, 0.0)
    return w / jnp.sum(w, axis=0, keepdims=True) * ROUTED_SCALE, sel


def _moe_kernel(cnt_ref, x_ref, sel_ref, w_ref, init_ref, wg_ref, wu_ref, wd_ref, lng_ref, lnb_ref, o_ref,
                rank_scr, ybuf_scr, sbuf_scr, xe_scr):
    i = pl.program_id(0)
    e = pl.program_id(1)
    tm = x_ref.shape[0]
    n_sub = tm // MOE_SUB
    tn = (((0,), (0,)), ((), ()))

    @pl.when(e == 0)
    def _():
        o_ref[...] = init_ref[...]
        before = (lax.broadcasted_iota(jnp.int32, (MOE_SUB, MOE_SUB), 0)
                  < lax.broadcasted_iota(jnp.int32, (MOE_SUB, MOE_SUB), 1))
        before = jnp.where(before, 1.0, 0.0).astype(BF16)
        for q in range(n_sub):
            cols = slice(q * MOE_SUB, (q + 1) * MOE_SUB)
            rank_scr[:, cols] = jnp.dot(sel_ref[:, cols].astype(BF16), before, preferred_element_type=F32)

    count = cnt_ref[i * N_EXPERTS + e]
    sel_e = sel_ref[pl.ds(e, 1), :]
    rank_e = rank_scr[pl.ds(e, 1), :]
    w_e = w_ref[pl.ds(e, 1), :]

    def one_hots(rank_row, sel_row, c):
        row = (c * MOE_ROWS + lax.broadcasted_iota(jnp.int32, (MOE_ROWS, MOE_SUB), 0)).astype(F32)
        hits = []
        for q in range(n_sub):
            cols = slice(q * MOE_SUB, (q + 1) * MOE_SUB)
            hits.append((rank_row[:, cols] == row) & (sel_row[:, cols] > 0.5))
        return hits

    def swiglu(xe):
        g = jnp.dot(xe, wg_ref[0], preferred_element_type=F32)
        u = jnp.dot(xe, wu_ref[0], preferred_element_type=F32)
        h = (jax.nn.silu(g) * u).astype(BF16)
        return jnp.dot(h, wd_ref[0], preferred_element_type=F32).astype(BF16)

    def weighted(hits):
        return [jnp.where(hits[q], w_e[:, q * MOE_SUB:(q + 1) * MOE_SUB], 0.0).astype(BF16) for q in range(n_sub)]

    slot = e % MOE_GROUP

    gslot = e % MOE_GATHER_GROUP

    @pl.when(gslot == 0)
    def _():
        stacks = [[] for _ in range(n_sub)]
        for gi in range(MOE_GATHER_GROUP):
            hits = one_hots(rank_scr[pl.ds(e + gi, 1), :], sel_ref[pl.ds(e + gi, 1), :], 0)
            for q in range(n_sub):
                stacks[q].append(jnp.where(hits[q], 1.0, 0.0).astype(BF16))
        for q in range(n_sub):
            cols = slice(q * MOE_SUB, (q + 1) * MOE_SUB)
            xg = jnp.dot(jnp.concatenate(stacks[q], axis=0), x_ref[cols, :],
                         preferred_element_type=F32).astype(BF16)
            for gi in range(MOE_GATHER_GROUP):
                xe_scr[gi, q] = xg[gi * MOE_ROWS:(gi + 1) * MOE_ROWS]

    scatters = weighted(one_hots(rank_e, sel_e, 0))
    y = swiglu(xe_scr[gslot].reshape(n_sub * MOE_ROWS, D_MODEL))
    spare = MOE_SLOT - MOE_ROWS
    for q in range(n_sub):
        sbuf_scr[q, slot] = jnp.concatenate([scatters[q], jnp.zeros((spare, MOE_SUB), BF16)], axis=0)
        ybuf_scr[q, slot] = jnp.concatenate(
            [y[q * MOE_ROWS:(q + 1) * MOE_ROWS], jnp.zeros((spare, D_MODEL), BF16)], axis=0)

    @pl.when(slot == MOE_GROUP - 1)
    def _():
        for q in range(n_sub):
            cols = slice(q * MOE_SUB, (q + 1) * MOE_SUB)
            o_ref[cols, :] += lax.dot_general(sbuf_scr[q].reshape(MOE_GROUP * MOE_SLOT, MOE_SUB),
                                              ybuf_scr[q].reshape(MOE_GROUP * MOE_SLOT, D_MODEL), tn,
                                              preferred_element_type=F32)

    def overflow_body(c, carry):
        hits = one_hots(rank_e, sel_e, c)
        sc = weighted(hits)
        xe = jnp.concatenate(
            [jnp.dot(jnp.where(hits[q], 1.0, 0.0).astype(BF16), x_ref[q * MOE_SUB:(q + 1) * MOE_SUB, :],
                     preferred_element_type=F32).astype(BF16) for q in range(n_sub)], axis=0)
        yy = swiglu(xe)
        for q in range(n_sub):
            cols = slice(q * MOE_SUB, (q + 1) * MOE_SUB)
            o_ref[cols, :] += lax.dot_general(sc[q], yy[q * MOE_ROWS:(q + 1) * MOE_ROWS], tn,
                                              preferred_element_type=F32)
        return carry

    lax.fori_loop(1, (count + MOE_ROWS - 1) // MOE_ROWS, overflow_body, 0)

    @pl.when(e == N_EXPERTS - 1)
    def _():
        o_ref[...] = _layer_norm(o_ref[...], lng_ref[...], lnb_ref[...])


def moe_routed(x_bf16, sel_t, w_t, init, w_gate, w_up, w_down, ln_g, ln_b):
    n_tok = x_bf16.shape[0]
    n_tiles = n_tok // MOE_TILE
    per_sub = jnp.sum(sel_t.reshape(N_EXPERTS, n_tiles, MOE_TILE // MOE_SUB, MOE_SUB), axis=-1)
    cnt = jnp.max(per_sub, axis=-1).T.astype(jnp.int32).reshape(-1)
    grid_spec = pltpu.PrefetchScalarGridSpec(
        num_scalar_prefetch=1,
        grid=(n_tiles, N_EXPERTS),
        in_specs=[
            pl.BlockSpec((MOE_TILE, D_MODEL), lambda i, e, cnt: (i, 0), pipeline_mode=pl.Buffered(1)),
            pl.BlockSpec((N_EXPERTS, MOE_TILE), lambda i, e, cnt: (0, i)),
            pl.BlockSpec((N_EXPERTS, MOE_TILE), lambda i, e, cnt: (0, i)),
            pl.BlockSpec((MOE_TILE, D_MODEL), lambda i, e, cnt: (i, 0), pipeline_mode=pl.Buffered(1)),
            pl.BlockSpec((1, D_MODEL, EXPERT_FF), lambda i, e, cnt: (e, 0, 0)),
            pl.BlockSpec((1, D_MODEL, EXPERT_FF), lambda i, e, cnt: (e, 0, 0)),
            pl.BlockSpec((1, EXPERT_FF, D_MODEL), lambda i, e, cnt: (e, 0, 0)),
            pl.BlockSpec((1, D_MODEL), lambda i, e, cnt: (0, 0)),
            pl.BlockSpec((1, D_MODEL), lambda i, e, cnt: (0, 0)),
        ],
        out_specs=pl.BlockSpec((MOE_TILE, D_MODEL), lambda i, e, cnt: (i, 0)),
        scratch_shapes=[pltpu.VMEM((N_EXPERTS, MOE_TILE), F32),
                        pltpu.VMEM((MOE_TILE // MOE_SUB, MOE_GROUP, MOE_SLOT, D_MODEL), BF16),
                        pltpu.VMEM((MOE_TILE // MOE_SUB, MOE_GROUP, MOE_SLOT, MOE_SUB), BF16),
                        pltpu.VMEM((MOE_GATHER_GROUP, MOE_TILE // MOE_SUB, MOE_ROWS, D_MODEL), BF16)],
    )
    return pl.pallas_call(
        _moe_kernel,
        grid_spec=grid_spec,
        out_shape=jax.ShapeDtypeStruct((n_tok, D_MODEL), F32),
        compiler_params=pltpu.CompilerParams(dimension_semantics=("arbitrary", "arbitrary"),
                                             vmem_limit_bytes=V7X_VMEM_LIMIT_BYTES),
        name="moe_routed",
    )(cnt, x_bf16, sel_t, w_t, init, w_gate, w_up, w_down, ln_g.reshape(1, D_MODEL), ln_b.reshape(1, D_MODEL))


def _post_mix_kernel(y_ref, a_ref, x_ref, wglu_ref, wout_ref, g_ref, b_ref, wrt_ref, rbias_ref,
                     wsg_ref, wsu_ref, wsd_ref, acc_ref, xb_ref, w_ref, sel_ref):
    y = y_ref[...]
    y_ssm = y * jax.nn.sigmoid(jnp.dot(y.astype(BF16), wglu_ref[...], preferred_element_type=F32))
    mix = (jnp.dot(y_ssm.astype(BF16), wout_ref[:SSM_WIDTH, :], preferred_element_type=F32)
           + jnp.dot(a_ref[...].astype(BF16), wout_ref[SSM_WIDTH:, :], preferred_element_type=F32))
    x1 = _layer_norm(DEEPNORM_ALPHA * x_ref[...] + mix, g_ref[...], b_ref[...])
    xb = x1.astype(BF16)
    xb_ref[...] = xb
    nt = (((1,), (1,)), ((), ()))
    w, sel = _route(lax.dot_general(wrt_ref[...], xb, nt, preferred_element_type=F32), rbias_ref[...])
    w_ref[...] = w
    sel_ref[...] = sel
    h = jax.nn.silu(jnp.dot(xb, wsg_ref[...], preferred_element_type=F32)) * jnp.dot(
        xb, wsu_ref[...], preferred_element_type=F32)
    acc_ref[...] = DEEPNORM_ALPHA * x1 + jnp.dot(h.astype(BF16), wsd_ref[...], preferred_element_type=F32)


def post_mix(y_s5, y_nsa, xt, w_glu, w_out, ln_g, ln_b, w_router, router_bias, wsg, wsu, wsd, tm=512):
    n_tok = xt.shape[0]
    row = lambda i: (i, 0)
    const = lambda i: (0, 0)
    once = pl.Buffered(1)
    wspec = lambda r, c: pl.BlockSpec((r, c), const, pipeline_mode=once)
    return pl.pallas_call(
        _post_mix_kernel,
        grid=(n_tok // tm,),
        in_specs=[pl.BlockSpec((tm, SSM_WIDTH), row), pl.BlockSpec((tm, NSA_WIDTH), row),
                  pl.BlockSpec((tm, D_MODEL), row),
                  wspec(SSM_WIDTH, SSM_WIDTH), wspec(D_MODEL, D_MODEL),
                  pl.BlockSpec((1, D_MODEL), const), pl.BlockSpec((1, D_MODEL), const),
                  wspec(N_EXPERTS, D_MODEL), pl.BlockSpec((N_EXPERTS, 1), const),
                  wspec(D_MODEL, EXPERT_FF), wspec(D_MODEL, EXPERT_FF), wspec(EXPERT_FF, D_MODEL)],
        out_specs=[pl.BlockSpec((tm, D_MODEL), row), pl.BlockSpec((tm, D_MODEL), row),
                   pl.BlockSpec((N_EXPERTS, tm), lambda i: (0, i)), pl.BlockSpec((N_EXPERTS, tm), lambda i: (0, i))],
        out_shape=[jax.ShapeDtypeStruct((n_tok, D_MODEL), F32), jax.ShapeDtypeStruct((n_tok, D_MODEL), BF16),
                   jax.ShapeDtypeStruct((N_EXPERTS, n_tok), F32), jax.ShapeDtypeStruct((N_EXPERTS, n_tok), F32)],
        compiler_params=pltpu.CompilerParams(dimension_semantics=("arbitrary",),
                                             vmem_limit_bytes=V7X_VMEM_LIMIT_BYTES),
        name="post_mix",
    )(y_s5, y_nsa, xt, w_glu.astype(BF16), w_out.astype(BF16), ln_g.reshape(1, D_MODEL), ln_b.reshape(1, D_MODEL),
      w_router.T.astype(BF16), router_bias.reshape(N_EXPERTS, 1), wsg.astype(BF16), wsu.astype(BF16),
      wsd.astype(BF16))


def hybrid_layer(x, positions, w_in, lam_re, lam_im, log_dt, ssm_b_re, ssm_b_im, ssm_c_re, ssm_c_im, ssm_d,
                 w_glu, cmp_pos_k, cmp_pos_v, w_cmp_k1, w_cmp_k2, w_cmp_v1, w_cmp_v2, w_out, ln1_g, ln1_b,
                 w_router, router_bias, w_gate, w_up, w_down, ws_gate, ws_up, ws_down, ln2_g, ln2_b):
    bsz, L, _ = x.shape
    sizes = [SSM_WIDTH, NSA_WIDTH] + [KV_WIDTH] * 6 + [NSA_HEADS * N_BRANCH]
    o = [0] + [int(v) for v in np.cumsum(sizes)]
    col = lambda j: w_in[:, o[j]:o[j + 1]]
    dup = lambda w: jnp.concatenate([w[:, h * HEAD_DIM:(h + 1) * HEAD_DIM] for h in (0, 0, 1, 1)], axis=1)
    gate_cols = jnp.pad(col(8), ((0, 0), (0, 128 - NSA_HEADS * N_BRANCH)))
    w_uq = w_in[:, :o[2]].astype(BF16)
    w_kv = jnp.concatenate([col(4), col(6), dup(col(5)), dup(col(7)), col(2), col(3), gate_cols], axis=1).astype(BF16)

    xt = x.reshape(bsz * L, D_MODEL)
    u, q, kst, kwt, vs, vw, kc_raw, vc_raw, gate_pad = proj_in(x, w_uq, w_kv, positions.reshape(bsz, L, 1))
    kct, vcd = compress_kv(kc_raw, vc_raw, positions, cmp_pos_k, cmp_pos_v, w_cmp_k1, w_cmp_k2, w_cmp_v1, w_cmp_v2)
    y_s5 = s5_scan(u, lam_re, lam_im, log_dt, ssm_b_re, ssm_b_im, ssm_c_re, ssm_c_im, ssm_d)
    vw = vw.reshape(bsz, NSA_KV_HEADS, L // Q_BLOCK, Q_BLOCK, 128)
    y_nsa = nsa_attention(q, gate_pad, kct, vcd, kst, vs, kwt, vw)
    acc0, x1b, w_t, sel_t = post_mix(y_s5.reshape(bsz * L, SSM_WIDTH), y_nsa.reshape(bsz * L, NSA_WIDTH), xt,
                                     w_glu, w_out, ln1_g, ln1_b, w_router, router_bias, ws_gate, ws_up, ws_down)
    out = moe_routed(x1b, sel_t, w_t, acc0, w_gate.astype(BF16), w_up.astype(BF16), w_down.astype(BF16),
                     ln2_g, ln2_b)
    return out.reshape(bsz, L, D_MODEL)


def kernel(x, positions, w_in, lam_re, lam_im, log_dt, ssm_b_re, ssm_b_im, ssm_c_re, ssm_c_im, ssm_d, w_glu, cmp_pos_k, cmp_pos_v, w_cmp_k1, w_cmp_k2, w_cmp_v1, w_cmp_v2, w_out, ln1_g, ln1_b, w_router, router_bias, w_gate, w_up, w_down, ws_gate, ws_up, ws_down, ln2_g, ln2_b):
    params = (w_in, lam_re, lam_im, log_dt, ssm_b_re, ssm_b_im, ssm_c_re, ssm_c_im, ssm_d,
              w_glu, cmp_pos_k, cmp_pos_v, w_cmp_k1, w_cmp_k2, w_cmp_v1, w_cmp_v2, w_out, ln1_g, ln1_b,
              w_router, router_bias, w_gate, w_up, w_down, ws_gate, ws_up, ws_down, ln2_g, ln2_b)
    return hybrid_layer(x, positions, *(p[0] for p in params))
```

```python
import functools
import math

import numpy as np
import jax
import jax.numpy as jnp
from jax import lax
from jax.experimental import pallas as pl
from jax.experimental.pallas import tpu as pltpu

D_MODEL = 2048
SSM_WIDTH = 1024
SSM_CH_PER_GROUP = 16
SSM_GROUPS = 64
SSM_STATE = 64
NSA_HEADS = 16
NSA_KV_HEADS = 2
HEAD_DIM = 64
Q_PER_KV = NSA_HEADS // NSA_KV_HEADS
NSA_WIDTH = NSA_HEADS * HEAD_DIM
KV_WIDTH = NSA_KV_HEADS * HEAD_DIM
N_BRANCH = 3
CMP_BLOCK = 32
CMP_STRIDE = 16
SEL_BLOCK = 64
SEL_TOPK = 16
WINDOW = 512
Q_BLOCK = 128
ROPE_THETA = 10000.0
N_EXPERTS = 64
TOP_K = 8
N_EXPERT_GROUPS = 8
TOPK_GROUPS = 4
ROUTED_SCALE = 2.5
EXPERT_FF = 512
DEPTH = 1
DEEPNORM_ALPHA = (2.0 * DEPTH) ** 0.25
LN_EPS = 1e-5
NEG = -1e30
FORCE = 1e4
F32 = jnp.float32
BF16 = jnp.bfloat16

V7X_VMEM_LIMIT_BYTES = 56 * 1024 * 1024


def _layer_norm(x, g, b):
    mu = jnp.mean(x, -1, keepdims=True)
    var = jnp.mean(jnp.square(x - mu), -1, keepdims=True)
    return (x - mu) * lax.rsqrt(var + LN_EPS) * g + b


def _rope_tables(pos_col, inv_row):
    ang = pos_col * inv_row
    return jnp.cos(ang), jnp.sin(ang)


def _rope_lanes(x, cos, sin):
    lane = lax.broadcasted_iota(jnp.int32, (x.shape[0], 128), 1)
    first_half = (lane % HEAD_DIM) < HEAD_DIM // 2
    outs = []
    for blk in range(x.shape[1] // 128):
        xb = x[:, blk * 128:(blk + 1) * 128]
        rot = jnp.where(first_half, -pltpu.roll(xb, 128 - HEAD_DIM // 2, 1), pltpu.roll(xb, HEAD_DIM // 2, 1))
        outs.append(xb * cos + rot * sin)
    return outs[0] if len(outs) == 1 else jnp.concatenate(outs, axis=1)


def _inv_freq_row():
    half = HEAD_DIM // 2
    inv = ROPE_THETA ** (-jnp.arange(half, dtype=F32) / half)
    return jnp.tile(inv, 128 // half).reshape(1, 128)


PROJ_TILE = 512
Q_SCALE = HEAD_DIM ** -0.5 * math.log2(math.e)


KV_COLS = 4 * KV_WIDTH + 2 * 2 * KV_WIDTH + 128


def _proj_in_kernel(x_ref, wuq_ref, wkv_ref, pos_ref, inv_ref,
                    u_ref, q_ref, kst_ref, kwt_ref, vs_ref, vw_ref, kc_ref, vc_ref, g_ref):
    xb = x_ref[0].astype(BF16)
    cos, sin = _rope_tables(pos_ref[0].astype(F32), inv_ref[...])
    uq = jnp.dot(xb, wuq_ref[...], preferred_element_type=F32)
    u_ref[0] = uq[:, :SSM_WIDTH]
    q_ref[0] = (_rope_lanes(uq[:, SSM_WIDTH:], cos, sin) * Q_SCALE).astype(BF16)
    acc = jnp.dot(xb, wkv_ref[...], preferred_element_type=F32)
    ks_t = _rope_lanes(acc[:, 0:128], cos, sin).T
    kw_t = _rope_lanes(acc[:, 128:256], cos, sin).T
    for k in range(NSA_KV_HEADS):
        kst_ref[0, k, 0] = ks_t[k * HEAD_DIM:(k + 1) * HEAD_DIM].astype(BF16)
        for j in range(PROJ_TILE // Q_BLOCK):
            kwt_ref[0, k, j] = kw_t[k * HEAD_DIM:(k + 1) * HEAD_DIM, j * Q_BLOCK:(j + 1) * Q_BLOCK].astype(BF16)
        vs_ref[0, k, 0] = acc[:, 256 + k * 128: 256 + (k + 1) * 128].astype(BF16)
        vw_ref[0, k] = acc[:, 512 + k * 128: 512 + (k + 1) * 128].astype(BF16)
    kc_ref[0] = acc[:, 768:896]
    vc_ref[0] = acc[:, 896:1024]
    g_ref[0] = acc[:, 1024:1152]


def proj_in(x, w_uq, w_kv, pos_col3):
    bsz, seq_len, _ = x.shape
    n_t = seq_len // PROJ_TILE
    per = PROJ_TILE // Q_BLOCK
    tok = lambda width: pl.BlockSpec((1, PROJ_TILE, width), lambda b, i: (b, i, 0))
    once = pl.Buffered(1)
    return pl.pallas_call(
        _proj_in_kernel,
        grid=(bsz, n_t),
        in_specs=[tok(D_MODEL),
                  pl.BlockSpec((D_MODEL, SSM_WIDTH + NSA_WIDTH), lambda b, i: (0, 0), pipeline_mode=once),
                  pl.BlockSpec((D_MODEL, KV_COLS), lambda b, i: (0, 0), pipeline_mode=once),
                  tok(1),
                  pl.BlockSpec((1, 128), lambda b, i: (0, 0))],
        out_specs=[
            tok(SSM_WIDTH), tok(NSA_WIDTH),
            pl.BlockSpec((1, NSA_KV_HEADS, 1, HEAD_DIM, PROJ_TILE), lambda b, i: (b, 0, i, 0, 0)),
            pl.BlockSpec((1, NSA_KV_HEADS, per, HEAD_DIM, Q_BLOCK), lambda b, i: (b, 0, i, 0, 0)),
            pl.BlockSpec((1, NSA_KV_HEADS, 1, PROJ_TILE, 128), lambda b, i: (b, 0, i, 0, 0)),
            pl.BlockSpec((1, NSA_KV_HEADS, PROJ_TILE, 128), lambda b, i: (b, 0, i, 0)),
            tok(128), tok(128), tok(128),
        ],
        out_shape=[
            jax.ShapeDtypeStruct((bsz, seq_len, SSM_WIDTH), F32),
            jax.ShapeDtypeStruct((bsz, seq_len, NSA_WIDTH), BF16),
            jax.ShapeDtypeStruct((bsz, NSA_KV_HEADS, n_t, HEAD_DIM, PROJ_TILE), BF16),
            jax.ShapeDtypeStruct((bsz, NSA_KV_HEADS, seq_len // Q_BLOCK, HEAD_DIM, Q_BLOCK), BF16),
            jax.ShapeDtypeStruct((bsz, NSA_KV_HEADS, n_t, PROJ_TILE, 128), BF16),
            jax.ShapeDtypeStruct((bsz, NSA_KV_HEADS, seq_len, 128), BF16),
            jax.ShapeDtypeStruct((bsz, seq_len, 128), F32),
            jax.ShapeDtypeStruct((bsz, seq_len, 128), F32),
            jax.ShapeDtypeStruct((bsz, seq_len, 128), F32),
        ],
        compiler_params=pltpu.CompilerParams(dimension_semantics=("arbitrary", "arbitrary"),
                                             vmem_limit_bytes=V7X_VMEM_LIMIT_BYTES),
        name="proj_in",
    )(x, w_uq, w_kv, pos_col3, _inv_freq_row())


def _compress_kernel(ck_ref, cv_ref, pek_ref, pev_ref, w1k_ref, w1v_ref, w2k_ref, w2v_ref, pos_ref, inv_ref,
                     kct_ref, vcd_ref):
    def hidden(c_ref, pe_ref, w1_ref):
        c = c_ref[0]
        lo = jnp.dot((c + pe_ref[0]).astype(BF16), w1_ref[0], preferred_element_type=F32)
        hi = jnp.dot((c + pe_ref[1]).astype(BF16), w1_ref[1], preferred_element_type=F32)
        hi_next = jnp.concatenate([hi[1:], jnp.zeros((1, hi.shape[1]), F32)], axis=0)
        return jax.nn.gelu(lo + hi_next).astype(BF16)

    kc = jnp.dot(hidden(ck_ref, pek_ref, w1k_ref), w2k_ref[...], preferred_element_type=F32)
    cos, sin = _rope_tables(pos_ref[0], inv_ref[...])
    kc_t = _rope_lanes(kc, cos, sin).T
    vc = jnp.dot(hidden(cv_ref, pev_ref, w1v_ref), w2v_ref[...], preferred_element_type=F32)
    for k in range(NSA_KV_HEADS):
        kct_ref[0, k] = kc_t[k * HEAD_DIM:(k + 1) * HEAD_DIM].astype(BF16)
        vcd_ref[0, k] = vc[:, k * 128:(k + 1) * 128].astype(BF16)


def compress_kv(kc_raw, vc_raw, positions, cmp_pos_k, cmp_pos_v, w_k1, w_k2, w_v1, w_v2):
    bsz, seq_len, _ = kc_raw.shape
    n_chunk = seq_len // CMP_STRIDE
    width = CMP_STRIDE * 128
    eye = jnp.eye(NSA_KV_HEADS, dtype=F32)

    def chunk_pe(pe):
        pe = pe.reshape(2, CMP_STRIDE, 1, HEAD_DIM)
        return jnp.broadcast_to(pe, (2, CMP_STRIDE, NSA_KV_HEADS, HEAD_DIM)).reshape(2, 1, width)

    def chunk_w1(w1):
        hid = w1.shape[1]
        w = w1.reshape(2, CMP_STRIDE, HEAD_DIM, hid)
        return jnp.einsum('htdj,kc->htkdcj', w, eye).reshape(2, width, NSA_KV_HEADS * hid).astype(BF16)

    hid = w_k2.shape[0]
    w2k = jnp.einsum('jd,kc->kjcd', w_k2, eye).reshape(NSA_KV_HEADS * hid, NSA_KV_HEADS * HEAD_DIM).astype(BF16)
    w2v = jnp.einsum('jd,kc,r->kjcrd', w_v2, eye, jnp.ones((2,), F32)).reshape(
        NSA_KV_HEADS * hid, NSA_KV_HEADS * 128).astype(BF16)
    pos = positions.astype(F32).reshape(bsz, n_chunk, CMP_STRIDE).sum(-1)
    pos_next = jnp.concatenate([pos[:, 1:], pos[:, -1:]], axis=1)
    cmp_pos = ((pos + pos_next) / CMP_BLOCK).reshape(bsz, n_chunk, 1)
    return pl.pallas_call(
        _compress_kernel,
        grid=(bsz,),
        in_specs=[pl.BlockSpec((1, n_chunk, width), lambda b: (b, 0, 0)),
                  pl.BlockSpec((1, n_chunk, width), lambda b: (b, 0, 0)),
                  pl.BlockSpec((2, 1, width), lambda b: (0, 0, 0)),
                  pl.BlockSpec((2, 1, width), lambda b: (0, 0, 0)),
                  pl.BlockSpec((2, width, NSA_KV_HEADS * hid), lambda b: (0, 0, 0)),
                  pl.BlockSpec((2, width, NSA_KV_HEADS * hid), lambda b: (0, 0, 0)),
                  pl.BlockSpec((NSA_KV_HEADS * hid, NSA_KV_HEADS * HEAD_DIM), lambda b: (0, 0)),
                  pl.BlockSpec((NSA_KV_HEADS * hid, NSA_KV_HEADS * 128), lambda b: (0, 0)),
                  pl.BlockSpec((1, n_chunk, 1), lambda b: (b, 0, 0)),
                  pl.BlockSpec((1, 128), lambda b: (0, 0))],
        out_specs=[pl.BlockSpec((1, NSA_KV_HEADS, HEAD_DIM, n_chunk), lambda b: (b, 0, 0, 0)),
                   pl.BlockSpec((1, NSA_KV_HEADS, n_chunk, 128), lambda b: (b, 0, 0, 0))],
        out_shape=[jax.ShapeDtypeStruct((bsz, NSA_KV_HEADS, HEAD_DIM, n_chunk), BF16),
                   jax.ShapeDtypeStruct((bsz, NSA_KV_HEADS, n_chunk, 128), BF16)],
        compiler_params=pltpu.CompilerParams(dimension_semantics=("arbitrary",),
                                             vmem_limit_bytes=V7X_VMEM_LIMIT_BYTES),
        name="compress_kv",
    )(kc_raw.reshape(bsz, n_chunk, width), vc_raw.reshape(bsz, n_chunk, width), chunk_pe(cmp_pos_k),
      chunk_pe(cmp_pos_v), chunk_w1(w_k1), chunk_w1(w_v1), w2k, w2v, cmp_pos, _inv_freq_row())


S5_CHUNK = 512
S5_SUB = S5_CHUNK // 8
S5_GROUPS_PER_BLOCK = 8
S5_STATES = S5_GROUPS_PER_BLOCK * SSM_STATE
S5_STREAMS = 8


def _cmul_add(ar, ai, xr, xi, br, bi):
    return ar * xr - ai * xi + br, ar * xi + ai * xr + bi


def _s5_kernel(u_ref, lam_ref, bmat_ref, cmat_ref, d_ref, perm_ref, permt_ref, o_ref,
               xr_scr, xi_scr, pr_scr, pi_scr, carry_scr, a_scr, bbar_scr):
    c = pl.program_id(2)
    streams = range(S5_STREAMS)

    @pl.when(c == 0)
    def _():
        powers = []
        for s in streams:
            lr, li = lam_ref[s, 0:1, :], lam_ref[s, 1:2, :]
            dt = jnp.exp(lam_ref[s, 2:3, :])
            mag = jnp.exp(lr * dt)
            ar, ai = mag * jnp.cos(li * dt), mag * jnp.sin(li * dt)
            zr, zi = ar - 1.0, ai
            den = lr * lr + li * li
            fr, fi = (zr * lr + zi * li) / den, (zi * lr - zr * li) / den
            a_scr[s, 0:1, :] = ar
            a_scr[s, 1:2, :] = ai
            b_re, b_im = bmat_ref[s, 0], bmat_ref[s, 1]
            bbar_scr[s, 0] = (fr * b_re - fi * b_im).astype(BF16)
            bbar_scr[s, 1] = (fr * b_im + fi * b_re).astype(BF16)
            powers += [jnp.broadcast_to(ar, (8, S5_STATES)), jnp.broadcast_to(ai, (8, S5_STATES))]
        carry_scr[...] = jnp.zeros(carry_scr.shape, F32)
        base = tuple(powers)

        def pw_body(i, pw):
            nxt = []
            for s in streams:
                pr, pi = pw[2 * s], pw[2 * s + 1]
                pr_scr[s, i] = pr
                pi_scr[s, i] = pi
                nxt += [base[2 * s] * pr - base[2 * s + 1] * pi, base[2 * s] * pi + base[2 * s + 1] * pr]
            return tuple(nxt)

        lax.fori_loop(0, S5_SUB, pw_body, base)

    a_re = [jnp.broadcast_to(a_scr[s, 0:1, :], (8, S5_STATES)) for s in streams]
    a_im = [jnp.broadcast_to(a_scr[s, 1:2, :], (8, S5_STATES)) for s in streams]
    perm = perm_ref[...]
    u = [u_ref[0, :, s * 128:(s + 1) * 128] for s in streams]
    for s in streams:
        u_p = jnp.dot(perm, u[s].astype(BF16), preferred_element_type=F32).astype(BF16)
        xr_scr[s] = jnp.dot(u_p, bbar_scr[s, 0], preferred_element_type=F32)
        xi_scr[s] = jnp.dot(u_p, bbar_scr[s, 1], preferred_element_type=F32)

    def scan_body(i, x):
        row = pl.multiple_of(i * 8, 8)
        out = []
        for s in streams:
            xr, xi = _cmul_add(a_re[s], a_im[s], x[2 * s], x[2 * s + 1],
                               xr_scr[s, pl.ds(row, 8), :], xi_scr[s, pl.ds(row, 8), :])
            xr_scr[s, pl.ds(row, 8), :] = xr
            xi_scr[s, pl.ds(row, 8), :] = xi
            out += [xr, xi]
        return tuple(out)

    zero = jnp.zeros((8, S5_STATES), F32)
    ends = lax.fori_loop(0, S5_SUB, scan_body, (zero,) * (2 * S5_STREAMS), unroll=4)

    cr, ci = [], []
    for s in streams:
        er, ei = ends[2 * s], ends[2 * s + 1]
        ar_s = pr_scr[s, S5_SUB - 1][0:1]
        ai_s = pi_scr[s, S5_SUB - 1][0:1]
        rows_r = [carry_scr[s, 0:1, :]]
        rows_i = [carry_scr[s, 1:2, :]]
        for j in range(8):
            nr, ni = _cmul_add(ar_s, ai_s, rows_r[-1], rows_i[-1], er[j:j + 1], ei[j:j + 1])
            rows_r.append(nr)
            rows_i.append(ni)
        carry_scr[s, 0:1, :] = rows_r[8]
        carry_scr[s, 1:2, :] = rows_i[8]
        cr.append(jnp.concatenate(rows_r[:8], axis=0))
        ci.append(jnp.concatenate(rows_i[:8], axis=0))

    def fix_body(i, carry):
        row = pl.multiple_of(i * 8, 8)
        for s in streams:
            xr, xi = _cmul_add(pr_scr[s, i], pi_scr[s, i], cr[s], ci[s],
                               xr_scr[s, pl.ds(row, 8), :], xi_scr[s, pl.ds(row, 8), :])
            xr_scr[s, pl.ds(row, 8), :] = xr
            xi_scr[s, pl.ds(row, 8), :] = xi
        return carry

    lax.fori_loop(0, S5_SUB, fix_body, 0, unroll=4)

    perm_t = permt_ref[...]
    for s in streams:
        y_p = (jnp.dot(xr_scr[s].astype(BF16), cmat_ref[s, 0], preferred_element_type=F32)
               - jnp.dot(xi_scr[s].astype(BF16), cmat_ref[s, 1], preferred_element_type=F32))
        y_hi = y_p.astype(BF16)
        y_lo = (y_p - y_hi.astype(F32)).astype(BF16)
        y = jnp.dot(perm_t, y_hi, preferred_element_type=F32) + jnp.dot(perm_t, y_lo, preferred_element_type=F32)
        o_ref[0, :, s * 128:(s + 1) * 128] = jax.nn.gelu(y + d_ref[s] * u[s])


def s5_scan(u, lam_re, lam_im, log_dt, b_re, b_im, c_re, c_im, d_skip):
    bsz, seq_len, _ = u.shape
    nb = SSM_GROUPS // S5_GROUPS_PER_BLOCK
    g = S5_STREAMS
    eye = jnp.eye(S5_GROUPS_PER_BLOCK, dtype=F32)

    def blockdiag_b(m):
        m = jnp.swapaxes(m, 1, 2).reshape(nb, S5_GROUPS_PER_BLOCK, SSM_CH_PER_GROUP, SSM_STATE)
        return jnp.einsum('nghp,gk->nghkp', m, eye).reshape(nb, 128, S5_STATES)

    def blockdiag_c(m):
        m = jnp.swapaxes(m, 1, 2).reshape(nb, S5_GROUPS_PER_BLOCK, SSM_STATE, SSM_CH_PER_GROUP)
        return jnp.einsum('ngph,gk->ngpkh', m, eye).reshape(nb, S5_STATES, 128)

    log_dt_states = jnp.broadcast_to(log_dt[:, None], lam_re.shape)
    lam = jnp.stack([m.reshape(nb, S5_STATES) for m in (lam_re, lam_im, log_dt_states)], axis=1)
    bmat = jnp.stack([blockdiag_b(b_re), blockdiag_b(b_im)], axis=1)
    cmat = jnp.stack([blockdiag_c(c_re), blockdiag_c(c_im)], axis=1).astype(BF16)
    d = d_skip.reshape(nb, 1, 128)
    r = np.arange(S5_CHUNK)
    perm = np.zeros((S5_CHUNK, S5_CHUNK), np.float32)
    perm[r, (r % 8) * S5_SUB + r // 8] = 1.0
    perm = jnp.asarray(perm, BF16)
    return pl.pallas_call(
        _s5_kernel,
        grid=(bsz, nb // g, seq_len // S5_CHUNK),
        in_specs=[
            pl.BlockSpec((1, S5_CHUNK, 128 * g), lambda b, k, c: (b, c, k)),
            pl.BlockSpec((g, 3, S5_STATES), lambda b, k, c: (k, 0, 0)),
            pl.BlockSpec((g, 2, 128, S5_STATES), lambda b, k, c: (k, 0, 0, 0)),
            pl.BlockSpec((g, 2, S5_STATES, 128), lambda b, k, c: (k, 0, 0, 0)),
            pl.BlockSpec((g, 1, 128), lambda b, k, c: (k, 0, 0)),
            pl.BlockSpec((S5_CHUNK, S5_CHUNK), lambda b, k, c: (0, 0)),
            pl.BlockSpec((S5_CHUNK, S5_CHUNK), lambda b, k, c: (0, 0)),
        ],
        out_specs=pl.BlockSpec((1, S5_CHUNK, 128 * g), lambda b, k, c: (b, c, k)),
        out_shape=jax.ShapeDtypeStruct((bsz, seq_len, SSM_WIDTH), F32),
        scratch_shapes=[pltpu.VMEM((g, S5_CHUNK, S5_STATES), F32), pltpu.VMEM((g, S5_CHUNK, S5_STATES), F32),
                        pltpu.VMEM((g, S5_SUB, 8, S5_STATES), F32), pltpu.VMEM((g, S5_SUB, 8, S5_STATES), F32),
                        pltpu.VMEM((g, 2, S5_STATES), F32), pltpu.VMEM((g, 2, S5_STATES), F32),
                        pltpu.VMEM((g, 2, 128, S5_STATES), BF16)],
        compiler_params=pltpu.CompilerParams(
            dimension_semantics=("arbitrary", "arbitrary", "arbitrary"), vmem_limit_bytes=V7X_VMEM_LIMIT_BYTES),
        name="s5_scan",
    )(u, lam, bmat, cmat, d, perm, perm.T)


def _softmax_tile(s, m_old):
    m_new = jnp.maximum(m_old, jnp.max(s, axis=1, keepdims=True))
    m_wide = jnp.concatenate([m_new] * (s.shape[1] // 128), axis=1)
    return m_new, jnp.exp2(m_old - m_new), jnp.exp2(s - m_wide)


def _lane_is_low(shape):
    return lax.broadcasted_iota(jnp.int32, shape, 1) < HEAD_DIM


def _pad_kt(kt, variant):
    z = jnp.zeros_like(kt)
    return jnp.concatenate([kt, z] if variant == 0 else [z, kt], axis=0)


def _pad_v(vv, variant):
    low = _lane_is_low(vv.shape)
    keep = low if variant == 0 else jnp.logical_not(low)
    return jnp.where(keep, vv, jnp.ones_like(vv))


def _finish(acc, variant):
    lane = lax.broadcasted_iota(jnp.int32, acc.shape, 1)
    lsel = lane == (HEAD_DIM if variant == 0 else 0)
    l = jnp.sum(jnp.where(lsel, acc, 0.0), axis=1, keepdims=True)
    keep = (lane < HEAD_DIM) if variant == 0 else (lane >= HEAD_DIM)
    return jnp.where(keep, acc / l, 0.0)


def _nsa_kernel(q_ref, g_ref, kct_ref, vc_ref, kst_ref, vs_ref, kwt_ref, vw_ref, ovl_ref, gx_ref, o_ref,
                m_scr, acc_scr, s_scr_a, s_scr_b, p_scr, *, seq_len):
    s_slots = (s_scr_a, s_scr_b)
    n_sel = seq_len // SEL_BLOCK
    n_cpad = seq_len // CMP_STRIDE
    sel_tile = PROJ_TILE
    blocks_per_tile = sel_tile // SEL_BLOCK
    win_tiles = WINDOW // Q_BLOCK + 1
    n_pair = Q_PER_KV // 2
    rows = n_pair * Q_BLOCK
    i = pl.program_id(2)
    t0 = i * Q_BLOCK

    qb = q_ref[0]
    qst = jnp.concatenate([qb[:, p * 128:(p + 1) * 128] for p in range(n_pair)], axis=0)

    sig = jax.nn.sigmoid(g_ref[0])
    sig_hi = sig.astype(BF16)
    sig_lo = (sig - sig_hi.astype(F32)).astype(BF16)
    gx = gx_ref[0]
    gexp = (jnp.dot(sig_hi, gx, preferred_element_type=F32) + jnp.dot(sig_lo, gx, preferred_element_type=F32))

    def gate_of(branch):
        base = branch * n_pair * 128
        return jnp.concatenate([gexp[:, base + p * 128: base + (p + 1) * 128] for p in range(n_pair)], axis=0)

    t_row = t0 + lax.broadcasted_iota(jnp.int32, (Q_BLOCK, 1), 0)

    slab = 64
    kct = kct_ref[0, 0]
    s_cmp = [jnp.dot(qst, _pad_kt(kct, v), preferred_element_type=F32) for v in range(2)]
    n_kblk = seq_len // Q_BLOCK
    w0 = jnp.clip(i - (win_tiles - 1), 0, n_kblk - win_tiles)
    kw = jnp.concatenate([kwt_ref[0, 0, w0 + j] for j in range(win_tiles)], axis=1)
    s_win = [jnp.dot(qst, _pad_kt(kw, v), preferred_element_type=F32) for v in range(2)]
    for v in range(2):
        s_slots[0][v] = jnp.dot(qst, _pad_kt(kst_ref[0, 0, 0], v), preferred_element_type=F32)

    n_iota = lax.broadcasted_iota(jnp.int32, (Q_BLOCK, n_cpad), 1)
    cmask = (n_iota * CMP_STRIDE + (CMP_BLOCK - 1)) <= t_row
    cmask4 = jnp.concatenate([cmask] * n_pair, axis=0)
    vcd = vc_ref[0, 0]
    p_sum = jnp.zeros((Q_BLOCK, n_cpad), F32)
    out = jnp.zeros((rows, 128), F32)
    o_c = jnp.zeros((rows, 128), F32)
    for v in range(2):
        s = jnp.where(cmask4, s_cmp[v], NEG)
        m = jnp.max(s, axis=1, keepdims=True)
        e = jnp.where(cmask4, jnp.exp2(s - m), 0.0)
        l = jnp.sum(e, axis=1, keepdims=True)
        p = e * (1.0 / jnp.maximum(l, 1e-30))
        for pp in range(n_pair):
            p_sum = p_sum + p[pp * Q_BLOCK:(pp + 1) * Q_BLOCK]
        low = _lane_is_low((n_cpad, 128))
        vz = jnp.where(low if v == 0 else jnp.logical_not(low), vcd, jnp.zeros_like(vcd))
        o_c = o_c + jnp.dot(p.astype(BF16), vz, preferred_element_type=F32)
    out = out + gate_of(0) * o_c

    ps_hi = p_sum.astype(BF16)
    ps_lo = (p_sum - ps_hi.astype(F32)).astype(BF16)
    ovl = ovl_ref[...]
    nt = (((1,), (1,)), ((), ()))
    imp_t = (lax.dot_general(ovl, ps_hi, nt, preferred_element_type=F32)
             + lax.dot_general(ovl, ps_lo, nt, preferred_element_type=F32))

    vw = jnp.concatenate([vw_ref[0, 0, w0 + j] for j in range(win_tiles)], axis=0)
    kpos_w = w0 * Q_BLOCK + lax.broadcasted_iota(jnp.int32, (Q_BLOCK, win_tiles * Q_BLOCK), 1)
    diff = t_row - kpos_w
    wbias = jnp.where((diff >= 0) & (diff < WINDOW), 0.0, NEG)
    wbias4 = jnp.concatenate([wbias] * n_pair, axis=0)
    o_w = jnp.zeros((rows, 128), F32)
    for v in range(2):
        s = s_win[v] + wbias4
        m = jnp.max(s, axis=1, keepdims=True)
        p = jnp.exp2(s - m)
        o_w = o_w + _finish(jnp.dot(p.astype(BF16), _pad_v(vw, v), preferred_element_type=F32), v)
    out = out + gate_of(2) * o_w

    s_iota = lax.broadcasted_iota(jnp.int32, (n_sel, Q_BLOCK), 0)
    t_lane = t0 + lax.broadcasted_iota(jnp.int32, (n_sel, Q_BLOCK), 1)
    cur = t_lane // SEL_BLOCK
    forced = (s_iota == 0) | (s_iota == cur) | (s_iota == cur - 1)
    valid = s_iota * SEL_BLOCK <= t_lane
    score = jnp.where(forced, FORCE, jnp.where(valid, imp_t, -1.0))
    s_f = s_iota.astype(F32)
    sel_t = jnp.zeros((n_sel, Q_BLOCK), F32)
    for _ in range(min(SEL_TOPK, n_sel)):
        mx = jnp.max(score, axis=0, keepdims=True)
        idx = jnp.min(jnp.where(score == mx, s_f, float(n_sel)), axis=0, keepdims=True)
        hit = s_f == idx
        sel_t = jnp.where(hit, 1.0, sel_t)
        score = jnp.where(hit, -3e38, score)
    selmask = sel_t.T.astype(BF16)

    m_scr[...] = jnp.full(m_scr.shape, NEG, F32)
    acc_scr[...] = jnp.zeros(acc_scr.shape, F32)
    n_tiles = (t0 + Q_BLOCK + sel_tile - 1) // sel_tile

    last_tile = seq_len // sel_tile - 1

    def bias_of(kt):
        blk = kt * blocks_per_tile + lax.broadcasted_iota(jnp.int32, (n_sel, sel_tile), 1) // SEL_BLOCK
        expand = (lax.broadcasted_iota(jnp.int32, (n_sel, sel_tile), 0) == blk).astype(BF16)
        selexp = jnp.dot(selmask, expand, preferred_element_type=F32)
        kpos = kt * sel_tile + lax.broadcasted_iota(jnp.int32, (Q_BLOCK, sel_tile), 1)
        bias = jnp.where((selexp > 0.5) & (kpos <= t_row), 0.0, NEG)
        return jnp.concatenate([bias] * n_pair, axis=0)

    def scores_into(slot, kt):
        bias4 = bias_of(kt)
        kt_tile = kst_ref[0, 0, jnp.minimum(kt, last_tile)]
        for v in range(2):
            s_slots[slot][v] = jnp.dot(qst, _pad_kt(kt_tile, v), preferred_element_type=F32) + bias4

    def attend_from(slot, kt):
        v_tile = vs_ref[0, 0, jnp.minimum(kt, last_tile)]
        for v in range(2):
            for h in range(rows // slab):
                r = slice(h * slab, (h + 1) * slab)
                m_new, alpha, p = _softmax_tile(s_slots[slot][v, r, :], m_scr[v, r, :])
                m_scr[v, r, :] = m_new
                acc_scr[v, r, :] = alpha * acc_scr[v, r, :]
                p_scr[v, r, :] = p.astype(BF16)
            acc_scr[v] += jnp.dot(p_scr[v], _pad_v(v_tile, v), preferred_element_type=F32)

    bias_first = bias_of(0)
    for v in range(2):
        s_slots[0][v] = s_slots[0][v] + bias_first

    def sel_body(j, carry):
        kt = 2 * j
        scores_into(1, kt + 1)
        attend_from(0, kt)
        scores_into(0, kt + 2)
        attend_from(1, kt + 1)
        return carry

    lax.fori_loop(0, (n_tiles + 1) // 2, sel_body, 0)
    out = out + gate_of(1) * (_finish(acc_scr[0], 0) + _finish(acc_scr[1], 1))

    o_ref[0] = jnp.concatenate([out[p * Q_BLOCK:(p + 1) * Q_BLOCK] for p in range(n_pair)], axis=1)


def nsa_attention(q, gate_pad, kct, vc, kst, vs, kwt, vw):
    bsz, seq_len, _ = q.shape
    n_sel = seq_len // SEL_BLOCK
    n_cpad = seq_len // CMP_STRIDE
    n_cmp = (seq_len - CMP_BLOCK) // CMP_STRIDE + 1
    n_pair = Q_PER_KV // 2
    cs = np.arange(n_cpad) * CMP_STRIDE
    ce = cs + CMP_BLOCK - 1
    ss = np.arange(n_sel) * SEL_BLOCK
    se = ss + SEL_BLOCK - 1
    ovl = (cs[None, :] <= se[:, None]) & (ce[None, :] >= ss[:, None]) & (np.arange(n_cpad)[None, :] < n_cmp)
    ovl = jnp.asarray(ovl.astype(np.float32), BF16)
    gx = np.zeros((NSA_KV_HEADS, 128, N_BRANCH * n_pair * 128), np.float32)
    for k in range(NSA_KV_HEADS):
        for hl in range(Q_PER_KV):
            for br in range(N_BRANCH):
                c0 = br * n_pair * 128 + hl * HEAD_DIM
                gx[k, (k * Q_PER_KV + hl) * N_BRANCH + br, c0:c0 + HEAD_DIM] = 1.0
    gx = jnp.asarray(gx, BF16)
    width = Q_PER_KV * HEAD_DIM
    full = lambda *shape: pl.BlockSpec((1, 1) + shape, lambda b, k, i: (b, k) + (0,) * len(shape))
    return pl.pallas_call(
        functools.partial(_nsa_kernel, seq_len=seq_len),
        grid=(bsz, NSA_KV_HEADS, seq_len // Q_BLOCK),
        in_specs=[
            pl.BlockSpec((1, Q_BLOCK, width), lambda b, k, i: (b, i, k)),
            pl.BlockSpec((1, Q_BLOCK, 128), lambda b, k, i: (b, i, 0)),
            full(HEAD_DIM, n_cpad), full(n_cpad, 128),
            full(seq_len // PROJ_TILE, HEAD_DIM, PROJ_TILE), full(seq_len // PROJ_TILE, PROJ_TILE, 128),
            full(seq_len // Q_BLOCK, HEAD_DIM, Q_BLOCK), full(seq_len // Q_BLOCK, Q_BLOCK, 128),
            pl.BlockSpec((n_sel, n_cpad), lambda b, k, i: (0, 0)),
            pl.BlockSpec((1, 128, N_BRANCH * n_pair * 128), lambda b, k, i: (k, 0, 0)),
        ],
        out_specs=pl.BlockSpec((1, Q_BLOCK, width), lambda b, k, i: (b, i, k)),
        out_shape=jax.ShapeDtypeStruct((bsz, seq_len, NSA_WIDTH), F32),
        scratch_shapes=[pltpu.VMEM((2, n_pair * Q_BLOCK, 128), F32), pltpu.VMEM((2, n_pair * Q_BLOCK, 128), F32),
                        pltpu.VMEM((2, n_pair * Q_BLOCK, PROJ_TILE), F32),
                        pltpu.VMEM((2, n_pair * Q_BLOCK, PROJ_TILE), F32),
                        pltpu.VMEM((2, n_pair * Q_BLOCK, PROJ_TILE), BF16)],
        compiler_params=pltpu.CompilerParams(
            dimension_semantics=("arbitrary", "arbitrary", "arbitrary"), vmem_limit_bytes=V7X_VMEM_LIMIT_BYTES),
        name="nsa_attention",
    )(q, gate_pad, kct, vc, kst, vs, kwt, vw, ovl, gx)


MOE_TILE = 1024
MOE_SUB = 256
MOE_ROWS = 48
MOE_SLOT = 64
MOE_GROUP = 4
MOE_GATHER_GROUP = 4


def _first_max_mask(x, idx_f, axis):
    mx = jnp.max(x, axis=axis, keepdims=True)
    first = jnp.min(jnp.where(x == mx, idx_f, 1e9), axis=axis, keepdims=True)
    return idx_f == first, mx


def _route(logits, bias):
    per_group = N_EXPERTS // N_EXPERT_GROUPS
    tr = logits.shape[1]
    aff = jax.nn.sigmoid(logits)
    biased = aff + bias
    grp = biased.reshape(N_EXPERT_GROUPS, per_group, tr)
    in_grp = lax.broadcasted_iota(jnp.int32, grp.shape, 1).astype(F32)
    hit1, m1 = _first_max_mask(grp, in_grp, 1)
    m2 = jnp.max(jnp.where(hit1, -jnp.inf, grp), axis=1, keepdims=True)
    gscore = (m1 + m2).reshape(N_EXPERT_GROUPS, tr)
    g_idx = lax.broadcasted_iota(jnp.int32, gscore.shape, 0).astype(F32)
    gsel = jnp.zeros(gscore.shape, F32)
    for _ in range(TOPK_GROUPS):
        hit, _ = _first_max_mask(gscore, g_idx, 0)
        gsel = jnp.where(hit, 1.0, gsel)
        gscore = jnp.where(hit, -jnp.inf, gscore)
    gmask = jnp.broadcast_to(gsel.reshape(N_EXPERT_GROUPS, 1, tr), grp.shape).reshape(N_EXPERTS, tr)
    cand = jnp.where(gmask > 0.5, biased, NEG)
    e_idx = lax.broadcasted_iota(jnp.int32, cand.shape, 0).astype(F32)
    sel = jnp.zeros(cand.shape, F32)
    for _ in range(TOP_K):
        hit, _ = _first_max_mask(cand, e_idx, 0)
        sel = jnp.where(hit, 1.0, sel)
        cand = jnp.where(hit, -jnp.inf, cand)
    w = jnp.where(sel > 0.5, aff, 0.0)
    return w / jnp.sum(w, axis=0, keepdims=True) * ROUTED_SCALE, sel


def _moe_kernel(cnt_ref, x_ref, sel_ref, w_ref, init_ref, wg_ref, wu_ref, wd_ref, lng_ref, lnb_ref, o_ref,
                rank_scr, ybuf_scr, sbuf_scr, xe_scr):
    i = pl.program_id(0)
    e = pl.program_id(1)
    tm = x_ref.shape[0]
    n_sub = tm // MOE_SUB
    tn = (((0,), (0,)), ((), ()))

    @pl.when(e == 0)
    def _():
        o_ref[...] = init_ref[...]
        before = (lax.broadcasted_iota(jnp.int32, (MOE_SUB, MOE_SUB), 0)
                  < lax.broadcasted_iota(jnp.int32, (MOE_SUB, MOE_SUB), 1))
        before = jnp.where(before, 1.0, 0.0).astype(BF16)
        for q in range(n_sub):
            cols = slice(q * MOE_SUB, (q + 1) * MOE_SUB)
            rank_scr[:, cols] = jnp.dot(sel_ref[:, cols].astype(BF16), before, preferred_element_type=F32)

    count = cnt_ref[i * N_EXPERTS + e]
    sel_e = sel_ref[pl.ds(e, 1), :]
    rank_e = rank_scr[pl.ds(e, 1), :]
    w_e = w_ref[pl.ds(e, 1), :]

    def one_hots(rank_row, sel_row, c):
        row = (c * MOE_ROWS + lax.broadcasted_iota(jnp.int32, (MOE_ROWS, MOE_SUB), 0)).astype(F32)
        hits = []
        for q in range(n_sub):
            cols = slice(q * MOE_SUB, (q + 1) * MOE_SUB)
            hits.append((rank_row[:, cols] == row) & (sel_row[:, cols] > 0.5))
        return hits

    def swiglu(xe):
        g = jnp.dot(xe, wg_ref[0], preferred_element_type=F32)
        u = jnp.dot(xe, wu_ref[0], preferred_element_type=F32)
        h = (jax.nn.silu(g) * u).astype(BF16)
        return jnp.dot(h, wd_ref[0], preferred_element_type=F32).astype(BF16)

    def weighted(hits):
        return [jnp.where(hits[q], w_e[:, q * MOE_SUB:(q + 1) * MOE_SUB], 0.0).astype(BF16) for q in range(n_sub)]

    slot = e % MOE_GROUP

    gslot = e % MOE_GATHER_GROUP

    @pl.when(gslot == 0)
    def _():
        stacks = [[] for _ in range(n_sub)]
        for gi in range(MOE_GATHER_GROUP):
            hits = one_hots(rank_scr[pl.ds(e + gi, 1), :], sel_ref[pl.ds(e + gi, 1), :], 0)
            for q in range(n_sub):
                stacks[q].append(jnp.where(hits[q], 1.0, 0.0).astype(BF16))
        for q in range(n_sub):
            cols = slice(q * MOE_SUB, (q + 1) * MOE_SUB)
            xg = jnp.dot(jnp.concatenate(stacks[q], axis=0), x_ref[cols, :],
                         preferred_element_type=F32).astype(BF16)
            for gi in range(MOE_GATHER_GROUP):
                xe_scr[gi, q] = xg[gi * MOE_ROWS:(gi + 1) * MOE_ROWS]

    scatters = weighted(one_hots(rank_e, sel_e, 0))
    y = swiglu(xe_scr[gslot].reshape(n_sub * MOE_ROWS, D_MODEL))
    spare = MOE_SLOT - MOE_ROWS
    for q in range(n_sub):
        sbuf_scr[q, slot] = jnp.concatenate([scatters[q], jnp.zeros((spare, MOE_SUB), BF16)], axis=0)
        ybuf_scr[q, slot] = jnp.concatenate(
            [y[q * MOE_ROWS:(q + 1) * MOE_ROWS], jnp.zeros((spare, D_MODEL), BF16)], axis=0)

    @pl.when(slot == MOE_GROUP - 1)
    def _():
        for q in range(n_sub):
            cols = slice(q * MOE_SUB, (q + 1) * MOE_SUB)
            o_ref[cols, :] += lax.dot_general(sbuf_scr[q].reshape(MOE_GROUP * MOE_SLOT, MOE_SUB),
                                              ybuf_scr[q].reshape(MOE_GROUP * MOE_SLOT, D_MODEL), tn,
                                              preferred_element_type=F32)

    def overflow_body(c, carry):
        hits = one_hots(rank_e, sel_e, c)
        sc = weighted(hits)
        xe = jnp.concatenate(
            [jnp.dot(jnp.where(hits[q], 1.0, 0.0).astype(BF16), x_ref[q * MOE_SUB:(q + 1) * MOE_SUB, :],
                     preferred_element_type=F32).astype(BF16) for q in range(n_sub)], axis=0)
        yy = swiglu(xe)
        for q in range(n_sub):
            cols = slice(q * MOE_SUB, (q + 1) * MOE_SUB)
            o_ref[cols, :] += lax.dot_general(sc[q], yy[q * MOE_ROWS:(q + 1) * MOE_ROWS], tn,
                                              preferred_element_type=F32)
        return carry

    lax.fori_loop(1, (count + MOE_ROWS - 1) // MOE_ROWS, overflow_body, 0)

    @pl.when(e == N_EXPERTS - 1)
    def _():
        o_ref[...] = _layer_norm(o_ref[...], lng_ref[...], lnb_ref[...])


def moe_routed(x_bf16, sel_t, w_t, init, w_gate, w_up, w_down, ln_g, ln_b):
    n_tok = x_bf16.shape[0]
    n_tiles = n_tok // MOE_TILE
    per_sub = jnp.sum(sel_t.reshape(N_EXPERTS, n_tiles, MOE_TILE // MOE_SUB, MOE_SUB), axis=-1)
    cnt = jnp.max(per_sub, axis=-1).T.astype(jnp.int32).reshape(-1)
    grid_spec = pltpu.PrefetchScalarGridSpec(
        num_scalar_prefetch=1,
        grid=(n_tiles, N_EXPERTS),
        in_specs=[
            pl.BlockSpec((MOE_TILE, D_MODEL), lambda i, e, cnt: (i, 0), pipeline_mode=pl.Buffered(1)),
            pl.BlockSpec((N_EXPERTS, MOE_TILE), lambda i, e, cnt: (0, i)),
            pl.BlockSpec((N_EXPERTS, MOE_TILE), lambda i, e, cnt: (0, i)),
            pl.BlockSpec((MOE_TILE, D_MODEL), lambda i, e, cnt: (i, 0), pipeline_mode=pl.Buffered(1)),
            pl.BlockSpec((1, D_MODEL, EXPERT_FF), lambda i, e, cnt: (e, 0, 0)),
            pl.BlockSpec((1, D_MODEL, EXPERT_FF), lambda i, e, cnt: (e, 0, 0)),
            pl.BlockSpec((1, EXPERT_FF, D_MODEL), lambda i, e, cnt: (e, 0, 0)),
            pl.BlockSpec((1, D_MODEL), lambda i, e, cnt: (0, 0)),
            pl.BlockSpec((1, D_MODEL), lambda i, e, cnt: (0, 0)),
        ],
        out_specs=pl.BlockSpec((MOE_TILE, D_MODEL), lambda i, e, cnt: (i, 0)),
        scratch_shapes=[pltpu.VMEM((N_EXPERTS, MOE_TILE), F32),
                        pltpu.VMEM((MOE_TILE // MOE_SUB, MOE_GROUP, MOE_SLOT, D_MODEL), BF16),
                        pltpu.VMEM((MOE_TILE // MOE_SUB, MOE_GROUP, MOE_SLOT, MOE_SUB), BF16),
                        pltpu.VMEM((MOE_GATHER_GROUP, MOE_TILE // MOE_SUB, MOE_ROWS, D_MODEL), BF16)],
    )
    return pl.pallas_call(
        _moe_kernel,
        grid_spec=grid_spec,
        out_shape=jax.ShapeDtypeStruct((n_tok, D_MODEL), F32),
        compiler_params=pltpu.CompilerParams(dimension_semantics=("arbitrary", "arbitrary"),
                                             vmem_limit_bytes=V7X_VMEM_LIMIT_BYTES),
        name="moe_routed",
    )(cnt, x_bf16, sel_t, w_t, init, w_gate, w_up, w_down, ln_g.reshape(1, D_MODEL), ln_b.reshape(1, D_MODEL))


def _post_mix_kernel(y_ref, a_ref, x_ref, wglu_ref, wout_ref, g_ref, b_ref, wrt_ref, rbias_ref,
                     wsg_ref, wsu_ref, wsd_ref, acc_ref, xb_ref, w_ref, sel_ref):
    y = y_ref[...]
    y_ssm = y * jax.nn.sigmoid(jnp.dot(y.astype(BF16), wglu_ref[...], preferred_element_type=F32))
    mix = (jnp.dot(y_ssm.astype(BF16), wout_ref[:SSM_WIDTH, :], preferred_element_type=F32)
           + jnp.dot(a_ref[...].astype(BF16), wout_ref[SSM_WIDTH:, :], preferred_element_type=F32))
    x1 = _layer_norm(DEEPNORM_ALPHA * x_ref[...] + mix, g_ref[...], b_ref[...])
    xb = x1.astype(BF16)
    xb_ref[...] = xb
    nt = (((1,), (1,)), ((), ()))
    w, sel = _route(lax.dot_general(wrt_ref[...], xb, nt, preferred_element_type=F32), rbias_ref[...])
    w_ref[...] = w
    sel_ref[...] = sel
    h = jax.nn.silu(jnp.dot(xb, wsg_ref[...], preferred_element_type=F32)) * jnp.dot(
        xb, wsu_ref[...], preferred_element_type=F32)
    acc_ref[...] = DEEPNORM_ALPHA * x1 + jnp.dot(h.astype(BF16), wsd_ref[...], preferred_element_type=F32)


def post_mix(y_s5, y_nsa, xt, w_glu, w_out, ln_g, ln_b, w_router, router_bias, wsg, wsu, wsd, tm=512):
    n_tok = xt.shape[0]
    row = lambda i: (i, 0)
    const = lambda i: (0, 0)
    once = pl.Buffered(1)
    wspec = lambda r, c: pl.BlockSpec((r, c), const, pipeline_mode=once)
    return pl.pallas_call(
        _post_mix_kernel,
        grid=(n_tok // tm,),
        in_specs=[pl.BlockSpec((tm, SSM_WIDTH), row), pl.BlockSpec((tm, NSA_WIDTH), row),
                  pl.BlockSpec((tm, D_MODEL), row),
                  wspec(SSM_WIDTH, SSM_WIDTH), wspec(D_MODEL, D_MODEL),
                  pl.BlockSpec((1, D_MODEL), const), pl.BlockSpec((1, D_MODEL), const),
                  wspec(N_EXPERTS, D_MODEL), pl.BlockSpec((N_EXPERTS, 1), const),
                  wspec(D_MODEL, EXPERT_FF), wspec(D_MODEL, EXPERT_FF), wspec(EXPERT_FF, D_MODEL)],
        out_specs=[pl.BlockSpec((tm, D_MODEL), row), pl.BlockSpec((tm, D_MODEL), row),
                   pl.BlockSpec((N_EXPERTS, tm), lambda i: (0, i)), pl.BlockSpec((N_EXPERTS, tm), lambda i: (0, i))],
        out_shape=[jax.ShapeDtypeStruct((n_tok, D_MODEL), F32), jax.ShapeDtypeStruct((n_tok, D_MODEL), BF16),
                   jax.ShapeDtypeStruct((N_EXPERTS, n_tok), F32), jax.ShapeDtypeStruct((N_EXPERTS, n_tok), F32)],
        compiler_params=pltpu.CompilerParams(dimension_semantics=("arbitrary",),
                                             vmem_limit_bytes=V7X_VMEM_LIMIT_BYTES),
        name="post_mix",
    )(y_s5, y_nsa, xt, w_glu.astype(BF16), w_out.astype(BF16), ln_g.reshape(1, D_MODEL), ln_b.reshape(1, D_MODEL),
      w_router.T.astype(BF16), router_bias.reshape(N_EXPERTS, 1), wsg.astype(BF16), wsu.astype(BF16),
      wsd.astype(BF16))


def hybrid_layer(x, positions, w_in, lam_re, lam_im, log_dt, ssm_b_re, ssm_b_im, ssm_c_re, ssm_c_im, ssm_d,
                 w_glu, cmp_pos_k, cmp_pos_v, w_cmp_k1, w_cmp_k2, w_cmp_v1, w_cmp_v2, w_out, ln1_g, ln1_b,
                 w_router, router_bias, w_gate, w_up, w_down, ws_gate, ws_up, ws_down, ln2_g, ln2_b):
    bsz, L, _ = x.shape
    sizes = [SSM_WIDTH, NSA_WIDTH] + [KV_WIDTH] * 6 + [NSA_HEADS * N_BRANCH]
    o = [0] + [int(v) for v in np.cumsum(sizes)]
    col = lambda j: w_in[:, o[j]:o[j + 1]]
    dup = lambda w: jnp.concatenate([w[:, h * HEAD_DIM:(h + 1) * HEAD_DIM] for h in (0, 0, 1, 1)], axis=1)
    gate_cols = jnp.pad(col(8), ((0, 0), (0, 128 - NSA_HEADS * N_BRANCH)))
    w_uq = w_in[:, :o[2]].astype(BF16)
    w_kv = jnp.concatenate([col(4), col(6), dup(col(5)), dup(col(7)), col(2), col(3), gate_cols], axis=1).astype(BF16)

    xt = x.reshape(bsz * L, D_MODEL)
    u, q, kst, kwt, vs, vw, kc_raw, vc_raw, gate_pad = proj_in(x, w_uq, w_kv, positions.reshape(bsz, L, 1))
    kct, vcd = compress_kv(kc_raw, vc_raw, positions, cmp_pos_k, cmp_pos_v, w_cmp_k1, w_cmp_k2, w_cmp_v1, w_cmp_v2)
    y_s5 = s5_scan(u, lam_re, lam_im, log_dt, ssm_b_re, ssm_b_im, ssm_c_re, ssm_c_im, ssm_d)
    vw = vw.reshape(bsz, NSA_KV_HEADS, L // Q_BLOCK, Q_BLOCK, 128)
    y_nsa = nsa_attention(q, gate_pad, kct, vcd, kst, vs, kwt, vw)
    acc0, x1b, w_t, sel_t = post_mix(y_s5.reshape(bsz * L, SSM_WIDTH), y_nsa.reshape(bsz * L, NSA_WIDTH), xt,
                                     w_glu, w_out, ln1_g, ln1_b, w_router, router_bias, ws_gate, ws_up, ws_down)
    out = moe_routed(x1b, sel_t, w_t, acc0, w_gate.astype(BF16), w_up.astype(BF16), w_down.astype(BF16),
                     ln2_g, ln2_b)
    return out.reshape(bsz, L, D_MODEL)


def kernel(x, positions, w_in, lam_re, lam_im, log_dt, ssm_b_re, ssm_b_im, ssm_c_re, ssm_c_im, ssm_d, w_glu, cmp_pos_k, cmp_pos_v, w_cmp_k1, w_cmp_k2, w_cmp_v1, w_cmp_v2, w_out, ln1_g, ln1_b, w_router, router_bias, w_gate, w_up, w_down, ws_gate, ws_up, ws_down, ln2_g, ln2_b):
    params = (w_in, lam_re, lam_im, log_dt, ssm_b_re, ssm_b_im, ssm_c_re, ssm_c_im, ssm_d,
              w_glu, cmp_pos_k, cmp_pos_v, w_cmp_k1, w_cmp_k2, w_cmp_v1, w_cmp_v2, w_out, ln1_g, ln1_b,
              w_router, router_bias, w_gate, w_up, w_down, ws_gate, ws_up, ws_down, ln2_g, ln2_b)
    return hybrid_layer(x, positions, *(p[0] for p in params))
```

```python
import functools
import math

import numpy as np
import jax
import jax.numpy as jnp
from jax import lax
from jax.experimental import pallas as pl
from jax.experimental.pallas import tpu as pltpu

D_MODEL = 2048
SSM_WIDTH = 1024
SSM_CH_PER_GROUP = 16
SSM_GROUPS = 64
SSM_STATE = 64
NSA_HEADS = 16
NSA_KV_HEADS = 2
HEAD_DIM = 64
Q_PER_KV = NSA_HEADS // NSA_KV_HEADS
NSA_WIDTH = NSA_HEADS * HEAD_DIM
KV_WIDTH = NSA_KV_HEADS * HEAD_DIM
N_BRANCH = 3
CMP_BLOCK = 32
CMP_STRIDE = 16
SEL_BLOCK = 64
SEL_TOPK = 16
WINDOW = 512
Q_BLOCK = 128
ROPE_THETA = 10000.0
N_EXPERTS = 64
TOP_K = 8
N_EXPERT_GROUPS = 8
TOPK_GROUPS = 4
ROUTED_SCALE = 2.5
EXPERT_FF = 512
DEPTH = 1
DEEPNORM_ALPHA = (2.0 * DEPTH) ** 0.25
LN_EPS = 1e-5
NEG = -1e30
FORCE = 1e4
F32 = jnp.float32
BF16 = jnp.bfloat16

V7X_VMEM_LIMIT_BYTES = 56 * 1024 * 1024


def _layer_norm(x, g, b):
    mu = jnp.mean(x, -1, keepdims=True)
    var = jnp.mean(jnp.square(x - mu), -1, keepdims=True)
    return (x - mu) * lax.rsqrt(var + LN_EPS) * g + b


def _rope_tables(pos_col, inv_row):
    ang = pos_col * inv_row
    return jnp.cos(ang), jnp.sin(ang)


def _rope_lanes(x, cos, sin):
    lane = lax.broadcasted_iota(jnp.int32, (x.shape[0], 128), 1)
    first_half = (lane % HEAD_DIM) < HEAD_DIM // 2
    outs = []
    for blk in range(x.shape[1] // 128):
        xb = x[:, blk * 128:(blk + 1) * 128]
        rot = jnp.where(first_half, -pltpu.roll(xb, 128 - HEAD_DIM // 2, 1), pltpu.roll(xb, HEAD_DIM // 2, 1))
        outs.append(xb * cos + rot * sin)
    return outs[0] if len(outs) == 1 else jnp.concatenate(outs, axis=1)


def _inv_freq_row():
    half = HEAD_DIM // 2
    inv = ROPE_THETA ** (-jnp.arange(half, dtype=F32) / half)
    return jnp.tile(inv, 128 // half).reshape(1, 128)


PROJ_TILE = 512
Q_SCALE = HEAD_DIM ** -0.5 * math.log2(math.e)


KV_COLS = 4 * KV_WIDTH + 2 * 2 * KV_WIDTH + 128


def _proj_in_kernel(x_ref, wuq_ref, wkv_ref, pos_ref, inv_ref,
                    u_ref, q_ref, kst_ref, kwt_ref, vs_ref, vw_ref, kc_ref, vc_ref, g_ref):
    xb = x_ref[0].astype(BF16)
    cos, sin = _rope_tables(pos_ref[0].astype(F32), inv_ref[...])
    uq = jnp.dot(xb, wuq_ref[...], preferred_element_type=F32)
    u_ref[0] = uq[:, :SSM_WIDTH]
    q_ref[0] = (_rope_lanes(uq[:, SSM_WIDTH:], cos, sin) * Q_SCALE).astype(BF16)
    acc = jnp.dot(xb, wkv_ref[...], preferred_element_type=F32)
    ks_t = _rope_lanes(acc[:, 0:128], cos, sin).T
    kw_t = _rope_lanes(acc[:, 128:256], cos, sin).T
    for k in range(NSA_KV_HEADS):
        kst_ref[0, k, 0] = ks_t[k * HEAD_DIM:(k + 1) * HEAD_DIM].astype(BF16)
        for j in range(PROJ_TILE // Q_BLOCK):
            kwt_ref[0, k, j] = kw_t[k * HEAD_DIM:(k + 1) * HEAD_DIM, j * Q_BLOCK:(j + 1) * Q_BLOCK].astype(BF16)
        vs_ref[0, k, 0] = acc[:, 256 + k * 128: 256 + (k + 1) * 128].astype(BF16)
        vw_ref[0, k] = acc[:, 512 + k * 128: 512 + (k + 1) * 128].astype(BF16)
    kc_ref[0] = acc[:, 768:896]
    vc_ref[0] = acc[:, 896:1024]
    g_ref[0] = acc[:, 1024:1152]


def proj_in(x, w_uq, w_kv, pos_col3):
    bsz, seq_len, _ = x.shape
    n_t = seq_len // PROJ_TILE
    per = PROJ_TILE // Q_BLOCK
    tok = lambda width: pl.BlockSpec((1, PROJ_TILE, width), lambda b, i: (b, i, 0))
    once = pl.Buffered(1)
    return pl.pallas_call(
        _proj_in_kernel,
        grid=(bsz, n_t),
        in_specs=[tok(D_MODEL),
                  pl.BlockSpec((D_MODEL, SSM_WIDTH + NSA_WIDTH), lambda b, i: (0, 0), pipeline_mode=once),
                  pl.BlockSpec((D_MODEL, KV_COLS), lambda b, i: (0, 0), pipeline_mode=once),
                  tok(1),
                  pl.BlockSpec((1, 128), lambda b, i: (0, 0))],
        out_specs=[
            tok(SSM_WIDTH), tok(NSA_WIDTH),
            pl.BlockSpec((1, NSA_KV_HEADS, 1, HEAD_DIM, PROJ_TILE), lambda b, i: (b, 0, i, 0, 0)),
            pl.BlockSpec((1, NSA_KV_HEADS, per, HEAD_DIM, Q_BLOCK), lambda b, i: (b, 0, i, 0, 0)),
            pl.BlockSpec((1, NSA_KV_HEADS, 1, PROJ_TILE, 128), lambda b, i: (b, 0, i, 0, 0)),
            pl.BlockSpec((1, NSA_KV_HEADS, PROJ_TILE, 128), lambda b, i: (b, 0, i, 0)),
            tok(128), tok(128), tok(128),
        ],
        out_shape=[
            jax.ShapeDtypeStruct((bsz, seq_len, SSM_WIDTH), F32),
            jax.ShapeDtypeStruct((bsz, seq_len, NSA_WIDTH), BF16),
            jax.ShapeDtypeStruct((bsz, NSA_KV_HEADS, n_t, HEAD_DIM, PROJ_TILE), BF16),
            jax.ShapeDtypeStruct((bsz, NSA_KV_HEADS, seq_len // Q_BLOCK, HEAD_DIM, Q_BLOCK), BF16),
            jax.ShapeDtypeStruct((bsz, NSA_KV_HEADS, n_t, PROJ_TILE, 128), BF16),
            jax.ShapeDtypeStruct((bsz, NSA_KV_HEADS, seq_len, 128), BF16),
            jax.ShapeDtypeStruct((bsz, seq_len, 128), F32),
            jax.ShapeDtypeStruct((bsz, seq_len, 128), F32),
            jax.ShapeDtypeStruct((bsz, seq_len, 128), F32),
        ],
        compiler_params=pltpu.CompilerParams(dimension_semantics=("arbitrary", "arbitrary"),
                                             vmem_limit_bytes=V7X_VMEM_LIMIT_BYTES),
        name="proj_in",
    )(x, w_uq, w_kv, pos_col3, _inv_freq_row())


def _compress_kernel(ck_ref, cv_ref, pek_ref, pev_ref, w1k_ref, w1v_ref, w2k_ref, w2v_ref, pos_ref, inv_ref,
                     kct_ref, vcd_ref):
    def hidden(c_ref, pe_ref, w1_ref):
        c = c_ref[0]
        lo = jnp.dot((c + pe_ref[0]).astype(BF16), w1_ref[0], preferred_element_type=F32)
        hi = jnp.dot((c + pe_ref[1]).astype(BF16), w1_ref[1], preferred_element_type=F32)
        hi_next = jnp.concatenate([hi[1:], jnp.zeros((1, hi.shape[1]), F32)], axis=0)
        return jax.nn.gelu(lo + hi_next).astype(BF16)

    kc = jnp.dot(hidden(ck_ref, pek_ref, w1k_ref), w2k_ref[...], preferred_element_type=F32)
    cos, sin = _rope_tables(pos_ref[0], inv_ref[...])
    kc_t = _rope_lanes(kc, cos, sin).T
    vc = jnp.dot(hidden(cv_ref, pev_ref, w1v_ref), w2v_ref[...], preferred_element_type=F32)
    for k in range(NSA_KV_HEADS):
        kct_ref[0, k] = kc_t[k * HEAD_DIM:(k + 1) * HEAD_DIM].astype(BF16)
        vcd_ref[0, k] = vc[:, k * 128:(k + 1) * 128].astype(BF16)


def compress_kv(kc_raw, vc_raw, positions, cmp_pos_k, cmp_pos_v, w_k1, w_k2, w_v1, w_v2):
    bsz, seq_len, _ = kc_raw.shape
    n_chunk = seq_len // CMP_STRIDE
    width = CMP_STRIDE * 128
    eye = jnp.eye(NSA_KV_HEADS, dtype=F32)

    def chunk_pe(pe):
        pe = pe.reshape(2, CMP_STRIDE, 1, HEAD_DIM)
        return jnp.broadcast_to(pe, (2, CMP_STRIDE, NSA_KV_HEADS, HEAD_DIM)).reshape(2, 1, width)

    def chunk_w1(w1):
        hid = w1.shape[1]
        w = w1.reshape(2, CMP_STRIDE, HEAD_DIM, hid)
        return jnp.einsum('htdj,kc->htkdcj', w, eye).reshape(2, width, NSA_KV_HEADS * hid).astype(BF16)

    hid = w_k2.shape[0]
    w2k = jnp.einsum('jd,kc->kjcd', w_k2, eye).reshape(NSA_KV_HEADS * hid, NSA_KV_HEADS * HEAD_DIM).astype(BF16)
    w2v = jnp.einsum('jd,kc,r->kjcrd', w_v2, eye, jnp.ones((2,), F32)).reshape(
        NSA_KV_HEADS * hid, NSA_KV_HEADS * 128).astype(BF16)
    pos = positions.astype(F32).reshape(bsz, n_chunk, CMP_STRIDE).sum(-1)
    pos_next = jnp.concatenate([pos[:, 1:], pos[:, -1:]], axis=1)
    cmp_pos = ((pos + pos_next) / CMP_BLOCK).reshape(bsz, n_chunk, 1)
    return pl.pallas_call(
        _compress_kernel,
        grid=(bsz,),
        in_specs=[pl.BlockSpec((1, n_chunk, width), lambda b: (b, 0, 0)),
                  pl.BlockSpec((1, n_chunk, width), lambda b: (b, 0, 0)),
                  pl.BlockSpec((2, 1, width), lambda b: (0, 0, 0)),
                  pl.BlockSpec((2, 1, width), lambda b: (0, 0, 0)),
                  pl.BlockSpec((2, width, NSA_KV_HEADS * hid), lambda b: (0, 0, 0)),
                  pl.BlockSpec((2, width, NSA_KV_HEADS * hid), lambda b: (0, 0, 0)),
                  pl.BlockSpec((NSA_KV_HEADS * hid, NSA_KV_HEADS * HEAD_DIM), lambda b: (0, 0)),
                  pl.BlockSpec((NSA_KV_HEADS * hid, NSA_KV_HEADS * 128), lambda b: (0, 0)),
                  pl.BlockSpec((1, n_chunk, 1), lambda b: (b, 0, 0)),
                  pl.BlockSpec((1, 128), lambda b: (0, 0))],
        out_specs=[pl.BlockSpec((1, NSA_KV_HEADS, HEAD_DIM, n_chunk), lambda b: (b, 0, 0, 0)),
                   pl.BlockSpec((1, NSA_KV_HEADS, n_chunk, 128), lambda b: (b, 0, 0, 0))],
        out_shape=[jax.ShapeDtypeStruct((bsz, NSA_KV_HEADS, HEAD_DIM, n_chunk), BF16),
                   jax.ShapeDtypeStruct((bsz, NSA_KV_HEADS, n_chunk, 128), BF16)],
        compiler_params=pltpu.CompilerParams(dimension_semantics=("arbitrary",),
                                             vmem_limit_bytes=V7X_VMEM_LIMIT_BYTES),
        name="compress_kv",
    )(kc_raw.reshape(bsz, n_chunk, width), vc_raw.reshape(bsz, n_chunk, width), chunk_pe(cmp_pos_k),
      chunk_pe(cmp_pos_v), chunk_w1(w_k1), chunk_w1(w_v1), w2k, w2v, cmp_pos, _inv_freq_row())


S5_CHUNK = 512
S5_SUB = S5_CHUNK // 8
S5_GROUPS_PER_BLOCK = 8
S5_STATES = S5_GROUPS_PER_BLOCK * SSM_STATE
S5_STREAMS = 8


def _cmul_add(ar, ai, xr, xi, br, bi):
    return ar * xr - ai * xi + br, ar * xi + ai * xr + bi


def _s5_kernel(u_ref, lam_ref, bmat_ref, cmat_ref, d_ref, perm_ref, permt_ref, o_ref,
               xr_scr, xi_scr, pr_scr, pi_scr, carry_scr, a_scr, bbar_scr):
    c = pl.program_id(2)
    streams = range(S5_STREAMS)

    @pl.when(c == 0)
    def _():
        powers = []
        for s in streams:
            lr, li = lam_ref[s, 0:1, :], lam_ref[s, 1:2, :]
            dt = jnp.exp(lam_ref[s, 2:3, :])
            mag = jnp.exp(lr * dt)
            ar, ai = mag * jnp.cos(li * dt), mag * jnp.sin(li * dt)
            zr, zi = ar - 1.0, ai
            den = lr * lr + li * li
            fr, fi = (zr * lr + zi * li) / den, (zi * lr - zr * li) / den
            a_scr[s, 0:1, :] = ar
            a_scr[s, 1:2, :] = ai
            b_re, b_im = bmat_ref[s, 0], bmat_ref[s, 1]
            bbar_scr[s, 0] = (fr * b_re - fi * b_im).astype(BF16)
            bbar_scr[s, 1] = (fr * b_im + fi * b_re).astype(BF16)
            powers += [jnp.broadcast_to(ar, (8, S5_STATES)), jnp.broadcast_to(ai, (8, S5_STATES))]
        carry_scr[...] = jnp.zeros(carry_scr.shape, F32)
        base = tuple(powers)

        def pw_body(i, pw):
            nxt = []
            for s in streams:
                pr, pi = pw[2 * s], pw[2 * s + 1]
                pr_scr[s, i] = pr
                pi_scr[s, i] = pi
                nxt += [base[2 * s] * pr - base[2 * s + 1] * pi, base[2 * s] * pi + base[2 * s + 1] * pr]
            return tuple(nxt)

        lax.fori_loop(0, S5_SUB, pw_body, base)

    a_re = [jnp.broadcast_to(a_scr[s, 0:1, :], (8, S5_STATES)) for s in streams]
    a_im = [jnp.broadcast_to(a_scr[s, 1:2, :], (8, S5_STATES)) for s in streams]
    perm = perm_ref[...]
    u = [u_ref[0, :, s * 128:(s + 1) * 128] for s in streams]
    for s in streams:
        u_p = jnp.dot(perm, u[s].astype(BF16), preferred_element_type=F32).astype(BF16)
        xr_scr[s] = jnp.dot(u_p, bbar_scr[s, 0], preferred_element_type=F32)
        xi_scr[s] = jnp.dot(u_p, bbar_scr[s, 1], preferred_element_type=F32)

    def scan_body(i, x):
        row = pl.multiple_of(i * 8, 8)
        out = []
        for s in streams:
            xr, xi = _cmul_add(a_re[s], a_im[s], x[2 * s], x[2 * s + 1],
                               xr_scr[s, pl.ds(row, 8), :], xi_scr[s, pl.ds(row, 8), :])
            xr_scr[s, pl.ds(row, 8), :] = xr
            xi_scr[s, pl.ds(row, 8), :] = xi
            out += [xr, xi]
        return tuple(out)

    zero = jnp.zeros((8, S5_STATES), F32)
    ends = lax.fori_loop(0, S5_SUB, scan_body, (zero,) * (2 * S5_STREAMS), unroll=4)

    cr, ci = [], []
    for s in streams:
        er, ei = ends[2 * s], ends[2 * s + 1]
        ar_s = pr_scr[s, S5_SUB - 1][0:1]
        ai_s = pi_scr[s, S5_SUB - 1][0:1]
        rows_r = [carry_scr[s, 0:1, :]]
        rows_i = [carry_scr[s, 1:2, :]]
        for j in range(8):
            nr, ni = _cmul_add(ar_s, ai_s, rows_r[-1], rows_i[-1], er[j:j + 1], ei[j:j + 1])
            rows_r.append(nr)
            rows_i.append(ni)
        carry_scr[s, 0:1, :] = rows_r[8]
        carry_scr[s, 1:2, :] = rows_i[8]
        cr.append(jnp.concatenate(rows_r[:8], axis=0))
        ci.append(jnp.concatenate(rows_i[:8], axis=0))

    def fix_body(i, carry):
        row = pl.multiple_of(i * 8, 8)
        for s in streams:
            xr, xi = _cmul_add(pr_scr[s, i], pi_scr[s, i], cr[s], ci[s],
                               xr_scr[s, pl.ds(row, 8), :], xi_scr[s, pl.ds(row, 8), :])
            xr_scr[s, pl.ds(row, 8), :] = xr
            xi_scr[s, pl.ds(row, 8), :] = xi
        return carry

    lax.fori_loop(0, S5_SUB, fix_body, 0, unroll=4)

    perm_t = permt_ref[...]
    for s in streams:
        y_p = (jnp.dot(xr_scr[s].astype(BF16), cmat_ref[s, 0], preferred_element_type=F32)
               - jnp.dot(xi_scr[s].astype(BF16), cmat_ref[s, 1], preferred_element_type=F32))
        y_hi = y_p.astype(BF16)
        y_lo = (y_p - y_hi.astype(F32)).astype(BF16)
        y = jnp.dot(perm_t, y_hi, preferred_element_type=F32) + jnp.dot(perm_t, y_lo, preferred_element_type=F32)
        o_ref[0, :, s * 128:(s + 1) * 128] = jax.nn.gelu(y + d_ref[s] * u[s])


def s5_scan(u, lam_re, lam_im, log_dt, b_re, b_im, c_re, c_im, d_skip):
    bsz, seq_len, _ = u.shape
    nb = SSM_GROUPS // S5_GROUPS_PER_BLOCK
    g = S5_STREAMS
    eye = jnp.eye(S5_GROUPS_PER_BLOCK, dtype=F32)

    def blockdiag_b(m):
        m = jnp.swapaxes(m, 1, 2).reshape(nb, S5_GROUPS_PER_BLOCK, SSM_CH_PER_GROUP, SSM_STATE)
        return jnp.einsum('nghp,gk->nghkp', m, eye).reshape(nb, 128, S5_STATES)

    def blockdiag_c(m):
        m = jnp.swapaxes(m, 1, 2).reshape(nb, S5_GROUPS_PER_BLOCK, SSM_STATE, SSM_CH_PER_GROUP)
        return jnp.einsum('ngph,gk->ngpkh', m, eye).reshape(nb, S5_STATES, 128)

    log_dt_states = jnp.broadcast_to(log_dt[:, None], lam_re.shape)
    lam = jnp.stack([m.reshape(nb, S5_STATES) for m in (lam_re, lam_im, log_dt_states)], axis=1)
    bmat = jnp.stack([blockdiag_b(b_re), blockdiag_b(b_im)], axis=1)
    cmat = jnp.stack([blockdiag_c(c_re), blockdiag_c(c_im)], axis=1).astype(BF16)
    d = d_skip.reshape(nb, 1, 128)
    r = np.arange(S5_CHUNK)
    perm = np.zeros((S5_CHUNK, S5_CHUNK), np.float32)
    perm[r, (r % 8) * S5_SUB + r // 8] = 1.0
    perm = jnp.asarray(perm, BF16)
    return pl.pallas_call(
        _s5_kernel,
        grid=(bsz, nb // g, seq_len // S5_CHUNK),
        in_specs=[
            pl.BlockSpec((1, S5_CHUNK, 128 * g), lambda b, k, c: (b, c, k)),
            pl.BlockSpec((g, 3, S5_STATES), lambda b, k, c: (k, 0, 0)),
            pl.BlockSpec((g, 2, 128, S5_STATES), lambda b, k, c: (k, 0, 0, 0)),
            pl.BlockSpec((g, 2, S5_STATES, 128), lambda b, k, c: (k, 0, 0, 0)),
            pl.BlockSpec((g, 1, 128), lambda b, k, c: (k, 0, 0)),
            pl.BlockSpec((S5_CHUNK, S5_CHUNK), lambda b, k, c: (0, 0)),
            pl.BlockSpec((S5_CHUNK, S5_CHUNK), lambda b, k, c: (0, 0)),
        ],
        out_specs=pl.BlockSpec((1, S5_CHUNK, 128 * g), lambda b, k, c: (b, c, k)),
        out_shape=jax.ShapeDtypeStruct((bsz, seq_len, SSM_WIDTH), F32),
        scratch_shapes=[pltpu.VMEM((g, S5_CHUNK, S5_STATES), F32), pltpu.VMEM((g, S5_CHUNK, S5_STATES), F32),
                        pltpu.VMEM((g, S5_SUB, 8, S5_STATES), F32), pltpu.VMEM((g, S5_SUB, 8, S5_STATES), F32),
                        pltpu.VMEM((g, 2, S5_STATES), F32), pltpu.VMEM((g, 2, S5_STATES), F32),
                        pltpu.VMEM((g, 2, 128, S5_STATES), BF16)],
        compiler_params=pltpu.CompilerParams(
            dimension_semantics=("arbitrary", "arbitrary", "arbitrary"), vmem_limit_bytes=V7X_VMEM_LIMIT_BYTES),
        name="s5_scan",
    )(u, lam, bmat, cmat, d, perm, perm.T)


def _softmax_tile(s, m_old):
    m_new = jnp.maximum(m_old, jnp.max(s, axis=1, keepdims=True))
    m_wide = jnp.concatenate([m_new] * (s.shape[1] // 128), axis=1)
    return m_new, jnp.exp2(m_old - m_new), jnp.exp2(s - m_wide)


def _lane_is_low(shape):
    return lax.broadcasted_iota(jnp.int32, shape, 1) < HEAD_DIM


def _pad_kt(kt, variant):
    z = jnp.zeros_like(kt)
    return jnp.concatenate([kt, z] if variant == 0 else [z, kt], axis=0)


def _pad_v(vv, variant):
    low = _lane_is_low(vv.shape)
    keep = low if variant == 0 else jnp.logical_not(low)
    return jnp.where(keep, vv, jnp.ones_like(vv))


def _finish(acc, variant):
    lane = lax.broadcasted_iota(jnp.int32, acc.shape, 1)
    lsel = lane == (HEAD_DIM if variant == 0 else 0)
    l = jnp.sum(jnp.where(lsel, acc, 0.0), axis=1, keepdims=True)
    keep = (lane < HEAD_DIM) if variant == 0 else (lane >= HEAD_DIM)
    return jnp.where(keep, acc / l, 0.0)


def _nsa_kernel(q_ref, g_ref, kct_ref, vc_ref, kst_ref, vs_ref, kwt_ref, vw_ref, ovl_ref, gx_ref, o_ref,
                m_scr, acc_scr, s_scr_a, s_scr_b, p_scr, *, seq_len):
    s_slots = (s_scr_a, s_scr_b)
    n_sel = seq_len // SEL_BLOCK
    n_cpad = seq_len // CMP_STRIDE
    sel_tile = PROJ_TILE
    blocks_per_tile = sel_tile // SEL_BLOCK
    win_tiles = WINDOW // Q_BLOCK + 1
    n_pair = Q_PER_KV // 2
    rows = n_pair * Q_BLOCK
    i = pl.program_id(2)
    t0 = i * Q_BLOCK

    qb = q_ref[0]
    qst = jnp.concatenate([qb[:, p * 128:(p + 1) * 128] for p in range(n_pair)], axis=0)

    sig = jax.nn.sigmoid(g_ref[0])
    sig_hi = sig.astype(BF16)
    sig_lo = (sig - sig_hi.astype(F32)).astype(BF16)
    gx = gx_ref[0]
    gexp = (jnp.dot(sig_hi, gx, preferred_element_type=F32) + jnp.dot(sig_lo, gx, preferred_element_type=F32))

    def gate_of(branch):
        base = branch * n_pair * 128
        return jnp.concatenate([gexp[:, base + p * 128: base + (p + 1) * 128] for p in range(n_pair)], axis=0)

    t_row = t0 + lax.broadcasted_iota(jnp.int32, (Q_BLOCK, 1), 0)

    slab = 64
    kct = kct_ref[0, 0]
    s_cmp = [jnp.dot(qst, _pad_kt(kct, v), preferred_element_type=F32) for v in range(2)]
    n_kblk = seq_len // Q_BLOCK
    w0 = jnp.clip(i - (win_tiles - 1), 0, n_kblk - win_tiles)
    kw = jnp.concatenate([kwt_ref[0, 0, w0 + j] for j in range(win_tiles)], axis=1)
    s_win = [jnp.dot(qst, _pad_kt(kw, v), preferred_element_type=F32) for v in range(2)]
    for v in range(2):
        s_slots[0][v] = jnp.dot(qst, _pad_kt(kst_ref[0, 0, 0], v), preferred_element_type=F32)

    n_iota = lax.broadcasted_iota(jnp.int32, (Q_BLOCK, n_cpad), 1)
    cmask = (n_iota * CMP_STRIDE + (CMP_BLOCK - 1)) <= t_row
    cmask4 = jnp.concatenate([cmask] * n_pair, axis=0)
    vcd = vc_ref[0, 0]
    p_sum = jnp.zeros((Q_BLOCK, n_cpad), F32)
    out = jnp.zeros((rows, 128), F32)
    o_c = jnp.zeros((rows, 128), F32)
    for v in range(2):
        s = jnp.where(cmask4, s_cmp[v], NEG)
        m = jnp.max(s, axis=1, keepdims=True)
        e = jnp.where(cmask4, jnp.exp2(s - m), 0.0)
        l = jnp.sum(e, axis=1, keepdims=True)
        p = e * (1.0 / jnp.maximum(l, 1e-30))
        for pp in range(n_pair):
            p_sum = p_sum + p[pp * Q_BLOCK:(pp + 1) * Q_BLOCK]
        low = _lane_is_low((n_cpad, 128))
        vz = jnp.where(low if v == 0 else jnp.logical_not(low), vcd, jnp.zeros_like(vcd))
        o_c = o_c + jnp.dot(p.astype(BF16), vz, preferred_element_type=F32)
    out = out + gate_of(0) * o_c

    ps_hi = p_sum.astype(BF16)
    ps_lo = (p_sum - ps_hi.astype(F32)).astype(BF16)
    ovl = ovl_ref[...]
    nt = (((1,), (1,)), ((), ()))
    imp_t = (lax.dot_general(ovl, ps_hi, nt, preferred_element_type=F32)
             + lax.dot_general(ovl, ps_lo, nt, preferred_element_type=F32))

    vw = jnp.concatenate([vw_ref[0, 0, w0 + j] for j in range(win_tiles)], axis=0)
    kpos_w = w0 * Q_BLOCK + lax.broadcasted_iota(jnp.int32, (Q_BLOCK, win_tiles * Q_BLOCK), 1)
    diff = t_row - kpos_w
    wbias = jnp.where((diff >= 0) & (diff < WINDOW), 0.0, NEG)
    wbias4 = jnp.concatenate([wbias] * n_pair, axis=0)
    o_w = jnp.zeros((rows, 128), F32)
    for v in range(2):
        s = s_win[v] + wbias4
        m = jnp.max(s, axis=1, keepdims=True)
        p = jnp.exp2(s - m)
        o_w = o_w + _finish(jnp.dot(p.astype(BF16), _pad_v(vw, v), preferred_element_type=F32), v)
    out = out + gate_of(2) * o_w

    s_iota = lax.broadcasted_iota(jnp.int32, (n_sel, Q_BLOCK), 0)
    t_lane = t0 + lax.broadcasted_iota(jnp.int32, (n_sel, Q_BLOCK), 1)
    cur = t_lane // SEL_BLOCK
    forced = (s_iota == 0) | (s_iota == cur) | (s_iota == cur - 1)
    valid = s_iota * SEL_BLOCK <= t_lane
    score = jnp.where(forced, FORCE, jnp.where(valid, imp_t, -1.0))
    s_f = s_iota.astype(F32)
    sel_t = jnp.zeros((n_sel, Q_BLOCK), F32)
    for _ in range(min(SEL_TOPK, n_sel)):
        mx = jnp.max(score, axis=0, keepdims=True)
        idx = jnp.min(jnp.where(score == mx, s_f, float(n_sel)), axis=0, keepdims=True)
        hit = s_f == idx
        sel_t = jnp.where(hit, 1.0, sel_t)
        score = jnp.where(hit, -3e38, score)
    selmask = sel_t.T.astype(BF16)

    m_scr[...] = jnp.full(m_scr.shape, NEG, F32)
    acc_scr[...] = jnp.zeros(acc_scr.shape, F32)
    n_tiles = (t0 + Q_BLOCK + sel_tile - 1) // sel_tile

    last_tile = seq_len // sel_tile - 1

    def bias_of(kt):
        blk = kt * blocks_per_tile + lax.broadcasted_iota(jnp.int32, (n_sel, sel_tile), 1) // SEL_BLOCK
        expand = (lax.broadcasted_iota(jnp.int32, (n_sel, sel_tile), 0) == blk).astype(BF16)
        selexp = jnp.dot(selmask, expand, preferred_element_type=F32)
        kpos = kt * sel_tile + lax.broadcasted_iota(jnp.int32, (Q_BLOCK, sel_tile), 1)
        bias = jnp.where((selexp > 0.5) & (kpos <= t_row), 0.0, NEG)
        return jnp.concatenate([bias] * n_pair, axis=0)

    def scores_into(slot, kt):
        bias4 = bias_of(kt)
        kt_tile = kst_ref[0, 0, jnp.minimum(kt, last_tile)]
        for v in range(2):
            s_slots[slot][v] = jnp.dot(qst, _pad_kt(kt_tile, v), preferred_element_type=F32) + bias4

    def attend_from(slot, kt):
        v_tile = vs_ref[0, 0, jnp.minimum(kt, last_tile)]
        for v in range(2):
            for h in range(rows // slab):
                r = slice(h * slab, (h + 1) * slab)
                m_new, alpha, p = _softmax_tile(s_slots[slot][v, r, :], m_scr[v, r, :])
                m_scr[v, r, :] = m_new
                acc_scr[v, r, :] = alpha * acc_scr[v, r, :]
                p_scr[v, r, :] = p.astype(BF16)
            acc_scr[v] += jnp.dot(p_scr[v], _pad_v(v_tile, v), preferred_element_type=F32)

    bias_first = bias_of(0)
    for v in range(2):
        s_slots[0][v] = s_slots[0][v] + bias_first

    def sel_body(j, carry):
        kt = 2 * j
        scores_into(1, kt + 1)
        attend_from(0, kt)
        scores_into(0, kt + 2)
        attend_from(1, kt + 1)
        return carry

    lax.fori_loop(0, (n_tiles + 1) // 2, sel_body, 0)
    out = out + gate_of(1) * (_finish(acc_scr[0], 0) + _finish(acc_scr[1], 1))

    o_ref[0] = jnp.concatenate([out[p * Q_BLOCK:(p + 1) * Q_BLOCK] for p in range(n_pair)], axis=1)


def nsa_attention(q, gate_pad, kct, vc, kst, vs, kwt, vw):
    bsz, seq_len, _ = q.shape
    n_sel = seq_len // SEL_BLOCK
    n_cpad = seq_len // CMP_STRIDE
    n_cmp = (seq_len - CMP_BLOCK) // CMP_STRIDE + 1
    n_pair = Q_PER_KV // 2
    cs = np.arange(n_cpad) * CMP_STRIDE
    ce = cs + CMP_BLOCK - 1
    ss = np.arange(n_sel) * SEL_BLOCK
    se = ss + SEL_BLOCK - 1
    ovl = (cs[None, :] <= se[:, None]) & (ce[None, :] >= ss[:, None]) & (np.arange(n_cpad)[None, :] < n_cmp)
    ovl = jnp.asarray(ovl.astype(np.float32), BF16)
    gx = np.zeros((NSA_KV_HEADS, 128, N_BRANCH * n_pair * 128), np.float32)
    for k in range(NSA_KV_HEADS):
        for hl in range(Q_PER_KV):
            for br in range(N_BRANCH):
                c0 = br * n_pair * 128 + hl * HEAD_DIM
                gx[k, (k * Q_PER_KV + hl) * N_BRANCH + br, c0:c0 + HEAD_DIM] = 1.0
    gx = jnp.asarray(gx, BF16)
    width = Q_PER_KV * HEAD_DIM
    full = lambda *shape: pl.BlockSpec((1, 1) + shape, lambda b, k, i: (b, k) + (0,) * len(shape))
    return pl.pallas_call(
        functools.partial(_nsa_kernel, seq_len=seq_len),
        grid=(bsz, NSA_KV_HEADS, seq_len // Q_BLOCK),
        in_specs=[
            pl.BlockSpec((1, Q_BLOCK, width), lambda b, k, i: (b, i, k)),
            pl.BlockSpec((1, Q_BLOCK, 128), lambda b, k, i: (b, i, 0)),
            full(HEAD_DIM, n_cpad), full(n_cpad, 128),
            full(seq_len // PROJ_TILE, HEAD_DIM, PROJ_TILE), full(seq_len // PROJ_TILE, PROJ_TILE, 128),
            full(seq_len // Q_BLOCK, HEAD_DIM, Q_BLOCK), full(seq_len // Q_BLOCK, Q_BLOCK, 128),
            pl.BlockSpec((n_sel, n_cpad), lambda b, k, i: (0, 0)),
            pl.BlockSpec((1, 128, N_BRANCH * n_pair * 128), lambda b, k, i: (k, 0, 0)),
        ],
        out_specs=pl.BlockSpec((1, Q_BLOCK, width), lambda b, k, i: (b, i, k)),
        out_shape=jax.ShapeDtypeStruct((bsz, seq_len, NSA_WIDTH), F32),
        scratch_shapes=[pltpu.VMEM((2, n_pair * Q_BLOCK, 128), F32), pltpu.VMEM((2, n_pair * Q_BLOCK, 128), F32),
                        pltpu.VMEM((2, n_pair * Q_BLOCK, PROJ_TILE), F32),
                        pltpu.VMEM((2, n_pair * Q_BLOCK, PROJ_TILE), F32),
                        pltpu.VMEM((2, n_pair * Q_BLOCK, PROJ_TILE), BF16)],
        compiler_params=pltpu.CompilerParams(
            dimension_semantics=("arbitrary", "arbitrary", "arbitrary"), vmem_limit_bytes=V7X_VMEM_LIMIT_BYTES),
        name="nsa_attention",
    )(q, gate_pad, kct, vc, kst, vs, kwt, vw, ovl, gx)


MOE_TILE = 1024
MOE_SUB = 256
MOE_ROWS = 48
MOE_SLOT = 64
MOE_GROUP = 4
MOE_GATHER_GROUP = 4


def _first_max_mask(x, idx_f, axis):
    mx = jnp.max(x, axis=axis, keepdims=True)
    first = jnp.min(jnp.where(x == mx, idx_f, 1e9), axis=axis, keepdims=True)
    return idx_f == first, mx


def _route(logits, bias):
    per_group = N_EXPERTS // N_EXPERT_GROUPS
    tr = logits.shape[1]
    aff = jax.nn.sigmoid(logits)
    biased = aff + bias
    grp = biased.reshape(N_EXPERT_GROUPS, per_group, tr)
    in_grp = lax.broadcasted_iota(jnp.int32, grp.shape, 1).astype(F32)
    hit1, m1 = _first_max_mask(grp, in_grp, 1)
    m2 = jnp.max(jnp.where(hit1, -jnp.inf, grp), axis=1, keepdims=True)
    gscore = (m1 + m2).reshape(N_EXPERT_GROUPS, tr)
    g_idx = lax.broadcasted_iota(jnp.int32, gscore.shape, 0).astype(F32)
    gsel = jnp.zeros(gscore.shape, F32)
    for _ in range(TOPK_GROUPS):
        hit, _ = _first_max_mask(gscore, g_idx, 0)
        gsel = jnp.where(hit, 1.0, gsel)
        gscore = jnp.where(hit, -jnp.inf, gscore)
    gmask = jnp.broadcast_to(gsel.reshape(N_EXPERT_GROUPS, 1, tr), grp.shape).reshape(N_EXPERTS, tr)
    cand = jnp.where(gmask > 0.5, biased, NEG)
    e_idx = lax.broadcasted_iota(jnp.int32, cand.shape, 0).astype(F32)
    sel = jnp.zeros(cand.shape, F32)
    for _ in range(TOP_K):
        hit, _ = _first_max_mask(cand, e_idx, 0)
        sel = jnp.where(hit, 1.0, sel)
        cand = jnp.where(hit, -jnp.inf, cand)
    w = jnp.where(sel > 0.5, aff, 0.0)
    return w / jnp.sum(w, axis=0, keepdims=True) * ROUTED_SCALE, sel


def _moe_kernel(cnt_ref, x_ref, sel_ref, w_ref, init_ref, wgu_ref, wd_ref, lng_ref, lnb_ref, o_ref,
                rank_scr, ybuf_scr, sbuf_scr, xe_scr):
    i = pl.program_id(0)
    e = pl.program_id(1)
    tm = x_ref.shape[0]
    n_sub = tm // MOE_SUB
    tn = (((0,), (0,)), ((), ()))

    @pl.when(e == 0)
    def _():
        o_ref[...] = init_ref[...]
        before = (lax.broadcasted_iota(jnp.int32, (MOE_SUB, MOE_SUB), 0)
                  < lax.broadcasted_iota(jnp.int32, (MOE_SUB, MOE_SUB), 1))
        before = jnp.where(before, 1.0, 0.0).astype(BF16)
        for q in range(n_sub):
            cols = slice(q * MOE_SUB, (q + 1) * MOE_SUB)
            rank_scr[:, cols] = jnp.dot(sel_ref[:, cols].astype(BF16), before, preferred_element_type=F32)

    count = cnt_ref[i * N_EXPERTS + e]
    sel_e = sel_ref[pl.ds(e, 1), :]
    rank_e = rank_scr[pl.ds(e, 1), :]
    w_e = w_ref[pl.ds(e, 1), :]

    def one_hots(rank_row, sel_row, c):
        row = (c * MOE_ROWS + lax.broadcasted_iota(jnp.int32, (MOE_ROWS, MOE_SUB), 0)).astype(F32)
        hits = []
        for q in range(n_sub):
            cols = slice(q * MOE_SUB, (q + 1) * MOE_SUB)
            hits.append((rank_row[:, cols] == row) & (sel_row[:, cols] > 0.5))
        return hits

    def swiglu(xe):
        gu = jnp.dot(xe, wgu_ref[0], preferred_element_type=F32)
        g, u = gu[:, :EXPERT_FF], gu[:, EXPERT_FF:]
        h = (jax.nn.silu(g) * u).astype(BF16)
        return jnp.dot(h, wd_ref[0], preferred_element_type=F32).astype(BF16)

    def weighted(hits):
        return [jnp.where(hits[q], w_e[:, q * MOE_SUB:(q + 1) * MOE_SUB], 0.0).astype(BF16) for q in range(n_sub)]

    slot = e % MOE_GROUP

    gslot = e % MOE_GATHER_GROUP

    @pl.when(gslot == 0)
    def _():
        stacks = [[] for _ in range(n_sub)]
        for gi in range(MOE_GATHER_GROUP):
            hits = one_hots(rank_scr[pl.ds(e + gi, 1), :], sel_ref[pl.ds(e + gi, 1), :], 0)
            for q in range(n_sub):
                stacks[q].append(jnp.where(hits[q], 1.0, 0.0).astype(BF16))
        for q in range(n_sub):
            cols = slice(q * MOE_SUB, (q + 1) * MOE_SUB)
            xg = jnp.dot(jnp.concatenate(stacks[q], axis=0), x_ref[cols, :],
                         preferred_element_type=F32).astype(BF16)
            for gi in range(MOE_GATHER_GROUP):
                xe_scr[gi, q] = xg[gi * MOE_ROWS:(gi + 1) * MOE_ROWS]

    scatters = weighted(one_hots(rank_e, sel_e, 0))
    y = swiglu(xe_scr[gslot].reshape(n_sub * MOE_ROWS, D_MODEL))
    spare = MOE_SLOT - MOE_ROWS
    for q in range(n_sub):
        sbuf_scr[q, slot] = jnp.concatenate([scatters[q], jnp.zeros((spare, MOE_SUB), BF16)], axis=0)
        ybuf_scr[q, slot] = jnp.concatenate(
            [y[q * MOE_ROWS:(q + 1) * MOE_ROWS], jnp.zeros((spare, D_MODEL), BF16)], axis=0)

    @pl.when(slot == MOE_GROUP - 1)
    def _():
        for q in range(n_sub):
            cols = slice(q * MOE_SUB, (q + 1) * MOE_SUB)
            o_ref[cols, :] += lax.dot_general(sbuf_scr[q].reshape(MOE_GROUP * MOE_SLOT, MOE_SUB),
                                              ybuf_scr[q].reshape(MOE_GROUP * MOE_SLOT, D_MODEL), tn,
                                              preferred_element_type=F32)

    def overflow_body(c, carry):
        hits = one_hots(rank_e, sel_e, c)
        sc = weighted(hits)
        xe = jnp.concatenate(
            [jnp.dot(jnp.where(hits[q], 1.0, 0.0).astype(BF16), x_ref[q * MOE_SUB:(q + 1) * MOE_SUB, :],
                     preferred_element_type=F32).astype(BF16) for q in range(n_sub)], axis=0)
        yy = swiglu(xe)
        for q in range(n_sub):
            cols = slice(q * MOE_SUB, (q + 1) * MOE_SUB)
            o_ref[cols, :] += lax.dot_general(sc[q], yy[q * MOE_ROWS:(q + 1) * MOE_ROWS], tn,
                                              preferred_element_type=F32)
        return carry

    lax.fori_loop(1, (count + MOE_ROWS - 1) // MOE_ROWS, overflow_body, 0)

    @pl.when(e == N_EXPERTS - 1)
    def _():
        o_ref[...] = _layer_norm(o_ref[...], lng_ref[...], lnb_ref[...])


def moe_routed(x_bf16, sel_t, w_t, init, w_gate, w_up, w_down, ln_g, ln_b):
    n_tok = x_bf16.shape[0]
    n_tiles = n_tok // MOE_TILE
    per_sub = jnp.sum(sel_t.reshape(N_EXPERTS, n_tiles, MOE_TILE // MOE_SUB, MOE_SUB), axis=-1)
    cnt = jnp.max(per_sub, axis=-1).T.astype(jnp.int32).reshape(-1)
    grid_spec = pltpu.PrefetchScalarGridSpec(
        num_scalar_prefetch=1,
        grid=(n_tiles, N_EXPERTS),
        in_specs=[
            pl.BlockSpec((MOE_TILE, D_MODEL), lambda i, e, cnt: (i, 0), pipeline_mode=pl.Buffered(1)),
            pl.BlockSpec((N_EXPERTS, MOE_TILE), lambda i, e, cnt: (0, i)),
            pl.BlockSpec((N_EXPERTS, MOE_TILE), lambda i, e, cnt: (0, i)),
            pl.BlockSpec((MOE_TILE, D_MODEL), lambda i, e, cnt: (i, 0), pipeline_mode=pl.Buffered(1)),
            pl.BlockSpec((1, D_MODEL, 2 * EXPERT_FF), lambda i, e, cnt: (e, 0, 0)),
            pl.BlockSpec((1, EXPERT_FF, D_MODEL), lambda i, e, cnt: (e, 0, 0)),
            pl.BlockSpec((1, D_MODEL), lambda i, e, cnt: (0, 0)),
            pl.BlockSpec((1, D_MODEL), lambda i, e, cnt: (0, 0)),
        ],
        out_specs=pl.BlockSpec((MOE_TILE, D_MODEL), lambda i, e, cnt: (i, 0)),
        scratch_shapes=[pltpu.VMEM((N_EXPERTS, MOE_TILE), F32),
                        pltpu.VMEM((MOE_TILE // MOE_SUB, MOE_GROUP, MOE_SLOT, D_MODEL), BF16),
                        pltpu.VMEM((MOE_TILE // MOE_SUB, MOE_GROUP, MOE_SLOT, MOE_SUB), BF16),
                        pltpu.VMEM((MOE_GATHER_GROUP, MOE_TILE // MOE_SUB, MOE_ROWS, D_MODEL), BF16)],
    )
    return pl.pallas_call(
        _moe_kernel,
        grid_spec=grid_spec,
        out_shape=jax.ShapeDtypeStruct((n_tok, D_MODEL), F32),
        compiler_params=pltpu.CompilerParams(dimension_semantics=("arbitrary", "arbitrary"),
                                             vmem_limit_bytes=V7X_VMEM_LIMIT_BYTES),
        name="moe_routed",
    )(cnt, x_bf16, sel_t, w_t, init, jnp.concatenate([w_gate, w_up], axis=-1), w_down,
      ln_g.reshape(1, D_MODEL), ln_b.reshape(1, D_MODEL))


def _post_mix_kernel(y_ref, a_ref, x_ref, wglu_ref, wout_ref, g_ref, b_ref, wrt_ref, rbias_ref,
                     wsg_ref, wsu_ref, wsd_ref, acc_ref, xb_ref, w_ref, sel_ref):
    y = y_ref[...]
    y_ssm = y * jax.nn.sigmoid(jnp.dot(y.astype(BF16), wglu_ref[...], preferred_element_type=F32))
    mix = (jnp.dot(y_ssm.astype(BF16), wout_ref[:SSM_WIDTH, :], preferred_element_type=F32)
           + jnp.dot(a_ref[...].astype(BF16), wout_ref[SSM_WIDTH:, :], preferred_element_type=F32))
    x1 = _layer_norm(DEEPNORM_ALPHA * x_ref[...] + mix, g_ref[...], b_ref[...])
    xb = x1.astype(BF16)
    xb_ref[...] = xb
    nt = (((1,), (1,)), ((), ()))
    w, sel = _route(lax.dot_general(wrt_ref[...], xb, nt, preferred_element_type=F32), rbias_ref[...])
    w_ref[...] = w
    sel_ref[...] = sel
    h = jax.nn.silu(jnp.dot(xb, wsg_ref[...], preferred_element_type=F32)) * jnp.dot(
        xb, wsu_ref[...], preferred_element_type=F32)
    acc_ref[...] = DEEPNORM_ALPHA * x1 + jnp.dot(h.astype(BF16), wsd_ref[...], preferred_element_type=F32)


def post_mix(y_s5, y_nsa, xt, w_glu, w_out, ln_g, ln_b, w_router, router_bias, wsg, wsu, wsd, tm=512):
    n_tok = xt.shape[0]
    row = lambda i: (i, 0)
    const = lambda i: (0, 0)
    once = pl.Buffered(1)
    wspec = lambda r, c: pl.BlockSpec((r, c), const, pipeline_mode=once)
    return pl.pallas_call(
        _post_mix_kernel,
        grid=(n_tok // tm,),
        in_specs=[pl.BlockSpec((tm, SSM_WIDTH), row), pl.BlockSpec((tm, NSA_WIDTH), row),
                  pl.BlockSpec((tm, D_MODEL), row),
                  wspec(SSM_WIDTH, SSM_WIDTH), wspec(D_MODEL, D_MODEL),
                  pl.BlockSpec((1, D_MODEL), const), pl.BlockSpec((1, D_MODEL), const),
                  wspec(N_EXPERTS, D_MODEL), pl.BlockSpec((N_EXPERTS, 1), const),
                  wspec(D_MODEL, EXPERT_FF), wspec(D_MODEL, EXPERT_FF), wspec(EXPERT_FF, D_MODEL)],
        out_specs=[pl.BlockSpec((tm, D_MODEL), row), pl.BlockSpec((tm, D_MODEL), row),
                   pl.BlockSpec((N_EXPERTS, tm), lambda i: (0, i)), pl.BlockSpec((N_EXPERTS, tm), lambda i: (0, i))],
        out_shape=[jax.ShapeDtypeStruct((n_tok, D_MODEL), F32), jax.ShapeDtypeStruct((n_tok, D_MODEL), BF16),
                   jax.ShapeDtypeStruct((N_EXPERTS, n_tok), F32), jax.ShapeDtypeStruct((N_EXPERTS, n_tok), F32)],
        compiler_params=pltpu.CompilerParams(dimension_semantics=("arbitrary",),
                                             vmem_limit_bytes=V7X_VMEM_LIMIT_BYTES),
        name="post_mix",
    )(y_s5, y_nsa, xt, w_glu.astype(BF16), w_out.astype(BF16), ln_g.reshape(1, D_MODEL), ln_b.reshape(1, D_MODEL),
      w_router.T.astype(BF16), router_bias.reshape(N_EXPERTS, 1), wsg.astype(BF16), wsu.astype(BF16),
      wsd.astype(BF16))


def hybrid_layer(x, positions, w_in, lam_re, lam_im, log_dt, ssm_b_re, ssm_b_im, ssm_c_re, ssm_c_im, ssm_d,
                 w_glu, cmp_pos_k, cmp_pos_v, w_cmp_k1, w_cmp_k2, w_cmp_v1, w_cmp_v2, w_out, ln1_g, ln1_b,
                 w_router, router_bias, w_gate, w_up, w_down, ws_gate, ws_up, ws_down, ln2_g, ln2_b):
    bsz, L, _ = x.shape
    sizes = [SSM_WIDTH, NSA_WIDTH] + [KV_WIDTH] * 6 + [NSA_HEADS * N_BRANCH]
    o = [0] + [int(v) for v in np.cumsum(sizes)]
    col = lambda j: w_in[:, o[j]:o[j + 1]]
    dup = lambda w: jnp.concatenate([w[:, h * HEAD_DIM:(h + 1) * HEAD_DIM] for h in (0, 0, 1, 1)], axis=1)
    gate_cols = jnp.pad(col(8), ((0, 0), (0, 128 - NSA_HEADS * N_BRANCH)))
    w_uq = w_in[:, :o[2]].astype(BF16)
    w_kv = jnp.concatenate([col(4), col(6), dup(col(5)), dup(col(7)), col(2), col(3), gate_cols], axis=1).astype(BF16)

    xt = x.reshape(bsz * L, D_MODEL)
    u, q, kst, kwt, vs, vw, kc_raw, vc_raw, gate_pad = proj_in(x, w_uq, w_kv, positions.reshape(bsz, L, 1))
    kct, vcd = compress_kv(kc_raw, vc_raw, positions, cmp_pos_k, cmp_pos_v, w_cmp_k1, w_cmp_k2, w_cmp_v1, w_cmp_v2)
    y_s5 = s5_scan(u, lam_re, lam_im, log_dt, ssm_b_re, ssm_b_im, ssm_c_re, ssm_c_im, ssm_d)
    vw = vw.reshape(bsz, NSA_KV_HEADS, L // Q_BLOCK, Q_BLOCK, 128)
    y_nsa = nsa_attention(q, gate_pad, kct, vcd, kst, vs, kwt, vw)
    acc0, x1b, w_t, sel_t = post_mix(y_s5.reshape(bsz * L, SSM_WIDTH), y_nsa.reshape(bsz * L, NSA_WIDTH), xt,
                                     w_glu, w_out, ln1_g, ln1_b, w_router, router_bias, ws_gate, ws_up, ws_down)
    out = moe_routed(x1b, sel_t, w_t, acc0, w_gate.astype(BF16), w_up.astype(BF16), w_down.astype(BF16),
                     ln2_g, ln2_b)
    return out.reshape(bsz, L, D_MODEL)


def kernel(x, positions, w_in, lam_re, lam_im, log_dt, ssm_b_re, ssm_b_im, ssm_c_re, ssm_c_im, ssm_d, w_glu, cmp_pos_k, cmp_pos_v, w_cmp_k1, w_cmp_k2, w_cmp_v1, w_cmp_v2, w_out, ln1_g, ln1_b, w_router, router_bias, w_gate, w_up, w_down, ws_gate, ws_up, ws_down, ln2_g, ln2_b):
    params = (w_in, lam_re, lam_im, log_dt, ssm_b_re, ssm_b_im, ssm_c_re, ssm_c_im, ssm_d,
              w_glu, cmp_pos_k, cmp_pos_v, w_cmp_k1, w_cmp_k2, w_cmp_v1, w_cmp_v2, w_out, ln1_g, ln1_b,
              w_router, router_bias, w_gate, w_up, w_down, ws_gate, ws_up, ws_down, ln2_g, ln2_b)
    return hybrid_layer(x, positions, *(p[0] for p in params))
```
